```python
import math
import jax, jax.numpy as jnp
from jax import lax
import numpy as np

D_MODEL = 2048
BATCH = 8
SEQ = 8192
DEPTH = 4

GRID_W = 64
CTX_LEN = 256
N_MIXERS = 2
N_SSD_LAYERS = (DEPTH + 1) // 2
N_ATTN_LAYERS = DEPTH // 2

SSD_EXPAND = 2
SSD_D_INNER = SSD_EXPAND * D_MODEL
SSD_HEAD_DIM = 64
SSD_HEADS = SSD_D_INNER // SSD_HEAD_DIM
SSD_GROUPS = 8
SSD_STATE = 128
SSD_CHUNK = 128
SSD_CONV = 3
SSD_XBC = SSD_D_INNER + 2 * SSD_GROUPS * SSD_STATE
SSD_IN = SSD_D_INNER + SSD_XBC + 2 * SSD_HEADS

ATTN_HEAD_DIM = 128
ATTN_HEADS = D_MODEL // ATTN_HEAD_DIM
ATTN_KV_HEADS = 4
ATTN_GROUP = ATTN_HEADS // ATTN_KV_HEADS
ATTN_Q_DIM = ATTN_HEADS * ATTN_HEAD_DIM
ATTN_KV_DIM = ATTN_KV_HEADS * ATTN_HEAD_DIM
ATTN_QKV = ATTN_Q_DIM + 2 * ATTN_KV_DIM
ATTN_WINDOW = 128
ATTN_BLOCK = 128
ROPE_THETA = 10000.0
ROPE_AXIS_DIM = ATTN_HEAD_DIM // 2

D_FF = 5632
FFN_CONV = 3

NORM_EPS = 1e-6

kernel_name = "hybrid_ssd_swa_convffn_prefix_dit"


def _rms_norm(x, w):
    xf = x.astype(jnp.float32)
    y = xf * lax.rsqrt(jnp.mean(xf * xf, axis=-1, keepdims=True) + NORM_EPS)
    return y.astype(x.dtype) * w


def _modulation(cvec, w, b):
    m = jax.nn.silu(cvec) @ w + b
    return jnp.split(m, 6, axis=-1)


def _modulate(h, shift, scale):
    return h * (1.0 + scale) + shift


def _dwconv(x, w, b):
    k = w.shape[0]
    pad = k // 2
    y = lax.conv_general_dilated(x, w[:, None, :], window_strides=(1,), padding=[(pad, pad)],
                                 dimension_numbers=('NWC', 'WIO', 'NWC'),
                                 feature_group_count=x.shape[-1])
    return y + b


def _ssd_chunked(x, dA, B, C, h0):
    b, l, _, p = x.shape
    g, n, e = SSD_GROUPS, SSD_STATE, SSD_HEADS // SSD_GROUPS
    nc = l // SSD_CHUNK
    xc = x.reshape(b, nc, SSD_CHUNK, g, e, p)
    Bc = B.reshape(b, nc, SSD_CHUNK, g, n)
    Cc = C.reshape(b, nc, SSD_CHUNK, g, n)
    a_cs = jnp.cumsum(dA.reshape(b, nc, SSD_CHUNK, g, e), axis=2)
    lower = jnp.tril(jnp.ones((SSD_CHUNK, SSD_CHUNK), dtype=bool))[None, None, :, :, None, None]
    diff = a_cs[:, :, :, None] - a_cs[:, :, None, :]
    seg = jnp.exp(jnp.where(lower, diff, -jnp.inf))
    cb = jnp.einsum('bclgn,bcsgn->bclsg', Cc, Bc)
    w = (cb[..., None] * seg).astype(x.dtype)
    y_diag = jnp.einsum('bclsge,bcsgep->bclgep', w, xc)
    decay_to_end = jnp.exp(a_cs[:, :, -1:] - a_cs).astype(x.dtype)
    states = jnp.einsum('bclgn,bclgep->bcgepn', Bc, xc * decay_to_end[..., None])
    chunk_decay = jnp.exp(a_cs[:, :, -1]).astype(states.dtype)
    if h0 is None:
        h0 = jnp.zeros((b, g, e, p, n), states.dtype)

    def step(h, inp):
        dec, st = inp
        return dec[..., None, None] * h + st, h

    h_last, h_in = lax.scan(step, h0.astype(states.dtype),
                            (jnp.moveaxis(chunk_decay, 1, 0), jnp.moveaxis(states, 1, 0)))
    h_in = jnp.moveaxis(h_in, 0, 1)
    decay_from_start = jnp.exp(a_cs).astype(x.dtype)
    y_off = jnp.einsum('bclgn,bcgepn->bclgep', Cc, h_in) * decay_from_start[..., None]
    y = (y_diag + y_off).reshape(b, l, g * e, p)
    return y, h_last


def _ssd_project(h, w_in, conv_w, conv_b):
    b, l, _ = h.shape
    zxbcdt = h @ w_in
    o1 = SSD_D_INNER
    o2 = o1 + SSD_XBC
    o3 = o2 + SSD_HEADS
    z = zxbcdt[..., :o1]
    xbc = jax.nn.silu(_dwconv(zxbcdt[..., o1:o2], conv_w, conv_b))
    dt_f = zxbcdt[..., o2:o3]
    dt_b = zxbcdt[..., o3:]
    gn = SSD_GROUPS * SSD_STATE
    xs = xbc[..., :SSD_D_INNER].reshape(b, l, SSD_HEADS, SSD_HEAD_DIM)
    B = xbc[..., SSD_D_INNER:SSD_D_INNER + gn].reshape(b, l, SSD_GROUPS, SSD_STATE)
    C = xbc[..., SSD_D_INNER + gn:].reshape(b, l, SSD_GROUPS, SSD_STATE)
    return z, xs, B, C, dt_f, dt_b


def _ssd_direction(xs, B, C, dt_raw, dt_bias, a_log, h0, reverse):
    dt = jax.nn.softplus((dt_raw + dt_bias).astype(jnp.float32))
    dA = -jnp.exp(a_log.astype(jnp.float32)) * dt
    xdt = xs * dt[..., None].astype(xs.dtype)
    if reverse:
        xdt, dA, B, C = (jnp.flip(t, axis=1) for t in (xdt, dA, B, C))
    y, h_last = _ssd_chunked(xdt, dA, B, C, h0)
    if reverse:
        y = jnp.flip(y, axis=1)
    return y, h_last


def _ssd_mixer(h_lat, h_ctx, w_in, conv_w, conv_b, dt_bias_f, dt_bias_b, a_log_f, a_log_b,
               d_skip, norm_w, w_out, need_ctx_out):
    z_c, xs_c, B_c, C_c, dtf_c, dtb_c = _ssd_project(h_ctx, w_in, conv_w, conv_b)
    yf_c, hf_c = _ssd_direction(xs_c, B_c, C_c, dtf_c, dt_bias_f, a_log_f, None, False)
    yb_c, hb_c = _ssd_direction(xs_c, B_c, C_c, dtb_c, dt_bias_b, a_log_b, None, True)
    z, xs, B, C, dtf, dtb = _ssd_project(h_lat, w_in, conv_w, conv_b)
    yf, _ = _ssd_direction(xs, B, C, dtf, dt_bias_f, a_log_f, hf_c, False)
    yb, _ = _ssd_direction(xs, B, C, dtb, dt_bias_b, a_log_b, hb_c, True)

    def finish(y_f, y_b, x_s, zz):
        y = (y_f + y_b + x_s * d_skip[:, None]).reshape(zz.shape)
        return _rms_norm(y * jax.nn.silu(zz), norm_w) @ w_out

    out_lat = finish(yf, yb, xs, z)
    out_ctx = finish(yf_c, yb_c, xs_c, z_c) if need_ctx_out else None
    return out_lat, out_ctx


def _axial_angles(rows):
    row = jnp.repeat(jnp.arange(rows, dtype=jnp.float32), GRID_W)
    col = jnp.tile(jnp.arange(GRID_W, dtype=jnp.float32), rows)
    inv_freq = ROPE_THETA ** (-jnp.arange(0, ROPE_AXIS_DIM, 2, dtype=jnp.float32) / ROPE_AXIS_DIM)
    return row[:, None] * inv_freq[None, :], col[:, None] * inv_freq[None, :]


def _rope_half(x, ang):
    m = ang.shape[-1]
    shape = (ang.shape[0],) + (1,) * (x.ndim - 3) + (m,)
    cos = jnp.cos(ang).reshape(shape).astype(x.dtype)
    sin = jnp.sin(ang).reshape(shape).astype(x.dtype)
    x1, x2 = x[..., :m], x[..., m:]
    return jnp.concatenate([x1 * cos - x2 * sin, x2 * cos + x1 * sin], axis=-1)


def _rope_2d(x, ang_row, ang_col):
    return jnp.concatenate([_rope_half(x[..., :ROPE_AXIS_DIM], ang_row),
                            _rope_half(x[..., ROPE_AXIS_DIM:], ang_col)], axis=-1)


def _attn_qkv(h, w_qkv, q_gain, k_gain):
    b, l, _ = h.shape
    qkv = h @ w_qkv
    q = qkv[..., :ATTN_Q_DIM].reshape(b, l, ATTN_KV_HEADS, ATTN_GROUP, ATTN_HEAD_DIM)
    k = qkv[..., ATTN_Q_DIM:ATTN_Q_DIM + ATTN_KV_DIM].reshape(b, l, ATTN_KV_HEADS, ATTN_HEAD_DIM)
    v = qkv[..., ATTN_Q_DIM + ATTN_KV_DIM:].reshape(b, l, ATTN_KV_HEADS, ATTN_HEAD_DIM)
    return _rms_norm(q, q_gain), _rms_norm(k, k_gain), v


def _attn_mixer(h_lat, h_ctx, w_qkv, q_gain, k_gain, sinks, w_o, ang_row, ang_col, need_ctx_out):
    scale = ATTN_HEAD_DIM ** -0.5
    q_c, k_c, v_c = _attn_qkv(h_ctx, w_qkv, q_gain, k_gain)
    q, k, v = _attn_qkv(h_lat, w_qkv, q_gain, k_gain)
    q = _rope_2d(q, ang_row, ang_col)
    k = _rope_2d(k, ang_row, ang_col)
    b, n = q.shape[:2]
    lc = k_c.shape[1]
    nb = n // ATTN_BLOCK
    sink = sinks.astype(jnp.float32).reshape(ATTN_KV_HEADS, ATTN_GROUP)

    qb = q.reshape(b, nb, ATTN_BLOCK, ATTN_KV_HEADS, ATTN_GROUP, ATTN_HEAD_DIM)

    def band(t):
        tb = t.reshape(b, nb, ATTN_BLOCK, ATTN_KV_HEADS, ATTN_HEAD_DIM)
        tp = jnp.pad(tb, ((0, 0), (1, 1), (0, 0), (0, 0), (0, 0)))
        return jnp.concatenate([tp[:, :-2], tp[:, 1:-1], tp[:, 2:]], axis=2)

    k_band, v_band = band(k), band(v)
    qi = jnp.arange(ATTN_BLOCK)[:, None]
    kj = jnp.arange(3 * ATTN_BLOCK)[None, :]
    kpos = (jnp.arange(nb)[:, None, None] - 1) * ATTN_BLOCK + kj[None]
    band_mask = (jnp.abs(kj - ATTN_BLOCK - qi) <= ATTN_WINDOW)[None] & (kpos >= 0) & (kpos < n)

    s_ctx = jnp.einsum('bkqhgd,bshd->bkhgqs', qb, k_c).astype(jnp.float32) * scale
    s_band = jnp.einsum('bkqhgd,bkshd->bkhgqs', qb, k_band).astype(jnp.float32) * scale
    s_band = jnp.where(band_mask[None, :, None, None], s_band, -jnp.inf)
    s_sink = jnp.broadcast_to(sink[None, None, :, :, None, None], s_band.shape[:-1] + (1,))
    p = jax.nn.softmax(jnp.concatenate([s_ctx, s_band, s_sink], axis=-1), axis=-1)
    p_ctx = p[..., :lc].astype(v.dtype)
    p_band = p[..., lc:-1].astype(v.dtype)
    o = (jnp.einsum('bkhgqs,bshd->bkqhgd', p_ctx, v_c)
         + jnp.einsum('bkhgqs,bkshd->bkqhgd', p_band, v_band))
    out_lat = o.reshape(b, n, ATTN_Q_DIM) @ w_o

    if need_ctx_out:
        s = jnp.einsum('bqhgd,bshd->bhgqs', q_c, k_c).astype(jnp.float32) * scale
        s_sk = jnp.broadcast_to(sink[None, :, :, None, None], s.shape[:-1] + (1,))
        pc = jax.nn.softmax(jnp.concatenate([s, s_sk], axis=-1), axis=-1)[..., :-1].astype(v_c.dtype)
        oc = jnp.einsum('bhgqs,bshd->bqhgd', pc, v_c)
        out_ctx = oc.reshape(b, lc, ATTN_Q_DIM) @ w_o
    else:
        out_ctx = None
    return out_lat, out_ctx


def _conv_ffn(h, w_up, conv_w, conv_b, w_down):
    u = _dwconv(h @ w_up, conv_w, conv_b)
    val, gate = u[..., :D_FF], u[..., D_FF:]
    return (jax.nn.silu(gate) * val) @ w_down


def _fwd_setup_inputs(seed: int = 0) -> dict:
    key = jax.random.key(seed)
    ks = jax.random.split(key, 32)
    f32 = jnp.float32

    def nrm(k, shape, s):
        return jax.random.normal(k, shape, f32) * s

    L, S, A = DEPTH, N_SSD_LAYERS, N_ATTN_LAYERS

    def dt_bias(k):
        dt = jnp.exp(jax.random.uniform(k, (S, SSD_HEADS), f32, math.log(1e-3), math.log(1e-1)))
        return dt + jnp.log(-jnp.expm1(-dt))

    return {
        "x": nrm(ks[0], (BATCH, SEQ, D_MODEL), 1.0),
        "c": nrm(ks[1], (BATCH, D_MODEL), 1.0),
        "ctx": nrm(ks[2], (BATCH, CTX_LEN, D_MODEL), 1.0),
        "c_ctx": nrm(ks[3], (D_MODEL,), 1.0),
        "ada_w": nrm(ks[4], (L, D_MODEL, 6 * D_MODEL), 0.5 * D_MODEL ** -0.5),
        "ada_b": nrm(ks[5], (L, 6 * D_MODEL), 0.01),
        "norm1_w": 1.0 + nrm(ks[6], (L, D_MODEL), 0.05),
        "norm2_w": 1.0 + nrm(ks[7], (L, D_MODEL), 0.05),
        "ssd_w_in": nrm(ks[8], (S, D_MODEL, SSD_IN), D_MODEL ** -0.5),
        "ssd_conv_w": nrm(ks[9], (S, SSD_CONV, SSD_XBC), SSD_CONV ** -0.5),
        "ssd_conv_b": nrm(ks[10], (S, SSD_XBC), 0.01),
        "ssd_dt_bias_f": dt_bias(ks[11]),
        "ssd_dt_bias_b": dt_bias(ks[12]),
        "ssd_a_log_f": jnp.log(jax.random.uniform(ks[13], (S, SSD_HEADS), f32, 1.0, 16.0)),
        "ssd_a_log_b": jnp.log(jax.random.uniform(ks[14], (S, SSD_HEADS), f32, 1.0, 16.0)),
        "ssd_d": 1.0 + nrm(ks[15], (S, SSD_HEADS), 0.1),
        "ssd_norm_w": 1.0 + nrm(ks[16], (S, SSD_D_INNER), 0.05),
        "ssd_w_out": nrm(ks[17], (S, SSD_D_INNER, D_MODEL), SSD_D_INNER ** -0.5),
        "attn_w_qkv": nrm(ks[18], (A, D_MODEL, ATTN_QKV), D_MODEL ** -0.5),
        "attn_q_gain": 1.0 + nrm(ks[19], (A, ATTN_HEAD_DIM), 0.05),
        "attn_k_gain": 1.0 + nrm(ks[20], (A, ATTN_HEAD_DIM), 0.05),
        "attn_sinks": nrm(ks[21], (A, ATTN_HEADS), 0.5),
        "attn_w_o": nrm(ks[22], (A, ATTN_Q_DIM, D_MODEL), ATTN_Q_DIM ** -0.5),
        "ffn_w_up": nrm(ks[23], (L, D_MODEL, 2 * D_FF), D_MODEL ** -0.5),
        "ffn_conv_w": nrm(ks[24], (L, FFN_CONV, 2 * D_FF), FFN_CONV ** -0.5),
        "ffn_conv_b": nrm(ks[25], (L, 2 * D_FF), 0.01),
        "ffn_w_down": nrm(ks[26], (L, D_FF, D_MODEL), D_FF ** -0.5),
    }


def _fwd_reference(x, c, ctx, c_ctx, ada_w, ada_b, norm1_w, norm2_w, ssd_w_in, ssd_conv_w, ssd_conv_b,
              ssd_dt_bias_f, ssd_dt_bias_b, ssd_a_log_f, ssd_a_log_b, ssd_d, ssd_norm_w, ssd_w_out,
              attn_w_qkv, attn_q_gain, attn_k_gain, attn_sinks, attn_w_o,
              ffn_w_up, ffn_conv_w, ffn_conv_b, ffn_w_down):
    n = x.shape[1]
    rows = n // GRID_W
    ang_row, ang_col = _axial_angles(rows)
    xc = ctx
    for i in range(DEPTH):
        last = i == DEPTH - 1
        sh1, sc1, g1, sh2, sc2, g2 = [t[:, None, :] for t in _modulation(c, ada_w[i], ada_b[i])]
        csh1, csc1, cg1, csh2, csc2, cg2 = _modulation(c_ctx, ada_w[i], ada_b[i])
        h = _modulate(_rms_norm(x, norm1_w[i]), sh1, sc1)
        hc = _modulate(_rms_norm(xc, norm1_w[i]), csh1, csc1)
        j = i // N_MIXERS
        if i % N_MIXERS == 0:
            mix, mix_c = _ssd_mixer(h, hc, ssd_w_in[j], ssd_conv_w[j], ssd_conv_b[j],
                                    ssd_dt_bias_f[j], ssd_dt_bias_b[j], ssd_a_log_f[j], ssd_a_log_b[j],
                                    ssd_d[j], ssd_norm_w[j], ssd_w_out[j], not last)
        else:
            mix, mix_c = _attn_mixer(h, hc, attn_w_qkv[j], attn_q_gain[j], attn_k_gain[j],
                                     attn_sinks[j], attn_w_o[j], ang_row, ang_col, not last)
        x = x + g1 * mix
        h = _modulate(_rms_norm(x, norm2_w[i]), sh2, sc2)
        x = x + g2 * _conv_ffn(h, ffn_w_up[i], ffn_conv_w[i], ffn_conv_b[i], ffn_w_down[i])
        if not last:
            xc = xc + cg1 * mix_c
            hc = _modulate(_rms_norm(xc, norm2_w[i]), csh2, csc2)
            xc = xc + cg2 * _conv_ffn(hc, ffn_w_up[i], ffn_conv_w[i], ffn_conv_b[i], ffn_w_down[i])
    return x


import jax as _jax
import jax.numpy as _jnp

TWIN_FORMAT = 'train_step'
FWD_PARAMS = ['x', 'c', 'ctx', 'c_ctx', 'ada_w', 'ada_b', 'norm1_w', 'norm2_w', 'ssd_w_in', 'ssd_conv_w', 'ssd_conv_b', 'ssd_dt_bias_f', 'ssd_dt_bias_b', 'ssd_a_log_f', 'ssd_a_log_b', 'ssd_d', 'ssd_norm_w', 'ssd_w_out', 'attn_w_qkv', 'attn_q_gain', 'attn_k_gain', 'attn_sinks', 'attn_w_o', 'ffn_w_up', 'ffn_conv_w', 'ffn_conv_b', 'ffn_w_down']
TWIN_WEIGHTS = ['c_ctx', 'ada_w', 'ada_b', 'norm1_w', 'norm2_w', 'ssd_w_in', 'ssd_conv_w', 'ssd_conv_b', 'ssd_dt_bias_f', 'ssd_dt_bias_b', 'ssd_a_log_f', 'ssd_a_log_b', 'ssd_d', 'ssd_norm_w', 'ssd_w_out', 'attn_w_qkv', 'attn_q_gain', 'attn_k_gain', 'attn_sinks', 'attn_w_o', 'ffn_w_up', 'ffn_conv_w', 'ffn_conv_b', 'ffn_w_down']
TWIN_DIFF_INPUT = 'x'
TWIN_INPUTS = ['x', 'c', 'ctx', 'c_ctx', 'ada_w', 'ada_b', 'norm1_w', 'norm2_w', 'ssd_w_in', 'ssd_conv_w', 'ssd_conv_b', 'ssd_dt_bias_f', 'ssd_dt_bias_b', 'ssd_a_log_f', 'ssd_a_log_b', 'ssd_d', 'ssd_norm_w', 'ssd_w_out', 'attn_w_qkv', 'attn_q_gain', 'attn_k_gain', 'attn_sinks', 'attn_w_o', 'ffn_w_up', 'ffn_conv_w', 'ffn_conv_b', 'ffn_w_down', 'loss_target', 'm_c_ctx', 'm_ada_w', 'm_ada_b', 'm_norm1_w', 'm_norm2_w', 'm_ssd_w_in', 'm_ssd_conv_w', 'm_ssd_conv_b', 'm_ssd_dt_bias_f', 'm_ssd_dt_bias_b', 'm_ssd_a_log_f', 'm_ssd_a_log_b', 'm_ssd_d', 'm_ssd_norm_w', 'm_ssd_w_out', 'm_attn_w_qkv', 'm_attn_q_gain', 'm_attn_k_gain', 'm_attn_sinks', 'm_attn_w_o', 'm_ffn_w_up', 'm_ffn_conv_w', 'm_ffn_conv_b', 'm_ffn_w_down', 'v_c_ctx', 'v_ada_w', 'v_ada_b', 'v_norm1_w', 'v_norm2_w', 'v_ssd_w_in', 'v_ssd_conv_w', 'v_ssd_conv_b', 'v_ssd_dt_bias_f', 'v_ssd_dt_bias_b', 'v_ssd_a_log_f', 'v_ssd_a_log_b', 'v_ssd_d', 'v_ssd_norm_w', 'v_ssd_w_out', 'v_attn_w_qkv', 'v_attn_q_gain', 'v_attn_k_gain', 'v_attn_sinks', 'v_attn_w_o', 'v_ffn_w_up', 'v_ffn_conv_w', 'v_ffn_conv_b', 'v_ffn_w_down']
TWIN_OUTPUTS = ['loss', 'grad_x', 'grad_c_ctx', 'grad_ada_w', 'grad_ada_b', 'grad_norm1_w', 'grad_norm2_w', 'grad_ssd_w_in', 'grad_ssd_conv_w', 'grad_ssd_conv_b', 'grad_ssd_dt_bias_f', 'grad_ssd_dt_bias_b', 'grad_ssd_a_log_f', 'grad_ssd_a_log_b', 'grad_ssd_d', 'grad_ssd_norm_w', 'grad_ssd_w_out', 'grad_attn_w_qkv', 'grad_attn_q_gain', 'grad_attn_k_gain', 'grad_attn_sinks', 'grad_attn_w_o', 'grad_ffn_w_up', 'grad_ffn_conv_w', 'grad_ffn_conv_b', 'grad_ffn_w_down', 'delta_c_ctx', 'delta_ada_w', 'delta_ada_b', 'delta_norm1_w', 'delta_norm2_w', 'delta_ssd_w_in', 'delta_ssd_conv_w', 'delta_ssd_conv_b', 'delta_ssd_dt_bias_f', 'delta_ssd_dt_bias_b', 'delta_ssd_a_log_f', 'delta_ssd_a_log_b', 'delta_ssd_d', 'delta_ssd_norm_w', 'delta_ssd_w_out', 'delta_attn_w_qkv', 'delta_attn_q_gain', 'delta_attn_k_gain', 'delta_attn_sinks', 'delta_attn_w_o', 'delta_ffn_w_up', 'delta_ffn_conv_w', 'delta_ffn_conv_b', 'delta_ffn_w_down', 'new_m_c_ctx', 'new_m_ada_w', 'new_m_ada_b', 'new_m_norm1_w', 'new_m_norm2_w', 'new_m_ssd_w_in', 'new_m_ssd_conv_w', 'new_m_ssd_conv_b', 'new_m_ssd_dt_bias_f', 'new_m_ssd_dt_bias_b', 'new_m_ssd_a_log_f', 'new_m_ssd_a_log_b', 'new_m_ssd_d', 'new_m_ssd_norm_w', 'new_m_ssd_w_out', 'new_m_attn_w_qkv', 'new_m_attn_q_gain', 'new_m_attn_k_gain', 'new_m_attn_sinks', 'new_m_attn_w_o', 'new_m_ffn_w_up', 'new_m_ffn_conv_w', 'new_m_ffn_conv_b', 'new_m_ffn_w_down', 'new_v_c_ctx', 'new_v_ada_w', 'new_v_ada_b', 'new_v_norm1_w', 'new_v_norm2_w', 'new_v_ssd_w_in', 'new_v_ssd_conv_w', 'new_v_ssd_conv_b', 'new_v_ssd_dt_bias_f', 'new_v_ssd_dt_bias_b', 'new_v_ssd_a_log_f', 'new_v_ssd_a_log_b', 'new_v_ssd_d', 'new_v_ssd_norm_w', 'new_v_ssd_w_out', 'new_v_attn_w_qkv', 'new_v_attn_q_gain', 'new_v_attn_k_gain', 'new_v_attn_sinks', 'new_v_attn_w_o', 'new_v_ffn_w_up', 'new_v_ffn_conv_w', 'new_v_ffn_conv_b', 'new_v_ffn_w_down']
TWIN_LEAF_KINDS = {'loss': 'loss', 'grad_x': 'grad_x', 'grad_c_ctx': 'grad_w', 'grad_ada_w': 'grad_w', 'grad_ada_b': 'grad_w', 'grad_norm1_w': 'grad_w', 'grad_norm2_w': 'grad_w', 'grad_ssd_w_in': 'grad_w', 'grad_ssd_conv_w': 'grad_w', 'grad_ssd_conv_b': 'grad_w', 'grad_ssd_dt_bias_f': 'grad_w', 'grad_ssd_dt_bias_b': 'grad_w', 'grad_ssd_a_log_f': 'grad_w', 'grad_ssd_a_log_b': 'grad_w', 'grad_ssd_d': 'grad_w', 'grad_ssd_norm_w': 'grad_w', 'grad_ssd_w_out': 'grad_w', 'grad_attn_w_qkv': 'grad_w', 'grad_attn_q_gain': 'grad_w', 'grad_attn_k_gain': 'grad_w', 'grad_attn_sinks': 'grad_w', 'grad_attn_w_o': 'grad_w', 'grad_ffn_w_up': 'grad_w', 'grad_ffn_conv_w': 'grad_w', 'grad_ffn_conv_b': 'grad_w', 'grad_ffn_w_down': 'grad_w', 'delta_c_ctx': 'delta_w', 'delta_ada_w': 'delta_w', 'delta_ada_b': 'delta_w', 'delta_norm1_w': 'delta_w', 'delta_norm2_w': 'delta_w', 'delta_ssd_w_in': 'delta_w', 'delta_ssd_conv_w': 'delta_w', 'delta_ssd_conv_b': 'delta_w', 'delta_ssd_dt_bias_f': 'delta_w', 'delta_ssd_dt_bias_b': 'delta_w', 'delta_ssd_a_log_f': 'delta_w', 'delta_ssd_a_log_b': 'delta_w', 'delta_ssd_d': 'delta_w', 'delta_ssd_norm_w': 'delta_w', 'delta_ssd_w_out': 'delta_w', 'delta_attn_w_qkv': 'delta_w', 'delta_attn_q_gain': 'delta_w', 'delta_attn_k_gain': 'delta_w', 'delta_attn_sinks': 'delta_w', 'delta_attn_w_o': 'delta_w', 'delta_ffn_w_up': 'delta_w', 'delta_ffn_conv_w': 'delta_w', 'delta_ffn_conv_b': 'delta_w', 'delta_ffn_w_down': 'delta_w', 'new_m_c_ctx': 'new_m', 'new_m_ada_w': 'new_m', 'new_m_ada_b': 'new_m', 'new_m_norm1_w': 'new_m', 'new_m_norm2_w': 'new_m', 'new_m_ssd_w_in': 'new_m', 'new_m_ssd_conv_w': 'new_m', 'new_m_ssd_conv_b': 'new_m', 'new_m_ssd_dt_bias_f': 'new_m', 'new_m_ssd_dt_bias_b': 'new_m', 'new_m_ssd_a_log_f': 'new_m', 'new_m_ssd_a_log_b': 'new_m', 'new_m_ssd_d': 'new_m', 'new_m_ssd_norm_w': 'new_m', 'new_m_ssd_w_out': 'new_m', 'new_m_attn_w_qkv': 'new_m', 'new_m_attn_q_gain': 'new_m', 'new_m_attn_k_gain': 'new_m', 'new_m_attn_sinks': 'new_m', 'new_m_attn_w_o': 'new_m', 'new_m_ffn_w_up': 'new_m', 'new_m_ffn_conv_w': 'new_m', 'new_m_ffn_conv_b': 'new_m', 'new_m_ffn_w_down': 'new_m', 'new_v_c_ctx': 'new_v', 'new_v_ada_w': 'new_v', 'new_v_ada_b': 'new_v', 'new_v_norm1_w': 'new_v', 'new_v_norm2_w': 'new_v', 'new_v_ssd_w_in': 'new_v', 'new_v_ssd_conv_w': 'new_v', 'new_v_ssd_conv_b': 'new_v', 'new_v_ssd_dt_bias_f': 'new_v', 'new_v_ssd_dt_bias_b': 'new_v', 'new_v_ssd_a_log_f': 'new_v', 'new_v_ssd_a_log_b': 'new_v', 'new_v_ssd_d': 'new_v', 'new_v_ssd_norm_w': 'new_v', 'new_v_ssd_w_out': 'new_v', 'new_v_attn_w_qkv': 'new_v', 'new_v_attn_q_gain': 'new_v', 'new_v_attn_k_gain': 'new_v', 'new_v_attn_sinks': 'new_v', 'new_v_attn_w_o': 'new_v', 'new_v_ffn_w_up': 'new_v', 'new_v_ffn_conv_w': 'new_v', 'new_v_ffn_conv_b': 'new_v', 'new_v_ffn_w_down': 'new_v'}


def _forward(args):
    return _fwd_reference(*[args[k] for k in FWD_PARAMS])


def _output_shape():
    def fwd():
        inp = _fwd_setup_inputs(0)
        return _fwd_reference(*[inp[k] for k in FWD_PARAMS])
    out = _jax.eval_shape(fwd)
    return out.shape, out.dtype

N_MICROBATCH = 1
ADAM_LR = 0.001
ADAM_B1 = 0.9
ADAM_B2 = 0.999
ADAM_EPS = 1e-08
ADAM_WD = 0.01
ADAM_STEP = 10
PER_EXAMPLE_BATCH_AXIS = {'x': 0, 'c': 0, 'ctx': 0, 'loss_target': 0}
SHARED_INPUTS = []
_WEIGHT_DTYPES = {'c_ctx': _jnp.float32, 'ada_w': _jnp.float32, 'ada_b': _jnp.float32, 'norm1_w': _jnp.float32, 'norm2_w': _jnp.float32, 'ssd_w_in': _jnp.float32, 'ssd_conv_w': _jnp.float32, 'ssd_conv_b': _jnp.float32, 'ssd_dt_bias_f': _jnp.float32, 'ssd_dt_bias_b': _jnp.float32, 'ssd_a_log_f': _jnp.float32, 'ssd_a_log_b': _jnp.float32, 'ssd_d': _jnp.float32, 'ssd_norm_w': _jnp.float32, 'ssd_w_out': _jnp.float32, 'attn_w_qkv': _jnp.float32, 'attn_q_gain': _jnp.float32, 'attn_k_gain': _jnp.float32, 'attn_sinks': _jnp.float32, 'attn_w_o': _jnp.float32, 'ffn_w_up': _jnp.float32, 'ffn_conv_w': _jnp.float32, 'ffn_conv_b': _jnp.float32, 'ffn_w_down': _jnp.float32}
MOMENT_SCALE = {'c_ctx': 4.021935e-01, 'ada_w': 8.605848e-01, 'ada_b': 2.156292e+00, 'norm1_w': 9.472596e-02, 'norm2_w': 3.179384e+00, 'ssd_w_in': 8.014086e-02, 'ssd_conv_w': 1.552269e-01, 'ssd_conv_b': 3.046320e-01, 'ssd_dt_bias_f': 2.620939e-01, 'ssd_dt_bias_b': 3.925067e-01, 'ssd_a_log_f': 5.527967e-01, 'ssd_a_log_b': 4.652402e-01, 'ssd_d': 4.989385e-01, 'ssd_norm_w': 2.762452e+00, 'ssd_w_out': 2.547538e-01, 'attn_w_qkv': 2.179240e-01, 'attn_q_gain': 2.812184e-01, 'attn_k_gain': 2.774417e-01, 'attn_sinks': 2.387310e-02, 'attn_w_o': 1.861747e-01, 'ffn_w_up': 8.348636e-02, 'ffn_conv_w': 4.718926e-01, 'ffn_conv_b': 3.947407e-01, 'ffn_w_down': 7.079205e-02}


def _to_microbatches(a, axis):
    t = _jnp.moveaxis(a, axis, 0)
    t = t.reshape((N_MICROBATCH, t.shape[0] // N_MICROBATCH) + t.shape[1:])
    return _jnp.moveaxis(t, 1, axis + 1)


def setup_inputs(seed: int = 0) -> dict:
    inp = _fwd_setup_inputs(seed)
    key = _jax.random.fold_in(_jax.random.key(seed), 7919)
    shape, _ = _output_shape()
    out = dict(inp)
    out["loss_target"] = _jax.random.normal(_jax.random.fold_in(key, 0), shape, _jnp.float32)
    for i, name in enumerate(TWIN_WEIGHTS):
        w = inp[name].astype(_jnp.float32)
        if MOMENT_SCALE is None:
            s = _jnp.sqrt(_jnp.mean(_jnp.square(w)) + 1e-30)
        else:
            s = MOMENT_SCALE[name]
        km, kv = _jax.random.split(_jax.random.fold_in(key, i + 1))
        out[name] = w
        out["m_" + name] = s * _jax.random.normal(km, w.shape, _jnp.float32)
        out["v_" + name] = (s * s) * _jax.random.uniform(kv, w.shape, _jnp.float32, 0.5, 1.5)
    if N_MICROBATCH > 1:
        for name, axis in PER_EXAMPLE_BATCH_AXIS.items():
            out[name] = _to_microbatches(out[name], axis)
    return {'x': out['x'], 'c': out['c'], 'ctx': out['ctx'], 'c_ctx': out['c_ctx'], 'ada_w': out['ada_w'], 'ada_b': out['ada_b'], 'norm1_w': out['norm1_w'], 'norm2_w': out['norm2_w'], 'ssd_w_in': out['ssd_w_in'], 'ssd_conv_w': out['ssd_conv_w'], 'ssd_conv_b': out['ssd_conv_b'], 'ssd_dt_bias_f': out['ssd_dt_bias_f'], 'ssd_dt_bias_b': out['ssd_dt_bias_b'], 'ssd_a_log_f': out['ssd_a_log_f'], 'ssd_a_log_b': out['ssd_a_log_b'], 'ssd_d': out['ssd_d'], 'ssd_norm_w': out['ssd_norm_w'], 'ssd_w_out': out['ssd_w_out'], 'attn_w_qkv': out['attn_w_qkv'], 'attn_q_gain': out['attn_q_gain'], 'attn_k_gain': out['attn_k_gain'], 'attn_sinks': out['attn_sinks'], 'attn_w_o': out['attn_w_o'], 'ffn_w_up': out['ffn_w_up'], 'ffn_conv_w': out['ffn_conv_w'], 'ffn_conv_b': out['ffn_conv_b'], 'ffn_w_down': out['ffn_w_down'], 'loss_target': out['loss_target'], 'm_c_ctx': out['m_c_ctx'], 'm_ada_w': out['m_ada_w'], 'm_ada_b': out['m_ada_b'], 'm_norm1_w': out['m_norm1_w'], 'm_norm2_w': out['m_norm2_w'], 'm_ssd_w_in': out['m_ssd_w_in'], 'm_ssd_conv_w': out['m_ssd_conv_w'], 'm_ssd_conv_b': out['m_ssd_conv_b'], 'm_ssd_dt_bias_f': out['m_ssd_dt_bias_f'], 'm_ssd_dt_bias_b': out['m_ssd_dt_bias_b'], 'm_ssd_a_log_f': out['m_ssd_a_log_f'], 'm_ssd_a_log_b': out['m_ssd_a_log_b'], 'm_ssd_d': out['m_ssd_d'], 'm_ssd_norm_w': out['m_ssd_norm_w'], 'm_ssd_w_out': out['m_ssd_w_out'], 'm_attn_w_qkv': out['m_attn_w_qkv'], 'm_attn_q_gain': out['m_attn_q_gain'], 'm_attn_k_gain': out['m_attn_k_gain'], 'm_attn_sinks': out['m_attn_sinks'], 'm_attn_w_o': out['m_attn_w_o'], 'm_ffn_w_up': out['m_ffn_w_up'], 'm_ffn_conv_w': out['m_ffn_conv_w'], 'm_ffn_conv_b': out['m_ffn_conv_b'], 'm_ffn_w_down': out['m_ffn_w_down'], 'v_c_ctx': out['v_c_ctx'], 'v_ada_w': out['v_ada_w'], 'v_ada_b': out['v_ada_b'], 'v_norm1_w': out['v_norm1_w'], 'v_norm2_w': out['v_norm2_w'], 'v_ssd_w_in': out['v_ssd_w_in'], 'v_ssd_conv_w': out['v_ssd_conv_w'], 'v_ssd_conv_b': out['v_ssd_conv_b'], 'v_ssd_dt_bias_f': out['v_ssd_dt_bias_f'], 'v_ssd_dt_bias_b': out['v_ssd_dt_bias_b'], 'v_ssd_a_log_f': out['v_ssd_a_log_f'], 'v_ssd_a_log_b': out['v_ssd_a_log_b'], 'v_ssd_d': out['v_ssd_d'], 'v_ssd_norm_w': out['v_ssd_norm_w'], 'v_ssd_w_out': out['v_ssd_w_out'], 'v_attn_w_qkv': out['v_attn_w_qkv'], 'v_attn_q_gain': out['v_attn_q_gain'], 'v_attn_k_gain': out['v_attn_k_gain'], 'v_attn_sinks': out['v_attn_sinks'], 'v_attn_w_o': out['v_attn_w_o'], 'v_ffn_w_up': out['v_ffn_w_up'], 'v_ffn_conv_w': out['v_ffn_conv_w'], 'v_ffn_conv_b': out['v_ffn_conv_b'], 'v_ffn_w_down': out['v_ffn_w_down']}


def _loss(weights, diff, rest, loss_target):
    with _jax.named_scope("forward"):
        args = {**rest, TWIN_DIFF_INPUT: diff, **{k: w.astype(_WEIGHT_DTYPES[k]) for k, w in weights.items()}}
        y = _forward(args)
    with _jax.named_scope("loss_head"):
        err = _jnp.square(y.astype(_jnp.float32) - loss_target)
        return 0.5 * _jnp.sum(_jnp.mean(err, axis=-1)) if err.ndim else 0.5 * err


def _adamw(w, g, m, v):
    m = ADAM_B1 * m + (1.0 - ADAM_B1) * g
    v = ADAM_B2 * v + (1.0 - ADAM_B2) * _jnp.square(g)
    m_hat = m / (1.0 - ADAM_B1 ** ADAM_STEP)
    v_hat = v / (1.0 - ADAM_B2 ** ADAM_STEP)
    delta = -ADAM_LR * (m_hat / (_jnp.sqrt(v_hat) + ADAM_EPS) + ADAM_WD * w)
    return delta, m, v


def reference(x, c, ctx, c_ctx, ada_w, ada_b, norm1_w, norm2_w, ssd_w_in, ssd_conv_w, ssd_conv_b, ssd_dt_bias_f, ssd_dt_bias_b, ssd_a_log_f, ssd_a_log_b, ssd_d, ssd_norm_w, ssd_w_out, attn_w_qkv, attn_q_gain, attn_k_gain, attn_sinks, attn_w_o, ffn_w_up, ffn_conv_w, ffn_conv_b, ffn_w_down, loss_target, m_c_ctx, m_ada_w, m_ada_b, m_norm1_w, m_norm2_w, m_ssd_w_in, m_ssd_conv_w, m_ssd_conv_b, m_ssd_dt_bias_f, m_ssd_dt_bias_b, m_ssd_a_log_f, m_ssd_a_log_b, m_ssd_d, m_ssd_norm_w, m_ssd_w_out, m_attn_w_qkv, m_attn_q_gain, m_attn_k_gain, m_attn_sinks, m_attn_w_o, m_ffn_w_up, m_ffn_conv_w, m_ffn_conv_b, m_ffn_w_down, v_c_ctx, v_ada_w, v_ada_b, v_norm1_w, v_norm2_w, v_ssd_w_in, v_ssd_conv_w, v_ssd_conv_b, v_ssd_dt_bias_f, v_ssd_dt_bias_b, v_ssd_a_log_f, v_ssd_a_log_b, v_ssd_d, v_ssd_norm_w, v_ssd_w_out, v_attn_w_qkv, v_attn_q_gain, v_attn_k_gain, v_attn_sinks, v_attn_w_o, v_ffn_w_up, v_ffn_conv_w, v_ffn_conv_b, v_ffn_w_down):
    given = dict(x=x, c=c, ctx=ctx, c_ctx=c_ctx, ada_w=ada_w, ada_b=ada_b, norm1_w=norm1_w, norm2_w=norm2_w, ssd_w_in=ssd_w_in, ssd_conv_w=ssd_conv_w, ssd_conv_b=ssd_conv_b, ssd_dt_bias_f=ssd_dt_bias_f, ssd_dt_bias_b=ssd_dt_bias_b, ssd_a_log_f=ssd_a_log_f, ssd_a_log_b=ssd_a_log_b, ssd_d=ssd_d, ssd_norm_w=ssd_norm_w, ssd_w_out=ssd_w_out, attn_w_qkv=attn_w_qkv, attn_q_gain=attn_q_gain, attn_k_gain=attn_k_gain, attn_sinks=attn_sinks, attn_w_o=attn_w_o, ffn_w_up=ffn_w_up, ffn_conv_w=ffn_conv_w, ffn_conv_b=ffn_conv_b, ffn_w_down=ffn_w_down, loss_target=loss_target, m_c_ctx=m_c_ctx, m_ada_w=m_ada_w, m_ada_b=m_ada_b, m_norm1_w=m_norm1_w, m_norm2_w=m_norm2_w, m_ssd_w_in=m_ssd_w_in, m_ssd_conv_w=m_ssd_conv_w, m_ssd_conv_b=m_ssd_conv_b, m_ssd_dt_bias_f=m_ssd_dt_bias_f, m_ssd_dt_bias_b=m_ssd_dt_bias_b, m_ssd_a_log_f=m_ssd_a_log_f, m_ssd_a_log_b=m_ssd_a_log_b, m_ssd_d=m_ssd_d, m_ssd_norm_w=m_ssd_norm_w, m_ssd_w_out=m_ssd_w_out, m_attn_w_qkv=m_attn_w_qkv, m_attn_q_gain=m_attn_q_gain, m_attn_k_gain=m_attn_k_gain, m_attn_sinks=m_attn_sinks, m_attn_w_o=m_attn_w_o, m_ffn_w_up=m_ffn_w_up, m_ffn_conv_w=m_ffn_conv_w, m_ffn_conv_b=m_ffn_conv_b, m_ffn_w_down=m_ffn_w_down, v_c_ctx=v_c_ctx, v_ada_w=v_ada_w, v_ada_b=v_ada_b, v_norm1_w=v_norm1_w, v_norm2_w=v_norm2_w, v_ssd_w_in=v_ssd_w_in, v_ssd_conv_w=v_ssd_conv_w, v_ssd_conv_b=v_ssd_conv_b, v_ssd_dt_bias_f=v_ssd_dt_bias_f, v_ssd_dt_bias_b=v_ssd_dt_bias_b, v_ssd_a_log_f=v_ssd_a_log_f, v_ssd_a_log_b=v_ssd_a_log_b, v_ssd_d=v_ssd_d, v_ssd_norm_w=v_ssd_norm_w, v_ssd_w_out=v_ssd_w_out, v_attn_w_qkv=v_attn_w_qkv, v_attn_q_gain=v_attn_q_gain, v_attn_k_gain=v_attn_k_gain, v_attn_sinks=v_attn_sinks, v_attn_w_o=v_attn_w_o, v_ffn_w_up=v_ffn_w_up, v_ffn_conv_w=v_ffn_conv_w, v_ffn_conv_b=v_ffn_conv_b, v_ffn_w_down=v_ffn_w_down)
    weights = {n: given[n] for n in TWIN_WEIGHTS}
    shared = {n: given[n] for n in SHARED_INPUTS}
    per_example = {n: given[n] for n in ['x', 'c', 'ctx']}
    grad_fn = _jax.value_and_grad(_loss, argnums=(0, 1))

    def one_microbatch(ex, loss_target):
        ex = dict(ex)
        diff = ex.pop(TWIN_DIFF_INPUT)
        return grad_fn(weights, diff, {**shared, **ex}, loss_target)

    if N_MICROBATCH == 1:
        loss, (grad_w, grad_x) = one_microbatch(per_example, given["loss_target"])
    else:
        def body(carry, xs):
            loss_sum, grad_sum = carry
            l_k, (gw_k, gx_k) = one_microbatch(xs[0], xs[1])
            with _jax.named_scope("update"):
                return (loss_sum + l_k, _jax.tree.map(_jnp.add, grad_sum, gw_k)), gx_k

        init = (_jnp.zeros((), _jnp.float32), _jax.tree.map(_jnp.zeros_like, weights))
        (loss, grad_w), grad_x = _jax.lax.scan(body, init, (per_example, given["loss_target"]))
    with _jax.named_scope("update"):
        delta_w, new_m, new_v = {}, {}, {}
        for n in TWIN_WEIGHTS:
            delta_w[n], new_m[n], new_v[n] = _adamw(weights[n], grad_w[n], given["m_" + n], given["v_" + n])
    return (loss, grad_x, *[grad_w[n] for n in TWIN_WEIGHTS], *[delta_w[n] for n in TWIN_WEIGHTS],
            *[new_m[n] for n in TWIN_WEIGHTS], *[new_v[n] for n in TWIN_WEIGHTS])
```

```python
import functools

import numpy as np
import jax
import jax.numpy as jnp
from jax import lax
from jax.experimental import pallas as pl
from jax.experimental.pallas import tpu as pltpu

F32 = jnp.float32
BF16 = jnp.bfloat16
MXU = BF16
NORM_EPS = 1e-6
VMEM_CAP = 56 * 1024 * 1024
HALO = 8
LANE = 128
NDEV = 8

GRID_W = 64
ROPE_THETA = 10000.0
ATTN_KV_HEADS = 4
ATTN_WINDOW = 128
BLK = 128
SSD_GROUPS = 8

ADAM_LR, ADAM_B1, ADAM_B2, ADAM_EPS, ADAM_WD, ADAM_STEP = 0.001, 0.9, 0.999, 1e-08, 0.01, 10

MESH_ID = pl.DeviceIdType.MESH


def _cparams(sem, est_bytes):
    lim = int(min(VMEM_CAP, max(16 * 1024 * 1024, est_bytes * 1.3 + (4 << 20))))
    return pltpu.CompilerParams(dimension_semantics=sem, vmem_limit_bytes=lim)


def _nbytes(shape, dtype):
    return int(np.prod(shape)) * jnp.dtype(dtype).itemsize


def _silu(x):
    return x * jax.nn.sigmoid(x)


def _dsilu(x):
    s = jax.nn.sigmoid(x)
    return s * (1.0 + x * (1.0 - s))


def _split3(v):
    h = v.astype(BF16)
    r = v - h.astype(F32)
    m = r.astype(BF16)
    l = (r - m.astype(F32)).astype(BF16)
    return h, m, l


def _dot(a, b, dn=(((1,), (0,)), ((), ()))):
    return lax.dot_general(a, b, dn, preferred_element_type=F32)


NT = (((1,), (1,)), ((), ()))
TN = (((0,), (0,)), ((), ()))


def _dot3_rhs(sel, v):
    return sum(_dot(sel, p) for p in _split3(v))


def _dot3_lhs(v, sel, dn=(((1,), (0,)), ((), ()))):
    return sum(_dot(p, sel, dn) for p in _split3(v))


def _ident(s):
    return s


def _ffn_perm(s):
    return (s % 2) * 4 + s // 2


def _mm(a, b, *, ta=False, tb=False, out_dtype=F32, tm, tn, tk, name, bslots=None, oslots=None,
        resid=None, gate=None, ctx_rows=0):
    M = a.shape[1] if ta else a.shape[0]
    K = a.shape[0] if ta else a.shape[1]
    if bslots is None:
        N = b.shape[0] if tb else b.shape[1]
    else:
        G = b.shape[2]
        N = b.shape[1] if tb else NDEV * G
        assert (NDEV * G == K) if tb else (b.shape[1] == K)
    tm, tn, tk = min(tm, M), min(tn, N), min(tk, K)
    if bslots is not None:
        if tb:
            tk = min(tk, G)
            assert G % tk == 0
        else:
            tn = min(tn, G)
            assert G % tn == 0
    if oslots is not None:
        Go = N // NDEV
        tn = min(tn, Go)
        assert Go % tn == 0
    assert M % tm == 0 and N % tn == 0 and K % tk == 0, (name, M, N, K, tm, tn, tk)
    nk = K // tk
    fused = resid is not None
    dn = (((0 if ta else 1,), (1 if tb else 0,)), ((), ()))

    def body(*refs):
        if fused:
            a_ref, b_ref, r_ref, g_ref, o_ref, x_ref = refs[:6]
            rest = refs[6:]
        else:
            a_ref, b_ref, o_ref = refs[:3]
            rest = refs[3:]
        bv = b_ref[0] if bslots is not None else b_ref[...]
        p = lax.dot_general(a_ref[...].astype(MXU), bv.astype(MXU), dn, preferred_element_type=F32)

        def finish(acc):
            if oslots is not None:
                o_ref[0] = acc.astype(o_ref.dtype)
            else:
                o_ref[...] = acc.astype(o_ref.dtype)
            if fused:
                row = pl.program_id(0) * tm + lax.broadcasted_iota(jnp.int32, (tm, 1), 0)
                g = jnp.where(row < ctx_rows, g_ref[0:1, :], g_ref[1:2, :])
                x_ref[...] = r_ref[...] + g * acc

        if nk == 1:
            finish(p)
        else:
            acc_ref = rest[0]
            k = pl.program_id(2)

            @pl.when(k == 0)
            def _():
                acc_ref[...] = p

            @pl.when(k > 0)
            def _():
                acc_ref[...] += p

            @pl.when(k == nk - 1)
            def _():
                finish(acc_ref[...])

    a_spec = pl.BlockSpec((tk, tm), lambda i, j, k: (k, i)) if ta else pl.BlockSpec((tm, tk), lambda i, j, k: (i, k))
    if bslots is None:
        b_spec = pl.BlockSpec((tn, tk), lambda i, j, k: (j, k)) if tb else pl.BlockSpec((tk, tn), lambda i, j, k: (k, j))
    elif tb:
        kpg = G // tk
        b_spec = pl.BlockSpec((1, tn, tk), lambda i, j, k: (bslots(k // kpg), j, k % kpg))
    else:
        npg = G // tn
        b_spec = pl.BlockSpec((1, tk, tn), lambda i, j, k: (bslots(j // npg), k, j % npg))
    if oslots is None:
        o_spec = pl.BlockSpec((tm, tn), lambda i, j, k: (i, j))
        o_shape = jax.ShapeDtypeStruct((M, N), out_dtype)
    else:
        opg = Go // tn
        o_spec = pl.BlockSpec((1, tm, tn), lambda i, j, k: (oslots(j // opg), i, j % opg))
        o_shape = jax.ShapeDtypeStruct((NDEV, M, Go), out_dtype)
    in_specs = [a_spec, b_spec]
    out_shape = [o_shape]
    out_specs = [o_spec]
    args = [a, b]
    est = 2 * (_nbytes((tm, tk), a.dtype) + _nbytes((tk, tn), b.dtype) + _nbytes((tm, tn), out_dtype)) + 3 * _nbytes((tm, tn), F32)
    if fused:
        in_specs += [o_spec, pl.BlockSpec((2, tn), lambda i, j, k: (0, j))]
        out_shape.append(jax.ShapeDtypeStruct((M, N), F32))
        out_specs.append(o_spec)
        args += [resid, gate]
        est += 4 * _nbytes((tm, tn), F32)
    scratch = [] if nk == 1 else [pltpu.VMEM((tm, tn), F32)]
    res = pl.pallas_call(
        body, name=name, grid=(M // tm, N // tn, nk), in_specs=in_specs, out_specs=out_specs, out_shape=out_shape,
        scratch_shapes=scratch, compiler_params=_cparams(("parallel", "parallel", "arbitrary"), est),
    )(*args)
    return res if fused else res[0]


def _stream_of(i, tr, ctx_rows):
    return jnp.where(i * tr < ctx_rows, 0, 1)


def _acc_by_stream(sums_ref, part, i, n_ctx):
    @pl.when((i == 0) | (i == n_ctx))
    def _():
        sums_ref[0] = part

    @pl.when((i != 0) & (i != n_ctx))
    def _():
        sums_ref[0] += part


def _normmod(x, nw, mod, which, ctx_rows, name, tr=256):
    T, D = x.shape
    tr = min(tr, ctx_rows)
    assert T % tr == 0 and ctx_rows % tr == 0
    s_sh, s_sc = 3 * which, 3 * which + 1

    def body(x_ref, nw_ref, mod_ref, h_ref):
        xv = x_ref[...]
        r = lax.rsqrt(jnp.mean(xv * xv, axis=-1, keepdims=True) + NORM_EPS)
        y = (xv * r) * nw_ref[...]
        h_ref[...] = (y * (1.0 + mod_ref[0, s_sc:s_sc + 1, :]) + mod_ref[0, s_sh:s_sh + 1, :]).astype(h_ref.dtype)

    return pl.pallas_call(
        body, name=name, grid=(T // tr,),
        in_specs=[pl.BlockSpec((tr, D), lambda i: (i, 0)), pl.BlockSpec((1, D), lambda i: (0, 0)),
                  pl.BlockSpec((1, 6, D), lambda i: (_stream_of(i, tr, ctx_rows), 0, 0))],
        out_specs=pl.BlockSpec((tr, D), lambda i: (i, 0)),
        out_shape=jax.ShapeDtypeStruct((T, D), MXU),
        compiler_params=_cparams(("parallel",), 10 * _nbytes((tr, D), F32)),
    )(x, nw, mod)


def _normmod_bwd(x, nw, mod, dh, dx_in, which, ctx_rows, name, tr=256):
    T, D = x.shape
    tr = min(tr, ctx_rows)
    s_sc = 3 * which + 1
    n_ctx = ctx_rows // tr

    def body(x_ref, nw_ref, mod_ref, dh_ref, dxi_ref, dx_ref, sums_ref):
        i = pl.program_id(0)
        xv = x_ref[...]
        r = lax.rsqrt(jnp.mean(xv * xv, axis=-1, keepdims=True) + NORM_EPS)
        xh = xv * r
        dh_v = dh_ref[...].astype(F32)
        sc1 = 1.0 + mod_ref[0, s_sc:s_sc + 1, :]
        nwv = nw_ref[...]
        dxh = dh_v * (nwv * sc1)
        dx_ref[...] = dxi_ref[...] + r * (dxh - xh * jnp.mean(dxh * xh, axis=-1, keepdims=True))
        t = dh_v * xh
        part = jnp.concatenate([jnp.sum(dh_v, axis=0, keepdims=True), jnp.sum(t * nwv, axis=0, keepdims=True),
                                jnp.sum(t * sc1, axis=0, keepdims=True), jnp.zeros((5, D), F32)], axis=0)
        _acc_by_stream(sums_ref, part, i, n_ctx)

    row = pl.BlockSpec((tr, D), lambda i: (i, 0))
    return pl.pallas_call(
        body, name=name, grid=(T // tr,),
        in_specs=[row, pl.BlockSpec((1, D), lambda i: (0, 0)),
                  pl.BlockSpec((1, 6, D), lambda i: (_stream_of(i, tr, ctx_rows), 0, 0)), row, row],
        out_specs=[row, pl.BlockSpec((1, 8, D), lambda i: (_stream_of(i, tr, ctx_rows), 0, 0))],
        out_shape=[jax.ShapeDtypeStruct((T, D), F32), jax.ShapeDtypeStruct((2, 8, D), F32)],
        compiler_params=_cparams(("arbitrary",), 16 * _nbytes((tr, D), F32)),
    )(x, nw, mod, dh, dx_in)


def _gate_bwd(dx, mix, mod, which, ctx_rows, name, tr=256):
    T, D = dx.shape
    tr = min(tr, ctx_rows)
    s_g = 3 * which + 2
    n_ctx = ctx_rows // tr

    def body(dx_ref, mix_ref, mod_ref, dm_ref, sums_ref):
        i = pl.program_id(0)
        dxv = dx_ref[...]
        dm_ref[...] = (dxv * mod_ref[0, s_g:s_g + 1, :]).astype(dm_ref.dtype)
        part = jnp.concatenate([jnp.sum(dxv * mix_ref[...], axis=0, keepdims=True), jnp.zeros((7, D), F32)], axis=0)
        _acc_by_stream(sums_ref, part, i, n_ctx)

    row = pl.BlockSpec((tr, D), lambda i: (i, 0))
    return pl.pallas_call(
        body, name=name, grid=(T // tr,),
        in_specs=[row, row, pl.BlockSpec((1, 6, D), lambda i: (_stream_of(i, tr, ctx_rows), 0, 0))],
        out_specs=[row, pl.BlockSpec((1, 8, D), lambda i: (_stream_of(i, tr, ctx_rows), 0, 0))],
        out_shape=[jax.ShapeDtypeStruct((T, D), MXU), jax.ShapeDtypeStruct((2, 8, D), F32)],
        compiler_params=_cparams(("arbitrary",), 10 * _nbytes((tr, D), F32)),
    )(dx, mix, mod)


def _halo_specs(tr, tn, n_row_tiles, col_of):
    g = tr // HALO
    last = n_row_tiles * g - 1
    return [pl.BlockSpec((HALO, tn), lambda j, i: (jnp.maximum(i * g - 1, 0), col_of(j))),
            pl.BlockSpec((tr, tn), lambda j, i: (i, col_of(j))),
            pl.BlockSpec((HALO, tn), lambda j, i: (jnp.minimum((i + 1) * g, last), col_of(j)))]


def _ext(p_ref, m_ref, n_ref):
    return jnp.concatenate([p_ref[...].astype(F32), m_ref[...].astype(F32), n_ref[...].astype(F32)], axis=0)


def _seq_masks(i, tr, ctx_rows, total_rows):
    row = i * tr - HALO + lax.broadcasted_iota(jnp.int32, (tr + 2 * HALO, 1), 0)
    has_prev = (row != 0) & (row != ctx_rows)
    has_next = (row != ctx_rows - 1) & (row != total_rows - 1)
    return has_prev, has_next


def _shift_down(e):
    return pltpu.roll(e, 1, 0)


def _shift_up(e):
    return pltpu.roll(e, e.shape[0] - 1, 0)


def _conv3(e, w, has_prev, has_next):
    prev = jnp.where(has_prev, _shift_down(e), 0.0)
    nxt = jnp.where(has_next, _shift_up(e), 0.0)
    return prev * w[0:1, :] + e * w[1:2, :] + nxt * w[2:3, :]


def _conv3_t(d, w, has_prev, has_next):
    from_next = jnp.where(has_next, _shift_up(d), 0.0)
    from_prev = jnp.where(has_prev, _shift_down(d), 0.0)
    return from_next * w[0:1, :] + d * w[1:2, :] + from_prev * w[2:3, :]


def _conv_wgrad(d, e, has_prev, has_next):
    n = e.shape[0]
    c = slice(HALO, n - HALO)
    prev = jnp.where(has_prev, _shift_down(e), 0.0)
    nxt = jnp.where(has_next, _shift_up(e), 0.0)
    dc = d[c]
    return jnp.concatenate([jnp.sum(dc * prev[c], axis=0, keepdims=True), jnp.sum(dc * e[c], axis=0, keepdims=True),
                            jnp.sum(dc * nxt[c], axis=0, keepdims=True), jnp.sum(dc, axis=0, keepdims=True),
                            jnp.zeros((4, e.shape[1]), F32)], axis=0)


def _acc_first(ref, part, i):
    @pl.when(i == 0)
    def _():
        ref[...] = part

    @pl.when(i > 0)
    def _():
        ref[...] += part


def _ffn_mid(u, cw, cb, ctx_rows, name, tr=128):
    T, F2 = u.shape
    G = F2 // NDEV
    tr = min(tr, ctx_rows)
    nr, nc = T // tr, NDEV // 2

    def body(up, um, un, w_ref, b_ref, a_ref):
        i = pl.program_id(1)
        hp, hn = _seq_masks(i, tr, ctx_rows, T)
        uc = _conv3(_ext(up, um, un), w_ref[...], hp, hn)[HALO:HALO + tr] + b_ref[...]
        a_ref[...] = (_silu(uc[:, G:]) * uc[:, :G]).astype(a_ref.dtype)

    return pl.pallas_call(
        body, name=name, grid=(nc, nr),
        in_specs=_halo_specs(tr, 2 * G, nr, lambda j: j) + [pl.BlockSpec((3, 2 * G), lambda j, i: (0, j)),
                                                             pl.BlockSpec((1, 2 * G), lambda j, i: (0, j))],
        out_specs=pl.BlockSpec((tr, G), lambda j, i: (i, j)),
        out_shape=jax.ShapeDtypeStruct((T, F2 // 2), MXU),
        compiler_params=_cparams(("parallel", "parallel"), 12 * _nbytes((tr + 16, 2 * G), F32)),
    )(u, u, u, cw, cb)


def _ffn_mid_bwd(u, da, cw, cb, ctx_rows, name, tr=128):
    T, F2 = u.shape
    G = F2 // NDEV
    tr = min(tr, ctx_rows)
    nr, nc = T // tr, NDEV // 2

    def body(up, um, un, dp, dm, dn_, w_ref, b_ref, du_ref, gw_ref):
        i = pl.program_id(1)
        hp, hn = _seq_masks(i, tr, ctx_rows, T)
        e = _ext(up, um, un)
        w = w_ref[...]
        uc = _conv3(e, w, hp, hn) + b_ref[...]
        val, gt = uc[:, :G], uc[:, G:]
        dav = _ext(dp, dm, dn_)
        duc = jnp.concatenate([dav * _silu(gt), dav * val * _dsilu(gt)], axis=1)
        du_ref[...] = _conv3_t(duc, w, hp, hn)[HALO:HALO + tr].astype(du_ref.dtype)
        _acc_first(gw_ref, _conv_wgrad(duc, e, hp, hn), i)

    du, gw = pl.pallas_call(
        body, name=name, grid=(nc, nr),
        in_specs=(_halo_specs(tr, 2 * G, nr, lambda j: j) + _halo_specs(tr, G, nr, lambda j: j)
                  + [pl.BlockSpec((3, 2 * G), lambda j, i: (0, j)), pl.BlockSpec((1, 2 * G), lambda j, i: (0, j))]),
        out_specs=[pl.BlockSpec((tr, 2 * G), lambda j, i: (i, j)), pl.BlockSpec((8, 2 * G), lambda j, i: (0, j))],
        out_shape=[jax.ShapeDtypeStruct((T, F2), MXU), jax.ShapeDtypeStruct((8, F2), F32)],
        compiler_params=_cparams(("parallel", "arbitrary"), 24 * _nbytes((tr + 16, 2 * G), F32)),
    )(u, u, u, da, da, da, cw, cb)
    return du, gw


def _ssd_conv(zx, cw, cb, col0, ctx_rows, name, tr=256, tn=512):
    T = zx.shape[0]
    C = cw.shape[1]
    tr, tn = min(tr, ctx_rows), min(tn, C)
    assert C % tn == 0 and col0 % tn == 0
    nr, nc, cb0 = T // tr, C // tn, col0 // tn

    def body(zp, zm, zn, w_ref, b_ref, o_ref):
        i = pl.program_id(1)
        hp, hn = _seq_masks(i, tr, ctx_rows, T)
        o_ref[...] = _silu(_conv3(_ext(zp, zm, zn), w_ref[...], hp, hn)[HALO:HALO + tr] + b_ref[...])

    return pl.pallas_call(
        body, name=name, grid=(nc, nr),
        in_specs=_halo_specs(tr, tn, nr, lambda j: j + cb0) + [pl.BlockSpec((3, tn), lambda j, i: (0, j)),
                                                                pl.BlockSpec((1, tn), lambda j, i: (0, j))],
        out_specs=pl.BlockSpec((tr, tn), lambda j, i: (i, j)),
        out_shape=jax.ShapeDtypeStruct((T, C), F32),
        compiler_params=_cparams(("parallel", "parallel"), 12 * _nbytes((tr + 16, tn), F32)),
    )(zx, zx, zx, cw, cb)


def _ssd_conv_bwd(zx, dxbc, dzx, cw, cb, col0, ctx_rows, name, tr=256, tn=512):
    T = zx.shape[0]
    C = cw.shape[1]
    tr, tn = min(tr, ctx_rows), min(tn, C)
    nr, nc, cb0 = T // tr, C // tn, col0 // tn

    def body(zp, zm, zn, dp, dm, dn_, w_ref, b_ref, dzx_in, dz_ref, gw_ref):
        del dzx_in
        i = pl.program_id(1)
        hp, hn = _seq_masks(i, tr, ctx_rows, T)
        e = _ext(zp, zm, zn)
        w = w_ref[...]
        pre = _conv3(e, w, hp, hn) + b_ref[...]
        dpre = _ext(dp, dm, dn_) * _dsilu(pre)
        dz_ref[...] = _conv3_t(dpre, w, hp, hn)[HALO:HALO + tr].astype(dz_ref.dtype)
        _acc_first(gw_ref, _conv_wgrad(dpre, e, hp, hn), i)

    return pl.pallas_call(
        body, name=name, grid=(nc, nr),
        in_specs=(_halo_specs(tr, tn, nr, lambda j: j + cb0) + _halo_specs(tr, tn, nr, lambda j: j)
                  + [pl.BlockSpec((3, tn), lambda j, i: (0, j)), pl.BlockSpec((1, tn), lambda j, i: (0, j)),
                     pl.BlockSpec(memory_space=pl.ANY)]),
        out_specs=[pl.BlockSpec((tr, tn), lambda j, i: (i, j + cb0)), pl.BlockSpec((8, tn), lambda j, i: (0, j))],
        out_shape=[jax.ShapeDtypeStruct(dzx.shape, dzx.dtype), jax.ShapeDtypeStruct((8, C), F32)],
        input_output_aliases={8: 0},
        compiler_params=_cparams(("parallel", "arbitrary"), 24 * _nbytes((tr + 16, tn), F32)),
    )(zx, zx, zx, dxbc, dxbc, dxbc, cw, cb, dzx)


def _put_cols(dst, src, col_blk, name, tr=256):
    T, W = src.shape
    tr = min(tr, T)

    def body(s_ref, d_in, o_ref):
        del d_in
        o_ref[...] = s_ref[...].astype(o_ref.dtype)

    return pl.pallas_call(
        body, name=name, grid=(T // tr,),
        in_specs=[pl.BlockSpec((tr, W), lambda i: (i, 0)), pl.BlockSpec(memory_space=pl.ANY)],
        out_specs=pl.BlockSpec((tr, W), lambda i: (i, col_blk)),
        out_shape=jax.ShapeDtypeStruct(dst.shape, dst.dtype),
        input_output_aliases={1: 0},
        compiler_params=_cparams(("parallel",), 8 * _nbytes((tr, W), F32)),
    )(src, dst)


def _rope_tables(t_lat, ctx_rows, hd):
    half, quarter = hd // 2, hd // 4
    pos = jnp.arange(t_lat)
    row = (pos // GRID_W).astype(F32)
    col = (pos % GRID_W).astype(F32)
    inv_freq = ROPE_THETA ** (-jnp.arange(0, half, 2, dtype=F32) / half)
    ar, ac = row[:, None] * inv_freq[None, :], col[:, None] * inv_freq[None, :]
    cos = jnp.concatenate([jnp.cos(ar), jnp.cos(ar), jnp.cos(ac), jnp.cos(ac)], axis=1)
    sin = jnp.concatenate([-jnp.sin(ar), jnp.sin(ar), -jnp.sin(ac), jnp.sin(ac)], axis=1)
    del quarter
    cos = jnp.concatenate([jnp.ones((ctx_rows, hd), F32), cos], axis=0)
    sin = jnp.concatenate([jnp.zeros((ctx_rows, hd), F32), sin], axis=0)
    return cos, sin


def _partner(y):
    hd = y.shape[1]
    q = hd // 4
    lane = lax.broadcasted_iota(jnp.int32, y.shape, 1)
    return jnp.where((lane % (2 * q)) < q, pltpu.roll(y, hd - q, 1), pltpu.roll(y, q, 1))


def _qk_prep(qkv, qg, kg, cos, sin, n_q, ctx_rows, name, tr=256):
    T = qkv.shape[0]
    hd = qg.shape[1]
    n_kv = ATTN_KV_HEADS
    tr = min(tr, ctx_rows)

    def body(x_ref, qg_ref, kg_ref, c_ref, s_ref, q_ref, k_ref, v_ref):
        cv, sv = c_ref[...], s_ref[...]
        for h in range(n_q + n_kv):
            xh = x_ref[:, h * hd:(h + 1) * hd]
            r = lax.rsqrt(jnp.mean(xh * xh, axis=-1, keepdims=True) + NORM_EPS)
            y = (xh * r) * (qg_ref[...] if h < n_q else kg_ref[...])
            rot = y * cv + _partner(y) * sv
            if h < n_q:
                q_ref[:, h * hd:(h + 1) * hd] = rot.astype(q_ref.dtype)
            else:
                k_ref[:, (h - n_q) * hd:(h - n_q + 1) * hd] = rot.astype(k_ref.dtype)
        v_ref[...] = x_ref[:, (n_q + n_kv) * hd:].astype(v_ref.dtype)

    W = qkv.shape[1]
    return pl.pallas_call(
        body, name=name, grid=(T // tr,),
        in_specs=[pl.BlockSpec((tr, W), lambda i: (i, 0)), pl.BlockSpec((1, hd), lambda i: (0, 0)),
                  pl.BlockSpec((1, hd), lambda i: (0, 0)), pl.BlockSpec((tr, hd), lambda i: (i, 0)),
                  pl.BlockSpec((tr, hd), lambda i: (i, 0))],
        out_specs=[pl.BlockSpec((tr, n_q * hd), lambda i: (i, 0)), pl.BlockSpec((tr, n_kv * hd), lambda i: (i, 0)),
                   pl.BlockSpec((tr, n_kv * hd), lambda i: (i, 0))],
        out_shape=[jax.ShapeDtypeStruct((T, n_q * hd), MXU), jax.ShapeDtypeStruct((T, n_kv * hd), MXU),
                   jax.ShapeDtypeStruct((T, n_kv * hd), MXU)],
        compiler_params=_cparams(("parallel",), 6 * _nbytes((tr, W), F32)),
    )(qkv, qg, kg, cos, sin)


def _qk_prep_bwd(qkv, qg, kg, cos, sin, dq, dk, dv, n_q, ctx_rows, name, tr=256):
    T, W = qkv.shape
    hd = qg.shape[1]
    n_kv = ATTN_KV_HEADS
    tr = min(tr, ctx_rows)

    def body(x_ref, qg_ref, kg_ref, c_ref, s_ref, dq_ref, dk_ref, dv_ref, o_ref, g_ref):
        i = pl.program_id(0)
        cv, sv = c_ref[...], s_ref[...]
        gq = jnp.zeros((1, hd), F32)
        gk = jnp.zeros((1, hd), F32)
        for h in range(n_q + n_kv):
            xh = x_ref[:, h * hd:(h + 1) * hd]
            gain = qg_ref[...] if h < n_q else kg_ref[...]
            drot = (dq_ref[:, h * hd:(h + 1) * hd] if h < n_q else dk_ref[:, (h - n_q) * hd:(h - n_q + 1) * hd]).astype(F32)
            dy = drot * cv + _partner(drot * sv)
            r = lax.rsqrt(jnp.mean(xh * xh, axis=-1, keepdims=True) + NORM_EPS)
            xn = xh * r
            gsum = jnp.sum(dy * xn, axis=0, keepdims=True)
            if h < n_q:
                gq = gq + gsum
            else:
                gk = gk + gsum
            dxn = dy * gain
            o_ref[:, h * hd:(h + 1) * hd] = (r * (dxn - xn * jnp.mean(dxn * xn, axis=-1, keepdims=True))).astype(o_ref.dtype)
        o_ref[:, (n_q + n_kv) * hd:] = dv_ref[...].astype(o_ref.dtype)
        _acc_first(g_ref, jnp.concatenate([gq, gk, jnp.zeros((6, hd), F32)], axis=0), i)

    return pl.pallas_call(
        body, name=name, grid=(T // tr,),
        in_specs=[pl.BlockSpec((tr, W), lambda i: (i, 0)), pl.BlockSpec((1, hd), lambda i: (0, 0)),
                  pl.BlockSpec((1, hd), lambda i: (0, 0)), pl.BlockSpec((tr, hd), lambda i: (i, 0)),
                  pl.BlockSpec((tr, hd), lambda i: (i, 0)), pl.BlockSpec((tr, n_q * hd), lambda i: (i, 0)),
                  pl.BlockSpec((tr, n_kv * hd), lambda i: (i, 0)), pl.BlockSpec((tr, n_kv * hd), lambda i: (i, 0))],
        out_specs=[pl.BlockSpec((tr, W), lambda i: (i, 0)), pl.BlockSpec((8, hd), lambda i: (0, 0))],
        out_shape=[jax.ShapeDtypeStruct((T, W), MXU), jax.ShapeDtypeStruct((8, hd), F32)],
        compiler_params=_cparams(("arbitrary",), 8 * _nbytes((tr, W), F32)),
    )(qkv, qg, kg, cos, sin, dq, dk, dv)


def _attn_scores(q_ref, k_ref, sink_ref, h, qb, ctx_rows, nb, hd, grp):
    scale = hd ** -0.5
    w0 = jnp.clip(qb - 1, 0, nb - 3) * BLK
    w0 = pl.multiple_of(w0, BLK)
    qv = q_ref[...]
    qs = jnp.concatenate([qv[:, g * hd:(g + 1) * hd] for g in range(grp)], axis=0)
    kc = k_ref[0:ctx_rows, :]
    kb = k_ref[pl.ds(w0, 3 * BLK), :]
    s_c = _dot(qs, kc, NT) * scale
    s_b = _dot(qs, kb, NT) * scale
    n = grp * BLK
    qpos = qb * BLK + lax.broadcasted_iota(jnp.int32, (n, 3 * BLK), 0) % BLK
    kpos = w0 + lax.broadcasted_iota(jnp.int32, (n, 3 * BLK), 1)
    ok = (jnp.abs(kpos - qpos) <= ATTN_WINDOW) & (kpos >= ctx_rows) & (qpos >= ctx_rows)
    s_b = jnp.where(ok, s_b, -jnp.inf)
    gi = lax.broadcasted_iota(jnp.int32, (n, 1), 0) // BLK
    sink = jnp.zeros((n, 1), F32)
    for g in range(grp):
        sink = jnp.where(gi == g, sink_ref[h * grp + g], sink)
    m = jnp.maximum(jnp.maximum(jnp.max(s_c, axis=1, keepdims=True), jnp.max(s_b, axis=1, keepdims=True)), sink)
    e_c, e_b, e_s = jnp.exp(s_c - m), jnp.exp(s_b - m), jnp.exp(sink - m)
    inv = 1.0 / (jnp.sum(e_c, axis=1, keepdims=True) + jnp.sum(e_b, axis=1, keepdims=True) + e_s)
    return qs, kc, kb, w0, e_c * inv, e_b * inv, e_s * inv, gi


def _attn_fwd(qr, kr, vb, sinks, n_q, ctx_rows, name):
    T = qr.shape[0]
    n_kv = ATTN_KV_HEADS
    grp = n_q // n_kv
    hd = qr.shape[1] // n_q
    nb = T // BLK

    def body(sink_ref, q_ref, k_ref, v_ref, o_ref):
        h, qb = pl.program_id(0), pl.program_id(1)
        _, _, _, w0, p_c, p_b, _, _ = _attn_scores(q_ref, k_ref, sink_ref, h, qb, ctx_rows, nb, hd, grp)
        o = _dot(p_c.astype(MXU), v_ref[0:ctx_rows, :]) + _dot(p_b.astype(MXU), v_ref[pl.ds(w0, 3 * BLK), :])
        o_ref[...] = jnp.concatenate([o[g * BLK:(g + 1) * BLK] for g in range(grp)], axis=1).astype(o_ref.dtype)

    return pl.pallas_call(
        body, name=name, grid=(n_kv, nb),
        in_specs=[pl.BlockSpec(memory_space=pltpu.SMEM), pl.BlockSpec((BLK, grp * hd), lambda h, i: (i, h)),
                  pl.BlockSpec((T, hd), lambda h, i: (0, h)), pl.BlockSpec((T, hd), lambda h, i: (0, h))],
        out_specs=pl.BlockSpec((BLK, grp * hd), lambda h, i: (i, h)),
        out_shape=jax.ShapeDtypeStruct((T, n_q * hd), MXU),
        compiler_params=_cparams(("parallel", "arbitrary"), 4 * _nbytes((T, hd), MXU) + 24 * _nbytes((grp * BLK, 5 * BLK), F32)),
    )(sinks, qr, kr, vb)


def _attn_bwd(qr, kr, vb, sinks, do, n_q, ctx_rows, name):
    T = qr.shape[0]
    n_kv = ATTN_KV_HEADS
    grp = n_q // n_kv
    hd = qr.shape[1] // n_q
    nb = T // BLK
    scale = hd ** -0.5

    def body(sink_ref, q_ref, k_ref, v_ref, do_ref, dq_ref, dk_ref, dv_ref, ds_ref):
        h, qb = pl.program_id(0), pl.program_id(1)
        qs, kc, kb, w0, p_c, p_b, p_s, gi = _attn_scores(q_ref, k_ref, sink_ref, h, qb, ctx_rows, nb, hd, grp)
        dov = do_ref[...]
        dos = jnp.concatenate([dov[:, g * hd:(g + 1) * hd] for g in range(grp)], axis=0)
        vc = v_ref[0:ctx_rows, :]
        vw = v_ref[pl.ds(w0, 3 * BLK), :]
        dp_c = _dot(dos, vc, NT)
        dp_b = _dot(dos, vw, NT)
        delta = jnp.sum(p_c * dp_c, axis=1, keepdims=True) + jnp.sum(p_b * dp_b, axis=1, keepdims=True)
        ds_c = (p_c * (dp_c - delta) * scale).astype(MXU)
        ds_b = (p_b * (dp_b - delta) * scale).astype(MXU)
        dq = _dot(ds_c, kc) + _dot(ds_b, kb)
        dq_ref[...] = jnp.concatenate([dq[g * BLK:(g + 1) * BLK] for g in range(grp)], axis=1)

        @pl.when(qb == 0)
        def _():
            dk_ref[...] = jnp.zeros(dk_ref.shape, F32)
            dv_ref[...] = jnp.zeros(dv_ref.shape, F32)

        dk_ref[0:ctx_rows, :] += _dot(ds_c, qs, TN)
        dv_ref[0:ctx_rows, :] += _dot(p_c.astype(MXU), dos, TN)
        dk_ref[pl.ds(w0, 3 * BLK), :] += _dot(ds_b, qs, TN)
        dv_ref[pl.ds(w0, 3 * BLK), :] += _dot(p_b.astype(MXU), dos, TN)
        t = -(p_s * delta)
        lane = lax.broadcasted_iota(jnp.int32, (8, LANE), 1)
        part = jnp.zeros((8, LANE), F32)
        for g in range(grp):
            part = jnp.where(lane == g, jnp.sum(jnp.where(gi == g, t, 0.0)), part)
        _acc_first(ds_ref, part, qb)

    return pl.pallas_call(
        body, name=name, grid=(n_kv, nb),
        in_specs=[pl.BlockSpec(memory_space=pltpu.SMEM), pl.BlockSpec((BLK, grp * hd), lambda h, i: (i, h)),
                  pl.BlockSpec((T, hd), lambda h, i: (0, h)), pl.BlockSpec((T, hd), lambda h, i: (0, h)),
                  pl.BlockSpec((BLK, grp * hd), lambda h, i: (i, h))],
        out_specs=[pl.BlockSpec((BLK, grp * hd), lambda h, i: (i, h)), pl.BlockSpec((T, hd), lambda h, i: (0, h)),
                   pl.BlockSpec((T, hd), lambda h, i: (0, h)), pl.BlockSpec((8, LANE), lambda h, i: (h, 0))],
        out_shape=[jax.ShapeDtypeStruct((T, n_q * hd), F32), jax.ShapeDtypeStruct((T, n_kv * hd), F32),
                   jax.ShapeDtypeStruct((T, n_kv * hd), F32), jax.ShapeDtypeStruct((n_kv * 8, LANE), F32)],
        compiler_params=_cparams(("parallel", "arbitrary"), 4 * _nbytes((T, hd), MXU) + 4 * _nbytes((T, hd), F32)
                                 + 40 * _nbytes((grp * BLK, 5 * BLK), F32)),
    )(sinks, qr, kr, vb, do)


def _chunk_order(s, n_chunks, n_ctx, rev):
    if not rev:
        return s
    return jnp.where(s < n_ctx, n_ctx - 1 - s, n_chunks - 1 + n_ctx - s)


def _softplus(x):
    return jnp.maximum(x, 0.0) + jnp.log(1.0 + jnp.exp(-jnp.abs(x)))


def _expand_matrix(n_heads, p, rev):
    e = np.zeros((LANE, n_heads * p), np.float32)
    for h in range(n_heads):
        e[h + (n_heads if rev else 0), h * p:(h + 1) * p] = 1.0
    return jnp.asarray(e, BF16)


def _ssd_chunk_prep(dt_ref, dtb_ref, alog_ref, e_ref, rev):
    dt = _softplus(dt_ref[...] + dtb_ref[...])
    a = -jnp.exp(alog_ref[...])
    li = lax.broadcasted_iota(jnp.int32, (BLK, BLK), 0)
    si = lax.broadcasted_iota(jnp.int32, (BLK, BLK), 1)
    tri = (si >= li) if rev else (si <= li)
    acs = _dot3_rhs(tri.astype(BF16), a * dt)
    ev = e_ref[...]
    return dt, a, tri, acs, _dot3_lhs(dt, ev), _dot3_lhs(acs, ev)


def _pair_cols(ap):
    lane = lax.broadcasted_iota(jnp.int32, ap.shape, 1)
    apr = pltpu.roll(ap, LANE // 2, 1)
    return jnp.where(lane < LANE // 2, ap, apr), jnp.where(lane < LANE // 2, apr, ap)


def _ssd_scan(xbc, zx, dtb, alog, emat, n_heads, n_ctx, rev, name):
    T, C = xbc.shape
    P = emat.shape[1] // n_heads
    DI = n_heads * P
    GN = (C - DI) // 2
    N = GN // SSD_GROUPS
    n_pairs = DI // LANE
    ppg = n_pairs // SSD_GROUPS
    n_chunks = T // BLK
    hoff = n_heads if rev else 0
    last = 0 if rev else BLK - 1
    dt_blk = zx.shape[1] // LANE - 1
    assert N == LANE and 2 * P == LANE and 2 * n_heads == LANE

    def body(xs_ref, b_ref, c_ref, dt_ref, dtb_ref, alog_ref, e_ref, y_ref, hin_ref, state_ref, xdt_s, xdec_s, aexp_s, at_s):
        s = pl.program_id(0)

        @pl.when(s == 0)
        def _():
            state_ref[...] = jnp.zeros(state_ref.shape, F32)

        dt, a, tri, acs, dtexp, aexp = _ssd_chunk_prep(dt_ref, dtb_ref, alog_ref, e_ref, rev)
        at_s[...] = acs.T
        aexp_s[...] = aexp
        xdt = xs_ref[...] * dtexp
        xdt_s[...] = xdt.astype(MXU)
        xdec_s[...] = (xdt * jnp.exp(aexp[last:last + 1, :] - aexp)).astype(MXU)
        hin_ref[0] = state_ref[...]
        lane = lax.broadcasted_iota(jnp.int32, (BLK, LANE), 1)

        def pair(k, carry):
            col = pl.multiple_of(k * LANE, LANE)
            gcol = pl.multiple_of((k // ppg) * N, N)
            bg = b_ref[:, pl.ds(gcol, N)].astype(MXU)
            cg = c_ref[:, pl.ds(gcol, N)].astype(MXU)
            cb = _dot(cg, bg, NT)
            ap = aexp_s[:, pl.ds(col, LANE)]
            ac0, ac1 = _pair_cols(ap)
            ar0 = at_s[pl.ds(2 * k + hoff, 1), :]
            ar1 = at_s[pl.ds(2 * k + 1 + hoff, 1), :]
            m0 = (cb * jnp.exp(jnp.where(tri, ac0 - ar0, -jnp.inf))).astype(MXU)
            m1 = (cb * jnp.exp(jnp.where(tri, ac1 - ar1, -jnp.inf))).astype(MXU)
            xp = xdt_s[:, pl.ds(col, LANE)]
            zero = jnp.zeros_like(xp)
            xbd = jnp.concatenate([jnp.where(lane < LANE // 2, xp, zero), jnp.where(lane >= LANE // 2, xp, zero)], axis=0)
            yd = _dot(jnp.concatenate([m0, m1], axis=1), xbd)
            ht = state_ref[k]
            yo = _dot(cg, ht.astype(MXU)) * jnp.exp(ap)
            y_ref[:, pl.ds(col, LANE)] = yd + yo
            st = _dot(bg, xdec_s[:, pl.ds(col, LANE)], TN)
            state_ref[k] = jnp.exp(aexp_s[pl.ds(last, 1), pl.ds(col, LANE)]) * ht + st
            return carry

        lax.fori_loop(0, n_pairs, pair, 0)

    order = lambda s: _chunk_order(s, n_chunks, n_ctx, rev)
    return pl.pallas_call(
        body, name=name, grid=(n_chunks,),
        in_specs=[pl.BlockSpec((BLK, DI), lambda s: (order(s), 0)),
                  pl.BlockSpec((BLK, GN), lambda s: (order(s), DI // GN)),
                  pl.BlockSpec((BLK, GN), lambda s: (order(s), DI // GN + 1)),
                  pl.BlockSpec((BLK, LANE), lambda s: (order(s), dt_blk)),
                  pl.BlockSpec((1, LANE), lambda s: (0, 0)), pl.BlockSpec((1, LANE), lambda s: (0, 0)),
                  pl.BlockSpec((LANE, DI), lambda s: (0, 0))],
        out_specs=[pl.BlockSpec((BLK, DI), lambda s: (order(s), 0)),
                   pl.BlockSpec((1, n_pairs, N, LANE), lambda s: (order(s), 0, 0, 0))],
        out_shape=[jax.ShapeDtypeStruct((T, DI), F32), jax.ShapeDtypeStruct((n_chunks, n_pairs, N, LANE), F32)],
        scratch_shapes=[pltpu.VMEM((n_pairs, N, LANE), F32), pltpu.VMEM((BLK, DI), MXU), pltpu.VMEM((BLK, DI), MXU),
                        pltpu.VMEM((BLK, DI), F32), pltpu.VMEM((LANE, BLK), F32)],
        compiler_params=_cparams(("arbitrary",), 20 * _nbytes((BLK, DI), F32)),
    )(xbc, xbc, xbc, zx, dtb, alog, emat)


def _ssd_scan_bwd(xbc, zx, dtb, alog, emat, emat_t, dexp, hin, dy, acc, n_heads, n_ctx, rev, name):
    T, C = xbc.shape
    P = emat.shape[1] // n_heads
    DI = n_heads * P
    GN = (C - DI) // 2
    N = GN // SSD_GROUPS
    n_pairs = DI // LANE
    ppg = n_pairs // SSD_GROUPS
    n_chunks = T // BLK
    hoff = n_heads if rev else 0
    last = 0 if rev else BLK - 1
    dt_blk = zx.shape[1] // LANE - 1
    has_acc = acc is not None

    def body(*refs):
        (xs_ref, b_ref, c_ref, dt_ref, dtb_ref, alog_ref, e_ref, et_ref, dexp_ref, hin_ref, dy_ref) = refs[:11]
        n_in = 11
        if has_acc:
            dxbc_in, ddt_in, sums_in = refs[11:14]
            n_in = 14
        dxbc_ref, ddt_ref, sums_ref = refs[n_in:n_in + 3]
        dstate_ref, xdt_s, xdec_s, aexp_s, at_s, dyd_s, z1_s, z3_s, dxdt_s, cdrow_s, rmat_s, cst_s = refs[n_in + 3:]
        s = pl.program_id(0)

        @pl.when(s == 0)
        def _():
            dstate_ref[...] = jnp.zeros(dstate_ref.shape, F32)

        dt, a, tri, acs, dtexp, aexp = _ssd_chunk_prep(dt_ref, dtb_ref, alog_ref, e_ref, rev)
        at_s[...] = acs.T
        aexp_s[...] = aexp
        xsv = xs_ref[...]
        xdt = xsv * dtexp
        xdt_s[...] = xdt.astype(MXU)
        decend = jnp.exp(aexp[last:last + 1, :] - aexp)
        xdec_s[...] = (xdt * decend).astype(MXU)
        dyv = dy_ref[...]
        dyd_s[...] = (dyv * jnp.exp(aexp)).astype(MXU)
        dxbc_ref[:, DI:] = jnp.zeros((BLK, 2 * GN), F32)
        rmat_s[...] = jnp.zeros(rmat_s.shape, F32)
        lane = lax.broadcasted_iota(jnp.int32, (BLK, LANE), 1)
        lo = lane < LANE // 2
        tri_t = (lax.broadcasted_iota(jnp.int32, (BLK, BLK), 0) <= lax.broadcasted_iota(jnp.int32, (BLK, BLK), 1)) if not rev \
            else (lax.broadcasted_iota(jnp.int32, (BLK, BLK), 0) >= lax.broadcasted_iota(jnp.int32, (BLK, BLK), 1))

        def pair(k, carry):
            col = pl.multiple_of(k * LANE, LANE)
            gcol = pl.multiple_of((k // ppg) * N, N)
            bg = b_ref[:, pl.ds(gcol, N)].astype(MXU)
            cg = c_ref[:, pl.ds(gcol, N)].astype(MXU)
            cb = _dot(cg, bg, NT)
            cbt = _dot(bg, cg, NT)
            ap = aexp_s[:, pl.ds(col, LANE)]
            ac0, ac1 = _pair_cols(ap)
            ar0 = at_s[pl.ds(2 * k + hoff, 1), :]
            ar1 = at_s[pl.ds(2 * k + 1 + hoff, 1), :]
            seg0 = jnp.exp(jnp.where(tri, ac0 - ar0, -jnp.inf))
            seg1 = jnp.exp(jnp.where(tri, ac1 - ar1, -jnp.inf))
            segt0 = jnp.exp(jnp.where(tri_t, ar0 - ac0, -jnp.inf))
            segt1 = jnp.exp(jnp.where(tri_t, ar1 - ac1, -jnp.inf))
            dyp = dy_ref[:, pl.ds(col, LANE)].astype(MXU)
            zero = jnp.zeros_like(dyp)
            dy0, dy1 = jnp.where(lo, dyp, zero), jnp.where(lo, zero, dyp)
            dht = dstate_ref[k]
            dhb = dht.astype(MXU)
            ht = hin_ref[0, k]
            mt = jnp.concatenate([(cbt * segt0).astype(MXU), (cbt * segt1).astype(MXU)], axis=1)
            bdh = _dot(bg, dhb)
            dec_p = jnp.exp(aexp_s[pl.ds(last, 1), pl.ds(col, LANE)] - ap)
            dxdt_s[:, pl.ds(col, LANE)] = _dot(mt, jnp.concatenate([dy0, dy1], axis=0)) + dec_p * bdh
            z3_s[:, pl.ds(col, LANE)] = bdh
            cdec = jnp.exp(aexp_s[pl.ds(last, 1), pl.ds(col, LANE)])
            dydp = dyd_s[:, pl.ds(col, LANE)]
            dstate_ref[k] = _dot(cg, dydp, TN) + cdec * dht
            cdrow_s[0:1, pl.ds(col, LANE)] = cdec * jnp.sum(dht * ht, axis=0, keepdims=True)
            z1_s[:, pl.ds(col, LANE)] = _dot(cg, ht.astype(MXU))
            dcg = _dot(dydp, ht.astype(MXU), NT)
            dbg = _dot(xdec_s[:, pl.ds(col, LANE)], dhb, NT)
            xp = xdt_s[:, pl.ds(col, LANE)]
            dg0 = _dot(dy0, xp, NT)
            dg1 = _dot(dy1, xp, NT)
            ds0, ds1 = dg0 * seg0, dg1 * seg1
            w0, w1 = ds0 * cb, ds1 * cb
            dcb = ds0 + ds1
            lane_h = lax.broadcasted_iota(jnp.int32, (BLK, LANE), 1)
            rmat_s[...] += (jnp.where(lane_h == 2 * k + hoff, jnp.sum(w0, axis=1, keepdims=True), 0.0)
                            + jnp.where(lane_h == 2 * k + 1 + hoff, jnp.sum(w1, axis=1, keepdims=True), 0.0))
            cst_s[pl.ds(2 * k + hoff, 1), :] = jnp.sum(w0, axis=0, keepdims=True)
            cst_s[pl.ds(2 * k + 1 + hoff, 1), :] = jnp.sum(w1, axis=0, keepdims=True)
            dcbb = dcb.astype(MXU)
            dxbc_ref[:, pl.ds(DI + GN + gcol, N)] += dcg + _dot(dcbb, bg)
            dxbc_ref[:, pl.ds(DI + gcol, N)] += dbg + _dot(dcbb, cg, TN)
            return carry

        cst_s[...] = jnp.zeros(cst_s.shape, F32)
        lax.fori_loop(0, n_pairs, pair, 0)

        etv = et_ref[...]
        dxdt = dxdt_s[...]
        z1 = _dot3_lhs(dyv * (z1_s[...] * jnp.exp(aexp)), etv)
        z2 = _dot3_lhs(dxdt * xsv, etv)
        q = _dot3_lhs(xdt * decend * z3_s[...], etv)
        cd8 = jnp.concatenate([cdrow_s[0:1, :], jnp.zeros((7, DI), F32)], axis=0)
        cdh = _dot3_lhs(cd8, etv)[0:1, :]
        dacs = rmat_s[...] - cst_s[...].T + z1 - q
        rowi = lax.broadcasted_iota(jnp.int32, (BLK, LANE), 0)
        dacs = dacs + jnp.where(rowi == last, jnp.sum(q, axis=0, keepdims=True) + cdh, 0.0)
        d_a = _dot3_rhs(tri_t.astype(BF16), dacs)
        ddt = a * d_a + z2
        x_raw = dt_ref[...] + dtb_ref[...]
        ddt_raw = ddt * jax.nn.sigmoid(x_raw)
        lane_l = lax.broadcasted_iota(jnp.int32, (BLK, LANE), 1)
        mine = (lane_l >= hoff) & (lane_l < hoff + n_heads)
        ddt_raw = jnp.where(mine, ddt_raw, 0.0)
        part = jnp.concatenate([jnp.sum(jnp.where(mine, dt * d_a, 0.0), axis=0, keepdims=True) * a,
                                jnp.sum(ddt_raw, axis=0, keepdims=True), jnp.zeros((6, LANE), F32)], axis=0)
        dxs = dxdt * dtexp
        if has_acc:
            dxbc_ref[:, 0:DI] = dxs + dxbc_in[:, 0:DI]
            dxbc_ref[:, DI:] += dxbc_in[:, DI:]
            ddt_ref[...] = ddt_raw + ddt_in[...]
            part = part + jnp.where(s == 0, sums_in[...], 0.0)
        else:
            dxbc_ref[:, 0:DI] = dxs + dyv * dexp_ref[...]
            ddt_ref[...] = ddt_raw
        _acc_first(sums_ref, part, s)

    order = lambda s: _chunk_order(n_chunks - 1 - s, n_chunks, n_ctx, rev)
    in_specs = [pl.BlockSpec((BLK, DI), lambda s: (order(s), 0)),
                pl.BlockSpec((BLK, GN), lambda s: (order(s), DI // GN)),
                pl.BlockSpec((BLK, GN), lambda s: (order(s), DI // GN + 1)),
                pl.BlockSpec((BLK, LANE), lambda s: (order(s), dt_blk)),
                pl.BlockSpec((1, LANE), lambda s: (0, 0)), pl.BlockSpec((1, LANE), lambda s: (0, 0)),
                pl.BlockSpec((LANE, DI), lambda s: (0, 0)), pl.BlockSpec((DI, LANE), lambda s: (0, 0)),
                pl.BlockSpec((1, DI), lambda s: (0, 0)),
                pl.BlockSpec((1, n_pairs, N, LANE), lambda s: (order(s), 0, 0, 0)),
                pl.BlockSpec((BLK, DI), lambda s: (order(s), 0))]
    args = [xbc, xbc, xbc, zx, dtb, alog, emat, emat_t, dexp, hin, dy]
    if has_acc:
        in_specs += [pl.BlockSpec((BLK, C), lambda s: (order(s), 0)), pl.BlockSpec((BLK, LANE), lambda s: (order(s), 0)),
                     pl.BlockSpec((8, LANE), lambda s: (0, 0))]
        args += list(acc)
    return pl.pallas_call(
        body, name=name, grid=(n_chunks,),
        in_specs=in_specs,
        out_specs=[pl.BlockSpec((BLK, C), lambda s: (order(s), 0)), pl.BlockSpec((BLK, LANE), lambda s: (order(s), 0)),
                   pl.BlockSpec((8, LANE), lambda s: (0, 0))],
        out_shape=[jax.ShapeDtypeStruct((T, C), F32), jax.ShapeDtypeStruct((T, LANE), F32), jax.ShapeDtypeStruct((8, LANE), F32)],
        scratch_shapes=[pltpu.VMEM((n_pairs, N, LANE), F32), pltpu.VMEM((BLK, DI), MXU), pltpu.VMEM((BLK, DI), MXU),
                        pltpu.VMEM((BLK, DI), F32), pltpu.VMEM((LANE, BLK), F32), pltpu.VMEM((BLK, DI), MXU),
                        pltpu.VMEM((BLK, DI), F32), pltpu.VMEM((BLK, DI), F32), pltpu.VMEM((BLK, DI), F32),
                        pltpu.VMEM((8, DI), F32), pltpu.VMEM((BLK, LANE), F32), pltpu.VMEM((LANE, BLK), F32)],
        compiler_params=_cparams(("arbitrary",), 36 * _nbytes((BLK, DI), F32)),
    )(*args)


def _ssd_finish(yf, yb, xbc, zx, dexp, nw, name, tr=256):
    T, DI = yf.shape
    tr = min(tr, T)

    def body(yf_ref, yb_ref, xs_ref, z_ref, d_ref, nw_ref, o_ref):
        y = yf_ref[...] + yb_ref[...] + xs_ref[...] * d_ref[...]
        gt = y * _silu(z_ref[...])
        r = lax.rsqrt(jnp.mean(gt * gt, axis=-1, keepdims=True) + NORM_EPS)
        o_ref[...] = ((gt * r) * nw_ref[...]).astype(o_ref.dtype)

    row = pl.BlockSpec((tr, DI), lambda i: (i, 0))
    vec = pl.BlockSpec((1, DI), lambda i: (0, 0))
    return pl.pallas_call(
        body, name=name, grid=(T // tr,), in_specs=[row, row, row, row, vec, vec], out_specs=row,
        out_shape=jax.ShapeDtypeStruct((T, DI), MXU),
        compiler_params=_cparams(("parallel",), 16 * _nbytes((tr, DI), F32)),
    )(yf, yb, xbc, zx, dexp, nw)


def _ssd_finish_bwd(yf, yb, xbc, zx, dexp, nw, do, emat_t, dzx_shape, name, tr=64):
    T, DI = yf.shape
    tr = min(tr, T)
    n_steps = T // tr

    def body(yf_ref, yb_ref, xs_ref, z_ref, d_ref, nw_ref, do_ref, et_ref, dy_ref, dz_ref, sums_ref, dd_ref):
        i = pl.program_id(0)
        xs = xs_ref[...]
        zv = z_ref[...]
        y = yf_ref[...] + yb_ref[...] + xs * d_ref[...]
        sz = _silu(zv)
        gt = y * sz
        r = lax.rsqrt(jnp.mean(gt * gt, axis=-1, keepdims=True) + NORM_EPS)
        gn = gt * r
        dov = do_ref[...].astype(F32)
        dgn = dov * nw_ref[...]
        dgt = r * (dgn - gn * jnp.mean(dgn * gn, axis=-1, keepdims=True))
        dy = dgt * sz
        dy_ref[...] = dy
        dz_ref[...] = (dgt * y * _dsilu(zv)).astype(dz_ref.dtype)
        part = jnp.concatenate([jnp.sum(dov * gn, axis=0, keepdims=True), jnp.sum(dy * xs, axis=0, keepdims=True),
                                jnp.zeros((6, DI), F32)], axis=0)
        _acc_first(sums_ref, part, i)

        @pl.when(i == n_steps - 1)
        def _():
            dd_ref[...] = _dot3_lhs(sums_ref[...], et_ref[...])

    row = pl.BlockSpec((tr, DI), lambda i: (i, 0))
    vec = pl.BlockSpec((1, DI), lambda i: (0, 0))
    return pl.pallas_call(
        body, name=name, grid=(n_steps,),
        in_specs=[row, row, row, row, vec, vec, row, pl.BlockSpec((DI, LANE), lambda i: (0, 0))],
        out_specs=[row, row, pl.BlockSpec((8, DI), lambda i: (0, 0)), pl.BlockSpec((8, LANE), lambda i: (0, 0))],
        out_shape=[jax.ShapeDtypeStruct((T, DI), F32), jax.ShapeDtypeStruct(dzx_shape, MXU), jax.ShapeDtypeStruct((8, DI), F32),
                   jax.ShapeDtypeStruct((8, LANE), F32)],
        compiler_params=_cparams(("arbitrary",), 40 * _nbytes((tr, DI), F32)),
    )(yf, yb, xbc, zx, dexp, nw, do, emat_t)


def _loss_head(xf, tgt, ctx_rows, name, tr=256):
    T, D = xf.shape
    tr = min(tr, ctx_rows)
    n_ctx = ctx_rows // tr

    def body(x_ref, t_ref, dx_ref, l_ref):
        i = pl.program_id(0)

        @pl.when(i < n_ctx)
        def _():
            dx_ref[...] = jnp.zeros(dx_ref.shape, F32)

        @pl.when(i == 0)
        def _():
            l_ref[...] = jnp.zeros(l_ref.shape, F32)

        @pl.when(i >= n_ctx)
        def _():
            e = x_ref[...] - t_ref[...]
            dx_ref[...] = e * (1.0 / D)
            l_ref[...] += 0.5 * jnp.sum(jnp.mean(e * e, axis=-1, keepdims=True))

    return pl.pallas_call(
        body, name=name, grid=(T // tr,),
        in_specs=[pl.BlockSpec((tr, D), lambda i: (i, 0)), pl.BlockSpec((tr, D), lambda i: (jnp.maximum(i - n_ctx, 0), 0))],
        out_specs=[pl.BlockSpec((tr, D), lambda i: (i, 0)), pl.BlockSpec((8, LANE), lambda i: (0, 0))],
        out_shape=[jax.ShapeDtypeStruct((T, D), F32), jax.ShapeDtypeStruct((8, LANE), F32)],
        compiler_params=_cparams(("arbitrary",), 10 * _nbytes((tr, D), F32)),
    )(xf, tgt)


def _adamw_math(w, g, m, v):
    m2 = ADAM_B1 * m + (1.0 - ADAM_B1) * g
    v2 = ADAM_B2 * v + (1.0 - ADAM_B2) * (g * g)
    m_hat = m2 / (1.0 - ADAM_B1 ** ADAM_STEP)
    v_hat = v2 / (1.0 - ADAM_B2 ** ADAM_STEP)
    delta = -ADAM_LR * (m_hat / (jnp.sqrt(v_hat) + ADAM_EPS) + ADAM_WD * w)
    return delta, m2, v2


def _row_tile(rows, target):
    if rows <= target:
        return rows
    t = target - target % 8
    while rows % t:
        t -= 8
    return t


def _adamw(parts, w, m, v, name, tr=128):
    n, L, rows, cols = parts.shape
    tr = _row_tile(rows, tr)

    def body(p_ref, w_ref, m_ref, v_ref, g_ref, d_ref, m2_ref, v2_ref):
        g = p_ref[0, 0]
        for q in range(1, n):
            g = g + p_ref[q, 0]
        d, m2, v2 = _adamw_math(w_ref[0], g, m_ref[0], v_ref[0])
        g_ref[0], d_ref[0], m2_ref[0], v2_ref[0] = g, d, m2, v2

    blk = pl.BlockSpec((1, tr, cols), lambda l, i: (l, i, 0))
    shp = jax.ShapeDtypeStruct((L, rows, cols), F32)
    return pl.pallas_call(
        body, name=name, grid=(L, rows // tr),
        in_specs=[pl.BlockSpec((n, 1, tr, cols), lambda l, i: (0, l, i, 0)), blk, blk, blk],
        out_specs=[blk, blk, blk, blk], out_shape=[shp, shp, shp, shp],
        compiler_params=_cparams(("parallel", "parallel"), 2 * (n + 8) * _nbytes((tr, cols), F32)),
    )(parts, w, m, v)


def _adamw_small(bufs, w, m, v, name):
    n, R, _ = bufs.shape

    def body(b_ref, w_ref, m_ref, v_ref, g_ref, d_ref, m2_ref, v2_ref):
        g = b_ref[0]
        for q in range(1, n):
            g = g + b_ref[q]
        d, m2, v2 = _adamw_math(w_ref[...], g, m_ref[...], v_ref[...])
        g_ref[...], d_ref[...], m2_ref[...], v2_ref[...] = g, d, m2, v2

    vm = pl.BlockSpec(memory_space=pltpu.VMEM)
    shp = jax.ShapeDtypeStruct((R, LANE), F32)
    return pl.pallas_call(body, name=name, in_specs=[vm, vm, vm, vm], out_specs=[vm, vm, vm, vm],
                          out_shape=[shp, shp, shp, shp],
                          compiler_params=pltpu.CompilerParams(vmem_limit_bytes=32 * 1024 * 1024))(bufs, w, m, v)


def _add_blocks(g, r1, c_idx, name, tr=128):
    _, rows, cols = g.shape
    tr = _row_tile(rows, tr)

    def body(c_ref, g_ref, r_ref, p_ref):
        del c_ref
        p_ref[...] = g_ref[...] + r_ref[...]

    return pl.pallas_call(
        body, name=name,
        grid_spec=pltpu.PrefetchScalarGridSpec(
            num_scalar_prefetch=1, grid=(NDEV // 2, rows // tr),
            in_specs=[pl.BlockSpec((1, tr, cols), lambda q, i, c: (2 * q + c[0], i, 0)),
                      pl.BlockSpec((1, tr, cols), lambda q, i, c: (q, i, 0))],
            out_specs=pl.BlockSpec((1, tr, cols), lambda q, i, c: (q, i, 0))),
        out_shape=jax.ShapeDtypeStruct((NDEV // 2, rows, cols), F32),
        compiler_params=_cparams(("parallel", "parallel"), 8 * _nbytes((tr, cols), F32)),
    )(c_idx, g, r1)


def _me():
    return lax.axis_index("x"), lax.axis_index("y"), lax.axis_index("c")


def _flip(v, bit):
    return 1 - v if bit else v


def _peer(k):
    x, y, c = _me()
    return _flip(x, (k >> 2) & 1), _flip(y, (k >> 1) & 1), _flip(c, k & 1)


def _dev_index(p):
    return 4 * p[0] + 2 * p[1] + p[2]


def _chip_index(p):
    return 2 * p[0] + p[1]


def _small_allgather(v, name):
    R = v.shape[0]

    def body(v_ref, out_ref, send_sems, recv_sems, loc_sem):
        me = _dev_index(_me())
        mine = pltpu.make_async_copy(v_ref, out_ref.at[me], loc_sem)
        mine.start()
        sends = []
        for k in range(1, NDEV):
            cp = pltpu.make_async_remote_copy(src_ref=v_ref, dst_ref=out_ref.at[me], send_sem=send_sems.at[k - 1],
                                              recv_sem=recv_sems.at[k - 1], device_id=_peer(k), device_id_type=MESH_ID)
            cp.start()
            sends.append(cp)
        for k in range(1, NDEV):
            pltpu.make_async_remote_copy(src_ref=v_ref, dst_ref=out_ref.at[_dev_index(_peer(k))], send_sem=send_sems.at[k - 1],
                                         recv_sem=recv_sems.at[k - 1], device_id=_peer(k), device_id_type=MESH_ID).wait_recv()
        for cp in sends:
            cp.wait_send()
        mine.wait()

    vm = pl.BlockSpec(memory_space=pltpu.VMEM)
    return pl.pallas_call(
        body, name=name, in_specs=[vm], out_specs=vm, out_shape=jax.ShapeDtypeStruct((NDEV, R, LANE), F32),
        scratch_shapes=[pltpu.SemaphoreType.DMA((NDEV - 1,)), pltpu.SemaphoreType.DMA((NDEV - 1,)), pltpu.SemaphoreType.DMA(())],
        compiler_params=pltpu.CompilerParams(vmem_limit_bytes=48 * 1024 * 1024),
    )(v)


def _allgather_weights(arrs, name):
    n = len(arrs)

    def body(*refs):
        x_refs, out_refs = refs[:n], refs[n:2 * n]
        send_sems, recv_sems, loc_sems = refs[2 * n:]
        x, y, c = _me()
        me, sib = (x, y, c), (x, y, 1 - c)
        chips = [(1 - x, y), (x, 1 - y), (1 - x, 1 - y)]

        def copy(a, k, block, to, src=None):
            dst = out_refs[a].at[_dev_index(block)]
            return pltpu.make_async_remote_copy(src_ref=dst if src is None else src, dst_ref=dst, send_sem=send_sems.at[a, k],
                                                recv_sem=recv_sems.at[a, k], device_id=to, device_id_type=MESH_ID)

        locs, first, passed = [], [], []
        for a in range(n):
            lc = pltpu.make_async_copy(x_refs[a], out_refs[a].at[_dev_index(me)], loc_sems.at[a])
            lc.start()
            locs.append(lc)
            cps = [copy(a, 0, me, sib, src=x_refs[a])] + [copy(a, 1 + j, me, (*chip, c), src=x_refs[a]) for j, chip in enumerate(chips)]
            for cp in cps:
                cp.start()
            first += cps
        for j, chip in enumerate(chips):
            for a in range(n):
                copy(a, 1 + j, (*chip, c), me).wait_recv()
                cp = copy(a, 4 + j, (*chip, c), sib)
                cp.start()
                passed.append(cp)
        for a in range(n):
            copy(a, 0, sib, me).wait_recv()
            for j, chip in enumerate(chips):
                copy(a, 4 + j, (*chip, 1 - c), me).wait_recv()
        for cp in first + passed:
            cp.wait_send()
        for lc in locs:
            lc.wait()

    hbm = pl.BlockSpec(memory_space=pl.ANY)
    return pl.pallas_call(
        body, name=name, in_specs=[hbm] * n, out_specs=[hbm] * n,
        out_shape=[jax.ShapeDtypeStruct((NDEV,) + a.shape, a.dtype) for a in arrs],
        scratch_shapes=[pltpu.SemaphoreType.DMA((n, 7)), pltpu.SemaphoreType.DMA((n, 7)), pltpu.SemaphoreType.DMA((n,))],
    )(*arrs)


def _rs_sibling(gs, name):
    n = len(gs)

    def body(*refs):
        g_refs, r_refs = refs[:n], refs[n:2 * n]
        send_sems, recv_sems = refs[2 * n:]
        x, y, c = _me()
        sib = (x, y, 1 - c)
        sends = []
        for a in range(n):
            for q in range(NDEV // 2):
                cp = pltpu.make_async_remote_copy(src_ref=g_refs[a].at[2 * q + (1 - c)], dst_ref=r_refs[a].at[q],
                                                  send_sem=send_sems.at[a, q], recv_sem=recv_sems.at[a, q],
                                                  device_id=sib, device_id_type=MESH_ID)
                cp.start()
                sends.append(cp)
        for cp in sends:
            cp.wait_recv()
        for cp in sends:
            cp.wait_send()

    hbm = pl.BlockSpec(memory_space=pl.ANY)
    return pl.pallas_call(
        body, name=name, in_specs=[hbm] * n, out_specs=[hbm] * n,
        out_shape=[jax.ShapeDtypeStruct((NDEV // 2,) + g.shape[1:], g.dtype) for g in gs],
        scratch_shapes=[pltpu.SemaphoreType.DMA((n, NDEV // 2)), pltpu.SemaphoreType.DMA((n, NDEV // 2))],
    )(*gs)


def _rs_chips(ps, groups, name):
    n = len(ps)
    where = {}
    for gi, grp in enumerate(groups):
        for li, a in enumerate(grp):
            where[a] = (gi, li)
    ng = len(groups)

    def body(*refs):
        p_refs, r_refs = refs[:n], refs[n:n + ng]
        send_sems, recv_sems, loc_sems = refs[n + ng:]
        x, y, c = _me()
        mychip = _chip_index((x, y))
        chips = [(1 - x, y), (x, 1 - y), (1 - x, 1 - y)]
        sends, locs = [], []
        for a in range(n):
            gi, li = where[a]
            lc = pltpu.make_async_copy(p_refs[a].at[mychip], r_refs[gi].at[mychip, li], loc_sems.at[a])
            lc.start()
            locs.append(lc)
            for j, chip in enumerate(chips):
                cp = pltpu.make_async_remote_copy(src_ref=p_refs[a].at[_chip_index(chip)], dst_ref=r_refs[gi].at[mychip, li],
                                                  send_sem=send_sems.at[a, j], recv_sem=recv_sems.at[a, j],
                                                  device_id=(*chip, c), device_id_type=MESH_ID)
                cp.start()
                sends.append(cp)
        for a in range(n):
            gi, li = where[a]
            for j, chip in enumerate(chips):
                pltpu.make_async_remote_copy(src_ref=p_refs[a].at[mychip], dst_ref=r_refs[gi].at[_chip_index(chip), li],
                                             send_sem=send_sems.at[a, j], recv_sem=recv_sems.at[a, j],
                                             device_id=(*chip, c), device_id_type=MESH_ID).wait_recv()
        for cp in sends:
            cp.wait_send()
        for lc in locs:
            lc.wait()

    hbm = pl.BlockSpec(memory_space=pl.ANY)
    return pl.pallas_call(
        body, name=name, in_specs=[hbm] * n, out_specs=[hbm] * ng,
        out_shape=[jax.ShapeDtypeStruct((NDEV // 2, len(grp)) + ps[grp[0]].shape[1:], F32) for grp in groups],
        scratch_shapes=[pltpu.SemaphoreType.DMA((n, 3)), pltpu.SemaphoreType.DMA((n, 3)), pltpu.SemaphoreType.DMA((n,))],
    )(*ps)


HI = lax.Precision.HIGHEST
MOD_ROWS = 16


def _col_tile(n, target=512):
    return target if n % target == 0 else n


def _modulation(s_in, ada_w, b_loc, name):
    L, D, nl = ada_w.shape
    tn = _col_tile(nl)

    def body(s_ref, w_ref, b_ref, o_ref):
        o_ref[0] = jnp.dot(_silu(s_ref[...]), w_ref[0], preferred_element_type=F32, precision=HI) + b_ref[0]

    return pl.pallas_call(
        body, name=name, grid=(L, nl // tn),
        in_specs=[pl.BlockSpec((MOD_ROWS, D), lambda l, j: (0, 0)), pl.BlockSpec((1, D, tn), lambda l, j: (l, 0, j)),
                  pl.BlockSpec((1, 1, tn), lambda l, j: (l, 0, j))],
        out_specs=pl.BlockSpec((1, MOD_ROWS, tn), lambda l, j: (l, 0, j)),
        out_shape=jax.ShapeDtypeStruct((L, MOD_ROWS, nl), F32),
        compiler_params=_cparams(("parallel", "parallel"), 4 * _nbytes((D, tn), F32)),
    )(s_in, ada_w, b_loc)


def _modulation_bwd(s_in, dml, dmc, ada_w, name):
    L, D, nl = ada_w.shape
    tn = _col_tile(nl)

    def body(s_ref, dml_ref, dmc_ref, w_ref, g_ref, pc_ref):
        l, j = pl.program_id(0), pl.program_id(1)
        a = _silu(s_ref[...])
        tot = dmc_ref[0, 0]
        for d in range(1, NDEV):
            tot = tot + dmc_ref[d, 0]
        row = lax.broadcasted_iota(jnp.int32, (MOD_ROWS, tn), 0)
        dm = jnp.where(row == NDEV, tot, dml_ref[:, 0, 0, :])
        g_ref[0] = lax.dot_general(a, dm, TN, preferred_element_type=F32, precision=HI)
        tot8 = jnp.where(lax.broadcasted_iota(jnp.int32, (8, tn), 0) == 0, tot, 0.0)
        part = lax.dot_general(tot8, w_ref[0], NT, preferred_element_type=F32, precision=HI)

        @pl.when((l == 0) & (j == 0))
        def _():
            pc_ref[...] = part

        @pl.when((l != 0) | (j != 0))
        def _():
            pc_ref[...] += part

    return pl.pallas_call(
        body, name=name, grid=(L, nl // tn),
        in_specs=[pl.BlockSpec((MOD_ROWS, D), lambda l, j: (0, 0)), pl.BlockSpec((MOD_ROWS, 1, 1, tn), lambda l, j: (0, l, 0, j)),
                  pl.BlockSpec((NDEV, 1, 1, tn), lambda l, j: (0, l, 0, j)), pl.BlockSpec((1, D, tn), lambda l, j: (l, 0, j))],
        out_specs=[pl.BlockSpec((1, D, tn), lambda l, j: (l, 0, j)), pl.BlockSpec((8, D), lambda l, j: (0, 0))],
        out_shape=[jax.ShapeDtypeStruct((L, D, nl), F32), jax.ShapeDtypeStruct((8, D), F32)],
        compiler_params=_cparams(("arbitrary", "arbitrary"), 8 * _nbytes((D, tn), F32)),
    )(s_in, dml.reshape(MOD_ROWS, L, 1, nl), dmc.reshape(NDEV, L, 1, nl), ada_w)


def _cctx_update(bufs, c_ctx, m, v, name):
    n, R, _ = bufs.shape

    def body(b_ref, w_ref, m_ref, v_ref, g_ref, d_ref, m2_ref, v2_ref):
        g = b_ref[0]
        for q in range(1, n):
            g = g + b_ref[q]
        g = g * _dsilu(w_ref[...])
        d, m2, v2 = _adamw_math(w_ref[...], g, m_ref[...], v_ref[...])
        g_ref[...], d_ref[...], m2_ref[...], v2_ref[...] = g, d, m2, v2

    vm = pl.BlockSpec(memory_space=pltpu.VMEM)
    shp = jax.ShapeDtypeStruct((R, LANE), F32)
    return pl.pallas_call(body, name=name, in_specs=[vm, vm, vm, vm], out_specs=[vm, vm, vm, vm],
                          out_shape=[shp, shp, shp, shp])(bufs, c_ctx, m, v)


def _pack(arrs):
    flat = jnp.concatenate([a.reshape(-1).astype(F32) for a in arrs])
    n = flat.shape[0]
    total = -(-n // (8 * LANE)) * (8 * LANE)
    return jnp.pad(flat, (0, total - n)).reshape(total // LANE, LANE)


def _unpack(buf, shapes):
    lead = buf.shape[:-2]
    flat = buf.reshape(lead + (-1,))
    out, off = [], 0
    for s in shapes:
        n = int(np.prod(s))
        out.append(flat[..., off:off + n].reshape(lead + tuple(s)))
        off += n
    return out


WEIGHTS = ['c_ctx', 'ada_w', 'ada_b', 'norm1_w', 'norm2_w', 'ssd_w_in', 'ssd_conv_w', 'ssd_conv_b', 'ssd_dt_bias_f',
           'ssd_dt_bias_b', 'ssd_a_log_f', 'ssd_a_log_b', 'ssd_d', 'ssd_norm_w', 'ssd_w_out', 'attn_w_qkv', 'attn_q_gain',
           'attn_k_gain', 'attn_sinks', 'attn_w_o', 'ffn_w_up', 'ffn_conv_w', 'ffn_conv_b', 'ffn_w_down']
SMALL = ['ada_b', 'norm1_w', 'norm2_w', 'ssd_conv_b', 'ssd_dt_bias_f', 'ssd_dt_bias_b', 'ssd_a_log_f', 'ssd_a_log_b', 'ssd_d',
         'ssd_norm_w', 'attn_q_gain', 'attn_k_gain', 'attn_sinks', 'ffn_conv_b']
BIG = ['ssd_w_in', 'ssd_w_out', 'attn_w_qkv', 'attn_w_o', 'ffn_w_up', 'ffn_w_down']


def _step(x, c, ctx, w, tgt, m, v):
    xi, yi, ci = _me()
    me = 4 * xi + 2 * yi + ci
    t_lat, D = x.shape[1], x.shape[2]
    ctx_rows = ctx.shape[1]
    T = ctx_rows + t_lat
    L, n_ssd, n_att = w['norm1_w'].shape[0], w['ssd_d'].shape[0], w['attn_sinks'].shape[0]
    H, DI, XBC = w['ssd_d'].shape[1], w['ssd_norm_w'].shape[1], w['ssd_conv_b'].shape[1]
    P = DI // H
    IN = w['ssd_w_in'].shape[2] * NDEV
    hd, n_q = w['attn_q_gain'].shape[1], w['attn_sinks'].shape[1]
    F2 = w['ffn_conv_b'].shape[1]
    G = F2 // NDEV
    nl = w['ada_w'].shape[2]
    ncc = ctx_rows // BLK
    perm = [_ffn_perm(s) for s in range(NDEV)]
    inv = [perm.index(d) for d in range(NDEV)]

    def reorder(t, order):
        return jnp.concatenate([t[..., o * G:(o + 1) * G] for o in order], axis=-1)

    def interleave(t):
        return reorder(t, perm)

    def deinterleave(t):
        return reorder(t, inv)

    shapes_a = [(D,), w['ssd_conv_w'].shape, w['ffn_conv_w'].shape]
    g_a = _small_allgather(_pack([c[0], w['ssd_conv_w'], w['ffn_conv_w']]), "gather_cond")
    c_all, scw_all, fcw_all = _unpack(g_a, shapes_a)
    ssd_cw = scw_all.transpose(1, 2, 0, 3).reshape(n_ssd, 3, XBC)
    ffn_cw = jnp.concatenate([fcw_all[d] for d in perm], axis=-1)
    ffn_cb = interleave(w['ffn_conv_b'])[:, None, :]

    s_in = jnp.concatenate([c_all, w['c_ctx'][None], jnp.zeros((MOD_ROWS - NDEV - 1, D), F32)], axis=0)
    b_loc = lax.dynamic_slice(w['ada_b'], (0, me * nl), (L, nl))[:, None, :]
    mod_loc = _modulation(s_in, w['ada_w'], b_loc, "modulation")
    g_b = _small_allgather(_pack([mod_loc]), "gather_mod")
    (mod_all,) = _unpack(g_b, [mod_loc.shape])
    mod_lat = lax.dynamic_index_in_dim(mod_all, me, axis=2, keepdims=False)
    mod_ctx = mod_all[:, :, NDEV, :]
    to_mod = lambda t: t.transpose(1, 0, 2).reshape(L, 6, D)
    mod = jnp.stack([to_mod(mod_ctx), to_mod(mod_lat)], axis=1)

    shards, slot = [], {}
    for name in BIG:
        for j in range(w[name].shape[0]):
            slot[(name, j)] = len(shards)
            shards.append(w[name][j].astype(MXU))
    gath = _allgather_weights(shards, "gather_weights")
    gw = lambda name, j: gath[slot[(name, j)]]
    w_in = [gw('ssd_w_in', j).transpose(1, 0, 2).reshape(D, IN) for j in range(n_ssd)]
    w_out = [gw('ssd_w_out', j).reshape(DI, D) for j in range(n_ssd)]
    w_qkv = [gw('attn_w_qkv', j) for j in range(n_att)]
    w_o = [gw('attn_w_o', j).reshape(n_q * hd, D) for j in range(n_att)]
    w_up = [gw('ffn_w_up', i) for i in range(L)]
    w_down = [gw('ffn_w_down', i).reshape(F2 // 2, D) for i in range(L)]

    cos, sin = _rope_tables(t_lat, ctx_rows, hd)
    e_f, e_b = _expand_matrix(H, P, False), _expand_matrix(H, P, True)
    et_f, et_b = e_f.T, e_b.T

    xs = jnp.concatenate([ctx[0], x[0]], axis=0)
    saved = []
    for i in range(L):
        j = i // 2
        s = dict(x0=xs)
        nw1, nw2 = w['norm1_w'][i][None], w['norm2_w'][i][None]
        s['h1'] = _normmod(xs, nw1, mod[i], 0, ctx_rows, "normmod")
        if i % 2 == 0:
            s['zx'] = _mm(s['h1'], w_in[j], tm=768, tn=1152, tk=2048, name="mm_ssd_in")
            s['cw'], s['cb'] = ssd_cw[j], w['ssd_conv_b'][j][None]
            s['xbc'] = _ssd_conv(s['zx'], s['cw'], s['cb'], DI, ctx_rows, "ssd_conv")
            s['dtb'] = jnp.concatenate([w['ssd_dt_bias_f'][j], w['ssd_dt_bias_b'][j]])[None]
            s['alog'] = jnp.concatenate([w['ssd_a_log_f'][j], w['ssd_a_log_b'][j]])[None]
            s['yf'], s['hin_f'] = _ssd_scan(s['xbc'], s['zx'], s['dtb'], s['alog'], e_f, H, ncc, False, "ssd_scan_f")
            s['yb'], s['hin_b'] = _ssd_scan(s['xbc'], s['zx'], s['dtb'], s['alog'], e_b, H, ncc, True, "ssd_scan_b")
            s['dexp'], s['snw'] = jnp.repeat(w['ssd_d'][j], P)[None], w['ssd_norm_w'][j][None]
            s['o'] = _ssd_finish(s['yf'], s['yb'], s['xbc'], s['zx'], s['dexp'], s['snw'], "ssd_finish")
            s['mix'], x1 = _mm(s['o'], w_out[j], tm=768, tn=1024, tk=2048, name="mm_ssd_out",
                               resid=xs, gate=mod[i][:, 2], ctx_rows=ctx_rows)
        else:
            s['qkv'] = _mm(s['h1'], w_qkv[j], tm=768, tn=384, tk=2048, name="mm_qkv", bslots=_ident)
            s['qg'], s['kg'] = w['attn_q_gain'][j][None], w['attn_k_gain'][j][None]
            s['qr'], s['kr'], s['vb'] = _qk_prep(s['qkv'], s['qg'], s['kg'], cos, sin, n_q, ctx_rows, "qk_prep")
            s['o'] = _attn_fwd(s['qr'], s['kr'], s['vb'], w['attn_sinks'][j], n_q, ctx_rows, "attn_fwd")
            s['mix'], x1 = _mm(s['o'], w_o[j], tm=768, tn=1024, tk=2048, name="mm_attn_out",
                               resid=xs, gate=mod[i][:, 2], ctx_rows=ctx_rows)
        s['x1'] = x1
        s['h2'] = _normmod(x1, nw2, mod[i], 1, ctx_rows, "normmod")
        s['u'] = _mm(s['h2'], w_up[i], tm=768, tn=1408, tk=2048, name="mm_ffn_up", bslots=_ffn_perm)
        s['a'] = _ffn_mid(s['u'], ffn_cw[i], ffn_cb[i], ctx_rows, "ffn_mid")
        s['f'], xs = _mm(s['a'], w_down[i], tm=768, tn=1024, tk=1408, name="mm_ffn_down",
                         resid=x1, gate=mod[i][:, 5], ctx_rows=ctx_rows)
        saved.append(s)

    dx, lacc = _loss_head(xs, tgt[0], ctx_rows, "loss_head")
    loss = lax.psum(lacc[0, 0], ("x", "y", "c"))

    gbig = {name: [None] * w[name].shape[0] for name in BIG}
    gs = {name: [None] * w[name].shape[0] for name in SMALL + ['ssd_conv_w', 'ffn_conv_w']}
    dmod = [None] * L
    for i in reversed(range(L)):
        j = i // 2
        s = saved[i]
        nw1, nw2 = w['norm1_w'][i][None], w['norm2_w'][i][None]
        dm2, dg2 = _gate_bwd(dx, s['f'], mod[i], 1, ctx_rows, "gate_bwd")
        da = _mm(dm2, w_down[i], tb=True, out_dtype=MXU, tm=768, tn=1408, tk=2048, name="mm_ffn_down_dx")
        gbig['ffn_w_down'][i] = _mm(s['a'], dm2, ta=True, tm=1408, tn=1024, tk=1056, name="mm_ffn_down_dw").reshape(NDEV, -1, D)
        du, gcw = _ffn_mid_bwd(s['u'], da, ffn_cw[i], ffn_cb[i], ctx_rows, "ffn_mid_bwd")
        gcw = deinterleave(gcw)
        gs['ffn_conv_w'][i], gs['ffn_conv_b'][i] = gcw[0:3], gcw[3]
        dh2 = _mm(du, w_up[i], tb=True, out_dtype=MXU, tm=768, tn=1024, tk=1408, name="mm_ffn_up_dx", bslots=_ffn_perm)
        gbig['ffn_w_up'][i] = _mm(s['h2'], du, ta=True, tm=1024, tn=1408, tk=1056, name="mm_ffn_up_dw", oslots=_ffn_perm)
        dx1, sums2 = _normmod_bwd(s['x1'], nw2, mod[i], dh2, dx, 1, ctx_rows, "normmod_bwd")
        dmix, dg1 = _gate_bwd(dx1, s['mix'], mod[i], 0, ctx_rows, "gate_bwd")
        if i % 2 == 0:
            do = _mm(dmix, w_out[j], tb=True, out_dtype=MXU, tm=768, tn=1024, tk=2048, name="mm_ssd_out_dx")
            gbig['ssd_w_out'][j] = _mm(s['o'], dmix, ta=True, tm=1024, tn=1024, tk=1056, name="mm_ssd_out_dw").reshape(NDEV, -1, D)
            dy, dzx, fs, dd = _ssd_finish_bwd(s['yf'], s['yb'], s['xbc'], s['zx'], s['dexp'], s['snw'], do, et_f, (T, IN),
                                              "ssd_finish_bwd")
            acc = _ssd_scan_bwd(s['xbc'], s['zx'], s['dtb'], s['alog'], e_f, et_f, s['dexp'], s['hin_f'], dy, None,
                                H, ncc, False, "ssd_scan_bwd_f")
            dxbc, ddt, ssm = _ssd_scan_bwd(s['xbc'], s['zx'], s['dtb'], s['alog'], e_b, et_b, s['dexp'], s['hin_b'], dy, acc,
                                           H, ncc, True, "ssd_scan_bwd_b")
            dzx, gscw = _ssd_conv_bwd(s['zx'], dxbc, dzx, s['cw'], s['cb'], DI, ctx_rows, "ssd_conv_bwd")
            dzx = _put_cols(dzx, ddt, IN // LANE - 1, "ssd_put_ddt")
            dh1 = _mm(dzx, w_in[j], tb=True, out_dtype=MXU, tm=768, tn=1024, tk=1152, name="mm_ssd_in_dx")
            dwi = _mm(s['h1'], dzx, ta=True, tm=1024, tn=1152, tk=1056, name="mm_ssd_in_dw")
            gbig['ssd_w_in'][j] = dwi.reshape(D, NDEV, IN // NDEV).transpose(1, 0, 2)
            gs['ssd_conv_w'][j], gs['ssd_conv_b'][j] = gscw[0:3], gscw[3]
            gs['ssd_dt_bias_f'][j], gs['ssd_dt_bias_b'][j] = ssm[1, :H], ssm[1, H:]
            gs['ssd_a_log_f'][j], gs['ssd_a_log_b'][j] = ssm[0, :H], ssm[0, H:]
            gs['ssd_d'][j], gs['ssd_norm_w'][j] = dd[1, :H], fs[0]
        else:
            do = _mm(dmix, w_o[j], tb=True, out_dtype=MXU, tm=768, tn=1024, tk=2048, name="mm_attn_out_dx")
            gbig['attn_w_o'][j] = _mm(s['o'], dmix, ta=True, tm=1024, tn=1024, tk=1056, name="mm_attn_out_dw").reshape(NDEV, -1, D)
            dq, dk, dv, dsk = _attn_bwd(s['qr'], s['kr'], s['vb'], w['attn_sinks'][j], do, n_q, ctx_rows, "attn_bwd")
            dqkv, gg = _qk_prep_bwd(s['qkv'], s['qg'], s['kg'], cos, sin, dq, dk, dv, n_q, ctx_rows, "qk_prep_bwd")
            dh1 = _mm(dqkv, w_qkv[j], tb=True, out_dtype=MXU, tm=768, tn=1024, tk=384, name="mm_qkv_dx", bslots=_ident)
            gbig['attn_w_qkv'][j] = _mm(s['h1'], dqkv, ta=True, tm=1024, tn=384, tk=1056, name="mm_qkv_dw", oslots=_ident)
            gs['attn_q_gain'][j], gs['attn_k_gain'][j] = gg[0], gg[1]
            gs['attn_sinks'][j] = dsk.reshape(ATTN_KV_HEADS, 8, LANE)[:, 0, :n_q // ATTN_KV_HEADS].reshape(n_q)
        dx, sums1 = _normmod_bwd(s['x0'], nw1, mod[i], dh1, dx1, 0, ctx_rows, "normmod_bwd")
        gs['norm1_w'][i], gs['norm2_w'][i] = sums1[0, 2] + sums1[1, 2], sums2[0, 2] + sums2[1, 2]
        dmod[i] = jnp.stack([sums1[:, 0], sums1[:, 1], dg1[:, 0], sums2[:, 0], sums2[:, 1], dg2[:, 0]], axis=1)
    grad_x = dx[ctx_rows:][None]
    dmod = jnp.stack(dmod)
    dmod_ctx, dmod_lat = dmod[:, 0].reshape(L, 6 * D), dmod[:, 1].reshape(L, 6 * D)
    gs['ada_b'] = dmod_ctx + dmod_lat

    out = {}

    flat, groups = [], []
    for name in BIG:
        groups.append(list(range(len(flat), len(flat) + len(gbig[name]))))
        flat += gbig[name]
    r1 = _rs_sibling(flat, "rs_sibling")
    cidx = jnp.reshape(ci, (1,)).astype(jnp.int32)
    ps = [_add_blocks(g, r, cidx, "rs_add") for g, r in zip(flat, r1)]
    r2 = _rs_chips(ps, groups, "rs_chips")
    for name, parts in zip(BIG, r2):
        out[name] = _adamw(parts, w[name], m[name], v[name], "adamw")

    small_g = [jnp.stack(gs[n]) if isinstance(gs[n], list) else gs[n] for n in SMALL]
    extras = [jnp.stack(gs['ssd_conv_w']), jnp.stack(gs['ffn_conv_w']), dmod_lat, dmod_ctx]
    shapes_c = [w[n].shape for n in SMALL] + [e.shape for e in extras]
    g_c = _small_allgather(_pack(small_g + extras), "gather_small")
    zeros = [jnp.zeros(e.shape, F32) for e in extras]
    res = _adamw_small(g_c, _pack([w[n] for n in SMALL] + zeros), _pack([m[n] for n in SMALL] + zeros),
                       _pack([v[n] for n in SMALL] + zeros), "adamw_small")
    res = [_unpack(r, shapes_c) for r in res]
    for k, n in enumerate(SMALL):
        out[n] = tuple(r[k] for r in res)
    g_scw, g_fcw = res[0][len(SMALL)], res[0][len(SMALL) + 1]
    g_scw = lax.dynamic_index_in_dim(g_scw.reshape(n_ssd, 3, NDEV, XBC // NDEV), me, axis=2, keepdims=False)
    g_fcw = lax.dynamic_index_in_dim(g_fcw.reshape(L, 3, NDEV, G), me, axis=2, keepdims=False)
    conv = ['ssd_conv_w', 'ffn_conv_w']
    res = _adamw_small(_pack([g_scw, g_fcw])[None], _pack([w[n] for n in conv]), _pack([m[n] for n in conv]),
                       _pack([v[n] for n in conv]), "adamw_conv")
    res = [_unpack(r, [w[n].shape for n in conv]) for r in res]
    for k, n in enumerate(conv):
        out[n] = tuple(r[k] for r in res)

    all_lat, all_ctx = _unpack(g_c, shapes_c)[-2:]
    my_cols = lambda t: lax.dynamic_slice(t, (0, 0, me * nl), (NDEV, L, nl))
    dml = jnp.concatenate([my_cols(all_lat), jnp.zeros((MOD_ROWS - NDEV, L, nl), F32)], axis=0)
    g_ada, pc = _modulation_bwd(s_in, dml, my_cols(all_ctx), w['ada_w'], "modulation_bwd")
    out['ada_w'] = _adamw(g_ada[None], w['ada_w'], m['ada_w'], v['ada_w'], "adamw")
    g_d = _small_allgather(_pack([pc[0]]), "gather_cctx")
    res = _cctx_update(g_d, _pack([w['c_ctx']]), _pack([m['c_ctx']]), _pack([v['c_ctx']]), "adamw_cctx")
    out['c_ctx'] = tuple(_unpack(r, [(D,)])[0] for r in res)

    return (loss, grad_x) + tuple(out[n][k] for k in range(4) for n in WEIGHTS)


def kernel(x, c, ctx, c_ctx, ada_w, ada_b, norm1_w, norm2_w, ssd_w_in, ssd_conv_w, ssd_conv_b, ssd_dt_bias_f, ssd_dt_bias_b, ssd_a_log_f, ssd_a_log_b, ssd_d, ssd_norm_w, ssd_w_out, attn_w_qkv, attn_q_gain, attn_k_gain, attn_sinks, attn_w_o, ffn_w_up, ffn_conv_w, ffn_conv_b, ffn_w_down, loss_target, m_c_ctx, m_ada_w, m_ada_b, m_norm1_w, m_norm2_w, m_ssd_w_in, m_ssd_conv_w, m_ssd_conv_b, m_ssd_dt_bias_f, m_ssd_dt_bias_b, m_ssd_a_log_f, m_ssd_a_log_b, m_ssd_d, m_ssd_norm_w, m_ssd_w_out, m_attn_w_qkv, m_attn_q_gain, m_attn_k_gain, m_attn_sinks, m_attn_w_o, m_ffn_w_up, m_ffn_conv_w, m_ffn_conv_b, m_ffn_w_down, v_c_ctx, v_ada_w, v_ada_b, v_norm1_w, v_norm2_w, v_ssd_w_in, v_ssd_conv_w, v_ssd_conv_b, v_ssd_dt_bias_f, v_ssd_dt_bias_b, v_ssd_a_log_f, v_ssd_a_log_b, v_ssd_d, v_ssd_norm_w, v_ssd_w_out, v_attn_w_qkv, v_attn_q_gain, v_attn_k_gain, v_attn_sinks, v_attn_w_o, v_ffn_w_up, v_ffn_conv_w, v_ffn_conv_b, v_ffn_w_down):
    w = dict(c_ctx=c_ctx, ada_w=ada_w, ada_b=ada_b, norm1_w=norm1_w, norm2_w=norm2_w, ssd_w_in=ssd_w_in, ssd_conv_w=ssd_conv_w, ssd_conv_b=ssd_conv_b, ssd_dt_bias_f=ssd_dt_bias_f, ssd_dt_bias_b=ssd_dt_bias_b, ssd_a_log_f=ssd_a_log_f, ssd_a_log_b=ssd_a_log_b, ssd_d=ssd_d, ssd_norm_w=ssd_norm_w, ssd_w_out=ssd_w_out, attn_w_qkv=attn_w_qkv, attn_q_gain=attn_q_gain, attn_k_gain=attn_k_gain, attn_sinks=attn_sinks, attn_w_o=attn_w_o, ffn_w_up=ffn_w_up, ffn_conv_w=ffn_conv_w, ffn_conv_b=ffn_conv_b, ffn_w_down=ffn_w_down)
    m = dict(c_ctx=m_c_ctx, ada_w=m_ada_w, ada_b=m_ada_b, norm1_w=m_norm1_w, norm2_w=m_norm2_w, ssd_w_in=m_ssd_w_in, ssd_conv_w=m_ssd_conv_w, ssd_conv_b=m_ssd_conv_b, ssd_dt_bias_f=m_ssd_dt_bias_f, ssd_dt_bias_b=m_ssd_dt_bias_b, ssd_a_log_f=m_ssd_a_log_f, ssd_a_log_b=m_ssd_a_log_b, ssd_d=m_ssd_d, ssd_norm_w=m_ssd_norm_w, ssd_w_out=m_ssd_w_out, attn_w_qkv=m_attn_w_qkv, attn_q_gain=m_attn_q_gain, attn_k_gain=m_attn_k_gain, attn_sinks=m_attn_sinks, attn_w_o=m_attn_w_o, ffn_w_up=m_ffn_w_up, ffn_conv_w=m_ffn_conv_w, ffn_conv_b=m_ffn_conv_b, ffn_w_down=m_ffn_w_down)
    v = dict(c_ctx=v_c_ctx, ada_w=v_ada_w, ada_b=v_ada_b, norm1_w=v_norm1_w, norm2_w=v_norm2_w, ssd_w_in=v_ssd_w_in, ssd_conv_w=v_ssd_conv_w, ssd_conv_b=v_ssd_conv_b, ssd_dt_bias_f=v_ssd_dt_bias_f, ssd_dt_bias_b=v_ssd_dt_bias_b, ssd_a_log_f=v_ssd_a_log_f, ssd_a_log_b=v_ssd_a_log_b, ssd_d=v_ssd_d, ssd_norm_w=v_ssd_norm_w, ssd_w_out=v_ssd_w_out, attn_w_qkv=v_attn_w_qkv, attn_q_gain=v_attn_q_gain, attn_k_gain=v_attn_k_gain, attn_sinks=v_attn_sinks, attn_w_o=v_attn_w_o, ffn_w_up=v_ffn_w_up, ffn_conv_w=v_ffn_conv_w, ffn_conv_b=v_ffn_conv_b, ffn_w_down=v_ffn_w_down)
    return _step(x, c, ctx, w, loss_target, m, v)
```

```python
import functools

import numpy as np
import jax
import jax.numpy as jnp
from jax import lax
from jax.experimental import pallas as pl
from jax.experimental.pallas import tpu as pltpu

F32 = jnp.float32
BF16 = jnp.bfloat16
MXU = BF16
XFER = BF16
NORM_EPS = 1e-6
VMEM_CAP = 56 * 1024 * 1024
HALO = 8
LANE = 128
NDEV = 8

GRID_W = 64
ROPE_THETA = 10000.0
ATTN_KV_HEADS = 4
ATTN_WINDOW = 128
BLK = 128
SSD_GROUPS = 8

ADAM_LR, ADAM_B1, ADAM_B2, ADAM_EPS, ADAM_WD, ADAM_STEP = 0.001, 0.9, 0.999, 1e-08, 0.01, 10

MESH_ID = pl.DeviceIdType.MESH


def _cparams(sem, est_bytes):
    lim = int(min(VMEM_CAP, max(16 * 1024 * 1024, est_bytes * 1.3 + (4 << 20))))
    return pltpu.CompilerParams(dimension_semantics=sem, vmem_limit_bytes=lim)


def _nbytes(shape, dtype):
    return int(np.prod(shape)) * jnp.dtype(dtype).itemsize


def _silu(x):
    return x * jax.nn.sigmoid(x)


def _dsilu(x):
    s = jax.nn.sigmoid(x)
    return s * (1.0 + x * (1.0 - s))


def _split3(v):
    h = v.astype(BF16)
    r = v - h.astype(F32)
    m = r.astype(BF16)
    l = (r - m.astype(F32)).astype(BF16)
    return h, m, l


def _dot(a, b, dn=(((1,), (0,)), ((), ()))):
    return lax.dot_general(a, b, dn, preferred_element_type=F32)


NT = (((1,), (1,)), ((), ()))
TN = (((0,), (0,)), ((), ()))


def _dot3_rhs(sel, v):
    return sum(_dot(sel, p) for p in _split3(v))


def _dot3_lhs(v, sel, dn=(((1,), (0,)), ((), ()))):
    return sum(_dot(p, sel, dn) for p in _split3(v))


def _dot2_lhs(v, sel):
    h, m, _ = _split3(v)
    return _dot(h, sel) + _dot(m, sel)


def _ident(s):
    return s


def _ffn_perm(s):
    return (s % 2) * 4 + s // 2


def _mm(a, b, *, ta=False, tb=False, out_dtype=F32, tm, tn, tk, name, bslots=None, oslots=None,
        resid=None, gate=None, ctx_rows=0):
    M = a.shape[1] if ta else a.shape[0]
    K = a.shape[0] if ta else a.shape[1]
    if bslots is None:
        N = b.shape[0] if tb else b.shape[1]
    else:
        G = b.shape[2]
        N = b.shape[1] if tb else NDEV * G
        assert (NDEV * G == K) if tb else (b.shape[1] == K)
    tm, tn, tk = min(tm, M), min(tn, N), min(tk, K)
    if bslots is not None:
        if tb:
            tk = min(tk, G)
            assert G % tk == 0
        else:
            tn = min(tn, G)
            assert G % tn == 0
    if oslots is not None:
        Go = N // NDEV
        tn = min(tn, Go)
        assert Go % tn == 0
    assert M % tm == 0 and N % tn == 0 and K % tk == 0, (name, M, N, K, tm, tn, tk)
    nk = K // tk
    fused = resid is not None
    dn = (((0 if ta else 1,), (1 if tb else 0,)), ((), ()))

    def body(*refs):
        if fused:
            a_ref, b_ref, r_ref, g_ref, o_ref, x_ref = refs[:6]
            rest = refs[6:]
        else:
            a_ref, b_ref, o_ref = refs[:3]
            rest = refs[3:]
        bv = b_ref[0] if bslots is not None else b_ref[...]
        p = lax.dot_general(a_ref[...].astype(MXU), bv.astype(MXU), dn, preferred_element_type=F32)

        def finish(acc):
            if oslots is not None:
                o_ref[0] = acc.astype(o_ref.dtype)
            else:
                o_ref[...] = acc.astype(o_ref.dtype)
            if fused:
                row = pl.program_id(0) * tm + lax.broadcasted_iota(jnp.int32, (tm, 1), 0)
                g = jnp.where(row < ctx_rows, g_ref[0:1, :], g_ref[1:2, :])
                x_ref[...] = r_ref[...] + g * acc

        if nk == 1:
            finish(p)
        else:
            acc_ref = rest[0]
            k = pl.program_id(2)

            @pl.when(k == 0)
            def _():
                acc_ref[...] = p

            @pl.when(k > 0)
            def _():
                acc_ref[...] += p

            @pl.when(k == nk - 1)
            def _():
                finish(acc_ref[...])

    a_spec = pl.BlockSpec((tk, tm), lambda i, j, k: (k, i)) if ta else pl.BlockSpec((tm, tk), lambda i, j, k: (i, k))
    if bslots is None:
        b_spec = pl.BlockSpec((tn, tk), lambda i, j, k: (j, k)) if tb else pl.BlockSpec((tk, tn), lambda i, j, k: (k, j))
    elif tb:
        kpg = G // tk
        b_spec = pl.BlockSpec((1, tn, tk), lambda i, j, k: (bslots(k // kpg), j, k % kpg))
    else:
        npg = G // tn
        b_spec = pl.BlockSpec((1, tk, tn), lambda i, j, k: (bslots(j // npg), k, j % npg))
    if oslots is None:
        o_spec = pl.BlockSpec((tm, tn), lambda i, j, k: (i, j))
        o_shape = jax.ShapeDtypeStruct((M, N), out_dtype)
    else:
        opg = Go // tn
        o_spec = pl.BlockSpec((1, tm, tn), lambda i, j, k: (oslots(j // opg), i, j % opg))
        o_shape = jax.ShapeDtypeStruct((NDEV, M, Go), out_dtype)
    in_specs = [a_spec, b_spec]
    out_shape = [o_shape]
    out_specs = [o_spec]
    args = [a, b]
    est = 2 * (_nbytes((tm, tk), a.dtype) + _nbytes((tk, tn), b.dtype) + _nbytes((tm, tn), out_dtype)) + 3 * _nbytes((tm, tn), F32)
    if fused:
        in_specs += [o_spec, pl.BlockSpec((2, tn), lambda i, j, k: (0, j))]
        out_shape.append(jax.ShapeDtypeStruct((M, N), F32))
        out_specs.append(o_spec)
        args += [resid, gate]
        est += 4 * _nbytes((tm, tn), F32)
    scratch = [] if nk == 1 else [pltpu.VMEM((tm, tn), F32)]
    res = pl.pallas_call(
        body, name=name, grid=(M // tm, N // tn, nk), in_specs=in_specs, out_specs=out_specs, out_shape=out_shape,
        scratch_shapes=scratch, compiler_params=_cparams(("parallel", "parallel", "arbitrary"), est),
    )(*args)
    return res if fused else res[0]


def _stream_of(i, tr, ctx_rows):
    return jnp.where(i * tr < ctx_rows, 0, 1)


def _acc_by_stream(sums_ref, part, i, n_ctx):
    @pl.when((i == 0) | (i == n_ctx))
    def _():
        sums_ref[0] = part

    @pl.when((i != 0) & (i != n_ctx))
    def _():
        sums_ref[0] += part


def _normmod(x, nw, mod, which, ctx_rows, name, tr=256):
    T, D = x.shape
    tr = min(tr, ctx_rows)
    assert T % tr == 0 and ctx_rows % tr == 0
    s_sh, s_sc = 3 * which, 3 * which + 1

    def body(x_ref, nw_ref, mod_ref, h_ref):
        xv = x_ref[...]
        r = lax.rsqrt(jnp.mean(xv * xv, axis=-1, keepdims=True) + NORM_EPS)
        y = (xv * r) * nw_ref[...]
        h_ref[...] = (y * (1.0 + mod_ref[0, s_sc:s_sc + 1, :]) + mod_ref[0, s_sh:s_sh + 1, :]).astype(h_ref.dtype)

    return pl.pallas_call(
        body, name=name, grid=(T // tr,),
        in_specs=[pl.BlockSpec((tr, D), lambda i: (i, 0)), pl.BlockSpec((1, D), lambda i: (0, 0)),
                  pl.BlockSpec((1, 6, D), lambda i: (_stream_of(i, tr, ctx_rows), 0, 0))],
        out_specs=pl.BlockSpec((tr, D), lambda i: (i, 0)),
        out_shape=jax.ShapeDtypeStruct((T, D), MXU),
        compiler_params=_cparams(("parallel",), 10 * _nbytes((tr, D), F32)),
    )(x, nw, mod)


def _normmod_bwd(x, nw, mod, dh, dx_in, which, ctx_rows, name, tr=256):
    T, D = x.shape
    tr = min(tr, ctx_rows)
    s_sc = 3 * which + 1
    n_ctx = ctx_rows // tr

    def body(x_ref, nw_ref, mod_ref, dh_ref, dxi_ref, dx_ref, sums_ref):
        i = pl.program_id(0)
        xv = x_ref[...]
        r = lax.rsqrt(jnp.mean(xv * xv, axis=-1, keepdims=True) + NORM_EPS)
        xh = xv * r
        dh_v = dh_ref[...].astype(F32)
        sc1 = 1.0 + mod_ref[0, s_sc:s_sc + 1, :]
        nwv = nw_ref[...]
        dxh = dh_v * (nwv * sc1)
        dx_ref[...] = dxi_ref[...] + r * (dxh - xh * jnp.mean(dxh * xh, axis=-1, keepdims=True))
        t = dh_v * xh
        part = jnp.concatenate([jnp.sum(dh_v, axis=0, keepdims=True), jnp.sum(t * nwv, axis=0, keepdims=True),
                                jnp.sum(t * sc1, axis=0, keepdims=True), jnp.zeros((5, D), F32)], axis=0)
        _acc_by_stream(sums_ref, part, i, n_ctx)

    row = pl.BlockSpec((tr, D), lambda i: (i, 0))
    return pl.pallas_call(
        body, name=name, grid=(T // tr,),
        in_specs=[row, pl.BlockSpec((1, D), lambda i: (0, 0)),
                  pl.BlockSpec((1, 6, D), lambda i: (_stream_of(i, tr, ctx_rows), 0, 0)), row, row],
        out_specs=[row, pl.BlockSpec((1, 8, D), lambda i: (_stream_of(i, tr, ctx_rows), 0, 0))],
        out_shape=[jax.ShapeDtypeStruct((T, D), F32), jax.ShapeDtypeStruct((2, 8, D), F32)],
        compiler_params=_cparams(("arbitrary",), 16 * _nbytes((tr, D), F32)),
    )(x, nw, mod, dh, dx_in)


def _gate_bwd(dx, mix, mod, which, ctx_rows, name, tr=256):
    T, D = dx.shape
    tr = min(tr, ctx_rows)
    s_g = 3 * which + 2
    n_ctx = ctx_rows // tr

    def body(dx_ref, mix_ref, mod_ref, dm_ref, sums_ref):
        i = pl.program_id(0)
        dxv = dx_ref[...]
        dm_ref[...] = (dxv * mod_ref[0, s_g:s_g + 1, :]).astype(dm_ref.dtype)
        part = jnp.concatenate([jnp.sum(dxv * mix_ref[...], axis=0, keepdims=True), jnp.zeros((7, D), F32)], axis=0)
        _acc_by_stream(sums_ref, part, i, n_ctx)

    row = pl.BlockSpec((tr, D), lambda i: (i, 0))
    return pl.pallas_call(
        body, name=name, grid=(T // tr,),
        in_specs=[row, row, pl.BlockSpec((1, 6, D), lambda i: (_stream_of(i, tr, ctx_rows), 0, 0))],
        out_specs=[row, pl.BlockSpec((1, 8, D), lambda i: (_stream_of(i, tr, ctx_rows), 0, 0))],
        out_shape=[jax.ShapeDtypeStruct((T, D), MXU), jax.ShapeDtypeStruct((2, 8, D), F32)],
        compiler_params=_cparams(("arbitrary",), 10 * _nbytes((tr, D), F32)),
    )(dx, mix, mod)


def _halo_specs(tr, tn, n_row_tiles, col_of):
    g = tr // HALO
    last = n_row_tiles * g - 1
    return [pl.BlockSpec((HALO, tn), lambda j, i: (jnp.maximum(i * g - 1, 0), col_of(j))),
            pl.BlockSpec((tr, tn), lambda j, i: (i, col_of(j))),
            pl.BlockSpec((HALO, tn), lambda j, i: (jnp.minimum((i + 1) * g, last), col_of(j)))]


def _ext(p_ref, m_ref, n_ref):
    return jnp.concatenate([p_ref[...].astype(F32), m_ref[...].astype(F32), n_ref[...].astype(F32)], axis=0)


def _seq_masks(i, tr, ctx_rows, total_rows):
    row = i * tr - HALO + lax.broadcasted_iota(jnp.int32, (tr + 2 * HALO, 1), 0)
    has_prev = (row != 0) & (row != ctx_rows)
    has_next = (row != ctx_rows - 1) & (row != total_rows - 1)
    return has_prev, has_next


def _shift_down(e):
    return pltpu.roll(e, 1, 0)


def _shift_up(e):
    return pltpu.roll(e, e.shape[0] - 1, 0)


def _conv3(e, w, has_prev, has_next):
    prev = jnp.where(has_prev, _shift_down(e), 0.0)
    nxt = jnp.where(has_next, _shift_up(e), 0.0)
    return prev * w[0:1, :] + e * w[1:2, :] + nxt * w[2:3, :]


def _conv3_t(d, w, has_prev, has_next):
    from_next = jnp.where(has_next, _shift_up(d), 0.0)
    from_prev = jnp.where(has_prev, _shift_down(d), 0.0)
    return from_next * w[0:1, :] + d * w[1:2, :] + from_prev * w[2:3, :]


def _conv_wgrad(d, e, has_prev, has_next):
    n = e.shape[0]
    c = slice(HALO, n - HALO)
    prev = jnp.where(has_prev, _shift_down(e), 0.0)
    nxt = jnp.where(has_next, _shift_up(e), 0.0)
    dc = d[c]
    return jnp.concatenate([jnp.sum(dc * prev[c], axis=0, keepdims=True), jnp.sum(dc * e[c], axis=0, keepdims=True),
                            jnp.sum(dc * nxt[c], axis=0, keepdims=True), jnp.sum(dc, axis=0, keepdims=True),
                            jnp.zeros((4, e.shape[1]), F32)], axis=0)


def _acc_first(ref, part, i):
    @pl.when(i == 0)
    def _():
        ref[...] = part

    @pl.when(i > 0)
    def _():
        ref[...] += part


def _ffn_mid(u, cw, cb, ctx_rows, name, tr=128):
    T, F2 = u.shape
    G = F2 // NDEV
    tr = min(tr, ctx_rows)
    nr, nc = T // tr, NDEV // 2

    def body(up, um, un, w_ref, b_ref, a_ref):
        i = pl.program_id(1)
        hp, hn = _seq_masks(i, tr, ctx_rows, T)
        uc = _conv3(_ext(up, um, un), w_ref[...], hp, hn)[HALO:HALO + tr] + b_ref[...]
        a_ref[...] = (_silu(uc[:, G:]) * uc[:, :G]).astype(a_ref.dtype)

    return pl.pallas_call(
        body, name=name, grid=(nc, nr),
        in_specs=_halo_specs(tr, 2 * G, nr, lambda j: j) + [pl.BlockSpec((3, 2 * G), lambda j, i: (0, j)),
                                                             pl.BlockSpec((1, 2 * G), lambda j, i: (0, j))],
        out_specs=pl.BlockSpec((tr, G), lambda j, i: (i, j)),
        out_shape=jax.ShapeDtypeStruct((T, F2 // 2), MXU),
        compiler_params=_cparams(("parallel", "parallel"), 12 * _nbytes((tr + 16, 2 * G), F32)),
    )(u, u, u, cw, cb)


def _ffn_mid_bwd(u, da, cw, cb, ctx_rows, name, tr=128):
    T, F2 = u.shape
    G = F2 // NDEV
    tr = min(tr, ctx_rows)
    nr, nc = T // tr, NDEV // 2

    def body(up, um, un, dp, dm, dn_, w_ref, b_ref, du_ref, gw_ref):
        i = pl.program_id(1)
        hp, hn = _seq_masks(i, tr, ctx_rows, T)
        e = _ext(up, um, un)
        w = w_ref[...]
        uc = _conv3(e, w, hp, hn) + b_ref[...]
        val, gt = uc[:, :G], uc[:, G:]
        dav = _ext(dp, dm, dn_)
        duc = jnp.concatenate([dav * _silu(gt), dav * val * _dsilu(gt)], axis=1)
        du_ref[...] = _conv3_t(duc, w, hp, hn)[HALO:HALO + tr].astype(du_ref.dtype)
        _acc_first(gw_ref, _conv_wgrad(duc, e, hp, hn), i)

    du, gw = pl.pallas_call(
        body, name=name, grid=(nc, nr),
        in_specs=(_halo_specs(tr, 2 * G, nr, lambda j: j) + _halo_specs(tr, G, nr, lambda j: j)
                  + [pl.BlockSpec((3, 2 * G), lambda j, i: (0, j)), pl.BlockSpec((1, 2 * G), lambda j, i: (0, j))]),
        out_specs=[pl.BlockSpec((tr, 2 * G), lambda j, i: (i, j)), pl.BlockSpec((8, 2 * G), lambda j, i: (0, j))],
        out_shape=[jax.ShapeDtypeStruct((T, F2), MXU), jax.ShapeDtypeStruct((8, F2), F32)],
        compiler_params=_cparams(("parallel", "arbitrary"), 24 * _nbytes((tr + 16, 2 * G), F32)),
    )(u, u, u, da, da, da, cw, cb)
    return du, gw


def _ssd_conv(zx, cw, cb, col0, ctx_rows, name, tr=256, tn=512):
    T = zx.shape[0]
    C = cw.shape[1]
    tr, tn = min(tr, ctx_rows), min(tn, C)
    assert C % tn == 0 and col0 % tn == 0
    nr, nc, cb0 = T // tr, C // tn, col0 // tn

    def body(zp, zm, zn, w_ref, b_ref, o_ref):
        i = pl.program_id(1)
        hp, hn = _seq_masks(i, tr, ctx_rows, T)
        o_ref[...] = _silu(_conv3(_ext(zp, zm, zn), w_ref[...], hp, hn)[HALO:HALO + tr] + b_ref[...])

    return pl.pallas_call(
        body, name=name, grid=(nc, nr),
        in_specs=_halo_specs(tr, tn, nr, lambda j: j + cb0) + [pl.BlockSpec((3, tn), lambda j, i: (0, j)),
                                                                pl.BlockSpec((1, tn), lambda j, i: (0, j))],
        out_specs=pl.BlockSpec((tr, tn), lambda j, i: (i, j)),
        out_shape=jax.ShapeDtypeStruct((T, C), F32),
        compiler_params=_cparams(("parallel", "parallel"), 12 * _nbytes((tr + 16, tn), F32)),
    )(zx, zx, zx, cw, cb)


def _ssd_conv_bwd(zx, dxbc, dzx, cw, cb, col0, ctx_rows, name, tr=256, tn=512):
    T = zx.shape[0]
    C = cw.shape[1]
    tr, tn = min(tr, ctx_rows), min(tn, C)
    nr, nc, cb0 = T // tr, C // tn, col0 // tn

    def body(zp, zm, zn, dp, dm, dn_, w_ref, b_ref, dzx_in, dz_ref, gw_ref):
        del dzx_in
        i = pl.program_id(1)
        hp, hn = _seq_masks(i, tr, ctx_rows, T)
        e = _ext(zp, zm, zn)
        w = w_ref[...]
        pre = _conv3(e, w, hp, hn) + b_ref[...]
        dpre = _ext(dp, dm, dn_) * _dsilu(pre)
        dz_ref[...] = _conv3_t(dpre, w, hp, hn)[HALO:HALO + tr].astype(dz_ref.dtype)
        _acc_first(gw_ref, _conv_wgrad(dpre, e, hp, hn), i)

    return pl.pallas_call(
        body, name=name, grid=(nc, nr),
        in_specs=(_halo_specs(tr, tn, nr, lambda j: j + cb0) + _halo_specs(tr, tn, nr, lambda j: j)
                  + [pl.BlockSpec((3, tn), lambda j, i: (0, j)), pl.BlockSpec((1, tn), lambda j, i: (0, j)),
                     pl.BlockSpec(memory_space=pl.ANY)]),
        out_specs=[pl.BlockSpec((tr, tn), lambda j, i: (i, j + cb0)), pl.BlockSpec((8, tn), lambda j, i: (0, j))],
        out_shape=[jax.ShapeDtypeStruct(dzx.shape, dzx.dtype), jax.ShapeDtypeStruct((8, C), F32)],
        input_output_aliases={8: 0},
        compiler_params=_cparams(("parallel", "arbitrary"), 24 * _nbytes((tr + 16, tn), F32)),
    )(zx, zx, zx, dxbc, dxbc, dxbc, cw, cb, dzx)


def _cat_cols(w3, name, tr=256):
    n, K, G = w3.shape
    tr = min(tr, K)

    def body(w_ref, o_ref):
        o_ref[...] = jnp.concatenate([w_ref[d].astype(F32) for d in range(n)], axis=1).astype(o_ref.dtype)

    return pl.pallas_call(
        body, name=name, grid=(K // tr,),
        in_specs=[pl.BlockSpec((n, tr, G), lambda i: (0, i, 0))], out_specs=pl.BlockSpec((tr, n * G), lambda i: (i, 0)),
        out_shape=jax.ShapeDtypeStruct((K, n * G), w3.dtype),
        compiler_params=_cparams(("parallel",), 6 * _nbytes((tr, n * G), F32)),
    )(w3)


def _split_cols(g, n, name, tr=256):
    K, NG = g.shape
    G = NG // n
    tr = min(tr, K)

    def body(g_ref, o_ref):
        for d in range(n):
            o_ref[d] = g_ref[:, d * G:(d + 1) * G]

    return pl.pallas_call(
        body, name=name, grid=(K // tr,),
        in_specs=[pl.BlockSpec((tr, NG), lambda i: (i, 0))], out_specs=pl.BlockSpec((n, tr, G), lambda i: (0, i, 0)),
        out_shape=jax.ShapeDtypeStruct((n, K, G), g.dtype),
        compiler_params=_cparams(("parallel",), 6 * _nbytes((tr, NG), F32)),
    )(g)


def _put_cols(dst, src, col_blk, name, tr=256):
    T, W = src.shape
    tr = min(tr, T)

    def body(s_ref, d_in, o_ref):
        del d_in
        o_ref[...] = s_ref[...].astype(o_ref.dtype)

    return pl.pallas_call(
        body, name=name, grid=(T // tr,),
        in_specs=[pl.BlockSpec((tr, W), lambda i: (i, 0)), pl.BlockSpec(memory_space=pl.ANY)],
        out_specs=pl.BlockSpec((tr, W), lambda i: (i, col_blk)),
        out_shape=jax.ShapeDtypeStruct(dst.shape, dst.dtype),
        input_output_aliases={1: 0},
        compiler_params=_cparams(("parallel",), 8 * _nbytes((tr, W), F32)),
    )(src, dst)


def _rope_tables(t_lat, ctx_rows, hd):
    half, quarter = hd // 2, hd // 4
    pos = jnp.arange(t_lat)
    row = (pos // GRID_W).astype(F32)
    col = (pos % GRID_W).astype(F32)
    inv_freq = ROPE_THETA ** (-jnp.arange(0, half, 2, dtype=F32) / half)
    ar, ac = row[:, None] * inv_freq[None, :], col[:, None] * inv_freq[None, :]
    cos = jnp.concatenate([jnp.cos(ar), jnp.cos(ar), jnp.cos(ac), jnp.cos(ac)], axis=1)
    sin = jnp.concatenate([-jnp.sin(ar), jnp.sin(ar), -jnp.sin(ac), jnp.sin(ac)], axis=1)
    del quarter
    cos = jnp.concatenate([jnp.ones((ctx_rows, hd), F32), cos], axis=0)
    sin = jnp.concatenate([jnp.zeros((ctx_rows, hd), F32), sin], axis=0)
    return cos, sin


def _partner(y):
    hd = y.shape[1]
    q = hd // 4
    lane = lax.broadcasted_iota(jnp.int32, y.shape, 1)
    return jnp.where((lane % (2 * q)) < q, pltpu.roll(y, hd - q, 1), pltpu.roll(y, q, 1))


def _qk_prep(qkv, qg, kg, cos, sin, n_q, ctx_rows, name, tr=256):
    T = qkv.shape[0]
    hd = qg.shape[1]
    n_kv = ATTN_KV_HEADS
    tr = min(tr, ctx_rows)

    def body(x_ref, qg_ref, kg_ref, c_ref, s_ref, q_ref, k_ref, v_ref):
        cv, sv = c_ref[...], s_ref[...]
        for h in range(n_q + n_kv):
            xh = x_ref[:, h * hd:(h + 1) * hd]
            r = lax.rsqrt(jnp.mean(xh * xh, axis=-1, keepdims=True) + NORM_EPS)
            y = (xh * r) * (qg_ref[...] if h < n_q else kg_ref[...])
            rot = y * cv + _partner(y) * sv
            if h < n_q:
                q_ref[:, h * hd:(h + 1) * hd] = rot.astype(q_ref.dtype)
            else:
                k_ref[:, (h - n_q) * hd:(h - n_q + 1) * hd] = rot.astype(k_ref.dtype)
        v_ref[...] = x_ref[:, (n_q + n_kv) * hd:].astype(v_ref.dtype)

    W = qkv.shape[1]
    return pl.pallas_call(
        body, name=name, grid=(T // tr,),
        in_specs=[pl.BlockSpec((tr, W), lambda i: (i, 0)), pl.BlockSpec((1, hd), lambda i: (0, 0)),
                  pl.BlockSpec((1, hd), lambda i: (0, 0)), pl.BlockSpec((tr, hd), lambda i: (i, 0)),
                  pl.BlockSpec((tr, hd), lambda i: (i, 0))],
        out_specs=[pl.BlockSpec((tr, n_q * hd), lambda i: (i, 0)), pl.BlockSpec((tr, n_kv * hd), lambda i: (i, 0)),
                   pl.BlockSpec((tr, n_kv * hd), lambda i: (i, 0))],
        out_shape=[jax.ShapeDtypeStruct((T, n_q * hd), MXU), jax.ShapeDtypeStruct((T, n_kv * hd), MXU),
                   jax.ShapeDtypeStruct((T, n_kv * hd), MXU)],
        compiler_params=_cparams(("parallel",), 6 * _nbytes((tr, W), F32)),
    )(qkv, qg, kg, cos, sin)


def _qk_prep_bwd(qkv, qg, kg, cos, sin, dq, dk, dv, n_q, ctx_rows, name, tr=256):
    T, W = qkv.shape
    hd = qg.shape[1]
    n_kv = ATTN_KV_HEADS
    tr = min(tr, ctx_rows)

    def body(x_ref, qg_ref, kg_ref, c_ref, s_ref, dq_ref, dk_ref, dv_ref, o_ref, g_ref):
        i = pl.program_id(0)
        cv, sv = c_ref[...], s_ref[...]
        gq = jnp.zeros((1, hd), F32)
        gk = jnp.zeros((1, hd), F32)
        for h in range(n_q + n_kv):
            xh = x_ref[:, h * hd:(h + 1) * hd]
            gain = qg_ref[...] if h < n_q else kg_ref[...]
            drot = (dq_ref[:, h * hd:(h + 1) * hd] if h < n_q else dk_ref[:, (h - n_q) * hd:(h - n_q + 1) * hd]).astype(F32)
            dy = drot * cv + _partner(drot * sv)
            r = lax.rsqrt(jnp.mean(xh * xh, axis=-1, keepdims=True) + NORM_EPS)
            xn = xh * r
            gsum = jnp.sum(dy * xn, axis=0, keepdims=True)
            if h < n_q:
                gq = gq + gsum
            else:
                gk = gk + gsum
            dxn = dy * gain
            o_ref[:, h * hd:(h + 1) * hd] = (r * (dxn - xn * jnp.mean(dxn * xn, axis=-1, keepdims=True))).astype(o_ref.dtype)
        o_ref[:, (n_q + n_kv) * hd:] = dv_ref[...].astype(o_ref.dtype)
        _acc_first(g_ref, jnp.concatenate([gq, gk, jnp.zeros((6, hd), F32)], axis=0), i)

    return pl.pallas_call(
        body, name=name, grid=(T // tr,),
        in_specs=[pl.BlockSpec((tr, W), lambda i: (i, 0)), pl.BlockSpec((1, hd), lambda i: (0, 0)),
                  pl.BlockSpec((1, hd), lambda i: (0, 0)), pl.BlockSpec((tr, hd), lambda i: (i, 0)),
                  pl.BlockSpec((tr, hd), lambda i: (i, 0)), pl.BlockSpec((tr, n_q * hd), lambda i: (i, 0)),
                  pl.BlockSpec((tr, n_kv * hd), lambda i: (i, 0)), pl.BlockSpec((tr, n_kv * hd), lambda i: (i, 0))],
        out_specs=[pl.BlockSpec((tr, W), lambda i: (i, 0)), pl.BlockSpec((8, hd), lambda i: (0, 0))],
        out_shape=[jax.ShapeDtypeStruct((T, W), MXU), jax.ShapeDtypeStruct((8, hd), F32)],
        compiler_params=_cparams(("arbitrary",), 8 * _nbytes((tr, W), F32)),
    )(qkv, qg, kg, cos, sin, dq, dk, dv)


def _attn_scores(q_ref, k_ref, sink_ref, h, qb, ctx_rows, nb, hd, grp):
    scale = hd ** -0.5
    w0 = jnp.clip(qb - 1, 0, nb - 3) * BLK
    w0 = pl.multiple_of(w0, BLK)
    qv = q_ref[...]
    qs = jnp.concatenate([qv[:, g * hd:(g + 1) * hd] for g in range(grp)], axis=0)
    kc = k_ref[0:ctx_rows, :]
    kb = k_ref[pl.ds(w0, 3 * BLK), :]
    s_c = _dot(qs, kc, NT) * scale
    s_b = _dot(qs, kb, NT) * scale
    n = grp * BLK
    qpos = qb * BLK + lax.broadcasted_iota(jnp.int32, (n, 3 * BLK), 0) % BLK
    kpos = w0 + lax.broadcasted_iota(jnp.int32, (n, 3 * BLK), 1)
    ok = (jnp.abs(kpos - qpos) <= ATTN_WINDOW) & (kpos >= ctx_rows) & (qpos >= ctx_rows)
    s_b = jnp.where(ok, s_b, -jnp.inf)
    gi = lax.broadcasted_iota(jnp.int32, (n, 1), 0) // BLK
    sink = jnp.zeros((n, 1), F32)
    for g in range(grp):
        sink = jnp.where(gi == g, sink_ref[h * grp + g], sink)
    m = jnp.maximum(jnp.maximum(jnp.max(s_c, axis=1, keepdims=True), jnp.max(s_b, axis=1, keepdims=True)), sink)
    e_c, e_b, e_s = jnp.exp(s_c - m), jnp.exp(s_b - m), jnp.exp(sink - m)
    inv = 1.0 / (jnp.sum(e_c, axis=1, keepdims=True) + jnp.sum(e_b, axis=1, keepdims=True) + e_s)
    return qs, kc, kb, w0, e_c * inv, e_b * inv, e_s * inv, gi


def _attn_fwd(qr, kr, vb, sinks, n_q, ctx_rows, name):
    T = qr.shape[0]
    n_kv = ATTN_KV_HEADS
    grp = n_q // n_kv
    hd = qr.shape[1] // n_q
    nb = T // BLK

    def body(sink_ref, q_ref, k_ref, v_ref, o_ref):
        h, qb = pl.program_id(0), pl.program_id(1)
        _, _, _, w0, p_c, p_b, _, _ = _attn_scores(q_ref, k_ref, sink_ref, h, qb, ctx_rows, nb, hd, grp)
        o = _dot(p_c.astype(MXU), v_ref[0:ctx_rows, :]) + _dot(p_b.astype(MXU), v_ref[pl.ds(w0, 3 * BLK), :])
        o_ref[...] = jnp.concatenate([o[g * BLK:(g + 1) * BLK] for g in range(grp)], axis=1).astype(o_ref.dtype)

    return pl.pallas_call(
        body, name=name, grid=(n_kv, nb),
        in_specs=[pl.BlockSpec(memory_space=pltpu.SMEM), pl.BlockSpec((BLK, grp * hd), lambda h, i: (i, h)),
                  pl.BlockSpec((T, hd), lambda h, i: (0, h)), pl.BlockSpec((T, hd), lambda h, i: (0, h))],
        out_specs=pl.BlockSpec((BLK, grp * hd), lambda h, i: (i, h)),
        out_shape=jax.ShapeDtypeStruct((T, n_q * hd), MXU),
        compiler_params=_cparams(("parallel", "arbitrary"), 4 * _nbytes((T, hd), MXU) + 24 * _nbytes((grp * BLK, 5 * BLK), F32)),
    )(sinks, qr, kr, vb)


def _attn_bwd(qr, kr, vb, sinks, do, n_q, ctx_rows, name):
    T = qr.shape[0]
    n_kv = ATTN_KV_HEADS
    grp = n_q // n_kv
    hd = qr.shape[1] // n_q
    nb = T // BLK
    scale = hd ** -0.5

    def body(sink_ref, q_ref, k_ref, v_ref, do_ref, dq_ref, dk_ref, dv_ref, ds_ref):
        h, qb = pl.program_id(0), pl.program_id(1)
        qs, kc, kb, w0, p_c, p_b, p_s, gi = _attn_scores(q_ref, k_ref, sink_ref, h, qb, ctx_rows, nb, hd, grp)
        dov = do_ref[...]
        dos = jnp.concatenate([dov[:, g * hd:(g + 1) * hd] for g in range(grp)], axis=0)
        vc = v_ref[0:ctx_rows, :]
        vw = v_ref[pl.ds(w0, 3 * BLK), :]
        dp_c = _dot(dos, vc, NT)
        dp_b = _dot(dos, vw, NT)
        delta = jnp.sum(p_c * dp_c, axis=1, keepdims=True) + jnp.sum(p_b * dp_b, axis=1, keepdims=True)
        ds_c = (p_c * (dp_c - delta) * scale).astype(MXU)
        ds_b = (p_b * (dp_b - delta) * scale).astype(MXU)
        dq = _dot(ds_c, kc) + _dot(ds_b, kb)
        dq_ref[...] = jnp.concatenate([dq[g * BLK:(g + 1) * BLK] for g in range(grp)], axis=1)

        @pl.when(qb == 0)
        def _():
            dk_ref[...] = jnp.zeros(dk_ref.shape, F32)
            dv_ref[...] = jnp.zeros(dv_ref.shape, F32)

        dk_ref[0:ctx_rows, :] += _dot(ds_c, qs, TN)
        dv_ref[0:ctx_rows, :] += _dot(p_c.astype(MXU), dos, TN)
        dk_ref[pl.ds(w0, 3 * BLK), :] += _dot(ds_b, qs, TN)
        dv_ref[pl.ds(w0, 3 * BLK), :] += _dot(p_b.astype(MXU), dos, TN)
        t = -(p_s * delta)
        lane = lax.broadcasted_iota(jnp.int32, (8, LANE), 1)
        part = jnp.zeros((8, LANE), F32)
        for g in range(grp):
            part = jnp.where(lane == g, jnp.sum(jnp.where(gi == g, t, 0.0)), part)
        _acc_first(ds_ref, part, qb)

    return pl.pallas_call(
        body, name=name, grid=(n_kv, nb),
        in_specs=[pl.BlockSpec(memory_space=pltpu.SMEM), pl.BlockSpec((BLK, grp * hd), lambda h, i: (i, h)),
                  pl.BlockSpec((T, hd), lambda h, i: (0, h)), pl.BlockSpec((T, hd), lambda h, i: (0, h)),
                  pl.BlockSpec((BLK, grp * hd), lambda h, i: (i, h))],
        out_specs=[pl.BlockSpec((BLK, grp * hd), lambda h, i: (i, h)), pl.BlockSpec((T, hd), lambda h, i: (0, h)),
                   pl.BlockSpec((T, hd), lambda h, i: (0, h)), pl.BlockSpec((8, LANE), lambda h, i: (h, 0))],
        out_shape=[jax.ShapeDtypeStruct((T, n_q * hd), F32), jax.ShapeDtypeStruct((T, n_kv * hd), F32),
                   jax.ShapeDtypeStruct((T, n_kv * hd), F32), jax.ShapeDtypeStruct((n_kv * 8, LANE), F32)],
        compiler_params=_cparams(("parallel", "arbitrary"), 4 * _nbytes((T, hd), MXU) + 4 * _nbytes((T, hd), F32)
                                 + 40 * _nbytes((grp * BLK, 5 * BLK), F32)),
    )(sinks, qr, kr, vb, do)


def _chunk_order(s, n_chunks, n_ctx, rev):
    if not rev:
        return s
    return jnp.where(s < n_ctx, n_ctx - 1 - s, n_chunks - 1 + n_ctx - s)


def _softplus(x):
    return jnp.maximum(x, 0.0) + jnp.log(1.0 + jnp.exp(-jnp.abs(x)))


def _expand_matrix(n_heads, p, rev):
    e = np.zeros((LANE, n_heads * p), np.float32)
    for h in range(n_heads):
        e[h + (n_heads if rev else 0), h * p:(h + 1) * p] = 1.0
    return jnp.asarray(e, BF16)


def _ssd_chunk_prep(dt_ref, dtb_ref, alog_ref, e_ref, rev):
    dt = _softplus(dt_ref[...] + dtb_ref[...])
    a = -jnp.exp(alog_ref[...])
    li = lax.broadcasted_iota(jnp.int32, (BLK, BLK), 0)
    si = lax.broadcasted_iota(jnp.int32, (BLK, BLK), 1)
    tri = (si >= li) if rev else (si <= li)
    acs = _dot3_rhs(tri.astype(BF16), a * dt)
    ev = e_ref[...]
    return dt, a, tri, acs, _dot2_lhs(dt, ev), _dot2_lhs(acs, ev)


def _pair_cols(ap):
    lane = lax.broadcasted_iota(jnp.int32, ap.shape, 1)
    apr = pltpu.roll(ap, LANE // 2, 1)
    return jnp.where(lane < LANE // 2, ap, apr), jnp.where(lane < LANE // 2, apr, ap)


def _ssd_scan(xbc, zx, dtb, alog, emat, n_heads, n_ctx, rev, name):
    T, C = xbc.shape
    P = emat.shape[1] // n_heads
    DI = n_heads * P
    GN = (C - DI) // 2
    N = GN // SSD_GROUPS
    n_pairs = DI // LANE
    ppg = n_pairs // SSD_GROUPS
    n_chunks = T // BLK
    hoff = n_heads if rev else 0
    last = 0 if rev else BLK - 1
    dt_blk = zx.shape[1] // LANE - 1
    assert N == LANE and 2 * P == LANE and 2 * n_heads == LANE

    def body(xs_ref, b_ref, c_ref, dt_ref, dtb_ref, alog_ref, e_ref, y_ref, hin_ref, state_ref, xdt_s, xdec_s, aexp_s, at_s):
        s = pl.program_id(0)

        @pl.when(s == 0)
        def _():
            state_ref[...] = jnp.zeros(state_ref.shape, F32)

        dt, a, tri, acs, dtexp, aexp = _ssd_chunk_prep(dt_ref, dtb_ref, alog_ref, e_ref, rev)
        at_s[...] = acs.T
        aexp_s[...] = aexp
        xdt = xs_ref[...] * dtexp
        xdt_s[...] = xdt.astype(MXU)
        xdec_s[...] = (xdt * jnp.exp(aexp[last:last + 1, :] - aexp)).astype(MXU)
        hin_ref[0] = state_ref[...]
        lane = lax.broadcasted_iota(jnp.int32, (BLK, LANE), 1)

        def pair(k, carry):
            col = pl.multiple_of(k * LANE, LANE)
            gcol = pl.multiple_of((k // ppg) * N, N)
            bg = b_ref[:, pl.ds(gcol, N)].astype(MXU)
            cg = c_ref[:, pl.ds(gcol, N)].astype(MXU)
            cb = _dot(cg, bg, NT)
            ap = aexp_s[:, pl.ds(col, LANE)]
            ac0, ac1 = _pair_cols(ap)
            ar0 = at_s[pl.ds(2 * k + hoff, 1), :]
            ar1 = at_s[pl.ds(2 * k + 1 + hoff, 1), :]
            m0 = (cb * jnp.exp(jnp.where(tri, ac0 - ar0, -jnp.inf))).astype(MXU)
            m1 = (cb * jnp.exp(jnp.where(tri, ac1 - ar1, -jnp.inf))).astype(MXU)
            xp = xdt_s[:, pl.ds(col, LANE)]
            zero = jnp.zeros_like(xp)
            xbd = jnp.concatenate([jnp.where(lane < LANE // 2, xp, zero), jnp.where(lane >= LANE // 2, xp, zero)], axis=0)
            yd = _dot(jnp.concatenate([m0, m1], axis=1), xbd)
            ht = state_ref[k]
            yo = _dot(cg, ht.astype(MXU)) * jnp.exp(ap)
            y_ref[:, pl.ds(col, LANE)] = yd + yo
            st = _dot(bg, xdec_s[:, pl.ds(col, LANE)], TN)
            state_ref[k] = jnp.exp(aexp_s[pl.ds(last, 1), pl.ds(col, LANE)]) * ht + st
            return carry

        lax.fori_loop(0, n_pairs, pair, 0, unroll=2)

    order = lambda s: _chunk_order(s, n_chunks, n_ctx, rev)
    return pl.pallas_call(
        body, name=name, grid=(n_chunks,),
        in_specs=[pl.BlockSpec((BLK, DI), lambda s: (order(s), 0)),
                  pl.BlockSpec((BLK, GN), lambda s: (order(s), DI // GN)),
                  pl.BlockSpec((BLK, GN), lambda s: (order(s), DI // GN + 1)),
                  pl.BlockSpec((BLK, LANE), lambda s: (order(s), dt_blk)),
                  pl.BlockSpec((1, LANE), lambda s: (0, 0)), pl.BlockSpec((1, LANE), lambda s: (0, 0)),
                  pl.BlockSpec((LANE, DI), lambda s: (0, 0))],
        out_specs=[pl.BlockSpec((BLK, DI), lambda s: (order(s), 0)),
                   pl.BlockSpec((1, n_pairs, N, LANE), lambda s: (order(s), 0, 0, 0))],
        out_shape=[jax.ShapeDtypeStruct((T, DI), F32), jax.ShapeDtypeStruct((n_chunks, n_pairs, N, LANE), F32)],
        scratch_shapes=[pltpu.VMEM((n_pairs, N, LANE), F32), pltpu.VMEM((BLK, DI), MXU), pltpu.VMEM((BLK, DI), MXU),
                        pltpu.VMEM((BLK, DI), F32), pltpu.VMEM((LANE, BLK), F32)],
        compiler_params=_cparams(("arbitrary",), 20 * _nbytes((BLK, DI), F32)),
    )(xbc, xbc, xbc, zx, dtb, alog, emat)


def _ssd_scan_bwd(xbc, zx, dtb, alog, emat, emat_t, dexp, hin, dy, acc, n_heads, n_ctx, rev, name):
    T, C = xbc.shape
    P = emat.shape[1] // n_heads
    DI = n_heads * P
    GN = (C - DI) // 2
    N = GN // SSD_GROUPS
    n_pairs = DI // LANE
    ppg = n_pairs // SSD_GROUPS
    n_chunks = T // BLK
    hoff = n_heads if rev else 0
    last = 0 if rev else BLK - 1
    dt_blk = zx.shape[1] // LANE - 1
    has_acc = acc is not None

    def body(*refs):
        (xs_ref, b_ref, c_ref, dt_ref, dtb_ref, alog_ref, e_ref, et_ref, dexp_ref, hin_ref, dy_ref) = refs[:11]
        n_in = 11
        if has_acc:
            dxbc_in, ddt_in, sums_in = refs[11:14]
            n_in = 14
        dxbc_ref, ddt_ref, sums_ref = refs[n_in:n_in + 3]
        dstate_ref, xdt_s, xdec_s, aexp_s, at_s, dyd_s, z1_s, z3_s, dxdt_s, cdrow_s, rmat_s, cst_s = refs[n_in + 3:]
        s = pl.program_id(0)

        @pl.when(s == 0)
        def _():
            dstate_ref[...] = jnp.zeros(dstate_ref.shape, F32)

        dt, a, tri, acs, dtexp, aexp = _ssd_chunk_prep(dt_ref, dtb_ref, alog_ref, e_ref, rev)
        at_s[...] = acs.T
        aexp_s[...] = aexp
        xsv = xs_ref[...]
        xdt = xsv * dtexp
        xdt_s[...] = xdt.astype(MXU)
        decend = jnp.exp(aexp[last:last + 1, :] - aexp)
        xdec_s[...] = (xdt * decend).astype(MXU)
        dyv = dy_ref[...]
        dyd_s[...] = (dyv * jnp.exp(aexp)).astype(MXU)
        dxbc_ref[:, DI:] = jnp.zeros((BLK, 2 * GN), F32)
        rmat_s[...] = jnp.zeros(rmat_s.shape, F32)
        lane = lax.broadcasted_iota(jnp.int32, (BLK, LANE), 1)
        lo = lane < LANE // 2
        tri_t = (lax.broadcasted_iota(jnp.int32, (BLK, BLK), 0) <= lax.broadcasted_iota(jnp.int32, (BLK, BLK), 1)) if not rev \
            else (lax.broadcasted_iota(jnp.int32, (BLK, BLK), 0) >= lax.broadcasted_iota(jnp.int32, (BLK, BLK), 1))

        def pair(k, carry):
            col = pl.multiple_of(k * LANE, LANE)
            gcol = pl.multiple_of((k // ppg) * N, N)
            bg = b_ref[:, pl.ds(gcol, N)].astype(MXU)
            cg = c_ref[:, pl.ds(gcol, N)].astype(MXU)
            cb = _dot(cg, bg, NT)
            cbt = _dot(bg, cg, NT)
            ap = aexp_s[:, pl.ds(col, LANE)]
            ac0, ac1 = _pair_cols(ap)
            ar0 = at_s[pl.ds(2 * k + hoff, 1), :]
            ar1 = at_s[pl.ds(2 * k + 1 + hoff, 1), :]
            seg0 = jnp.exp(jnp.where(tri, ac0 - ar0, -jnp.inf))
            seg1 = jnp.exp(jnp.where(tri, ac1 - ar1, -jnp.inf))
            segt0 = jnp.exp(jnp.where(tri_t, ar0 - ac0, -jnp.inf))
            segt1 = jnp.exp(jnp.where(tri_t, ar1 - ac1, -jnp.inf))
            dyp = dy_ref[:, pl.ds(col, LANE)].astype(MXU)
            zero = jnp.zeros_like(dyp)
            dy0, dy1 = jnp.where(lo, dyp, zero), jnp.where(lo, zero, dyp)
            dht = dstate_ref[k]
            dhb = dht.astype(MXU)
            ht = hin_ref[0, k]
            mt = jnp.concatenate([(cbt * segt0).astype(MXU), (cbt * segt1).astype(MXU)], axis=1)
            bdh = _dot(bg, dhb)
            dec_p = jnp.exp(aexp_s[pl.ds(last, 1), pl.ds(col, LANE)] - ap)
            dxdt_s[:, pl.ds(col, LANE)] = _dot(mt, jnp.concatenate([dy0, dy1], axis=0)) + dec_p * bdh
            z3_s[:, pl.ds(col, LANE)] = bdh
            cdec = jnp.exp(aexp_s[pl.ds(last, 1), pl.ds(col, LANE)])
            dydp = dyd_s[:, pl.ds(col, LANE)]
            dstate_ref[k] = _dot(cg, dydp, TN) + cdec * dht
            cdrow_s[0:1, pl.ds(col, LANE)] = cdec * jnp.sum(dht * ht, axis=0, keepdims=True)
            z1_s[:, pl.ds(col, LANE)] = _dot(cg, ht.astype(MXU))
            dcg = _dot(dydp, ht.astype(MXU), NT)
            dbg = _dot(xdec_s[:, pl.ds(col, LANE)], dhb, NT)
            xp = xdt_s[:, pl.ds(col, LANE)]
            dg0 = _dot(dy0, xp, NT)
            dg1 = _dot(dy1, xp, NT)
            ds0, ds1 = dg0 * seg0, dg1 * seg1
            w0, w1 = ds0 * cb, ds1 * cb
            dcb = ds0 + ds1
            lane_h = lax.broadcasted_iota(jnp.int32, (BLK, LANE), 1)
            rmat_s[...] += (jnp.where(lane_h == 2 * k + hoff, jnp.sum(w0, axis=1, keepdims=True), 0.0)
                            + jnp.where(lane_h == 2 * k + 1 + hoff, jnp.sum(w1, axis=1, keepdims=True), 0.0))
            cst_s[pl.ds(2 * k + hoff, 1), :] = jnp.sum(w0, axis=0, keepdims=True)
            cst_s[pl.ds(2 * k + 1 + hoff, 1), :] = jnp.sum(w1, axis=0, keepdims=True)
            dcbb = dcb.astype(MXU)
            dxbc_ref[:, pl.ds(DI + GN + gcol, N)] += dcg + _dot(dcbb, bg)
            dxbc_ref[:, pl.ds(DI + gcol, N)] += dbg + _dot(dcbb, cg, TN)
            return carry

        cst_s[...] = jnp.zeros(cst_s.shape, F32)
        lax.fori_loop(0, n_pairs, pair, 0, unroll=2)

        etv = et_ref[...]
        dxdt = dxdt_s[...]
        z1 = _dot2_lhs(dyv * (z1_s[...] * jnp.exp(aexp)), etv)
        z2 = _dot2_lhs(dxdt * xsv, etv)
        q = _dot2_lhs(xdt * decend * z3_s[...], etv)
        cd8 = jnp.concatenate([cdrow_s[0:1, :], jnp.zeros((7, DI), F32)], axis=0)
        cdh = _dot2_lhs(cd8, etv)[0:1, :]
        dacs = rmat_s[...] - cst_s[...].T + z1 - q
        rowi = lax.broadcasted_iota(jnp.int32, (BLK, LANE), 0)
        dacs = dacs + jnp.where(rowi == last, jnp.sum(q, axis=0, keepdims=True) + cdh, 0.0)
        d_a = _dot3_rhs(tri_t.astype(BF16), dacs)
        ddt = a * d_a + z2
        x_raw = dt_ref[...] + dtb_ref[...]
        ddt_raw = ddt * jax.nn.sigmoid(x_raw)
        lane_l = lax.broadcasted_iota(jnp.int32, (BLK, LANE), 1)
        mine = (lane_l >= hoff) & (lane_l < hoff + n_heads)
        ddt_raw = jnp.where(mine, ddt_raw, 0.0)
        part = jnp.concatenate([jnp.sum(jnp.where(mine, dt * d_a, 0.0), axis=0, keepdims=True) * a,
                                jnp.sum(ddt_raw, axis=0, keepdims=True), jnp.zeros((6, LANE), F32)], axis=0)
        dxs = dxdt * dtexp
        if has_acc:
            dxbc_ref[:, 0:DI] = dxs + dxbc_in[:, 0:DI]
            dxbc_ref[:, DI:] += dxbc_in[:, DI:]
            ddt_ref[...] = ddt_raw + ddt_in[...]
            part = part + jnp.where(s == 0, sums_in[...], 0.0)
        else:
            dxbc_ref[:, 0:DI] = dxs + dyv * dexp_ref[...]
            ddt_ref[...] = ddt_raw
        _acc_first(sums_ref, part, s)

    order = lambda s: _chunk_order(n_chunks - 1 - s, n_chunks, n_ctx, rev)
    in_specs = [pl.BlockSpec((BLK, DI), lambda s: (order(s), 0)),
                pl.BlockSpec((BLK, GN), lambda s: (order(s), DI // GN)),
                pl.BlockSpec((BLK, GN), lambda s: (order(s), DI // GN + 1)),
                pl.BlockSpec((BLK, LANE), lambda s: (order(s), dt_blk)),
                pl.BlockSpec((1, LANE), lambda s: (0, 0)), pl.BlockSpec((1, LANE), lambda s: (0, 0)),
                pl.BlockSpec((LANE, DI), lambda s: (0, 0)), pl.BlockSpec((DI, LANE), lambda s: (0, 0)),
                pl.BlockSpec((1, DI), lambda s: (0, 0)),
                pl.BlockSpec((1, n_pairs, N, LANE), lambda s: (order(s), 0, 0, 0)),
                pl.BlockSpec((BLK, DI), lambda s: (order(s), 0))]
    args = [xbc, xbc, xbc, zx, dtb, alog, emat, emat_t, dexp, hin, dy]
    if has_acc:
        in_specs += [pl.BlockSpec((BLK, C), lambda s: (order(s), 0)), pl.BlockSpec((BLK, LANE), lambda s: (order(s), 0)),
                     pl.BlockSpec((8, LANE), lambda s: (0, 0))]
        args += list(acc)
    return pl.pallas_call(
        body, name=name, grid=(n_chunks,),
        in_specs=in_specs,
        out_specs=[pl.BlockSpec((BLK, C), lambda s: (order(s), 0)), pl.BlockSpec((BLK, LANE), lambda s: (order(s), 0)),
                   pl.BlockSpec((8, LANE), lambda s: (0, 0))],
        out_shape=[jax.ShapeDtypeStruct((T, C), F32), jax.ShapeDtypeStruct((T, LANE), F32), jax.ShapeDtypeStruct((8, LANE), F32)],
        scratch_shapes=[pltpu.VMEM((n_pairs, N, LANE), F32), pltpu.VMEM((BLK, DI), MXU), pltpu.VMEM((BLK, DI), MXU),
                        pltpu.VMEM((BLK, DI), F32), pltpu.VMEM((LANE, BLK), F32), pltpu.VMEM((BLK, DI), MXU),
                        pltpu.VMEM((BLK, DI), F32), pltpu.VMEM((BLK, DI), F32), pltpu.VMEM((BLK, DI), F32),
                        pltpu.VMEM((8, DI), F32), pltpu.VMEM((BLK, LANE), F32), pltpu.VMEM((LANE, BLK), F32)],
        compiler_params=_cparams(("arbitrary",), 36 * _nbytes((BLK, DI), F32)),
    )(*args)


def _ssd_finish(yf, yb, xbc, zx, dexp, nw, name, tr=256):
    T, DI = yf.shape
    tr = min(tr, T)

    def body(yf_ref, yb_ref, xs_ref, z_ref, d_ref, nw_ref, o_ref):
        y = yf_ref[...] + yb_ref[...] + xs_ref[...] * d_ref[...]
        gt = y * _silu(z_ref[...])
        r = lax.rsqrt(jnp.mean(gt * gt, axis=-1, keepdims=True) + NORM_EPS)
        o_ref[...] = ((gt * r) * nw_ref[...]).astype(o_ref.dtype)

    row = pl.BlockSpec((tr, DI), lambda i: (i, 0))
    vec = pl.BlockSpec((1, DI), lambda i: (0, 0))
    return pl.pallas_call(
        body, name=name, grid=(T // tr,), in_specs=[row, row, row, row, vec, vec], out_specs=row,
        out_shape=jax.ShapeDtypeStruct((T, DI), MXU),
        compiler_params=_cparams(("parallel",), 16 * _nbytes((tr, DI), F32)),
    )(yf, yb, xbc, zx, dexp, nw)


def _ssd_finish_bwd(yf, yb, xbc, zx, dexp, nw, do, emat_t, dzx_shape, name, tr=64):
    T, DI = yf.shape
    tr = min(tr, T)
    n_steps = T // tr

    def body(yf_ref, yb_ref, xs_ref, z_ref, d_ref, nw_ref, do_ref, et_ref, dy_ref, dz_ref, sums_ref, dd_ref):
        i = pl.program_id(0)
        xs = xs_ref[...]
        zv = z_ref[...]
        y = yf_ref[...] + yb_ref[...] + xs * d_ref[...]
        sz = _silu(zv)
        gt = y * sz
        r = lax.rsqrt(jnp.mean(gt * gt, axis=-1, keepdims=True) + NORM_EPS)
        gn = gt * r
        dov = do_ref[...].astype(F32)
        dgn = dov * nw_ref[...]
        dgt = r * (dgn - gn * jnp.mean(dgn * gn, axis=-1, keepdims=True))
        dy = dgt * sz
        dy_ref[...] = dy
        dz_ref[...] = (dgt * y * _dsilu(zv)).astype(dz_ref.dtype)
        part = jnp.concatenate([jnp.sum(dov * gn, axis=0, keepdims=True), jnp.sum(dy * xs, axis=0, keepdims=True),
                                jnp.zeros((6, DI), F32)], axis=0)
        _acc_first(sums_ref, part, i)

        @pl.when(i == n_steps - 1)
        def _():
            dd_ref[...] = _dot3_lhs(sums_ref[...], et_ref[...])

    row = pl.BlockSpec((tr, DI), lambda i: (i, 0))
    vec = pl.BlockSpec((1, DI), lambda i: (0, 0))
    return pl.pallas_call(
        body, name=name, grid=(n_steps,),
        in_specs=[row, row, row, row, vec, vec, row, pl.BlockSpec((DI, LANE), lambda i: (0, 0))],
        out_specs=[row, row, pl.BlockSpec((8, DI), lambda i: (0, 0)), pl.BlockSpec((8, LANE), lambda i: (0, 0))],
        out_shape=[jax.ShapeDtypeStruct((T, DI), F32), jax.ShapeDtypeStruct(dzx_shape, MXU), jax.ShapeDtypeStruct((8, DI), F32),
                   jax.ShapeDtypeStruct((8, LANE), F32)],
        compiler_params=_cparams(("arbitrary",), 40 * _nbytes((tr, DI), F32)),
    )(yf, yb, xbc, zx, dexp, nw, do, emat_t)


def _loss_head(xf, tgt, ctx_rows, name, tr=256):
    T, D = xf.shape
    tr = min(tr, ctx_rows)
    n_ctx = ctx_rows // tr

    def body(x_ref, t_ref, dx_ref, l_ref):
        i = pl.program_id(0)

        @pl.when(i < n_ctx)
        def _():
            dx_ref[...] = jnp.zeros(dx_ref.shape, F32)

        @pl.when(i == 0)
        def _():
            l_ref[...] = jnp.zeros(l_ref.shape, F32)

        @pl.when(i >= n_ctx)
        def _():
            e = x_ref[...] - t_ref[...]
            dx_ref[...] = e * (1.0 / D)
            l_ref[...] += 0.5 * jnp.sum(jnp.mean(e * e, axis=-1, keepdims=True))

    return pl.pallas_call(
        body, name=name, grid=(T // tr,),
        in_specs=[pl.BlockSpec((tr, D), lambda i: (i, 0)), pl.BlockSpec((tr, D), lambda i: (jnp.maximum(i - n_ctx, 0), 0))],
        out_specs=[pl.BlockSpec((tr, D), lambda i: (i, 0)), pl.BlockSpec((8, LANE), lambda i: (0, 0))],
        out_shape=[jax.ShapeDtypeStruct((T, D), F32), jax.ShapeDtypeStruct((8, LANE), F32)],
        compiler_params=_cparams(("arbitrary",), 10 * _nbytes((tr, D), F32)),
    )(xf, tgt)


def _adamw_math(w, g, m, v):
    m2 = ADAM_B1 * m + (1.0 - ADAM_B1) * g
    v2 = ADAM_B2 * v + (1.0 - ADAM_B2) * (g * g)
    m_hat = m2 / (1.0 - ADAM_B1 ** ADAM_STEP)
    v_hat = v2 / (1.0 - ADAM_B2 ** ADAM_STEP)
    delta = -ADAM_LR * (m_hat / (jnp.sqrt(v_hat) + ADAM_EPS) + ADAM_WD * w)
    return delta, m2, v2


def _row_tile(rows, target):
    if rows <= target:
        return rows
    t = target - target % 8
    while rows % t:
        t -= 8
    return t


def _adamw(parts, w, m, v, name, tr=128):
    n, L, rows, cols = parts.shape
    tr = _row_tile(rows, tr)

    def body(p_ref, w_ref, m_ref, v_ref, g_ref, d_ref, m2_ref, v2_ref):
        g = p_ref[0, 0].astype(F32)
        for q in range(1, n):
            g = g + p_ref[q, 0].astype(F32)
        d, m2, v2 = _adamw_math(w_ref[0], g, m_ref[0], v_ref[0])
        g_ref[0], d_ref[0], m2_ref[0], v2_ref[0] = g, d, m2, v2

    blk = pl.BlockSpec((1, tr, cols), lambda l, i: (l, i, 0))
    shp = jax.ShapeDtypeStruct((L, rows, cols), F32)
    return pl.pallas_call(
        body, name=name, grid=(L, rows // tr),
        in_specs=[pl.BlockSpec((n, 1, tr, cols), lambda l, i: (0, l, i, 0)), blk, blk, blk],
        out_specs=[blk, blk, blk, blk], out_shape=[shp, shp, shp, shp],
        compiler_params=_cparams(("parallel", "parallel"), 2 * (n + 8) * _nbytes((tr, cols), F32)),
    )(parts, w, m, v)


def _adamw_small(bufs, w, m, v, name):
    n, R, _ = bufs.shape

    def body(b_ref, w_ref, m_ref, v_ref, g_ref, d_ref, m2_ref, v2_ref):
        g = b_ref[0]
        for q in range(1, n):
            g = g + b_ref[q]
        d, m2, v2 = _adamw_math(w_ref[...], g, m_ref[...], v_ref[...])
        g_ref[...], d_ref[...], m2_ref[...], v2_ref[...] = g, d, m2, v2

    vm = pl.BlockSpec(memory_space=pltpu.VMEM)
    shp = jax.ShapeDtypeStruct((R, LANE), F32)
    return pl.pallas_call(body, name=name, in_specs=[vm, vm, vm, vm], out_specs=[vm, vm, vm, vm],
                          out_shape=[shp, shp, shp, shp],
                          compiler_params=pltpu.CompilerParams(vmem_limit_bytes=32 * 1024 * 1024))(bufs, w, m, v)


def _add_blocks(g, r1, c_idx, name, tr=128):
    _, rows, cols = g.shape
    tr = _row_tile(rows, tr)

    def body(c_ref, g_ref, r_ref, p_ref):
        del c_ref
        p_ref[...] = (g_ref[...] + r_ref[...]).astype(p_ref.dtype)

    return pl.pallas_call(
        body, name=name,
        grid_spec=pltpu.PrefetchScalarGridSpec(
            num_scalar_prefetch=1, grid=(NDEV // 2, rows // tr),
            in_specs=[pl.BlockSpec((1, tr, cols), lambda q, i, c: (2 * q + c[0], i, 0)),
                      pl.BlockSpec((1, tr, cols), lambda q, i, c: (q, i, 0))],
            out_specs=pl.BlockSpec((1, tr, cols), lambda q, i, c: (q, i, 0))),
        out_shape=jax.ShapeDtypeStruct((NDEV // 2, rows, cols), XFER),
        compiler_params=_cparams(("parallel", "parallel"), 8 * _nbytes((tr, cols), F32)),
    )(c_idx, g, r1)


def _me():
    return lax.axis_index("x"), lax.axis_index("y"), lax.axis_index("c")


def _flip(v, bit):
    return 1 - v if bit else v


def _peer(k):
    x, y, c = _me()
    return _flip(x, (k >> 2) & 1), _flip(y, (k >> 1) & 1), _flip(c, k & 1)


def _dev_index(p):
    return 4 * p[0] + 2 * p[1] + p[2]


def _chip_index(p):
    return 2 * p[0] + p[1]


def _small_allgather(v, name):
    R, C = v.shape

    def body(v_ref, out_ref, send_sems, recv_sems, loc_sem):
        me = _dev_index(_me())
        mine = pltpu.make_async_copy(v_ref, out_ref.at[me], loc_sem)
        mine.start()
        sends = []
        for k in range(1, NDEV):
            cp = pltpu.make_async_remote_copy(src_ref=v_ref, dst_ref=out_ref.at[me], send_sem=send_sems.at[k - 1],
                                              recv_sem=recv_sems.at[k - 1], device_id=_peer(k), device_id_type=MESH_ID)
            cp.start()
            sends.append(cp)
        for k in range(1, NDEV):
            pltpu.make_async_remote_copy(src_ref=v_ref, dst_ref=out_ref.at[_dev_index(_peer(k))], send_sem=send_sems.at[k - 1],
                                         recv_sem=recv_sems.at[k - 1], device_id=_peer(k), device_id_type=MESH_ID).wait_recv()
        for cp in sends:
            cp.wait_send()
        mine.wait()

    vm = pl.BlockSpec(memory_space=pltpu.VMEM)
    return pl.pallas_call(
        body, name=name, in_specs=[vm], out_specs=vm, out_shape=jax.ShapeDtypeStruct((NDEV, R, C), F32),
        scratch_shapes=[pltpu.SemaphoreType.DMA((NDEV - 1,)), pltpu.SemaphoreType.DMA((NDEV - 1,)), pltpu.SemaphoreType.DMA(())],
        compiler_params=pltpu.CompilerParams(vmem_limit_bytes=48 * 1024 * 1024),
    )(v)


def _allgather_weights(arrs, name):
    n = len(arrs)

    def body(*refs):
        x_refs, out_refs = refs[:n], refs[n:2 * n]
        send_sems, recv_sems, loc_sems = refs[2 * n:]
        x, y, c = _me()
        me, sib = (x, y, c), (x, y, 1 - c)
        chips = [(1 - x, y), (x, 1 - y), (1 - x, 1 - y)]

        def copy(a, k, block, to, src=None):
            dst = out_refs[a].at[_dev_index(block)]
            return pltpu.make_async_remote_copy(src_ref=dst if src is None else src, dst_ref=dst, send_sem=send_sems.at[a, k],
                                                recv_sem=recv_sems.at[a, k], device_id=to, device_id_type=MESH_ID)

        locs, first, passed = [], [], []
        for a in range(n):
            lc = pltpu.make_async_copy(x_refs[a], out_refs[a].at[_dev_index(me)], loc_sems.at[a])
            lc.start()
            locs.append(lc)
            cps = [copy(a, 0, me, sib, src=x_refs[a])] + [copy(a, 1 + j, me, (*chip, c), src=x_refs[a]) for j, chip in enumerate(chips)]
            for cp in cps:
                cp.start()
            first += cps
        for j, chip in enumerate(chips):
            for a in range(n):
                copy(a, 1 + j, (*chip, c), me).wait_recv()
                cp = copy(a, 4 + j, (*chip, c), sib)
                cp.start()
                passed.append(cp)
        for a in range(n):
            copy(a, 0, sib, me).wait_recv()
            for j, chip in enumerate(chips):
                copy(a, 4 + j, (*chip, 1 - c), me).wait_recv()
        for cp in first + passed:
            cp.wait_send()
        for lc in locs:
            lc.wait()

    hbm = pl.BlockSpec(memory_space=pl.ANY)
    return pl.pallas_call(
        body, name=name, in_specs=[hbm] * n, out_specs=[hbm] * n,
        out_shape=[jax.ShapeDtypeStruct((NDEV,) + a.shape, a.dtype) for a in arrs],
        scratch_shapes=[pltpu.SemaphoreType.DMA((n, 7)), pltpu.SemaphoreType.DMA((n, 7)), pltpu.SemaphoreType.DMA((n,))],
    )(*arrs)


def _rs_sibling(gs, name):
    n = len(gs)

    def body(*refs):
        g_refs, r_refs = refs[:n], refs[n:2 * n]
        send_sems, recv_sems = refs[2 * n:]
        x, y, c = _me()
        sib = (x, y, 1 - c)
        sends = []
        for a in range(n):
            for q in range(NDEV // 2):
                cp = pltpu.make_async_remote_copy(src_ref=g_refs[a].at[2 * q + (1 - c)], dst_ref=r_refs[a].at[q],
                                                  send_sem=send_sems.at[a, q], recv_sem=recv_sems.at[a, q],
                                                  device_id=sib, device_id_type=MESH_ID)
                cp.start()
                sends.append(cp)
        for cp in sends:
            cp.wait_recv()
        for cp in sends:
            cp.wait_send()

    hbm = pl.BlockSpec(memory_space=pl.ANY)
    return pl.pallas_call(
        body, name=name, in_specs=[hbm] * n, out_specs=[hbm] * n,
        out_shape=[jax.ShapeDtypeStruct((NDEV // 2,) + g.shape[1:], g.dtype) for g in gs],
        scratch_shapes=[pltpu.SemaphoreType.DMA((n, NDEV // 2)), pltpu.SemaphoreType.DMA((n, NDEV // 2))],
    )(*gs)


def _rs_chips(ps, groups, name):
    n = len(ps)
    where = {}
    for gi, grp in enumerate(groups):
        for li, a in enumerate(grp):
            where[a] = (gi, li)
    ng = len(groups)

    def body(*refs):
        p_refs, r_refs = refs[:n], refs[n:n + ng]
        send_sems, recv_sems, loc_sems = refs[n + ng:]
        x, y, c = _me()
        mychip = _chip_index((x, y))
        chips = [(1 - x, y), (x, 1 - y), (1 - x, 1 - y)]
        sends, locs = [], []
        for a in range(n):
            gi, li = where[a]
            lc = pltpu.make_async_copy(p_refs[a].at[mychip], r_refs[gi].at[mychip, li], loc_sems.at[a])
            lc.start()
            locs.append(lc)
            for j, chip in enumerate(chips):
                cp = pltpu.make_async_remote_copy(src_ref=p_refs[a].at[_chip_index(chip)], dst_ref=r_refs[gi].at[mychip, li],
                                                  send_sem=send_sems.at[a, j], recv_sem=recv_sems.at[a, j],
                                                  device_id=(*chip, c), device_id_type=MESH_ID)
                cp.start()
                sends.append(cp)
        for a in range(n):
            gi, li = where[a]
            for j, chip in enumerate(chips):
                pltpu.make_async_remote_copy(src_ref=p_refs[a].at[mychip], dst_ref=r_refs[gi].at[_chip_index(chip), li],
                                             send_sem=send_sems.at[a, j], recv_sem=recv_sems.at[a, j],
                                             device_id=(*chip, c), device_id_type=MESH_ID).wait_recv()
        for cp in sends:
            cp.wait_send()
        for lc in locs:
            lc.wait()

    hbm = pl.BlockSpec(memory_space=pl.ANY)
    return pl.pallas_call(
        body, name=name, in_specs=[hbm] * n, out_specs=[hbm] * ng,
        out_shape=[jax.ShapeDtypeStruct((NDEV // 2, len(grp)) + ps[grp[0]].shape[1:], ps[grp[0]].dtype) for grp in groups],
        scratch_shapes=[pltpu.SemaphoreType.DMA((n, 3)), pltpu.SemaphoreType.DMA((n, 3)), pltpu.SemaphoreType.DMA((n,))],
    )(*ps)


HI = lax.Precision.HIGHEST
MOD_ROWS = 16


def _col_tile(n, target=512):
    return target if n % target == 0 else n


def _modulation(s_in, ada_w, b_loc, name):
    L, D, nl = ada_w.shape
    tn = _col_tile(nl)

    def body(s_ref, w_ref, b_ref, o_ref):
        o_ref[0] = jnp.dot(_silu(s_ref[...]), w_ref[0], preferred_element_type=F32, precision=HI) + b_ref[0]

    return pl.pallas_call(
        body, name=name, grid=(L, nl // tn),
        in_specs=[pl.BlockSpec((MOD_ROWS, D), lambda l, j: (0, 0)), pl.BlockSpec((1, D, tn), lambda l, j: (l, 0, j)),
                  pl.BlockSpec((1, 1, tn), lambda l, j: (l, 0, j))],
        out_specs=pl.BlockSpec((1, MOD_ROWS, tn), lambda l, j: (l, 0, j)),
        out_shape=jax.ShapeDtypeStruct((L, MOD_ROWS, nl), F32),
        compiler_params=_cparams(("parallel", "parallel"), 4 * _nbytes((D, tn), F32)),
    )(s_in, ada_w, b_loc)


def _modulation_bwd(s_in, dml, dmc, ada_w, name):
    L, D, nl = ada_w.shape
    tn = _col_tile(nl)

    def body(s_ref, dml_ref, dmc_ref, w_ref, g_ref, pc_ref):
        l, j = pl.program_id(0), pl.program_id(1)
        a = _silu(s_ref[...])
        tot = dmc_ref[0, 0]
        for d in range(1, NDEV):
            tot = tot + dmc_ref[d, 0]
        row = lax.broadcasted_iota(jnp.int32, (MOD_ROWS, tn), 0)
        dm = jnp.where(row == NDEV, tot, dml_ref[:, 0, 0, :])
        g_ref[0] = lax.dot_general(a, dm, TN, preferred_element_type=F32, precision=HI)
        tot8 = jnp.where(lax.broadcasted_iota(jnp.int32, (8, tn), 0) == 0, tot, 0.0)
        part = lax.dot_general(tot8, w_ref[0], NT, preferred_element_type=F32, precision=HI)

        @pl.when((l == 0) & (j == 0))
        def _():
            pc_ref[...] = part

        @pl.when((l != 0) | (j != 0))
        def _():
            pc_ref[...] += part

    return pl.pallas_call(
        body, name=name, grid=(L, nl // tn),
        in_specs=[pl.BlockSpec((MOD_ROWS, D), lambda l, j: (0, 0)), pl.BlockSpec((MOD_ROWS, 1, 1, tn), lambda l, j: (0, l, 0, j)),
                  pl.BlockSpec((NDEV, 1, 1, tn), lambda l, j: (0, l, 0, j)), pl.BlockSpec((1, D, tn), lambda l, j: (l, 0, j))],
        out_specs=[pl.BlockSpec((1, D, tn), lambda l, j: (l, 0, j)), pl.BlockSpec((8, D), lambda l, j: (0, 0))],
        out_shape=[jax.ShapeDtypeStruct((L, D, nl), F32), jax.ShapeDtypeStruct((8, D), F32)],
        compiler_params=_cparams(("arbitrary", "arbitrary"), 8 * _nbytes((D, tn), F32)),
    )(s_in, dml.reshape(MOD_ROWS, L, 1, nl), dmc.reshape(NDEV, L, 1, nl), ada_w)


def _cctx_update(bufs, c_ctx, m, v, name):
    n, R, _ = bufs.shape

    def body(b_ref, w_ref, m_ref, v_ref, g_ref, d_ref, m2_ref, v2_ref):
        g = b_ref[0]
        for q in range(1, n):
            g = g + b_ref[q]
        g = g * _dsilu(w_ref[...])
        d, m2, v2 = _adamw_math(w_ref[...], g, m_ref[...], v_ref[...])
        g_ref[...], d_ref[...], m2_ref[...], v2_ref[...] = g, d, m2, v2

    vm = pl.BlockSpec(memory_space=pltpu.VMEM)
    shp = jax.ShapeDtypeStruct((R, LANE), F32)
    return pl.pallas_call(body, name=name, in_specs=[vm, vm, vm, vm], out_specs=[vm, vm, vm, vm],
                          out_shape=[shp, shp, shp, shp])(bufs, c_ctx, m, v)


def _pack(arrs):
    flat = jnp.concatenate([a.reshape(-1).astype(F32) for a in arrs])
    n = flat.shape[0]
    total = -(-n // (8 * LANE)) * (8 * LANE)
    return jnp.pad(flat, (0, total - n)).reshape(total // LANE, LANE)


def _unpack(buf, shapes):
    lead = buf.shape[:-2]
    flat = buf.reshape(lead + (-1,))
    out, off = [], 0
    for s in shapes:
        n = int(np.prod(s))
        out.append(flat[..., off:off + n].reshape(lead + tuple(s)))
        off += n
    return out


WEIGHTS = ['c_ctx', 'ada_w', 'ada_b', 'norm1_w', 'norm2_w', 'ssd_w_in', 'ssd_conv_w', 'ssd_conv_b', 'ssd_dt_bias_f',
           'ssd_dt_bias_b', 'ssd_a_log_f', 'ssd_a_log_b', 'ssd_d', 'ssd_norm_w', 'ssd_w_out', 'attn_w_qkv', 'attn_q_gain',
           'attn_k_gain', 'attn_sinks', 'attn_w_o', 'ffn_w_up', 'ffn_conv_w', 'ffn_conv_b', 'ffn_w_down']
SMALL = ['ada_b', 'norm1_w', 'norm2_w', 'ssd_conv_b', 'ssd_dt_bias_f', 'ssd_dt_bias_b', 'ssd_a_log_f', 'ssd_a_log_b', 'ssd_d',
         'ssd_norm_w', 'attn_q_gain', 'attn_k_gain', 'attn_sinks', 'ffn_conv_b']
BIG = ['ssd_w_in', 'ssd_w_out', 'attn_w_qkv', 'attn_w_o', 'ffn_w_up', 'ffn_w_down']


def _step(x, c, ctx, w, tgt, m, v):
    xi, yi, ci = _me()
    me = 4 * xi + 2 * yi + ci
    t_lat, D = x.shape[1], x.shape[2]
    ctx_rows = ctx.shape[1]
    T = ctx_rows + t_lat
    L, n_ssd, n_att = w['norm1_w'].shape[0], w['ssd_d'].shape[0], w['attn_sinks'].shape[0]
    H, DI, XBC = w['ssd_d'].shape[1], w['ssd_norm_w'].shape[1], w['ssd_conv_b'].shape[1]
    P = DI // H
    IN = w['ssd_w_in'].shape[2] * NDEV
    hd, n_q = w['attn_q_gain'].shape[1], w['attn_sinks'].shape[1]
    F2 = w['ffn_conv_b'].shape[1]
    G = F2 // NDEV
    nl = w['ada_w'].shape[2]
    ncc = ctx_rows // BLK
    perm = [_ffn_perm(s) for s in range(NDEV)]
    inv = [perm.index(d) for d in range(NDEV)]

    def reorder(t, order):
        return jnp.concatenate([t[..., o * G:(o + 1) * G] for o in order], axis=-1)

    def interleave(t):
        return reorder(t, perm)

    def deinterleave(t):
        return reorder(t, inv)

    shapes_a = [(D,), w['ssd_conv_w'].shape, w['ffn_conv_w'].shape]
    g_a = _small_allgather(_pack([c[0], w['ssd_conv_w'], w['ffn_conv_w']]), "gather_cond")
    c_all, scw_all, fcw_all = _unpack(g_a, shapes_a)
    ssd_cw = scw_all.transpose(1, 2, 0, 3).reshape(n_ssd, 3, XBC)
    ffn_cw = jnp.concatenate([fcw_all[d] for d in perm], axis=-1)
    ffn_cb = interleave(w['ffn_conv_b'])[:, None, :]

    s_in = jnp.concatenate([c_all, w['c_ctx'][None], jnp.zeros((MOD_ROWS - NDEV - 1, D), F32)], axis=0)
    b_loc = lax.dynamic_slice(w['ada_b'], (0, me * nl), (L, nl))[:, None, :]
    mod_loc = _modulation(s_in, w['ada_w'], b_loc, "modulation")
    g_b = _small_allgather(_pack([mod_loc]), "gather_mod")
    (mod_all,) = _unpack(g_b, [mod_loc.shape])
    mod_lat = lax.dynamic_index_in_dim(mod_all, me, axis=2, keepdims=False)
    mod_ctx = mod_all[:, :, NDEV, :]
    to_mod = lambda t: t.transpose(1, 0, 2).reshape(L, 6, D)
    mod = jnp.stack([to_mod(mod_ctx), to_mod(mod_lat)], axis=1)

    shards, slot = [], {}
    for name in BIG:
        for j in range(w[name].shape[0]):
            slot[(name, j)] = len(shards)
            shards.append(w[name][j].astype(MXU))
    gath = _allgather_weights(shards, "gather_weights")
    gw = lambda name, j: gath[slot[(name, j)]]
    w_in = [_cat_cols(gw('ssd_w_in', j), "ssd_in_cat") for j in range(n_ssd)]
    w_out = [gw('ssd_w_out', j).reshape(DI, D) for j in range(n_ssd)]
    w_qkv = [gw('attn_w_qkv', j) for j in range(n_att)]
    w_o = [gw('attn_w_o', j).reshape(n_q * hd, D) for j in range(n_att)]
    w_up = [gw('ffn_w_up', i) for i in range(L)]
    w_down = [gw('ffn_w_down', i).reshape(F2 // 2, D) for i in range(L)]

    cos, sin = _rope_tables(t_lat, ctx_rows, hd)
    e_f, e_b = _expand_matrix(H, P, False), _expand_matrix(H, P, True)
    et_f, et_b = e_f.T, e_b.T

    xs = jnp.concatenate([ctx[0], x[0]], axis=0)
    saved = []
    for i in range(L):
        j = i // 2
        s = dict(x0=xs)
        nw1, nw2 = w['norm1_w'][i][None], w['norm2_w'][i][None]
        s['h1'] = _normmod(xs, nw1, mod[i], 0, ctx_rows, "normmod")
        if i % 2 == 0:
            s['zx'] = _mm(s['h1'], w_in[j], tm=768, tn=1152, tk=2048, name="mm_ssd_in")
            s['cw'], s['cb'] = ssd_cw[j], w['ssd_conv_b'][j][None]
            s['xbc'] = _ssd_conv(s['zx'], s['cw'], s['cb'], DI, ctx_rows, "ssd_conv")
            s['dtb'] = jnp.concatenate([w['ssd_dt_bias_f'][j], w['ssd_dt_bias_b'][j]])[None]
            s['alog'] = jnp.concatenate([w['ssd_a_log_f'][j], w['ssd_a_log_b'][j]])[None]
            s['yf'], s['hin_f'] = _ssd_scan(s['xbc'], s['zx'], s['dtb'], s['alog'], e_f, H, ncc, False, "ssd_scan_f")
            s['yb'], s['hin_b'] = _ssd_scan(s['xbc'], s['zx'], s['dtb'], s['alog'], e_b, H, ncc, True, "ssd_scan_b")
            s['dexp'], s['snw'] = jnp.repeat(w['ssd_d'][j], P)[None], w['ssd_norm_w'][j][None]
            s['o'] = _ssd_finish(s['yf'], s['yb'], s['xbc'], s['zx'], s['dexp'], s['snw'], "ssd_finish")
            s['mix'], x1 = _mm(s['o'], w_out[j], tm=768, tn=1024, tk=2048, name="mm_ssd_out",
                               resid=xs, gate=mod[i][:, 2], ctx_rows=ctx_rows)
        else:
            s['qkv'] = _mm(s['h1'], w_qkv[j], tm=768, tn=384, tk=2048, name="mm_qkv", bslots=_ident)
            s['qg'], s['kg'] = w['attn_q_gain'][j][None], w['attn_k_gain'][j][None]
            s['qr'], s['kr'], s['vb'] = _qk_prep(s['qkv'], s['qg'], s['kg'], cos, sin, n_q, ctx_rows, "qk_prep")
            s['o'] = _attn_fwd(s['qr'], s['kr'], s['vb'], w['attn_sinks'][j], n_q, ctx_rows, "attn_fwd")
            s['mix'], x1 = _mm(s['o'], w_o[j], tm=768, tn=1024, tk=2048, name="mm_attn_out",
                               resid=xs, gate=mod[i][:, 2], ctx_rows=ctx_rows)
        s['x1'] = x1
        s['h2'] = _normmod(x1, nw2, mod[i], 1, ctx_rows, "normmod")
        s['u'] = _mm(s['h2'], w_up[i], tm=768, tn=1408, tk=2048, name="mm_ffn_up", bslots=_ffn_perm)
        s['a'] = _ffn_mid(s['u'], ffn_cw[i], ffn_cb[i], ctx_rows, "ffn_mid")
        s['f'], xs = _mm(s['a'], w_down[i], tm=768, tn=1024, tk=1408, name="mm_ffn_down",
                         resid=x1, gate=mod[i][:, 5], ctx_rows=ctx_rows)
        saved.append(s)

    dx, lacc = _loss_head(xs, tgt[0], ctx_rows, "loss_head")
    loss = lax.psum(lacc[0, 0], ("x", "y", "c"))

    gbig = {name: [None] * w[name].shape[0] for name in BIG}
    gs = {name: [None] * w[name].shape[0] for name in SMALL + ['ssd_conv_w', 'ffn_conv_w']}
    dmod = [None] * L
    for i in reversed(range(L)):
        j = i // 2
        s = saved[i]
        nw1, nw2 = w['norm1_w'][i][None], w['norm2_w'][i][None]
        dm2, dg2 = _gate_bwd(dx, s['f'], mod[i], 1, ctx_rows, "gate_bwd")
        da = _mm(dm2, w_down[i], tb=True, out_dtype=MXU, tm=768, tn=1408, tk=2048, name="mm_ffn_down_dx")
        gbig['ffn_w_down'][i] = _mm(s['a'], dm2, ta=True, tm=1408, tn=1024, tk=1056, name="mm_ffn_down_dw").reshape(NDEV, -1, D)
        du, gcw = _ffn_mid_bwd(s['u'], da, ffn_cw[i], ffn_cb[i], ctx_rows, "ffn_mid_bwd")
        gcw = deinterleave(gcw)
        gs['ffn_conv_w'][i], gs['ffn_conv_b'][i] = gcw[0:3], gcw[3]
        dh2 = _mm(du, w_up[i], tb=True, out_dtype=MXU, tm=768, tn=1024, tk=1408, name="mm_ffn_up_dx", bslots=_ffn_perm)
        gbig['ffn_w_up'][i] = _mm(s['h2'], du, ta=True, tm=1024, tn=1408, tk=1056, name="mm_ffn_up_dw", oslots=_ffn_perm)
        dx1, sums2 = _normmod_bwd(s['x1'], nw2, mod[i], dh2, dx, 1, ctx_rows, "normmod_bwd")
        dmix, dg1 = _gate_bwd(dx1, s['mix'], mod[i], 0, ctx_rows, "gate_bwd")
        if i % 2 == 0:
            do = _mm(dmix, w_out[j], tb=True, out_dtype=MXU, tm=768, tn=1024, tk=2048, name="mm_ssd_out_dx")
            gbig['ssd_w_out'][j] = _mm(s['o'], dmix, ta=True, tm=1024, tn=1024, tk=1056, name="mm_ssd_out_dw").reshape(NDEV, -1, D)
            dy, dzx, fs, dd = _ssd_finish_bwd(s['yf'], s['yb'], s['xbc'], s['zx'], s['dexp'], s['snw'], do, et_f, (T, IN),
                                              "ssd_finish_bwd")
            acc = _ssd_scan_bwd(s['xbc'], s['zx'], s['dtb'], s['alog'], e_f, et_f, s['dexp'], s['hin_f'], dy, None,
                                H, ncc, False, "ssd_scan_bwd_f")
            dxbc, ddt, ssm = _ssd_scan_bwd(s['xbc'], s['zx'], s['dtb'], s['alog'], e_b, et_b, s['dexp'], s['hin_b'], dy, acc,
                                           H, ncc, True, "ssd_scan_bwd_b")
            dzx, gscw = _ssd_conv_bwd(s['zx'], dxbc, dzx, s['cw'], s['cb'], DI, ctx_rows, "ssd_conv_bwd")
            dzx = _put_cols(dzx, ddt, IN // LANE - 1, "ssd_put_ddt")
            dh1 = _mm(dzx, w_in[j], tb=True, out_dtype=MXU, tm=768, tn=1024, tk=1152, name="mm_ssd_in_dx")
            dwi = _mm(s['h1'], dzx, ta=True, tm=1024, tn=1152, tk=1056, name="mm_ssd_in_dw")
            gbig['ssd_w_in'][j] = _split_cols(dwi, NDEV, "ssd_in_split")
            gs['ssd_conv_w'][j], gs['ssd_conv_b'][j] = gscw[0:3], gscw[3]
            gs['ssd_dt_bias_f'][j], gs['ssd_dt_bias_b'][j] = ssm[1, :H], ssm[1, H:]
            gs['ssd_a_log_f'][j], gs['ssd_a_log_b'][j] = ssm[0, :H], ssm[0, H:]
            gs['ssd_d'][j], gs['ssd_norm_w'][j] = dd[1, :H], fs[0]
        else:
            do = _mm(dmix, w_o[j], tb=True, out_dtype=MXU, tm=768, tn=1024, tk=2048, name="mm_attn_out_dx")
            gbig['attn_w_o'][j] = _mm(s['o'], dmix, ta=True, tm=1024, tn=1024, tk=1056, name="mm_attn_out_dw").reshape(NDEV, -1, D)
            dq, dk, dv, dsk = _attn_bwd(s['qr'], s['kr'], s['vb'], w['attn_sinks'][j], do, n_q, ctx_rows, "attn_bwd")
            dqkv, gg = _qk_prep_bwd(s['qkv'], s['qg'], s['kg'], cos, sin, dq, dk, dv, n_q, ctx_rows, "qk_prep_bwd")
            dh1 = _mm(dqkv, w_qkv[j], tb=True, out_dtype=MXU, tm=768, tn=1024, tk=384, name="mm_qkv_dx", bslots=_ident)
            gbig['attn_w_qkv'][j] = _mm(s['h1'], dqkv, ta=True, tm=1024, tn=384, tk=1056, name="mm_qkv_dw", oslots=_ident)
            gs['attn_q_gain'][j], gs['attn_k_gain'][j] = gg[0], gg[1]
            gs['attn_sinks'][j] = dsk.reshape(ATTN_KV_HEADS, 8, LANE)[:, 0, :n_q // ATTN_KV_HEADS].reshape(n_q)
        dx, sums1 = _normmod_bwd(s['x0'], nw1, mod[i], dh1, dx1, 0, ctx_rows, "normmod_bwd")
        gs['norm1_w'][i], gs['norm2_w'][i] = sums1[0, 2] + sums1[1, 2], sums2[0, 2] + sums2[1, 2]
        dmod[i] = jnp.stack([sums1[:, 0], sums1[:, 1], dg1[:, 0], sums2[:, 0], sums2[:, 1], dg2[:, 0]], axis=1)
    grad_x = dx[ctx_rows:][None]
    dmod = jnp.stack(dmod)
    dmod_ctx, dmod_lat = dmod[:, 0].reshape(L, 6 * D), dmod[:, 1].reshape(L, 6 * D)
    gs['ada_b'] = dmod_ctx + dmod_lat

    out = {}

    flat, groups = [], []
    for name in BIG:
        groups.append(list(range(len(flat), len(flat) + len(gbig[name]))))
        flat += gbig[name]
    r1 = _rs_sibling(flat, "rs_sibling")
    cidx = jnp.reshape(ci, (1,)).astype(jnp.int32)
    ps = [_add_blocks(g, r, cidx, "rs_add") for g, r in zip(flat, r1)]
    r2 = _rs_chips(ps, groups, "rs_chips")
    for name, parts in zip(BIG, r2):
        out[name] = _adamw(parts, w[name], m[name], v[name], "adamw")

    small_g = [jnp.stack(gs[n]) if isinstance(gs[n], list) else gs[n] for n in SMALL]
    extras = [jnp.stack(gs['ssd_conv_w']), jnp.stack(gs['ffn_conv_w'])]
    shapes_c = [w[n].shape for n in SMALL] + [e.shape for e in extras]
    g_c = _small_allgather(_pack(small_g + extras), "gather_small")
    zeros = [jnp.zeros(e.shape, F32) for e in extras]
    res = _adamw_small(g_c, _pack([w[n] for n in SMALL] + zeros), _pack([m[n] for n in SMALL] + zeros),
                       _pack([v[n] for n in SMALL] + zeros), "adamw_small")
    res = [_unpack(r, shapes_c) for r in res]
    for k, n in enumerate(SMALL):
        out[n] = tuple(r[k] for r in res)
    g_scw, g_fcw = res[0][len(SMALL)], res[0][len(SMALL) + 1]
    g_scw = lax.dynamic_index_in_dim(g_scw.reshape(n_ssd, 3, NDEV, XBC // NDEV), me, axis=2, keepdims=False)
    g_fcw = lax.dynamic_index_in_dim(g_fcw.reshape(L, 3, NDEV, G), me, axis=2, keepdims=False)
    conv = ['ssd_conv_w', 'ffn_conv_w']
    res = _adamw_small(_pack([g_scw, g_fcw])[None], _pack([w[n] for n in conv]), _pack([m[n] for n in conv]),
                       _pack([v[n] for n in conv]), "adamw_conv")
    res = [_unpack(r, [w[n].shape for n in conv]) for r in res]
    for k, n in enumerate(conv):
        out[n] = tuple(r[k] for r in res)

    g_m = _small_allgather(jnp.concatenate([dmod_lat, dmod_ctx], axis=0), "gather_dmod")
    all_lat, all_ctx = g_m[:, :L], g_m[:, L:]
    my_cols = lambda t: lax.dynamic_slice(t, (0, 0, me * nl), (NDEV, L, nl))
    dml = jnp.concatenate([my_cols(all_lat), jnp.zeros((MOD_ROWS - NDEV, L, nl), F32)], axis=0)
    g_ada, pc = _modulation_bwd(s_in, dml, my_cols(all_ctx), w['ada_w'], "modulation_bwd")
    out['ada_w'] = _adamw(g_ada[None], w['ada_w'], m['ada_w'], v['ada_w'], "adamw")
    g_d = _small_allgather(_pack([pc[0]]), "gather_cctx")
    res = _cctx_update(g_d, _pack([w['c_ctx']]), _pack([m['c_ctx']]), _pack([v['c_ctx']]), "adamw_cctx")
    out['c_ctx'] = tuple(_unpack(r, [(D,)])[0] for r in res)

    return (loss, grad_x) + tuple(out[n][k] for k in range(4) for n in WEIGHTS)


def kernel(x, c, ctx, c_ctx, ada_w, ada_b, norm1_w, norm2_w, ssd_w_in, ssd_conv_w, ssd_conv_b, ssd_dt_bias_f, ssd_dt_bias_b, ssd_a_log_f, ssd_a_log_b, ssd_d, ssd_norm_w, ssd_w_out, attn_w_qkv, attn_q_gain, attn_k_gain, attn_sinks, attn_w_o, ffn_w_up, ffn_conv_w, ffn_conv_b, ffn_w_down, loss_target, m_c_ctx, m_ada_w, m_ada_b, m_norm1_w, m_norm2_w, m_ssd_w_in, m_ssd_conv_w, m_ssd_conv_b, m_ssd_dt_bias_f, m_ssd_dt_bias_b, m_ssd_a_log_f, m_ssd_a_log_b, m_ssd_d, m_ssd_norm_w, m_ssd_w_out, m_attn_w_qkv, m_attn_q_gain, m_attn_k_gain, m_attn_sinks, m_attn_w_o, m_ffn_w_up, m_ffn_conv_w, m_ffn_conv_b, m_ffn_w_down, v_c_ctx, v_ada_w, v_ada_b, v_norm1_w, v_norm2_w, v_ssd_w_in, v_ssd_conv_w, v_ssd_conv_b, v_ssd_dt_bias_f, v_ssd_dt_bias_b, v_ssd_a_log_f, v_ssd_a_log_b, v_ssd_d, v_ssd_norm_w, v_ssd_w_out, v_attn_w_qkv, v_attn_q_gain, v_attn_k_gain, v_attn_sinks, v_attn_w_o, v_ffn_w_up, v_ffn_conv_w, v_ffn_conv_b, v_ffn_w_down):
    w = dict(c_ctx=c_ctx, ada_w=ada_w, ada_b=ada_b, norm1_w=norm1_w, norm2_w=norm2_w, ssd_w_in=ssd_w_in, ssd_conv_w=ssd_conv_w, ssd_conv_b=ssd_conv_b, ssd_dt_bias_f=ssd_dt_bias_f, ssd_dt_bias_b=ssd_dt_bias_b, ssd_a_log_f=ssd_a_log_f, ssd_a_log_b=ssd_a_log_b, ssd_d=ssd_d, ssd_norm_w=ssd_norm_w, ssd_w_out=ssd_w_out, attn_w_qkv=attn_w_qkv, attn_q_gain=attn_q_gain, attn_k_gain=attn_k_gain, attn_sinks=attn_sinks, attn_w_o=attn_w_o, ffn_w_up=ffn_w_up, ffn_conv_w=ffn_conv_w, ffn_conv_b=ffn_conv_b, ffn_w_down=ffn_w_down)
    m = dict(c_ctx=m_c_ctx, ada_w=m_ada_w, ada_b=m_ada_b, norm1_w=m_norm1_w, norm2_w=m_norm2_w, ssd_w_in=m_ssd_w_in, ssd_conv_w=m_ssd_conv_w, ssd_conv_b=m_ssd_conv_b, ssd_dt_bias_f=m_ssd_dt_bias_f, ssd_dt_bias_b=m_ssd_dt_bias_b, ssd_a_log_f=m_ssd_a_log_f, ssd_a_log_b=m_ssd_a_log_b, ssd_d=m_ssd_d, ssd_norm_w=m_ssd_norm_w, ssd_w_out=m_ssd_w_out, attn_w_qkv=m_attn_w_qkv, attn_q_gain=m_attn_q_gain, attn_k_gain=m_attn_k_gain, attn_sinks=m_attn_sinks, attn_w_o=m_attn_w_o, ffn_w_up=m_ffn_w_up, ffn_conv_w=m_ffn_conv_w, ffn_conv_b=m_ffn_conv_b, ffn_w_down=m_ffn_w_down)
    v = dict(c_ctx=v_c_ctx, ada_w=v_ada_w, ada_b=v_ada_b, norm1_w=v_norm1_w, norm2_w=v_norm2_w, ssd_w_in=v_ssd_w_in, ssd_conv_w=v_ssd_conv_w, ssd_conv_b=v_ssd_conv_b, ssd_dt_bias_f=v_ssd_dt_bias_f, ssd_dt_bias_b=v_ssd_dt_bias_b, ssd_a_log_f=v_ssd_a_log_f, ssd_a_log_b=v_ssd_a_log_b, ssd_d=v_ssd_d, ssd_norm_w=v_ssd_norm_w, ssd_w_out=v_ssd_w_out, attn_w_qkv=v_attn_w_qkv, attn_q_gain=v_attn_q_gain, attn_k_gain=v_attn_k_gain, attn_sinks=v_attn_sinks, attn_w_o=v_attn_w_o, ffn_w_up=v_ffn_w_up, ffn_conv_w=v_ffn_conv_w, ffn_conv_b=v_ffn_conv_b, ffn_w_down=v_ffn_w_down)
    return _step(x, c, ctx, w, loss_target, m, v)
```

```python
import functools

import numpy as np
import jax
import jax.numpy as jnp
from jax import lax
from jax.experimental import pallas as pl
from jax.experimental.pallas import tpu as pltpu

F32 = jnp.float32
BF16 = jnp.bfloat16
MXU = BF16
XFER = BF16
NORM_EPS = 1e-6
VMEM_CAP = 56 * 1024 * 1024
HALO = 8
LANE = 128
NDEV = 8

GRID_W = 64
ROPE_THETA = 10000.0
ATTN_KV_HEADS = 4
ATTN_WINDOW = 128
BLK = 128
SSD_GROUPS = 8

ADAM_LR, ADAM_B1, ADAM_B2, ADAM_EPS, ADAM_WD, ADAM_STEP = 0.001, 0.9, 0.999, 1e-08, 0.01, 10

MESH_ID = pl.DeviceIdType.MESH


def _cparams(sem, est_bytes):
    lim = int(min(VMEM_CAP, max(16 * 1024 * 1024, est_bytes * 1.3 + (4 << 20))))
    return pltpu.CompilerParams(dimension_semantics=sem, vmem_limit_bytes=lim)


def _nbytes(shape, dtype):
    return int(np.prod(shape)) * jnp.dtype(dtype).itemsize


def _silu(x):
    return x * jax.nn.sigmoid(x)


def _dsilu(x):
    s = jax.nn.sigmoid(x)
    return s * (1.0 + x * (1.0 - s))


def _split3(v):
    h = v.astype(BF16)
    r = v - h.astype(F32)
    m = r.astype(BF16)
    l = (r - m.astype(F32)).astype(BF16)
    return h, m, l


def _dot(a, b, dn=(((1,), (0,)), ((), ()))):
    return lax.dot_general(a, b, dn, preferred_element_type=F32)


NT = (((1,), (1,)), ((), ()))
TN = (((0,), (0,)), ((), ()))


def _dot3_rhs(sel, v):
    return sum(_dot(sel, p) for p in _split3(v))


def _dot3_lhs(v, sel, dn=(((1,), (0,)), ((), ()))):
    return sum(_dot(p, sel, dn) for p in _split3(v))


def _dot2_lhs(v, sel):
    h, m, _ = _split3(v)
    return _dot(h, sel) + _dot(m, sel)


def _ident(s):
    return s


def _ffn_perm(s):
    return (s % 2) * 4 + s // 2


def _mm(a, b, *, ta=False, tb=False, out_dtype=F32, tm, tn, tk, name, bslots=None, oslots=None,
        resid=None, gate=None, ctx_rows=0):
    M = a.shape[1] if ta else a.shape[0]
    K = a.shape[0] if ta else a.shape[1]
    if bslots is None:
        N = b.shape[0] if tb else b.shape[1]
    else:
        G = b.shape[2]
        N = b.shape[1] if tb else NDEV * G
        assert (NDEV * G == K) if tb else (b.shape[1] == K)
    tm, tn, tk = min(tm, M), min(tn, N), min(tk, K)
    if bslots is not None:
        if tb:
            tk = min(tk, G)
            assert G % tk == 0
        else:
            tn = min(tn, G)
            assert G % tn == 0
    if oslots is not None:
        Go = N // NDEV
        tn = min(tn, Go)
        assert Go % tn == 0
    assert M % tm == 0 and N % tn == 0 and K % tk == 0, (name, M, N, K, tm, tn, tk)
    nk = K // tk
    fused = resid is not None
    dn = (((0 if ta else 1,), (1 if tb else 0,)), ((), ()))

    def body(*refs):
        if fused:
            a_ref, b_ref, r_ref, g_ref, o_ref, x_ref = refs[:6]
            rest = refs[6:]
        else:
            a_ref, b_ref, o_ref = refs[:3]
            rest = refs[3:]
        bv = b_ref[0] if bslots is not None else b_ref[...]
        p = lax.dot_general(a_ref[...].astype(MXU), bv.astype(MXU), dn, preferred_element_type=F32)

        def finish(acc):
            if oslots is not None:
                o_ref[0] = acc.astype(o_ref.dtype)
            else:
                o_ref[...] = acc.astype(o_ref.dtype)
            if fused:
                row = pl.program_id(0) * tm + lax.broadcasted_iota(jnp.int32, (tm, 1), 0)
                g = jnp.where(row < ctx_rows, g_ref[0:1, :], g_ref[1:2, :])
                x_ref[...] = r_ref[...] + g * acc

        if nk == 1:
            finish(p)
        else:
            acc_ref = rest[0]
            k = pl.program_id(2)

            @pl.when(k == 0)
            def _():
                acc_ref[...] = p

            @pl.when(k > 0)
            def _():
                acc_ref[...] += p

            @pl.when(k == nk - 1)
            def _():
                finish(acc_ref[...])

    a_spec = pl.BlockSpec((tk, tm), lambda i, j, k: (k, i)) if ta else pl.BlockSpec((tm, tk), lambda i, j, k: (i, k))
    if bslots is None:
        b_spec = pl.BlockSpec((tn, tk), lambda i, j, k: (j, k)) if tb else pl.BlockSpec((tk, tn), lambda i, j, k: (k, j))
    elif tb:
        kpg = G // tk
        b_spec = pl.BlockSpec((1, tn, tk), lambda i, j, k: (bslots(k // kpg), j, k % kpg))
    else:
        npg = G // tn
        b_spec = pl.BlockSpec((1, tk, tn), lambda i, j, k: (bslots(j // npg), k, j % npg))
    if oslots is None:
        o_spec = pl.BlockSpec((tm, tn), lambda i, j, k: (i, j))
        o_shape = jax.ShapeDtypeStruct((M, N), out_dtype)
    else:
        opg = Go // tn
        o_spec = pl.BlockSpec((1, tm, tn), lambda i, j, k: (oslots(j // opg), i, j % opg))
        o_shape = jax.ShapeDtypeStruct((NDEV, M, Go), out_dtype)
    in_specs = [a_spec, b_spec]
    out_shape = [o_shape]
    out_specs = [o_spec]
    args = [a, b]
    est = 2 * (_nbytes((tm, tk), a.dtype) + _nbytes((tk, tn), b.dtype) + _nbytes((tm, tn), out_dtype)) + 3 * _nbytes((tm, tn), F32)
    if fused:
        in_specs += [o_spec, pl.BlockSpec((2, tn), lambda i, j, k: (0, j))]
        out_shape.append(jax.ShapeDtypeStruct((M, N), F32))
        out_specs.append(o_spec)
        args += [resid, gate]
        est += 4 * _nbytes((tm, tn), F32)
    scratch = [] if nk == 1 else [pltpu.VMEM((tm, tn), F32)]
    res = pl.pallas_call(
        body, name=name, grid=(M // tm, N // tn, nk), in_specs=in_specs, out_specs=out_specs, out_shape=out_shape,
        scratch_shapes=scratch, compiler_params=_cparams(("parallel", "parallel", "arbitrary"), est),
    )(*args)
    return res if fused else res[0]


def _stream_of(i, tr, ctx_rows):
    return jnp.where(i * tr < ctx_rows, 0, 1)


def _acc_by_stream(sums_ref, part, i, n_ctx):
    @pl.when((i == 0) | (i == n_ctx))
    def _():
        sums_ref[0] = part

    @pl.when((i != 0) & (i != n_ctx))
    def _():
        sums_ref[0] += part


def _normmod(x, nw, mod, which, ctx_rows, name, tr=256):
    T, D = x.shape
    tr = min(tr, ctx_rows)
    assert T % tr == 0 and ctx_rows % tr == 0
    s_sh, s_sc = 3 * which, 3 * which + 1

    def body(x_ref, nw_ref, mod_ref, h_ref):
        xv = x_ref[...]
        r = lax.rsqrt(jnp.mean(xv * xv, axis=-1, keepdims=True) + NORM_EPS)
        y = (xv * r) * nw_ref[...]
        h_ref[...] = (y * (1.0 + mod_ref[0, s_sc:s_sc + 1, :]) + mod_ref[0, s_sh:s_sh + 1, :]).astype(h_ref.dtype)

    return pl.pallas_call(
        body, name=name, grid=(T // tr,),
        in_specs=[pl.BlockSpec((tr, D), lambda i: (i, 0)), pl.BlockSpec((1, D), lambda i: (0, 0)),
                  pl.BlockSpec((1, 6, D), lambda i: (_stream_of(i, tr, ctx_rows), 0, 0))],
        out_specs=pl.BlockSpec((tr, D), lambda i: (i, 0)),
        out_shape=jax.ShapeDtypeStruct((T, D), MXU),
        compiler_params=_cparams(("parallel",), 10 * _nbytes((tr, D), F32)),
    )(x, nw, mod)


def _normmod_bwd(x, nw, mod, dh, dx_in, which, ctx_rows, name, tr=256):
    T, D = x.shape
    tr = min(tr, ctx_rows)
    s_sc = 3 * which + 1
    n_ctx = ctx_rows // tr

    def body(x_ref, nw_ref, mod_ref, dh_ref, dxi_ref, dx_ref, sums_ref):
        i = pl.program_id(0)
        xv = x_ref[...]
        r = lax.rsqrt(jnp.mean(xv * xv, axis=-1, keepdims=True) + NORM_EPS)
        xh = xv * r
        dh_v = dh_ref[...].astype(F32)
        sc1 = 1.0 + mod_ref[0, s_sc:s_sc + 1, :]
        nwv = nw_ref[...]
        dxh = dh_v * (nwv * sc1)
        dx_ref[...] = dxi_ref[...] + r * (dxh - xh * jnp.mean(dxh * xh, axis=-1, keepdims=True))
        t = dh_v * xh
        part = jnp.concatenate([jnp.sum(dh_v, axis=0, keepdims=True), jnp.sum(t * nwv, axis=0, keepdims=True),
                                jnp.sum(t * sc1, axis=0, keepdims=True), jnp.zeros((5, D), F32)], axis=0)
        _acc_by_stream(sums_ref, part, i, n_ctx)

    row = pl.BlockSpec((tr, D), lambda i: (i, 0))
    return pl.pallas_call(
        body, name=name, grid=(T // tr,),
        in_specs=[row, pl.BlockSpec((1, D), lambda i: (0, 0)),
                  pl.BlockSpec((1, 6, D), lambda i: (_stream_of(i, tr, ctx_rows), 0, 0)), row, row],
        out_specs=[row, pl.BlockSpec((1, 8, D), lambda i: (_stream_of(i, tr, ctx_rows), 0, 0))],
        out_shape=[jax.ShapeDtypeStruct((T, D), F32), jax.ShapeDtypeStruct((2, 8, D), F32)],
        compiler_params=_cparams(("arbitrary",), 16 * _nbytes((tr, D), F32)),
    )(x, nw, mod, dh, dx_in)


def _gate_bwd(dx, mix, mod, which, ctx_rows, name, tr=256):
    T, D = dx.shape
    tr = min(tr, ctx_rows)
    s_g = 3 * which + 2
    n_ctx = ctx_rows // tr

    def body(dx_ref, mix_ref, mod_ref, dm_ref, sums_ref):
        i = pl.program_id(0)
        dxv = dx_ref[...]
        dm_ref[...] = (dxv * mod_ref[0, s_g:s_g + 1, :]).astype(dm_ref.dtype)
        part = jnp.concatenate([jnp.sum(dxv * mix_ref[...], axis=0, keepdims=True), jnp.zeros((7, D), F32)], axis=0)
        _acc_by_stream(sums_ref, part, i, n_ctx)

    row = pl.BlockSpec((tr, D), lambda i: (i, 0))
    return pl.pallas_call(
        body, name=name, grid=(T // tr,),
        in_specs=[row, row, pl.BlockSpec((1, 6, D), lambda i: (_stream_of(i, tr, ctx_rows), 0, 0))],
        out_specs=[row, pl.BlockSpec((1, 8, D), lambda i: (_stream_of(i, tr, ctx_rows), 0, 0))],
        out_shape=[jax.ShapeDtypeStruct((T, D), MXU), jax.ShapeDtypeStruct((2, 8, D), F32)],
        compiler_params=_cparams(("arbitrary",), 10 * _nbytes((tr, D), F32)),
    )(dx, mix, mod)


def _halo_specs(tr, tn, n_row_tiles, col_of):
    g = tr // HALO
    last = n_row_tiles * g - 1
    return [pl.BlockSpec((HALO, tn), lambda j, i: (jnp.maximum(i * g - 1, 0), col_of(j))),
            pl.BlockSpec((tr, tn), lambda j, i: (i, col_of(j))),
            pl.BlockSpec((HALO, tn), lambda j, i: (jnp.minimum((i + 1) * g, last), col_of(j)))]


def _ext(p_ref, m_ref, n_ref):
    return jnp.concatenate([p_ref[...].astype(F32), m_ref[...].astype(F32), n_ref[...].astype(F32)], axis=0)


def _seq_masks(i, tr, ctx_rows, total_rows):
    row = i * tr - HALO + lax.broadcasted_iota(jnp.int32, (tr + 2 * HALO, 1), 0)
    has_prev = (row != 0) & (row != ctx_rows)
    has_next = (row != ctx_rows - 1) & (row != total_rows - 1)
    return has_prev, has_next


def _shift_down(e):
    return pltpu.roll(e, 1, 0)


def _shift_up(e):
    return pltpu.roll(e, e.shape[0] - 1, 0)


def _conv3(e, w, has_prev, has_next):
    prev = jnp.where(has_prev, _shift_down(e), 0.0)
    nxt = jnp.where(has_next, _shift_up(e), 0.0)
    return prev * w[0:1, :] + e * w[1:2, :] + nxt * w[2:3, :]


def _conv3_t(d, w, has_prev, has_next):
    from_next = jnp.where(has_next, _shift_up(d), 0.0)
    from_prev = jnp.where(has_prev, _shift_down(d), 0.0)
    return from_next * w[0:1, :] + d * w[1:2, :] + from_prev * w[2:3, :]


def _conv_wgrad(d, e, has_prev, has_next):
    n = e.shape[0]
    c = slice(HALO, n - HALO)
    prev = jnp.where(has_prev, _shift_down(e), 0.0)
    nxt = jnp.where(has_next, _shift_up(e), 0.0)
    dc = d[c]
    return jnp.concatenate([jnp.sum(dc * prev[c], axis=0, keepdims=True), jnp.sum(dc * e[c], axis=0, keepdims=True),
                            jnp.sum(dc * nxt[c], axis=0, keepdims=True), jnp.sum(dc, axis=0, keepdims=True),
                            jnp.zeros((4, e.shape[1]), F32)], axis=0)


def _acc_first(ref, part, i):
    @pl.when(i == 0)
    def _():
        ref[...] = part

    @pl.when(i > 0)
    def _():
        ref[...] += part


def _ffn_mid(u, cw, cb, ctx_rows, name, tr=128):
    T, F2 = u.shape
    G = F2 // NDEV
    tr = min(tr, ctx_rows)
    nr, nc = T // tr, NDEV // 2

    def body(up, um, un, w_ref, b_ref, a_ref):
        i = pl.program_id(1)
        hp, hn = _seq_masks(i, tr, ctx_rows, T)
        uc = _conv3(_ext(up, um, un), w_ref[...], hp, hn)[HALO:HALO + tr] + b_ref[...]
        a_ref[...] = (_silu(uc[:, G:]) * uc[:, :G]).astype(a_ref.dtype)

    return pl.pallas_call(
        body, name=name, grid=(nc, nr),
        in_specs=_halo_specs(tr, 2 * G, nr, lambda j: j) + [pl.BlockSpec((3, 2 * G), lambda j, i: (0, j)),
                                                             pl.BlockSpec((1, 2 * G), lambda j, i: (0, j))],
        out_specs=pl.BlockSpec((tr, G), lambda j, i: (i, j)),
        out_shape=jax.ShapeDtypeStruct((T, F2 // 2), MXU),
        compiler_params=_cparams(("parallel", "parallel"), 12 * _nbytes((tr + 16, 2 * G), F32)),
    )(u, u, u, cw, cb)


def _ffn_mid_bwd(u, da, cw, cb, ctx_rows, name, tr=128):
    T, F2 = u.shape
    G = F2 // NDEV
    tr = min(tr, ctx_rows)
    nr, nc = T // tr, NDEV // 2

    def body(up, um, un, dp, dm, dn_, w_ref, b_ref, du_ref, gw_ref):
        i = pl.program_id(1)
        hp, hn = _seq_masks(i, tr, ctx_rows, T)
        e = _ext(up, um, un)
        w = w_ref[...]
        uc = _conv3(e, w, hp, hn) + b_ref[...]
        val, gt = uc[:, :G], uc[:, G:]
        dav = _ext(dp, dm, dn_)
        duc = jnp.concatenate([dav * _silu(gt), dav * val * _dsilu(gt)], axis=1)
        du_ref[...] = _conv3_t(duc, w, hp, hn)[HALO:HALO + tr].astype(du_ref.dtype)
        _acc_first(gw_ref, _conv_wgrad(duc, e, hp, hn), i)

    du, gw = pl.pallas_call(
        body, name=name, grid=(nc, nr),
        in_specs=(_halo_specs(tr, 2 * G, nr, lambda j: j) + _halo_specs(tr, G, nr, lambda j: j)
                  + [pl.BlockSpec((3, 2 * G), lambda j, i: (0, j)), pl.BlockSpec((1, 2 * G), lambda j, i: (0, j))]),
        out_specs=[pl.BlockSpec((tr, 2 * G), lambda j, i: (i, j)), pl.BlockSpec((8, 2 * G), lambda j, i: (0, j))],
        out_shape=[jax.ShapeDtypeStruct((T, F2), MXU), jax.ShapeDtypeStruct((8, F2), F32)],
        compiler_params=_cparams(("parallel", "arbitrary"), 24 * _nbytes((tr + 16, 2 * G), F32)),
    )(u, u, u, da, da, da, cw, cb)
    return du, gw


def _ssd_conv(zx, cw, cb, col0, ctx_rows, name, tr=256, tn=512):
    T = zx.shape[0]
    C = cw.shape[1]
    tr, tn = min(tr, ctx_rows), min(tn, C)
    assert C % tn == 0 and col0 % tn == 0
    nr, nc, cb0 = T // tr, C // tn, col0 // tn

    def body(zp, zm, zn, w_ref, b_ref, o_ref):
        i = pl.program_id(1)
        hp, hn = _seq_masks(i, tr, ctx_rows, T)
        o_ref[...] = _silu(_conv3(_ext(zp, zm, zn), w_ref[...], hp, hn)[HALO:HALO + tr] + b_ref[...])

    return pl.pallas_call(
        body, name=name, grid=(nc, nr),
        in_specs=_halo_specs(tr, tn, nr, lambda j: j + cb0) + [pl.BlockSpec((3, tn), lambda j, i: (0, j)),
                                                                pl.BlockSpec((1, tn), lambda j, i: (0, j))],
        out_specs=pl.BlockSpec((tr, tn), lambda j, i: (i, j)),
        out_shape=jax.ShapeDtypeStruct((T, C), F32),
        compiler_params=_cparams(("parallel", "parallel"), 12 * _nbytes((tr + 16, tn), F32)),
    )(zx, zx, zx, cw, cb)


def _ssd_conv_bwd(zx, dxbc, dzx, cw, cb, col0, ctx_rows, name, tr=256, tn=512):
    T = zx.shape[0]
    C = cw.shape[1]
    tr, tn = min(tr, ctx_rows), min(tn, C)
    nr, nc, cb0 = T // tr, C // tn, col0 // tn

    def body(zp, zm, zn, dp, dm, dn_, w_ref, b_ref, dzx_in, dz_ref, gw_ref):
        del dzx_in
        i = pl.program_id(1)
        hp, hn = _seq_masks(i, tr, ctx_rows, T)
        e = _ext(zp, zm, zn)
        w = w_ref[...]
        pre = _conv3(e, w, hp, hn) + b_ref[...]
        dpre = _ext(dp, dm, dn_) * _dsilu(pre)
        dz_ref[...] = _conv3_t(dpre, w, hp, hn)[HALO:HALO + tr].astype(dz_ref.dtype)
        _acc_first(gw_ref, _conv_wgrad(dpre, e, hp, hn), i)

    return pl.pallas_call(
        body, name=name, grid=(nc, nr),
        in_specs=(_halo_specs(tr, tn, nr, lambda j: j + cb0) + _halo_specs(tr, tn, nr, lambda j: j)
                  + [pl.BlockSpec((3, tn), lambda j, i: (0, j)), pl.BlockSpec((1, tn), lambda j, i: (0, j)),
                     pl.BlockSpec(memory_space=pl.ANY)]),
        out_specs=[pl.BlockSpec((tr, tn), lambda j, i: (i, j + cb0)), pl.BlockSpec((8, tn), lambda j, i: (0, j))],
        out_shape=[jax.ShapeDtypeStruct(dzx.shape, dzx.dtype), jax.ShapeDtypeStruct((8, C), F32)],
        input_output_aliases={8: 0},
        compiler_params=_cparams(("parallel", "arbitrary"), 24 * _nbytes((tr + 16, tn), F32)),
    )(zx, zx, zx, dxbc, dxbc, dxbc, cw, cb, dzx)


def _cat_cols(w3, name, tr=256):
    n, K, G = w3.shape
    tr = min(tr, K)

    def body(w_ref, o_ref):
        o_ref[...] = jnp.concatenate([w_ref[d].astype(F32) for d in range(n)], axis=1).astype(o_ref.dtype)

    return pl.pallas_call(
        body, name=name, grid=(K // tr,),
        in_specs=[pl.BlockSpec((n, tr, G), lambda i: (0, i, 0))], out_specs=pl.BlockSpec((tr, n * G), lambda i: (i, 0)),
        out_shape=jax.ShapeDtypeStruct((K, n * G), w3.dtype),
        compiler_params=_cparams(("parallel",), 6 * _nbytes((tr, n * G), F32)),
    )(w3)


def _split_cols(g, n, out_dtype, name, tr=256):
    K, NG = g.shape
    G = NG // n
    tr = min(tr, K)

    def body(g_ref, o_ref):
        for d in range(n):
            o_ref[d] = g_ref[:, d * G:(d + 1) * G].astype(o_ref.dtype)

    return pl.pallas_call(
        body, name=name, grid=(K // tr,),
        in_specs=[pl.BlockSpec((tr, NG), lambda i: (i, 0))], out_specs=pl.BlockSpec((n, tr, G), lambda i: (0, i, 0)),
        out_shape=jax.ShapeDtypeStruct((n, K, G), out_dtype),
        compiler_params=_cparams(("parallel",), 6 * _nbytes((tr, NG), F32)),
    )(g)


def _put_cols(dst, src, col_blk, name, tr=256):
    T, W = src.shape
    tr = min(tr, T)

    def body(s_ref, d_in, o_ref):
        del d_in
        o_ref[...] = s_ref[...].astype(o_ref.dtype)

    return pl.pallas_call(
        body, name=name, grid=(T // tr,),
        in_specs=[pl.BlockSpec((tr, W), lambda i: (i, 0)), pl.BlockSpec(memory_space=pl.ANY)],
        out_specs=pl.BlockSpec((tr, W), lambda i: (i, col_blk)),
        out_shape=jax.ShapeDtypeStruct(dst.shape, dst.dtype),
        input_output_aliases={1: 0},
        compiler_params=_cparams(("parallel",), 8 * _nbytes((tr, W), F32)),
    )(src, dst)


def _rope_tables(t_lat, ctx_rows, hd):
    half, quarter = hd // 2, hd // 4
    pos = jnp.arange(t_lat)
    row = (pos // GRID_W).astype(F32)
    col = (pos % GRID_W).astype(F32)
    inv_freq = ROPE_THETA ** (-jnp.arange(0, half, 2, dtype=F32) / half)
    ar, ac = row[:, None] * inv_freq[None, :], col[:, None] * inv_freq[None, :]
    cos = jnp.concatenate([jnp.cos(ar), jnp.cos(ar), jnp.cos(ac), jnp.cos(ac)], axis=1)
    sin = jnp.concatenate([-jnp.sin(ar), jnp.sin(ar), -jnp.sin(ac), jnp.sin(ac)], axis=1)
    del quarter
    cos = jnp.concatenate([jnp.ones((ctx_rows, hd), F32), cos], axis=0)
    sin = jnp.concatenate([jnp.zeros((ctx_rows, hd), F32), sin], axis=0)
    return cos, sin


def _partner(y):
    hd = y.shape[1]
    q = hd // 4
    lane = lax.broadcasted_iota(jnp.int32, y.shape, 1)
    return jnp.where((lane % (2 * q)) < q, pltpu.roll(y, hd - q, 1), pltpu.roll(y, q, 1))


def _qk_prep(qkv, qg, kg, cos, sin, n_q, ctx_rows, name, tr=256):
    T = qkv.shape[0]
    hd = qg.shape[1]
    n_kv = ATTN_KV_HEADS
    tr = min(tr, ctx_rows)

    def body(x_ref, qg_ref, kg_ref, c_ref, s_ref, q_ref, k_ref, v_ref):
        cv, sv = c_ref[...], s_ref[...]
        for h in range(n_q + n_kv):
            xh = x_ref[:, h * hd:(h + 1) * hd]
            r = lax.rsqrt(jnp.mean(xh * xh, axis=-1, keepdims=True) + NORM_EPS)
            y = (xh * r) * (qg_ref[...] if h < n_q else kg_ref[...])
            rot = y * cv + _partner(y) * sv
            if h < n_q:
                q_ref[:, h * hd:(h + 1) * hd] = rot.astype(q_ref.dtype)
            else:
                k_ref[:, (h - n_q) * hd:(h - n_q + 1) * hd] = rot.astype(k_ref.dtype)
        v_ref[...] = x_ref[:, (n_q + n_kv) * hd:].astype(v_ref.dtype)

    W = qkv.shape[1]
    return pl.pallas_call(
        body, name=name, grid=(T // tr,),
        in_specs=[pl.BlockSpec((tr, W), lambda i: (i, 0)), pl.BlockSpec((1, hd), lambda i: (0, 0)),
                  pl.BlockSpec((1, hd), lambda i: (0, 0)), pl.BlockSpec((tr, hd), lambda i: (i, 0)),
                  pl.BlockSpec((tr, hd), lambda i: (i, 0))],
        out_specs=[pl.BlockSpec((tr, n_q * hd), lambda i: (i, 0)), pl.BlockSpec((tr, n_kv * hd), lambda i: (i, 0)),
                   pl.BlockSpec((tr, n_kv * hd), lambda i: (i, 0))],
        out_shape=[jax.ShapeDtypeStruct((T, n_q * hd), MXU), jax.ShapeDtypeStruct((T, n_kv * hd), MXU),
                   jax.ShapeDtypeStruct((T, n_kv * hd), MXU)],
        compiler_params=_cparams(("parallel",), 6 * _nbytes((tr, W), F32)),
    )(qkv, qg, kg, cos, sin)


def _qk_prep_bwd(qkv, qg, kg, cos, sin, dq, dk, dv, n_q, ctx_rows, name, tr=256):
    T, W = qkv.shape
    hd = qg.shape[1]
    n_kv = ATTN_KV_HEADS
    tr = min(tr, ctx_rows)

    def body(x_ref, qg_ref, kg_ref, c_ref, s_ref, dq_ref, dk_ref, dv_ref, o_ref, g_ref):
        i = pl.program_id(0)
        cv, sv = c_ref[...], s_ref[...]
        gq = jnp.zeros((1, hd), F32)
        gk = jnp.zeros((1, hd), F32)
        for h in range(n_q + n_kv):
            xh = x_ref[:, h * hd:(h + 1) * hd]
            gain = qg_ref[...] if h < n_q else kg_ref[...]
            drot = (dq_ref[:, h * hd:(h + 1) * hd] if h < n_q else dk_ref[:, (h - n_q) * hd:(h - n_q + 1) * hd]).astype(F32)
            dy = drot * cv + _partner(drot * sv)
            r = lax.rsqrt(jnp.mean(xh * xh, axis=-1, keepdims=True) + NORM_EPS)
            xn = xh * r
            gsum = jnp.sum(dy * xn, axis=0, keepdims=True)
            if h < n_q:
                gq = gq + gsum
            else:
                gk = gk + gsum
            dxn = dy * gain
            o_ref[:, h * hd:(h + 1) * hd] = (r * (dxn - xn * jnp.mean(dxn * xn, axis=-1, keepdims=True))).astype(o_ref.dtype)
        o_ref[:, (n_q + n_kv) * hd:] = dv_ref[...].astype(o_ref.dtype)
        _acc_first(g_ref, jnp.concatenate([gq, gk, jnp.zeros((6, hd), F32)], axis=0), i)

    return pl.pallas_call(
        body, name=name, grid=(T // tr,),
        in_specs=[pl.BlockSpec((tr, W), lambda i: (i, 0)), pl.BlockSpec((1, hd), lambda i: (0, 0)),
                  pl.BlockSpec((1, hd), lambda i: (0, 0)), pl.BlockSpec((tr, hd), lambda i: (i, 0)),
                  pl.BlockSpec((tr, hd), lambda i: (i, 0)), pl.BlockSpec((tr, n_q * hd), lambda i: (i, 0)),
                  pl.BlockSpec((tr, n_kv * hd), lambda i: (i, 0)), pl.BlockSpec((tr, n_kv * hd), lambda i: (i, 0))],
        out_specs=[pl.BlockSpec((tr, W), lambda i: (i, 0)), pl.BlockSpec((8, hd), lambda i: (0, 0))],
        out_shape=[jax.ShapeDtypeStruct((T, W), MXU), jax.ShapeDtypeStruct((8, hd), F32)],
        compiler_params=_cparams(("arbitrary",), 8 * _nbytes((tr, W), F32)),
    )(qkv, qg, kg, cos, sin, dq, dk, dv)


def _attn_scores(q_ref, k_ref, sink_ref, h, qb, ctx_rows, nb, hd, grp):
    scale = hd ** -0.5
    w0 = jnp.clip(qb - 1, 0, nb - 3) * BLK
    w0 = pl.multiple_of(w0, BLK)
    qv = q_ref[...]
    qs = jnp.concatenate([qv[:, g * hd:(g + 1) * hd] for g in range(grp)], axis=0)
    kc = k_ref[0:ctx_rows, :]
    kb = k_ref[pl.ds(w0, 3 * BLK), :]
    s_c = _dot(qs, kc, NT) * scale
    s_b = _dot(qs, kb, NT) * scale
    n = grp * BLK
    qpos = qb * BLK + lax.broadcasted_iota(jnp.int32, (n, 3 * BLK), 0) % BLK
    kpos = w0 + lax.broadcasted_iota(jnp.int32, (n, 3 * BLK), 1)
    ok = (jnp.abs(kpos - qpos) <= ATTN_WINDOW) & (kpos >= ctx_rows) & (qpos >= ctx_rows)
    s_b = jnp.where(ok, s_b, -jnp.inf)
    gi = lax.broadcasted_iota(jnp.int32, (n, 1), 0) // BLK
    sink = jnp.zeros((n, 1), F32)
    for g in range(grp):
        sink = jnp.where(gi == g, sink_ref[h * grp + g], sink)
    m = jnp.maximum(jnp.maximum(jnp.max(s_c, axis=1, keepdims=True), jnp.max(s_b, axis=1, keepdims=True)), sink)
    e_c, e_b, e_s = jnp.exp(s_c - m), jnp.exp(s_b - m), jnp.exp(sink - m)
    inv = 1.0 / (jnp.sum(e_c, axis=1, keepdims=True) + jnp.sum(e_b, axis=1, keepdims=True) + e_s)
    return qs, kc, kb, w0, e_c * inv, e_b * inv, e_s * inv, gi


def _attn_fwd(qr, kr, vb, sinks, n_q, ctx_rows, name):
    T = qr.shape[0]
    n_kv = ATTN_KV_HEADS
    grp = n_q // n_kv
    hd = qr.shape[1] // n_q
    nb = T // BLK

    def body(sink_ref, q_ref, k_ref, v_ref, o_ref):
        h, qb = pl.program_id(0), pl.program_id(1)
        _, _, _, w0, p_c, p_b, _, _ = _attn_scores(q_ref, k_ref, sink_ref, h, qb, ctx_rows, nb, hd, grp)
        o = _dot(p_c.astype(MXU), v_ref[0:ctx_rows, :]) + _dot(p_b.astype(MXU), v_ref[pl.ds(w0, 3 * BLK), :])
        o_ref[...] = jnp.concatenate([o[g * BLK:(g + 1) * BLK] for g in range(grp)], axis=1).astype(o_ref.dtype)

    return pl.pallas_call(
        body, name=name, grid=(n_kv, nb),
        in_specs=[pl.BlockSpec(memory_space=pltpu.SMEM), pl.BlockSpec((BLK, grp * hd), lambda h, i: (i, h)),
                  pl.BlockSpec((T, hd), lambda h, i: (0, h)), pl.BlockSpec((T, hd), lambda h, i: (0, h))],
        out_specs=pl.BlockSpec((BLK, grp * hd), lambda h, i: (i, h)),
        out_shape=jax.ShapeDtypeStruct((T, n_q * hd), MXU),
        compiler_params=_cparams(("parallel", "arbitrary"), 4 * _nbytes((T, hd), MXU) + 24 * _nbytes((grp * BLK, 5 * BLK), F32)),
    )(sinks, qr, kr, vb)


def _attn_bwd(qr, kr, vb, sinks, do, n_q, ctx_rows, name):
    T = qr.shape[0]
    n_kv = ATTN_KV_HEADS
    grp = n_q // n_kv
    hd = qr.shape[1] // n_q
    nb = T // BLK
    scale = hd ** -0.5

    def body(sink_ref, q_ref, k_ref, v_ref, do_ref, dq_ref, dk_ref, dv_ref, ds_ref):
        h, qb = pl.program_id(0), pl.program_id(1)
        qs, kc, kb, w0, p_c, p_b, p_s, gi = _attn_scores(q_ref, k_ref, sink_ref, h, qb, ctx_rows, nb, hd, grp)
        dov = do_ref[...]
        dos = jnp.concatenate([dov[:, g * hd:(g + 1) * hd] for g in range(grp)], axis=0)
        vc = v_ref[0:ctx_rows, :]
        vw = v_ref[pl.ds(w0, 3 * BLK), :]
        dp_c = _dot(dos, vc, NT)
        dp_b = _dot(dos, vw, NT)
        delta = jnp.sum(p_c * dp_c, axis=1, keepdims=True) + jnp.sum(p_b * dp_b, axis=1, keepdims=True)
        ds_c = (p_c * (dp_c - delta) * scale).astype(MXU)
        ds_b = (p_b * (dp_b - delta) * scale).astype(MXU)
        dq = _dot(ds_c, kc) + _dot(ds_b, kb)
        dq_ref[...] = jnp.concatenate([dq[g * BLK:(g + 1) * BLK] for g in range(grp)], axis=1)

        @pl.when(qb == 0)
        def _():
            dk_ref[...] = jnp.zeros(dk_ref.shape, F32)
            dv_ref[...] = jnp.zeros(dv_ref.shape, F32)

        dk_ref[0:ctx_rows, :] += _dot(ds_c, qs, TN)
        dv_ref[0:ctx_rows, :] += _dot(p_c.astype(MXU), dos, TN)
        dk_ref[pl.ds(w0, 3 * BLK), :] += _dot(ds_b, qs, TN)
        dv_ref[pl.ds(w0, 3 * BLK), :] += _dot(p_b.astype(MXU), dos, TN)
        t = -(p_s * delta)
        lane = lax.broadcasted_iota(jnp.int32, (8, LANE), 1)
        part = jnp.zeros((8, LANE), F32)
        for g in range(grp):
            part = jnp.where(lane == g, jnp.sum(jnp.where(gi == g, t, 0.0)), part)
        _acc_first(ds_ref, part, qb)

    return pl.pallas_call(
        body, name=name, grid=(n_kv, nb),
        in_specs=[pl.BlockSpec(memory_space=pltpu.SMEM), pl.BlockSpec((BLK, grp * hd), lambda h, i: (i, h)),
                  pl.BlockSpec((T, hd), lambda h, i: (0, h)), pl.BlockSpec((T, hd), lambda h, i: (0, h)),
                  pl.BlockSpec((BLK, grp * hd), lambda h, i: (i, h))],
        out_specs=[pl.BlockSpec((BLK, grp * hd), lambda h, i: (i, h)), pl.BlockSpec((T, hd), lambda h, i: (0, h)),
                   pl.BlockSpec((T, hd), lambda h, i: (0, h)), pl.BlockSpec((8, LANE), lambda h, i: (h, 0))],
        out_shape=[jax.ShapeDtypeStruct((T, n_q * hd), F32), jax.ShapeDtypeStruct((T, n_kv * hd), F32),
                   jax.ShapeDtypeStruct((T, n_kv * hd), F32), jax.ShapeDtypeStruct((n_kv * 8, LANE), F32)],
        compiler_params=_cparams(("parallel", "arbitrary"), 4 * _nbytes((T, hd), MXU) + 4 * _nbytes((T, hd), F32)
                                 + 40 * _nbytes((grp * BLK, 5 * BLK), F32)),
    )(sinks, qr, kr, vb, do)


def _chunk_order(s, n_chunks, n_ctx, rev):
    if not rev:
        return s
    return jnp.where(s < n_ctx, n_ctx - 1 - s, n_chunks - 1 + n_ctx - s)


def _softplus(x):
    return jnp.maximum(x, 0.0) + jnp.log(1.0 + jnp.exp(-jnp.abs(x)))


def _expand_matrix(n_heads, p, rev):
    e = np.zeros((LANE, n_heads * p), np.float32)
    for h in range(n_heads):
        e[h + (n_heads if rev else 0), h * p:(h + 1) * p] = 1.0
    return jnp.asarray(e, BF16)


def _ssd_chunk_prep(dt_ref, dtb_ref, alog_ref, e_ref, rev):
    dt = _softplus(dt_ref[...] + dtb_ref[...])
    a = -jnp.exp(alog_ref[...])
    li = lax.broadcasted_iota(jnp.int32, (BLK, BLK), 0)
    si = lax.broadcasted_iota(jnp.int32, (BLK, BLK), 1)
    tri = (si >= li) if rev else (si <= li)
    acs = _dot3_rhs(tri.astype(BF16), a * dt)
    ev = e_ref[...]
    return dt, a, tri, acs, _dot2_lhs(dt, ev), _dot2_lhs(acs, ev)


def _pair_cols(ap):
    lane = lax.broadcasted_iota(jnp.int32, ap.shape, 1)
    apr = pltpu.roll(ap, LANE // 2, 1)
    return jnp.where(lane < LANE // 2, ap, apr), jnp.where(lane < LANE // 2, apr, ap)


def _ssd_scan(xbc, zx, dtb, alog, emat, n_heads, n_ctx, rev, name):
    T, C = xbc.shape
    P = emat.shape[1] // n_heads
    DI = n_heads * P
    GN = (C - DI) // 2
    N = GN // SSD_GROUPS
    n_pairs = DI // LANE
    ppg = n_pairs // SSD_GROUPS
    n_chunks = T // BLK
    hoff = n_heads if rev else 0
    last = 0 if rev else BLK - 1
    dt_blk = zx.shape[1] // LANE - 1
    assert N == LANE and 2 * P == LANE and 2 * n_heads == LANE

    def body(xs_ref, b_ref, c_ref, dt_ref, dtb_ref, alog_ref, e_ref, y_ref, hin_ref, state_ref, xdt_s, xdec_s, aexp_s, at_s):
        s = pl.program_id(0)

        @pl.when(s == 0)
        def _():
            state_ref[...] = jnp.zeros(state_ref.shape, F32)

        dt, a, tri, acs, dtexp, aexp = _ssd_chunk_prep(dt_ref, dtb_ref, alog_ref, e_ref, rev)
        at_s[...] = acs.T
        aexp_s[...] = aexp
        xdt = xs_ref[...] * dtexp
        xdt_s[...] = xdt.astype(MXU)
        xdec_s[...] = (xdt * jnp.exp(aexp[last:last + 1, :] - aexp)).astype(MXU)
        hin_ref[0] = state_ref[...]
        lane = lax.broadcasted_iota(jnp.int32, (BLK, LANE), 1)

        def pair(k, carry):
            col = pl.multiple_of(k * LANE, LANE)
            gcol = pl.multiple_of((k // ppg) * N, N)
            bg = b_ref[:, pl.ds(gcol, N)].astype(MXU)
            cg = c_ref[:, pl.ds(gcol, N)].astype(MXU)
            cb = _dot(cg, bg, NT)
            ap = aexp_s[:, pl.ds(col, LANE)]
            ac0, ac1 = _pair_cols(ap)
            ar0 = at_s[pl.ds(2 * k + hoff, 1), :]
            ar1 = at_s[pl.ds(2 * k + 1 + hoff, 1), :]
            m0 = (cb * jnp.exp(jnp.where(tri, ac0 - ar0, -jnp.inf))).astype(MXU)
            m1 = (cb * jnp.exp(jnp.where(tri, ac1 - ar1, -jnp.inf))).astype(MXU)
            xp = xdt_s[:, pl.ds(col, LANE)]
            zero = jnp.zeros_like(xp)
            xbd = jnp.concatenate([jnp.where(lane < LANE // 2, xp, zero), jnp.where(lane >= LANE // 2, xp, zero)], axis=0)
            yd = _dot(jnp.concatenate([m0, m1], axis=1), xbd)
            ht = state_ref[k]
            yo = _dot(cg, ht.astype(MXU)) * jnp.exp(ap)
            y_ref[:, pl.ds(col, LANE)] = yd + yo
            st = _dot(bg, xdec_s[:, pl.ds(col, LANE)], TN)
            state_ref[k] = jnp.exp(aexp_s[pl.ds(last, 1), pl.ds(col, LANE)]) * ht + st
            return carry

        lax.fori_loop(0, n_pairs, pair, 0, unroll=2)

    order = lambda s: _chunk_order(s, n_chunks, n_ctx, rev)
    return pl.pallas_call(
        body, name=name, grid=(n_chunks,),
        in_specs=[pl.BlockSpec((BLK, DI), lambda s: (order(s), 0)),
                  pl.BlockSpec((BLK, GN), lambda s: (order(s), DI // GN)),
                  pl.BlockSpec((BLK, GN), lambda s: (order(s), DI // GN + 1)),
                  pl.BlockSpec((BLK, LANE), lambda s: (order(s), dt_blk)),
                  pl.BlockSpec((1, LANE), lambda s: (0, 0)), pl.BlockSpec((1, LANE), lambda s: (0, 0)),
                  pl.BlockSpec((LANE, DI), lambda s: (0, 0))],
        out_specs=[pl.BlockSpec((BLK, DI), lambda s: (order(s), 0)),
                   pl.BlockSpec((1, n_pairs, N, LANE), lambda s: (order(s), 0, 0, 0))],
        out_shape=[jax.ShapeDtypeStruct((T, DI), F32), jax.ShapeDtypeStruct((n_chunks, n_pairs, N, LANE), F32)],
        scratch_shapes=[pltpu.VMEM((n_pairs, N, LANE), F32), pltpu.VMEM((BLK, DI), MXU), pltpu.VMEM((BLK, DI), MXU),
                        pltpu.VMEM((BLK, DI), F32), pltpu.VMEM((LANE, BLK), F32)],
        compiler_params=_cparams(("arbitrary",), 20 * _nbytes((BLK, DI), F32)),
    )(xbc, xbc, xbc, zx, dtb, alog, emat)


def _ssd_scan_bwd(xbc, zx, dtb, alog, emat, emat_t, dexp, hin, dy, acc, n_heads, n_ctx, rev, name):
    T, C = xbc.shape
    P = emat.shape[1] // n_heads
    DI = n_heads * P
    GN = (C - DI) // 2
    N = GN // SSD_GROUPS
    n_pairs = DI // LANE
    ppg = n_pairs // SSD_GROUPS
    n_chunks = T // BLK
    hoff = n_heads if rev else 0
    last = 0 if rev else BLK - 1
    dt_blk = zx.shape[1] // LANE - 1
    has_acc = acc is not None

    def body(*refs):
        (xs_ref, b_ref, c_ref, dt_ref, dtb_ref, alog_ref, e_ref, et_ref, dexp_ref, hin_ref, dy_ref) = refs[:11]
        n_in = 11
        if has_acc:
            dxbc_in, ddt_in, sums_in = refs[11:14]
            n_in = 14
        dxbc_ref, ddt_ref, sums_ref = refs[n_in:n_in + 3]
        dstate_ref, xdt_s, xdec_s, aexp_s, at_s, dyd_s, z1_s, z3_s, dxdt_s, cdrow_s, rmat_s, cst_s = refs[n_in + 3:]
        s = pl.program_id(0)

        @pl.when(s == 0)
        def _():
            dstate_ref[...] = jnp.zeros(dstate_ref.shape, F32)

        dt, a, tri, acs, dtexp, aexp = _ssd_chunk_prep(dt_ref, dtb_ref, alog_ref, e_ref, rev)
        at_s[...] = acs.T
        aexp_s[...] = aexp
        xsv = xs_ref[...]
        xdt = xsv * dtexp
        xdt_s[...] = xdt.astype(MXU)
        decend = jnp.exp(aexp[last:last + 1, :] - aexp)
        xdec_s[...] = (xdt * decend).astype(MXU)
        dyv = dy_ref[...]
        dyd_s[...] = (dyv * jnp.exp(aexp)).astype(MXU)
        dxbc_ref[:, DI:] = jnp.zeros((BLK, 2 * GN), F32)
        rmat_s[...] = jnp.zeros(rmat_s.shape, F32)
        lane = lax.broadcasted_iota(jnp.int32, (BLK, LANE), 1)
        lo = lane < LANE // 2
        tri_t = (lax.broadcasted_iota(jnp.int32, (BLK, BLK), 0) <= lax.broadcasted_iota(jnp.int32, (BLK, BLK), 1)) if not rev \
            else (lax.broadcasted_iota(jnp.int32, (BLK, BLK), 0) >= lax.broadcasted_iota(jnp.int32, (BLK, BLK), 1))

        def pair(k, carry):
            col = pl.multiple_of(k * LANE, LANE)
            gcol = pl.multiple_of((k // ppg) * N, N)
            bg = b_ref[:, pl.ds(gcol, N)].astype(MXU)
            cg = c_ref[:, pl.ds(gcol, N)].astype(MXU)
            cb = _dot(cg, bg, NT)
            cbt = _dot(bg, cg, NT)
            ap = aexp_s[:, pl.ds(col, LANE)]
            ac0, ac1 = _pair_cols(ap)
            ar0 = at_s[pl.ds(2 * k + hoff, 1), :]
            ar1 = at_s[pl.ds(2 * k + 1 + hoff, 1), :]
            seg0 = jnp.exp(jnp.where(tri, ac0 - ar0, -jnp.inf))
            seg1 = jnp.exp(jnp.where(tri, ac1 - ar1, -jnp.inf))
            segt0 = jnp.exp(jnp.where(tri_t, ar0 - ac0, -jnp.inf))
            segt1 = jnp.exp(jnp.where(tri_t, ar1 - ac1, -jnp.inf))
            dyp = dy_ref[:, pl.ds(col, LANE)].astype(MXU)
            zero = jnp.zeros_like(dyp)
            dy0, dy1 = jnp.where(lo, dyp, zero), jnp.where(lo, zero, dyp)
            dht = dstate_ref[k]
            dhb = dht.astype(MXU)
            ht = hin_ref[0, k]
            mt = jnp.concatenate([(cbt * segt0).astype(MXU), (cbt * segt1).astype(MXU)], axis=1)
            bdh = _dot(bg, dhb)
            dec_p = jnp.exp(aexp_s[pl.ds(last, 1), pl.ds(col, LANE)] - ap)
            dxdt_s[:, pl.ds(col, LANE)] = _dot(mt, jnp.concatenate([dy0, dy1], axis=0)) + dec_p * bdh
            z3_s[:, pl.ds(col, LANE)] = bdh
            cdec = jnp.exp(aexp_s[pl.ds(last, 1), pl.ds(col, LANE)])
            dydp = dyd_s[:, pl.ds(col, LANE)]
            dstate_ref[k] = _dot(cg, dydp, TN) + cdec * dht
            cdrow_s[0:1, pl.ds(col, LANE)] = cdec * jnp.sum(dht * ht, axis=0, keepdims=True)
            z1_s[:, pl.ds(col, LANE)] = _dot(cg, ht.astype(MXU))
            dcg = _dot(dydp, ht.astype(MXU), NT)
            dbg = _dot(xdec_s[:, pl.ds(col, LANE)], dhb, NT)
            xp = xdt_s[:, pl.ds(col, LANE)]
            dg0 = _dot(dy0, xp, NT)
            dg1 = _dot(dy1, xp, NT)
            ds0, ds1 = dg0 * seg0, dg1 * seg1
            w0, w1 = ds0 * cb, ds1 * cb
            dcb = ds0 + ds1
            lane_h = lax.broadcasted_iota(jnp.int32, (BLK, LANE), 1)
            rmat_s[...] += (jnp.where(lane_h == 2 * k + hoff, jnp.sum(w0, axis=1, keepdims=True), 0.0)
                            + jnp.where(lane_h == 2 * k + 1 + hoff, jnp.sum(w1, axis=1, keepdims=True), 0.0))
            cst_s[pl.ds(2 * k + hoff, 1), :] = jnp.sum(w0, axis=0, keepdims=True)
            cst_s[pl.ds(2 * k + 1 + hoff, 1), :] = jnp.sum(w1, axis=0, keepdims=True)
            dcbb = dcb.astype(MXU)
            dxbc_ref[:, pl.ds(DI + GN + gcol, N)] += dcg + _dot(dcbb, bg)
            dxbc_ref[:, pl.ds(DI + gcol, N)] += dbg + _dot(dcbb, cg, TN)
            return carry

        cst_s[...] = jnp.zeros(cst_s.shape, F32)
        lax.fori_loop(0, n_pairs, pair, 0, unroll=2)

        etv = et_ref[...]
        dxdt = dxdt_s[...]
        z1 = _dot2_lhs(dyv * (z1_s[...] * jnp.exp(aexp)), etv)
        z2 = _dot2_lhs(dxdt * xsv, etv)
        q = _dot2_lhs(xdt * decend * z3_s[...], etv)
        cd8 = jnp.concatenate([cdrow_s[0:1, :], jnp.zeros((7, DI), F32)], axis=0)
        cdh = _dot2_lhs(cd8, etv)[0:1, :]
        dacs = rmat_s[...] - cst_s[...].T + z1 - q
        rowi = lax.broadcasted_iota(jnp.int32, (BLK, LANE), 0)
        dacs = dacs + jnp.where(rowi == last, jnp.sum(q, axis=0, keepdims=True) + cdh, 0.0)
        d_a = _dot3_rhs(tri_t.astype(BF16), dacs)
        ddt = a * d_a + z2
        x_raw = dt_ref[...] + dtb_ref[...]
        ddt_raw = ddt * jax.nn.sigmoid(x_raw)
        lane_l = lax.broadcasted_iota(jnp.int32, (BLK, LANE), 1)
        mine = (lane_l >= hoff) & (lane_l < hoff + n_heads)
        ddt_raw = jnp.where(mine, ddt_raw, 0.0)
        part = jnp.concatenate([jnp.sum(jnp.where(mine, dt * d_a, 0.0), axis=0, keepdims=True) * a,
                                jnp.sum(ddt_raw, axis=0, keepdims=True), jnp.zeros((6, LANE), F32)], axis=0)
        dxs = dxdt * dtexp
        if has_acc:
            dxbc_ref[:, 0:DI] = dxs + dxbc_in[:, 0:DI]
            dxbc_ref[:, DI:] += dxbc_in[:, DI:]
            ddt_ref[...] = ddt_raw + ddt_in[...]
            part = part + jnp.where(s == 0, sums_in[...], 0.0)
        else:
            dxbc_ref[:, 0:DI] = dxs + dyv * dexp_ref[...]
            ddt_ref[...] = ddt_raw
        _acc_first(sums_ref, part, s)

    order = lambda s: _chunk_order(n_chunks - 1 - s, n_chunks, n_ctx, rev)
    in_specs = [pl.BlockSpec((BLK, DI), lambda s: (order(s), 0)),
                pl.BlockSpec((BLK, GN), lambda s: (order(s), DI // GN)),
                pl.BlockSpec((BLK, GN), lambda s: (order(s), DI // GN + 1)),
                pl.BlockSpec((BLK, LANE), lambda s: (order(s), dt_blk)),
                pl.BlockSpec((1, LANE), lambda s: (0, 0)), pl.BlockSpec((1, LANE), lambda s: (0, 0)),
                pl.BlockSpec((LANE, DI), lambda s: (0, 0)), pl.BlockSpec((DI, LANE), lambda s: (0, 0)),
                pl.BlockSpec((1, DI), lambda s: (0, 0)),
                pl.BlockSpec((1, n_pairs, N, LANE), lambda s: (order(s), 0, 0, 0)),
                pl.BlockSpec((BLK, DI), lambda s: (order(s), 0))]
    args = [xbc, xbc, xbc, zx, dtb, alog, emat, emat_t, dexp, hin, dy]
    if has_acc:
        in_specs += [pl.BlockSpec((BLK, C), lambda s: (order(s), 0)), pl.BlockSpec((BLK, LANE), lambda s: (order(s), 0)),
                     pl.BlockSpec((8, LANE), lambda s: (0, 0))]
        args += list(acc)
    return pl.pallas_call(
        body, name=name, grid=(n_chunks,),
        in_specs=in_specs,
        out_specs=[pl.BlockSpec((BLK, C), lambda s: (order(s), 0)), pl.BlockSpec((BLK, LANE), lambda s: (order(s), 0)),
                   pl.BlockSpec((8, LANE), lambda s: (0, 0))],
        out_shape=[jax.ShapeDtypeStruct((T, C), F32), jax.ShapeDtypeStruct((T, LANE), F32), jax.ShapeDtypeStruct((8, LANE), F32)],
        scratch_shapes=[pltpu.VMEM((n_pairs, N, LANE), F32), pltpu.VMEM((BLK, DI), MXU), pltpu.VMEM((BLK, DI), MXU),
                        pltpu.VMEM((BLK, DI), F32), pltpu.VMEM((LANE, BLK), F32), pltpu.VMEM((BLK, DI), MXU),
                        pltpu.VMEM((BLK, DI), F32), pltpu.VMEM((BLK, DI), F32), pltpu.VMEM((BLK, DI), F32),
                        pltpu.VMEM((8, DI), F32), pltpu.VMEM((BLK, LANE), F32), pltpu.VMEM((LANE, BLK), F32)],
        compiler_params=_cparams(("arbitrary",), 36 * _nbytes((BLK, DI), F32)),
    )(*args)


def _ssd_finish(yf, yb, xbc, zx, dexp, nw, name, tr=256):
    T, DI = yf.shape
    tr = min(tr, T)

    def body(yf_ref, yb_ref, xs_ref, z_ref, d_ref, nw_ref, o_ref):
        y = yf_ref[...] + yb_ref[...] + xs_ref[...] * d_ref[...]
        gt = y * _silu(z_ref[...])
        r = lax.rsqrt(jnp.mean(gt * gt, axis=-1, keepdims=True) + NORM_EPS)
        o_ref[...] = ((gt * r) * nw_ref[...]).astype(o_ref.dtype)

    row = pl.BlockSpec((tr, DI), lambda i: (i, 0))
    vec = pl.BlockSpec((1, DI), lambda i: (0, 0))
    return pl.pallas_call(
        body, name=name, grid=(T // tr,), in_specs=[row, row, row, row, vec, vec], out_specs=row,
        out_shape=jax.ShapeDtypeStruct((T, DI), MXU),
        compiler_params=_cparams(("parallel",), 16 * _nbytes((tr, DI), F32)),
    )(yf, yb, xbc, zx, dexp, nw)


def _ssd_finish_bwd(yf, yb, xbc, zx, dexp, nw, do, emat_t, dzx_shape, name, tr=64):
    T, DI = yf.shape
    tr = min(tr, T)
    n_steps = T // tr

    def body(yf_ref, yb_ref, xs_ref, z_ref, d_ref, nw_ref, do_ref, et_ref, dy_ref, dz_ref, sums_ref, dd_ref):
        i = pl.program_id(0)
        xs = xs_ref[...]
        zv = z_ref[...]
        y = yf_ref[...] + yb_ref[...] + xs * d_ref[...]
        sz = _silu(zv)
        gt = y * sz
        r = lax.rsqrt(jnp.mean(gt * gt, axis=-1, keepdims=True) + NORM_EPS)
        gn = gt * r
        dov = do_ref[...].astype(F32)
        dgn = dov * nw_ref[...]
        dgt = r * (dgn - gn * jnp.mean(dgn * gn, axis=-1, keepdims=True))
        dy = dgt * sz
        dy_ref[...] = dy
        dz_ref[...] = (dgt * y * _dsilu(zv)).astype(dz_ref.dtype)
        part = jnp.concatenate([jnp.sum(dov * gn, axis=0, keepdims=True), jnp.sum(dy * xs, axis=0, keepdims=True),
                                jnp.zeros((6, DI), F32)], axis=0)
        _acc_first(sums_ref, part, i)

        @pl.when(i == n_steps - 1)
        def _():
            dd_ref[...] = _dot3_lhs(sums_ref[...], et_ref[...])

    row = pl.BlockSpec((tr, DI), lambda i: (i, 0))
    vec = pl.BlockSpec((1, DI), lambda i: (0, 0))
    return pl.pallas_call(
        body, name=name, grid=(n_steps,),
        in_specs=[row, row, row, row, vec, vec, row, pl.BlockSpec((DI, LANE), lambda i: (0, 0))],
        out_specs=[row, row, pl.BlockSpec((8, DI), lambda i: (0, 0)), pl.BlockSpec((8, LANE), lambda i: (0, 0))],
        out_shape=[jax.ShapeDtypeStruct((T, DI), F32), jax.ShapeDtypeStruct(dzx_shape, MXU), jax.ShapeDtypeStruct((8, DI), F32),
                   jax.ShapeDtypeStruct((8, LANE), F32)],
        compiler_params=_cparams(("arbitrary",), 40 * _nbytes((tr, DI), F32)),
    )(yf, yb, xbc, zx, dexp, nw, do, emat_t)


def _loss_head(xf, tgt, ctx_rows, name, tr=256):
    T, D = xf.shape
    tr = min(tr, ctx_rows)
    n_ctx = ctx_rows // tr

    def body(x_ref, t_ref, dx_ref, l_ref):
        i = pl.program_id(0)

        @pl.when(i < n_ctx)
        def _():
            dx_ref[...] = jnp.zeros(dx_ref.shape, F32)

        @pl.when(i == 0)
        def _():
            l_ref[...] = jnp.zeros(l_ref.shape, F32)

        @pl.when(i >= n_ctx)
        def _():
            e = x_ref[...] - t_ref[...]
            dx_ref[...] = e * (1.0 / D)
            l_ref[...] += 0.5 * jnp.sum(jnp.mean(e * e, axis=-1, keepdims=True))

    return pl.pallas_call(
        body, name=name, grid=(T // tr,),
        in_specs=[pl.BlockSpec((tr, D), lambda i: (i, 0)), pl.BlockSpec((tr, D), lambda i: (jnp.maximum(i - n_ctx, 0), 0))],
        out_specs=[pl.BlockSpec((tr, D), lambda i: (i, 0)), pl.BlockSpec((8, LANE), lambda i: (0, 0))],
        out_shape=[jax.ShapeDtypeStruct((T, D), F32), jax.ShapeDtypeStruct((8, LANE), F32)],
        compiler_params=_cparams(("arbitrary",), 10 * _nbytes((tr, D), F32)),
    )(xf, tgt)


def _adamw_math(w, g, m, v):
    m2 = ADAM_B1 * m + (1.0 - ADAM_B1) * g
    v2 = ADAM_B2 * v + (1.0 - ADAM_B2) * (g * g)
    m_hat = m2 / (1.0 - ADAM_B1 ** ADAM_STEP)
    v_hat = v2 / (1.0 - ADAM_B2 ** ADAM_STEP)
    delta = -ADAM_LR * (m_hat / (jnp.sqrt(v_hat) + ADAM_EPS) + ADAM_WD * w)
    return delta, m2, v2


def _row_tile(rows, target):
    if rows <= target:
        return rows
    t = target - target % 8
    while rows % t:
        t -= 8
    return t


def _adamw(parts, w, m, v, name, tr=128):
    n, L, rows, cols = parts.shape
    tr = _row_tile(rows, tr)

    def body(p_ref, w_ref, m_ref, v_ref, g_ref, d_ref, m2_ref, v2_ref):
        g = p_ref[0, 0].astype(F32)
        for q in range(1, n):
            g = g + p_ref[q, 0].astype(F32)
        d, m2, v2 = _adamw_math(w_ref[0], g, m_ref[0], v_ref[0])
        g_ref[0], d_ref[0], m2_ref[0], v2_ref[0] = g, d, m2, v2

    blk = pl.BlockSpec((1, tr, cols), lambda l, i: (l, i, 0))
    shp = jax.ShapeDtypeStruct((L, rows, cols), F32)
    return pl.pallas_call(
        body, name=name, grid=(L, rows // tr),
        in_specs=[pl.BlockSpec((n, 1, tr, cols), lambda l, i: (0, l, i, 0)), blk, blk, blk],
        out_specs=[blk, blk, blk, blk], out_shape=[shp, shp, shp, shp],
        compiler_params=_cparams(("parallel", "parallel"), 2 * (n + 8) * _nbytes((tr, cols), F32)),
    )(parts, w, m, v)


def _adamw_small(bufs, w, m, v, name):
    n, R, _ = bufs.shape

    def body(b_ref, w_ref, m_ref, v_ref, g_ref, d_ref, m2_ref, v2_ref):
        g = b_ref[0]
        for q in range(1, n):
            g = g + b_ref[q]
        d, m2, v2 = _adamw_math(w_ref[...], g, m_ref[...], v_ref[...])
        g_ref[...], d_ref[...], m2_ref[...], v2_ref[...] = g, d, m2, v2

    vm = pl.BlockSpec(memory_space=pltpu.VMEM)
    shp = jax.ShapeDtypeStruct((R, LANE), F32)
    return pl.pallas_call(body, name=name, in_specs=[vm, vm, vm, vm], out_specs=[vm, vm, vm, vm],
                          out_shape=[shp, shp, shp, shp],
                          compiler_params=pltpu.CompilerParams(vmem_limit_bytes=32 * 1024 * 1024))(bufs, w, m, v)


def _add_blocks(g, r1, c_idx, name, tr=128):
    _, rows, cols = g.shape
    tr = _row_tile(rows, tr)

    def body(c_ref, g_ref, r_ref, p_ref):
        del c_ref
        p_ref[...] = (g_ref[...] + r_ref[...]).astype(p_ref.dtype)

    return pl.pallas_call(
        body, name=name,
        grid_spec=pltpu.PrefetchScalarGridSpec(
            num_scalar_prefetch=1, grid=(NDEV // 2, rows // tr),
            in_specs=[pl.BlockSpec((1, tr, cols), lambda q, i, c: (2 * q + c[0], i, 0)),
                      pl.BlockSpec((1, tr, cols), lambda q, i, c: (q, i, 0))],
            out_specs=pl.BlockSpec((1, tr, cols), lambda q, i, c: (q, i, 0))),
        out_shape=jax.ShapeDtypeStruct((NDEV // 2, rows, cols), XFER),
        compiler_params=_cparams(("parallel", "parallel"), 8 * _nbytes((tr, cols), F32)),
    )(c_idx, g, r1)


def _me():
    return lax.axis_index("x"), lax.axis_index("y"), lax.axis_index("c")


def _flip(v, bit):
    return 1 - v if bit else v


def _peer(k):
    x, y, c = _me()
    return _flip(x, (k >> 2) & 1), _flip(y, (k >> 1) & 1), _flip(c, k & 1)


def _dev_index(p):
    return 4 * p[0] + 2 * p[1] + p[2]


def _chip_index(p):
    return 2 * p[0] + p[1]


def _small_allgather(v, name):
    R, C = v.shape

    def body(v_ref, out_ref, send_sems, recv_sems, loc_sem):
        me = _dev_index(_me())
        mine = pltpu.make_async_copy(v_ref, out_ref.at[me], loc_sem)
        mine.start()
        sends = []
        for k in range(1, NDEV):
            cp = pltpu.make_async_remote_copy(src_ref=v_ref, dst_ref=out_ref.at[me], send_sem=send_sems.at[k - 1],
                                              recv_sem=recv_sems.at[k - 1], device_id=_peer(k), device_id_type=MESH_ID)
            cp.start()
            sends.append(cp)
        for k in range(1, NDEV):
            pltpu.make_async_remote_copy(src_ref=v_ref, dst_ref=out_ref.at[_dev_index(_peer(k))], send_sem=send_sems.at[k - 1],
                                         recv_sem=recv_sems.at[k - 1], device_id=_peer(k), device_id_type=MESH_ID).wait_recv()
        for cp in sends:
            cp.wait_send()
        mine.wait()

    vm = pl.BlockSpec(memory_space=pltpu.VMEM)
    return pl.pallas_call(
        body, name=name, in_specs=[vm], out_specs=vm, out_shape=jax.ShapeDtypeStruct((NDEV, R, C), F32),
        scratch_shapes=[pltpu.SemaphoreType.DMA((NDEV - 1,)), pltpu.SemaphoreType.DMA((NDEV - 1,)), pltpu.SemaphoreType.DMA(())],
        compiler_params=pltpu.CompilerParams(vmem_limit_bytes=48 * 1024 * 1024),
    )(v)


def _allgather_weights(arrs, name):
    n = len(arrs)

    def body(*refs):
        x_refs, out_refs = refs[:n], refs[n:2 * n]
        send_sems, recv_sems, loc_sems = refs[2 * n:]
        x, y, c = _me()
        me, sib = (x, y, c), (x, y, 1 - c)
        chips = [(1 - x, y), (x, 1 - y), (1 - x, 1 - y)]

        def copy(a, k, block, to, src=None):
            dst = out_refs[a].at[_dev_index(block)]
            return pltpu.make_async_remote_copy(src_ref=dst if src is None else src, dst_ref=dst, send_sem=send_sems.at[a, k],
                                                recv_sem=recv_sems.at[a, k], device_id=to, device_id_type=MESH_ID)

        locs, first, passed = [], [], []
        for a in range(n):
            lc = pltpu.make_async_copy(x_refs[a], out_refs[a].at[_dev_index(me)], loc_sems.at[a])
            lc.start()
            locs.append(lc)
            cps = [copy(a, 0, me, sib, src=x_refs[a])] + [copy(a, 1 + j, me, (*chip, c), src=x_refs[a]) for j, chip in enumerate(chips)]
            for cp in cps:
                cp.start()
            first += cps
        for j, chip in enumerate(chips):
            for a in range(n):
                copy(a, 1 + j, (*chip, c), me).wait_recv()
                cp = copy(a, 4 + j, (*chip, c), sib)
                cp.start()
                passed.append(cp)
        for a in range(n):
            copy(a, 0, sib, me).wait_recv()
            for j, chip in enumerate(chips):
                copy(a, 4 + j, (*chip, 1 - c), me).wait_recv()
        for cp in first + passed:
            cp.wait_send()
        for lc in locs:
            lc.wait()

    hbm = pl.BlockSpec(memory_space=pl.ANY)
    return pl.pallas_call(
        body, name=name, in_specs=[hbm] * n, out_specs=[hbm] * n,
        out_shape=[jax.ShapeDtypeStruct((NDEV,) + a.shape, a.dtype) for a in arrs],
        scratch_shapes=[pltpu.SemaphoreType.DMA((n, 7)), pltpu.SemaphoreType.DMA((n, 7)), pltpu.SemaphoreType.DMA((n,))],
    )(*arrs)


def _rs_sibling(gs, name):
    n = len(gs)

    def body(*refs):
        g_refs, r_refs = refs[:n], refs[n:2 * n]
        send_sems, recv_sems = refs[2 * n:]
        x, y, c = _me()
        sib = (x, y, 1 - c)
        sends = []
        for a in range(n):
            for q in range(NDEV // 2):
                cp = pltpu.make_async_remote_copy(src_ref=g_refs[a].at[2 * q + (1 - c)], dst_ref=r_refs[a].at[q],
                                                  send_sem=send_sems.at[a, q], recv_sem=recv_sems.at[a, q],
                                                  device_id=sib, device_id_type=MESH_ID)
                cp.start()
                sends.append(cp)
        for cp in sends:
            cp.wait_recv()
        for cp in sends:
            cp.wait_send()

    hbm = pl.BlockSpec(memory_space=pl.ANY)
    return pl.pallas_call(
        body, name=name, in_specs=[hbm] * n, out_specs=[hbm] * n,
        out_shape=[jax.ShapeDtypeStruct((NDEV // 2,) + g.shape[1:], g.dtype) for g in gs],
        scratch_shapes=[pltpu.SemaphoreType.DMA((n, NDEV // 2)), pltpu.SemaphoreType.DMA((n, NDEV // 2))],
    )(*gs)


def _rs_chips(ps, groups, name):
    n = len(ps)
    where = {}
    for gi, grp in enumerate(groups):
        for li, a in enumerate(grp):
            where[a] = (gi, li)
    ng = len(groups)

    def body(*refs):
        p_refs, r_refs = refs[:n], refs[n:n + ng]
        send_sems, recv_sems, loc_sems = refs[n + ng:]
        x, y, c = _me()
        mychip = _chip_index((x, y))
        chips = [(1 - x, y), (x, 1 - y), (1 - x, 1 - y)]
        sends, locs = [], []
        for a in range(n):
            gi, li = where[a]
            lc = pltpu.make_async_copy(p_refs[a].at[mychip], r_refs[gi].at[mychip, li], loc_sems.at[a])
            lc.start()
            locs.append(lc)
            for j, chip in enumerate(chips):
                cp = pltpu.make_async_remote_copy(src_ref=p_refs[a].at[_chip_index(chip)], dst_ref=r_refs[gi].at[mychip, li],
                                                  send_sem=send_sems.at[a, j], recv_sem=recv_sems.at[a, j],
                                                  device_id=(*chip, c), device_id_type=MESH_ID)
                cp.start()
                sends.append(cp)
        for a in range(n):
            gi, li = where[a]
            for j, chip in enumerate(chips):
                pltpu.make_async_remote_copy(src_ref=p_refs[a].at[mychip], dst_ref=r_refs[gi].at[_chip_index(chip), li],
                                             send_sem=send_sems.at[a, j], recv_sem=recv_sems.at[a, j],
                                             device_id=(*chip, c), device_id_type=MESH_ID).wait_recv()
        for cp in sends:
            cp.wait_send()
        for lc in locs:
            lc.wait()

    hbm = pl.BlockSpec(memory_space=pl.ANY)
    return pl.pallas_call(
        body, name=name, in_specs=[hbm] * n, out_specs=[hbm] * ng,
        out_shape=[jax.ShapeDtypeStruct((NDEV // 2, len(grp)) + ps[grp[0]].shape[1:], ps[grp[0]].dtype) for grp in groups],
        scratch_shapes=[pltpu.SemaphoreType.DMA((n, 3)), pltpu.SemaphoreType.DMA((n, 3)), pltpu.SemaphoreType.DMA((n,))],
    )(*ps)


HBM_SPEC = pl.BlockSpec(memory_space=pltpu.HBM)
SEM_SPEC = pl.BlockSpec(memory_space=pltpu.SEMAPHORE)
DATAFLOW = pltpu.SideEffectType.DATAFLOW_SIDE_EFFECTING


def _xfer_copy(src_ref, land_ref, sems, a, k, layer, scatter, arriving):
    send_sems, recv_sems = sems
    me, peer = _dev_index(_me()), _dev_index(_peer(k))
    src = src_ref.at[peer] if scatter else src_ref
    slot = peer if arriving else me
    dst = land_ref.at[slot] if layer is None else land_ref.at[slot, layer]
    return pltpu.make_async_remote_copy(src_ref=src, dst_ref=dst, send_sem=send_sems.at[a * (NDEV - 1) + k - 1],
                                        recv_sem=recv_sems.at[a * (NDEV - 1) + k - 1], device_id=_peer(k),
                                        device_id_type=MESH_ID)


def _xfer_start(srcs, lands, layers, scatter, name):
    n = len(srcs)

    def body(*refs):
        src_refs, land_refs = refs[:n], refs[n:2 * n]
        sems = refs[2 * n], refs[2 * n + 1]
        token = refs[-1]
        for a in range(n):
            for k in range(1, NDEV):
                _xfer_copy(src_refs[a], land_refs[a], sems, a, k, layers[a], scatter, False).start()
        token[...] = jnp.zeros(token.shape, token.dtype)

    ops = [pltpu.with_memory_space_constraint(t, pltpu.HBM) for t in list(srcs) + list(lands)]
    n_sem = n * (NDEV - 1)
    res = pl.pallas_call(
        body, name=name,
        out_shape=(pltpu.SemaphoreType.DMA((n_sem,)), pltpu.SemaphoreType.DMA((n_sem,)),
                   *[pltpu.HBM(t.shape, t.dtype) for t in ops], jax.ShapeDtypeStruct((8, LANE), F32)),
        in_specs=[HBM_SPEC] * (2 * n),
        out_specs=(SEM_SPEC, SEM_SPEC, *[HBM_SPEC] * (2 * n), pl.BlockSpec(memory_space=pltpu.VMEM)),
        input_output_aliases={i: 2 + i for i in range(2 * n)},
        compiler_params=pltpu.CompilerParams(has_side_effects=DATAFLOW),
    )(*ops)
    return res[0], res[1], list(res[2:2 + n]), list(res[2 + n:2 + 2 * n]), res[-1]


def _xfer_wait(send_sems, recv_sems, srcs, lands, layers, scatter, after, name):
    n = len(srcs)

    def body(*refs):
        src_refs, land_refs = refs[:n], refs[n:2 * n]
        sems = refs[2 * n], refs[2 * n + 1]
        for a in range(n):
            for k in range(1, NDEV):
                _xfer_copy(src_refs[a], land_refs[a], sems, a, k, layers[a], scatter, False).wait_send()
                _xfer_copy(src_refs[a], land_refs[a], sems, a, k, layers[a], scatter, True).wait_recv()

    ops = list(srcs) + list(lands)
    res = pl.pallas_call(
        body, name=name,
        out_shape=tuple(pltpu.HBM(t.shape, t.dtype) for t in ops),
        in_specs=[HBM_SPEC] * (2 * n) + [SEM_SPEC, SEM_SPEC, pl.BlockSpec(memory_space=pl.ANY)],
        out_specs=tuple([HBM_SPEC] * (2 * n)),
        input_output_aliases={i: i for i in range(2 * n)},
        compiler_params=pltpu.CompilerParams(has_side_effects=DATAFLOW),
    )(*ops, send_sems, recv_sems, after)
    return list(res[n:])


HI = lax.Precision.HIGHEST
MOD_ROWS = 16


def _col_tile(n, target=512):
    return target if n % target == 0 else n


def _modulation(s_in, ada_w, b_loc, name):
    L, D, nl = ada_w.shape
    tn = _col_tile(nl)

    def body(s_ref, w_ref, b_ref, o_ref):
        o_ref[0] = jnp.dot(_silu(s_ref[...]), w_ref[0], preferred_element_type=F32, precision=HI) + b_ref[0]

    return pl.pallas_call(
        body, name=name, grid=(L, nl // tn),
        in_specs=[pl.BlockSpec((MOD_ROWS, D), lambda l, j: (0, 0)), pl.BlockSpec((1, D, tn), lambda l, j: (l, 0, j)),
                  pl.BlockSpec((1, 1, tn), lambda l, j: (l, 0, j))],
        out_specs=pl.BlockSpec((1, MOD_ROWS, tn), lambda l, j: (l, 0, j)),
        out_shape=jax.ShapeDtypeStruct((L, MOD_ROWS, nl), F32),
        compiler_params=_cparams(("parallel", "parallel"), 4 * _nbytes((D, tn), F32)),
    )(s_in, ada_w, b_loc)


def _modulation_bwd(s_in, dml, dmc, ada_w, name):
    L, D, nl = ada_w.shape
    tn = _col_tile(nl)

    def body(s_ref, dml_ref, dmc_ref, w_ref, g_ref, pc_ref):
        l, j = pl.program_id(0), pl.program_id(1)
        a = _silu(s_ref[...])
        tot = dmc_ref[0, 0]
        for d in range(1, NDEV):
            tot = tot + dmc_ref[d, 0]
        row = lax.broadcasted_iota(jnp.int32, (MOD_ROWS, tn), 0)
        dm = jnp.where(row == NDEV, tot, dml_ref[:, 0, 0, :])
        g_ref[0] = lax.dot_general(a, dm, TN, preferred_element_type=F32, precision=HI)
        tot8 = jnp.where(lax.broadcasted_iota(jnp.int32, (8, tn), 0) == 0, tot, 0.0)
        part = lax.dot_general(tot8, w_ref[0], NT, preferred_element_type=F32, precision=HI)

        @pl.when((l == 0) & (j == 0))
        def _():
            pc_ref[...] = part

        @pl.when((l != 0) | (j != 0))
        def _():
            pc_ref[...] += part

    return pl.pallas_call(
        body, name=name, grid=(L, nl // tn),
        in_specs=[pl.BlockSpec((MOD_ROWS, D), lambda l, j: (0, 0)), pl.BlockSpec((MOD_ROWS, 1, 1, tn), lambda l, j: (0, l, 0, j)),
                  pl.BlockSpec((NDEV, 1, 1, tn), lambda l, j: (0, l, 0, j)), pl.BlockSpec((1, D, tn), lambda l, j: (l, 0, j))],
        out_specs=[pl.BlockSpec((1, D, tn), lambda l, j: (l, 0, j)), pl.BlockSpec((8, D), lambda l, j: (0, 0))],
        out_shape=[jax.ShapeDtypeStruct((L, D, nl), F32), jax.ShapeDtypeStruct((8, D), F32)],
        compiler_params=_cparams(("arbitrary", "arbitrary"), 8 * _nbytes((D, tn), F32)),
    )(s_in, dml.reshape(MOD_ROWS, L, 1, nl), dmc.reshape(NDEV, L, 1, nl), ada_w)


def _cctx_update(bufs, c_ctx, m, v, name):
    n, R, _ = bufs.shape

    def body(b_ref, w_ref, m_ref, v_ref, g_ref, d_ref, m2_ref, v2_ref):
        g = b_ref[0]
        for q in range(1, n):
            g = g + b_ref[q]
        g = g * _dsilu(w_ref[...])
        d, m2, v2 = _adamw_math(w_ref[...], g, m_ref[...], v_ref[...])
        g_ref[...], d_ref[...], m2_ref[...], v2_ref[...] = g, d, m2, v2

    vm = pl.BlockSpec(memory_space=pltpu.VMEM)
    shp = jax.ShapeDtypeStruct((R, LANE), F32)
    return pl.pallas_call(body, name=name, in_specs=[vm, vm, vm, vm], out_specs=[vm, vm, vm, vm],
                          out_shape=[shp, shp, shp, shp])(bufs, c_ctx, m, v)


def _pack(arrs):
    flat = jnp.concatenate([a.reshape(-1).astype(F32) for a in arrs])
    n = flat.shape[0]
    total = -(-n // (8 * LANE)) * (8 * LANE)
    return jnp.pad(flat, (0, total - n)).reshape(total // LANE, LANE)


def _unpack(buf, shapes):
    lead = buf.shape[:-2]
    flat = buf.reshape(lead + (-1,))
    out, off = [], 0
    for s in shapes:
        n = int(np.prod(s))
        out.append(flat[..., off:off + n].reshape(lead + tuple(s)))
        off += n
    return out


WEIGHTS = ['c_ctx', 'ada_w', 'ada_b', 'norm1_w', 'norm2_w', 'ssd_w_in', 'ssd_conv_w', 'ssd_conv_b', 'ssd_dt_bias_f',
           'ssd_dt_bias_b', 'ssd_a_log_f', 'ssd_a_log_b', 'ssd_d', 'ssd_norm_w', 'ssd_w_out', 'attn_w_qkv', 'attn_q_gain',
           'attn_k_gain', 'attn_sinks', 'attn_w_o', 'ffn_w_up', 'ffn_conv_w', 'ffn_conv_b', 'ffn_w_down']
SMALL = ['ada_b', 'norm1_w', 'norm2_w', 'ssd_conv_b', 'ssd_dt_bias_f', 'ssd_dt_bias_b', 'ssd_a_log_f', 'ssd_a_log_b', 'ssd_d',
         'ssd_norm_w', 'attn_q_gain', 'attn_k_gain', 'attn_sinks', 'ffn_conv_b']
BIG = ['ssd_w_in', 'ssd_w_out', 'attn_w_qkv', 'attn_w_o', 'ffn_w_up', 'ffn_w_down']


def _step(x, c, ctx, w, tgt, m, v):
    xi, yi, ci = _me()
    me = 4 * xi + 2 * yi + ci
    t_lat, D = x.shape[1], x.shape[2]
    ctx_rows = ctx.shape[1]
    T = ctx_rows + t_lat
    L, n_ssd, n_att = w['norm1_w'].shape[0], w['ssd_d'].shape[0], w['attn_sinks'].shape[0]
    H, DI, XBC = w['ssd_d'].shape[1], w['ssd_norm_w'].shape[1], w['ssd_conv_b'].shape[1]
    P = DI // H
    IN = w['ssd_w_in'].shape[2] * NDEV
    hd, n_q = w['attn_q_gain'].shape[1], w['attn_sinks'].shape[1]
    F2 = w['ffn_conv_b'].shape[1]
    G = F2 // NDEV
    nl = w['ada_w'].shape[2]
    ncc = ctx_rows // BLK
    perm = [_ffn_perm(s) for s in range(NDEV)]
    inv = [perm.index(d) for d in range(NDEV)]

    def reorder(t, order):
        return jnp.concatenate([t[..., o * G:(o + 1) * G] for o in order], axis=-1)

    def interleave(t):
        return reorder(t, perm)

    def deinterleave(t):
        return reorder(t, inv)

    def layer_weights(i):
        mixer = ['ssd_w_in', 'ssd_w_out'] if i % 2 == 0 else ['attn_w_qkv', 'attn_w_o']
        return [(n, i // 2) for n in mixer] + [('ffn_w_up', i), ('ffn_w_down', i)]

    def gather_start(i):
        srcs = [w[n][j].astype(MXU) for n, j in layer_weights(i)]
        lands = [lax.dynamic_update_index_in_dim(lax.empty((NDEV,) + t.shape, t.dtype), t, me, 0) for t in srcs]
        ss, rs, srcs, lands, tok = _xfer_start(srcs, lands, [None] * len(srcs), False, f"gather_start_{i}")
        return (ss, rs, srcs, lands), tok[0, 0]

    def gather_wait(pending, i, after):
        return _xfer_wait(*pending, [None] * len(pending[2]), False, after, f"gather_wait_{i}")

    pending, tok = gather_start(0)

    shapes_a = [(D,), w['ssd_conv_w'].shape, w['ffn_conv_w'].shape]
    g_a = _small_allgather(_pack([c[0], w['ssd_conv_w'], w['ffn_conv_w']]), "gather_cond")
    c_all, scw_all, fcw_all = _unpack(g_a, shapes_a)
    ssd_cw = scw_all.transpose(1, 2, 0, 3).reshape(n_ssd, 3, XBC)
    ffn_cw = jnp.concatenate([fcw_all[d] for d in perm], axis=-1)
    ffn_cb = interleave(w['ffn_conv_b'])[:, None, :]

    s_in = jnp.concatenate([c_all, w['c_ctx'][None], jnp.zeros((MOD_ROWS - NDEV - 1, D), F32)], axis=0)
    b_loc = lax.dynamic_slice(w['ada_b'], (0, me * nl), (L, nl))[:, None, :]
    mod_loc = _modulation(s_in, w['ada_w'], b_loc, "modulation")
    g_b = _small_allgather(_pack([mod_loc]), "gather_mod")
    (mod_all,) = _unpack(g_b, [mod_loc.shape])
    mod_lat = lax.dynamic_index_in_dim(mod_all, me, axis=2, keepdims=False)
    mod_ctx = mod_all[:, :, NDEV, :]
    to_mod = lambda t: t.transpose(1, 0, 2).reshape(L, 6, D)
    mod = jnp.stack([to_mod(mod_ctx), to_mod(mod_lat)], axis=1)

    w_in, w_out, w_qkv, w_o = [None] * n_ssd, [None] * n_ssd, [None] * n_att, [None] * n_att
    w_up, w_down = [None] * L, [None] * L

    cos, sin = _rope_tables(t_lat, ctx_rows, hd)
    e_f, e_b = _expand_matrix(H, P, False), _expand_matrix(H, P, True)
    et_f, et_b = e_f.T, e_b.T

    xs = jnp.concatenate([ctx[0], x[0]], axis=0)
    saved = []
    for i in range(L):
        j = i // 2
        s = dict(x0=xs)
        got = gather_wait(pending, i, mod if i == 0 else xs)
        if i % 2 == 0:
            w_in[j], w_out[j] = _cat_cols(got[0], "ssd_in_cat"), got[1].reshape(DI, D)
        else:
            w_qkv[j], w_o[j] = got[0], got[1].reshape(n_q * hd, D)
        w_up[i], w_down[i] = got[2], got[3].reshape(F2 // 2, D)
        if i + 1 < L:
            pending, tok = gather_start(i + 1)
        nw1, nw2 = w['norm1_w'][i][None] + tok, w['norm2_w'][i][None]
        s['h1'] = _normmod(xs, nw1, mod[i], 0, ctx_rows, "normmod")
        if i % 2 == 0:
            s['zx'] = _mm(s['h1'], w_in[j], tm=768, tn=1152, tk=2048, name="mm_ssd_in")
            s['cw'], s['cb'] = ssd_cw[j], w['ssd_conv_b'][j][None]
            s['xbc'] = _ssd_conv(s['zx'], s['cw'], s['cb'], DI, ctx_rows, "ssd_conv")
            s['dtb'] = jnp.concatenate([w['ssd_dt_bias_f'][j], w['ssd_dt_bias_b'][j]])[None]
            s['alog'] = jnp.concatenate([w['ssd_a_log_f'][j], w['ssd_a_log_b'][j]])[None]
            s['yf'], s['hin_f'] = _ssd_scan(s['xbc'], s['zx'], s['dtb'], s['alog'], e_f, H, ncc, False, "ssd_scan_f")
            s['yb'], s['hin_b'] = _ssd_scan(s['xbc'], s['zx'], s['dtb'], s['alog'], e_b, H, ncc, True, "ssd_scan_b")
            s['dexp'], s['snw'] = jnp.repeat(w['ssd_d'][j], P)[None], w['ssd_norm_w'][j][None]
            s['o'] = _ssd_finish(s['yf'], s['yb'], s['xbc'], s['zx'], s['dexp'], s['snw'], "ssd_finish")
            s['mix'], x1 = _mm(s['o'], w_out[j], tm=768, tn=1024, tk=2048, name="mm_ssd_out",
                               resid=xs, gate=mod[i][:, 2], ctx_rows=ctx_rows)
        else:
            s['qkv'] = _mm(s['h1'], w_qkv[j], tm=768, tn=384, tk=2048, name="mm_qkv", bslots=_ident)
            s['qg'], s['kg'] = w['attn_q_gain'][j][None], w['attn_k_gain'][j][None]
            s['qr'], s['kr'], s['vb'] = _qk_prep(s['qkv'], s['qg'], s['kg'], cos, sin, n_q, ctx_rows, "qk_prep")
            s['o'] = _attn_fwd(s['qr'], s['kr'], s['vb'], w['attn_sinks'][j], n_q, ctx_rows, "attn_fwd")
            s['mix'], x1 = _mm(s['o'], w_o[j], tm=768, tn=1024, tk=2048, name="mm_attn_out",
                               resid=xs, gate=mod[i][:, 2], ctx_rows=ctx_rows)
        s['x1'] = x1
        s['h2'] = _normmod(x1, nw2, mod[i], 1, ctx_rows, "normmod")
        s['u'] = _mm(s['h2'], w_up[i], tm=768, tn=1408, tk=2048, name="mm_ffn_up", bslots=_ffn_perm)
        s['a'] = _ffn_mid(s['u'], ffn_cw[i], ffn_cb[i], ctx_rows, "ffn_mid")
        s['f'], xs = _mm(s['a'], w_down[i], tm=768, tn=1024, tk=1408, name="mm_ffn_down",
                         resid=x1, gate=mod[i][:, 5], ctx_rows=ctx_rows)
        saved.append(s)

    dx, lacc = _loss_head(xs, tgt[0], ctx_rows, "loss_head")
    loss = lax.psum(lacc[0, 0], ("x", "y", "c"))

    gbig = {name: [None] * w[name].shape[0] for name in BIG}
    gs = {name: [None] * w[name].shape[0] for name in SMALL + ['ssd_conv_w', 'ffn_conv_w']}
    dmod = [None] * L
    lands = {n: lax.empty((NDEV,) + w[n].shape, XFER) for n in BIG}
    pending, tok = None, jnp.zeros((), F32)

    def scatter_start(i):
        keys = layer_weights(i)
        srcs = [gbig[n][j] for n, j in keys]
        for (n, j), g in zip(keys, srcs):
            own = lax.dynamic_index_in_dim(g, me, 0, keepdims=False)
            lands[n] = lax.dynamic_update_slice(lands[n], own[None, None], (me, j, 0, 0))
        ss, rs, srcs, got, t = _xfer_start(srcs, [lands[n] for n, _ in keys], [j for _, j in keys], True, f"scatter_start_{i}")
        return (ss, rs, srcs, got, keys), t[0, 0]

    def scatter_wait(p, i, after):
        ss, rs, srcs, got, keys = p
        got = _xfer_wait(ss, rs, srcs, got, [j for _, j in keys], True, after, f"scatter_wait_{i}")
        for (n, _), t in zip(keys, got):
            lands[n] = t

    for i in reversed(range(L)):
        j = i // 2
        s = saved[i]
        nw1, nw2 = w['norm1_w'][i][None], w['norm2_w'][i][None]
        dm2, dg2 = _gate_bwd(dx, s['f'], mod[i] + tok, 1, ctx_rows, "gate_bwd")
        da = _mm(dm2, w_down[i], tb=True, out_dtype=MXU, tm=768, tn=1408, tk=2048, name="mm_ffn_down_dx")
        gbig['ffn_w_down'][i] = _mm(s['a'], dm2, ta=True, out_dtype=XFER, tm=1408, tn=1024, tk=1056,
                                    name="mm_ffn_down_dw").reshape(NDEV, -1, D)
        du, gcw = _ffn_mid_bwd(s['u'], da, ffn_cw[i], ffn_cb[i], ctx_rows, "ffn_mid_bwd")
        gcw = deinterleave(gcw)
        gs['ffn_conv_w'][i], gs['ffn_conv_b'][i] = gcw[0:3], gcw[3]
        dh2 = _mm(du, w_up[i], tb=True, out_dtype=MXU, tm=768, tn=1024, tk=1408, name="mm_ffn_up_dx", bslots=_ffn_perm)
        gbig['ffn_w_up'][i] = _mm(s['h2'], du, ta=True, out_dtype=XFER, tm=1024, tn=1408, tk=1056, name="mm_ffn_up_dw",
                                  oslots=_ffn_perm)
        dx1, sums2 = _normmod_bwd(s['x1'], nw2, mod[i], dh2, dx, 1, ctx_rows, "normmod_bwd")
        dmix, dg1 = _gate_bwd(dx1, s['mix'], mod[i], 0, ctx_rows, "gate_bwd")
        if i % 2 == 0:
            do = _mm(dmix, w_out[j], tb=True, out_dtype=MXU, tm=768, tn=1024, tk=2048, name="mm_ssd_out_dx")
            gbig['ssd_w_out'][j] = _mm(s['o'], dmix, ta=True, out_dtype=XFER, tm=1024, tn=1024, tk=1056,
                                       name="mm_ssd_out_dw").reshape(NDEV, -1, D)
            dy, dzx, fs, dd = _ssd_finish_bwd(s['yf'], s['yb'], s['xbc'], s['zx'], s['dexp'], s['snw'], do, et_f, (T, IN),
                                              "ssd_finish_bwd")
            acc = _ssd_scan_bwd(s['xbc'], s['zx'], s['dtb'], s['alog'], e_f, et_f, s['dexp'], s['hin_f'], dy, None,
                                H, ncc, False, "ssd_scan_bwd_f")
            dxbc, ddt, ssm = _ssd_scan_bwd(s['xbc'], s['zx'], s['dtb'], s['alog'], e_b, et_b, s['dexp'], s['hin_b'], dy, acc,
                                           H, ncc, True, "ssd_scan_bwd_b")
            dzx, gscw = _ssd_conv_bwd(s['zx'], dxbc, dzx, s['cw'], s['cb'], DI, ctx_rows, "ssd_conv_bwd")
            dzx = _put_cols(dzx, ddt, IN // LANE - 1, "ssd_put_ddt")
            dh1 = _mm(dzx, w_in[j], tb=True, out_dtype=MXU, tm=768, tn=1024, tk=1152, name="mm_ssd_in_dx")
            dwi = _mm(s['h1'], dzx, ta=True, tm=1024, tn=1152, tk=1056, name="mm_ssd_in_dw")
            gbig['ssd_w_in'][j] = _split_cols(dwi, NDEV, XFER, "ssd_in_split")
            gs['ssd_conv_w'][j], gs['ssd_conv_b'][j] = gscw[0:3], gscw[3]
            gs['ssd_dt_bias_f'][j], gs['ssd_dt_bias_b'][j] = ssm[1, :H], ssm[1, H:]
            gs['ssd_a_log_f'][j], gs['ssd_a_log_b'][j] = ssm[0, :H], ssm[0, H:]
            gs['ssd_d'][j], gs['ssd_norm_w'][j] = dd[1, :H], fs[0]
        else:
            do = _mm(dmix, w_o[j], tb=True, out_dtype=MXU, tm=768, tn=1024, tk=2048, name="mm_attn_out_dx")
            gbig['attn_w_o'][j] = _mm(s['o'], dmix, ta=True, out_dtype=XFER, tm=1024, tn=1024, tk=1056,
                                      name="mm_attn_out_dw").reshape(NDEV, -1, D)
            dq, dk, dv, dsk = _attn_bwd(s['qr'], s['kr'], s['vb'], w['attn_sinks'][j], do, n_q, ctx_rows, "attn_bwd")
            dqkv, gg = _qk_prep_bwd(s['qkv'], s['qg'], s['kg'], cos, sin, dq, dk, dv, n_q, ctx_rows, "qk_prep_bwd")
            dh1 = _mm(dqkv, w_qkv[j], tb=True, out_dtype=MXU, tm=768, tn=1024, tk=384, name="mm_qkv_dx", bslots=_ident)
            gbig['attn_w_qkv'][j] = _mm(s['h1'], dqkv, ta=True, out_dtype=XFER, tm=1024, tn=384, tk=1056, name="mm_qkv_dw",
                                        oslots=_ident)
            gs['attn_q_gain'][j], gs['attn_k_gain'][j] = gg[0], gg[1]
            gs['attn_sinks'][j] = dsk.reshape(ATTN_KV_HEADS, 8, LANE)[:, 0, :n_q // ATTN_KV_HEADS].reshape(n_q)
        dx, sums1 = _normmod_bwd(s['x0'], nw1, mod[i], dh1, dx1, 0, ctx_rows, "normmod_bwd")
        gs['norm1_w'][i], gs['norm2_w'][i] = sums1[0, 2] + sums1[1, 2], sums2[0, 2] + sums2[1, 2]
        dmod[i] = jnp.stack([sums1[:, 0], sums1[:, 1], dg1[:, 0], sums2[:, 0], sums2[:, 1], dg2[:, 0]], axis=1)
        if pending is not None:
            scatter_wait(pending, i + 1, dx)
        pending, tok = scatter_start(i)
    grad_x = dx[ctx_rows:][None]
    dmod = jnp.stack(dmod)
    dmod_ctx, dmod_lat = dmod[:, 0].reshape(L, 6 * D), dmod[:, 1].reshape(L, 6 * D)
    gs['ada_b'] = dmod_ctx + dmod_lat

    out = {}

    small_g = [jnp.stack(gs[n]) if isinstance(gs[n], list) else gs[n] for n in SMALL]
    extras = [jnp.stack(gs['ssd_conv_w']), jnp.stack(gs['ffn_conv_w'])]
    shapes_c = [w[n].shape for n in SMALL] + [e.shape for e in extras]
    g_c = _small_allgather(_pack(small_g + extras), "gather_small")
    zeros = [jnp.zeros(e.shape, F32) for e in extras]
    res = _adamw_small(g_c, _pack([w[n] for n in SMALL] + zeros), _pack([m[n] for n in SMALL] + zeros),
                       _pack([v[n] for n in SMALL] + zeros), "adamw_small")
    res = [_unpack(r, shapes_c) for r in res]
    for k, n in enumerate(SMALL):
        out[n] = tuple(r[k] for r in res)
    g_scw, g_fcw = res[0][len(SMALL)], res[0][len(SMALL) + 1]
    g_scw = lax.dynamic_index_in_dim(g_scw.reshape(n_ssd, 3, NDEV, XBC // NDEV), me, axis=2, keepdims=False)
    g_fcw = lax.dynamic_index_in_dim(g_fcw.reshape(L, 3, NDEV, G), me, axis=2, keepdims=False)
    conv = ['ssd_conv_w', 'ffn_conv_w']
    res = _adamw_small(_pack([g_scw, g_fcw])[None], _pack([w[n] for n in conv]), _pack([m[n] for n in conv]),
                       _pack([v[n] for n in conv]), "adamw_conv")
    res = [_unpack(r, [w[n].shape for n in conv]) for r in res]
    for k, n in enumerate(conv):
        out[n] = tuple(r[k] for r in res)

    g_m = _small_allgather(jnp.concatenate([dmod_lat, dmod_ctx], axis=0), "gather_dmod")
    all_lat, all_ctx = g_m[:, :L], g_m[:, L:]
    my_cols = lambda t: lax.dynamic_slice(t, (0, 0, me * nl), (NDEV, L, nl))
    dml = jnp.concatenate([my_cols(all_lat), jnp.zeros((MOD_ROWS - NDEV, L, nl), F32)], axis=0)
    g_ada, pc = _modulation_bwd(s_in, dml, my_cols(all_ctx), w['ada_w'], "modulation_bwd")
    out['ada_w'] = _adamw(g_ada[None], w['ada_w'], m['ada_w'], v['ada_w'], "adamw")
    g_d = _small_allgather(_pack([pc[0]]), "gather_cctx")
    res = _cctx_update(g_d, _pack([w['c_ctx']]), _pack([m['c_ctx']]), _pack([v['c_ctx']]), "adamw_cctx")
    out['c_ctx'] = tuple(_unpack(r, [(D,)])[0] for r in res)

    scatter_wait(pending, 0, out['c_ctx'][0])
    for name in BIG:
        out[name] = _adamw(lands[name], w[name], m[name], v[name], "adamw")

    return (loss, grad_x) + tuple(out[n][k] for k in range(4) for n in WEIGHTS)


def kernel(x, c, ctx, c_ctx, ada_w, ada_b, norm1_w, norm2_w, ssd_w_in, ssd_conv_w, ssd_conv_b, ssd_dt_bias_f, ssd_dt_bias_b, ssd_a_log_f, ssd_a_log_b, ssd_d, ssd_norm_w, ssd_w_out, attn_w_qkv, attn_q_gain, attn_k_gain, attn_sinks, attn_w_o, ffn_w_up, ffn_conv_w, ffn_conv_b, ffn_w_down, loss_target, m_c_ctx, m_ada_w, m_ada_b, m_norm1_w, m_norm2_w, m_ssd_w_in, m_ssd_conv_w, m_ssd_conv_b, m_ssd_dt_bias_f, m_ssd_dt_bias_b, m_ssd_a_log_f, m_ssd_a_log_b, m_ssd_d, m_ssd_norm_w, m_ssd_w_out, m_attn_w_qkv, m_attn_q_gain, m_attn_k_gain, m_attn_sinks, m_attn_w_o, m_ffn_w_up, m_ffn_conv_w, m_ffn_conv_b, m_ffn_w_down, v_c_ctx, v_ada_w, v_ada_b, v_norm1_w, v_norm2_w, v_ssd_w_in, v_ssd_conv_w, v_ssd_conv_b, v_ssd_dt_bias_f, v_ssd_dt_bias_b, v_ssd_a_log_f, v_ssd_a_log_b, v_ssd_d, v_ssd_norm_w, v_ssd_w_out, v_attn_w_qkv, v_attn_q_gain, v_attn_k_gain, v_attn_sinks, v_attn_w_o, v_ffn_w_up, v_ffn_conv_w, v_ffn_conv_b, v_ffn_w_down):
    w = dict(c_ctx=c_ctx, ada_w=ada_w, ada_b=ada_b, norm1_w=norm1_w, norm2_w=norm2_w, ssd_w_in=ssd_w_in, ssd_conv_w=ssd_conv_w, ssd_conv_b=ssd_conv_b, ssd_dt_bias_f=ssd_dt_bias_f, ssd_dt_bias_b=ssd_dt_bias_b, ssd_a_log_f=ssd_a_log_f, ssd_a_log_b=ssd_a_log_b, ssd_d=ssd_d, ssd_norm_w=ssd_norm_w, ssd_w_out=ssd_w_out, attn_w_qkv=attn_w_qkv, attn_q_gain=attn_q_gain, attn_k_gain=attn_k_gain, attn_sinks=attn_sinks, attn_w_o=attn_w_o, ffn_w_up=ffn_w_up, ffn_conv_w=ffn_conv_w, ffn_conv_b=ffn_conv_b, ffn_w_down=ffn_w_down)
    m = dict(c_ctx=m_c_ctx, ada_w=m_ada_w, ada_b=m_ada_b, norm1_w=m_norm1_w, norm2_w=m_norm2_w, ssd_w_in=m_ssd_w_in, ssd_conv_w=m_ssd_conv_w, ssd_conv_b=m_ssd_conv_b, ssd_dt_bias_f=m_ssd_dt_bias_f, ssd_dt_bias_b=m_ssd_dt_bias_b, ssd_a_log_f=m_ssd_a_log_f, ssd_a_log_b=m_ssd_a_log_b, ssd_d=m_ssd_d, ssd_norm_w=m_ssd_norm_w, ssd_w_out=m_ssd_w_out, attn_w_qkv=m_attn_w_qkv, attn_q_gain=m_attn_q_gain, attn_k_gain=m_attn_k_gain, attn_sinks=m_attn_sinks, attn_w_o=m_attn_w_o, ffn_w_up=m_ffn_w_up, ffn_conv_w=m_ffn_conv_w, ffn_conv_b=m_ffn_conv_b, ffn_w_down=m_ffn_w_down)
    v = dict(c_ctx=v_c_ctx, ada_w=v_ada_w, ada_b=v_ada_b, norm1_w=v_norm1_w, norm2_w=v_norm2_w, ssd_w_in=v_ssd_w_in, ssd_conv_w=v_ssd_conv_w, ssd_conv_b=v_ssd_conv_b, ssd_dt_bias_f=v_ssd_dt_bias_f, ssd_dt_bias_b=v_ssd_dt_bias_b, ssd_a_log_f=v_ssd_a_log_f, ssd_a_log_b=v_ssd_a_log_b, ssd_d=v_ssd_d, ssd_norm_w=v_ssd_norm_w, ssd_w_out=v_ssd_w_out, attn_w_qkv=v_attn_w_qkv, attn_q_gain=v_attn_q_gain, attn_k_gain=v_attn_k_gain, attn_sinks=v_attn_sinks, attn_w_o=v_attn_w_o, ffn_w_up=v_ffn_w_up, ffn_conv_w=v_ffn_conv_w, ffn_conv_b=v_ffn_conv_b, ffn_w_down=v_ffn_w_down)
    return _step(x, c, ctx, w, loss_target, m, v)
```

```python
import functools

import numpy as np
import jax
import jax.numpy as jnp
from jax import lax
from jax.experimental import pallas as pl
from jax.experimental.pallas import tpu as pltpu

F32 = jnp.float32
BF16 = jnp.bfloat16
MXU = BF16
XFER = BF16
NORM_EPS = 1e-6
VMEM_CAP = 56 * 1024 * 1024
HALO = 8
LANE = 128
NDEV = 8

GRID_W = 64
ROPE_THETA = 10000.0
ATTN_KV_HEADS = 4
ATTN_WINDOW = 128
BLK = 128
SSD_GROUPS = 8

ADAM_LR, ADAM_B1, ADAM_B2, ADAM_EPS, ADAM_WD, ADAM_STEP = 0.001, 0.9, 0.999, 1e-08, 0.01, 10

MESH_ID = pl.DeviceIdType.MESH


def _cparams(sem, est_bytes):
    lim = int(min(VMEM_CAP, max(16 * 1024 * 1024, est_bytes * 1.3 + (4 << 20))))
    return pltpu.CompilerParams(dimension_semantics=sem, vmem_limit_bytes=lim)


def _nbytes(shape, dtype):
    return int(np.prod(shape)) * jnp.dtype(dtype).itemsize


def _silu(x):
    return x * jax.nn.sigmoid(x)


def _dsilu(x):
    s = jax.nn.sigmoid(x)
    return s * (1.0 + x * (1.0 - s))


def _split3(v):
    h = v.astype(BF16)
    r = v - h.astype(F32)
    m = r.astype(BF16)
    l = (r - m.astype(F32)).astype(BF16)
    return h, m, l


def _dot(a, b, dn=(((1,), (0,)), ((), ()))):
    return lax.dot_general(a, b, dn, preferred_element_type=F32)


NT = (((1,), (1,)), ((), ()))
TN = (((0,), (0,)), ((), ()))


def _dot3_rhs(sel, v):
    return sum(_dot(sel, p) for p in _split3(v))


def _dot3_lhs(v, sel, dn=(((1,), (0,)), ((), ()))):
    return sum(_dot(p, sel, dn) for p in _split3(v))


def _dot2_lhs(v, sel):
    h, m, _ = _split3(v)
    return _dot(h, sel) + _dot(m, sel)


def _ident(s):
    return s


def _ffn_perm(s):
    return (s % 2) * 4 + s // 2


def _mm(a, b, *, ta=False, tb=False, out_dtype=F32, tm, tn, tk, name, bslots=None, oslots=None,
        resid=None, gate=None, ctx_rows=0):
    M = a.shape[1] if ta else a.shape[0]
    K = a.shape[0] if ta else a.shape[1]
    if bslots is None:
        N = b.shape[0] if tb else b.shape[1]
    else:
        G = b.shape[2]
        N = b.shape[1] if tb else NDEV * G
        assert (NDEV * G == K) if tb else (b.shape[1] == K)
    tm, tn, tk = min(tm, M), min(tn, N), min(tk, K)
    if bslots is not None:
        if tb:
            tk = min(tk, G)
            assert G % tk == 0
        else:
            tn = min(tn, G)
            assert G % tn == 0
    if oslots is not None:
        Go = N // NDEV
        tn = min(tn, Go)
        assert Go % tn == 0
    assert M % tm == 0 and N % tn == 0 and K % tk == 0, (name, M, N, K, tm, tn, tk)
    nk = K // tk
    fused = resid is not None
    dn = (((0 if ta else 1,), (1 if tb else 0,)), ((), ()))

    def body(*refs):
        if fused:
            a_ref, b_ref, r_ref, g_ref, o_ref, x_ref = refs[:6]
            rest = refs[6:]
        else:
            a_ref, b_ref, o_ref = refs[:3]
            rest = refs[3:]
        bv = b_ref[0] if bslots is not None else b_ref[...]
        p = lax.dot_general(a_ref[...].astype(MXU), bv.astype(MXU), dn, preferred_element_type=F32)

        def finish(acc):
            if oslots is not None:
                o_ref[0] = acc.astype(o_ref.dtype)
            else:
                o_ref[...] = acc.astype(o_ref.dtype)
            if fused:
                row = pl.program_id(0) * tm + lax.broadcasted_iota(jnp.int32, (tm, 1), 0)
                g = jnp.where(row < ctx_rows, g_ref[0:1, :], g_ref[1:2, :])
                x_ref[...] = r_ref[...] + g * acc

        if nk == 1:
            finish(p)
        else:
            acc_ref = rest[0]
            k = pl.program_id(2)

            @pl.when(k == 0)
            def _():
                acc_ref[...] = p

            @pl.when(k > 0)
            def _():
                acc_ref[...] += p

            @pl.when(k == nk - 1)
            def _():
                finish(acc_ref[...])

    a_spec = pl.BlockSpec((tk, tm), lambda i, j, k: (k, i)) if ta else pl.BlockSpec((tm, tk), lambda i, j, k: (i, k))
    if bslots is None:
        b_spec = pl.BlockSpec((tn, tk), lambda i, j, k: (j, k)) if tb else pl.BlockSpec((tk, tn), lambda i, j, k: (k, j))
    elif tb:
        kpg = G // tk
        b_spec = pl.BlockSpec((1, tn, tk), lambda i, j, k: (bslots(k // kpg), j, k % kpg))
    else:
        npg = G // tn
        b_spec = pl.BlockSpec((1, tk, tn), lambda i, j, k: (bslots(j // npg), k, j % npg))
    if oslots is None:
        o_spec = pl.BlockSpec((tm, tn), lambda i, j, k: (i, j))
        o_shape = jax.ShapeDtypeStruct((M, N), out_dtype)
    else:
        opg = Go // tn
        o_spec = pl.BlockSpec((1, tm, tn), lambda i, j, k: (oslots(j // opg), i, j % opg))
        o_shape = jax.ShapeDtypeStruct((NDEV, M, Go), out_dtype)
    in_specs = [a_spec, b_spec]
    out_shape = [o_shape]
    out_specs = [o_spec]
    args = [a, b]
    est = 2 * (_nbytes((tm, tk), a.dtype) + _nbytes((tk, tn), b.dtype) + _nbytes((tm, tn), out_dtype)) + 3 * _nbytes((tm, tn), F32)
    if fused:
        in_specs += [o_spec, pl.BlockSpec((2, tn), lambda i, j, k: (0, j))]
        out_shape.append(jax.ShapeDtypeStruct((M, N), F32))
        out_specs.append(o_spec)
        args += [resid, gate]
        est += 4 * _nbytes((tm, tn), F32)
    scratch = [] if nk == 1 else [pltpu.VMEM((tm, tn), F32)]
    res = pl.pallas_call(
        body, name=name, grid=(M // tm, N // tn, nk), in_specs=in_specs, out_specs=out_specs, out_shape=out_shape,
        scratch_shapes=scratch, compiler_params=_cparams(("parallel", "parallel", "arbitrary"), est),
    )(*args)
    return res if fused else res[0]


def _stream_of(i, tr, ctx_rows):
    return jnp.where(i * tr < ctx_rows, 0, 1)


def _acc_by_stream(sums_ref, part, i, n_ctx):
    @pl.when((i == 0) | (i == n_ctx))
    def _():
        sums_ref[0] = part

    @pl.when((i != 0) & (i != n_ctx))
    def _():
        sums_ref[0] += part


def _normmod(x, nw, mod, which, ctx_rows, name, tr=256):
    T, D = x.shape
    tr = min(tr, ctx_rows)
    assert T % tr == 0 and ctx_rows % tr == 0
    s_sh, s_sc = 3 * which, 3 * which + 1

    def body(x_ref, nw_ref, mod_ref, h_ref):
        xv = x_ref[...]
        r = lax.rsqrt(jnp.mean(xv * xv, axis=-1, keepdims=True) + NORM_EPS)
        y = (xv * r) * nw_ref[...]
        h_ref[...] = (y * (1.0 + mod_ref[0, s_sc:s_sc + 1, :]) + mod_ref[0, s_sh:s_sh + 1, :]).astype(h_ref.dtype)

    return pl.pallas_call(
        body, name=name, grid=(T // tr,),
        in_specs=[pl.BlockSpec((tr, D), lambda i: (i, 0)), pl.BlockSpec((1, D), lambda i: (0, 0)),
                  pl.BlockSpec((1, 6, D), lambda i: (_stream_of(i, tr, ctx_rows), 0, 0))],
        out_specs=pl.BlockSpec((tr, D), lambda i: (i, 0)),
        out_shape=jax.ShapeDtypeStruct((T, D), MXU),
        compiler_params=_cparams(("parallel",), 10 * _nbytes((tr, D), F32)),
    )(x, nw, mod)


def _normmod_bwd(x, nw, mod, dh, dx_in, which, ctx_rows, name, tr=256):
    T, D = x.shape
    tr = min(tr, ctx_rows)
    s_sc = 3 * which + 1
    n_ctx = ctx_rows // tr

    def body(x_ref, nw_ref, mod_ref, dh_ref, dxi_ref, dx_ref, sums_ref):
        i = pl.program_id(0)
        xv = x_ref[...]
        r = lax.rsqrt(jnp.mean(xv * xv, axis=-1, keepdims=True) + NORM_EPS)
        xh = xv * r
        dh_v = dh_ref[...].astype(F32)
        sc1 = 1.0 + mod_ref[0, s_sc:s_sc + 1, :]
        nwv = nw_ref[...]
        dxh = dh_v * (nwv * sc1)
        dx_ref[...] = dxi_ref[...] + r * (dxh - xh * jnp.mean(dxh * xh, axis=-1, keepdims=True))
        t = dh_v * xh
        part = jnp.concatenate([jnp.sum(dh_v, axis=0, keepdims=True), jnp.sum(t * nwv, axis=0, keepdims=True),
                                jnp.sum(t * sc1, axis=0, keepdims=True), jnp.zeros((5, D), F32)], axis=0)
        _acc_by_stream(sums_ref, part, i, n_ctx)

    row = pl.BlockSpec((tr, D), lambda i: (i, 0))
    return pl.pallas_call(
        body, name=name, grid=(T // tr,),
        in_specs=[row, pl.BlockSpec((1, D), lambda i: (0, 0)),
                  pl.BlockSpec((1, 6, D), lambda i: (_stream_of(i, tr, ctx_rows), 0, 0)), row, row],
        out_specs=[row, pl.BlockSpec((1, 8, D), lambda i: (_stream_of(i, tr, ctx_rows), 0, 0))],
        out_shape=[jax.ShapeDtypeStruct((T, D), F32), jax.ShapeDtypeStruct((2, 8, D), F32)],
        compiler_params=_cparams(("arbitrary",), 16 * _nbytes((tr, D), F32)),
    )(x, nw, mod, dh, dx_in)


def _gate_bwd(dx, mix, mod, which, ctx_rows, name, tr=256):
    T, D = dx.shape
    tr = min(tr, ctx_rows)
    s_g = 3 * which + 2
    n_ctx = ctx_rows // tr

    def body(dx_ref, mix_ref, mod_ref, dm_ref, sums_ref):
        i = pl.program_id(0)
        dxv = dx_ref[...]
        dm_ref[...] = (dxv * mod_ref[0, s_g:s_g + 1, :]).astype(dm_ref.dtype)
        part = jnp.concatenate([jnp.sum(dxv * mix_ref[...], axis=0, keepdims=True), jnp.zeros((7, D), F32)], axis=0)
        _acc_by_stream(sums_ref, part, i, n_ctx)

    row = pl.BlockSpec((tr, D), lambda i: (i, 0))
    return pl.pallas_call(
        body, name=name, grid=(T // tr,),
        in_specs=[row, row, pl.BlockSpec((1, 6, D), lambda i: (_stream_of(i, tr, ctx_rows), 0, 0))],
        out_specs=[row, pl.BlockSpec((1, 8, D), lambda i: (_stream_of(i, tr, ctx_rows), 0, 0))],
        out_shape=[jax.ShapeDtypeStruct((T, D), MXU), jax.ShapeDtypeStruct((2, 8, D), F32)],
        compiler_params=_cparams(("arbitrary",), 10 * _nbytes((tr, D), F32)),
    )(dx, mix, mod)


def _halo_specs(tr, tn, n_row_tiles, col_of):
    g = tr // HALO
    last = n_row_tiles * g - 1
    return [pl.BlockSpec((HALO, tn), lambda j, i: (jnp.maximum(i * g - 1, 0), col_of(j))),
            pl.BlockSpec((tr, tn), lambda j, i: (i, col_of(j))),
            pl.BlockSpec((HALO, tn), lambda j, i: (jnp.minimum((i + 1) * g, last), col_of(j)))]


def _ext(p_ref, m_ref, n_ref):
    return jnp.concatenate([p_ref[...].astype(F32), m_ref[...].astype(F32), n_ref[...].astype(F32)], axis=0)


def _seq_masks(i, tr, ctx_rows, total_rows):
    row = i * tr - HALO + lax.broadcasted_iota(jnp.int32, (tr + 2 * HALO, 1), 0)
    has_prev = (row != 0) & (row != ctx_rows)
    has_next = (row != ctx_rows - 1) & (row != total_rows - 1)
    return has_prev, has_next


def _shift_down(e):
    return pltpu.roll(e, 1, 0)


def _shift_up(e):
    return pltpu.roll(e, e.shape[0] - 1, 0)


def _conv3(e, w, has_prev, has_next):
    prev = jnp.where(has_prev, _shift_down(e), 0.0)
    nxt = jnp.where(has_next, _shift_up(e), 0.0)
    return prev * w[0:1, :] + e * w[1:2, :] + nxt * w[2:3, :]


def _conv3_t(d, w, has_prev, has_next):
    from_next = jnp.where(has_next, _shift_up(d), 0.0)
    from_prev = jnp.where(has_prev, _shift_down(d), 0.0)
    return from_next * w[0:1, :] + d * w[1:2, :] + from_prev * w[2:3, :]


def _conv_wgrad(d, e, has_prev, has_next):
    n = e.shape[0]
    c = slice(HALO, n - HALO)
    prev = jnp.where(has_prev, _shift_down(e), 0.0)
    nxt = jnp.where(has_next, _shift_up(e), 0.0)
    dc = d[c]
    return jnp.concatenate([jnp.sum(dc * prev[c], axis=0, keepdims=True), jnp.sum(dc * e[c], axis=0, keepdims=True),
                            jnp.sum(dc * nxt[c], axis=0, keepdims=True), jnp.sum(dc, axis=0, keepdims=True),
                            jnp.zeros((4, e.shape[1]), F32)], axis=0)


def _acc_first(ref, part, i):
    @pl.when(i == 0)
    def _():
        ref[...] = part

    @pl.when(i > 0)
    def _():
        ref[...] += part


def _ffn_mid(u, cw, cb, ctx_rows, name, tr=128):
    T, F2 = u.shape
    G = F2 // NDEV
    tr = min(tr, ctx_rows)
    nr, nc = T // tr, NDEV // 2

    def body(up, um, un, w_ref, b_ref, a_ref):
        i = pl.program_id(1)
        hp, hn = _seq_masks(i, tr, ctx_rows, T)
        uc = _conv3(_ext(up, um, un), w_ref[...], hp, hn)[HALO:HALO + tr] + b_ref[...]
        a_ref[...] = (_silu(uc[:, G:]) * uc[:, :G]).astype(a_ref.dtype)

    return pl.pallas_call(
        body, name=name, grid=(nc, nr),
        in_specs=_halo_specs(tr, 2 * G, nr, lambda j: j) + [pl.BlockSpec((3, 2 * G), lambda j, i: (0, j)),
                                                             pl.BlockSpec((1, 2 * G), lambda j, i: (0, j))],
        out_specs=pl.BlockSpec((tr, G), lambda j, i: (i, j)),
        out_shape=jax.ShapeDtypeStruct((T, F2 // 2), MXU),
        compiler_params=_cparams(("parallel", "parallel"), 12 * _nbytes((tr + 16, 2 * G), F32)),
    )(u, u, u, cw, cb)


def _ffn_mid_bwd(u, da, cw, cb, ctx_rows, name, tr=128):
    T, F2 = u.shape
    G = F2 // NDEV
    tr = min(tr, ctx_rows)
    nr, nc = T // tr, NDEV // 2

    def body(up, um, un, dp, dm, dn_, w_ref, b_ref, du_ref, gw_ref):
        i = pl.program_id(1)
        hp, hn = _seq_masks(i, tr, ctx_rows, T)
        e = _ext(up, um, un)
        w = w_ref[...]
        uc = _conv3(e, w, hp, hn) + b_ref[...]
        val, gt = uc[:, :G], uc[:, G:]
        dav = _ext(dp, dm, dn_)
        duc = jnp.concatenate([dav * _silu(gt), dav * val * _dsilu(gt)], axis=1)
        du_ref[...] = _conv3_t(duc, w, hp, hn)[HALO:HALO + tr].astype(du_ref.dtype)
        _acc_first(gw_ref, _conv_wgrad(duc, e, hp, hn), i)

    du, gw = pl.pallas_call(
        body, name=name, grid=(nc, nr),
        in_specs=(_halo_specs(tr, 2 * G, nr, lambda j: j) + _halo_specs(tr, G, nr, lambda j: j)
                  + [pl.BlockSpec((3, 2 * G), lambda j, i: (0, j)), pl.BlockSpec((1, 2 * G), lambda j, i: (0, j))]),
        out_specs=[pl.BlockSpec((tr, 2 * G), lambda j, i: (i, j)), pl.BlockSpec((8, 2 * G), lambda j, i: (0, j))],
        out_shape=[jax.ShapeDtypeStruct((T, F2), MXU), jax.ShapeDtypeStruct((8, F2), F32)],
        compiler_params=_cparams(("parallel", "arbitrary"), 24 * _nbytes((tr + 16, 2 * G), F32)),
    )(u, u, u, da, da, da, cw, cb)
    return du, gw


def _ssd_conv(zx, cw, cb, col0, ctx_rows, name, tr=256, tn=512):
    T = zx.shape[0]
    C = cw.shape[1]
    tr, tn = min(tr, ctx_rows), min(tn, C)
    assert C % tn == 0 and col0 % tn == 0
    nr, nc, cb0 = T // tr, C // tn, col0 // tn

    def body(zp, zm, zn, w_ref, b_ref, o_ref):
        i = pl.program_id(1)
        hp, hn = _seq_masks(i, tr, ctx_rows, T)
        o_ref[...] = _silu(_conv3(_ext(zp, zm, zn), w_ref[...], hp, hn)[HALO:HALO + tr] + b_ref[...])

    return pl.pallas_call(
        body, name=name, grid=(nc, nr),
        in_specs=_halo_specs(tr, tn, nr, lambda j: j + cb0) + [pl.BlockSpec((3, tn), lambda j, i: (0, j)),
                                                                pl.BlockSpec((1, tn), lambda j, i: (0, j))],
        out_specs=pl.BlockSpec((tr, tn), lambda j, i: (i, j)),
        out_shape=jax.ShapeDtypeStruct((T, C), F32),
        compiler_params=_cparams(("parallel", "parallel"), 12 * _nbytes((tr + 16, tn), F32)),
    )(zx, zx, zx, cw, cb)


def _ssd_conv_bwd(zx, dxbc, dzx, cw, cb, col0, ctx_rows, name, tr=256, tn=512):
    T = zx.shape[0]
    C = cw.shape[1]
    tr, tn = min(tr, ctx_rows), min(tn, C)
    nr, nc, cb0 = T // tr, C // tn, col0 // tn

    def body(zp, zm, zn, dp, dm, dn_, w_ref, b_ref, dzx_in, dz_ref, gw_ref):
        del dzx_in
        i = pl.program_id(1)
        hp, hn = _seq_masks(i, tr, ctx_rows, T)
        e = _ext(zp, zm, zn)
        w = w_ref[...]
        pre = _conv3(e, w, hp, hn) + b_ref[...]
        dpre = _ext(dp, dm, dn_) * _dsilu(pre)
        dz_ref[...] = _conv3_t(dpre, w, hp, hn)[HALO:HALO + tr].astype(dz_ref.dtype)
        _acc_first(gw_ref, _conv_wgrad(dpre, e, hp, hn), i)

    return pl.pallas_call(
        body, name=name, grid=(nc, nr),
        in_specs=(_halo_specs(tr, tn, nr, lambda j: j + cb0) + _halo_specs(tr, tn, nr, lambda j: j)
                  + [pl.BlockSpec((3, tn), lambda j, i: (0, j)), pl.BlockSpec((1, tn), lambda j, i: (0, j)),
                     pl.BlockSpec(memory_space=pl.ANY)]),
        out_specs=[pl.BlockSpec((tr, tn), lambda j, i: (i, j + cb0)), pl.BlockSpec((8, tn), lambda j, i: (0, j))],
        out_shape=[jax.ShapeDtypeStruct(dzx.shape, dzx.dtype), jax.ShapeDtypeStruct((8, C), F32)],
        input_output_aliases={8: 0},
        compiler_params=_cparams(("parallel", "arbitrary"), 24 * _nbytes((tr + 16, tn), F32)),
    )(zx, zx, zx, dxbc, dxbc, dxbc, cw, cb, dzx)


def _cat_cols(w3, name, tr=256):
    n, K, G = w3.shape
    tr = min(tr, K)

    def body(w_ref, o_ref):
        o_ref[...] = jnp.concatenate([w_ref[d].astype(F32) for d in range(n)], axis=1).astype(o_ref.dtype)

    return pl.pallas_call(
        body, name=name, grid=(K // tr,),
        in_specs=[pl.BlockSpec((n, tr, G), lambda i: (0, i, 0))], out_specs=pl.BlockSpec((tr, n * G), lambda i: (i, 0)),
        out_shape=jax.ShapeDtypeStruct((K, n * G), w3.dtype),
        compiler_params=_cparams(("parallel",), 6 * _nbytes((tr, n * G), F32)),
    )(w3)


def _split_cols(g, n, out_dtype, name, tr=256):
    K, NG = g.shape
    G = NG // n
    tr = min(tr, K)

    def body(g_ref, o_ref):
        for d in range(n):
            o_ref[d] = g_ref[:, d * G:(d + 1) * G].astype(o_ref.dtype)

    return pl.pallas_call(
        body, name=name, grid=(K // tr,),
        in_specs=[pl.BlockSpec((tr, NG), lambda i: (i, 0))], out_specs=pl.BlockSpec((n, tr, G), lambda i: (0, i, 0)),
        out_shape=jax.ShapeDtypeStruct((n, K, G), out_dtype),
        compiler_params=_cparams(("parallel",), 6 * _nbytes((tr, NG), F32)),
    )(g)


def _put_cols(dst, src, col_blk, name, tr=256):
    T, W = src.shape
    tr = min(tr, T)

    def body(s_ref, d_in, o_ref):
        del d_in
        o_ref[...] = s_ref[...].astype(o_ref.dtype)

    return pl.pallas_call(
        body, name=name, grid=(T // tr,),
        in_specs=[pl.BlockSpec((tr, W), lambda i: (i, 0)), pl.BlockSpec(memory_space=pl.ANY)],
        out_specs=pl.BlockSpec((tr, W), lambda i: (i, col_blk)),
        out_shape=jax.ShapeDtypeStruct(dst.shape, dst.dtype),
        input_output_aliases={1: 0},
        compiler_params=_cparams(("parallel",), 8 * _nbytes((tr, W), F32)),
    )(src, dst)


def _rope_tables(t_lat, ctx_rows, hd):
    half, quarter = hd // 2, hd // 4
    pos = jnp.arange(t_lat)
    row = (pos // GRID_W).astype(F32)
    col = (pos % GRID_W).astype(F32)
    inv_freq = ROPE_THETA ** (-jnp.arange(0, half, 2, dtype=F32) / half)
    ar, ac = row[:, None] * inv_freq[None, :], col[:, None] * inv_freq[None, :]
    cos = jnp.concatenate([jnp.cos(ar), jnp.cos(ar), jnp.cos(ac), jnp.cos(ac)], axis=1)
    sin = jnp.concatenate([-jnp.sin(ar), jnp.sin(ar), -jnp.sin(ac), jnp.sin(ac)], axis=1)
    del quarter
    cos = jnp.concatenate([jnp.ones((ctx_rows, hd), F32), cos], axis=0)
    sin = jnp.concatenate([jnp.zeros((ctx_rows, hd), F32), sin], axis=0)
    return cos, sin


def _partner(y):
    hd = y.shape[1]
    q = hd // 4
    lane = lax.broadcasted_iota(jnp.int32, y.shape, 1)
    return jnp.where((lane % (2 * q)) < q, pltpu.roll(y, hd - q, 1), pltpu.roll(y, q, 1))


def _qk_prep(qkv, qg, kg, cos, sin, n_q, ctx_rows, name, tr=256):
    T = qkv.shape[0]
    hd = qg.shape[1]
    n_kv = ATTN_KV_HEADS
    tr = min(tr, ctx_rows)

    def body(x_ref, qg_ref, kg_ref, c_ref, s_ref, q_ref, k_ref, v_ref):
        cv, sv = c_ref[...], s_ref[...]
        for h in range(n_q + n_kv):
            xh = x_ref[:, h * hd:(h + 1) * hd]
            r = lax.rsqrt(jnp.mean(xh * xh, axis=-1, keepdims=True) + NORM_EPS)
            y = (xh * r) * (qg_ref[...] if h < n_q else kg_ref[...])
            rot = y * cv + _partner(y) * sv
            if h < n_q:
                q_ref[:, h * hd:(h + 1) * hd] = rot.astype(q_ref.dtype)
            else:
                k_ref[:, (h - n_q) * hd:(h - n_q + 1) * hd] = rot.astype(k_ref.dtype)
        v_ref[...] = x_ref[:, (n_q + n_kv) * hd:].astype(v_ref.dtype)

    W = qkv.shape[1]
    return pl.pallas_call(
        body, name=name, grid=(T // tr,),
        in_specs=[pl.BlockSpec((tr, W), lambda i: (i, 0)), pl.BlockSpec((1, hd), lambda i: (0, 0)),
                  pl.BlockSpec((1, hd), lambda i: (0, 0)), pl.BlockSpec((tr, hd), lambda i: (i, 0)),
                  pl.BlockSpec((tr, hd), lambda i: (i, 0))],
        out_specs=[pl.BlockSpec((tr, n_q * hd), lambda i: (i, 0)), pl.BlockSpec((tr, n_kv * hd), lambda i: (i, 0)),
                   pl.BlockSpec((tr, n_kv * hd), lambda i: (i, 0))],
        out_shape=[jax.ShapeDtypeStruct((T, n_q * hd), MXU), jax.ShapeDtypeStruct((T, n_kv * hd), MXU),
                   jax.ShapeDtypeStruct((T, n_kv * hd), MXU)],
        compiler_params=_cparams(("parallel",), 6 * _nbytes((tr, W), F32)),
    )(qkv, qg, kg, cos, sin)


def _qk_prep_bwd(qkv, qg, kg, cos, sin, dq, dk, dv, n_q, ctx_rows, name, tr=256):
    T, W = qkv.shape
    hd = qg.shape[1]
    n_kv = ATTN_KV_HEADS
    tr = min(tr, ctx_rows)

    def body(x_ref, qg_ref, kg_ref, c_ref, s_ref, dq_ref, dk_ref, dv_ref, o_ref, g_ref):
        i = pl.program_id(0)
        cv, sv = c_ref[...], s_ref[...]
        gq = jnp.zeros((1, hd), F32)
        gk = jnp.zeros((1, hd), F32)
        for h in range(n_q + n_kv):
            xh = x_ref[:, h * hd:(h + 1) * hd]
            gain = qg_ref[...] if h < n_q else kg_ref[...]
            drot = (dq_ref[:, h * hd:(h + 1) * hd] if h < n_q else dk_ref[:, (h - n_q) * hd:(h - n_q + 1) * hd]).astype(F32)
            dy = drot * cv + _partner(drot * sv)
            r = lax.rsqrt(jnp.mean(xh * xh, axis=-1, keepdims=True) + NORM_EPS)
            xn = xh * r
            gsum = jnp.sum(dy * xn, axis=0, keepdims=True)
            if h < n_q:
                gq = gq + gsum
            else:
                gk = gk + gsum
            dxn = dy * gain
            o_ref[:, h * hd:(h + 1) * hd] = (r * (dxn - xn * jnp.mean(dxn * xn, axis=-1, keepdims=True))).astype(o_ref.dtype)
        o_ref[:, (n_q + n_kv) * hd:] = dv_ref[...].astype(o_ref.dtype)
        _acc_first(g_ref, jnp.concatenate([gq, gk, jnp.zeros((6, hd), F32)], axis=0), i)

    return pl.pallas_call(
        body, name=name, grid=(T // tr,),
        in_specs=[pl.BlockSpec((tr, W), lambda i: (i, 0)), pl.BlockSpec((1, hd), lambda i: (0, 0)),
                  pl.BlockSpec((1, hd), lambda i: (0, 0)), pl.BlockSpec((tr, hd), lambda i: (i, 0)),
                  pl.BlockSpec((tr, hd), lambda i: (i, 0)), pl.BlockSpec((tr, n_q * hd), lambda i: (i, 0)),
                  pl.BlockSpec((tr, n_kv * hd), lambda i: (i, 0)), pl.BlockSpec((tr, n_kv * hd), lambda i: (i, 0))],
        out_specs=[pl.BlockSpec((tr, W), lambda i: (i, 0)), pl.BlockSpec((8, hd), lambda i: (0, 0))],
        out_shape=[jax.ShapeDtypeStruct((T, W), MXU), jax.ShapeDtypeStruct((8, hd), F32)],
        compiler_params=_cparams(("arbitrary",), 8 * _nbytes((tr, W), F32)),
    )(qkv, qg, kg, cos, sin, dq, dk, dv)


def _attn_scores(q_ref, k_ref, sink_ref, h, qb, ctx_rows, nb, hd, grp):
    scale = hd ** -0.5
    w0 = jnp.clip(qb - 1, 0, nb - 3) * BLK
    w0 = pl.multiple_of(w0, BLK)
    qv = q_ref[...]
    qs = jnp.concatenate([qv[:, g * hd:(g + 1) * hd] for g in range(grp)], axis=0)
    kc = k_ref[0:ctx_rows, :]
    kb = k_ref[pl.ds(w0, 3 * BLK), :]
    s_c = _dot(qs, kc, NT) * scale
    s_b = _dot(qs, kb, NT) * scale
    n = grp * BLK
    qpos = qb * BLK + lax.broadcasted_iota(jnp.int32, (n, 3 * BLK), 0) % BLK
    kpos = w0 + lax.broadcasted_iota(jnp.int32, (n, 3 * BLK), 1)
    ok = (jnp.abs(kpos - qpos) <= ATTN_WINDOW) & (kpos >= ctx_rows) & (qpos >= ctx_rows)
    s_b = jnp.where(ok, s_b, -jnp.inf)
    gi = lax.broadcasted_iota(jnp.int32, (n, 1), 0) // BLK
    sink = jnp.zeros((n, 1), F32)
    for g in range(grp):
        sink = jnp.where(gi == g, sink_ref[h * grp + g], sink)
    m = jnp.maximum(jnp.maximum(jnp.max(s_c, axis=1, keepdims=True), jnp.max(s_b, axis=1, keepdims=True)), sink)
    e_c, e_b, e_s = jnp.exp(s_c - m), jnp.exp(s_b - m), jnp.exp(sink - m)
    inv = 1.0 / (jnp.sum(e_c, axis=1, keepdims=True) + jnp.sum(e_b, axis=1, keepdims=True) + e_s)
    return qs, kc, kb, w0, e_c * inv, e_b * inv, e_s * inv, gi


def _attn_fwd(qr, kr, vb, sinks, n_q, ctx_rows, name):
    T = qr.shape[0]
    n_kv = ATTN_KV_HEADS
    grp = n_q // n_kv
    hd = qr.shape[1] // n_q
    nb = T // BLK

    def body(sink_ref, q_ref, k_ref, v_ref, o_ref):
        h, qb = pl.program_id(0), pl.program_id(1)
        _, _, _, w0, p_c, p_b, _, _ = _attn_scores(q_ref, k_ref, sink_ref, h, qb, ctx_rows, nb, hd, grp)
        o = _dot(p_c.astype(MXU), v_ref[0:ctx_rows, :]) + _dot(p_b.astype(MXU), v_ref[pl.ds(w0, 3 * BLK), :])
        o_ref[...] = jnp.concatenate([o[g * BLK:(g + 1) * BLK] for g in range(grp)], axis=1).astype(o_ref.dtype)

    return pl.pallas_call(
        body, name=name, grid=(n_kv, nb),
        in_specs=[pl.BlockSpec(memory_space=pltpu.SMEM), pl.BlockSpec((BLK, grp * hd), lambda h, i: (i, h)),
                  pl.BlockSpec((T, hd), lambda h, i: (0, h)), pl.BlockSpec((T, hd), lambda h, i: (0, h))],
        out_specs=pl.BlockSpec((BLK, grp * hd), lambda h, i: (i, h)),
        out_shape=jax.ShapeDtypeStruct((T, n_q * hd), MXU),
        compiler_params=_cparams(("parallel", "arbitrary"), 4 * _nbytes((T, hd), MXU) + 24 * _nbytes((grp * BLK, 5 * BLK), F32)),
    )(sinks, qr, kr, vb)


def _attn_bwd(qr, kr, vb, sinks, do, n_q, ctx_rows, name):
    T = qr.shape[0]
    n_kv = ATTN_KV_HEADS
    grp = n_q // n_kv
    hd = qr.shape[1] // n_q
    nb = T // BLK
    scale = hd ** -0.5

    def body(sink_ref, q_ref, k_ref, v_ref, do_ref, dq_ref, dk_ref, dv_ref, ds_ref):
        h, qb = pl.program_id(0), pl.program_id(1)
        qs, kc, kb, w0, p_c, p_b, p_s, gi = _attn_scores(q_ref, k_ref, sink_ref, h, qb, ctx_rows, nb, hd, grp)
        dov = do_ref[...]
        dos = jnp.concatenate([dov[:, g * hd:(g + 1) * hd] for g in range(grp)], axis=0)
        vc = v_ref[0:ctx_rows, :]
        vw = v_ref[pl.ds(w0, 3 * BLK), :]
        dp_c = _dot(dos, vc, NT)
        dp_b = _dot(dos, vw, NT)
        delta = jnp.sum(p_c * dp_c, axis=1, keepdims=True) + jnp.sum(p_b * dp_b, axis=1, keepdims=True)
        ds_c = (p_c * (dp_c - delta) * scale).astype(MXU)
        ds_b = (p_b * (dp_b - delta) * scale).astype(MXU)
        dq = _dot(ds_c, kc) + _dot(ds_b, kb)
        dq_ref[...] = jnp.concatenate([dq[g * BLK:(g + 1) * BLK] for g in range(grp)], axis=1)

        @pl.when(qb == 0)
        def _():
            dk_ref[...] = jnp.zeros(dk_ref.shape, F32)
            dv_ref[...] = jnp.zeros(dv_ref.shape, F32)

        dk_ref[0:ctx_rows, :] += _dot(ds_c, qs, TN)
        dv_ref[0:ctx_rows, :] += _dot(p_c.astype(MXU), dos, TN)
        dk_ref[pl.ds(w0, 3 * BLK), :] += _dot(ds_b, qs, TN)
        dv_ref[pl.ds(w0, 3 * BLK), :] += _dot(p_b.astype(MXU), dos, TN)
        t = -(p_s * delta)
        lane = lax.broadcasted_iota(jnp.int32, (8, LANE), 1)
        part = jnp.zeros((8, LANE), F32)
        for g in range(grp):
            part = jnp.where(lane == g, jnp.sum(jnp.where(gi == g, t, 0.0)), part)
        _acc_first(ds_ref, part, qb)

    return pl.pallas_call(
        body, name=name, grid=(n_kv, nb),
        in_specs=[pl.BlockSpec(memory_space=pltpu.SMEM), pl.BlockSpec((BLK, grp * hd), lambda h, i: (i, h)),
                  pl.BlockSpec((T, hd), lambda h, i: (0, h)), pl.BlockSpec((T, hd), lambda h, i: (0, h)),
                  pl.BlockSpec((BLK, grp * hd), lambda h, i: (i, h))],
        out_specs=[pl.BlockSpec((BLK, grp * hd), lambda h, i: (i, h)), pl.BlockSpec((T, hd), lambda h, i: (0, h)),
                   pl.BlockSpec((T, hd), lambda h, i: (0, h)), pl.BlockSpec((8, LANE), lambda h, i: (h, 0))],
        out_shape=[jax.ShapeDtypeStruct((T, n_q * hd), F32), jax.ShapeDtypeStruct((T, n_kv * hd), F32),
                   jax.ShapeDtypeStruct((T, n_kv * hd), F32), jax.ShapeDtypeStruct((n_kv * 8, LANE), F32)],
        compiler_params=_cparams(("parallel", "arbitrary"), 4 * _nbytes((T, hd), MXU) + 4 * _nbytes((T, hd), F32)
                                 + 40 * _nbytes((grp * BLK, 5 * BLK), F32)),
    )(sinks, qr, kr, vb, do)


def _chunk_order(s, n_chunks, n_ctx, rev):
    if not rev:
        return s
    return jnp.where(s < n_ctx, n_ctx - 1 - s, n_chunks - 1 + n_ctx - s)


def _softplus(x):
    return jnp.maximum(x, 0.0) + jnp.log(1.0 + jnp.exp(-jnp.abs(x)))


def _expand_matrix(n_heads, p, rev):
    e = np.zeros((LANE, n_heads * p), np.float32)
    for h in range(n_heads):
        e[h + (n_heads if rev else 0), h * p:(h + 1) * p] = 1.0
    return jnp.asarray(e, BF16)


def _ssd_chunk_prep(dt_ref, dtb_ref, alog_ref, e_ref, rev):
    dt = _softplus(dt_ref[...] + dtb_ref[...])
    a = -jnp.exp(alog_ref[...])
    li = lax.broadcasted_iota(jnp.int32, (BLK, BLK), 0)
    si = lax.broadcasted_iota(jnp.int32, (BLK, BLK), 1)
    tri = (si >= li) if rev else (si <= li)
    acs = _dot3_rhs(tri.astype(BF16), a * dt)
    ev = e_ref[...]
    return dt, a, tri, acs, _dot2_lhs(dt, ev), _dot2_lhs(acs, ev)


def _pair_cols(ap):
    lane = lax.broadcasted_iota(jnp.int32, ap.shape, 1)
    apr = pltpu.roll(ap, LANE // 2, 1)
    return jnp.where(lane < LANE // 2, ap, apr), jnp.where(lane < LANE // 2, apr, ap)


def _ssd_scan(xbc, zx, dtb, alog, emat, n_heads, n_ctx, rev, name):
    T, C = xbc.shape
    P = emat.shape[1] // n_heads
    DI = n_heads * P
    GN = (C - DI) // 2
    N = GN // SSD_GROUPS
    n_pairs = DI // LANE
    ppg = n_pairs // SSD_GROUPS
    n_chunks = T // BLK
    hoff = n_heads if rev else 0
    last = 0 if rev else BLK - 1
    dt_blk = zx.shape[1] // LANE - 1
    assert N == LANE and 2 * P == LANE and 2 * n_heads == LANE

    def body(xs_ref, b_ref, c_ref, dt_ref, dtb_ref, alog_ref, e_ref, y_ref, hin_ref, state_ref, xdt_s, xdec_s, aexp_s, at_s):
        s = pl.program_id(0)

        @pl.when(s == 0)
        def _():
            state_ref[...] = jnp.zeros(state_ref.shape, F32)

        dt, a, tri, acs, dtexp, aexp = _ssd_chunk_prep(dt_ref, dtb_ref, alog_ref, e_ref, rev)
        at_s[...] = acs.T
        aexp_s[...] = aexp
        xdt = xs_ref[...] * dtexp
        xdt_s[...] = xdt.astype(MXU)
        xdec_s[...] = (xdt * jnp.exp(aexp[last:last + 1, :] - aexp)).astype(MXU)
        hin_ref[0] = state_ref[...]
        lane = lax.broadcasted_iota(jnp.int32, (BLK, LANE), 1)

        def pair(k, carry):
            col = pl.multiple_of(k * LANE, LANE)
            gcol = pl.multiple_of((k // ppg) * N, N)
            bg = b_ref[:, pl.ds(gcol, N)].astype(MXU)
            cg = c_ref[:, pl.ds(gcol, N)].astype(MXU)
            cb = _dot(cg, bg, NT)
            ap = aexp_s[:, pl.ds(col, LANE)]
            ac0, ac1 = _pair_cols(ap)
            ar0 = at_s[pl.ds(2 * k + hoff, 1), :]
            ar1 = at_s[pl.ds(2 * k + 1 + hoff, 1), :]
            m0 = (cb * jnp.exp(jnp.where(tri, ac0 - ar0, -jnp.inf))).astype(MXU)
            m1 = (cb * jnp.exp(jnp.where(tri, ac1 - ar1, -jnp.inf))).astype(MXU)
            xp = xdt_s[:, pl.ds(col, LANE)]
            zero = jnp.zeros_like(xp)
            xbd = jnp.concatenate([jnp.where(lane < LANE // 2, xp, zero), jnp.where(lane >= LANE // 2, xp, zero)], axis=0)
            yd = _dot(jnp.concatenate([m0, m1], axis=1), xbd)
            ht = state_ref[k]
            yo = _dot(cg, ht.astype(MXU)) * jnp.exp(ap)
            y_ref[:, pl.ds(col, LANE)] = yd + yo
            st = _dot(bg, xdec_s[:, pl.ds(col, LANE)], TN)
            state_ref[k] = jnp.exp(aexp_s[pl.ds(last, 1), pl.ds(col, LANE)]) * ht + st
            return carry

        lax.fori_loop(0, n_pairs, pair, 0, unroll=2)

    order = lambda s: _chunk_order(s, n_chunks, n_ctx, rev)
    return pl.pallas_call(
        body, name=name, grid=(n_chunks,),
        in_specs=[pl.BlockSpec((BLK, DI), lambda s: (order(s), 0)),
                  pl.BlockSpec((BLK, GN), lambda s: (order(s), DI // GN)),
                  pl.BlockSpec((BLK, GN), lambda s: (order(s), DI // GN + 1)),
                  pl.BlockSpec((BLK, LANE), lambda s: (order(s), dt_blk)),
                  pl.BlockSpec((1, LANE), lambda s: (0, 0)), pl.BlockSpec((1, LANE), lambda s: (0, 0)),
                  pl.BlockSpec((LANE, DI), lambda s: (0, 0))],
        out_specs=[pl.BlockSpec((BLK, DI), lambda s: (order(s), 0)),
                   pl.BlockSpec((1, n_pairs, N, LANE), lambda s: (order(s), 0, 0, 0))],
        out_shape=[jax.ShapeDtypeStruct((T, DI), F32), jax.ShapeDtypeStruct((n_chunks, n_pairs, N, LANE), F32)],
        scratch_shapes=[pltpu.VMEM((n_pairs, N, LANE), F32), pltpu.VMEM((BLK, DI), MXU), pltpu.VMEM((BLK, DI), MXU),
                        pltpu.VMEM((BLK, DI), F32), pltpu.VMEM((LANE, BLK), F32)],
        compiler_params=_cparams(("arbitrary",), 20 * _nbytes((BLK, DI), F32)),
    )(xbc, xbc, xbc, zx, dtb, alog, emat)


def _ssd_scan_bwd(xbc, zx, dtb, alog, emat, emat_t, dexp, hin, dy, acc, n_heads, n_ctx, rev, name):
    T, C = xbc.shape
    P = emat.shape[1] // n_heads
    DI = n_heads * P
    GN = (C - DI) // 2
    N = GN // SSD_GROUPS
    n_pairs = DI // LANE
    ppg = n_pairs // SSD_GROUPS
    n_chunks = T // BLK
    hoff = n_heads if rev else 0
    last = 0 if rev else BLK - 1
    dt_blk = zx.shape[1] // LANE - 1
    has_acc = acc is not None

    def body(*refs):
        (xs_ref, b_ref, c_ref, dt_ref, dtb_ref, alog_ref, e_ref, et_ref, dexp_ref, hin_ref, dy_ref) = refs[:11]
        n_in = 11
        if has_acc:
            dxbc_in, ddt_in, sums_in = refs[11:14]
            n_in = 14
        dxbc_ref, ddt_ref, sums_ref = refs[n_in:n_in + 3]
        dstate_ref, xdt_s, xdec_s, aexp_s, at_s, dyd_s, z1_s, z3_s, dxdt_s, cdrow_s, rmat_s, cst_s = refs[n_in + 3:]
        s = pl.program_id(0)

        @pl.when(s == 0)
        def _():
            dstate_ref[...] = jnp.zeros(dstate_ref.shape, F32)

        dt, a, tri, acs, dtexp, aexp = _ssd_chunk_prep(dt_ref, dtb_ref, alog_ref, e_ref, rev)
        at_s[...] = acs.T
        aexp_s[...] = aexp
        xsv = xs_ref[...]
        xdt = xsv * dtexp
        xdt_s[...] = xdt.astype(MXU)
        decend = jnp.exp(aexp[last:last + 1, :] - aexp)
        xdec_s[...] = (xdt * decend).astype(MXU)
        dyv = dy_ref[...]
        dyd_s[...] = (dyv * jnp.exp(aexp)).astype(MXU)
        dxbc_ref[:, DI:] = jnp.zeros((BLK, 2 * GN), F32)
        rmat_s[...] = jnp.zeros(rmat_s.shape, F32)
        lane = lax.broadcasted_iota(jnp.int32, (BLK, LANE), 1)
        lo = lane < LANE // 2
        tri_t = (lax.broadcasted_iota(jnp.int32, (BLK, BLK), 0) <= lax.broadcasted_iota(jnp.int32, (BLK, BLK), 1)) if not rev \
            else (lax.broadcasted_iota(jnp.int32, (BLK, BLK), 0) >= lax.broadcasted_iota(jnp.int32, (BLK, BLK), 1))

        def pair(k, carry):
            col = pl.multiple_of(k * LANE, LANE)
            gcol = pl.multiple_of((k // ppg) * N, N)
            bg = b_ref[:, pl.ds(gcol, N)].astype(MXU)
            cg = c_ref[:, pl.ds(gcol, N)].astype(MXU)
            cb = _dot(cg, bg, NT)
            cbt = _dot(bg, cg, NT)
            ap = aexp_s[:, pl.ds(col, LANE)]
            ac0, ac1 = _pair_cols(ap)
            ar0 = at_s[pl.ds(2 * k + hoff, 1), :]
            ar1 = at_s[pl.ds(2 * k + 1 + hoff, 1), :]
            seg0 = jnp.exp(jnp.where(tri, ac0 - ar0, -jnp.inf))
            seg1 = jnp.exp(jnp.where(tri, ac1 - ar1, -jnp.inf))
            segt0 = jnp.exp(jnp.where(tri_t, ar0 - ac0, -jnp.inf))
            segt1 = jnp.exp(jnp.where(tri_t, ar1 - ac1, -jnp.inf))
            dyp = dy_ref[:, pl.ds(col, LANE)].astype(MXU)
            zero = jnp.zeros_like(dyp)
            dy0, dy1 = jnp.where(lo, dyp, zero), jnp.where(lo, zero, dyp)
            dht = dstate_ref[k]
            dhb = dht.astype(MXU)
            ht = hin_ref[0, k]
            mt = jnp.concatenate([(cbt * segt0).astype(MXU), (cbt * segt1).astype(MXU)], axis=1)
            bdh = _dot(bg, dhb)
            dec_p = jnp.exp(aexp_s[pl.ds(last, 1), pl.ds(col, LANE)] - ap)
            dxdt_s[:, pl.ds(col, LANE)] = _dot(mt, jnp.concatenate([dy0, dy1], axis=0)) + dec_p * bdh
            z3_s[:, pl.ds(col, LANE)] = bdh
            cdec = jnp.exp(aexp_s[pl.ds(last, 1), pl.ds(col, LANE)])
            dydp = dyd_s[:, pl.ds(col, LANE)]
            dstate_ref[k] = _dot(cg, dydp, TN) + cdec * dht
            cdrow_s[0:1, pl.ds(col, LANE)] = cdec * jnp.sum(dht * ht, axis=0, keepdims=True)
            z1_s[:, pl.ds(col, LANE)] = _dot(cg, ht.astype(MXU))
            dcg = _dot(dydp, ht.astype(MXU), NT)
            dbg = _dot(xdec_s[:, pl.ds(col, LANE)], dhb, NT)
            xp = xdt_s[:, pl.ds(col, LANE)]
            dg0 = _dot(dy0, xp, NT)
            dg1 = _dot(dy1, xp, NT)
            ds0, ds1 = dg0 * seg0, dg1 * seg1
            w0, w1 = ds0 * cb, ds1 * cb
            dcb = ds0 + ds1
            lane_h = lax.broadcasted_iota(jnp.int32, (BLK, LANE), 1)
            rmat_s[...] += (jnp.where(lane_h == 2 * k + hoff, jnp.sum(w0, axis=1, keepdims=True), 0.0)
                            + jnp.where(lane_h == 2 * k + 1 + hoff, jnp.sum(w1, axis=1, keepdims=True), 0.0))
            cst_s[pl.ds(2 * k + hoff, 1), :] = jnp.sum(w0, axis=0, keepdims=True)
            cst_s[pl.ds(2 * k + 1 + hoff, 1), :] = jnp.sum(w1, axis=0, keepdims=True)
            dcbb = dcb.astype(MXU)
            dxbc_ref[:, pl.ds(DI + GN + gcol, N)] += dcg + _dot(dcbb, bg)
            dxbc_ref[:, pl.ds(DI + gcol, N)] += dbg + _dot(dcbb, cg, TN)
            return carry

        cst_s[...] = jnp.zeros(cst_s.shape, F32)
        lax.fori_loop(0, n_pairs, pair, 0, unroll=2)

        etv = et_ref[...]
        dxdt = dxdt_s[...]
        z1 = _dot2_lhs(dyv * (z1_s[...] * jnp.exp(aexp)), etv)
        z2 = _dot2_lhs(dxdt * xsv, etv)
        q = _dot2_lhs(xdt * decend * z3_s[...], etv)
        cd8 = jnp.concatenate([cdrow_s[0:1, :], jnp.zeros((7, DI), F32)], axis=0)
        cdh = _dot2_lhs(cd8, etv)[0:1, :]
        dacs = rmat_s[...] - cst_s[...].T + z1 - q
        rowi = lax.broadcasted_iota(jnp.int32, (BLK, LANE), 0)
        dacs = dacs + jnp.where(rowi == last, jnp.sum(q, axis=0, keepdims=True) + cdh, 0.0)
        d_a = _dot3_rhs(tri_t.astype(BF16), dacs)
        ddt = a * d_a + z2
        x_raw = dt_ref[...] + dtb_ref[...]
        ddt_raw = ddt * jax.nn.sigmoid(x_raw)
        lane_l = lax.broadcasted_iota(jnp.int32, (BLK, LANE), 1)
        mine = (lane_l >= hoff) & (lane_l < hoff + n_heads)
        ddt_raw = jnp.where(mine, ddt_raw, 0.0)
        part = jnp.concatenate([jnp.sum(jnp.where(mine, dt * d_a, 0.0), axis=0, keepdims=True) * a,
                                jnp.sum(ddt_raw, axis=0, keepdims=True), jnp.zeros((6, LANE), F32)], axis=0)
        dxs = dxdt * dtexp
        if has_acc:
            dxbc_ref[:, 0:DI] = dxs + dxbc_in[:, 0:DI]
            dxbc_ref[:, DI:] += dxbc_in[:, DI:]
            ddt_ref[...] = ddt_raw + ddt_in[...]
            part = part + jnp.where(s == 0, sums_in[...], 0.0)
        else:
            dxbc_ref[:, 0:DI] = dxs + dyv * dexp_ref[...]
            ddt_ref[...] = ddt_raw
        _acc_first(sums_ref, part, s)

    order = lambda s: _chunk_order(n_chunks - 1 - s, n_chunks, n_ctx, rev)
    in_specs = [pl.BlockSpec((BLK, DI), lambda s: (order(s), 0)),
                pl.BlockSpec((BLK, GN), lambda s: (order(s), DI // GN)),
                pl.BlockSpec((BLK, GN), lambda s: (order(s), DI // GN + 1)),
                pl.BlockSpec((BLK, LANE), lambda s: (order(s), dt_blk)),
                pl.BlockSpec((1, LANE), lambda s: (0, 0)), pl.BlockSpec((1, LANE), lambda s: (0, 0)),
                pl.BlockSpec((LANE, DI), lambda s: (0, 0)), pl.BlockSpec((DI, LANE), lambda s: (0, 0)),
                pl.BlockSpec((1, DI), lambda s: (0, 0)),
                pl.BlockSpec((1, n_pairs, N, LANE), lambda s: (order(s), 0, 0, 0)),
                pl.BlockSpec((BLK, DI), lambda s: (order(s), 0))]
    args = [xbc, xbc, xbc, zx, dtb, alog, emat, emat_t, dexp, hin, dy]
    if has_acc:
        in_specs += [pl.BlockSpec((BLK, C), lambda s: (order(s), 0)), pl.BlockSpec((BLK, LANE), lambda s: (order(s), 0)),
                     pl.BlockSpec((8, LANE), lambda s: (0, 0))]
        args += list(acc)
    return pl.pallas_call(
        body, name=name, grid=(n_chunks,),
        in_specs=in_specs,
        out_specs=[pl.BlockSpec((BLK, C), lambda s: (order(s), 0)), pl.BlockSpec((BLK, LANE), lambda s: (order(s), 0)),
                   pl.BlockSpec((8, LANE), lambda s: (0, 0))],
        out_shape=[jax.ShapeDtypeStruct((T, C), F32), jax.ShapeDtypeStruct((T, LANE), F32), jax.ShapeDtypeStruct((8, LANE), F32)],
        scratch_shapes=[pltpu.VMEM((n_pairs, N, LANE), F32), pltpu.VMEM((BLK, DI), MXU), pltpu.VMEM((BLK, DI), MXU),
                        pltpu.VMEM((BLK, DI), F32), pltpu.VMEM((LANE, BLK), F32), pltpu.VMEM((BLK, DI), MXU),
                        pltpu.VMEM((BLK, DI), F32), pltpu.VMEM((BLK, DI), F32), pltpu.VMEM((BLK, DI), F32),
                        pltpu.VMEM((8, DI), F32), pltpu.VMEM((BLK, LANE), F32), pltpu.VMEM((LANE, BLK), F32)],
        compiler_params=_cparams(("arbitrary",), 36 * _nbytes((BLK, DI), F32)),
    )(*args)


def _ssd_finish(yf, yb, xbc, zx, dexp, nw, name, tr=256):
    T, DI = yf.shape
    tr = min(tr, T)

    def body(yf_ref, yb_ref, xs_ref, z_ref, d_ref, nw_ref, o_ref):
        y = yf_ref[...] + yb_ref[...] + xs_ref[...] * d_ref[...]
        gt = y * _silu(z_ref[...])
        r = lax.rsqrt(jnp.mean(gt * gt, axis=-1, keepdims=True) + NORM_EPS)
        o_ref[...] = ((gt * r) * nw_ref[...]).astype(o_ref.dtype)

    row = pl.BlockSpec((tr, DI), lambda i: (i, 0))
    vec = pl.BlockSpec((1, DI), lambda i: (0, 0))
    return pl.pallas_call(
        body, name=name, grid=(T // tr,), in_specs=[row, row, row, row, vec, vec], out_specs=row,
        out_shape=jax.ShapeDtypeStruct((T, DI), MXU),
        compiler_params=_cparams(("parallel",), 16 * _nbytes((tr, DI), F32)),
    )(yf, yb, xbc, zx, dexp, nw)


def _ssd_finish_bwd(yf, yb, xbc, zx, dexp, nw, do, emat_t, dzx_shape, name, tr=64):
    T, DI = yf.shape
    tr = min(tr, T)
    n_steps = T // tr

    def body(yf_ref, yb_ref, xs_ref, z_ref, d_ref, nw_ref, do_ref, et_ref, dy_ref, dz_ref, sums_ref, dd_ref):
        i = pl.program_id(0)
        xs = xs_ref[...]
        zv = z_ref[...]
        y = yf_ref[...] + yb_ref[...] + xs * d_ref[...]
        sz = _silu(zv)
        gt = y * sz
        r = lax.rsqrt(jnp.mean(gt * gt, axis=-1, keepdims=True) + NORM_EPS)
        gn = gt * r
        dov = do_ref[...].astype(F32)
        dgn = dov * nw_ref[...]
        dgt = r * (dgn - gn * jnp.mean(dgn * gn, axis=-1, keepdims=True))
        dy = dgt * sz
        dy_ref[...] = dy
        dz_ref[...] = (dgt * y * _dsilu(zv)).astype(dz_ref.dtype)
        part = jnp.concatenate([jnp.sum(dov * gn, axis=0, keepdims=True), jnp.sum(dy * xs, axis=0, keepdims=True),
                                jnp.zeros((6, DI), F32)], axis=0)
        _acc_first(sums_ref, part, i)

        @pl.when(i == n_steps - 1)
        def _():
            dd_ref[...] = _dot3_lhs(sums_ref[...], et_ref[...])

    row = pl.BlockSpec((tr, DI), lambda i: (i, 0))
    vec = pl.BlockSpec((1, DI), lambda i: (0, 0))
    return pl.pallas_call(
        body, name=name, grid=(n_steps,),
        in_specs=[row, row, row, row, vec, vec, row, pl.BlockSpec((DI, LANE), lambda i: (0, 0))],
        out_specs=[row, row, pl.BlockSpec((8, DI), lambda i: (0, 0)), pl.BlockSpec((8, LANE), lambda i: (0, 0))],
        out_shape=[jax.ShapeDtypeStruct((T, DI), F32), jax.ShapeDtypeStruct(dzx_shape, MXU), jax.ShapeDtypeStruct((8, DI), F32),
                   jax.ShapeDtypeStruct((8, LANE), F32)],
        compiler_params=_cparams(("arbitrary",), 40 * _nbytes((tr, DI), F32)),
    )(yf, yb, xbc, zx, dexp, nw, do, emat_t)


def _loss_head(xf, tgt, ctx_rows, name, tr=256):
    T, D = xf.shape
    tr = min(tr, ctx_rows)
    n_ctx = ctx_rows // tr

    def body(x_ref, t_ref, dx_ref, l_ref):
        i = pl.program_id(0)

        @pl.when(i < n_ctx)
        def _():
            dx_ref[...] = jnp.zeros(dx_ref.shape, F32)

        @pl.when(i == 0)
        def _():
            l_ref[...] = jnp.zeros(l_ref.shape, F32)

        @pl.when(i >= n_ctx)
        def _():
            e = x_ref[...] - t_ref[...]
            dx_ref[...] = e * (1.0 / D)
            l_ref[...] += 0.5 * jnp.sum(jnp.mean(e * e, axis=-1, keepdims=True))

    return pl.pallas_call(
        body, name=name, grid=(T // tr,),
        in_specs=[pl.BlockSpec((tr, D), lambda i: (i, 0)), pl.BlockSpec((tr, D), lambda i: (jnp.maximum(i - n_ctx, 0), 0))],
        out_specs=[pl.BlockSpec((tr, D), lambda i: (i, 0)), pl.BlockSpec((8, LANE), lambda i: (0, 0))],
        out_shape=[jax.ShapeDtypeStruct((T, D), F32), jax.ShapeDtypeStruct((8, LANE), F32)],
        compiler_params=_cparams(("arbitrary",), 10 * _nbytes((tr, D), F32)),
    )(xf, tgt)


def _adamw_math(w, g, m, v):
    m2 = ADAM_B1 * m + (1.0 - ADAM_B1) * g
    v2 = ADAM_B2 * v + (1.0 - ADAM_B2) * (g * g)
    m_hat = m2 / (1.0 - ADAM_B1 ** ADAM_STEP)
    v_hat = v2 / (1.0 - ADAM_B2 ** ADAM_STEP)
    delta = -ADAM_LR * (m_hat / (jnp.sqrt(v_hat) + ADAM_EPS) + ADAM_WD * w)
    return delta, m2, v2


def _row_tile(rows, target):
    if rows <= target:
        return rows
    t = target - target % 8
    while rows % t:
        t -= 8
    return t


def _adamw(parts, w, m, v, name, tr=128):
    n, L, rows, cols = parts.shape
    tr = _row_tile(rows, tr)

    def body(p_ref, w_ref, m_ref, v_ref, g_ref, d_ref, m2_ref, v2_ref):
        g = p_ref[0, 0].astype(F32)
        for q in range(1, n):
            g = g + p_ref[q, 0].astype(F32)
        d, m2, v2 = _adamw_math(w_ref[0], g, m_ref[0], v_ref[0])
        g_ref[0], d_ref[0], m2_ref[0], v2_ref[0] = g, d, m2, v2

    blk = pl.BlockSpec((1, tr, cols), lambda l, i: (l, i, 0))
    shp = jax.ShapeDtypeStruct((L, rows, cols), F32)
    return pl.pallas_call(
        body, name=name, grid=(L, rows // tr),
        in_specs=[pl.BlockSpec((n, 1, tr, cols), lambda l, i: (0, l, i, 0)), blk, blk, blk],
        out_specs=[blk, blk, blk, blk], out_shape=[shp, shp, shp, shp],
        compiler_params=_cparams(("parallel", "parallel"), 2 * (n + 8) * _nbytes((tr, cols), F32)),
    )(parts, w, m, v)


def _adamw_small(bufs, w, m, v, name):
    n, R, _ = bufs.shape

    def body(b_ref, w_ref, m_ref, v_ref, g_ref, d_ref, m2_ref, v2_ref):
        g = b_ref[0]
        for q in range(1, n):
            g = g + b_ref[q]
        d, m2, v2 = _adamw_math(w_ref[...], g, m_ref[...], v_ref[...])
        g_ref[...], d_ref[...], m2_ref[...], v2_ref[...] = g, d, m2, v2

    vm = pl.BlockSpec(memory_space=pltpu.VMEM)
    shp = jax.ShapeDtypeStruct((R, LANE), F32)
    return pl.pallas_call(body, name=name, in_specs=[vm, vm, vm, vm], out_specs=[vm, vm, vm, vm],
                          out_shape=[shp, shp, shp, shp],
                          compiler_params=pltpu.CompilerParams(vmem_limit_bytes=32 * 1024 * 1024))(bufs, w, m, v)


def _add_blocks(g, r1, c_idx, name, tr=128):
    _, rows, cols = g.shape
    tr = _row_tile(rows, tr)

    def body(c_ref, g_ref, r_ref, p_ref):
        del c_ref
        p_ref[...] = (g_ref[...] + r_ref[...]).astype(p_ref.dtype)

    return pl.pallas_call(
        body, name=name,
        grid_spec=pltpu.PrefetchScalarGridSpec(
            num_scalar_prefetch=1, grid=(NDEV // 2, rows // tr),
            in_specs=[pl.BlockSpec((1, tr, cols), lambda q, i, c: (2 * q + c[0], i, 0)),
                      pl.BlockSpec((1, tr, cols), lambda q, i, c: (q, i, 0))],
            out_specs=pl.BlockSpec((1, tr, cols), lambda q, i, c: (q, i, 0))),
        out_shape=jax.ShapeDtypeStruct((NDEV // 2, rows, cols), XFER),
        compiler_params=_cparams(("parallel", "parallel"), 8 * _nbytes((tr, cols), F32)),
    )(c_idx, g, r1)


def _me():
    return lax.axis_index("x"), lax.axis_index("y"), lax.axis_index("c")


def _flip(v, bit):
    return 1 - v if bit else v


def _peer(k):
    x, y, c = _me()
    return _flip(x, (k >> 2) & 1), _flip(y, (k >> 1) & 1), _flip(c, k & 1)


def _dev_index(p):
    return 4 * p[0] + 2 * p[1] + p[2]


def _chip_index(p):
    return 2 * p[0] + p[1]


def _small_allgather(v, name):
    R, C = v.shape

    def body(v_ref, out_ref, send_sems, recv_sems, loc_sem):
        me = _dev_index(_me())
        mine = pltpu.make_async_copy(v_ref, out_ref.at[me], loc_sem)
        mine.start()
        sends = []
        for k in range(1, NDEV):
            cp = pltpu.make_async_remote_copy(src_ref=v_ref, dst_ref=out_ref.at[me], send_sem=send_sems.at[k - 1],
                                              recv_sem=recv_sems.at[k - 1], device_id=_peer(k), device_id_type=MESH_ID)
            cp.start()
            sends.append(cp)
        for k in range(1, NDEV):
            pltpu.make_async_remote_copy(src_ref=v_ref, dst_ref=out_ref.at[_dev_index(_peer(k))], send_sem=send_sems.at[k - 1],
                                         recv_sem=recv_sems.at[k - 1], device_id=_peer(k), device_id_type=MESH_ID).wait_recv()
        for cp in sends:
            cp.wait_send()
        mine.wait()

    vm = pl.BlockSpec(memory_space=pltpu.VMEM)
    return pl.pallas_call(
        body, name=name, in_specs=[vm], out_specs=vm, out_shape=jax.ShapeDtypeStruct((NDEV, R, C), F32),
        scratch_shapes=[pltpu.SemaphoreType.DMA((NDEV - 1,)), pltpu.SemaphoreType.DMA((NDEV - 1,)), pltpu.SemaphoreType.DMA(())],
        compiler_params=pltpu.CompilerParams(vmem_limit_bytes=48 * 1024 * 1024),
    )(v)


def _allgather_weights(arrs, name):
    n = len(arrs)

    def body(*refs):
        x_refs, out_refs = refs[:n], refs[n:2 * n]
        send_sems, recv_sems, loc_sems = refs[2 * n:]
        x, y, c = _me()
        me, sib = (x, y, c), (x, y, 1 - c)
        chips = [(1 - x, y), (x, 1 - y), (1 - x, 1 - y)]

        def copy(a, k, block, to, src=None):
            dst = out_refs[a].at[_dev_index(block)]
            return pltpu.make_async_remote_copy(src_ref=dst if src is None else src, dst_ref=dst, send_sem=send_sems.at[a, k],
                                                recv_sem=recv_sems.at[a, k], device_id=to, device_id_type=MESH_ID)

        locs, first, passed = [], [], []
        for a in range(n):
            lc = pltpu.make_async_copy(x_refs[a], out_refs[a].at[_dev_index(me)], loc_sems.at[a])
            lc.start()
            locs.append(lc)
            cps = [copy(a, 0, me, sib, src=x_refs[a])] + [copy(a, 1 + j, me, (*chip, c), src=x_refs[a]) for j, chip in enumerate(chips)]
            for cp in cps:
                cp.start()
            first += cps
        for j, chip in enumerate(chips):
            for a in range(n):
                copy(a, 1 + j, (*chip, c), me).wait_recv()
                cp = copy(a, 4 + j, (*chip, c), sib)
                cp.start()
                passed.append(cp)
        for a in range(n):
            copy(a, 0, sib, me).wait_recv()
            for j, chip in enumerate(chips):
                copy(a, 4 + j, (*chip, 1 - c), me).wait_recv()
        for cp in first + passed:
            cp.wait_send()
        for lc in locs:
            lc.wait()

    hbm = pl.BlockSpec(memory_space=pl.ANY)
    return pl.pallas_call(
        body, name=name, in_specs=[hbm] * n, out_specs=[hbm] * n,
        out_shape=[jax.ShapeDtypeStruct((NDEV,) + a.shape, a.dtype) for a in arrs],
        scratch_shapes=[pltpu.SemaphoreType.DMA((n, 7)), pltpu.SemaphoreType.DMA((n, 7)), pltpu.SemaphoreType.DMA((n,))],
    )(*arrs)


def _rs_sibling(gs, name):
    n = len(gs)

    def body(*refs):
        g_refs, r_refs = refs[:n], refs[n:2 * n]
        send_sems, recv_sems = refs[2 * n:]
        x, y, c = _me()
        sib = (x, y, 1 - c)
        sends = []
        for a in range(n):
            for q in range(NDEV // 2):
                cp = pltpu.make_async_remote_copy(src_ref=g_refs[a].at[2 * q + (1 - c)], dst_ref=r_refs[a].at[q],
                                                  send_sem=send_sems.at[a, q], recv_sem=recv_sems.at[a, q],
                                                  device_id=sib, device_id_type=MESH_ID)
                cp.start()
                sends.append(cp)
        for cp in sends:
            cp.wait_recv()
        for cp in sends:
            cp.wait_send()

    hbm = pl.BlockSpec(memory_space=pl.ANY)
    return pl.pallas_call(
        body, name=name, in_specs=[hbm] * n, out_specs=[hbm] * n,
        out_shape=[jax.ShapeDtypeStruct((NDEV // 2,) + g.shape[1:], g.dtype) for g in gs],
        scratch_shapes=[pltpu.SemaphoreType.DMA((n, NDEV // 2)), pltpu.SemaphoreType.DMA((n, NDEV // 2))],
    )(*gs)


def _rs_chips(ps, groups, name):
    n = len(ps)
    where = {}
    for gi, grp in enumerate(groups):
        for li, a in enumerate(grp):
            where[a] = (gi, li)
    ng = len(groups)

    def body(*refs):
        p_refs, r_refs = refs[:n], refs[n:n + ng]
        send_sems, recv_sems, loc_sems = refs[n + ng:]
        x, y, c = _me()
        mychip = _chip_index((x, y))
        chips = [(1 - x, y), (x, 1 - y), (1 - x, 1 - y)]
        sends, locs = [], []
        for a in range(n):
            gi, li = where[a]
            lc = pltpu.make_async_copy(p_refs[a].at[mychip], r_refs[gi].at[mychip, li], loc_sems.at[a])
            lc.start()
            locs.append(lc)
            for j, chip in enumerate(chips):
                cp = pltpu.make_async_remote_copy(src_ref=p_refs[a].at[_chip_index(chip)], dst_ref=r_refs[gi].at[mychip, li],
                                                  send_sem=send_sems.at[a, j], recv_sem=recv_sems.at[a, j],
                                                  device_id=(*chip, c), device_id_type=MESH_ID)
                cp.start()
                sends.append(cp)
        for a in range(n):
            gi, li = where[a]
            for j, chip in enumerate(chips):
                pltpu.make_async_remote_copy(src_ref=p_refs[a].at[mychip], dst_ref=r_refs[gi].at[_chip_index(chip), li],
                                             send_sem=send_sems.at[a, j], recv_sem=recv_sems.at[a, j],
                                             device_id=(*chip, c), device_id_type=MESH_ID).wait_recv()
        for cp in sends:
            cp.wait_send()
        for lc in locs:
            lc.wait()

    hbm = pl.BlockSpec(memory_space=pl.ANY)
    return pl.pallas_call(
        body, name=name, in_specs=[hbm] * n, out_specs=[hbm] * ng,
        out_shape=[jax.ShapeDtypeStruct((NDEV // 2, len(grp)) + ps[grp[0]].shape[1:], ps[grp[0]].dtype) for grp in groups],
        scratch_shapes=[pltpu.SemaphoreType.DMA((n, 3)), pltpu.SemaphoreType.DMA((n, 3)), pltpu.SemaphoreType.DMA((n,))],
    )(*ps)


HBM_SPEC = pl.BlockSpec(memory_space=pltpu.HBM)
SEM_SPEC = pl.BlockSpec(memory_space=pltpu.SEMAPHORE)
DATAFLOW = pltpu.SideEffectType.DATAFLOW_SIDE_EFFECTING


def _xfer_copy(src_ref, land_ref, sems, a, k, layer, scatter, arriving):
    send_sems, recv_sems = sems
    me, peer = _dev_index(_me()), _dev_index(_peer(k))
    src = src_ref.at[peer] if scatter else src_ref
    slot = peer if arriving else me
    dst = land_ref.at[slot] if layer is None else land_ref.at[slot, layer]
    return pltpu.make_async_remote_copy(src_ref=src, dst_ref=dst, send_sem=send_sems.at[a * (NDEV - 1) + k - 1],
                                        recv_sem=recv_sems.at[a * (NDEV - 1) + k - 1], device_id=_peer(k),
                                        device_id_type=MESH_ID)


def _xfer_start(srcs, lands, layers, scatter, name):
    n = len(srcs)

    def body(*refs):
        src_refs, land_refs = refs[:n], refs[n:2 * n]
        sems = refs[2 * n], refs[2 * n + 1]
        token = refs[-1]
        for a in range(n):
            for k in range(1, NDEV):
                _xfer_copy(src_refs[a], land_refs[a], sems, a, k, layers[a], scatter, False).start()
        token[...] = jnp.zeros(token.shape, token.dtype)

    ops = [pltpu.with_memory_space_constraint(t, pltpu.HBM) for t in list(srcs) + list(lands)]
    n_sem = n * (NDEV - 1)
    res = pl.pallas_call(
        body, name=name,
        out_shape=(pltpu.SemaphoreType.DMA((n_sem,)), pltpu.SemaphoreType.DMA((n_sem,)),
                   *[pltpu.HBM(t.shape, t.dtype) for t in ops], jax.ShapeDtypeStruct((8, LANE), F32)),
        in_specs=[HBM_SPEC] * (2 * n),
        out_specs=(SEM_SPEC, SEM_SPEC, *[HBM_SPEC] * (2 * n), pl.BlockSpec(memory_space=pltpu.VMEM)),
        input_output_aliases={i: 2 + i for i in range(2 * n)},
        compiler_params=pltpu.CompilerParams(has_side_effects=DATAFLOW),
    )(*ops)
    return res[0], res[1], list(res[2:2 + n]), list(res[2 + n:2 + 2 * n]), res[-1]


def _xfer_wait(send_sems, recv_sems, srcs, lands, layers, scatter, after, name):
    n = len(srcs)

    def body(*refs):
        src_refs, land_refs = refs[:n], refs[n:2 * n]
        sems = refs[2 * n], refs[2 * n + 1]
        for a in range(n):
            for k in range(1, NDEV):
                _xfer_copy(src_refs[a], land_refs[a], sems, a, k, layers[a], scatter, False).wait_send()
                _xfer_copy(src_refs[a], land_refs[a], sems, a, k, layers[a], scatter, True).wait_recv()

    ops = list(srcs) + list(lands)
    res = pl.pallas_call(
        body, name=name,
        out_shape=tuple(pltpu.HBM(t.shape, t.dtype) for t in ops),
        in_specs=[HBM_SPEC] * (2 * n) + [SEM_SPEC, SEM_SPEC, pl.BlockSpec(memory_space=pl.ANY)],
        out_specs=tuple([HBM_SPEC] * (2 * n)),
        input_output_aliases={i: i for i in range(2 * n)},
        compiler_params=pltpu.CompilerParams(has_side_effects=DATAFLOW),
    )(*ops, send_sems, recv_sems, after)
    return list(res[n:])


HI = lax.Precision.HIGHEST
MOD_ROWS = 16


def _col_tile(n, target=512):
    return target if n % target == 0 else n


def _modulation(s_in, ada_w, b_loc, name):
    L, D, nl = ada_w.shape
    tn = _col_tile(nl)

    def body(s_ref, w_ref, b_ref, o_ref):
        o_ref[0] = jnp.dot(_silu(s_ref[...]), w_ref[0], preferred_element_type=F32, precision=HI) + b_ref[0]

    return pl.pallas_call(
        body, name=name, grid=(L, nl // tn),
        in_specs=[pl.BlockSpec((MOD_ROWS, D), lambda l, j: (0, 0)), pl.BlockSpec((1, D, tn), lambda l, j: (l, 0, j)),
                  pl.BlockSpec((1, 1, tn), lambda l, j: (l, 0, j))],
        out_specs=pl.BlockSpec((1, MOD_ROWS, tn), lambda l, j: (l, 0, j)),
        out_shape=jax.ShapeDtypeStruct((L, MOD_ROWS, nl), F32),
        compiler_params=_cparams(("parallel", "parallel"), 4 * _nbytes((D, tn), F32)),
    )(s_in, ada_w, b_loc)


def _modulation_bwd(s_in, dml, dmc, ada_w, name):
    L, D, nl = ada_w.shape
    tn = _col_tile(nl)

    def body(s_ref, dml_ref, dmc_ref, w_ref, g_ref, pc_ref):
        l, j = pl.program_id(0), pl.program_id(1)
        a = _silu(s_ref[...])
        tot = dmc_ref[0, 0]
        for d in range(1, NDEV):
            tot = tot + dmc_ref[d, 0]
        row = lax.broadcasted_iota(jnp.int32, (MOD_ROWS, tn), 0)
        dm = jnp.where(row == NDEV, tot, dml_ref[:, 0, 0, :])
        g_ref[0] = lax.dot_general(a, dm, TN, preferred_element_type=F32, precision=HI)
        tot8 = jnp.where(lax.broadcasted_iota(jnp.int32, (8, tn), 0) == 0, tot, 0.0)
        part = lax.dot_general(tot8, w_ref[0], NT, preferred_element_type=F32, precision=HI)

        @pl.when((l == 0) & (j == 0))
        def _():
            pc_ref[...] = part

        @pl.when((l != 0) | (j != 0))
        def _():
            pc_ref[...] += part

    return pl.pallas_call(
        body, name=name, grid=(L, nl // tn),
        in_specs=[pl.BlockSpec((MOD_ROWS, D), lambda l, j: (0, 0)), pl.BlockSpec((MOD_ROWS, 1, 1, tn), lambda l, j: (0, l, 0, j)),
                  pl.BlockSpec((NDEV, 1, 1, tn), lambda l, j: (0, l, 0, j)), pl.BlockSpec((1, D, tn), lambda l, j: (l, 0, j))],
        out_specs=[pl.BlockSpec((1, D, tn), lambda l, j: (l, 0, j)), pl.BlockSpec((8, D), lambda l, j: (0, 0))],
        out_shape=[jax.ShapeDtypeStruct((L, D, nl), F32), jax.ShapeDtypeStruct((8, D), F32)],
        compiler_params=_cparams(("arbitrary", "arbitrary"), 8 * _nbytes((D, tn), F32)),
    )(s_in, dml.reshape(MOD_ROWS, L, 1, nl), dmc.reshape(NDEV, L, 1, nl), ada_w)


def _cctx_update(bufs, c_ctx, m, v, name):
    n, R, _ = bufs.shape

    def body(b_ref, w_ref, m_ref, v_ref, g_ref, d_ref, m2_ref, v2_ref):
        g = b_ref[0]
        for q in range(1, n):
            g = g + b_ref[q]
        g = g * _dsilu(w_ref[...])
        d, m2, v2 = _adamw_math(w_ref[...], g, m_ref[...], v_ref[...])
        g_ref[...], d_ref[...], m2_ref[...], v2_ref[...] = g, d, m2, v2

    vm = pl.BlockSpec(memory_space=pltpu.VMEM)
    shp = jax.ShapeDtypeStruct((R, LANE), F32)
    return pl.pallas_call(body, name=name, in_specs=[vm, vm, vm, vm], out_specs=[vm, vm, vm, vm],
                          out_shape=[shp, shp, shp, shp])(bufs, c_ctx, m, v)


def _pack(arrs):
    flat = jnp.concatenate([a.reshape(-1).astype(F32) for a in arrs])
    n = flat.shape[0]
    total = -(-n // (8 * LANE)) * (8 * LANE)
    return jnp.pad(flat, (0, total - n)).reshape(total // LANE, LANE)


def _unpack(buf, shapes):
    lead = buf.shape[:-2]
    flat = buf.reshape(lead + (-1,))
    out, off = [], 0
    for s in shapes:
        n = int(np.prod(s))
        out.append(flat[..., off:off + n].reshape(lead + tuple(s)))
        off += n
    return out


WEIGHTS = ['c_ctx', 'ada_w', 'ada_b', 'norm1_w', 'norm2_w', 'ssd_w_in', 'ssd_conv_w', 'ssd_conv_b', 'ssd_dt_bias_f',
           'ssd_dt_bias_b', 'ssd_a_log_f', 'ssd_a_log_b', 'ssd_d', 'ssd_norm_w', 'ssd_w_out', 'attn_w_qkv', 'attn_q_gain',
           'attn_k_gain', 'attn_sinks', 'attn_w_o', 'ffn_w_up', 'ffn_conv_w', 'ffn_conv_b', 'ffn_w_down']
SMALL = ['ada_b', 'norm1_w', 'norm2_w', 'ssd_conv_b', 'ssd_dt_bias_f', 'ssd_dt_bias_b', 'ssd_a_log_f', 'ssd_a_log_b', 'ssd_d',
         'ssd_norm_w', 'attn_q_gain', 'attn_k_gain', 'attn_sinks', 'ffn_conv_b']
BIG = ['ssd_w_in', 'ssd_w_out', 'attn_w_qkv', 'attn_w_o', 'ffn_w_up', 'ffn_w_down']


def _step(x, c, ctx, w, tgt, m, v):
    xi, yi, ci = _me()
    me = 4 * xi + 2 * yi + ci
    t_lat, D = x.shape[1], x.shape[2]
    ctx_rows = ctx.shape[1]
    T = ctx_rows + t_lat
    L, n_ssd, n_att = w['norm1_w'].shape[0], w['ssd_d'].shape[0], w['attn_sinks'].shape[0]
    H, DI, XBC = w['ssd_d'].shape[1], w['ssd_norm_w'].shape[1], w['ssd_conv_b'].shape[1]
    P = DI // H
    IN = w['ssd_w_in'].shape[2] * NDEV
    hd, n_q = w['attn_q_gain'].shape[1], w['attn_sinks'].shape[1]
    F2 = w['ffn_conv_b'].shape[1]
    G = F2 // NDEV
    nl = w['ada_w'].shape[2]
    ncc = ctx_rows // BLK
    perm = [_ffn_perm(s) for s in range(NDEV)]
    inv = [perm.index(d) for d in range(NDEV)]

    def reorder(t, order):
        return jnp.concatenate([t[..., o * G:(o + 1) * G] for o in order], axis=-1)

    def interleave(t):
        return reorder(t, perm)

    def deinterleave(t):
        return reorder(t, inv)

    def layer_weights(i):
        mixer = ['ssd_w_in', 'ssd_w_out'] if i % 2 == 0 else ['attn_w_qkv', 'attn_w_o']
        return [(n, i // 2) for n in mixer] + [('ffn_w_up', i), ('ffn_w_down', i)]

    def gather_start(i, zero):
        srcs = [(w[n][j] + zero).astype(MXU) for n, j in layer_weights(i)]
        lands = [lax.dynamic_update_index_in_dim(lax.empty((NDEV,) + t.shape, t.dtype), t, me, 0) for t in srcs]
        ss, rs, srcs, lands, tok = _xfer_start(srcs, lands, [None] * len(srcs), False, f"gather_start_{i}")
        return (ss, rs, srcs, lands), tok[0, 0]

    def gather_wait(pending, i, after):
        return _xfer_wait(*pending, [None] * len(pending[2]), False, after, f"gather_wait_{i}")

    pending, tok = gather_start(0, jnp.zeros((), F32))

    shapes_a = [(D,), w['ssd_conv_w'].shape, w['ffn_conv_w'].shape]
    g_a = _small_allgather(_pack([c[0], w['ssd_conv_w'], w['ffn_conv_w']]), "gather_cond")
    c_all, scw_all, fcw_all = _unpack(g_a, shapes_a)
    ssd_cw = scw_all.transpose(1, 2, 0, 3).reshape(n_ssd, 3, XBC)
    ffn_cw = jnp.concatenate([fcw_all[d] for d in perm], axis=-1)
    ffn_cb = interleave(w['ffn_conv_b'])[:, None, :]

    s_in = jnp.concatenate([c_all, w['c_ctx'][None], jnp.zeros((MOD_ROWS - NDEV - 1, D), F32)], axis=0)
    b_loc = lax.dynamic_slice(w['ada_b'], (0, me * nl), (L, nl))[:, None, :]
    mod_loc = _modulation(s_in, w['ada_w'], b_loc, "modulation")
    g_b = _small_allgather(_pack([mod_loc]), "gather_mod")
    (mod_all,) = _unpack(g_b, [mod_loc.shape])
    mod_lat = lax.dynamic_index_in_dim(mod_all, me, axis=2, keepdims=False)
    mod_ctx = mod_all[:, :, NDEV, :]
    to_mod = lambda t: t.transpose(1, 0, 2).reshape(L, 6, D)
    mod = jnp.stack([to_mod(mod_ctx), to_mod(mod_lat)], axis=1)

    w_in, w_out, w_qkv, w_o = [None] * n_ssd, [None] * n_ssd, [None] * n_att, [None] * n_att
    w_up, w_down = [None] * L, [None] * L

    cos, sin = _rope_tables(t_lat, ctx_rows, hd)
    e_f, e_b = _expand_matrix(H, P, False), _expand_matrix(H, P, True)
    et_f, et_b = e_f.T, e_b.T

    xs = jnp.concatenate([ctx[0], x[0]], axis=0)
    saved = []
    for i in range(L):
        j = i // 2
        s = dict(x0=xs)
        got = gather_wait(pending, i, mod if i == 0 else xs)
        if i % 2 == 0:
            w_in[j], w_out[j] = _cat_cols(got[0], "ssd_in_cat"), got[1].reshape(DI, D)
        else:
            w_qkv[j], w_o[j] = got[0], got[1].reshape(n_q * hd, D)
        w_up[i], w_down[i] = got[2], got[3].reshape(F2 // 2, D)
        if i + 1 < L:
            pending, tok = gather_start(i + 1, jnp.minimum(jnp.abs(got[1][0, 0, 0].astype(F32)), 0.0))
        nw1, nw2 = w['norm1_w'][i][None] + tok, w['norm2_w'][i][None]
        s['h1'] = _normmod(xs, nw1, mod[i], 0, ctx_rows, "normmod")
        if i % 2 == 0:
            s['zx'] = _mm(s['h1'], w_in[j], tm=768, tn=1152, tk=2048, name="mm_ssd_in")
            s['cw'], s['cb'] = ssd_cw[j], w['ssd_conv_b'][j][None]
            s['xbc'] = _ssd_conv(s['zx'], s['cw'], s['cb'], DI, ctx_rows, "ssd_conv")
            s['dtb'] = jnp.concatenate([w['ssd_dt_bias_f'][j], w['ssd_dt_bias_b'][j]])[None]
            s['alog'] = jnp.concatenate([w['ssd_a_log_f'][j], w['ssd_a_log_b'][j]])[None]
            s['yf'], s['hin_f'] = _ssd_scan(s['xbc'], s['zx'], s['dtb'], s['alog'], e_f, H, ncc, False, "ssd_scan_f")
            s['yb'], s['hin_b'] = _ssd_scan(s['xbc'], s['zx'], s['dtb'], s['alog'], e_b, H, ncc, True, "ssd_scan_b")
            s['dexp'], s['snw'] = jnp.repeat(w['ssd_d'][j], P)[None], w['ssd_norm_w'][j][None]
            s['o'] = _ssd_finish(s['yf'], s['yb'], s['xbc'], s['zx'], s['dexp'], s['snw'], "ssd_finish")
            s['mix'], x1 = _mm(s['o'], w_out[j], tm=768, tn=1024, tk=2048, name="mm_ssd_out",
                               resid=xs, gate=mod[i][:, 2], ctx_rows=ctx_rows)
        else:
            s['qkv'] = _mm(s['h1'], w_qkv[j], tm=768, tn=384, tk=2048, name="mm_qkv", bslots=_ident)
            s['qg'], s['kg'] = w['attn_q_gain'][j][None], w['attn_k_gain'][j][None]
            s['qr'], s['kr'], s['vb'] = _qk_prep(s['qkv'], s['qg'], s['kg'], cos, sin, n_q, ctx_rows, "qk_prep")
            s['o'] = _attn_fwd(s['qr'], s['kr'], s['vb'], w['attn_sinks'][j], n_q, ctx_rows, "attn_fwd")
            s['mix'], x1 = _mm(s['o'], w_o[j], tm=768, tn=1024, tk=2048, name="mm_attn_out",
                               resid=xs, gate=mod[i][:, 2], ctx_rows=ctx_rows)
        s['x1'] = x1
        s['h2'] = _normmod(x1, nw2, mod[i], 1, ctx_rows, "normmod")
        s['u'] = _mm(s['h2'], w_up[i], tm=768, tn=1408, tk=2048, name="mm_ffn_up", bslots=_ffn_perm)
        s['a'] = _ffn_mid(s['u'], ffn_cw[i], ffn_cb[i], ctx_rows, "ffn_mid")
        s['f'], xs = _mm(s['a'], w_down[i], tm=768, tn=1024, tk=1408, name="mm_ffn_down",
                         resid=x1, gate=mod[i][:, 5], ctx_rows=ctx_rows)
        saved.append(s)

    dx, lacc = _loss_head(xs, tgt[0], ctx_rows, "loss_head")
    loss = lax.psum(lacc[0, 0], ("x", "y", "c"))

    gbig = {name: [None] * w[name].shape[0] for name in BIG}
    gs = {name: [None] * w[name].shape[0] for name in SMALL + ['ssd_conv_w', 'ffn_conv_w']}
    dmod = [None] * L
    lands = {n: lax.empty((NDEV,) + w[n].shape, XFER) for n in BIG}
    pend_ffn, pend_mix, tok = None, None, jnp.zeros((), F32)

    def scatter_start(keys, tag):
        srcs = [gbig[n][j] for n, j in keys]
        for (n, j), g in zip(keys, srcs):
            own = lax.dynamic_index_in_dim(g, me, 0, keepdims=False)
            lands[n] = lax.dynamic_update_slice(lands[n], own[None, None], (me, j, 0, 0))
        ss, rs, srcs, got, t = _xfer_start(srcs, [lands[n] for n, _ in keys], [j for _, j in keys], True, f"scatter_start_{tag}")
        return (ss, rs, srcs, got, keys, tag), t[0, 0]

    def scatter_wait(p, after):
        ss, rs, srcs, got, keys, tag = p
        got = _xfer_wait(ss, rs, srcs, got, [j for _, j in keys], True, after, f"scatter_wait_{tag}")
        for (n, _), t in zip(keys, got):
            lands[n] = t

    for i in reversed(range(L)):
        j = i // 2
        s = saved[i]
        nw1, nw2 = w['norm1_w'][i][None], w['norm2_w'][i][None]
        dm2, dg2 = _gate_bwd(dx, s['f'], mod[i] + tok, 1, ctx_rows, "gate_bwd")
        da = _mm(dm2, w_down[i], tb=True, out_dtype=MXU, tm=768, tn=1408, tk=2048, name="mm_ffn_down_dx")
        gbig['ffn_w_down'][i] = _mm(s['a'], dm2, ta=True, out_dtype=XFER, tm=1408, tn=1024, tk=1056,
                                    name="mm_ffn_down_dw").reshape(NDEV, -1, D)
        du, gcw = _ffn_mid_bwd(s['u'], da, ffn_cw[i], ffn_cb[i], ctx_rows, "ffn_mid_bwd")
        gcw = deinterleave(gcw)
        gs['ffn_conv_w'][i], gs['ffn_conv_b'][i] = gcw[0:3], gcw[3]
        dh2 = _mm(du, w_up[i], tb=True, out_dtype=MXU, tm=768, tn=1024, tk=1408, name="mm_ffn_up_dx", bslots=_ffn_perm)
        gbig['ffn_w_up'][i] = _mm(s['h2'], du, ta=True, out_dtype=XFER, tm=1024, tn=1408, tk=1056, name="mm_ffn_up_dw",
                                  oslots=_ffn_perm)
        dx1, sums2 = _normmod_bwd(s['x1'], nw2, mod[i], dh2, dx, 1, ctx_rows, "normmod_bwd")
        if pend_ffn is not None:
            scatter_wait(pend_ffn, dx1)
        pend_ffn, tok = scatter_start(layer_weights(i)[2:], f"ffn_{i}")
        dmix, dg1 = _gate_bwd(dx1, s['mix'], mod[i] + tok, 0, ctx_rows, "gate_bwd")
        if i % 2 == 0:
            do = _mm(dmix, w_out[j], tb=True, out_dtype=MXU, tm=768, tn=1024, tk=2048, name="mm_ssd_out_dx")
            gbig['ssd_w_out'][j] = _mm(s['o'], dmix, ta=True, out_dtype=XFER, tm=1024, tn=1024, tk=1056,
                                       name="mm_ssd_out_dw").reshape(NDEV, -1, D)
            dy, dzx, fs, dd = _ssd_finish_bwd(s['yf'], s['yb'], s['xbc'], s['zx'], s['dexp'], s['snw'], do, et_f, (T, IN),
                                              "ssd_finish_bwd")
            acc = _ssd_scan_bwd(s['xbc'], s['zx'], s['dtb'], s['alog'], e_f, et_f, s['dexp'], s['hin_f'], dy, None,
                                H, ncc, False, "ssd_scan_bwd_f")
            dxbc, ddt, ssm = _ssd_scan_bwd(s['xbc'], s['zx'], s['dtb'], s['alog'], e_b, et_b, s['dexp'], s['hin_b'], dy, acc,
                                           H, ncc, True, "ssd_scan_bwd_b")
            dzx, gscw = _ssd_conv_bwd(s['zx'], dxbc, dzx, s['cw'], s['cb'], DI, ctx_rows, "ssd_conv_bwd")
            dzx = _put_cols(dzx, ddt, IN // LANE - 1, "ssd_put_ddt")
            dh1 = _mm(dzx, w_in[j], tb=True, out_dtype=MXU, tm=768, tn=1024, tk=1152, name="mm_ssd_in_dx")
            dwi = _mm(s['h1'], dzx, ta=True, tm=1024, tn=1152, tk=1056, name="mm_ssd_in_dw")
            gbig['ssd_w_in'][j] = _split_cols(dwi, NDEV, XFER, "ssd_in_split")
            gs['ssd_conv_w'][j], gs['ssd_conv_b'][j] = gscw[0:3], gscw[3]
            gs['ssd_dt_bias_f'][j], gs['ssd_dt_bias_b'][j] = ssm[1, :H], ssm[1, H:]
            gs['ssd_a_log_f'][j], gs['ssd_a_log_b'][j] = ssm[0, :H], ssm[0, H:]
            gs['ssd_d'][j], gs['ssd_norm_w'][j] = dd[1, :H], fs[0]
        else:
            do = _mm(dmix, w_o[j], tb=True, out_dtype=MXU, tm=768, tn=1024, tk=2048, name="mm_attn_out_dx")
            gbig['attn_w_o'][j] = _mm(s['o'], dmix, ta=True, out_dtype=XFER, tm=1024, tn=1024, tk=1056,
                                      name="mm_attn_out_dw").reshape(NDEV, -1, D)
            dq, dk, dv, dsk = _attn_bwd(s['qr'], s['kr'], s['vb'], w['attn_sinks'][j], do, n_q, ctx_rows, "attn_bwd")
            dqkv, gg = _qk_prep_bwd(s['qkv'], s['qg'], s['kg'], cos, sin, dq, dk, dv, n_q, ctx_rows, "qk_prep_bwd")
            dh1 = _mm(dqkv, w_qkv[j], tb=True, out_dtype=MXU, tm=768, tn=1024, tk=384, name="mm_qkv_dx", bslots=_ident)
            gbig['attn_w_qkv'][j] = _mm(s['h1'], dqkv, ta=True, out_dtype=XFER, tm=1024, tn=384, tk=1056, name="mm_qkv_dw",
                                        oslots=_ident)
            gs['attn_q_gain'][j], gs['attn_k_gain'][j] = gg[0], gg[1]
            gs['attn_sinks'][j] = dsk.reshape(ATTN_KV_HEADS, 8, LANE)[:, 0, :n_q // ATTN_KV_HEADS].reshape(n_q)
        dx, sums1 = _normmod_bwd(s['x0'], nw1, mod[i], dh1, dx1, 0, ctx_rows, "normmod_bwd")
        gs['norm1_w'][i], gs['norm2_w'][i] = sums1[0, 2] + sums1[1, 2], sums2[0, 2] + sums2[1, 2]
        dmod[i] = jnp.stack([sums1[:, 0], sums1[:, 1], dg1[:, 0], sums2[:, 0], sums2[:, 1], dg2[:, 0]], axis=1)
        if pend_mix is not None:
            scatter_wait(pend_mix, dx)
        pend_mix, tok = scatter_start(layer_weights(i)[:2], f"mix_{i}")
    grad_x = dx[ctx_rows:][None]
    dmod = jnp.stack(dmod)
    dmod_ctx, dmod_lat = dmod[:, 0].reshape(L, 6 * D), dmod[:, 1].reshape(L, 6 * D)
    gs['ada_b'] = dmod_ctx + dmod_lat

    out = {}

    small_g = [jnp.stack(gs[n]) if isinstance(gs[n], list) else gs[n] for n in SMALL]
    extras = [jnp.stack(gs['ssd_conv_w']), jnp.stack(gs['ffn_conv_w'])]
    shapes_c = [w[n].shape for n in SMALL] + [e.shape for e in extras]
    g_c = _small_allgather(_pack(small_g + extras), "gather_small")
    zeros = [jnp.zeros(e.shape, F32) for e in extras]
    res = _adamw_small(g_c, _pack([w[n] for n in SMALL] + zeros), _pack([m[n] for n in SMALL] + zeros),
                       _pack([v[n] for n in SMALL] + zeros), "adamw_small")
    res = [_unpack(r, shapes_c) for r in res]
    for k, n in enumerate(SMALL):
        out[n] = tuple(r[k] for r in res)
    g_scw, g_fcw = res[0][len(SMALL)], res[0][len(SMALL) + 1]
    g_scw = lax.dynamic_index_in_dim(g_scw.reshape(n_ssd, 3, NDEV, XBC // NDEV), me, axis=2, keepdims=False)
    g_fcw = lax.dynamic_index_in_dim(g_fcw.reshape(L, 3, NDEV, G), me, axis=2, keepdims=False)
    conv = ['ssd_conv_w', 'ffn_conv_w']
    res = _adamw_small(_pack([g_scw, g_fcw])[None], _pack([w[n] for n in conv]), _pack([m[n] for n in conv]),
                       _pack([v[n] for n in conv]), "adamw_conv")
    res = [_unpack(r, [w[n].shape for n in conv]) for r in res]
    for k, n in enumerate(conv):
        out[n] = tuple(r[k] for r in res)

    g_m = _small_allgather(jnp.concatenate([dmod_lat, dmod_ctx], axis=0), "gather_dmod")
    all_lat, all_ctx = g_m[:, :L], g_m[:, L:]
    my_cols = lambda t: lax.dynamic_slice(t, (0, 0, me * nl), (NDEV, L, nl))
    dml = jnp.concatenate([my_cols(all_lat), jnp.zeros((MOD_ROWS - NDEV, L, nl), F32)], axis=0)
    g_ada, pc = _modulation_bwd(s_in, dml, my_cols(all_ctx), w['ada_w'], "modulation_bwd")
    out['ada_w'] = _adamw(g_ada[None], w['ada_w'], m['ada_w'], v['ada_w'], "adamw")
    g_d = _small_allgather(_pack([pc[0]]), "gather_cctx")
    res = _cctx_update(g_d, _pack([w['c_ctx']]), _pack([m['c_ctx']]), _pack([v['c_ctx']]), "adamw_cctx")
    out['c_ctx'] = tuple(_unpack(r, [(D,)])[0] for r in res)

    scatter_wait(pend_ffn, out['c_ctx'][0])
    last = [n for n, _ in layer_weights(0)[:2]]
    for name in [n for n in BIG if n not in last]:
        out[name] = _adamw(lands[name], w[name], m[name], v[name], "adamw")
    scatter_wait(pend_mix, out['ffn_w_down'][0])
    for name in last:
        out[name] = _adamw(lands[name], w[name], m[name], v[name], "adamw")

    return (loss, grad_x) + tuple(out[n][k] for k in range(4) for n in WEIGHTS)


def kernel(x, c, ctx, c_ctx, ada_w, ada_b, norm1_w, norm2_w, ssd_w_in, ssd_conv_w, ssd_conv_b, ssd_dt_bias_f, ssd_dt_bias_b, ssd_a_log_f, ssd_a_log_b, ssd_d, ssd_norm_w, ssd_w_out, attn_w_qkv, attn_q_gain, attn_k_gain, attn_sinks, attn_w_o, ffn_w_up, ffn_conv_w, ffn_conv_b, ffn_w_down, loss_target, m_c_ctx, m_ada_w, m_ada_b, m_norm1_w, m_norm2_w, m_ssd_w_in, m_ssd_conv_w, m_ssd_conv_b, m_ssd_dt_bias_f, m_ssd_dt_bias_b, m_ssd_a_log_f, m_ssd_a_log_b, m_ssd_d, m_ssd_norm_w, m_ssd_w_out, m_attn_w_qkv, m_attn_q_gain, m_attn_k_gain, m_attn_sinks, m_attn_w_o, m_ffn_w_up, m_ffn_conv_w, m_ffn_conv_b, m_ffn_w_down, v_c_ctx, v_ada_w, v_ada_b, v_norm1_w, v_norm2_w, v_ssd_w_in, v_ssd_conv_w, v_ssd_conv_b, v_ssd_dt_bias_f, v_ssd_dt_bias_b, v_ssd_a_log_f, v_ssd_a_log_b, v_ssd_d, v_ssd_norm_w, v_ssd_w_out, v_attn_w_qkv, v_attn_q_gain, v_attn_k_gain, v_attn_sinks, v_attn_w_o, v_ffn_w_up, v_ffn_conv_w, v_ffn_conv_b, v_ffn_w_down):
    w = dict(c_ctx=c_ctx, ada_w=ada_w, ada_b=ada_b, norm1_w=norm1_w, norm2_w=norm2_w, ssd_w_in=ssd_w_in, ssd_conv_w=ssd_conv_w, ssd_conv_b=ssd_conv_b, ssd_dt_bias_f=ssd_dt_bias_f, ssd_dt_bias_b=ssd_dt_bias_b, ssd_a_log_f=ssd_a_log_f, ssd_a_log_b=ssd_a_log_b, ssd_d=ssd_d, ssd_norm_w=ssd_norm_w, ssd_w_out=ssd_w_out, attn_w_qkv=attn_w_qkv, attn_q_gain=attn_q_gain, attn_k_gain=attn_k_gain, attn_sinks=attn_sinks, attn_w_o=attn_w_o, ffn_w_up=ffn_w_up, ffn_conv_w=ffn_conv_w, ffn_conv_b=ffn_conv_b, ffn_w_down=ffn_w_down)
    m = dict(c_ctx=m_c_ctx, ada_w=m_ada_w, ada_b=m_ada_b, norm1_w=m_norm1_w, norm2_w=m_norm2_w, ssd_w_in=m_ssd_w_in, ssd_conv_w=m_ssd_conv_w, ssd_conv_b=m_ssd_conv_b, ssd_dt_bias_f=m_ssd_dt_bias_f, ssd_dt_bias_b=m_ssd_dt_bias_b, ssd_a_log_f=m_ssd_a_log_f, ssd_a_log_b=m_ssd_a_log_b, ssd_d=m_ssd_d, ssd_norm_w=m_ssd_norm_w, ssd_w_out=m_ssd_w_out, attn_w_qkv=m_attn_w_qkv, attn_q_gain=m_attn_q_gain, attn_k_gain=m_attn_k_gain, attn_sinks=m_attn_sinks, attn_w_o=m_attn_w_o, ffn_w_up=m_ffn_w_up, ffn_conv_w=m_ffn_conv_w, ffn_conv_b=m_ffn_conv_b, ffn_w_down=m_ffn_w_down)
    v = dict(c_ctx=v_c_ctx, ada_w=v_ada_w, ada_b=v_ada_b, norm1_w=v_norm1_w, norm2_w=v_norm2_w, ssd_w_in=v_ssd_w_in, ssd_conv_w=v_ssd_conv_w, ssd_conv_b=v_ssd_conv_b, ssd_dt_bias_f=v_ssd_dt_bias_f, ssd_dt_bias_b=v_ssd_dt_bias_b, ssd_a_log_f=v_ssd_a_log_f, ssd_a_log_b=v_ssd_a_log_b, ssd_d=v_ssd_d, ssd_norm_w=v_ssd_norm_w, ssd_w_out=v_ssd_w_out, attn_w_qkv=v_attn_w_qkv, attn_q_gain=v_attn_q_gain, attn_k_gain=v_attn_k_gain, attn_sinks=v_attn_sinks, attn_w_o=v_attn_w_o, ffn_w_up=v_ffn_w_up, ffn_conv_w=v_ffn_conv_w, ffn_conv_b=v_ffn_conv_b, ffn_w_down=v_ffn_w_down)
    return _step(x, c, ctx, w, loss_target, m, v)
```

```python
import functools

import numpy as np
import jax
import jax.numpy as jnp
from jax import lax
from jax.experimental import pallas as pl
from jax.experimental.pallas import tpu as pltpu

F32 = jnp.float32
BF16 = jnp.bfloat16
MXU = BF16
XFER = BF16
NORM_EPS = 1e-6
VMEM_CAP = 56 * 1024 * 1024
HALO = 8
LANE = 128
NDEV = 8

GRID_W = 64
ROPE_THETA = 10000.0
ATTN_KV_HEADS = 4
ATTN_WINDOW = 128
BLK = 128
SSD_GROUPS = 8

ADAM_LR, ADAM_B1, ADAM_B2, ADAM_EPS, ADAM_WD, ADAM_STEP = 0.001, 0.9, 0.999, 1e-08, 0.01, 10

MESH_ID = pl.DeviceIdType.MESH


def _cparams(sem, est_bytes):
    lim = int(min(VMEM_CAP, max(16 * 1024 * 1024, est_bytes * 1.3 + (4 << 20))))
    return pltpu.CompilerParams(dimension_semantics=sem, vmem_limit_bytes=lim)


def _nbytes(shape, dtype):
    return int(np.prod(shape)) * jnp.dtype(dtype).itemsize


def _silu(x):
    return x * jax.nn.sigmoid(x)


def _dsilu(x):
    s = jax.nn.sigmoid(x)
    return s * (1.0 + x * (1.0 - s))


def _split3(v):
    h = v.astype(BF16)
    r = v - h.astype(F32)
    m = r.astype(BF16)
    l = (r - m.astype(F32)).astype(BF16)
    return h, m, l


def _dot(a, b, dn=(((1,), (0,)), ((), ()))):
    return lax.dot_general(a, b, dn, preferred_element_type=F32)


NT = (((1,), (1,)), ((), ()))
TN = (((0,), (0,)), ((), ()))


def _dot3_rhs(sel, v):
    return sum(_dot(sel, p) for p in _split3(v))


def _dot3_lhs(v, sel, dn=(((1,), (0,)), ((), ()))):
    return sum(_dot(p, sel, dn) for p in _split3(v))


def _dot2_stacked(vals, sel):
    pieces = []
    for v in vals:
        h, m, _ = _split3(v)
        pieces += [h, m]
    r = _dot(jnp.concatenate(pieces, axis=0), sel)
    out, row = [], 0
    for v in vals:
        n = v.shape[0]
        out.append(r[row:row + n] + r[row + n:row + 2 * n])
        row += 2 * n
    return out


def _expand2(vals, sel2):
    lhs = []
    for v in vals:
        h, m, _ = _split3(v)
        lhs.append(jnp.concatenate([h, m], axis=1))
    r = _dot(jnp.concatenate(lhs, axis=0), sel2)
    n = vals[0].shape[0]
    return [r[i * n:(i + 1) * n] for i in range(len(vals))]


def _ident(s):
    return s


def _ffn_perm(s):
    return (s % 2) * 4 + s // 2


def _mm(a, b, *, ta=False, tb=False, out_dtype=F32, tm, tn, tk, name, bslots=None, oslots=None,
        resid=None, gate=None, ctx_rows=0):
    M = a.shape[1] if ta else a.shape[0]
    K = a.shape[0] if ta else a.shape[1]
    if bslots is None:
        N = b.shape[0] if tb else b.shape[1]
    else:
        G = b.shape[2]
        N = b.shape[1] if tb else NDEV * G
        assert (NDEV * G == K) if tb else (b.shape[1] == K)
    tm, tn, tk = min(tm, M), min(tn, N), min(tk, K)
    if bslots is not None:
        if tb:
            tk = min(tk, G)
            assert G % tk == 0
        else:
            tn = min(tn, G)
            assert G % tn == 0
    if oslots is not None:
        Go = N // NDEV
        tn = min(tn, Go)
        assert Go % tn == 0
    assert M % tm == 0 and N % tn == 0 and K % tk == 0, (name, M, N, K, tm, tn, tk)
    nk = K // tk
    fused = resid is not None
    dn = (((0 if ta else 1,), (1 if tb else 0,)), ((), ()))

    def body(*refs):
        if fused:
            a_ref, b_ref, r_ref, g_ref, o_ref, x_ref = refs[:6]
            rest = refs[6:]
        else:
            a_ref, b_ref, o_ref = refs[:3]
            rest = refs[3:]
        bv = b_ref[0] if bslots is not None else b_ref[...]
        p = lax.dot_general(a_ref[...].astype(MXU), bv.astype(MXU), dn, preferred_element_type=F32)

        def finish(acc):
            if oslots is not None:
                o_ref[0] = acc.astype(o_ref.dtype)
            else:
                o_ref[...] = acc.astype(o_ref.dtype)
            if fused:
                row = pl.program_id(0) * tm + lax.broadcasted_iota(jnp.int32, (tm, 1), 0)
                g = jnp.where(row < ctx_rows, g_ref[0:1, :], g_ref[1:2, :])
                x_ref[...] = r_ref[...] + g * acc

        if nk == 1:
            finish(p)
        else:
            acc_ref = rest[0]
            k = pl.program_id(2)

            @pl.when(k == 0)
            def _():
                acc_ref[...] = p

            @pl.when(k > 0)
            def _():
                acc_ref[...] += p

            @pl.when(k == nk - 1)
            def _():
                finish(acc_ref[...])

    a_spec = pl.BlockSpec((tk, tm), lambda i, j, k: (k, i)) if ta else pl.BlockSpec((tm, tk), lambda i, j, k: (i, k))
    if bslots is None:
        b_spec = pl.BlockSpec((tn, tk), lambda i, j, k: (j, k)) if tb else pl.BlockSpec((tk, tn), lambda i, j, k: (k, j))
    elif tb:
        kpg = G // tk
        b_spec = pl.BlockSpec((1, tn, tk), lambda i, j, k: (bslots(k // kpg), j, k % kpg))
    else:
        npg = G // tn
        b_spec = pl.BlockSpec((1, tk, tn), lambda i, j, k: (bslots(j // npg), k, j % npg))
    if oslots is None:
        o_spec = pl.BlockSpec((tm, tn), lambda i, j, k: (i, j))
        o_shape = jax.ShapeDtypeStruct((M, N), out_dtype)
    else:
        opg = Go // tn
        o_spec = pl.BlockSpec((1, tm, tn), lambda i, j, k: (oslots(j // opg), i, j % opg))
        o_shape = jax.ShapeDtypeStruct((NDEV, M, Go), out_dtype)
    in_specs = [a_spec, b_spec]
    out_shape = [o_shape]
    out_specs = [o_spec]
    args = [a, b]
    est = 2 * (_nbytes((tm, tk), a.dtype) + _nbytes((tk, tn), b.dtype) + _nbytes((tm, tn), out_dtype)) + 3 * _nbytes((tm, tn), F32)
    if fused:
        in_specs += [o_spec, pl.BlockSpec((2, tn), lambda i, j, k: (0, j))]
        out_shape.append(jax.ShapeDtypeStruct((M, N), F32))
        out_specs.append(o_spec)
        args += [resid, gate]
        est += 4 * _nbytes((tm, tn), F32)
    scratch = [] if nk == 1 else [pltpu.VMEM((tm, tn), F32)]
    res = pl.pallas_call(
        body, name=name, grid=(M // tm, N // tn, nk), in_specs=in_specs, out_specs=out_specs, out_shape=out_shape,
        scratch_shapes=scratch, compiler_params=_cparams(("parallel", "parallel", "arbitrary"), est),
    )(*args)
    return res if fused else res[0]


def _stream_of(i, tr, ctx_rows):
    return jnp.where(i * tr < ctx_rows, 0, 1)


def _acc_by_stream(sums_ref, part, i, n_ctx):
    @pl.when((i == 0) | (i == n_ctx))
    def _():
        sums_ref[0] = part

    @pl.when((i != 0) & (i != n_ctx))
    def _():
        sums_ref[0] += part


def _normmod(x, nw, mod, which, ctx_rows, name, tr=256):
    T, D = x.shape
    tr = min(tr, ctx_rows)
    assert T % tr == 0 and ctx_rows % tr == 0
    s_sh, s_sc = 3 * which, 3 * which + 1

    def body(x_ref, nw_ref, mod_ref, h_ref):
        xv = x_ref[...]
        r = lax.rsqrt(jnp.mean(xv * xv, axis=-1, keepdims=True) + NORM_EPS)
        y = (xv * r) * nw_ref[...]
        h_ref[...] = (y * (1.0 + mod_ref[0, s_sc:s_sc + 1, :]) + mod_ref[0, s_sh:s_sh + 1, :]).astype(h_ref.dtype)

    return pl.pallas_call(
        body, name=name, grid=(T // tr,),
        in_specs=[pl.BlockSpec((tr, D), lambda i: (i, 0)), pl.BlockSpec((1, D), lambda i: (0, 0)),
                  pl.BlockSpec((1, 6, D), lambda i: (_stream_of(i, tr, ctx_rows), 0, 0))],
        out_specs=pl.BlockSpec((tr, D), lambda i: (i, 0)),
        out_shape=jax.ShapeDtypeStruct((T, D), MXU),
        compiler_params=_cparams(("parallel",), 10 * _nbytes((tr, D), F32)),
    )(x, nw, mod)


def _normmod_bwd(x, nw, mod, dh, dx_in, which, ctx_rows, name, tr=256):
    T, D = x.shape
    tr = min(tr, ctx_rows)
    s_sc = 3 * which + 1
    n_ctx = ctx_rows // tr

    def body(x_ref, nw_ref, mod_ref, dh_ref, dxi_ref, dx_ref, sums_ref):
        i = pl.program_id(0)
        xv = x_ref[...]
        r = lax.rsqrt(jnp.mean(xv * xv, axis=-1, keepdims=True) + NORM_EPS)
        xh = xv * r
        dh_v = dh_ref[...].astype(F32)
        sc1 = 1.0 + mod_ref[0, s_sc:s_sc + 1, :]
        nwv = nw_ref[...]
        dxh = dh_v * (nwv * sc1)
        dx_ref[...] = dxi_ref[...] + r * (dxh - xh * jnp.mean(dxh * xh, axis=-1, keepdims=True))
        t = dh_v * xh
        part = jnp.concatenate([jnp.sum(dh_v, axis=0, keepdims=True), jnp.sum(t * nwv, axis=0, keepdims=True),
                                jnp.sum(t * sc1, axis=0, keepdims=True), jnp.zeros((5, D), F32)], axis=0)
        _acc_by_stream(sums_ref, part, i, n_ctx)

    row = pl.BlockSpec((tr, D), lambda i: (i, 0))
    return pl.pallas_call(
        body, name=name, grid=(T // tr,),
        in_specs=[row, pl.BlockSpec((1, D), lambda i: (0, 0)),
                  pl.BlockSpec((1, 6, D), lambda i: (_stream_of(i, tr, ctx_rows), 0, 0)), row, row],
        out_specs=[row, pl.BlockSpec((1, 8, D), lambda i: (_stream_of(i, tr, ctx_rows), 0, 0))],
        out_shape=[jax.ShapeDtypeStruct((T, D), F32), jax.ShapeDtypeStruct((2, 8, D), F32)],
        compiler_params=_cparams(("arbitrary",), 16 * _nbytes((tr, D), F32)),
    )(x, nw, mod, dh, dx_in)


def _gate_bwd(dx, mix, mod, which, ctx_rows, name, tr=256):
    T, D = dx.shape
    tr = min(tr, ctx_rows)
    s_g = 3 * which + 2
    n_ctx = ctx_rows // tr

    def body(dx_ref, mix_ref, mod_ref, dm_ref, sums_ref):
        i = pl.program_id(0)
        dxv = dx_ref[...]
        dm_ref[...] = (dxv * mod_ref[0, s_g:s_g + 1, :]).astype(dm_ref.dtype)
        part = jnp.concatenate([jnp.sum(dxv * mix_ref[...], axis=0, keepdims=True), jnp.zeros((7, D), F32)], axis=0)
        _acc_by_stream(sums_ref, part, i, n_ctx)

    row = pl.BlockSpec((tr, D), lambda i: (i, 0))
    return pl.pallas_call(
        body, name=name, grid=(T // tr,),
        in_specs=[row, row, pl.BlockSpec((1, 6, D), lambda i: (_stream_of(i, tr, ctx_rows), 0, 0))],
        out_specs=[row, pl.BlockSpec((1, 8, D), lambda i: (_stream_of(i, tr, ctx_rows), 0, 0))],
        out_shape=[jax.ShapeDtypeStruct((T, D), MXU), jax.ShapeDtypeStruct((2, 8, D), F32)],
        compiler_params=_cparams(("arbitrary",), 10 * _nbytes((tr, D), F32)),
    )(dx, mix, mod)


def _halo_specs(tr, tn, n_row_tiles, col_of):
    g = tr // HALO
    last = n_row_tiles * g - 1
    return [pl.BlockSpec((HALO, tn), lambda j, i: (jnp.maximum(i * g - 1, 0), col_of(j))),
            pl.BlockSpec((tr, tn), lambda j, i: (i, col_of(j))),
            pl.BlockSpec((HALO, tn), lambda j, i: (jnp.minimum((i + 1) * g, last), col_of(j)))]


def _ext(p_ref, m_ref, n_ref):
    return jnp.concatenate([p_ref[...].astype(F32), m_ref[...].astype(F32), n_ref[...].astype(F32)], axis=0)


def _seq_masks(i, tr, ctx_rows, total_rows):
    row = i * tr - HALO + lax.broadcasted_iota(jnp.int32, (tr + 2 * HALO, 1), 0)
    has_prev = (row != 0) & (row != ctx_rows)
    has_next = (row != ctx_rows - 1) & (row != total_rows - 1)
    return has_prev, has_next


def _shift_down(e):
    return pltpu.roll(e, 1, 0)


def _shift_up(e):
    return pltpu.roll(e, e.shape[0] - 1, 0)


def _conv3(e, w, has_prev, has_next):
    prev = jnp.where(has_prev, _shift_down(e), 0.0)
    nxt = jnp.where(has_next, _shift_up(e), 0.0)
    return prev * w[0:1, :] + e * w[1:2, :] + nxt * w[2:3, :]


def _conv3_t(d, w, has_prev, has_next):
    from_next = jnp.where(has_next, _shift_up(d), 0.0)
    from_prev = jnp.where(has_prev, _shift_down(d), 0.0)
    return from_next * w[0:1, :] + d * w[1:2, :] + from_prev * w[2:3, :]


def _conv_wgrad(d, e, has_prev, has_next):
    n = e.shape[0]
    c = slice(HALO, n - HALO)
    prev = jnp.where(has_prev, _shift_down(e), 0.0)
    nxt = jnp.where(has_next, _shift_up(e), 0.0)
    dc = d[c]
    return jnp.concatenate([jnp.sum(dc * prev[c], axis=0, keepdims=True), jnp.sum(dc * e[c], axis=0, keepdims=True),
                            jnp.sum(dc * nxt[c], axis=0, keepdims=True), jnp.sum(dc, axis=0, keepdims=True),
                            jnp.zeros((4, e.shape[1]), F32)], axis=0)


def _acc_first(ref, part, i):
    @pl.when(i == 0)
    def _():
        ref[...] = part

    @pl.when(i > 0)
    def _():
        ref[...] += part


def _ffn_mid(u, cw, cb, ctx_rows, name, tr=128):
    T, F2 = u.shape
    G = F2 // NDEV
    tr = min(tr, ctx_rows)
    nr, nc = T // tr, NDEV // 2

    def body(up, um, un, w_ref, b_ref, a_ref):
        i = pl.program_id(1)
        hp, hn = _seq_masks(i, tr, ctx_rows, T)
        uc = _conv3(_ext(up, um, un), w_ref[...], hp, hn)[HALO:HALO + tr] + b_ref[...]
        a_ref[...] = (_silu(uc[:, G:]) * uc[:, :G]).astype(a_ref.dtype)

    return pl.pallas_call(
        body, name=name, grid=(nc, nr),
        in_specs=_halo_specs(tr, 2 * G, nr, lambda j: j) + [pl.BlockSpec((3, 2 * G), lambda j, i: (0, j)),
                                                             pl.BlockSpec((1, 2 * G), lambda j, i: (0, j))],
        out_specs=pl.BlockSpec((tr, G), lambda j, i: (i, j)),
        out_shape=jax.ShapeDtypeStruct((T, F2 // 2), MXU),
        compiler_params=_cparams(("parallel", "parallel"), 12 * _nbytes((tr + 16, 2 * G), F32)),
    )(u, u, u, cw, cb)


def _ffn_mid_bwd(u, da, cw, cb, ctx_rows, name, tr=128):
    T, F2 = u.shape
    G = F2 // NDEV
    tr = min(tr, ctx_rows)
    nr, nc = T // tr, NDEV // 2

    def body(up, um, un, dp, dm, dn_, w_ref, b_ref, du_ref, gw_ref):
        i = pl.program_id(1)
        hp, hn = _seq_masks(i, tr, ctx_rows, T)
        e = _ext(up, um, un)
        w = w_ref[...]
        uc = _conv3(e, w, hp, hn) + b_ref[...]
        val, gt = uc[:, :G], uc[:, G:]
        dav = _ext(dp, dm, dn_)
        duc = jnp.concatenate([dav * _silu(gt), dav * val * _dsilu(gt)], axis=1)
        du_ref[...] = _conv3_t(duc, w, hp, hn)[HALO:HALO + tr].astype(du_ref.dtype)
        _acc_first(gw_ref, _conv_wgrad(duc, e, hp, hn), i)

    du, gw = pl.pallas_call(
        body, name=name, grid=(nc, nr),
        in_specs=(_halo_specs(tr, 2 * G, nr, lambda j: j) + _halo_specs(tr, G, nr, lambda j: j)
                  + [pl.BlockSpec((3, 2 * G), lambda j, i: (0, j)), pl.BlockSpec((1, 2 * G), lambda j, i: (0, j))]),
        out_specs=[pl.BlockSpec((tr, 2 * G), lambda j, i: (i, j)), pl.BlockSpec((8, 2 * G), lambda j, i: (0, j))],
        out_shape=[jax.ShapeDtypeStruct((T, F2), MXU), jax.ShapeDtypeStruct((8, F2), F32)],
        compiler_params=_cparams(("parallel", "arbitrary"), 24 * _nbytes((tr + 16, 2 * G), F32)),
    )(u, u, u, da, da, da, cw, cb)
    return du, gw


def _ssd_conv(zx, cw, cb, col0, ctx_rows, name, tr=256, tn=512):
    T = zx.shape[0]
    C = cw.shape[1]
    tr, tn = min(tr, ctx_rows), min(tn, C)
    assert C % tn == 0 and col0 % tn == 0
    nr, nc, cb0 = T // tr, C // tn, col0 // tn

    def body(zp, zm, zn, w_ref, b_ref, o_ref):
        i = pl.program_id(1)
        hp, hn = _seq_masks(i, tr, ctx_rows, T)
        o_ref[...] = _silu(_conv3(_ext(zp, zm, zn), w_ref[...], hp, hn)[HALO:HALO + tr] + b_ref[...])

    return pl.pallas_call(
        body, name=name, grid=(nc, nr),
        in_specs=_halo_specs(tr, tn, nr, lambda j: j + cb0) + [pl.BlockSpec((3, tn), lambda j, i: (0, j)),
                                                                pl.BlockSpec((1, tn), lambda j, i: (0, j))],
        out_specs=pl.BlockSpec((tr, tn), lambda j, i: (i, j)),
        out_shape=jax.ShapeDtypeStruct((T, C), F32),
        compiler_params=_cparams(("parallel", "parallel"), 12 * _nbytes((tr + 16, tn), F32)),
    )(zx, zx, zx, cw, cb)


def _ssd_conv_bwd(zx, dxbc, dzx, cw, cb, col0, ctx_rows, name, tr=256, tn=512):
    T = zx.shape[0]
    C = cw.shape[1]
    tr, tn = min(tr, ctx_rows), min(tn, C)
    nr, nc, cb0 = T // tr, C // tn, col0 // tn

    def body(zp, zm, zn, dp, dm, dn_, w_ref, b_ref, dzx_in, dz_ref, gw_ref):
        del dzx_in
        i = pl.program_id(1)
        hp, hn = _seq_masks(i, tr, ctx_rows, T)
        e = _ext(zp, zm, zn)
        w = w_ref[...]
        pre = _conv3(e, w, hp, hn) + b_ref[...]
        dpre = _ext(dp, dm, dn_) * _dsilu(pre)
        dz_ref[...] = _conv3_t(dpre, w, hp, hn)[HALO:HALO + tr].astype(dz_ref.dtype)
        _acc_first(gw_ref, _conv_wgrad(dpre, e, hp, hn), i)

    return pl.pallas_call(
        body, name=name, grid=(nc, nr),
        in_specs=(_halo_specs(tr, tn, nr, lambda j: j + cb0) + _halo_specs(tr, tn, nr, lambda j: j)
                  + [pl.BlockSpec((3, tn), lambda j, i: (0, j)), pl.BlockSpec((1, tn), lambda j, i: (0, j)),
                     pl.BlockSpec(memory_space=pl.ANY)]),
        out_specs=[pl.BlockSpec((tr, tn), lambda j, i: (i, j + cb0)), pl.BlockSpec((8, tn), lambda j, i: (0, j))],
        out_shape=[jax.ShapeDtypeStruct(dzx.shape, dzx.dtype), jax.ShapeDtypeStruct((8, C), F32)],
        input_output_aliases={8: 0},
        compiler_params=_cparams(("parallel", "arbitrary"), 24 * _nbytes((tr + 16, tn), F32)),
    )(zx, zx, zx, dxbc, dxbc, dxbc, cw, cb, dzx)


def _cat_cols(w3, name, tr=256):
    n, K, G = w3.shape
    tr = min(tr, K)

    def body(w_ref, o_ref):
        o_ref[...] = jnp.concatenate([w_ref[d].astype(F32) for d in range(n)], axis=1).astype(o_ref.dtype)

    return pl.pallas_call(
        body, name=name, grid=(K // tr,),
        in_specs=[pl.BlockSpec((n, tr, G), lambda i: (0, i, 0))], out_specs=pl.BlockSpec((tr, n * G), lambda i: (i, 0)),
        out_shape=jax.ShapeDtypeStruct((K, n * G), w3.dtype),
        compiler_params=_cparams(("parallel",), 6 * _nbytes((tr, n * G), F32)),
    )(w3)


def _split_cols(g, n, out_dtype, name, tr=256):
    K, NG = g.shape
    G = NG // n
    tr = min(tr, K)

    def body(g_ref, o_ref):
        for d in range(n):
            o_ref[d] = g_ref[:, d * G:(d + 1) * G].astype(o_ref.dtype)

    return pl.pallas_call(
        body, name=name, grid=(K // tr,),
        in_specs=[pl.BlockSpec((tr, NG), lambda i: (i, 0))], out_specs=pl.BlockSpec((n, tr, G), lambda i: (0, i, 0)),
        out_shape=jax.ShapeDtypeStruct((n, K, G), out_dtype),
        compiler_params=_cparams(("parallel",), 6 * _nbytes((tr, NG), F32)),
    )(g)


def _put_cols(dst, src, col_blk, name, tr=256):
    T, W = src.shape
    tr = min(tr, T)

    def body(s_ref, d_in, o_ref):
        del d_in
        o_ref[...] = s_ref[...].astype(o_ref.dtype)

    return pl.pallas_call(
        body, name=name, grid=(T // tr,),
        in_specs=[pl.BlockSpec((tr, W), lambda i: (i, 0)), pl.BlockSpec(memory_space=pl.ANY)],
        out_specs=pl.BlockSpec((tr, W), lambda i: (i, col_blk)),
        out_shape=jax.ShapeDtypeStruct(dst.shape, dst.dtype),
        input_output_aliases={1: 0},
        compiler_params=_cparams(("parallel",), 8 * _nbytes((tr, W), F32)),
    )(src, dst)


def _rope_tables(t_lat, ctx_rows, hd):
    half, quarter = hd // 2, hd // 4
    pos = jnp.arange(t_lat)
    row = (pos // GRID_W).astype(F32)
    col = (pos % GRID_W).astype(F32)
    inv_freq = ROPE_THETA ** (-jnp.arange(0, half, 2, dtype=F32) / half)
    ar, ac = row[:, None] * inv_freq[None, :], col[:, None] * inv_freq[None, :]
    cos = jnp.concatenate([jnp.cos(ar), jnp.cos(ar), jnp.cos(ac), jnp.cos(ac)], axis=1)
    sin = jnp.concatenate([-jnp.sin(ar), jnp.sin(ar), -jnp.sin(ac), jnp.sin(ac)], axis=1)
    del quarter
    cos = jnp.concatenate([jnp.ones((ctx_rows, hd), F32), cos], axis=0)
    sin = jnp.concatenate([jnp.zeros((ctx_rows, hd), F32), sin], axis=0)
    return cos, sin


def _partner(y):
    hd = y.shape[1]
    q = hd // 4
    lane = lax.broadcasted_iota(jnp.int32, y.shape, 1)
    return jnp.where((lane % (2 * q)) < q, pltpu.roll(y, hd - q, 1), pltpu.roll(y, q, 1))


def _qk_prep(qkv, qg, kg, cos, sin, n_q, ctx_rows, name, tr=256):
    T = qkv.shape[0]
    hd = qg.shape[1]
    n_kv = ATTN_KV_HEADS
    tr = min(tr, ctx_rows)

    def body(x_ref, qg_ref, kg_ref, c_ref, s_ref, q_ref, k_ref, v_ref):
        cv, sv = c_ref[...], s_ref[...]
        for h in range(n_q + n_kv):
            xh = x_ref[:, h * hd:(h + 1) * hd]
            r = lax.rsqrt(jnp.mean(xh * xh, axis=-1, keepdims=True) + NORM_EPS)
            y = (xh * r) * (qg_ref[...] if h < n_q else kg_ref[...])
            rot = y * cv + _partner(y) * sv
            if h < n_q:
                q_ref[:, h * hd:(h + 1) * hd] = rot.astype(q_ref.dtype)
            else:
                k_ref[:, (h - n_q) * hd:(h - n_q + 1) * hd] = rot.astype(k_ref.dtype)
        v_ref[...] = x_ref[:, (n_q + n_kv) * hd:].astype(v_ref.dtype)

    W = qkv.shape[1]
    return pl.pallas_call(
        body, name=name, grid=(T // tr,),
        in_specs=[pl.BlockSpec((tr, W), lambda i: (i, 0)), pl.BlockSpec((1, hd), lambda i: (0, 0)),
                  pl.BlockSpec((1, hd), lambda i: (0, 0)), pl.BlockSpec((tr, hd), lambda i: (i, 0)),
                  pl.BlockSpec((tr, hd), lambda i: (i, 0))],
        out_specs=[pl.BlockSpec((tr, n_q * hd), lambda i: (i, 0)), pl.BlockSpec((tr, n_kv * hd), lambda i: (i, 0)),
                   pl.BlockSpec((tr, n_kv * hd), lambda i: (i, 0))],
        out_shape=[jax.ShapeDtypeStruct((T, n_q * hd), MXU), jax.ShapeDtypeStruct((T, n_kv * hd), MXU),
                   jax.ShapeDtypeStruct((T, n_kv * hd), MXU)],
        compiler_params=_cparams(("parallel",), 6 * _nbytes((tr, W), F32)),
    )(qkv, qg, kg, cos, sin)


def _qk_prep_bwd(qkv, qg, kg, cos, sin, dq, dk, dv, n_q, ctx_rows, name, tr=256):
    T, W = qkv.shape
    hd = qg.shape[1]
    n_kv = ATTN_KV_HEADS
    tr = min(tr, ctx_rows)

    def body(x_ref, qg_ref, kg_ref, c_ref, s_ref, dq_ref, dk_ref, dv_ref, o_ref, g_ref):
        i = pl.program_id(0)
        cv, sv = c_ref[...], s_ref[...]
        gq = jnp.zeros((1, hd), F32)
        gk = jnp.zeros((1, hd), F32)
        for h in range(n_q + n_kv):
            xh = x_ref[:, h * hd:(h + 1) * hd]
            gain = qg_ref[...] if h < n_q else kg_ref[...]
            drot = (dq_ref[:, h * hd:(h + 1) * hd] if h < n_q else dk_ref[:, (h - n_q) * hd:(h - n_q + 1) * hd]).astype(F32)
            dy = drot * cv + _partner(drot * sv)
            r = lax.rsqrt(jnp.mean(xh * xh, axis=-1, keepdims=True) + NORM_EPS)
            xn = xh * r
            gsum = jnp.sum(dy * xn, axis=0, keepdims=True)
            if h < n_q:
                gq = gq + gsum
            else:
                gk = gk + gsum
            dxn = dy * gain
            o_ref[:, h * hd:(h + 1) * hd] = (r * (dxn - xn * jnp.mean(dxn * xn, axis=-1, keepdims=True))).astype(o_ref.dtype)
        o_ref[:, (n_q + n_kv) * hd:] = dv_ref[...].astype(o_ref.dtype)
        _acc_first(g_ref, jnp.concatenate([gq, gk, jnp.zeros((6, hd), F32)], axis=0), i)

    return pl.pallas_call(
        body, name=name, grid=(T // tr,),
        in_specs=[pl.BlockSpec((tr, W), lambda i: (i, 0)), pl.BlockSpec((1, hd), lambda i: (0, 0)),
                  pl.BlockSpec((1, hd), lambda i: (0, 0)), pl.BlockSpec((tr, hd), lambda i: (i, 0)),
                  pl.BlockSpec((tr, hd), lambda i: (i, 0)), pl.BlockSpec((tr, n_q * hd), lambda i: (i, 0)),
                  pl.BlockSpec((tr, n_kv * hd), lambda i: (i, 0)), pl.BlockSpec((tr, n_kv * hd), lambda i: (i, 0))],
        out_specs=[pl.BlockSpec((tr, W), lambda i: (i, 0)), pl.BlockSpec((8, hd), lambda i: (0, 0))],
        out_shape=[jax.ShapeDtypeStruct((T, W), MXU), jax.ShapeDtypeStruct((8, hd), F32)],
        compiler_params=_cparams(("arbitrary",), 8 * _nbytes((tr, W), F32)),
    )(qkv, qg, kg, cos, sin, dq, dk, dv)


def _attn_scores(q_ref, k_ref, sink_ref, h, qb, ctx_rows, nb, hd, grp):
    scale = hd ** -0.5
    w0 = jnp.clip(qb - 1, 0, nb - 3) * BLK
    w0 = pl.multiple_of(w0, BLK)
    qv = q_ref[...]
    qs = jnp.concatenate([qv[:, g * hd:(g + 1) * hd] for g in range(grp)], axis=0)
    kc = k_ref[0:ctx_rows, :]
    kb = k_ref[pl.ds(w0, 3 * BLK), :]
    s_c = _dot(qs, kc, NT) * scale
    s_b = _dot(qs, kb, NT) * scale
    n = grp * BLK
    qpos = qb * BLK + lax.broadcasted_iota(jnp.int32, (n, 3 * BLK), 0) % BLK
    kpos = w0 + lax.broadcasted_iota(jnp.int32, (n, 3 * BLK), 1)
    ok = (jnp.abs(kpos - qpos) <= ATTN_WINDOW) & (kpos >= ctx_rows) & (qpos >= ctx_rows)
    s_b = jnp.where(ok, s_b, -jnp.inf)
    gi = lax.broadcasted_iota(jnp.int32, (n, 1), 0) // BLK
    sink = jnp.zeros((n, 1), F32)
    for g in range(grp):
        sink = jnp.where(gi == g, sink_ref[h * grp + g], sink)
    m = jnp.maximum(jnp.maximum(jnp.max(s_c, axis=1, keepdims=True), jnp.max(s_b, axis=1, keepdims=True)), sink)
    e_c, e_b, e_s = jnp.exp(s_c - m), jnp.exp(s_b - m), jnp.exp(sink - m)
    inv = 1.0 / (jnp.sum(e_c, axis=1, keepdims=True) + jnp.sum(e_b, axis=1, keepdims=True) + e_s)
    return qs, kc, kb, w0, e_c * inv, e_b * inv, e_s * inv, gi


def _attn_fwd(qr, kr, vb, sinks, n_q, ctx_rows, name):
    T = qr.shape[0]
    n_kv = ATTN_KV_HEADS
    grp = n_q // n_kv
    hd = qr.shape[1] // n_q
    nb = T // BLK

    def body(sink_ref, q_ref, k_ref, v_ref, o_ref):
        h, qb = pl.program_id(0), pl.program_id(1)
        _, _, _, w0, p_c, p_b, _, _ = _attn_scores(q_ref, k_ref, sink_ref, h, qb, ctx_rows, nb, hd, grp)
        o = _dot(p_c.astype(MXU), v_ref[0:ctx_rows, :]) + _dot(p_b.astype(MXU), v_ref[pl.ds(w0, 3 * BLK), :])
        o_ref[...] = jnp.concatenate([o[g * BLK:(g + 1) * BLK] for g in range(grp)], axis=1).astype(o_ref.dtype)

    return pl.pallas_call(
        body, name=name, grid=(n_kv, nb),
        in_specs=[pl.BlockSpec(memory_space=pltpu.SMEM), pl.BlockSpec((BLK, grp * hd), lambda h, i: (i, h)),
                  pl.BlockSpec((T, hd), lambda h, i: (0, h)), pl.BlockSpec((T, hd), lambda h, i: (0, h))],
        out_specs=pl.BlockSpec((BLK, grp * hd), lambda h, i: (i, h)),
        out_shape=jax.ShapeDtypeStruct((T, n_q * hd), MXU),
        compiler_params=_cparams(("parallel", "arbitrary"), 4 * _nbytes((T, hd), MXU) + 24 * _nbytes((grp * BLK, 5 * BLK), F32)),
    )(sinks, qr, kr, vb)


def _attn_bwd(qr, kr, vb, sinks, do, n_q, ctx_rows, name):
    T = qr.shape[0]
    n_kv = ATTN_KV_HEADS
    grp = n_q // n_kv
    hd = qr.shape[1] // n_q
    nb = T // BLK
    scale = hd ** -0.5

    def body(sink_ref, q_ref, k_ref, v_ref, do_ref, dq_ref, dk_ref, dv_ref, ds_ref):
        h, qb = pl.program_id(0), pl.program_id(1)
        qs, kc, kb, w0, p_c, p_b, p_s, gi = _attn_scores(q_ref, k_ref, sink_ref, h, qb, ctx_rows, nb, hd, grp)
        dov = do_ref[...]
        dos = jnp.concatenate([dov[:, g * hd:(g + 1) * hd] for g in range(grp)], axis=0)
        vc = v_ref[0:ctx_rows, :]
        vw = v_ref[pl.ds(w0, 3 * BLK), :]
        dp_c = _dot(dos, vc, NT)
        dp_b = _dot(dos, vw, NT)
        delta = jnp.sum(p_c * dp_c, axis=1, keepdims=True) + jnp.sum(p_b * dp_b, axis=1, keepdims=True)
        ds_c = (p_c * (dp_c - delta) * scale).astype(MXU)
        ds_b = (p_b * (dp_b - delta) * scale).astype(MXU)
        dq = _dot(ds_c, kc) + _dot(ds_b, kb)
        dq_ref[...] = jnp.concatenate([dq[g * BLK:(g + 1) * BLK] for g in range(grp)], axis=1)

        @pl.when(qb == 0)
        def _():
            dk_ref[...] = jnp.zeros(dk_ref.shape, F32)
            dv_ref[...] = jnp.zeros(dv_ref.shape, F32)

        dk_ref[0:ctx_rows, :] += _dot(ds_c, qs, TN)
        dv_ref[0:ctx_rows, :] += _dot(p_c.astype(MXU), dos, TN)
        dk_ref[pl.ds(w0, 3 * BLK), :] += _dot(ds_b, qs, TN)
        dv_ref[pl.ds(w0, 3 * BLK), :] += _dot(p_b.astype(MXU), dos, TN)
        t = -(p_s * delta)
        lane = lax.broadcasted_iota(jnp.int32, (8, LANE), 1)
        part = jnp.zeros((8, LANE), F32)
        for g in range(grp):
            part = jnp.where(lane == g, jnp.sum(jnp.where(gi == g, t, 0.0)), part)
        _acc_first(ds_ref, part, qb)

    return pl.pallas_call(
        body, name=name, grid=(n_kv, nb),
        in_specs=[pl.BlockSpec(memory_space=pltpu.SMEM), pl.BlockSpec((BLK, grp * hd), lambda h, i: (i, h)),
                  pl.BlockSpec((T, hd), lambda h, i: (0, h)), pl.BlockSpec((T, hd), lambda h, i: (0, h)),
                  pl.BlockSpec((BLK, grp * hd), lambda h, i: (i, h))],
        out_specs=[pl.BlockSpec((BLK, grp * hd), lambda h, i: (i, h)), pl.BlockSpec((T, hd), lambda h, i: (0, h)),
                   pl.BlockSpec((T, hd), lambda h, i: (0, h)), pl.BlockSpec((8, LANE), lambda h, i: (h, 0))],
        out_shape=[jax.ShapeDtypeStruct((T, n_q * hd), F32), jax.ShapeDtypeStruct((T, n_kv * hd), F32),
                   jax.ShapeDtypeStruct((T, n_kv * hd), F32), jax.ShapeDtypeStruct((n_kv * 8, LANE), F32)],
        compiler_params=_cparams(("parallel", "arbitrary"), 4 * _nbytes((T, hd), MXU) + 4 * _nbytes((T, hd), F32)
                                 + 40 * _nbytes((grp * BLK, 5 * BLK), F32)),
    )(sinks, qr, kr, vb, do)


def _chunk_order(s, n_chunks, n_ctx, rev):
    if not rev:
        return s
    return jnp.where(s < n_ctx, n_ctx - 1 - s, n_chunks - 1 + n_ctx - s)


def _softplus(x):
    return jnp.maximum(x, 0.0) + jnp.log(1.0 + jnp.exp(-jnp.abs(x)))


def _expand_matrix(n_heads, p, rev):
    e = np.zeros((LANE, n_heads * p), np.float32)
    for h in range(n_heads):
        e[h + (n_heads if rev else 0), h * p:(h + 1) * p] = 1.0
    return jnp.asarray(e, BF16)


def _ssd_chunk_prep(dt_ref, dtb_ref, alog_ref, e_ref, rev):
    dt = _softplus(dt_ref[...] + dtb_ref[...])
    a = -jnp.exp(alog_ref[...])
    li = lax.broadcasted_iota(jnp.int32, (BLK, BLK), 0)
    si = lax.broadcasted_iota(jnp.int32, (BLK, BLK), 1)
    tri = (si >= li) if rev else (si <= li)
    acs = _dot3_rhs(tri.astype(BF16), a * dt)
    dtexp, aexp = _expand2([dt, acs], e_ref[...])
    return dt, a, tri, acs, dtexp, aexp


def _pair_cols(ap):
    lane = lax.broadcasted_iota(jnp.int32, ap.shape, 1)
    apr = pltpu.roll(ap, LANE // 2, 1)
    return jnp.where(lane < LANE // 2, ap, apr), jnp.where(lane < LANE // 2, apr, ap)


def _ssd_scan(xbc, zx, dtb, alog, emat, n_heads, n_ctx, rev, name):
    T, C = xbc.shape
    P = emat.shape[1] // n_heads
    DI = n_heads * P
    GN = (C - DI) // 2
    N = GN // SSD_GROUPS
    n_pairs = DI // LANE
    ppg = n_pairs // SSD_GROUPS
    n_chunks = T // BLK
    hoff = n_heads if rev else 0
    last = 0 if rev else BLK - 1
    dt_blk = zx.shape[1] // LANE - 1
    assert N == LANE and 2 * P == LANE and 2 * n_heads == LANE

    def body(xs_ref, b_ref, c_ref, dt_ref, dtb_ref, alog_ref, e_ref, y_ref, hin_ref, state_ref, xdt_s, xdec_s, aexp_s, at_s):
        s = pl.program_id(0)

        @pl.when(s == 0)
        def _():
            state_ref[...] = jnp.zeros(state_ref.shape, F32)

        dt, a, tri, acs, dtexp, aexp = _ssd_chunk_prep(dt_ref, dtb_ref, alog_ref, e_ref, rev)
        at_s[...] = acs.T
        aexp_s[...] = aexp
        xdt = xs_ref[...] * dtexp
        xdt_s[...] = xdt.astype(MXU)
        xdec_s[...] = (xdt * jnp.exp(aexp[last:last + 1, :] - aexp)).astype(MXU)
        hin_ref[0] = state_ref[...]
        lane = lax.broadcasted_iota(jnp.int32, (BLK, LANE), 1)

        def pair(k, carry):
            col = pl.multiple_of(k * LANE, LANE)
            gcol = pl.multiple_of((k // ppg) * N, N)
            bg = b_ref[:, pl.ds(gcol, N)].astype(MXU)
            cg = c_ref[:, pl.ds(gcol, N)].astype(MXU)
            cb = _dot(cg, bg, NT)
            ap = aexp_s[:, pl.ds(col, LANE)]
            ac0, ac1 = _pair_cols(ap)
            ar0 = at_s[pl.ds(2 * k + hoff, 1), :]
            ar1 = at_s[pl.ds(2 * k + 1 + hoff, 1), :]
            m0 = (cb * jnp.exp(jnp.where(tri, ac0 - ar0, -jnp.inf))).astype(MXU)
            m1 = (cb * jnp.exp(jnp.where(tri, ac1 - ar1, -jnp.inf))).astype(MXU)
            xp = xdt_s[:, pl.ds(col, LANE)]
            zero = jnp.zeros_like(xp)
            xbd = jnp.concatenate([jnp.where(lane < LANE // 2, xp, zero), jnp.where(lane >= LANE // 2, xp, zero)], axis=0)
            yd = _dot(jnp.concatenate([m0, m1], axis=1), xbd)
            ht = state_ref[k]
            yo = _dot(cg, ht.astype(MXU)) * jnp.exp(ap)
            y_ref[:, pl.ds(col, LANE)] = yd + yo
            st = _dot(bg, xdec_s[:, pl.ds(col, LANE)], TN)
            state_ref[k] = jnp.exp(aexp_s[pl.ds(last, 1), pl.ds(col, LANE)]) * ht + st
            return carry

        lax.fori_loop(0, n_pairs, pair, 0, unroll=4)

    order = lambda s: _chunk_order(s, n_chunks, n_ctx, rev)
    return pl.pallas_call(
        body, name=name, grid=(n_chunks,),
        in_specs=[pl.BlockSpec((BLK, DI), lambda s: (order(s), 0)),
                  pl.BlockSpec((BLK, GN), lambda s: (order(s), DI // GN)),
                  pl.BlockSpec((BLK, GN), lambda s: (order(s), DI // GN + 1)),
                  pl.BlockSpec((BLK, LANE), lambda s: (order(s), dt_blk)),
                  pl.BlockSpec((1, LANE), lambda s: (0, 0)), pl.BlockSpec((1, LANE), lambda s: (0, 0)),
                  pl.BlockSpec((2 * LANE, DI), lambda s: (0, 0))],
        out_specs=[pl.BlockSpec((BLK, DI), lambda s: (order(s), 0)),
                   pl.BlockSpec((1, n_pairs, N, LANE), lambda s: (order(s), 0, 0, 0))],
        out_shape=[jax.ShapeDtypeStruct((T, DI), F32), jax.ShapeDtypeStruct((n_chunks, n_pairs, N, LANE), F32)],
        scratch_shapes=[pltpu.VMEM((n_pairs, N, LANE), F32), pltpu.VMEM((BLK, DI), MXU), pltpu.VMEM((BLK, DI), MXU),
                        pltpu.VMEM((BLK, DI), F32), pltpu.VMEM((LANE, BLK), F32)],
        compiler_params=_cparams(("arbitrary",), 20 * _nbytes((BLK, DI), F32)),
    )(xbc, xbc, xbc, zx, dtb, alog, jnp.concatenate([emat, emat], axis=0))


def _ssd_scan_bwd(xbc, zx, dtb, alog, emat, emat_t, dexp, hin, dy, acc, n_heads, n_ctx, rev, name):
    T, C = xbc.shape
    P = emat.shape[1] // n_heads
    DI = n_heads * P
    GN = (C - DI) // 2
    N = GN // SSD_GROUPS
    n_pairs = DI // LANE
    ppg = n_pairs // SSD_GROUPS
    n_chunks = T // BLK
    hoff = n_heads if rev else 0
    last = 0 if rev else BLK - 1
    dt_blk = zx.shape[1] // LANE - 1
    has_acc = acc is not None

    def body(*refs):
        (xs_ref, b_ref, c_ref, dt_ref, dtb_ref, alog_ref, e_ref, et_ref, dexp_ref, hin_ref, dy_ref) = refs[:11]
        n_in = 11
        if has_acc:
            dxbc_in, ddt_in, sums_in = refs[11:14]
            n_in = 14
        dxbc_ref, ddt_ref, sums_ref = refs[n_in:n_in + 3]
        dstate_ref, xdt_s, xdec_s, aexp_s, at_s, dyd_s, z1_s, z3_s, dxdt_s, cdrow_s, rmat_s, cst_s = refs[n_in + 3:]
        s = pl.program_id(0)

        @pl.when(s == 0)
        def _():
            dstate_ref[...] = jnp.zeros(dstate_ref.shape, F32)

        dt, a, tri, acs, dtexp, aexp = _ssd_chunk_prep(dt_ref, dtb_ref, alog_ref, e_ref, rev)
        at_s[...] = acs.T
        aexp_s[...] = aexp
        xsv = xs_ref[...]
        xdt = xsv * dtexp
        xdt_s[...] = xdt.astype(MXU)
        decend = jnp.exp(aexp[last:last + 1, :] - aexp)
        xdec_s[...] = (xdt * decend).astype(MXU)
        dyv = dy_ref[...]
        dyd_s[...] = (dyv * jnp.exp(aexp)).astype(MXU)
        dxbc_ref[:, DI:] = jnp.zeros((BLK, 2 * GN), F32)
        rmat_s[...] = jnp.zeros(rmat_s.shape, F32)
        lane = lax.broadcasted_iota(jnp.int32, (BLK, LANE), 1)
        lo = lane < LANE // 2
        tri_t = (lax.broadcasted_iota(jnp.int32, (BLK, BLK), 0) <= lax.broadcasted_iota(jnp.int32, (BLK, BLK), 1)) if not rev \
            else (lax.broadcasted_iota(jnp.int32, (BLK, BLK), 0) >= lax.broadcasted_iota(jnp.int32, (BLK, BLK), 1))

        def pair(k, carry):
            col = pl.multiple_of(k * LANE, LANE)
            gcol = pl.multiple_of((k // ppg) * N, N)
            bg = b_ref[:, pl.ds(gcol, N)].astype(MXU)
            cg = c_ref[:, pl.ds(gcol, N)].astype(MXU)
            cb = _dot(cg, bg, NT)
            cbt = _dot(bg, cg, NT)
            ap = aexp_s[:, pl.ds(col, LANE)]
            ac0, ac1 = _pair_cols(ap)
            ar0 = at_s[pl.ds(2 * k + hoff, 1), :]
            ar1 = at_s[pl.ds(2 * k + 1 + hoff, 1), :]
            seg0 = jnp.exp(jnp.where(tri, ac0 - ar0, -jnp.inf))
            seg1 = jnp.exp(jnp.where(tri, ac1 - ar1, -jnp.inf))
            segt0 = jnp.exp(jnp.where(tri_t, ar0 - ac0, -jnp.inf))
            segt1 = jnp.exp(jnp.where(tri_t, ar1 - ac1, -jnp.inf))
            dyp = dy_ref[:, pl.ds(col, LANE)].astype(MXU)
            zero = jnp.zeros_like(dyp)
            dy0, dy1 = jnp.where(lo, dyp, zero), jnp.where(lo, zero, dyp)
            dht = dstate_ref[k]
            dhb = dht.astype(MXU)
            ht = hin_ref[0, k]
            mt = jnp.concatenate([(cbt * segt0).astype(MXU), (cbt * segt1).astype(MXU)], axis=1)
            bdh = _dot(bg, dhb)
            dec_p = jnp.exp(aexp_s[pl.ds(last, 1), pl.ds(col, LANE)] - ap)
            dxdt_s[:, pl.ds(col, LANE)] = _dot(mt, jnp.concatenate([dy0, dy1], axis=0)) + dec_p * bdh
            z3_s[:, pl.ds(col, LANE)] = bdh
            cdec = jnp.exp(aexp_s[pl.ds(last, 1), pl.ds(col, LANE)])
            dydp = dyd_s[:, pl.ds(col, LANE)]
            dstate_ref[k] = _dot(cg, dydp, TN) + cdec * dht
            cdrow_s[0:1, pl.ds(col, LANE)] = cdec * jnp.sum(dht * ht, axis=0, keepdims=True)
            z1_s[:, pl.ds(col, LANE)] = _dot(cg, ht.astype(MXU))
            dcg = _dot(dydp, ht.astype(MXU), NT)
            dbg = _dot(xdec_s[:, pl.ds(col, LANE)], dhb, NT)
            xp = xdt_s[:, pl.ds(col, LANE)]
            dg0 = _dot(dy0, xp, NT)
            dg1 = _dot(dy1, xp, NT)
            ds0, ds1 = dg0 * seg0, dg1 * seg1
            w0, w1 = ds0 * cb, ds1 * cb
            dcb = ds0 + ds1
            lane_h = lax.broadcasted_iota(jnp.int32, (BLK, LANE), 1)
            rmat_s[...] += (jnp.where(lane_h == 2 * k + hoff, jnp.sum(w0, axis=1, keepdims=True), 0.0)
                            + jnp.where(lane_h == 2 * k + 1 + hoff, jnp.sum(w1, axis=1, keepdims=True), 0.0))
            cst_s[pl.ds(2 * k + hoff, 1), :] = jnp.sum(w0, axis=0, keepdims=True)
            cst_s[pl.ds(2 * k + 1 + hoff, 1), :] = jnp.sum(w1, axis=0, keepdims=True)
            dcbb = dcb.astype(MXU)
            dxbc_ref[:, pl.ds(DI + GN + gcol, N)] += dcg + _dot(dcbb, bg)
            dxbc_ref[:, pl.ds(DI + gcol, N)] += dbg + _dot(dcbb, cg, TN)
            return carry

        cst_s[...] = jnp.zeros(cst_s.shape, F32)
        lax.fori_loop(0, n_pairs, pair, 0, unroll=4)

        etv = et_ref[...]
        dxdt = dxdt_s[...]
        qfull = xdt * decend * z3_s[...]
        tot8 = jnp.concatenate([jnp.sum(qfull, axis=0, keepdims=True) + cdrow_s[0:1, :], jnp.zeros((7, DI), F32)], axis=0)
        z1q, z2, tot = _dot2_stacked([dyv * (z1_s[...] * jnp.exp(aexp)) - qfull, dxdt * xsv, tot8], etv)
        dacs = rmat_s[...] - cst_s[...].T + z1q
        rowi = lax.broadcasted_iota(jnp.int32, (BLK, LANE), 0)
        dacs = dacs + jnp.where(rowi == last, tot[0:1, :], 0.0)
        d_a = _dot3_rhs(tri_t.astype(BF16), dacs)
        ddt = a * d_a + z2
        x_raw = dt_ref[...] + dtb_ref[...]
        ddt_raw = ddt * jax.nn.sigmoid(x_raw)
        lane_l = lax.broadcasted_iota(jnp.int32, (BLK, LANE), 1)
        mine = (lane_l >= hoff) & (lane_l < hoff + n_heads)
        ddt_raw = jnp.where(mine, ddt_raw, 0.0)
        part = jnp.concatenate([jnp.sum(jnp.where(mine, dt * d_a, 0.0), axis=0, keepdims=True) * a,
                                jnp.sum(ddt_raw, axis=0, keepdims=True), jnp.zeros((6, LANE), F32)], axis=0)
        dxs = dxdt * dtexp
        if has_acc:
            dxbc_ref[:, 0:DI] = dxs + dxbc_in[:, 0:DI]
            dxbc_ref[:, DI:] += dxbc_in[:, DI:]
            ddt_ref[...] = ddt_raw + ddt_in[...]
            part = part + jnp.where(s == 0, sums_in[...], 0.0)
        else:
            dxbc_ref[:, 0:DI] = dxs + dyv * dexp_ref[...]
            ddt_ref[...] = ddt_raw
        _acc_first(sums_ref, part, s)

    order = lambda s: _chunk_order(n_chunks - 1 - s, n_chunks, n_ctx, rev)
    in_specs = [pl.BlockSpec((BLK, DI), lambda s: (order(s), 0)),
                pl.BlockSpec((BLK, GN), lambda s: (order(s), DI // GN)),
                pl.BlockSpec((BLK, GN), lambda s: (order(s), DI // GN + 1)),
                pl.BlockSpec((BLK, LANE), lambda s: (order(s), dt_blk)),
                pl.BlockSpec((1, LANE), lambda s: (0, 0)), pl.BlockSpec((1, LANE), lambda s: (0, 0)),
                pl.BlockSpec((2 * LANE, DI), lambda s: (0, 0)), pl.BlockSpec((DI, LANE), lambda s: (0, 0)),
                pl.BlockSpec((1, DI), lambda s: (0, 0)),
                pl.BlockSpec((1, n_pairs, N, LANE), lambda s: (order(s), 0, 0, 0)),
                pl.BlockSpec((BLK, DI), lambda s: (order(s), 0))]
    args = [xbc, xbc, xbc, zx, dtb, alog, jnp.concatenate([emat, emat], axis=0), emat_t, dexp, hin, dy]
    if has_acc:
        in_specs += [pl.BlockSpec((BLK, C), lambda s: (order(s), 0)), pl.BlockSpec((BLK, LANE), lambda s: (order(s), 0)),
                     pl.BlockSpec((8, LANE), lambda s: (0, 0))]
        args += list(acc)
    return pl.pallas_call(
        body, name=name, grid=(n_chunks,),
        in_specs=in_specs,
        out_specs=[pl.BlockSpec((BLK, C), lambda s: (order(s), 0)), pl.BlockSpec((BLK, LANE), lambda s: (order(s), 0)),
                   pl.BlockSpec((8, LANE), lambda s: (0, 0))],
        out_shape=[jax.ShapeDtypeStruct((T, C), F32), jax.ShapeDtypeStruct((T, LANE), F32), jax.ShapeDtypeStruct((8, LANE), F32)],
        scratch_shapes=[pltpu.VMEM((n_pairs, N, LANE), F32), pltpu.VMEM((BLK, DI), MXU), pltpu.VMEM((BLK, DI), MXU),
                        pltpu.VMEM((BLK, DI), F32), pltpu.VMEM((LANE, BLK), F32), pltpu.VMEM((BLK, DI), MXU),
                        pltpu.VMEM((BLK, DI), F32), pltpu.VMEM((BLK, DI), F32), pltpu.VMEM((BLK, DI), F32),
                        pltpu.VMEM((8, DI), F32), pltpu.VMEM((BLK, LANE), F32), pltpu.VMEM((LANE, BLK), F32)],
        compiler_params=_cparams(("arbitrary",), 36 * _nbytes((BLK, DI), F32)),
    )(*args)


def _ssd_finish(yf, yb, xbc, zx, dexp, nw, name, tr=256):
    T, DI = yf.shape
    tr = min(tr, T)

    def body(yf_ref, yb_ref, xs_ref, z_ref, d_ref, nw_ref, o_ref):
        y = yf_ref[...] + yb_ref[...] + xs_ref[...] * d_ref[...]
        gt = y * _silu(z_ref[...])
        r = lax.rsqrt(jnp.mean(gt * gt, axis=-1, keepdims=True) + NORM_EPS)
        o_ref[...] = ((gt * r) * nw_ref[...]).astype(o_ref.dtype)

    row = pl.BlockSpec((tr, DI), lambda i: (i, 0))
    vec = pl.BlockSpec((1, DI), lambda i: (0, 0))
    return pl.pallas_call(
        body, name=name, grid=(T // tr,), in_specs=[row, row, row, row, vec, vec], out_specs=row,
        out_shape=jax.ShapeDtypeStruct((T, DI), MXU),
        compiler_params=_cparams(("parallel",), 16 * _nbytes((tr, DI), F32)),
    )(yf, yb, xbc, zx, dexp, nw)


def _ssd_finish_bwd(yf, yb, xbc, zx, dexp, nw, do, emat_t, dzx_shape, name, tr=64):
    T, DI = yf.shape
    tr = min(tr, T)
    n_steps = T // tr

    def body(yf_ref, yb_ref, xs_ref, z_ref, d_ref, nw_ref, do_ref, et_ref, dy_ref, dz_ref, sums_ref, dd_ref):
        i = pl.program_id(0)
        xs = xs_ref[...]
        zv = z_ref[...]
        y = yf_ref[...] + yb_ref[...] + xs * d_ref[...]
        sz = _silu(zv)
        gt = y * sz
        r = lax.rsqrt(jnp.mean(gt * gt, axis=-1, keepdims=True) + NORM_EPS)
        gn = gt * r
        dov = do_ref[...].astype(F32)
        dgn = dov * nw_ref[...]
        dgt = r * (dgn - gn * jnp.mean(dgn * gn, axis=-1, keepdims=True))
        dy = dgt * sz
        dy_ref[...] = dy
        dz_ref[...] = (dgt * y * _dsilu(zv)).astype(dz_ref.dtype)
        part = jnp.concatenate([jnp.sum(dov * gn, axis=0, keepdims=True), jnp.sum(dy * xs, axis=0, keepdims=True),
                                jnp.zeros((6, DI), F32)], axis=0)
        _acc_first(sums_ref, part, i)

        @pl.when(i == n_steps - 1)
        def _():
            dd_ref[...] = _dot3_lhs(sums_ref[...], et_ref[...])

    row = pl.BlockSpec((tr, DI), lambda i: (i, 0))
    vec = pl.BlockSpec((1, DI), lambda i: (0, 0))
    return pl.pallas_call(
        body, name=name, grid=(n_steps,),
        in_specs=[row, row, row, row, vec, vec, row, pl.BlockSpec((DI, LANE), lambda i: (0, 0))],
        out_specs=[row, row, pl.BlockSpec((8, DI), lambda i: (0, 0)), pl.BlockSpec((8, LANE), lambda i: (0, 0))],
        out_shape=[jax.ShapeDtypeStruct((T, DI), F32), jax.ShapeDtypeStruct(dzx_shape, MXU), jax.ShapeDtypeStruct((8, DI), F32),
                   jax.ShapeDtypeStruct((8, LANE), F32)],
        compiler_params=_cparams(("arbitrary",), 40 * _nbytes((tr, DI), F32)),
    )(yf, yb, xbc, zx, dexp, nw, do, emat_t)


def _loss_head(xf, tgt, ctx_rows, name, tr=256):
    T, D = xf.shape
    tr = min(tr, ctx_rows)
    n_ctx = ctx_rows // tr

    def body(x_ref, t_ref, dx_ref, l_ref):
        i = pl.program_id(0)

        @pl.when(i < n_ctx)
        def _():
            dx_ref[...] = jnp.zeros(dx_ref.shape, F32)

        @pl.when(i == 0)
        def _():
            l_ref[...] = jnp.zeros(l_ref.shape, F32)

        @pl.when(i >= n_ctx)
        def _():
            e = x_ref[...] - t_ref[...]
            dx_ref[...] = e * (1.0 / D)
            l_ref[...] += 0.5 * jnp.sum(jnp.mean(e * e, axis=-1, keepdims=True))

    return pl.pallas_call(
        body, name=name, grid=(T // tr,),
        in_specs=[pl.BlockSpec((tr, D), lambda i: (i, 0)), pl.BlockSpec((tr, D), lambda i: (jnp.maximum(i - n_ctx, 0), 0))],
        out_specs=[pl.BlockSpec((tr, D), lambda i: (i, 0)), pl.BlockSpec((8, LANE), lambda i: (0, 0))],
        out_shape=[jax.ShapeDtypeStruct((T, D), F32), jax.ShapeDtypeStruct((8, LANE), F32)],
        compiler_params=_cparams(("arbitrary",), 10 * _nbytes((tr, D), F32)),
    )(xf, tgt)


def _adamw_math(w, g, m, v):
    m2 = ADAM_B1 * m + (1.0 - ADAM_B1) * g
    v2 = ADAM_B2 * v + (1.0 - ADAM_B2) * (g * g)
    m_hat = m2 / (1.0 - ADAM_B1 ** ADAM_STEP)
    v_hat = v2 / (1.0 - ADAM_B2 ** ADAM_STEP)
    delta = -ADAM_LR * (m_hat / (jnp.sqrt(v_hat) + ADAM_EPS) + ADAM_WD * w)
    return delta, m2, v2


def _row_tile(rows, target):
    if rows <= target:
        return rows
    t = target - target % 8
    while rows % t:
        t -= 8
    return t


def _adamw(parts, w, m, v, name, tr=128):
    n, L, rows, cols = parts.shape
    tr = _row_tile(rows, tr)

    def body(p_ref, w_ref, m_ref, v_ref, g_ref, d_ref, m2_ref, v2_ref):
        g = p_ref[0, 0].astype(F32)
        for q in range(1, n):
            g = g + p_ref[q, 0].astype(F32)
        d, m2, v2 = _adamw_math(w_ref[0], g, m_ref[0], v_ref[0])
        g_ref[0], d_ref[0], m2_ref[0], v2_ref[0] = g, d, m2, v2

    blk = pl.BlockSpec((1, tr, cols), lambda l, i: (l, i, 0))
    shp = jax.ShapeDtypeStruct((L, rows, cols), F32)
    return pl.pallas_call(
        body, name=name, grid=(L, rows // tr),
        in_specs=[pl.BlockSpec((n, 1, tr, cols), lambda l, i: (0, l, i, 0)), blk, blk, blk],
        out_specs=[blk, blk, blk, blk], out_shape=[shp, shp, shp, shp],
        compiler_params=_cparams(("parallel", "parallel"), 2 * (n + 8) * _nbytes((tr, cols), F32)),
    )(parts, w, m, v)


def _adamw_small(bufs, w, m, v, name):
    n, R, _ = bufs.shape

    def body(b_ref, w_ref, m_ref, v_ref, g_ref, d_ref, m2_ref, v2_ref):
        g = b_ref[0]
        for q in range(1, n):
            g = g + b_ref[q]
        d, m2, v2 = _adamw_math(w_ref[...], g, m_ref[...], v_ref[...])
        g_ref[...], d_ref[...], m2_ref[...], v2_ref[...] = g, d, m2, v2

    vm = pl.BlockSpec(memory_space=pltpu.VMEM)
    shp = jax.ShapeDtypeStruct((R, LANE), F32)
    return pl.pallas_call(body, name=name, in_specs=[vm, vm, vm, vm], out_specs=[vm, vm, vm, vm],
                          out_shape=[shp, shp, shp, shp],
                          compiler_params=pltpu.CompilerParams(vmem_limit_bytes=32 * 1024 * 1024))(bufs, w, m, v)


def _add_blocks(g, r1, c_idx, name, tr=128):
    _, rows, cols = g.shape
    tr = _row_tile(rows, tr)

    def body(c_ref, g_ref, r_ref, p_ref):
        del c_ref
        p_ref[...] = (g_ref[...] + r_ref[...]).astype(p_ref.dtype)

    return pl.pallas_call(
        body, name=name,
        grid_spec=pltpu.PrefetchScalarGridSpec(
            num_scalar_prefetch=1, grid=(NDEV // 2, rows // tr),
            in_specs=[pl.BlockSpec((1, tr, cols), lambda q, i, c: (2 * q + c[0], i, 0)),
                      pl.BlockSpec((1, tr, cols), lambda q, i, c: (q, i, 0))],
            out_specs=pl.BlockSpec((1, tr, cols), lambda q, i, c: (q, i, 0))),
        out_shape=jax.ShapeDtypeStruct((NDEV // 2, rows, cols), XFER),
        compiler_params=_cparams(("parallel", "parallel"), 8 * _nbytes((tr, cols), F32)),
    )(c_idx, g, r1)


def _me():
    return lax.axis_index("x"), lax.axis_index("y"), lax.axis_index("c")


def _flip(v, bit):
    return 1 - v if bit else v


def _peer(k):
    x, y, c = _me()
    return _flip(x, (k >> 2) & 1), _flip(y, (k >> 1) & 1), _flip(c, k & 1)


def _dev_index(p):
    return 4 * p[0] + 2 * p[1] + p[2]


def _chip_index(p):
    return 2 * p[0] + p[1]


def _small_allgather(v, name):
    R, C = v.shape

    def body(v_ref, out_ref, send_sems, recv_sems, loc_sem):
        me = _dev_index(_me())
        mine = pltpu.make_async_copy(v_ref, out_ref.at[me], loc_sem)
        mine.start()
        sends = []
        for k in range(1, NDEV):
            cp = pltpu.make_async_remote_copy(src_ref=v_ref, dst_ref=out_ref.at[me], send_sem=send_sems.at[k - 1],
                                              recv_sem=recv_sems.at[k - 1], device_id=_peer(k), device_id_type=MESH_ID)
            cp.start()
            sends.append(cp)
        for k in range(1, NDEV):
            pltpu.make_async_remote_copy(src_ref=v_ref, dst_ref=out_ref.at[_dev_index(_peer(k))], send_sem=send_sems.at[k - 1],
                                         recv_sem=recv_sems.at[k - 1], device_id=_peer(k), device_id_type=MESH_ID).wait_recv()
        for cp in sends:
            cp.wait_send()
        mine.wait()

    vm = pl.BlockSpec(memory_space=pltpu.VMEM)
    return pl.pallas_call(
        body, name=name, in_specs=[vm], out_specs=vm, out_shape=jax.ShapeDtypeStruct((NDEV, R, C), F32),
        scratch_shapes=[pltpu.SemaphoreType.DMA((NDEV - 1,)), pltpu.SemaphoreType.DMA((NDEV - 1,)), pltpu.SemaphoreType.DMA(())],
        compiler_params=pltpu.CompilerParams(vmem_limit_bytes=48 * 1024 * 1024),
    )(v)


def _allgather_weights(arrs, name):
    n = len(arrs)

    def body(*refs):
        x_refs, out_refs = refs[:n], refs[n:2 * n]
        send_sems, recv_sems, loc_sems = refs[2 * n:]
        x, y, c = _me()
        me, sib = (x, y, c), (x, y, 1 - c)
        chips = [(1 - x, y), (x, 1 - y), (1 - x, 1 - y)]

        def copy(a, k, block, to, src=None):
            dst = out_refs[a].at[_dev_index(block)]
            return pltpu.make_async_remote_copy(src_ref=dst if src is None else src, dst_ref=dst, send_sem=send_sems.at[a, k],
                                                recv_sem=recv_sems.at[a, k], device_id=to, device_id_type=MESH_ID)

        locs, first, passed = [], [], []
        for a in range(n):
            lc = pltpu.make_async_copy(x_refs[a], out_refs[a].at[_dev_index(me)], loc_sems.at[a])
            lc.start()
            locs.append(lc)
            cps = [copy(a, 0, me, sib, src=x_refs[a])] + [copy(a, 1 + j, me, (*chip, c), src=x_refs[a]) for j, chip in enumerate(chips)]
            for cp in cps:
                cp.start()
            first += cps
        for j, chip in enumerate(chips):
            for a in range(n):
                copy(a, 1 + j, (*chip, c), me).wait_recv()
                cp = copy(a, 4 + j, (*chip, c), sib)
                cp.start()
                passed.append(cp)
        for a in range(n):
            copy(a, 0, sib, me).wait_recv()
            for j, chip in enumerate(chips):
                copy(a, 4 + j, (*chip, 1 - c), me).wait_recv()
        for cp in first + passed:
            cp.wait_send()
        for lc in locs:
            lc.wait()

    hbm = pl.BlockSpec(memory_space=pl.ANY)
    return pl.pallas_call(
        body, name=name, in_specs=[hbm] * n, out_specs=[hbm] * n,
        out_shape=[jax.ShapeDtypeStruct((NDEV,) + a.shape, a.dtype) for a in arrs],
        scratch_shapes=[pltpu.SemaphoreType.DMA((n, 7)), pltpu.SemaphoreType.DMA((n, 7)), pltpu.SemaphoreType.DMA((n,))],
    )(*arrs)


def _rs_sibling(gs, name):
    n = len(gs)

    def body(*refs):
        g_refs, r_refs = refs[:n], refs[n:2 * n]
        send_sems, recv_sems = refs[2 * n:]
        x, y, c = _me()
        sib = (x, y, 1 - c)
        sends = []
        for a in range(n):
            for q in range(NDEV // 2):
                cp = pltpu.make_async_remote_copy(src_ref=g_refs[a].at[2 * q + (1 - c)], dst_ref=r_refs[a].at[q],
                                                  send_sem=send_sems.at[a, q], recv_sem=recv_sems.at[a, q],
                                                  device_id=sib, device_id_type=MESH_ID)
                cp.start()
                sends.append(cp)
        for cp in sends:
            cp.wait_recv()
        for cp in sends:
            cp.wait_send()

    hbm = pl.BlockSpec(memory_space=pl.ANY)
    return pl.pallas_call(
        body, name=name, in_specs=[hbm] * n, out_specs=[hbm] * n,
        out_shape=[jax.ShapeDtypeStruct((NDEV // 2,) + g.shape[1:], g.dtype) for g in gs],
        scratch_shapes=[pltpu.SemaphoreType.DMA((n, NDEV // 2)), pltpu.SemaphoreType.DMA((n, NDEV // 2))],
    )(*gs)


def _rs_chips(ps, groups, name):
    n = len(ps)
    where = {}
    for gi, grp in enumerate(groups):
        for li, a in enumerate(grp):
            where[a] = (gi, li)
    ng = len(groups)

    def body(*refs):
        p_refs, r_refs = refs[:n], refs[n:n + ng]
        send_sems, recv_sems, loc_sems = refs[n + ng:]
        x, y, c = _me()
        mychip = _chip_index((x, y))
        chips = [(1 - x, y), (x, 1 - y), (1 - x, 1 - y)]
        sends, locs = [], []
        for a in range(n):
            gi, li = where[a]
            lc = pltpu.make_async_copy(p_refs[a].at[mychip], r_refs[gi].at[mychip, li], loc_sems.at[a])
            lc.start()
            locs.append(lc)
            for j, chip in enumerate(chips):
                cp = pltpu.make_async_remote_copy(src_ref=p_refs[a].at[_chip_index(chip)], dst_ref=r_refs[gi].at[mychip, li],
                                                  send_sem=send_sems.at[a, j], recv_sem=recv_sems.at[a, j],
                                                  device_id=(*chip, c), device_id_type=MESH_ID)
                cp.start()
                sends.append(cp)
        for a in range(n):
            gi, li = where[a]
            for j, chip in enumerate(chips):
                pltpu.make_async_remote_copy(src_ref=p_refs[a].at[mychip], dst_ref=r_refs[gi].at[_chip_index(chip), li],
                                             send_sem=send_sems.at[a, j], recv_sem=recv_sems.at[a, j],
                                             device_id=(*chip, c), device_id_type=MESH_ID).wait_recv()
        for cp in sends:
            cp.wait_send()
        for lc in locs:
            lc.wait()

    hbm = pl.BlockSpec(memory_space=pl.ANY)
    return pl.pallas_call(
        body, name=name, in_specs=[hbm] * n, out_specs=[hbm] * ng,
        out_shape=[jax.ShapeDtypeStruct((NDEV // 2, len(grp)) + ps[grp[0]].shape[1:], ps[grp[0]].dtype) for grp in groups],
        scratch_shapes=[pltpu.SemaphoreType.DMA((n, 3)), pltpu.SemaphoreType.DMA((n, 3)), pltpu.SemaphoreType.DMA((n,))],
    )(*ps)


HBM_SPEC = pl.BlockSpec(memory_space=pltpu.HBM)
SEM_SPEC = pl.BlockSpec(memory_space=pltpu.SEMAPHORE)
DATAFLOW = pltpu.SideEffectType.DATAFLOW_SIDE_EFFECTING


def _xfer_copy(src_ref, land_ref, sems, a, k, layer, scatter, arriving):
    send_sems, recv_sems = sems
    me, peer = _dev_index(_me()), _dev_index(_peer(k))
    src = src_ref.at[peer] if scatter else src_ref
    slot = peer if arriving else me
    dst = land_ref.at[slot] if layer is None else land_ref.at[slot, layer]
    return pltpu.make_async_remote_copy(src_ref=src, dst_ref=dst, send_sem=send_sems.at[a * (NDEV - 1) + k - 1],
                                        recv_sem=recv_sems.at[a * (NDEV - 1) + k - 1], device_id=_peer(k),
                                        device_id_type=MESH_ID)


def _xfer_start(srcs, lands, layers, scatter, name):
    n = len(srcs)

    def body(*refs):
        src_refs, land_refs = refs[:n], refs[n:2 * n]
        sems = refs[2 * n], refs[2 * n + 1]
        token = refs[-1]
        for a in range(n):
            for k in range(1, NDEV):
                _xfer_copy(src_refs[a], land_refs[a], sems, a, k, layers[a], scatter, False).start()
        token[...] = jnp.zeros(token.shape, token.dtype)

    ops = [pltpu.with_memory_space_constraint(t, pltpu.HBM) for t in list(srcs) + list(lands)]
    n_sem = n * (NDEV - 1)
    res = pl.pallas_call(
        body, name=name,
        out_shape=(pltpu.SemaphoreType.DMA((n_sem,)), pltpu.SemaphoreType.DMA((n_sem,)),
                   *[pltpu.HBM(t.shape, t.dtype) for t in ops], jax.ShapeDtypeStruct((8, LANE), F32)),
        in_specs=[HBM_SPEC] * (2 * n),
        out_specs=(SEM_SPEC, SEM_SPEC, *[HBM_SPEC] * (2 * n), pl.BlockSpec(memory_space=pltpu.VMEM)),
        input_output_aliases={i: 2 + i for i in range(2 * n)},
        compiler_params=pltpu.CompilerParams(has_side_effects=DATAFLOW),
    )(*ops)
    return res[0], res[1], list(res[2:2 + n]), list(res[2 + n:2 + 2 * n]), res[-1]


def _xfer_wait(send_sems, recv_sems, srcs, lands, layers, scatter, after, name):
    n = len(srcs)

    def body(*refs):
        src_refs, land_refs = refs[:n], refs[n:2 * n]
        sems = refs[2 * n], refs[2 * n + 1]
        for a in range(n):
            for k in range(1, NDEV):
                _xfer_copy(src_refs[a], land_refs[a], sems, a, k, layers[a], scatter, False).wait_send()
                _xfer_copy(src_refs[a], land_refs[a], sems, a, k, layers[a], scatter, True).wait_recv()

    ops = list(srcs) + list(lands)
    res = pl.pallas_call(
        body, name=name,
        out_shape=tuple(pltpu.HBM(t.shape, t.dtype) for t in ops),
        in_specs=[HBM_SPEC] * (2 * n) + [SEM_SPEC, SEM_SPEC, pl.BlockSpec(memory_space=pl.ANY)],
        out_specs=tuple([HBM_SPEC] * (2 * n)),
        input_output_aliases={i: i for i in range(2 * n)},
        compiler_params=pltpu.CompilerParams(has_side_effects=DATAFLOW),
    )(*ops, send_sems, recv_sems, after)
    return list(res[n:])


HI = lax.Precision.HIGHEST
MOD_ROWS = 16


def _col_tile(n, target=512):
    return target if n % target == 0 else n


def _modulation(s_in, ada_w, b_loc, name):
    L, D, nl = ada_w.shape
    tn = _col_tile(nl)

    def body(s_ref, w_ref, b_ref, o_ref):
        o_ref[0] = jnp.dot(_silu(s_ref[...]), w_ref[0], preferred_element_type=F32, precision=HI) + b_ref[0]

    return pl.pallas_call(
        body, name=name, grid=(L, nl // tn),
        in_specs=[pl.BlockSpec((MOD_ROWS, D), lambda l, j: (0, 0)), pl.BlockSpec((1, D, tn), lambda l, j: (l, 0, j)),
                  pl.BlockSpec((1, 1, tn), lambda l, j: (l, 0, j))],
        out_specs=pl.BlockSpec((1, MOD_ROWS, tn), lambda l, j: (l, 0, j)),
        out_shape=jax.ShapeDtypeStruct((L, MOD_ROWS, nl), F32),
        compiler_params=_cparams(("parallel", "parallel"), 4 * _nbytes((D, tn), F32)),
    )(s_in, ada_w, b_loc)


def _modulation_bwd(s_in, dml, dmc, ada_w, name):
    L, D, nl = ada_w.shape
    tn = _col_tile(nl)

    def body(s_ref, dml_ref, dmc_ref, w_ref, g_ref, pc_ref):
        l, j = pl.program_id(0), pl.program_id(1)
        a = _silu(s_ref[...])
        tot = dmc_ref[0, 0]
        for d in range(1, NDEV):
            tot = tot + dmc_ref[d, 0]
        row = lax.broadcasted_iota(jnp.int32, (MOD_ROWS, tn), 0)
        dm = jnp.where(row == NDEV, tot, dml_ref[:, 0, 0, :])
        g_ref[0] = lax.dot_general(a, dm, TN, preferred_element_type=F32, precision=HI)
        tot8 = jnp.where(lax.broadcasted_iota(jnp.int32, (8, tn), 0) == 0, tot, 0.0)
        part = lax.dot_general(tot8, w_ref[0], NT, preferred_element_type=F32, precision=HI)

        @pl.when((l == 0) & (j == 0))
        def _():
            pc_ref[...] = part

        @pl.when((l != 0) | (j != 0))
        def _():
            pc_ref[...] += part

    return pl.pallas_call(
        body, name=name, grid=(L, nl // tn),
        in_specs=[pl.BlockSpec((MOD_ROWS, D), lambda l, j: (0, 0)), pl.BlockSpec((MOD_ROWS, 1, 1, tn), lambda l, j: (0, l, 0, j)),
                  pl.BlockSpec((NDEV, 1, 1, tn), lambda l, j: (0, l, 0, j)), pl.BlockSpec((1, D, tn), lambda l, j: (l, 0, j))],
        out_specs=[pl.BlockSpec((1, D, tn), lambda l, j: (l, 0, j)), pl.BlockSpec((8, D), lambda l, j: (0, 0))],
        out_shape=[jax.ShapeDtypeStruct((L, D, nl), F32), jax.ShapeDtypeStruct((8, D), F32)],
        compiler_params=_cparams(("arbitrary", "arbitrary"), 8 * _nbytes((D, tn), F32)),
    )(s_in, dml.reshape(MOD_ROWS, L, 1, nl), dmc.reshape(NDEV, L, 1, nl), ada_w)


def _cctx_update(bufs, c_ctx, m, v, name):
    n, R, _ = bufs.shape

    def body(b_ref, w_ref, m_ref, v_ref, g_ref, d_ref, m2_ref, v2_ref):
        g = b_ref[0]
        for q in range(1, n):
            g = g + b_ref[q]
        g = g * _dsilu(w_ref[...])
        d, m2, v2 = _adamw_math(w_ref[...], g, m_ref[...], v_ref[...])
        g_ref[...], d_ref[...], m2_ref[...], v2_ref[...] = g, d, m2, v2

    vm = pl.BlockSpec(memory_space=pltpu.VMEM)
    shp = jax.ShapeDtypeStruct((R, LANE), F32)
    return pl.pallas_call(body, name=name, in_specs=[vm, vm, vm, vm], out_specs=[vm, vm, vm, vm],
                          out_shape=[shp, shp, shp, shp])(bufs, c_ctx, m, v)


def _pack(arrs):
    flat = jnp.concatenate([a.reshape(-1).astype(F32) for a in arrs])
    n = flat.shape[0]
    total = -(-n // (8 * LANE)) * (8 * LANE)
    return jnp.pad(flat, (0, total - n)).reshape(total // LANE, LANE)


def _unpack(buf, shapes):
    lead = buf.shape[:-2]
    flat = buf.reshape(lead + (-1,))
    out, off = [], 0
    for s in shapes:
        n = int(np.prod(s))
        out.append(flat[..., off:off + n].reshape(lead + tuple(s)))
        off += n
    return out


WEIGHTS = ['c_ctx', 'ada_w', 'ada_b', 'norm1_w', 'norm2_w', 'ssd_w_in', 'ssd_conv_w', 'ssd_conv_b', 'ssd_dt_bias_f',
           'ssd_dt_bias_b', 'ssd_a_log_f', 'ssd_a_log_b', 'ssd_d', 'ssd_norm_w', 'ssd_w_out', 'attn_w_qkv', 'attn_q_gain',
           'attn_k_gain', 'attn_sinks', 'attn_w_o', 'ffn_w_up', 'ffn_conv_w', 'ffn_conv_b', 'ffn_w_down']
SMALL = ['ada_b', 'norm1_w', 'norm2_w', 'ssd_conv_b', 'ssd_dt_bias_f', 'ssd_dt_bias_b', 'ssd_a_log_f', 'ssd_a_log_b', 'ssd_d',
         'ssd_norm_w', 'attn_q_gain', 'attn_k_gain', 'attn_sinks', 'ffn_conv_b']
BIG = ['ssd_w_in', 'ssd_w_out', 'attn_w_qkv', 'attn_w_o', 'ffn_w_up', 'ffn_w_down']


def _step(x, c, ctx, w, tgt, m, v):
    xi, yi, ci = _me()
    me = 4 * xi + 2 * yi + ci
    t_lat, D = x.shape[1], x.shape[2]
    ctx_rows = ctx.shape[1]
    T = ctx_rows + t_lat
    L, n_ssd, n_att = w['norm1_w'].shape[0], w['ssd_d'].shape[0], w['attn_sinks'].shape[0]
    H, DI, XBC = w['ssd_d'].shape[1], w['ssd_norm_w'].shape[1], w['ssd_conv_b'].shape[1]
    P = DI // H
    IN = w['ssd_w_in'].shape[2] * NDEV
    hd, n_q = w['attn_q_gain'].shape[1], w['attn_sinks'].shape[1]
    F2 = w['ffn_conv_b'].shape[1]
    G = F2 // NDEV
    nl = w['ada_w'].shape[2]
    ncc = ctx_rows // BLK
    perm = [_ffn_perm(s) for s in range(NDEV)]
    inv = [perm.index(d) for d in range(NDEV)]

    def reorder(t, order):
        return jnp.concatenate([t[..., o * G:(o + 1) * G] for o in order], axis=-1)

    def interleave(t):
        return reorder(t, perm)

    def deinterleave(t):
        return reorder(t, inv)

    def layer_weights(i):
        mixer = ['ssd_w_in', 'ssd_w_out'] if i % 2 == 0 else ['attn_w_qkv', 'attn_w_o']
        return [(n, i // 2) for n in mixer] + [('ffn_w_up', i), ('ffn_w_down', i)]

    def gather_start(keys, tag, zero):
        srcs = [(w[n][j] + zero).astype(MXU) for n, j in keys]
        lands = [lax.dynamic_update_index_in_dim(lax.empty((NDEV,) + t.shape, t.dtype), t, me, 0) for t in srcs]
        ss, rs, srcs, lands, tok = _xfer_start(srcs, lands, [None] * len(srcs), False, f"gather_start_{tag}")
        return (ss, rs, srcs, lands, tag), tok[0, 0]

    def gather_wait(p, after):
        return _xfer_wait(*p[:4], [None] * len(p[2]), False, after, f"gather_wait_{p[4]}")

    first, tok = gather_start(layer_weights(0)[:1], "0a", jnp.zeros((), F32))
    pending, tok = gather_start(layer_weights(0)[1:], "0b", tok)

    shapes_a = [(D,), w['ssd_conv_w'].shape, w['ffn_conv_w'].shape]
    g_a = _small_allgather(_pack([c[0], w['ssd_conv_w'], w['ffn_conv_w']]), "gather_cond")
    c_all, scw_all, fcw_all = _unpack(g_a, shapes_a)
    ssd_cw = scw_all.transpose(1, 2, 0, 3).reshape(n_ssd, 3, XBC)
    ffn_cw = jnp.concatenate([fcw_all[d] for d in perm], axis=-1)
    ffn_cb = interleave(w['ffn_conv_b'])[:, None, :]

    s_in = jnp.concatenate([c_all, w['c_ctx'][None], jnp.zeros((MOD_ROWS - NDEV - 1, D), F32)], axis=0)
    b_loc = lax.dynamic_slice(w['ada_b'], (0, me * nl), (L, nl))[:, None, :]
    mod_loc = _modulation(s_in, w['ada_w'], b_loc, "modulation")
    g_b = _small_allgather(_pack([mod_loc]), "gather_mod")
    (mod_all,) = _unpack(g_b, [mod_loc.shape])
    mod_lat = lax.dynamic_index_in_dim(mod_all, me, axis=2, keepdims=False)
    mod_ctx = mod_all[:, :, NDEV, :]
    to_mod = lambda t: t.transpose(1, 0, 2).reshape(L, 6, D)
    mod = jnp.stack([to_mod(mod_ctx), to_mod(mod_lat)], axis=1)

    w_in, w_out, w_qkv, w_o = [None] * n_ssd, [None] * n_ssd, [None] * n_att, [None] * n_att
    w_up, w_down = [None] * L, [None] * L

    cos, sin = _rope_tables(t_lat, ctx_rows, hd)
    e_f, e_b = _expand_matrix(H, P, False), _expand_matrix(H, P, True)
    et_f, et_b = e_f.T, e_b.T

    xs = jnp.concatenate([ctx[0], x[0]], axis=0)
    saved = []
    for i in range(L):
        j = i // 2
        s = dict(x0=xs)
        zero_of = lambda t: jnp.minimum(jnp.abs(t[0, 0, 0].astype(F32)), 0.0)
        if i == 0:
            w_in[0] = _cat_cols(gather_wait(first, mod)[0], "ssd_in_cat")
        else:
            got = gather_wait(pending, xs)
            if i % 2 == 0:
                w_in[j], w_out[j] = _cat_cols(got[0], "ssd_in_cat"), got[1].reshape(DI, D)
            else:
                w_qkv[j], w_o[j] = got[0], got[1].reshape(n_q * hd, D)
            w_up[i], w_down[i] = got[2], got[3].reshape(F2 // 2, D)
            if i + 1 < L:
                pending, tok = gather_start(layer_weights(i + 1), f"{i + 1}", zero_of(got[1]))
        nw1, nw2 = w['norm1_w'][i][None] + tok, w['norm2_w'][i][None]
        s['h1'] = _normmod(xs, nw1, mod[i], 0, ctx_rows, "normmod")
        if i % 2 == 0:
            s['zx'] = _mm(s['h1'], w_in[j], tm=768, tn=1152, tk=2048, name="mm_ssd_in")
            s['cw'], s['cb'] = ssd_cw[j], w['ssd_conv_b'][j][None]
            s['xbc'] = _ssd_conv(s['zx'], s['cw'], s['cb'], DI, ctx_rows, "ssd_conv")
            s['dtb'] = jnp.concatenate([w['ssd_dt_bias_f'][j], w['ssd_dt_bias_b'][j]])[None]
            s['alog'] = jnp.concatenate([w['ssd_a_log_f'][j], w['ssd_a_log_b'][j]])[None]
            s['yf'], s['hin_f'] = _ssd_scan(s['xbc'], s['zx'], s['dtb'], s['alog'], e_f, H, ncc, False, "ssd_scan_f")
            s['yb'], s['hin_b'] = _ssd_scan(s['xbc'], s['zx'], s['dtb'], s['alog'], e_b, H, ncc, True, "ssd_scan_b")
            s['dexp'], s['snw'] = jnp.repeat(w['ssd_d'][j], P)[None], w['ssd_norm_w'][j][None]
            if i == 0:
                got = gather_wait(pending, s['yb'])
                w_out[0], w_up[0], w_down[0] = got[0].reshape(DI, D), got[1], got[2].reshape(F2 // 2, D)
                if L > 1:
                    pending, tok = gather_start(layer_weights(1), "1", zero_of(got[0]))
                    s['snw'] = s['snw'] + tok
            s['o'] = _ssd_finish(s['yf'], s['yb'], s['xbc'], s['zx'], s['dexp'], s['snw'], "ssd_finish")
            s['mix'], x1 = _mm(s['o'], w_out[j], tm=768, tn=1024, tk=2048, name="mm_ssd_out",
                               resid=xs, gate=mod[i][:, 2], ctx_rows=ctx_rows)
        else:
            s['qkv'] = _mm(s['h1'], w_qkv[j], tm=768, tn=384, tk=2048, name="mm_qkv", bslots=_ident)
            s['qg'], s['kg'] = w['attn_q_gain'][j][None], w['attn_k_gain'][j][None]
            s['qr'], s['kr'], s['vb'] = _qk_prep(s['qkv'], s['qg'], s['kg'], cos, sin, n_q, ctx_rows, "qk_prep")
            s['o'] = _attn_fwd(s['qr'], s['kr'], s['vb'], w['attn_sinks'][j], n_q, ctx_rows, "attn_fwd")
            s['mix'], x1 = _mm(s['o'], w_o[j], tm=768, tn=1024, tk=2048, name="mm_attn_out",
                               resid=xs, gate=mod[i][:, 2], ctx_rows=ctx_rows)
        s['x1'] = x1
        s['h2'] = _normmod(x1, nw2, mod[i], 1, ctx_rows, "normmod")
        s['u'] = _mm(s['h2'], w_up[i], tm=768, tn=1408, tk=2048, name="mm_ffn_up", bslots=_ffn_perm)
        s['a'] = _ffn_mid(s['u'], ffn_cw[i], ffn_cb[i], ctx_rows, "ffn_mid")
        s['f'], xs = _mm(s['a'], w_down[i], tm=768, tn=1024, tk=1408, name="mm_ffn_down",
                         resid=x1, gate=mod[i][:, 5], ctx_rows=ctx_rows)
        saved.append(s)

    dx, lacc = _loss_head(xs, tgt[0], ctx_rows, "loss_head")
    loss = lax.psum(lacc[0, 0], ("x", "y", "c"))

    gbig = {name: [None] * w[name].shape[0] for name in BIG}
    gs = {name: [None] * w[name].shape[0] for name in SMALL + ['ssd_conv_w', 'ffn_conv_w']}
    dmod = [None] * L
    lands = {n: lax.empty((NDEV,) + w[n].shape, XFER) for n in BIG}
    pend_early, pend_late, tok = None, None, jnp.zeros((), F32)

    def scatter_start(keys, tag, zero=None):
        srcs = [gbig[n][j] for n, j in keys]
        for (n, j), g in zip(keys, srcs):
            own = lax.dynamic_index_in_dim(g, me, 0, keepdims=False)
            if zero is not None:
                own = own + zero.astype(own.dtype)
            lands[n] = lax.dynamic_update_slice(lands[n], own[None, None], (me, j, 0, 0))
        ss, rs, srcs, got, t = _xfer_start(srcs, [lands[n] for n, _ in keys], [j for _, j in keys], True, f"scatter_start_{tag}")
        return (ss, rs, srcs, got, keys, tag), t[0, 0]

    def scatter_wait(p, after):
        ss, rs, srcs, got, keys, tag = p
        got = _xfer_wait(ss, rs, srcs, got, [j for _, j in keys], True, after, f"scatter_wait_{tag}")
        for (n, _), t in zip(keys, got):
            lands[n] = t

    for i in reversed(range(L)):
        j = i // 2
        s = saved[i]
        nw1, nw2 = w['norm1_w'][i][None], w['norm2_w'][i][None]
        dm2, dg2 = _gate_bwd(dx, s['f'], mod[i] + tok, 1, ctx_rows, "gate_bwd")
        da = _mm(dm2, w_down[i], tb=True, out_dtype=MXU, tm=768, tn=1408, tk=2048, name="mm_ffn_down_dx")
        gbig['ffn_w_down'][i] = _mm(s['a'], dm2, ta=True, out_dtype=XFER, tm=1408, tn=1024, tk=1056,
                                    name="mm_ffn_down_dw").reshape(NDEV, -1, D)
        du, gcw = _ffn_mid_bwd(s['u'], da, ffn_cw[i], ffn_cb[i], ctx_rows, "ffn_mid_bwd")
        gcw = deinterleave(gcw)
        gs['ffn_conv_w'][i], gs['ffn_conv_b'][i] = gcw[0:3], gcw[3]
        dh2 = _mm(du, w_up[i], tb=True, out_dtype=MXU, tm=768, tn=1024, tk=1408, name="mm_ffn_up_dx", bslots=_ffn_perm)
        gbig['ffn_w_up'][i] = _mm(s['h2'], du, ta=True, out_dtype=XFER, tm=1024, tn=1408, tk=1056, name="mm_ffn_up_dw",
                                  oslots=_ffn_perm)
        dx1, sums2 = _normmod_bwd(s['x1'], nw2, mod[i], dh2, dx, 1, ctx_rows, "normmod_bwd")
        dmix, dg1 = _gate_bwd(dx1, s['mix'], mod[i], 0, ctx_rows, "gate_bwd")

        def send_early(after):
            if pend_early is not None:
                scatter_wait(pend_early, after)
            return scatter_start(layer_weights(i)[1:], f"early_{i}")

        if i % 2 == 0:
            do = _mm(dmix, w_out[j], tb=True, out_dtype=MXU, tm=768, tn=1024, tk=2048, name="mm_ssd_out_dx")
            gbig['ssd_w_out'][j] = _mm(s['o'], dmix, ta=True, out_dtype=XFER, tm=1024, tn=1024, tk=1056,
                                       name="mm_ssd_out_dw").reshape(NDEV, -1, D)
            pend_early, tok = send_early(do)
            dy, dzx, fs, dd = _ssd_finish_bwd(s['yf'], s['yb'], s['xbc'], s['zx'], s['dexp'], s['snw'] + tok, do, et_f, (T, IN),
                                              "ssd_finish_bwd")
            acc = _ssd_scan_bwd(s['xbc'], s['zx'], s['dtb'], s['alog'], e_f, et_f, s['dexp'], s['hin_f'], dy, None,
                                H, ncc, False, "ssd_scan_bwd_f")
            dxbc, ddt, ssm = _ssd_scan_bwd(s['xbc'], s['zx'], s['dtb'], s['alog'], e_b, et_b, s['dexp'], s['hin_b'], dy, acc,
                                           H, ncc, True, "ssd_scan_bwd_b")
            dzx, gscw = _ssd_conv_bwd(s['zx'], dxbc, dzx, s['cw'], s['cb'], DI, ctx_rows, "ssd_conv_bwd")
            dzx = _put_cols(dzx, ddt, IN // LANE - 1, "ssd_put_ddt")
            dh1 = _mm(dzx, w_in[j], tb=True, out_dtype=MXU, tm=768, tn=1024, tk=1152, name="mm_ssd_in_dx")
            dwi = _mm(s['h1'], dzx, ta=True, tm=1024, tn=1152, tk=1056, name="mm_ssd_in_dw")
            gbig['ssd_w_in'][j] = _split_cols(dwi, NDEV, XFER, "ssd_in_split")
            gs['ssd_conv_w'][j], gs['ssd_conv_b'][j] = gscw[0:3], gscw[3]
            gs['ssd_dt_bias_f'][j], gs['ssd_dt_bias_b'][j] = ssm[1, :H], ssm[1, H:]
            gs['ssd_a_log_f'][j], gs['ssd_a_log_b'][j] = ssm[0, :H], ssm[0, H:]
            gs['ssd_d'][j], gs['ssd_norm_w'][j] = dd[1, :H], fs[0]
        else:
            do = _mm(dmix, w_o[j], tb=True, out_dtype=MXU, tm=768, tn=1024, tk=2048, name="mm_attn_out_dx")
            gbig['attn_w_o'][j] = _mm(s['o'], dmix, ta=True, out_dtype=XFER, tm=1024, tn=1024, tk=1056,
                                      name="mm_attn_out_dw").reshape(NDEV, -1, D)
            pend_early, tok = send_early(do)
            dq, dk, dv, dsk = _attn_bwd(s['qr'], s['kr'], s['vb'], w['attn_sinks'][j] + tok, do, n_q, ctx_rows, "attn_bwd")
            dqkv, gg = _qk_prep_bwd(s['qkv'], s['qg'], s['kg'], cos, sin, dq, dk, dv, n_q, ctx_rows, "qk_prep_bwd")
            dh1 = _mm(dqkv, w_qkv[j], tb=True, out_dtype=MXU, tm=768, tn=1024, tk=384, name="mm_qkv_dx", bslots=_ident)
            gbig['attn_w_qkv'][j] = _mm(s['h1'], dqkv, ta=True, out_dtype=XFER, tm=1024, tn=384, tk=1056, name="mm_qkv_dw",
                                        oslots=_ident)
            gs['attn_q_gain'][j], gs['attn_k_gain'][j] = gg[0], gg[1]
            gs['attn_sinks'][j] = dsk.reshape(ATTN_KV_HEADS, 8, LANE)[:, 0, :n_q // ATTN_KV_HEADS].reshape(n_q)
        dx, sums1 = _normmod_bwd(s['x0'], nw1, mod[i], dh1, dx1, 0, ctx_rows, "normmod_bwd")
        gs['norm1_w'][i], gs['norm2_w'][i] = sums1[0, 2] + sums1[1, 2], sums2[0, 2] + sums2[1, 2]
        dmod[i] = jnp.stack([sums1[:, 0], sums1[:, 1], dg1[:, 0], sums2[:, 0], sums2[:, 1], dg2[:, 0]], axis=1)
        if pend_late is not None:
            scatter_wait(pend_late, dx)
            pend_late = None
        if i > 0:
            pend_late, tok = scatter_start(layer_weights(i)[:1], f"late_{i}")
    grad_x = dx[ctx_rows:][None]
    dmod = jnp.stack(dmod)
    dmod_ctx, dmod_lat = dmod[:, 0].reshape(L, 6 * D), dmod[:, 1].reshape(L, 6 * D)
    gs['ada_b'] = dmod_ctx + dmod_lat

    out = {}

    small_g = [jnp.stack(gs[n]) if isinstance(gs[n], list) else gs[n] for n in SMALL]
    extras = [jnp.stack(gs['ssd_conv_w']), jnp.stack(gs['ffn_conv_w'])]
    shapes_c = [w[n].shape for n in SMALL] + [e.shape for e in extras]
    g_c = _small_allgather(_pack(small_g + extras), "gather_small")
    zeros = [jnp.zeros(e.shape, F32) for e in extras]
    res = _adamw_small(g_c, _pack([w[n] for n in SMALL] + zeros), _pack([m[n] for n in SMALL] + zeros),
                       _pack([v[n] for n in SMALL] + zeros), "adamw_small")
    res = [_unpack(r, shapes_c) for r in res]
    for k, n in enumerate(SMALL):
        out[n] = tuple(r[k] for r in res)
    g_scw, g_fcw = res[0][len(SMALL)], res[0][len(SMALL) + 1]
    g_scw = lax.dynamic_index_in_dim(g_scw.reshape(n_ssd, 3, NDEV, XBC // NDEV), me, axis=2, keepdims=False)
    g_fcw = lax.dynamic_index_in_dim(g_fcw.reshape(L, 3, NDEV, G), me, axis=2, keepdims=False)
    conv = ['ssd_conv_w', 'ffn_conv_w']
    res = _adamw_small(_pack([g_scw, g_fcw])[None], _pack([w[n] for n in conv]), _pack([m[n] for n in conv]),
                       _pack([v[n] for n in conv]), "adamw_conv")
    res = [_unpack(r, [w[n].shape for n in conv]) for r in res]
    for k, n in enumerate(conv):
        out[n] = tuple(r[k] for r in res)

    g_m = _small_allgather(jnp.concatenate([dmod_lat, dmod_ctx], axis=0), "gather_dmod")
    all_lat, all_ctx = g_m[:, :L], g_m[:, L:]
    my_cols = lambda t: lax.dynamic_slice(t, (0, 0, me * nl), (NDEV, L, nl))
    dml = jnp.concatenate([my_cols(all_lat), jnp.zeros((MOD_ROWS - NDEV, L, nl), F32)], axis=0)
    g_ada, pc = _modulation_bwd(s_in, dml, my_cols(all_ctx), w['ada_w'], "modulation_bwd")
    out['ada_w'] = _adamw(g_ada[None], w['ada_w'], m['ada_w'], v['ada_w'], "adamw")
    g_d = _small_allgather(_pack([pc[0]]), "gather_cctx")
    res = _cctx_update(g_d, _pack([w['c_ctx']]), _pack([m['c_ctx']]), _pack([v['c_ctx']]), "adamw_cctx")
    out['c_ctx'] = tuple(_unpack(r, [(D,)])[0] for r in res)

    c_new = out['c_ctx'][0]
    pend_late, _ = scatter_start(layer_weights(0)[:1], "late_0", jnp.minimum(jnp.abs(c_new[0]), 0.0))
    scatter_wait(pend_early, c_new)
    last = [n for n, _ in layer_weights(0)[:1]]
    for name in [n for n in BIG if n not in last]:
        out[name] = _adamw(lands[name], w[name], m[name], v[name], "adamw")
    scatter_wait(pend_late, out['ffn_w_down'][0])
    for name in last:
        out[name] = _adamw(lands[name], w[name], m[name], v[name], "adamw")

    return (loss, grad_x) + tuple(out[n][k] for k in range(4) for n in WEIGHTS)


def kernel(x, c, ctx, c_ctx, ada_w, ada_b, norm1_w, norm2_w, ssd_w_in, ssd_conv_w, ssd_conv_b, ssd_dt_bias_f, ssd_dt_bias_b, ssd_a_log_f, ssd_a_log_b, ssd_d, ssd_norm_w, ssd_w_out, attn_w_qkv, attn_q_gain, attn_k_gain, attn_sinks, attn_w_o, ffn_w_up, ffn_conv_w, ffn_conv_b, ffn_w_down, loss_target, m_c_ctx, m_ada_w, m_ada_b, m_norm1_w, m_norm2_w, m_ssd_w_in, m_ssd_conv_w, m_ssd_conv_b, m_ssd_dt_bias_f, m_ssd_dt_bias_b, m_ssd_a_log_f, m_ssd_a_log_b, m_ssd_d, m_ssd_norm_w, m_ssd_w_out, m_attn_w_qkv, m_attn_q_gain, m_attn_k_gain, m_attn_sinks, m_attn_w_o, m_ffn_w_up, m_ffn_conv_w, m_ffn_conv_b, m_ffn_w_down, v_c_ctx, v_ada_w, v_ada_b, v_norm1_w, v_norm2_w, v_ssd_w_in, v_ssd_conv_w, v_ssd_conv_b, v_ssd_dt_bias_f, v_ssd_dt_bias_b, v_ssd_a_log_f, v_ssd_a_log_b, v_ssd_d, v_ssd_norm_w, v_ssd_w_out, v_attn_w_qkv, v_attn_q_gain, v_attn_k_gain, v_attn_sinks, v_attn_w_o, v_ffn_w_up, v_ffn_conv_w, v_ffn_conv_b, v_ffn_w_down):
    w = dict(c_ctx=c_ctx, ada_w=ada_w, ada_b=ada_b, norm1_w=norm1_w, norm2_w=norm2_w, ssd_w_in=ssd_w_in, ssd_conv_w=ssd_conv_w, ssd_conv_b=ssd_conv_b, ssd_dt_bias_f=ssd_dt_bias_f, ssd_dt_bias_b=ssd_dt_bias_b, ssd_a_log_f=ssd_a_log_f, ssd_a_log_b=ssd_a_log_b, ssd_d=ssd_d, ssd_norm_w=ssd_norm_w, ssd_w_out=ssd_w_out, attn_w_qkv=attn_w_qkv, attn_q_gain=attn_q_gain, attn_k_gain=attn_k_gain, attn_sinks=attn_sinks, attn_w_o=attn_w_o, ffn_w_up=ffn_w_up, ffn_conv_w=ffn_conv_w, ffn_conv_b=ffn_conv_b, ffn_w_down=ffn_w_down)
    m = dict(c_ctx=m_c_ctx, ada_w=m_ada_w, ada_b=m_ada_b, norm1_w=m_norm1_w, norm2_w=m_norm2_w, ssd_w_in=m_ssd_w_in, ssd_conv_w=m_ssd_conv_w, ssd_conv_b=m_ssd_conv_b, ssd_dt_bias_f=m_ssd_dt_bias_f, ssd_dt_bias_b=m_ssd_dt_bias_b, ssd_a_log_f=m_ssd_a_log_f, ssd_a_log_b=m_ssd_a_log_b, ssd_d=m_ssd_d, ssd_norm_w=m_ssd_norm_w, ssd_w_out=m_ssd_w_out, attn_w_qkv=m_attn_w_qkv, attn_q_gain=m_attn_q_gain, attn_k_gain=m_attn_k_gain, attn_sinks=m_attn_sinks, attn_w_o=m_attn_w_o, ffn_w_up=m_ffn_w_up, ffn_conv_w=m_ffn_conv_w, ffn_conv_b=m_ffn_conv_b, ffn_w_down=m_ffn_w_down)
    v = dict(c_ctx=v_c_ctx, ada_w=v_ada_w, ada_b=v_ada_b, norm1_w=v_norm1_w, norm2_w=v_norm2_w, ssd_w_in=v_ssd_w_in, ssd_conv_w=v_ssd_conv_w, ssd_conv_b=v_ssd_conv_b, ssd_dt_bias_f=v_ssd_dt_bias_f, ssd_dt_bias_b=v_ssd_dt_bias_b, ssd_a_log_f=v_ssd_a_log_f, ssd_a_log_b=v_ssd_a_log_b, ssd_d=v_ssd_d, ssd_norm_w=v_ssd_norm_w, ssd_w_out=v_ssd_w_out, attn_w_qkv=v_attn_w_qkv, attn_q_gain=v_attn_q_gain, attn_k_gain=v_attn_k_gain, attn_sinks=v_attn_sinks, attn_w_o=v_attn_w_o, ffn_w_up=v_ffn_w_up, ffn_conv_w=v_ffn_conv_w, ffn_conv_b=v_ffn_conv_b, ffn_w_down=v_ffn_w_down)
    return _step(x, c, ctx, w, loss_target, m, v)
```

```python
import functools

import numpy as np
import jax
import jax.numpy as jnp
from jax import lax
from jax.experimental import pallas as pl
from jax.experimental.pallas import tpu as pltpu

F32 = jnp.float32
BF16 = jnp.bfloat16
MXU = BF16
XFER = BF16
NORM_EPS = 1e-6
VMEM_CAP = 56 * 1024 * 1024
HALO = 8
LANE = 128
NDEV = 8

GRID_W = 64
ROPE_THETA = 10000.0
ATTN_KV_HEADS = 4
ATTN_WINDOW = 128
BLK = 128
SSD_GROUPS = 8

ADAM_LR, ADAM_B1, ADAM_B2, ADAM_EPS, ADAM_WD, ADAM_STEP = 0.001, 0.9, 0.999, 1e-08, 0.01, 10

MESH_ID = pl.DeviceIdType.MESH


def _cparams(sem, est_bytes):
    lim = int(min(VMEM_CAP, max(16 * 1024 * 1024, est_bytes * 1.3 + (4 << 20))))
    return pltpu.CompilerParams(dimension_semantics=sem, vmem_limit_bytes=lim)


def _nbytes(shape, dtype):
    return int(np.prod(shape)) * jnp.dtype(dtype).itemsize


def _silu(x):
    return x * jax.nn.sigmoid(x)


def _dsilu(x):
    s = jax.nn.sigmoid(x)
    return s * (1.0 + x * (1.0 - s))


def _split3(v):
    h = v.astype(BF16)
    r = v - h.astype(F32)
    m = r.astype(BF16)
    l = (r - m.astype(F32)).astype(BF16)
    return h, m, l


def _dot(a, b, dn=(((1,), (0,)), ((), ()))):
    return lax.dot_general(a, b, dn, preferred_element_type=F32)


NT = (((1,), (1,)), ((), ()))
TN = (((0,), (0,)), ((), ()))


def _dot3_rhs(sel, v):
    return sum(_dot(sel, p) for p in _split3(v))


def _dot3_lhs(v, sel, dn=(((1,), (0,)), ((), ()))):
    return sum(_dot(p, sel, dn) for p in _split3(v))


def _dot2_stacked(vals, sel):
    pieces = []
    for v in vals:
        h, m, _ = _split3(v)
        pieces += [h, m]
    r = _dot(jnp.concatenate(pieces, axis=0), sel)
    out, row = [], 0
    for v in vals:
        n = v.shape[0]
        out.append(r[row:row + n] + r[row + n:row + 2 * n])
        row += 2 * n
    return out


def _expand2(vals, sel2):
    lhs = []
    for v in vals:
        h, m, _ = _split3(v)
        lhs.append(jnp.concatenate([h, m], axis=1))
    r = _dot(jnp.concatenate(lhs, axis=0), sel2)
    n = vals[0].shape[0]
    return [r[i * n:(i + 1) * n] for i in range(len(vals))]


def _ident(s):
    return s


def _ffn_perm(s):
    return (s % 2) * 4 + s // 2


def _mm(a, b, *, ta=False, tb=False, out_dtype=F32, tm, tn, tk, name, bslots=None, oslots=None,
        resid=None, gate=None, ctx_rows=0):
    M = a.shape[1] if ta else a.shape[0]
    K = a.shape[0] if ta else a.shape[1]
    if bslots is None:
        N = b.shape[0] if tb else b.shape[1]
    else:
        G = b.shape[2]
        N = b.shape[1] if tb else NDEV * G
        assert (NDEV * G == K) if tb else (b.shape[1] == K)
    tm, tn, tk = min(tm, M), min(tn, N), min(tk, K)
    if bslots is not None:
        if tb:
            tk = min(tk, G)
            assert G % tk == 0
        else:
            tn = min(tn, G)
            assert G % tn == 0
    if oslots is not None:
        Go = N // NDEV
        tn = min(tn, Go)
        assert Go % tn == 0
    assert M % tm == 0 and N % tn == 0 and K % tk == 0, (name, M, N, K, tm, tn, tk)
    nk = K // tk
    fused = resid is not None
    dn = (((0 if ta else 1,), (1 if tb else 0,)), ((), ()))

    def body(*refs):
        if fused:
            a_ref, b_ref, r_ref, g_ref, o_ref, x_ref = refs[:6]
            rest = refs[6:]
        else:
            a_ref, b_ref, o_ref = refs[:3]
            rest = refs[3:]
        bv = b_ref[0] if bslots is not None else b_ref[...]
        p = lax.dot_general(a_ref[...].astype(MXU), bv.astype(MXU), dn, preferred_element_type=F32)

        def finish(acc):
            if oslots is not None:
                o_ref[0] = acc.astype(o_ref.dtype)
            else:
                o_ref[...] = acc.astype(o_ref.dtype)
            if fused:
                row = pl.program_id(0) * tm + lax.broadcasted_iota(jnp.int32, (tm, 1), 0)
                g = jnp.where(row < ctx_rows, g_ref[0:1, :], g_ref[1:2, :])
                x_ref[...] = r_ref[...] + g * acc

        if nk == 1:
            finish(p)
        else:
            acc_ref = rest[0]
            k = pl.program_id(2)

            @pl.when(k == 0)
            def _():
                acc_ref[...] = p

            @pl.when(k > 0)
            def _():
                acc_ref[...] += p

            @pl.when(k == nk - 1)
            def _():
                finish(acc_ref[...])

    a_spec = pl.BlockSpec((tk, tm), lambda i, j, k: (k, i)) if ta else pl.BlockSpec((tm, tk), lambda i, j, k: (i, k))
    if bslots is None:
        b_spec = pl.BlockSpec((tn, tk), lambda i, j, k: (j, k)) if tb else pl.BlockSpec((tk, tn), lambda i, j, k: (k, j))
    elif tb:
        kpg = G // tk
        b_spec = pl.BlockSpec((1, tn, tk), lambda i, j, k: (bslots(k // kpg), j, k % kpg))
    else:
        npg = G // tn
        b_spec = pl.BlockSpec((1, tk, tn), lambda i, j, k: (bslots(j // npg), k, j % npg))
    if oslots is None:
        o_spec = pl.BlockSpec((tm, tn), lambda i, j, k: (i, j))
        o_shape = jax.ShapeDtypeStruct((M, N), out_dtype)
    else:
        opg = Go // tn
        o_spec = pl.BlockSpec((1, tm, tn), lambda i, j, k: (oslots(j // opg), i, j % opg))
        o_shape = jax.ShapeDtypeStruct((NDEV, M, Go), out_dtype)
    in_specs = [a_spec, b_spec]
    out_shape = [o_shape]
    out_specs = [o_spec]
    args = [a, b]
    est = 2 * (_nbytes((tm, tk), a.dtype) + _nbytes((tk, tn), b.dtype) + _nbytes((tm, tn), out_dtype)) + 3 * _nbytes((tm, tn), F32)
    if fused:
        in_specs += [o_spec, pl.BlockSpec((2, tn), lambda i, j, k: (0, j))]
        out_shape.append(jax.ShapeDtypeStruct((M, N), F32))
        out_specs.append(o_spec)
        args += [resid, gate]
        est += 4 * _nbytes((tm, tn), F32)
    scratch = [] if nk == 1 else [pltpu.VMEM((tm, tn), F32)]
    res = pl.pallas_call(
        body, name=name, grid=(M // tm, N // tn, nk), in_specs=in_specs, out_specs=out_specs, out_shape=out_shape,
        scratch_shapes=scratch, compiler_params=_cparams(("parallel", "parallel", "arbitrary"), est),
    )(*args)
    return res if fused else res[0]


def _stream_of(i, tr, ctx_rows):
    return jnp.where(i * tr < ctx_rows, 0, 1)


def _acc_by_stream(sums_ref, part, i, n_ctx):
    @pl.when((i == 0) | (i == n_ctx))
    def _():
        sums_ref[0] = part

    @pl.when((i != 0) & (i != n_ctx))
    def _():
        sums_ref[0] += part


def _normmod(x, nw, mod, which, ctx_rows, name, tr=256):
    T, D = x.shape
    tr = min(tr, ctx_rows)
    assert T % tr == 0 and ctx_rows % tr == 0
    s_sh, s_sc = 3 * which, 3 * which + 1

    def body(x_ref, nw_ref, mod_ref, h_ref):
        xv = x_ref[...]
        r = lax.rsqrt(jnp.mean(xv * xv, axis=-1, keepdims=True) + NORM_EPS)
        y = (xv * r) * nw_ref[...]
        h_ref[...] = (y * (1.0 + mod_ref[0, s_sc:s_sc + 1, :]) + mod_ref[0, s_sh:s_sh + 1, :]).astype(h_ref.dtype)

    return pl.pallas_call(
        body, name=name, grid=(T // tr,),
        in_specs=[pl.BlockSpec((tr, D), lambda i: (i, 0)), pl.BlockSpec((1, D), lambda i: (0, 0)),
                  pl.BlockSpec((1, 6, D), lambda i: (_stream_of(i, tr, ctx_rows), 0, 0))],
        out_specs=pl.BlockSpec((tr, D), lambda i: (i, 0)),
        out_shape=jax.ShapeDtypeStruct((T, D), MXU),
        compiler_params=_cparams(("parallel",), 10 * _nbytes((tr, D), F32)),
    )(x, nw, mod)


def _normmod_bwd(x, nw, mod, dh, dx_in, which, ctx_rows, name, tr=256):
    T, D = x.shape
    tr = min(tr, ctx_rows)
    s_sc = 3 * which + 1
    n_ctx = ctx_rows // tr

    def body(x_ref, nw_ref, mod_ref, dh_ref, dxi_ref, dx_ref, sums_ref):
        i = pl.program_id(0)
        xv = x_ref[...]
        r = lax.rsqrt(jnp.mean(xv * xv, axis=-1, keepdims=True) + NORM_EPS)
        xh = xv * r
        dh_v = dh_ref[...].astype(F32)
        sc1 = 1.0 + mod_ref[0, s_sc:s_sc + 1, :]
        nwv = nw_ref[...]
        dxh = dh_v * (nwv * sc1)
        dx_ref[...] = dxi_ref[...] + r * (dxh - xh * jnp.mean(dxh * xh, axis=-1, keepdims=True))
        t = dh_v * xh
        part = jnp.concatenate([jnp.sum(dh_v, axis=0, keepdims=True), jnp.sum(t * nwv, axis=0, keepdims=True),
                                jnp.sum(t * sc1, axis=0, keepdims=True), jnp.zeros((5, D), F32)], axis=0)
        _acc_by_stream(sums_ref, part, i, n_ctx)

    row = pl.BlockSpec((tr, D), lambda i: (i, 0))
    return pl.pallas_call(
        body, name=name, grid=(T // tr,),
        in_specs=[row, pl.BlockSpec((1, D), lambda i: (0, 0)),
                  pl.BlockSpec((1, 6, D), lambda i: (_stream_of(i, tr, ctx_rows), 0, 0)), row, row],
        out_specs=[row, pl.BlockSpec((1, 8, D), lambda i: (_stream_of(i, tr, ctx_rows), 0, 0))],
        out_shape=[jax.ShapeDtypeStruct((T, D), F32), jax.ShapeDtypeStruct((2, 8, D), F32)],
        compiler_params=_cparams(("arbitrary",), 16 * _nbytes((tr, D), F32)),
    )(x, nw, mod, dh, dx_in)


def _gate_bwd(dx, mix, mod, which, ctx_rows, name, tr=256):
    T, D = dx.shape
    tr = min(tr, ctx_rows)
    s_g = 3 * which + 2
    n_ctx = ctx_rows // tr

    def body(dx_ref, mix_ref, mod_ref, dm_ref, sums_ref):
        i = pl.program_id(0)
        dxv = dx_ref[...]
        dm_ref[...] = (dxv * mod_ref[0, s_g:s_g + 1, :]).astype(dm_ref.dtype)
        part = jnp.concatenate([jnp.sum(dxv * mix_ref[...], axis=0, keepdims=True), jnp.zeros((7, D), F32)], axis=0)
        _acc_by_stream(sums_ref, part, i, n_ctx)

    row = pl.BlockSpec((tr, D), lambda i: (i, 0))
    return pl.pallas_call(
        body, name=name, grid=(T // tr,),
        in_specs=[row, row, pl.BlockSpec((1, 6, D), lambda i: (_stream_of(i, tr, ctx_rows), 0, 0))],
        out_specs=[row, pl.BlockSpec((1, 8, D), lambda i: (_stream_of(i, tr, ctx_rows), 0, 0))],
        out_shape=[jax.ShapeDtypeStruct((T, D), MXU), jax.ShapeDtypeStruct((2, 8, D), F32)],
        compiler_params=_cparams(("arbitrary",), 10 * _nbytes((tr, D), F32)),
    )(dx, mix, mod)


def _halo_specs(tr, tn, n_row_tiles, col_of):
    g = tr // HALO
    last = n_row_tiles * g - 1
    return [pl.BlockSpec((HALO, tn), lambda j, i: (jnp.maximum(i * g - 1, 0), col_of(j))),
            pl.BlockSpec((tr, tn), lambda j, i: (i, col_of(j))),
            pl.BlockSpec((HALO, tn), lambda j, i: (jnp.minimum((i + 1) * g, last), col_of(j)))]


def _ext(p_ref, m_ref, n_ref):
    return jnp.concatenate([p_ref[...].astype(F32), m_ref[...].astype(F32), n_ref[...].astype(F32)], axis=0)


def _seq_masks(i, tr, ctx_rows, total_rows):
    row = i * tr - HALO + lax.broadcasted_iota(jnp.int32, (tr + 2 * HALO, 1), 0)
    has_prev = (row != 0) & (row != ctx_rows)
    has_next = (row != ctx_rows - 1) & (row != total_rows - 1)
    return has_prev, has_next


def _shift_down(e):
    return pltpu.roll(e, 1, 0)


def _shift_up(e):
    return pltpu.roll(e, e.shape[0] - 1, 0)


def _neighbours(e, masks):
    prev, nxt = _shift_down(e), _shift_up(e)
    if masks is not None:
        prev, nxt = jnp.where(masks[0], prev, 0.0), jnp.where(masks[1], nxt, 0.0)
    return prev, nxt


def _conv3(e, prev, nxt, w):
    return prev * w[0:1, :] + e * w[1:2, :] + nxt * w[2:3, :]


def _conv3_t(d, w, masks):
    from_prev, from_next = _neighbours(d, masks)
    return from_next * w[0:1, :] + d * w[1:2, :] + from_prev * w[2:3, :]


def _conv_wgrad(d, e, prev, nxt):
    c = slice(HALO, e.shape[0] - HALO)
    dc = d[c]
    return jnp.concatenate([jnp.sum(dc * prev[c], axis=0, keepdims=True), jnp.sum(dc * e[c], axis=0, keepdims=True),
                            jnp.sum(dc * nxt[c], axis=0, keepdims=True), jnp.sum(dc, axis=0, keepdims=True),
                            jnp.zeros((4, e.shape[1]), F32)], axis=0)


def _per_tile_kind(i, tr, ctx_rows, total_rows, fn):
    n_ctx, n_all = ctx_rows // tr, total_rows // tr
    at_end = (i == 0) | (i == n_ctx - 1) | (i == n_ctx) | (i == n_all - 1)

    @pl.when(at_end)
    def _():
        fn(_seq_masks(i, tr, ctx_rows, total_rows))

    @pl.when(jnp.logical_not(at_end))
    def _():
        fn(None)


def _acc_first(ref, part, i):
    @pl.when(i == 0)
    def _():
        ref[...] = part

    @pl.when(i > 0)
    def _():
        ref[...] += part


def _ffn_mid(u, cw, cb, ctx_rows, name, tr=128):
    T, F2 = u.shape
    G = F2 // NDEV
    tr = min(tr, ctx_rows)
    nr, nc = T // tr, NDEV // 2

    def body(up, um, un, w_ref, b_ref, a_ref):
        def tile(masks):
            e = _ext(up, um, un)
            uc = _conv3(e, *_neighbours(e, masks), w_ref[...])[HALO:HALO + tr] + b_ref[...]
            a_ref[...] = (_silu(uc[:, G:]) * uc[:, :G]).astype(a_ref.dtype)

        _per_tile_kind(pl.program_id(1), tr, ctx_rows, T, tile)

    return pl.pallas_call(
        body, name=name, grid=(nc, nr),
        in_specs=_halo_specs(tr, 2 * G, nr, lambda j: j) + [pl.BlockSpec((3, 2 * G), lambda j, i: (0, j)),
                                                             pl.BlockSpec((1, 2 * G), lambda j, i: (0, j))],
        out_specs=pl.BlockSpec((tr, G), lambda j, i: (i, j)),
        out_shape=jax.ShapeDtypeStruct((T, F2 // 2), MXU),
        compiler_params=_cparams(("parallel", "parallel"), 12 * _nbytes((tr + 16, 2 * G), F32)),
    )(u, u, u, cw, cb)


def _ffn_mid_bwd(u, da, cw, cb, ctx_rows, name, tr=128):
    T, F2 = u.shape
    G = F2 // NDEV
    tr = min(tr, ctx_rows)
    nr, nc = T // tr, NDEV // 2

    def body(up, um, un, dp, dm, dn_, w_ref, b_ref, du_ref, gw_ref):
        i = pl.program_id(1)

        def tile(masks):
            e = _ext(up, um, un)
            w = w_ref[...]
            prev, nxt = _neighbours(e, masks)
            uc = _conv3(e, prev, nxt, w) + b_ref[...]
            val, gt = uc[:, :G], uc[:, G:]
            dav = _ext(dp, dm, dn_)
            sg = jax.nn.sigmoid(gt)
            duc = jnp.concatenate([dav * (gt * sg), dav * val * (sg * (1.0 + gt * (1.0 - sg)))], axis=1)
            du_ref[...] = _conv3_t(duc, w, masks)[HALO:HALO + tr].astype(du_ref.dtype)
            _acc_first(gw_ref, _conv_wgrad(duc, e, prev, nxt), i)

        _per_tile_kind(i, tr, ctx_rows, T, tile)

    du, gw = pl.pallas_call(
        body, name=name, grid=(nc, nr),
        in_specs=(_halo_specs(tr, 2 * G, nr, lambda j: j) + _halo_specs(tr, G, nr, lambda j: j)
                  + [pl.BlockSpec((3, 2 * G), lambda j, i: (0, j)), pl.BlockSpec((1, 2 * G), lambda j, i: (0, j))]),
        out_specs=[pl.BlockSpec((tr, 2 * G), lambda j, i: (i, j)), pl.BlockSpec((8, 2 * G), lambda j, i: (0, j))],
        out_shape=[jax.ShapeDtypeStruct((T, F2), MXU), jax.ShapeDtypeStruct((8, F2), F32)],
        compiler_params=_cparams(("parallel", "arbitrary"), 24 * _nbytes((tr + 16, 2 * G), F32)),
    )(u, u, u, da, da, da, cw, cb)
    return du, gw


def _ssd_conv(zx, cw, cb, col0, ctx_rows, name, tr=256, tn=512):
    T = zx.shape[0]
    C = cw.shape[1]
    tr, tn = min(tr, ctx_rows), min(tn, C)
    assert C % tn == 0 and col0 % tn == 0
    nr, nc, cb0 = T // tr, C // tn, col0 // tn

    def body(zp, zm, zn, w_ref, b_ref, o_ref):
        def tile(masks):
            e = _ext(zp, zm, zn)
            o_ref[...] = _silu(_conv3(e, *_neighbours(e, masks), w_ref[...])[HALO:HALO + tr] + b_ref[...])

        _per_tile_kind(pl.program_id(1), tr, ctx_rows, T, tile)

    return pl.pallas_call(
        body, name=name, grid=(nc, nr),
        in_specs=_halo_specs(tr, tn, nr, lambda j: j + cb0) + [pl.BlockSpec((3, tn), lambda j, i: (0, j)),
                                                                pl.BlockSpec((1, tn), lambda j, i: (0, j))],
        out_specs=pl.BlockSpec((tr, tn), lambda j, i: (i, j)),
        out_shape=jax.ShapeDtypeStruct((T, C), F32),
        compiler_params=_cparams(("parallel", "parallel"), 12 * _nbytes((tr + 16, tn), F32)),
    )(zx, zx, zx, cw, cb)


def _ssd_conv_bwd(zx, dxbc, dzx, cw, cb, col0, ctx_rows, name, tr=256, tn=512):
    T = zx.shape[0]
    C = cw.shape[1]
    tr, tn = min(tr, ctx_rows), min(tn, C)
    nr, nc, cb0 = T // tr, C // tn, col0 // tn

    def body(zp, zm, zn, dp, dm, dn_, w_ref, b_ref, dzx_in, dz_ref, gw_ref):
        del dzx_in
        i = pl.program_id(1)

        def tile(masks):
            e = _ext(zp, zm, zn)
            w = w_ref[...]
            prev, nxt = _neighbours(e, masks)
            pre = _conv3(e, prev, nxt, w) + b_ref[...]
            dpre = _ext(dp, dm, dn_) * _dsilu(pre)
            dz_ref[...] = _conv3_t(dpre, w, masks)[HALO:HALO + tr].astype(dz_ref.dtype)
            _acc_first(gw_ref, _conv_wgrad(dpre, e, prev, nxt), i)

        _per_tile_kind(i, tr, ctx_rows, T, tile)

    return pl.pallas_call(
        body, name=name, grid=(nc, nr),
        in_specs=(_halo_specs(tr, tn, nr, lambda j: j + cb0) + _halo_specs(tr, tn, nr, lambda j: j)
                  + [pl.BlockSpec((3, tn), lambda j, i: (0, j)), pl.BlockSpec((1, tn), lambda j, i: (0, j)),
                     pl.BlockSpec(memory_space=pl.ANY)]),
        out_specs=[pl.BlockSpec((tr, tn), lambda j, i: (i, j + cb0)), pl.BlockSpec((8, tn), lambda j, i: (0, j))],
        out_shape=[jax.ShapeDtypeStruct(dzx.shape, dzx.dtype), jax.ShapeDtypeStruct((8, C), F32)],
        input_output_aliases={8: 0},
        compiler_params=_cparams(("parallel", "arbitrary"), 24 * _nbytes((tr + 16, tn), F32)),
    )(zx, zx, zx, dxbc, dxbc, dxbc, cw, cb, dzx)


def _cat_cols(w3, name, tr=256):
    n, K, G = w3.shape
    tr = min(tr, K)

    def body(w_ref, o_ref):
        o_ref[...] = jnp.concatenate([w_ref[d].astype(F32) for d in range(n)], axis=1).astype(o_ref.dtype)

    return pl.pallas_call(
        body, name=name, grid=(K // tr,),
        in_specs=[pl.BlockSpec((n, tr, G), lambda i: (0, i, 0))], out_specs=pl.BlockSpec((tr, n * G), lambda i: (i, 0)),
        out_shape=jax.ShapeDtypeStruct((K, n * G), w3.dtype),
        compiler_params=_cparams(("parallel",), 6 * _nbytes((tr, n * G), F32)),
    )(w3)


def _split_cols(g, n, out_dtype, name, tr=256):
    K, NG = g.shape
    G = NG // n
    tr = min(tr, K)

    def body(g_ref, o_ref):
        for d in range(n):
            o_ref[d] = g_ref[:, d * G:(d + 1) * G].astype(o_ref.dtype)

    return pl.pallas_call(
        body, name=name, grid=(K // tr,),
        in_specs=[pl.BlockSpec((tr, NG), lambda i: (i, 0))], out_specs=pl.BlockSpec((n, tr, G), lambda i: (0, i, 0)),
        out_shape=jax.ShapeDtypeStruct((n, K, G), out_dtype),
        compiler_params=_cparams(("parallel",), 6 * _nbytes((tr, NG), F32)),
    )(g)


def _put_cols(dst, src, col_blk, name, tr=256):
    T, W = src.shape
    tr = min(tr, T)

    def body(s_ref, d_in, o_ref):
        del d_in
        o_ref[...] = s_ref[...].astype(o_ref.dtype)

    return pl.pallas_call(
        body, name=name, grid=(T // tr,),
        in_specs=[pl.BlockSpec((tr, W), lambda i: (i, 0)), pl.BlockSpec(memory_space=pl.ANY)],
        out_specs=pl.BlockSpec((tr, W), lambda i: (i, col_blk)),
        out_shape=jax.ShapeDtypeStruct(dst.shape, dst.dtype),
        input_output_aliases={1: 0},
        compiler_params=_cparams(("parallel",), 8 * _nbytes((tr, W), F32)),
    )(src, dst)


def _rope_tables(t_lat, ctx_rows, hd):
    half, quarter = hd // 2, hd // 4
    pos = jnp.arange(t_lat)
    row = (pos // GRID_W).astype(F32)
    col = (pos % GRID_W).astype(F32)
    inv_freq = ROPE_THETA ** (-jnp.arange(0, half, 2, dtype=F32) / half)
    ar, ac = row[:, None] * inv_freq[None, :], col[:, None] * inv_freq[None, :]
    cos = jnp.concatenate([jnp.cos(ar), jnp.cos(ar), jnp.cos(ac), jnp.cos(ac)], axis=1)
    sin = jnp.concatenate([-jnp.sin(ar), jnp.sin(ar), -jnp.sin(ac), jnp.sin(ac)], axis=1)
    del quarter
    cos = jnp.concatenate([jnp.ones((ctx_rows, hd), F32), cos], axis=0)
    sin = jnp.concatenate([jnp.zeros((ctx_rows, hd), F32), sin], axis=0)
    return cos, sin


def _partner(y):
    hd = y.shape[1]
    q = hd // 4
    lane = lax.broadcasted_iota(jnp.int32, y.shape, 1)
    return jnp.where((lane % (2 * q)) < q, pltpu.roll(y, hd - q, 1), pltpu.roll(y, q, 1))


def _qk_prep(qkv, qg, kg, cos, sin, n_q, ctx_rows, name, tr=256):
    T = qkv.shape[0]
    hd = qg.shape[1]
    n_kv = ATTN_KV_HEADS
    tr = min(tr, ctx_rows)

    def body(x_ref, qg_ref, kg_ref, c_ref, s_ref, q_ref, k_ref, v_ref):
        cv, sv = c_ref[...], s_ref[...]
        for h in range(n_q + n_kv):
            xh = x_ref[:, h * hd:(h + 1) * hd]
            r = lax.rsqrt(jnp.mean(xh * xh, axis=-1, keepdims=True) + NORM_EPS)
            y = (xh * r) * (qg_ref[...] if h < n_q else kg_ref[...])
            rot = y * cv + _partner(y) * sv
            if h < n_q:
                q_ref[:, h * hd:(h + 1) * hd] = rot.astype(q_ref.dtype)
            else:
                k_ref[:, (h - n_q) * hd:(h - n_q + 1) * hd] = rot.astype(k_ref.dtype)
        v_ref[...] = x_ref[:, (n_q + n_kv) * hd:].astype(v_ref.dtype)

    W = qkv.shape[1]
    return pl.pallas_call(
        body, name=name, grid=(T // tr,),
        in_specs=[pl.BlockSpec((tr, W), lambda i: (i, 0)), pl.BlockSpec((1, hd), lambda i: (0, 0)),
                  pl.BlockSpec((1, hd), lambda i: (0, 0)), pl.BlockSpec((tr, hd), lambda i: (i, 0)),
                  pl.BlockSpec((tr, hd), lambda i: (i, 0))],
        out_specs=[pl.BlockSpec((tr, n_q * hd), lambda i: (i, 0)), pl.BlockSpec((tr, n_kv * hd), lambda i: (i, 0)),
                   pl.BlockSpec((tr, n_kv * hd), lambda i: (i, 0))],
        out_shape=[jax.ShapeDtypeStruct((T, n_q * hd), MXU), jax.ShapeDtypeStruct((T, n_kv * hd), MXU),
                   jax.ShapeDtypeStruct((T, n_kv * hd), MXU)],
        compiler_params=_cparams(("parallel",), 6 * _nbytes((tr, W), F32)),
    )(qkv, qg, kg, cos, sin)


def _qk_prep_bwd(qkv, qg, kg, cos, sin, dq, dk, dv, n_q, ctx_rows, name, tr=256):
    T, W = qkv.shape
    hd = qg.shape[1]
    n_kv = ATTN_KV_HEADS
    tr = min(tr, ctx_rows)

    def body(x_ref, qg_ref, kg_ref, c_ref, s_ref, dq_ref, dk_ref, dv_ref, o_ref, g_ref):
        i = pl.program_id(0)
        cv, sv = c_ref[...], s_ref[...]
        gq = jnp.zeros((1, hd), F32)
        gk = jnp.zeros((1, hd), F32)
        for h in range(n_q + n_kv):
            xh = x_ref[:, h * hd:(h + 1) * hd]
            gain = qg_ref[...] if h < n_q else kg_ref[...]
            drot = (dq_ref[:, h * hd:(h + 1) * hd] if h < n_q else dk_ref[:, (h - n_q) * hd:(h - n_q + 1) * hd]).astype(F32)
            dy = drot * cv + _partner(drot * sv)
            r = lax.rsqrt(jnp.mean(xh * xh, axis=-1, keepdims=True) + NORM_EPS)
            xn = xh * r
            gsum = jnp.sum(dy * xn, axis=0, keepdims=True)
            if h < n_q:
                gq = gq + gsum
            else:
                gk = gk + gsum
            dxn = dy * gain
            o_ref[:, h * hd:(h + 1) * hd] = (r * (dxn - xn * jnp.mean(dxn * xn, axis=-1, keepdims=True))).astype(o_ref.dtype)
        o_ref[:, (n_q + n_kv) * hd:] = dv_ref[...].astype(o_ref.dtype)
        _acc_first(g_ref, jnp.concatenate([gq, gk, jnp.zeros((6, hd), F32)], axis=0), i)

    return pl.pallas_call(
        body, name=name, grid=(T // tr,),
        in_specs=[pl.BlockSpec((tr, W), lambda i: (i, 0)), pl.BlockSpec((1, hd), lambda i: (0, 0)),
                  pl.BlockSpec((1, hd), lambda i: (0, 0)), pl.BlockSpec((tr, hd), lambda i: (i, 0)),
                  pl.BlockSpec((tr, hd), lambda i: (i, 0)), pl.BlockSpec((tr, n_q * hd), lambda i: (i, 0)),
                  pl.BlockSpec((tr, n_kv * hd), lambda i: (i, 0)), pl.BlockSpec((tr, n_kv * hd), lambda i: (i, 0))],
        out_specs=[pl.BlockSpec((tr, W), lambda i: (i, 0)), pl.BlockSpec((8, hd), lambda i: (0, 0))],
        out_shape=[jax.ShapeDtypeStruct((T, W), MXU), jax.ShapeDtypeStruct((8, hd), F32)],
        compiler_params=_cparams(("arbitrary",), 8 * _nbytes((tr, W), F32)),
    )(qkv, qg, kg, cos, sin, dq, dk, dv)


def _attn_scores(q_ref, k_ref, sink_ref, h, qb, ctx_rows, nb, hd, grp):
    scale = hd ** -0.5
    w0 = jnp.clip(qb - 1, 0, nb - 3) * BLK
    w0 = pl.multiple_of(w0, BLK)
    qv = q_ref[...]
    qs = jnp.concatenate([qv[:, g * hd:(g + 1) * hd] for g in range(grp)], axis=0)
    kc = k_ref[0:ctx_rows, :]
    kb = k_ref[pl.ds(w0, 3 * BLK), :]
    s_c = _dot(qs, kc, NT) * scale
    s_b = _dot(qs, kb, NT) * scale
    n = grp * BLK
    qpos = qb * BLK + lax.broadcasted_iota(jnp.int32, (n, 3 * BLK), 0) % BLK
    kpos = w0 + lax.broadcasted_iota(jnp.int32, (n, 3 * BLK), 1)
    ok = (jnp.abs(kpos - qpos) <= ATTN_WINDOW) & (kpos >= ctx_rows) & (qpos >= ctx_rows)
    s_b = jnp.where(ok, s_b, -jnp.inf)
    gi = lax.broadcasted_iota(jnp.int32, (n, 1), 0) // BLK
    sink = jnp.zeros((n, 1), F32)
    for g in range(grp):
        sink = jnp.where(gi == g, sink_ref[h * grp + g], sink)
    m = jnp.maximum(jnp.maximum(jnp.max(s_c, axis=1, keepdims=True), jnp.max(s_b, axis=1, keepdims=True)), sink)
    e_c, e_b, e_s = jnp.exp(s_c - m), jnp.exp(s_b - m), jnp.exp(sink - m)
    inv = 1.0 / (jnp.sum(e_c, axis=1, keepdims=True) + jnp.sum(e_b, axis=1, keepdims=True) + e_s)
    return qs, kc, kb, w0, e_c * inv, e_b * inv, e_s * inv, gi


def _attn_fwd(qr, kr, vb, sinks, n_q, ctx_rows, name):
    T = qr.shape[0]
    n_kv = ATTN_KV_HEADS
    grp = n_q // n_kv
    hd = qr.shape[1] // n_q
    nb = T // BLK

    def body(sink_ref, q_ref, k_ref, v_ref, o_ref):
        h, qb = pl.program_id(0), pl.program_id(1)
        _, _, _, w0, p_c, p_b, _, _ = _attn_scores(q_ref, k_ref, sink_ref, h, qb, ctx_rows, nb, hd, grp)
        o = _dot(p_c.astype(MXU), v_ref[0:ctx_rows, :]) + _dot(p_b.astype(MXU), v_ref[pl.ds(w0, 3 * BLK), :])
        o_ref[...] = jnp.concatenate([o[g * BLK:(g + 1) * BLK] for g in range(grp)], axis=1).astype(o_ref.dtype)

    return pl.pallas_call(
        body, name=name, grid=(n_kv, nb),
        in_specs=[pl.BlockSpec(memory_space=pltpu.SMEM), pl.BlockSpec((BLK, grp * hd), lambda h, i: (i, h)),
                  pl.BlockSpec((T, hd), lambda h, i: (0, h)), pl.BlockSpec((T, hd), lambda h, i: (0, h))],
        out_specs=pl.BlockSpec((BLK, grp * hd), lambda h, i: (i, h)),
        out_shape=jax.ShapeDtypeStruct((T, n_q * hd), MXU),
        compiler_params=_cparams(("parallel", "arbitrary"), 4 * _nbytes((T, hd), MXU) + 24 * _nbytes((grp * BLK, 5 * BLK), F32)),
    )(sinks, qr, kr, vb)


def _attn_bwd(qr, kr, vb, sinks, do, n_q, ctx_rows, name):
    T = qr.shape[0]
    n_kv = ATTN_KV_HEADS
    grp = n_q // n_kv
    hd = qr.shape[1] // n_q
    nb = T // BLK
    scale = hd ** -0.5

    def body(sink_ref, q_ref, k_ref, v_ref, do_ref, dq_ref, dk_ref, dv_ref, ds_ref):
        h, qb = pl.program_id(0), pl.program_id(1)
        qs, kc, kb, w0, p_c, p_b, p_s, gi = _attn_scores(q_ref, k_ref, sink_ref, h, qb, ctx_rows, nb, hd, grp)
        dov = do_ref[...]
        dos = jnp.concatenate([dov[:, g * hd:(g + 1) * hd] for g in range(grp)], axis=0)
        vc = v_ref[0:ctx_rows, :]
        vw = v_ref[pl.ds(w0, 3 * BLK), :]
        dp_c = _dot(dos, vc, NT)
        dp_b = _dot(dos, vw, NT)
        delta = jnp.sum(p_c * dp_c, axis=1, keepdims=True) + jnp.sum(p_b * dp_b, axis=1, keepdims=True)
        ds_c = (p_c * (dp_c - delta) * scale).astype(MXU)
        ds_b = (p_b * (dp_b - delta) * scale).astype(MXU)
        dq = _dot(ds_c, kc) + _dot(ds_b, kb)
        dq_ref[...] = jnp.concatenate([dq[g * BLK:(g + 1) * BLK] for g in range(grp)], axis=1)

        @pl.when(qb == 0)
        def _():
            dk_ref[...] = jnp.zeros(dk_ref.shape, F32)
            dv_ref[...] = jnp.zeros(dv_ref.shape, F32)

        dk_ref[0:ctx_rows, :] += _dot(ds_c, qs, TN)
        dv_ref[0:ctx_rows, :] += _dot(p_c.astype(MXU), dos, TN)
        dk_ref[pl.ds(w0, 3 * BLK), :] += _dot(ds_b, qs, TN)
        dv_ref[pl.ds(w0, 3 * BLK), :] += _dot(p_b.astype(MXU), dos, TN)
        t = -(p_s * delta)
        lane = lax.broadcasted_iota(jnp.int32, (8, LANE), 1)
        part = jnp.zeros((8, LANE), F32)
        for g in range(grp):
            part = jnp.where(lane == g, jnp.sum(jnp.where(gi == g, t, 0.0)), part)
        _acc_first(ds_ref, part, qb)

    return pl.pallas_call(
        body, name=name, grid=(n_kv, nb),
        in_specs=[pl.BlockSpec(memory_space=pltpu.SMEM), pl.BlockSpec((BLK, grp * hd), lambda h, i: (i, h)),
                  pl.BlockSpec((T, hd), lambda h, i: (0, h)), pl.BlockSpec((T, hd), lambda h, i: (0, h)),
                  pl.BlockSpec((BLK, grp * hd), lambda h, i: (i, h))],
        out_specs=[pl.BlockSpec((BLK, grp * hd), lambda h, i: (i, h)), pl.BlockSpec((T, hd), lambda h, i: (0, h)),
                   pl.BlockSpec((T, hd), lambda h, i: (0, h)), pl.BlockSpec((8, LANE), lambda h, i: (h, 0))],
        out_shape=[jax.ShapeDtypeStruct((T, n_q * hd), F32), jax.ShapeDtypeStruct((T, n_kv * hd), F32),
                   jax.ShapeDtypeStruct((T, n_kv * hd), F32), jax.ShapeDtypeStruct((n_kv * 8, LANE), F32)],
        compiler_params=_cparams(("parallel", "arbitrary"), 4 * _nbytes((T, hd), MXU) + 4 * _nbytes((T, hd), F32)
                                 + 40 * _nbytes((grp * BLK, 5 * BLK), F32)),
    )(sinks, qr, kr, vb, do)


def _chunk_order(s, n_chunks, n_ctx, rev):
    if not rev:
        return s
    return jnp.where(s < n_ctx, n_ctx - 1 - s, n_chunks - 1 + n_ctx - s)


def _softplus(x):
    return jnp.maximum(x, 0.0) + jnp.log(1.0 + jnp.exp(-jnp.abs(x)))


def _expand_matrix(n_heads, p, rev):
    e = np.zeros((LANE, n_heads * p), np.float32)
    for h in range(n_heads):
        e[h + (n_heads if rev else 0), h * p:(h + 1) * p] = 1.0
    return jnp.asarray(e, BF16)


def _ssd_chunk_prep(dt_ref, dtb_ref, alog_ref, e_ref, rev):
    dt = _softplus(dt_ref[...] + dtb_ref[...])
    a = -jnp.exp(alog_ref[...])
    li = lax.broadcasted_iota(jnp.int32, (BLK, BLK), 0)
    si = lax.broadcasted_iota(jnp.int32, (BLK, BLK), 1)
    tri = (si >= li) if rev else (si <= li)
    acs = _dot3_rhs(tri.astype(BF16), a * dt)
    dtexp, aexp = _expand2([dt, acs], e_ref[...])
    return dt, a, tri, acs, dtexp, aexp


def _pair_cols(ap):
    lane = lax.broadcasted_iota(jnp.int32, ap.shape, 1)
    apr = pltpu.roll(ap, LANE // 2, 1)
    return jnp.where(lane < LANE // 2, ap, apr), jnp.where(lane < LANE // 2, apr, ap)


def _ssd_scan(xbc, zx, dtb, alog, emat, n_heads, n_ctx, rev, name):
    T, C = xbc.shape
    P = emat.shape[1] // n_heads
    DI = n_heads * P
    GN = (C - DI) // 2
    N = GN // SSD_GROUPS
    n_pairs = DI // LANE
    ppg = n_pairs // SSD_GROUPS
    n_chunks = T // BLK
    hoff = n_heads if rev else 0
    last = 0 if rev else BLK - 1
    dt_blk = zx.shape[1] // LANE - 1
    assert N == LANE and 2 * P == LANE and 2 * n_heads == LANE

    def body(xs_ref, b_ref, c_ref, dt_ref, dtb_ref, alog_ref, e_ref, y_ref, hin_ref, state_ref, xdt_s, xdec_s, aexp_s, at_s):
        s = pl.program_id(0)

        @pl.when(s == 0)
        def _():
            state_ref[...] = jnp.zeros(state_ref.shape, F32)

        dt, a, tri, acs, dtexp, aexp = _ssd_chunk_prep(dt_ref, dtb_ref, alog_ref, e_ref, rev)
        at_s[...] = acs.T
        aexp_s[...] = aexp
        xdt = xs_ref[...] * dtexp
        xdt_s[...] = xdt.astype(MXU)
        xdec_s[...] = (xdt * jnp.exp(aexp[last:last + 1, :] - aexp)).astype(MXU)
        hin_ref[0] = state_ref[...]
        lane = lax.broadcasted_iota(jnp.int32, (BLK, LANE), 1)

        def pair(k, carry):
            col = pl.multiple_of(k * LANE, LANE)
            gcol = pl.multiple_of((k // ppg) * N, N)
            bg = b_ref[:, pl.ds(gcol, N)].astype(MXU)
            cg = c_ref[:, pl.ds(gcol, N)].astype(MXU)
            cb = _dot(cg, bg, NT)
            ap = aexp_s[:, pl.ds(col, LANE)]
            ac0, ac1 = _pair_cols(ap)
            ar0 = at_s[pl.ds(2 * k + hoff, 1), :]
            ar1 = at_s[pl.ds(2 * k + 1 + hoff, 1), :]
            m0 = (cb * jnp.exp(jnp.where(tri, ac0 - ar0, -jnp.inf))).astype(MXU)
            m1 = (cb * jnp.exp(jnp.where(tri, ac1 - ar1, -jnp.inf))).astype(MXU)
            xp = xdt_s[:, pl.ds(col, LANE)]
            zero = jnp.zeros_like(xp)
            xbd = jnp.concatenate([jnp.where(lane < LANE // 2, xp, zero), jnp.where(lane >= LANE // 2, xp, zero)], axis=0)
            yd = _dot(jnp.concatenate([m0, m1], axis=1), xbd)
            ht = state_ref[k]
            yo = _dot(cg, ht.astype(MXU)) * jnp.exp(ap)
            y_ref[:, pl.ds(col, LANE)] = yd + yo
            st = _dot(bg, xdec_s[:, pl.ds(col, LANE)], TN)
            state_ref[k] = jnp.exp(aexp_s[pl.ds(last, 1), pl.ds(col, LANE)]) * ht + st
            return carry

        lax.fori_loop(0, n_pairs, pair, 0, unroll=4)

    order = lambda s: _chunk_order(s, n_chunks, n_ctx, rev)
    return pl.pallas_call(
        body, name=name, grid=(n_chunks,),
        in_specs=[pl.BlockSpec((BLK, DI), lambda s: (order(s), 0)),
                  pl.BlockSpec((BLK, GN), lambda s: (order(s), DI // GN)),
                  pl.BlockSpec((BLK, GN), lambda s: (order(s), DI // GN + 1)),
                  pl.BlockSpec((BLK, LANE), lambda s: (order(s), dt_blk)),
                  pl.BlockSpec((1, LANE), lambda s: (0, 0)), pl.BlockSpec((1, LANE), lambda s: (0, 0)),
                  pl.BlockSpec((2 * LANE, DI), lambda s: (0, 0))],
        out_specs=[pl.BlockSpec((BLK, DI), lambda s: (order(s), 0)),
                   pl.BlockSpec((1, n_pairs, N, LANE), lambda s: (order(s), 0, 0, 0))],
        out_shape=[jax.ShapeDtypeStruct((T, DI), F32), jax.ShapeDtypeStruct((n_chunks, n_pairs, N, LANE), F32)],
        scratch_shapes=[pltpu.VMEM((n_pairs, N, LANE), F32), pltpu.VMEM((BLK, DI), MXU), pltpu.VMEM((BLK, DI), MXU),
                        pltpu.VMEM((BLK, DI), F32), pltpu.VMEM((LANE, BLK), F32)],
        compiler_params=_cparams(("arbitrary",), 20 * _nbytes((BLK, DI), F32)),
    )(xbc, xbc, xbc, zx, dtb, alog, jnp.concatenate([emat, emat], axis=0))


def _ssd_scan_bwd(xbc, zx, dtb, alog, emat, emat_t, dexp, hin, dy, acc, n_heads, n_ctx, rev, name):
    T, C = xbc.shape
    P = emat.shape[1] // n_heads
    DI = n_heads * P
    GN = (C - DI) // 2
    N = GN // SSD_GROUPS
    n_pairs = DI // LANE
    ppg = n_pairs // SSD_GROUPS
    n_chunks = T // BLK
    hoff = n_heads if rev else 0
    last = 0 if rev else BLK - 1
    dt_blk = zx.shape[1] // LANE - 1
    has_acc = acc is not None

    def body(*refs):
        (xs_ref, b_ref, c_ref, dt_ref, dtb_ref, alog_ref, e_ref, et_ref, dexp_ref, hin_ref, dy_ref) = refs[:11]
        n_in = 11
        if has_acc:
            dxbc_in, ddt_in, sums_in = refs[11:14]
            n_in = 14
        dxbc_ref, ddt_ref, sums_ref = refs[n_in:n_in + 3]
        dstate_ref, xdt_s, xdec_s, aexp_s, at_s, dyd_s, z1_s, z3_s, dxdt_s, cdrow_s, rmat_s, cst_s = refs[n_in + 3:]
        s = pl.program_id(0)

        @pl.when(s == 0)
        def _():
            dstate_ref[...] = jnp.zeros(dstate_ref.shape, F32)

        dt, a, tri, acs, dtexp, aexp = _ssd_chunk_prep(dt_ref, dtb_ref, alog_ref, e_ref, rev)
        at_s[...] = acs.T
        aexp_s[...] = aexp
        xsv = xs_ref[...]
        xdt = xsv * dtexp
        xdt_s[...] = xdt.astype(MXU)
        decend = jnp.exp(aexp[last:last + 1, :] - aexp)
        xdec_s[...] = (xdt * decend).astype(MXU)
        dyv = dy_ref[...]
        dyd_s[...] = (dyv * jnp.exp(aexp)).astype(MXU)
        dxbc_ref[:, DI:] = jnp.zeros((BLK, 2 * GN), F32)
        rmat_s[...] = jnp.zeros(rmat_s.shape, F32)
        lane = lax.broadcasted_iota(jnp.int32, (BLK, LANE), 1)
        lo = lane < LANE // 2
        tri_t = (lax.broadcasted_iota(jnp.int32, (BLK, BLK), 0) <= lax.broadcasted_iota(jnp.int32, (BLK, BLK), 1)) if not rev \
            else (lax.broadcasted_iota(jnp.int32, (BLK, BLK), 0) >= lax.broadcasted_iota(jnp.int32, (BLK, BLK), 1))

        def pair(k, carry):
            col = pl.multiple_of(k * LANE, LANE)
            gcol = pl.multiple_of((k // ppg) * N, N)
            bg = b_ref[:, pl.ds(gcol, N)].astype(MXU)
            cg = c_ref[:, pl.ds(gcol, N)].astype(MXU)
            cb = _dot(cg, bg, NT)
            cbt = _dot(bg, cg, NT)
            ap = aexp_s[:, pl.ds(col, LANE)]
            ac0, ac1 = _pair_cols(ap)
            ar0 = at_s[pl.ds(2 * k + hoff, 1), :]
            ar1 = at_s[pl.ds(2 * k + 1 + hoff, 1), :]
            seg0 = jnp.exp(jnp.where(tri, ac0 - ar0, -jnp.inf))
            seg1 = jnp.exp(jnp.where(tri, ac1 - ar1, -jnp.inf))
            segt0 = jnp.exp(jnp.where(tri_t, ar0 - ac0, -jnp.inf))
            segt1 = jnp.exp(jnp.where(tri_t, ar1 - ac1, -jnp.inf))
            dyp = dy_ref[:, pl.ds(col, LANE)].astype(MXU)
            zero = jnp.zeros_like(dyp)
            dy0, dy1 = jnp.where(lo, dyp, zero), jnp.where(lo, zero, dyp)
            dht = dstate_ref[k]
            dhb = dht.astype(MXU)
            ht = hin_ref[0, k]
            mt = jnp.concatenate([(cbt * segt0).astype(MXU), (cbt * segt1).astype(MXU)], axis=1)
            bdh = _dot(bg, dhb)
            dec_p = jnp.exp(aexp_s[pl.ds(last, 1), pl.ds(col, LANE)] - ap)
            dxdt_s[:, pl.ds(col, LANE)] = _dot(mt, jnp.concatenate([dy0, dy1], axis=0)) + dec_p * bdh
            z3_s[:, pl.ds(col, LANE)] = bdh
            cdec = jnp.exp(aexp_s[pl.ds(last, 1), pl.ds(col, LANE)])
            dydp = dyd_s[:, pl.ds(col, LANE)]
            dstate_ref[k] = _dot(cg, dydp, TN) + cdec * dht
            cdrow_s[0:1, pl.ds(col, LANE)] = cdec * jnp.sum(dht * ht, axis=0, keepdims=True)
            z1_s[:, pl.ds(col, LANE)] = _dot(cg, ht.astype(MXU))
            dcg = _dot(dydp, ht.astype(MXU), NT)
            dbg = _dot(xdec_s[:, pl.ds(col, LANE)], dhb, NT)
            xp = xdt_s[:, pl.ds(col, LANE)]
            dg0 = _dot(dy0, xp, NT)
            dg1 = _dot(dy1, xp, NT)
            ds0, ds1 = dg0 * seg0, dg1 * seg1
            w0, w1 = ds0 * cb, ds1 * cb
            dcb = ds0 + ds1
            lane_h = lax.broadcasted_iota(jnp.int32, (BLK, LANE), 1)
            rmat_s[...] += (jnp.where(lane_h == 2 * k + hoff, jnp.sum(w0, axis=1, keepdims=True), 0.0)
                            + jnp.where(lane_h == 2 * k + 1 + hoff, jnp.sum(w1, axis=1, keepdims=True), 0.0))
            cst_s[pl.ds(2 * k + hoff, 1), :] = jnp.sum(w0, axis=0, keepdims=True)
            cst_s[pl.ds(2 * k + 1 + hoff, 1), :] = jnp.sum(w1, axis=0, keepdims=True)
            dcbb = dcb.astype(MXU)
            dxbc_ref[:, pl.ds(DI + GN + gcol, N)] += dcg + _dot(dcbb, bg)
            dxbc_ref[:, pl.ds(DI + gcol, N)] += dbg + _dot(dcbb, cg, TN)
            return carry

        cst_s[...] = jnp.zeros(cst_s.shape, F32)
        lax.fori_loop(0, n_pairs, pair, 0, unroll=4)

        etv = et_ref[...]
        dxdt = dxdt_s[...]
        qfull = xdt * decend * z3_s[...]
        tot8 = jnp.concatenate([jnp.sum(qfull, axis=0, keepdims=True) + cdrow_s[0:1, :], jnp.zeros((7, DI), F32)], axis=0)
        z1q, z2, tot = _dot2_stacked([dyv * (z1_s[...] * jnp.exp(aexp)) - qfull, dxdt * xsv, tot8], etv)
        dacs = rmat_s[...] - cst_s[...].T + z1q
        rowi = lax.broadcasted_iota(jnp.int32, (BLK, LANE), 0)
        dacs = dacs + jnp.where(rowi == last, tot[0:1, :], 0.0)
        d_a = _dot3_rhs(tri_t.astype(BF16), dacs)
        ddt = a * d_a + z2
        x_raw = dt_ref[...] + dtb_ref[...]
        ddt_raw = ddt * jax.nn.sigmoid(x_raw)
        lane_l = lax.broadcasted_iota(jnp.int32, (BLK, LANE), 1)
        mine = (lane_l >= hoff) & (lane_l < hoff + n_heads)
        ddt_raw = jnp.where(mine, ddt_raw, 0.0)
        part = jnp.concatenate([jnp.sum(jnp.where(mine, dt * d_a, 0.0), axis=0, keepdims=True) * a,
                                jnp.sum(ddt_raw, axis=0, keepdims=True), jnp.zeros((6, LANE), F32)], axis=0)
        dxs = dxdt * dtexp
        if has_acc:
            dxbc_ref[:, 0:DI] = dxs + dxbc_in[:, 0:DI]
            dxbc_ref[:, DI:] += dxbc_in[:, DI:]
            ddt_ref[...] = ddt_raw + ddt_in[...]
            part = part + jnp.where(s == 0, sums_in[...], 0.0)
        else:
            dxbc_ref[:, 0:DI] = dxs + dyv * dexp_ref[...]
            ddt_ref[...] = ddt_raw
        _acc_first(sums_ref, part, s)

    order = lambda s: _chunk_order(n_chunks - 1 - s, n_chunks, n_ctx, rev)
    in_specs = [pl.BlockSpec((BLK, DI), lambda s: (order(s), 0)),
                pl.BlockSpec((BLK, GN), lambda s: (order(s), DI // GN)),
                pl.BlockSpec((BLK, GN), lambda s: (order(s), DI // GN + 1)),
                pl.BlockSpec((BLK, LANE), lambda s: (order(s), dt_blk)),
                pl.BlockSpec((1, LANE), lambda s: (0, 0)), pl.BlockSpec((1, LANE), lambda s: (0, 0)),
                pl.BlockSpec((2 * LANE, DI), lambda s: (0, 0)), pl.BlockSpec((DI, LANE), lambda s: (0, 0)),
                pl.BlockSpec((1, DI), lambda s: (0, 0)),
                pl.BlockSpec((1, n_pairs, N, LANE), lambda s: (order(s), 0, 0, 0)),
                pl.BlockSpec((BLK, DI), lambda s: (order(s), 0))]
    args = [xbc, xbc, xbc, zx, dtb, alog, jnp.concatenate([emat, emat], axis=0), emat_t, dexp, hin, dy]
    if has_acc:
        in_specs += [pl.BlockSpec((BLK, C), lambda s: (order(s), 0)), pl.BlockSpec((BLK, LANE), lambda s: (order(s), 0)),
                     pl.BlockSpec((8, LANE), lambda s: (0, 0))]
        args += list(acc)
    return pl.pallas_call(
        body, name=name, grid=(n_chunks,),
        in_specs=in_specs,
        out_specs=[pl.BlockSpec((BLK, C), lambda s: (order(s), 0)), pl.BlockSpec((BLK, LANE), lambda s: (order(s), 0)),
                   pl.BlockSpec((8, LANE), lambda s: (0, 0))],
        out_shape=[jax.ShapeDtypeStruct((T, C), F32), jax.ShapeDtypeStruct((T, LANE), F32), jax.ShapeDtypeStruct((8, LANE), F32)],
        scratch_shapes=[pltpu.VMEM((n_pairs, N, LANE), F32), pltpu.VMEM((BLK, DI), MXU), pltpu.VMEM((BLK, DI), MXU),
                        pltpu.VMEM((BLK, DI), F32), pltpu.VMEM((LANE, BLK), F32), pltpu.VMEM((BLK, DI), MXU),
                        pltpu.VMEM((BLK, DI), F32), pltpu.VMEM((BLK, DI), F32), pltpu.VMEM((BLK, DI), F32),
                        pltpu.VMEM((8, DI), F32), pltpu.VMEM((BLK, LANE), F32), pltpu.VMEM((LANE, BLK), F32)],
        compiler_params=_cparams(("arbitrary",), 36 * _nbytes((BLK, DI), F32)),
    )(*args)


def _ssd_finish(yf, yb, xbc, zx, dexp, nw, name, tr=256):
    T, DI = yf.shape
    tr = min(tr, T)

    def body(yf_ref, yb_ref, xs_ref, z_ref, d_ref, nw_ref, o_ref):
        y = yf_ref[...] + yb_ref[...] + xs_ref[...] * d_ref[...]
        gt = y * _silu(z_ref[...])
        r = lax.rsqrt(jnp.mean(gt * gt, axis=-1, keepdims=True) + NORM_EPS)
        o_ref[...] = ((gt * r) * nw_ref[...]).astype(o_ref.dtype)

    row = pl.BlockSpec((tr, DI), lambda i: (i, 0))
    vec = pl.BlockSpec((1, DI), lambda i: (0, 0))
    return pl.pallas_call(
        body, name=name, grid=(T // tr,), in_specs=[row, row, row, row, vec, vec], out_specs=row,
        out_shape=jax.ShapeDtypeStruct((T, DI), MXU),
        compiler_params=_cparams(("parallel",), 16 * _nbytes((tr, DI), F32)),
    )(yf, yb, xbc, zx, dexp, nw)


def _ssd_finish_bwd(yf, yb, xbc, zx, dexp, nw, do, emat_t, dzx_shape, name, tr=64):
    T, DI = yf.shape
    tr = min(tr, T)
    n_steps = T // tr

    def body(yf_ref, yb_ref, xs_ref, z_ref, d_ref, nw_ref, do_ref, et_ref, dy_ref, dz_ref, sums_ref, dd_ref):
        i = pl.program_id(0)
        xs = xs_ref[...]
        zv = z_ref[...]
        y = yf_ref[...] + yb_ref[...] + xs * d_ref[...]
        sz = _silu(zv)
        gt = y * sz
        r = lax.rsqrt(jnp.mean(gt * gt, axis=-1, keepdims=True) + NORM_EPS)
        gn = gt * r
        dov = do_ref[...].astype(F32)
        dgn = dov * nw_ref[...]
        dgt = r * (dgn - gn * jnp.mean(dgn * gn, axis=-1, keepdims=True))
        dy = dgt * sz
        dy_ref[...] = dy
        dz_ref[...] = (dgt * y * _dsilu(zv)).astype(dz_ref.dtype)
        part = jnp.concatenate([jnp.sum(dov * gn, axis=0, keepdims=True), jnp.sum(dy * xs, axis=0, keepdims=True),
                                jnp.zeros((6, DI), F32)], axis=0)
        _acc_first(sums_ref, part, i)

        @pl.when(i == n_steps - 1)
        def _():
            dd_ref[...] = _dot3_lhs(sums_ref[...], et_ref[...])

    row = pl.BlockSpec((tr, DI), lambda i: (i, 0))
    vec = pl.BlockSpec((1, DI), lambda i: (0, 0))
    return pl.pallas_call(
        body, name=name, grid=(n_steps,),
        in_specs=[row, row, row, row, vec, vec, row, pl.BlockSpec((DI, LANE), lambda i: (0, 0))],
        out_specs=[row, row, pl.BlockSpec((8, DI), lambda i: (0, 0)), pl.BlockSpec((8, LANE), lambda i: (0, 0))],
        out_shape=[jax.ShapeDtypeStruct((T, DI), F32), jax.ShapeDtypeStruct(dzx_shape, MXU), jax.ShapeDtypeStruct((8, DI), F32),
                   jax.ShapeDtypeStruct((8, LANE), F32)],
        compiler_params=_cparams(("arbitrary",), 40 * _nbytes((tr, DI), F32)),
    )(yf, yb, xbc, zx, dexp, nw, do, emat_t)


def _loss_head(xf, tgt, ctx_rows, name, tr=256):
    T, D = xf.shape
    tr = min(tr, ctx_rows)
    n_ctx = ctx_rows // tr

    def body(x_ref, t_ref, dx_ref, l_ref):
        i = pl.program_id(0)

        @pl.when(i < n_ctx)
        def _():
            dx_ref[...] = jnp.zeros(dx_ref.shape, F32)

        @pl.when(i == 0)
        def _():
            l_ref[...] = jnp.zeros(l_ref.shape, F32)

        @pl.when(i >= n_ctx)
        def _():
            e = x_ref[...] - t_ref[...]
            dx_ref[...] = e * (1.0 / D)
            l_ref[...] += 0.5 * jnp.sum(jnp.mean(e * e, axis=-1, keepdims=True))

    return pl.pallas_call(
        body, name=name, grid=(T // tr,),
        in_specs=[pl.BlockSpec((tr, D), lambda i: (i, 0)), pl.BlockSpec((tr, D), lambda i: (jnp.maximum(i - n_ctx, 0), 0))],
        out_specs=[pl.BlockSpec((tr, D), lambda i: (i, 0)), pl.BlockSpec((8, LANE), lambda i: (0, 0))],
        out_shape=[jax.ShapeDtypeStruct((T, D), F32), jax.ShapeDtypeStruct((8, LANE), F32)],
        compiler_params=_cparams(("arbitrary",), 10 * _nbytes((tr, D), F32)),
    )(xf, tgt)


def _adamw_math(w, g, m, v):
    m2 = ADAM_B1 * m + (1.0 - ADAM_B1) * g
    v2 = ADAM_B2 * v + (1.0 - ADAM_B2) * (g * g)
    m_hat = m2 / (1.0 - ADAM_B1 ** ADAM_STEP)
    v_hat = v2 / (1.0 - ADAM_B2 ** ADAM_STEP)
    delta = -ADAM_LR * (m_hat / (jnp.sqrt(v_hat) + ADAM_EPS) + ADAM_WD * w)
    return delta, m2, v2


def _row_tile(rows, target):
    if rows <= target:
        return rows
    t = target - target % 8
    while rows % t:
        t -= 8
    return t


def _adamw(parts, w, m, v, name, tr=128):
    n, L, rows, cols = parts.shape
    tr = _row_tile(rows, tr)

    def body(p_ref, w_ref, m_ref, v_ref, g_ref, d_ref, m2_ref, v2_ref):
        g = p_ref[0, 0].astype(F32)
        for q in range(1, n):
            g = g + p_ref[q, 0].astype(F32)
        d, m2, v2 = _adamw_math(w_ref[0], g, m_ref[0], v_ref[0])
        g_ref[0], d_ref[0], m2_ref[0], v2_ref[0] = g, d, m2, v2

    blk = pl.BlockSpec((1, tr, cols), lambda l, i: (l, i, 0))
    shp = jax.ShapeDtypeStruct((L, rows, cols), F32)
    return pl.pallas_call(
        body, name=name, grid=(L, rows // tr),
        in_specs=[pl.BlockSpec((n, 1, tr, cols), lambda l, i: (0, l, i, 0)), blk, blk, blk],
        out_specs=[blk, blk, blk, blk], out_shape=[shp, shp, shp, shp],
        compiler_params=_cparams(("parallel", "parallel"), 2 * (n + 8) * _nbytes((tr, cols), F32)),
    )(parts, w, m, v)


def _adamw_small(bufs, w, m, v, name):
    n, R, _ = bufs.shape

    def body(b_ref, w_ref, m_ref, v_ref, g_ref, d_ref, m2_ref, v2_ref):
        g = b_ref[0]
        for q in range(1, n):
            g = g + b_ref[q]
        d, m2, v2 = _adamw_math(w_ref[...], g, m_ref[...], v_ref[...])
        g_ref[...], d_ref[...], m2_ref[...], v2_ref[...] = g, d, m2, v2

    vm = pl.BlockSpec(memory_space=pltpu.VMEM)
    shp = jax.ShapeDtypeStruct((R, LANE), F32)
    return pl.pallas_call(body, name=name, in_specs=[vm, vm, vm, vm], out_specs=[vm, vm, vm, vm],
                          out_shape=[shp, shp, shp, shp],
                          compiler_params=pltpu.CompilerParams(vmem_limit_bytes=32 * 1024 * 1024))(bufs, w, m, v)


def _add_blocks(g, r1, c_idx, name, tr=128):
    _, rows, cols = g.shape
    tr = _row_tile(rows, tr)

    def body(c_ref, g_ref, r_ref, p_ref):
        del c_ref
        p_ref[...] = (g_ref[...] + r_ref[...]).astype(p_ref.dtype)

    return pl.pallas_call(
        body, name=name,
        grid_spec=pltpu.PrefetchScalarGridSpec(
            num_scalar_prefetch=1, grid=(NDEV // 2, rows // tr),
            in_specs=[pl.BlockSpec((1, tr, cols), lambda q, i, c: (2 * q + c[0], i, 0)),
                      pl.BlockSpec((1, tr, cols), lambda q, i, c: (q, i, 0))],
            out_specs=pl.BlockSpec((1, tr, cols), lambda q, i, c: (q, i, 0))),
        out_shape=jax.ShapeDtypeStruct((NDEV // 2, rows, cols), XFER),
        compiler_params=_cparams(("parallel", "parallel"), 8 * _nbytes((tr, cols), F32)),
    )(c_idx, g, r1)


def _me():
    return lax.axis_index("x"), lax.axis_index("y"), lax.axis_index("c")


def _flip(v, bit):
    return 1 - v if bit else v


def _peer(k):
    x, y, c = _me()
    return _flip(x, (k >> 2) & 1), _flip(y, (k >> 1) & 1), _flip(c, k & 1)


def _dev_index(p):
    return 4 * p[0] + 2 * p[1] + p[2]


def _chip_index(p):
    return 2 * p[0] + p[1]


def _small_allgather(v, name):
    R, C = v.shape

    def body(v_ref, out_ref, send_sems, recv_sems, loc_sem):
        me = _dev_index(_me())
        mine = pltpu.make_async_copy(v_ref, out_ref.at[me], loc_sem)
        mine.start()
        sends = []
        for k in range(1, NDEV):
            cp = pltpu.make_async_remote_copy(src_ref=v_ref, dst_ref=out_ref.at[me], send_sem=send_sems.at[k - 1],
                                              recv_sem=recv_sems.at[k - 1], device_id=_peer(k), device_id_type=MESH_ID)
            cp.start()
            sends.append(cp)
        for k in range(1, NDEV):
            pltpu.make_async_remote_copy(src_ref=v_ref, dst_ref=out_ref.at[_dev_index(_peer(k))], send_sem=send_sems.at[k - 1],
                                         recv_sem=recv_sems.at[k - 1], device_id=_peer(k), device_id_type=MESH_ID).wait_recv()
        for cp in sends:
            cp.wait_send()
        mine.wait()

    vm = pl.BlockSpec(memory_space=pltpu.VMEM)
    return pl.pallas_call(
        body, name=name, in_specs=[vm], out_specs=vm, out_shape=jax.ShapeDtypeStruct((NDEV, R, C), F32),
        scratch_shapes=[pltpu.SemaphoreType.DMA((NDEV - 1,)), pltpu.SemaphoreType.DMA((NDEV - 1,)), pltpu.SemaphoreType.DMA(())],
        compiler_params=pltpu.CompilerParams(vmem_limit_bytes=48 * 1024 * 1024),
    )(v)


def _allgather_weights(arrs, name):
    n = len(arrs)

    def body(*refs):
        x_refs, out_refs = refs[:n], refs[n:2 * n]
        send_sems, recv_sems, loc_sems = refs[2 * n:]
        x, y, c = _me()
        me, sib = (x, y, c), (x, y, 1 - c)
        chips = [(1 - x, y), (x, 1 - y), (1 - x, 1 - y)]

        def copy(a, k, block, to, src=None):
            dst = out_refs[a].at[_dev_index(block)]
            return pltpu.make_async_remote_copy(src_ref=dst if src is None else src, dst_ref=dst, send_sem=send_sems.at[a, k],
                                                recv_sem=recv_sems.at[a, k], device_id=to, device_id_type=MESH_ID)

        locs, first, passed = [], [], []
        for a in range(n):
            lc = pltpu.make_async_copy(x_refs[a], out_refs[a].at[_dev_index(me)], loc_sems.at[a])
            lc.start()
            locs.append(lc)
            cps = [copy(a, 0, me, sib, src=x_refs[a])] + [copy(a, 1 + j, me, (*chip, c), src=x_refs[a]) for j, chip in enumerate(chips)]
            for cp in cps:
                cp.start()
            first += cps
        for j, chip in enumerate(chips):
            for a in range(n):
                copy(a, 1 + j, (*chip, c), me).wait_recv()
                cp = copy(a, 4 + j, (*chip, c), sib)
                cp.start()
                passed.append(cp)
        for a in range(n):
            copy(a, 0, sib, me).wait_recv()
            for j, chip in enumerate(chips):
                copy(a, 4 + j, (*chip, 1 - c), me).wait_recv()
        for cp in first + passed:
            cp.wait_send()
        for lc in locs:
            lc.wait()

    hbm = pl.BlockSpec(memory_space=pl.ANY)
    return pl.pallas_call(
        body, name=name, in_specs=[hbm] * n, out_specs=[hbm] * n,
        out_shape=[jax.ShapeDtypeStruct((NDEV,) + a.shape, a.dtype) for a in arrs],
        scratch_shapes=[pltpu.SemaphoreType.DMA((n, 7)), pltpu.SemaphoreType.DMA((n, 7)), pltpu.SemaphoreType.DMA((n,))],
    )(*arrs)


def _rs_sibling(gs, name):
    n = len(gs)

    def body(*refs):
        g_refs, r_refs = refs[:n], refs[n:2 * n]
        send_sems, recv_sems = refs[2 * n:]
        x, y, c = _me()
        sib = (x, y, 1 - c)
        sends = []
        for a in range(n):
            for q in range(NDEV // 2):
                cp = pltpu.make_async_remote_copy(src_ref=g_refs[a].at[2 * q + (1 - c)], dst_ref=r_refs[a].at[q],
                                                  send_sem=send_sems.at[a, q], recv_sem=recv_sems.at[a, q],
                                                  device_id=sib, device_id_type=MESH_ID)
                cp.start()
                sends.append(cp)
        for cp in sends:
            cp.wait_recv()
        for cp in sends:
            cp.wait_send()

    hbm = pl.BlockSpec(memory_space=pl.ANY)
    return pl.pallas_call(
        body, name=name, in_specs=[hbm] * n, out_specs=[hbm] * n,
        out_shape=[jax.ShapeDtypeStruct((NDEV // 2,) + g.shape[1:], g.dtype) for g in gs],
        scratch_shapes=[pltpu.SemaphoreType.DMA((n, NDEV // 2)), pltpu.SemaphoreType.DMA((n, NDEV // 2))],
    )(*gs)


def _rs_chips(ps, groups, name):
    n = len(ps)
    where = {}
    for gi, grp in enumerate(groups):
        for li, a in enumerate(grp):
            where[a] = (gi, li)
    ng = len(groups)

    def body(*refs):
        p_refs, r_refs = refs[:n], refs[n:n + ng]
        send_sems, recv_sems, loc_sems = refs[n + ng:]
        x, y, c = _me()
        mychip = _chip_index((x, y))
        chips = [(1 - x, y), (x, 1 - y), (1 - x, 1 - y)]
        sends, locs = [], []
        for a in range(n):
            gi, li = where[a]
            lc = pltpu.make_async_copy(p_refs[a].at[mychip], r_refs[gi].at[mychip, li], loc_sems.at[a])
            lc.start()
            locs.append(lc)
            for j, chip in enumerate(chips):
                cp = pltpu.make_async_remote_copy(src_ref=p_refs[a].at[_chip_index(chip)], dst_ref=r_refs[gi].at[mychip, li],
                                                  send_sem=send_sems.at[a, j], recv_sem=recv_sems.at[a, j],
                                                  device_id=(*chip, c), device_id_type=MESH_ID)
                cp.start()
                sends.append(cp)
        for a in range(n):
            gi, li = where[a]
            for j, chip in enumerate(chips):
                pltpu.make_async_remote_copy(src_ref=p_refs[a].at[mychip], dst_ref=r_refs[gi].at[_chip_index(chip), li],
                                             send_sem=send_sems.at[a, j], recv_sem=recv_sems.at[a, j],
                                             device_id=(*chip, c), device_id_type=MESH_ID).wait_recv()
        for cp in sends:
            cp.wait_send()
        for lc in locs:
            lc.wait()

    hbm = pl.BlockSpec(memory_space=pl.ANY)
    return pl.pallas_call(
        body, name=name, in_specs=[hbm] * n, out_specs=[hbm] * ng,
        out_shape=[jax.ShapeDtypeStruct((NDEV // 2, len(grp)) + ps[grp[0]].shape[1:], ps[grp[0]].dtype) for grp in groups],
        scratch_shapes=[pltpu.SemaphoreType.DMA((n, 3)), pltpu.SemaphoreType.DMA((n, 3)), pltpu.SemaphoreType.DMA((n,))],
    )(*ps)


HBM_SPEC = pl.BlockSpec(memory_space=pltpu.HBM)
SEM_SPEC = pl.BlockSpec(memory_space=pltpu.SEMAPHORE)
DATAFLOW = pltpu.SideEffectType.DATAFLOW_SIDE_EFFECTING


def _xfer_copy(src_ref, land_ref, sems, a, k, layer, scatter, arriving):
    send_sems, recv_sems = sems
    me, peer = _dev_index(_me()), _dev_index(_peer(k))
    src = src_ref.at[peer] if scatter else src_ref
    slot = peer if arriving else me
    dst = land_ref.at[slot] if layer is None else land_ref.at[slot, layer]
    return pltpu.make_async_remote_copy(src_ref=src, dst_ref=dst, send_sem=send_sems.at[a * (NDEV - 1) + k - 1],
                                        recv_sem=recv_sems.at[a * (NDEV - 1) + k - 1], device_id=_peer(k),
                                        device_id_type=MESH_ID)


def _xfer_start(srcs, lands, layers, scatter, name):
    n = len(srcs)

    def body(*refs):
        src_refs, land_refs = refs[:n], refs[n:2 * n]
        sems = refs[2 * n], refs[2 * n + 1]
        token = refs[-1]
        for a in range(n):
            for k in range(1, NDEV):
                _xfer_copy(src_refs[a], land_refs[a], sems, a, k, layers[a], scatter, False).start()
        token[...] = jnp.zeros(token.shape, token.dtype)

    ops = [pltpu.with_memory_space_constraint(t, pltpu.HBM) for t in list(srcs) + list(lands)]
    n_sem = n * (NDEV - 1)
    res = pl.pallas_call(
        body, name=name,
        out_shape=(pltpu.SemaphoreType.DMA((n_sem,)), pltpu.SemaphoreType.DMA((n_sem,)),
                   *[pltpu.HBM(t.shape, t.dtype) for t in ops], jax.ShapeDtypeStruct((8, LANE), F32)),
        in_specs=[HBM_SPEC] * (2 * n),
        out_specs=(SEM_SPEC, SEM_SPEC, *[HBM_SPEC] * (2 * n), pl.BlockSpec(memory_space=pltpu.VMEM)),
        input_output_aliases={i: 2 + i for i in range(2 * n)},
        compiler_params=pltpu.CompilerParams(has_side_effects=DATAFLOW),
    )(*ops)
    return res[0], res[1], list(res[2:2 + n]), list(res[2 + n:2 + 2 * n]), res[-1]


def _xfer_wait(send_sems, recv_sems, srcs, lands, layers, scatter, after, name):
    n = len(srcs)

    def body(*refs):
        src_refs, land_refs = refs[:n], refs[n:2 * n]
        sems = refs[2 * n], refs[2 * n + 1]
        for a in range(n):
            for k in range(1, NDEV):
                _xfer_copy(src_refs[a], land_refs[a], sems, a, k, layers[a], scatter, False).wait_send()
                _xfer_copy(src_refs[a], land_refs[a], sems, a, k, layers[a], scatter, True).wait_recv()

    ops = list(srcs) + list(lands)
    res = pl.pallas_call(
        body, name=name,
        out_shape=tuple(pltpu.HBM(t.shape, t.dtype) for t in ops),
        in_specs=[HBM_SPEC] * (2 * n) + [SEM_SPEC, SEM_SPEC, pl.BlockSpec(memory_space=pl.ANY)],
        out_specs=tuple([HBM_SPEC] * (2 * n)),
        input_output_aliases={i: i for i in range(2 * n)},
        compiler_params=pltpu.CompilerParams(has_side_effects=DATAFLOW),
    )(*ops, send_sems, recv_sems, after)
    return list(res[n:])


HI = lax.Precision.HIGHEST
MOD_ROWS = 16


def _col_tile(n, target=512):
    return target if n % target == 0 else n


def _modulation(s_in, ada_w, b_loc, name):
    L, D, nl = ada_w.shape
    tn = _col_tile(nl)

    def body(s_ref, w_ref, b_ref, o_ref):
        o_ref[0] = jnp.dot(_silu(s_ref[...]), w_ref[0], preferred_element_type=F32, precision=HI) + b_ref[0]

    return pl.pallas_call(
        body, name=name, grid=(L, nl // tn),
        in_specs=[pl.BlockSpec((MOD_ROWS, D), lambda l, j: (0, 0)), pl.BlockSpec((1, D, tn), lambda l, j: (l, 0, j)),
                  pl.BlockSpec((1, 1, tn), lambda l, j: (l, 0, j))],
        out_specs=pl.BlockSpec((1, MOD_ROWS, tn), lambda l, j: (l, 0, j)),
        out_shape=jax.ShapeDtypeStruct((L, MOD_ROWS, nl), F32),
        compiler_params=_cparams(("parallel", "parallel"), 4 * _nbytes((D, tn), F32)),
    )(s_in, ada_w, b_loc)


def _modulation_bwd(s_in, dml, dmc, ada_w, name):
    L, D, nl = ada_w.shape
    tn = _col_tile(nl)

    def body(s_ref, dml_ref, dmc_ref, w_ref, g_ref, pc_ref):
        l, j = pl.program_id(0), pl.program_id(1)
        a = _silu(s_ref[...])
        tot = dmc_ref[0, 0]
        for d in range(1, NDEV):
            tot = tot + dmc_ref[d, 0]
        row = lax.broadcasted_iota(jnp.int32, (MOD_ROWS, tn), 0)
        dm = jnp.where(row == NDEV, tot, dml_ref[:, 0, 0, :])
        g_ref[0] = lax.dot_general(a, dm, TN, preferred_element_type=F32, precision=HI)
        tot8 = jnp.where(lax.broadcasted_iota(jnp.int32, (8, tn), 0) == 0, tot, 0.0)
        part = lax.dot_general(tot8, w_ref[0], NT, preferred_element_type=F32, precision=HI)

        @pl.when((l == 0) & (j == 0))
        def _():
            pc_ref[...] = part

        @pl.when((l != 0) | (j != 0))
        def _():
            pc_ref[...] += part

    return pl.pallas_call(
        body, name=name, grid=(L, nl // tn),
        in_specs=[pl.BlockSpec((MOD_ROWS, D), lambda l, j: (0, 0)), pl.BlockSpec((MOD_ROWS, 1, 1, tn), lambda l, j: (0, l, 0, j)),
                  pl.BlockSpec((NDEV, 1, 1, tn), lambda l, j: (0, l, 0, j)), pl.BlockSpec((1, D, tn), lambda l, j: (l, 0, j))],
        out_specs=[pl.BlockSpec((1, D, tn), lambda l, j: (l, 0, j)), pl.BlockSpec((8, D), lambda l, j: (0, 0))],
        out_shape=[jax.ShapeDtypeStruct((L, D, nl), F32), jax.ShapeDtypeStruct((8, D), F32)],
        compiler_params=_cparams(("arbitrary", "arbitrary"), 8 * _nbytes((D, tn), F32)),
    )(s_in, dml.reshape(MOD_ROWS, L, 1, nl), dmc.reshape(NDEV, L, 1, nl), ada_w)


def _cctx_update(bufs, c_ctx, m, v, name):
    n, R, _ = bufs.shape

    def body(b_ref, w_ref, m_ref, v_ref, g_ref, d_ref, m2_ref, v2_ref):
        g = b_ref[0]
        for q in range(1, n):
            g = g + b_ref[q]
        g = g * _dsilu(w_ref[...])
        d, m2, v2 = _adamw_math(w_ref[...], g, m_ref[...], v_ref[...])
        g_ref[...], d_ref[...], m2_ref[...], v2_ref[...] = g, d, m2, v2

    vm = pl.BlockSpec(memory_space=pltpu.VMEM)
    shp = jax.ShapeDtypeStruct((R, LANE), F32)
    return pl.pallas_call(body, name=name, in_specs=[vm, vm, vm, vm], out_specs=[vm, vm, vm, vm],
                          out_shape=[shp, shp, shp, shp])(bufs, c_ctx, m, v)


def _pack(arrs):
    flat = jnp.concatenate([a.reshape(-1).astype(F32) for a in arrs])
    n = flat.shape[0]
    total = -(-n // (8 * LANE)) * (8 * LANE)
    return jnp.pad(flat, (0, total - n)).reshape(total // LANE, LANE)


def _unpack(buf, shapes):
    lead = buf.shape[:-2]
    flat = buf.reshape(lead + (-1,))
    out, off = [], 0
    for s in shapes:
        n = int(np.prod(s))
        out.append(flat[..., off:off + n].reshape(lead + tuple(s)))
        off += n
    return out


WEIGHTS = ['c_ctx', 'ada_w', 'ada_b', 'norm1_w', 'norm2_w', 'ssd_w_in', 'ssd_conv_w', 'ssd_conv_b', 'ssd_dt_bias_f',
           'ssd_dt_bias_b', 'ssd_a_log_f', 'ssd_a_log_b', 'ssd_d', 'ssd_norm_w', 'ssd_w_out', 'attn_w_qkv', 'attn_q_gain',
           'attn_k_gain', 'attn_sinks', 'attn_w_o', 'ffn_w_up', 'ffn_conv_w', 'ffn_conv_b', 'ffn_w_down']
SMALL = ['ada_b', 'norm1_w', 'norm2_w', 'ssd_conv_b', 'ssd_dt_bias_f', 'ssd_dt_bias_b', 'ssd_a_log_f', 'ssd_a_log_b', 'ssd_d',
         'ssd_norm_w', 'attn_q_gain', 'attn_k_gain', 'attn_sinks', 'ffn_conv_b']
BIG = ['ssd_w_in', 'ssd_w_out', 'attn_w_qkv', 'attn_w_o', 'ffn_w_up', 'ffn_w_down']


def _step(x, c, ctx, w, tgt, m, v):
    xi, yi, ci = _me()
    me = 4 * xi + 2 * yi + ci
    t_lat, D = x.shape[1], x.shape[2]
    ctx_rows = ctx.shape[1]
    T = ctx_rows + t_lat
    L, n_ssd, n_att = w['norm1_w'].shape[0], w['ssd_d'].shape[0], w['attn_sinks'].shape[0]
    H, DI, XBC = w['ssd_d'].shape[1], w['ssd_norm_w'].shape[1], w['ssd_conv_b'].shape[1]
    P = DI // H
    IN = w['ssd_w_in'].shape[2] * NDEV
    hd, n_q = w['attn_q_gain'].shape[1], w['attn_sinks'].shape[1]
    F2 = w['ffn_conv_b'].shape[1]
    G = F2 // NDEV
    nl = w['ada_w'].shape[2]
    ncc = ctx_rows // BLK
    perm = [_ffn_perm(s) for s in range(NDEV)]
    inv = [perm.index(d) for d in range(NDEV)]

    def reorder(t, order):
        return jnp.concatenate([t[..., o * G:(o + 1) * G] for o in order], axis=-1)

    def interleave(t):
        return reorder(t, perm)

    def deinterleave(t):
        return reorder(t, inv)

    def layer_weights(i):
        mixer = ['ssd_w_in', 'ssd_w_out'] if i % 2 == 0 else ['attn_w_qkv', 'attn_w_o']
        return [(n, i // 2) for n in mixer] + [('ffn_w_up', i), ('ffn_w_down', i)]

    def gather_start(keys, tag, zero):
        srcs = [(w[n][j] + zero).astype(MXU) for n, j in keys]
        lands = [lax.dynamic_update_index_in_dim(lax.empty((NDEV,) + t.shape, t.dtype), t, me, 0) for t in srcs]
        ss, rs, srcs, lands, tok = _xfer_start(srcs, lands, [None] * len(srcs), False, f"gather_start_{tag}")
        return (ss, rs, srcs, lands, tag), tok[0, 0]

    def gather_wait(p, after):
        return _xfer_wait(*p[:4], [None] * len(p[2]), False, after, f"gather_wait_{p[4]}")

    first, tok = gather_start(layer_weights(0)[:1], "0a", jnp.zeros((), F32))
    pending, tok = gather_start(layer_weights(0)[1:], "0b", tok)

    shapes_a = [(D,), w['ssd_conv_w'].shape, w['ffn_conv_w'].shape]
    g_a = _small_allgather(_pack([c[0], w['ssd_conv_w'], w['ffn_conv_w']]), "gather_cond")
    c_all, scw_all, fcw_all = _unpack(g_a, shapes_a)
    ssd_cw = scw_all.transpose(1, 2, 0, 3).reshape(n_ssd, 3, XBC)
    ffn_cw = jnp.concatenate([fcw_all[d] for d in perm], axis=-1)
    ffn_cb = interleave(w['ffn_conv_b'])[:, None, :]

    s_in = jnp.concatenate([c_all, w['c_ctx'][None], jnp.zeros((MOD_ROWS - NDEV - 1, D), F32)], axis=0)
    b_loc = lax.dynamic_slice(w['ada_b'], (0, me * nl), (L, nl))[:, None, :]
    mod_loc = _modulation(s_in, w['ada_w'], b_loc, "modulation")
    g_b = _small_allgather(_pack([mod_loc]), "gather_mod")
    (mod_all,) = _unpack(g_b, [mod_loc.shape])
    mod_lat = lax.dynamic_index_in_dim(mod_all, me, axis=2, keepdims=False)
    mod_ctx = mod_all[:, :, NDEV, :]
    to_mod = lambda t: t.transpose(1, 0, 2).reshape(L, 6, D)
    mod = jnp.stack([to_mod(mod_ctx), to_mod(mod_lat)], axis=1)

    w_in, w_out, w_qkv, w_o = [None] * n_ssd, [None] * n_ssd, [None] * n_att, [None] * n_att
    w_up, w_down = [None] * L, [None] * L

    cos, sin = _rope_tables(t_lat, ctx_rows, hd)
    e_f, e_b = _expand_matrix(H, P, False), _expand_matrix(H, P, True)
    et_f, et_b = e_f.T, e_b.T

    xs = jnp.concatenate([ctx[0], x[0]], axis=0)
    saved = []
    for i in range(L):
        j = i // 2
        s = dict(x0=xs)
        zero_of = lambda t: jnp.minimum(jnp.abs(t[0, 0, 0].astype(F32)), 0.0)
        if i == 0:
            w_in[0] = _cat_cols(gather_wait(first, mod)[0], "ssd_in_cat")
        else:
            got = gather_wait(pending, xs)
            if i % 2 == 0:
                w_in[j], w_out[j] = _cat_cols(got[0], "ssd_in_cat"), got[1].reshape(DI, D)
            else:
                w_qkv[j], w_o[j] = got[0], got[1].reshape(n_q * hd, D)
            w_up[i], w_down[i] = got[2], got[3].reshape(F2 // 2, D)
            if i + 1 < L:
                pending, tok = gather_start(layer_weights(i + 1), f"{i + 1}", zero_of(got[1]))
        nw1, nw2 = w['norm1_w'][i][None] + tok, w['norm2_w'][i][None]
        s['h1'] = _normmod(xs, nw1, mod[i], 0, ctx_rows, "normmod")
        if i % 2 == 0:
            s['zx'] = _mm(s['h1'], w_in[j], tm=768, tn=1152, tk=2048, name="mm_ssd_in")
            s['cw'], s['cb'] = ssd_cw[j], w['ssd_conv_b'][j][None]
            s['xbc'] = _ssd_conv(s['zx'], s['cw'], s['cb'], DI, ctx_rows, "ssd_conv")
            s['dtb'] = jnp.concatenate([w['ssd_dt_bias_f'][j], w['ssd_dt_bias_b'][j]])[None]
            s['alog'] = jnp.concatenate([w['ssd_a_log_f'][j], w['ssd_a_log_b'][j]])[None]
            s['yf'], s['hin_f'] = _ssd_scan(s['xbc'], s['zx'], s['dtb'], s['alog'], e_f, H, ncc, False, "ssd_scan_f")
            s['yb'], s['hin_b'] = _ssd_scan(s['xbc'], s['zx'], s['dtb'], s['alog'], e_b, H, ncc, True, "ssd_scan_b")
            s['dexp'], s['snw'] = jnp.repeat(w['ssd_d'][j], P)[None], w['ssd_norm_w'][j][None]
            if i == 0:
                got = gather_wait(pending, s['yb'])
                w_out[0], w_up[0], w_down[0] = got[0].reshape(DI, D), got[1], got[2].reshape(F2 // 2, D)
                if L > 1:
                    pending, tok = gather_start(layer_weights(1), "1", zero_of(got[0]))
                    s['snw'] = s['snw'] + tok
            s['o'] = _ssd_finish(s['yf'], s['yb'], s['xbc'], s['zx'], s['dexp'], s['snw'], "ssd_finish")
            s['mix'], x1 = _mm(s['o'], w_out[j], tm=768, tn=512, tk=4096, name="mm_ssd_out",
                               resid=xs, gate=mod[i][:, 2], ctx_rows=ctx_rows)
        else:
            s['qkv'] = _mm(s['h1'], w_qkv[j], tm=768, tn=384, tk=2048, name="mm_qkv", bslots=_ident)
            s['qg'], s['kg'] = w['attn_q_gain'][j][None], w['attn_k_gain'][j][None]
            s['qr'], s['kr'], s['vb'] = _qk_prep(s['qkv'], s['qg'], s['kg'], cos, sin, n_q, ctx_rows, "qk_prep")
            s['o'] = _attn_fwd(s['qr'], s['kr'], s['vb'], w['attn_sinks'][j], n_q, ctx_rows, "attn_fwd")
            s['mix'], x1 = _mm(s['o'], w_o[j], tm=768, tn=1024, tk=2048, name="mm_attn_out",
                               resid=xs, gate=mod[i][:, 2], ctx_rows=ctx_rows)
        s['x1'] = x1
        s['h2'] = _normmod(x1, nw2, mod[i], 1, ctx_rows, "normmod")
        s['u'] = _mm(s['h2'], w_up[i], tm=768, tn=1408, tk=2048, name="mm_ffn_up", bslots=_ffn_perm)
        s['a'] = _ffn_mid(s['u'], ffn_cw[i], ffn_cb[i], ctx_rows, "ffn_mid")
        s['f'], xs = _mm(s['a'], w_down[i], tm=768, tn=512, tk=5632, name="mm_ffn_down",
                         resid=x1, gate=mod[i][:, 5], ctx_rows=ctx_rows)
        saved.append(s)

    dx, lacc = _loss_head(xs, tgt[0], ctx_rows, "loss_head")
    loss = lax.psum(lacc[0, 0], ("x", "y", "c"))

    gbig = {name: [None] * w[name].shape[0] for name in BIG}
    gs = {name: [None] * w[name].shape[0] for name in SMALL + ['ssd_conv_w', 'ffn_conv_w']}
    dmod = [None] * L
    lands = {n: lax.empty((NDEV,) + w[n].shape, XFER) for n in BIG}
    pend_early, pend_late, tok = None, None, jnp.zeros((), F32)

    def scatter_start(keys, tag, zero=None):
        srcs = [gbig[n][j] for n, j in keys]
        for (n, j), g in zip(keys, srcs):
            own = lax.dynamic_index_in_dim(g, me, 0, keepdims=False)
            if zero is not None:
                own = own + zero.astype(own.dtype)
            lands[n] = lax.dynamic_update_slice(lands[n], own[None, None], (me, j, 0, 0))
        ss, rs, srcs, got, t = _xfer_start(srcs, [lands[n] for n, _ in keys], [j for _, j in keys], True, f"scatter_start_{tag}")
        return (ss, rs, srcs, got, keys, tag), t[0, 0]

    def scatter_wait(p, after):
        ss, rs, srcs, got, keys, tag = p
        got = _xfer_wait(ss, rs, srcs, got, [j for _, j in keys], True, after, f"scatter_wait_{tag}")
        for (n, _), t in zip(keys, got):
            lands[n] = t

    for i in reversed(range(L)):
        j = i // 2
        s = saved[i]
        nw1, nw2 = w['norm1_w'][i][None], w['norm2_w'][i][None]
        dm2, dg2 = _gate_bwd(dx, s['f'], mod[i] + tok, 1, ctx_rows, "gate_bwd")
        da = _mm(dm2, w_down[i], tb=True, out_dtype=MXU, tm=768, tn=1408, tk=2048, name="mm_ffn_down_dx")
        gbig['ffn_w_down'][i] = _mm(s['a'], dm2, ta=True, out_dtype=XFER, tm=1408, tn=1024, tk=2112,
                                    name="mm_ffn_down_dw").reshape(NDEV, -1, D)
        du, gcw = _ffn_mid_bwd(s['u'], da, ffn_cw[i], ffn_cb[i], ctx_rows, "ffn_mid_bwd")
        gcw = deinterleave(gcw)
        gs['ffn_conv_w'][i], gs['ffn_conv_b'][i] = gcw[0:3], gcw[3]
        dh2 = _mm(du, w_up[i], tb=True, out_dtype=MXU, tm=768, tn=1024, tk=1408, name="mm_ffn_up_dx", bslots=_ffn_perm)
        gbig['ffn_w_up'][i] = _mm(s['h2'], du, ta=True, out_dtype=XFER, tm=1024, tn=1408, tk=2112, name="mm_ffn_up_dw",
                                  oslots=_ffn_perm)
        dx1, sums2 = _normmod_bwd(s['x1'], nw2, mod[i], dh2, dx, 1, ctx_rows, "normmod_bwd")
        dmix, dg1 = _gate_bwd(dx1, s['mix'], mod[i], 0, ctx_rows, "gate_bwd")

        def send_early(after):
            if pend_early is not None:
                scatter_wait(pend_early, after)
            return scatter_start(layer_weights(i)[1:], f"early_{i}")

        if i % 2 == 0:
            do = _mm(dmix, w_out[j], tb=True, out_dtype=MXU, tm=768, tn=1024, tk=2048, name="mm_ssd_out_dx")
            gbig['ssd_w_out'][j] = _mm(s['o'], dmix, ta=True, out_dtype=XFER, tm=1024, tn=1024, tk=2112,
                                       name="mm_ssd_out_dw").reshape(NDEV, -1, D)
            pend_early, tok = send_early(do)
            dy, dzx, fs, dd = _ssd_finish_bwd(s['yf'], s['yb'], s['xbc'], s['zx'], s['dexp'], s['snw'] + tok, do, et_f, (T, IN),
                                              "ssd_finish_bwd")
            acc = _ssd_scan_bwd(s['xbc'], s['zx'], s['dtb'], s['alog'], e_f, et_f, s['dexp'], s['hin_f'], dy, None,
                                H, ncc, False, "ssd_scan_bwd_f")
            dxbc, ddt, ssm = _ssd_scan_bwd(s['xbc'], s['zx'], s['dtb'], s['alog'], e_b, et_b, s['dexp'], s['hin_b'], dy, acc,
                                           H, ncc, True, "ssd_scan_bwd_b")
            dzx, gscw = _ssd_conv_bwd(s['zx'], dxbc, dzx, s['cw'], s['cb'], DI, ctx_rows, "ssd_conv_bwd")
            dzx = _put_cols(dzx, ddt, IN // LANE - 1, "ssd_put_ddt")
            dh1 = _mm(dzx, w_in[j], tb=True, out_dtype=MXU, tm=768, tn=1024, tk=3456, name="mm_ssd_in_dx")
            dwi = _mm(s['h1'], dzx, ta=True, tm=1024, tn=1152, tk=2112, name="mm_ssd_in_dw")
            gbig['ssd_w_in'][j] = _split_cols(dwi, NDEV, XFER, "ssd_in_split")
            gs['ssd_conv_w'][j], gs['ssd_conv_b'][j] = gscw[0:3], gscw[3]
            gs['ssd_dt_bias_f'][j], gs['ssd_dt_bias_b'][j] = ssm[1, :H], ssm[1, H:]
            gs['ssd_a_log_f'][j], gs['ssd_a_log_b'][j] = ssm[0, :H], ssm[0, H:]
            gs['ssd_d'][j], gs['ssd_norm_w'][j] = dd[1, :H], fs[0]
        else:
            do = _mm(dmix, w_o[j], tb=True, out_dtype=MXU, tm=768, tn=1024, tk=2048, name="mm_attn_out_dx")
            gbig['attn_w_o'][j] = _mm(s['o'], dmix, ta=True, out_dtype=XFER, tm=1024, tn=1024, tk=2112,
                                      name="mm_attn_out_dw").reshape(NDEV, -1, D)
            pend_early, tok = send_early(do)
            dq, dk, dv, dsk = _attn_bwd(s['qr'], s['kr'], s['vb'], w['attn_sinks'][j] + tok, do, n_q, ctx_rows, "attn_bwd")
            dqkv, gg = _qk_prep_bwd(s['qkv'], s['qg'], s['kg'], cos, sin, dq, dk, dv, n_q, ctx_rows, "qk_prep_bwd")
            dh1 = _mm(dqkv, w_qkv[j], tb=True, out_dtype=MXU, tm=768, tn=1024, tk=384, name="mm_qkv_dx", bslots=_ident)
            gbig['attn_w_qkv'][j] = _mm(s['h1'], dqkv, ta=True, out_dtype=XFER, tm=1024, tn=384, tk=2112, name="mm_qkv_dw",
                                        oslots=_ident)
            gs['attn_q_gain'][j], gs['attn_k_gain'][j] = gg[0], gg[1]
            gs['attn_sinks'][j] = dsk.reshape(ATTN_KV_HEADS, 8, LANE)[:, 0, :n_q // ATTN_KV_HEADS].reshape(n_q)
        dx, sums1 = _normmod_bwd(s['x0'], nw1, mod[i], dh1, dx1, 0, ctx_rows, "normmod_bwd")
        gs['norm1_w'][i], gs['norm2_w'][i] = sums1[0, 2] + sums1[1, 2], sums2[0, 2] + sums2[1, 2]
        dmod[i] = jnp.stack([sums1[:, 0], sums1[:, 1], dg1[:, 0], sums2[:, 0], sums2[:, 1], dg2[:, 0]], axis=1)
        if pend_late is not None:
            scatter_wait(pend_late, dx)
            pend_late = None
        if i > 0:
            pend_late, tok = scatter_start(layer_weights(i)[:1], f"late_{i}")
    grad_x = dx[ctx_rows:][None]
    dmod = jnp.stack(dmod)
    dmod_ctx, dmod_lat = dmod[:, 0].reshape(L, 6 * D), dmod[:, 1].reshape(L, 6 * D)
    gs['ada_b'] = dmod_ctx + dmod_lat

    out = {}

    small_g = [jnp.stack(gs[n]) if isinstance(gs[n], list) else gs[n] for n in SMALL]
    extras = [jnp.stack(gs['ssd_conv_w']), jnp.stack(gs['ffn_conv_w'])]
    shapes_c = [w[n].shape for n in SMALL] + [e.shape for e in extras]
    g_c = _small_allgather(_pack(small_g + extras), "gather_small")
    zeros = [jnp.zeros(e.shape, F32) for e in extras]
    res = _adamw_small(g_c, _pack([w[n] for n in SMALL] + zeros), _pack([m[n] for n in SMALL] + zeros),
                       _pack([v[n] for n in SMALL] + zeros), "adamw_small")
    res = [_unpack(r, shapes_c) for r in res]
    for k, n in enumerate(SMALL):
        out[n] = tuple(r[k] for r in res)
    g_scw, g_fcw = res[0][len(SMALL)], res[0][len(SMALL) + 1]
    g_scw = lax.dynamic_index_in_dim(g_scw.reshape(n_ssd, 3, NDEV, XBC // NDEV), me, axis=2, keepdims=False)
    g_fcw = lax.dynamic_index_in_dim(g_fcw.reshape(L, 3, NDEV, G), me, axis=2, keepdims=False)
    conv = ['ssd_conv_w', 'ffn_conv_w']
    res = _adamw_small(_pack([g_scw, g_fcw])[None], _pack([w[n] for n in conv]), _pack([m[n] for n in conv]),
                       _pack([v[n] for n in conv]), "adamw_conv")
    res = [_unpack(r, [w[n].shape for n in conv]) for r in res]
    for k, n in enumerate(conv):
        out[n] = tuple(r[k] for r in res)

    g_m = _small_allgather(jnp.concatenate([dmod_lat, dmod_ctx], axis=0), "gather_dmod")
    all_lat, all_ctx = g_m[:, :L], g_m[:, L:]
    my_cols = lambda t: lax.dynamic_slice(t, (0, 0, me * nl), (NDEV, L, nl))
    dml = jnp.concatenate([my_cols(all_lat), jnp.zeros((MOD_ROWS - NDEV, L, nl), F32)], axis=0)
    g_ada, pc = _modulation_bwd(s_in, dml, my_cols(all_ctx), w['ada_w'], "modulation_bwd")
    out['ada_w'] = _adamw(g_ada[None], w['ada_w'], m['ada_w'], v['ada_w'], "adamw")
    g_d = _small_allgather(_pack([pc[0]]), "gather_cctx")
    res = _cctx_update(g_d, _pack([w['c_ctx']]), _pack([m['c_ctx']]), _pack([v['c_ctx']]), "adamw_cctx")
    out['c_ctx'] = tuple(_unpack(r, [(D,)])[0] for r in res)

    c_new = out['c_ctx'][0]
    pend_late, _ = scatter_start(layer_weights(0)[:1], "late_0", jnp.minimum(jnp.abs(c_new[0]), 0.0))
    scatter_wait(pend_early, c_new)
    last = [n for n, _ in layer_weights(0)[:1]]
    for name in [n for n in BIG if n not in last]:
        out[name] = _adamw(lands[name], w[name], m[name], v[name], "adamw")
    scatter_wait(pend_late, out['ffn_w_down'][0])
    for name in last:
        out[name] = _adamw(lands[name], w[name], m[name], v[name], "adamw")

    return (loss, grad_x) + tuple(out[n][k] for k in range(4) for n in WEIGHTS)


def kernel(x, c, ctx, c_ctx, ada_w, ada_b, norm1_w, norm2_w, ssd_w_in, ssd_conv_w, ssd_conv_b, ssd_dt_bias_f, ssd_dt_bias_b, ssd_a_log_f, ssd_a_log_b, ssd_d, ssd_norm_w, ssd_w_out, attn_w_qkv, attn_q_gain, attn_k_gain, attn_sinks, attn_w_o, ffn_w_up, ffn_conv_w, ffn_conv_b, ffn_w_down, loss_target, m_c_ctx, m_ada_w, m_ada_b, m_norm1_w, m_norm2_w, m_ssd_w_in, m_ssd_conv_w, m_ssd_conv_b, m_ssd_dt_bias_f, m_ssd_dt_bias_b, m_ssd_a_log_f, m_ssd_a_log_b, m_ssd_d, m_ssd_norm_w, m_ssd_w_out, m_attn_w_qkv, m_attn_q_gain, m_attn_k_gain, m_attn_sinks, m_attn_w_o, m_ffn_w_up, m_ffn_conv_w, m_ffn_conv_b, m_ffn_w_down, v_c_ctx, v_ada_w, v_ada_b, v_norm1_w, v_norm2_w, v_ssd_w_in, v_ssd_conv_w, v_ssd_conv_b, v_ssd_dt_bias_f, v_ssd_dt_bias_b, v_ssd_a_log_f, v_ssd_a_log_b, v_ssd_d, v_ssd_norm_w, v_ssd_w_out, v_attn_w_qkv, v_attn_q_gain, v_attn_k_gain, v_attn_sinks, v_attn_w_o, v_ffn_w_up, v_ffn_conv_w, v_ffn_conv_b, v_ffn_w_down):
    w = dict(c_ctx=c_ctx, ada_w=ada_w, ada_b=ada_b, norm1_w=norm1_w, norm2_w=norm2_w, ssd_w_in=ssd_w_in, ssd_conv_w=ssd_conv_w, ssd_conv_b=ssd_conv_b, ssd_dt_bias_f=ssd_dt_bias_f, ssd_dt_bias_b=ssd_dt_bias_b, ssd_a_log_f=ssd_a_log_f, ssd_a_log_b=ssd_a_log_b, ssd_d=ssd_d, ssd_norm_w=ssd_norm_w, ssd_w_out=ssd_w_out, attn_w_qkv=attn_w_qkv, attn_q_gain=attn_q_gain, attn_k_gain=attn_k_gain, attn_sinks=attn_sinks, attn_w_o=attn_w_o, ffn_w_up=ffn_w_up, ffn_conv_w=ffn_conv_w, ffn_conv_b=ffn_conv_b, ffn_w_down=ffn_w_down)
    m = dict(c_ctx=m_c_ctx, ada_w=m_ada_w, ada_b=m_ada_b, norm1_w=m_norm1_w, norm2_w=m_norm2_w, ssd_w_in=m_ssd_w_in, ssd_conv_w=m_ssd_conv_w, ssd_conv_b=m_ssd_conv_b, ssd_dt_bias_f=m_ssd_dt_bias_f, ssd_dt_bias_b=m_ssd_dt_bias_b, ssd_a_log_f=m_ssd_a_log_f, ssd_a_log_b=m_ssd_a_log_b, ssd_d=m_ssd_d, ssd_norm_w=m_ssd_norm_w, ssd_w_out=m_ssd_w_out, attn_w_qkv=m_attn_w_qkv, attn_q_gain=m_attn_q_gain, attn_k_gain=m_attn_k_gain, attn_sinks=m_attn_sinks, attn_w_o=m_attn_w_o, ffn_w_up=m_ffn_w_up, ffn_conv_w=m_ffn_conv_w, ffn_conv_b=m_ffn_conv_b, ffn_w_down=m_ffn_w_down)
    v = dict(c_ctx=v_c_ctx, ada_w=v_ada_w, ada_b=v_ada_b, norm1_w=v_norm1_w, norm2_w=v_norm2_w, ssd_w_in=v_ssd_w_in, ssd_conv_w=v_ssd_conv_w, ssd_conv_b=v_ssd_conv_b, ssd_dt_bias_f=v_ssd_dt_bias_f, ssd_dt_bias_b=v_ssd_dt_bias_b, ssd_a_log_f=v_ssd_a_log_f, ssd_a_log_b=v_ssd_a_log_b, ssd_d=v_ssd_d, ssd_norm_w=v_ssd_norm_w, ssd_w_out=v_ssd_w_out, attn_w_qkv=v_attn_w_qkv, attn_q_gain=v_attn_q_gain, attn_k_gain=v_attn_k_gain, attn_sinks=v_attn_sinks, attn_w_o=v_attn_w_o, ffn_w_up=v_ffn_w_up, ffn_conv_w=v_ffn_conv_w, ffn_conv_b=v_ffn_conv_b, ffn_w_down=v_ffn_w_down)
    return _step(x, c, ctx, w, loss_target, m, v)
```

```python
import functools

import numpy as np
import jax
import jax.numpy as jnp
from jax import lax
from jax.experimental import pallas as pl
from jax.experimental.pallas import tpu as pltpu

F32 = jnp.float32
BF16 = jnp.bfloat16
MXU = BF16
XFER = BF16
NORM_EPS = 1e-6
VMEM_CAP = 56 * 1024 * 1024
HALO = 8
LANE = 128
NDEV = 8

GRID_W = 64
ROPE_THETA = 10000.0
ATTN_KV_HEADS = 4
ATTN_WINDOW = 128
BLK = 128
SSD_GROUPS = 8

ADAM_LR, ADAM_B1, ADAM_B2, ADAM_EPS, ADAM_WD, ADAM_STEP = 0.001, 0.9, 0.999, 1e-08, 0.01, 10

MESH_ID = pl.DeviceIdType.MESH


def _cparams(sem, est_bytes):
    lim = int(min(VMEM_CAP, max(16 * 1024 * 1024, est_bytes * 1.3 + (4 << 20))))
    return pltpu.CompilerParams(dimension_semantics=sem, vmem_limit_bytes=lim)


def _nbytes(shape, dtype):
    return int(np.prod(shape)) * jnp.dtype(dtype).itemsize


def _silu(x):
    return x * jax.nn.sigmoid(x)


def _dsilu(x):
    s = jax.nn.sigmoid(x)
    return s * (1.0 + x * (1.0 - s))


def _split3(v):
    h = v.astype(BF16)
    r = v - h.astype(F32)
    m = r.astype(BF16)
    l = (r - m.astype(F32)).astype(BF16)
    return h, m, l


def _dot(a, b, dn=(((1,), (0,)), ((), ()))):
    return lax.dot_general(a, b, dn, preferred_element_type=F32)


NT = (((1,), (1,)), ((), ()))
TN = (((0,), (0,)), ((), ()))


def _dot3_rhs(sel, v):
    return sum(_dot(sel, p) for p in _split3(v))


def _dot3_lhs(v, sel, dn=(((1,), (0,)), ((), ()))):
    return sum(_dot(p, sel, dn) for p in _split3(v))


def _dot2_stacked(vals, sel):
    pieces = []
    for v in vals:
        h, m, _ = _split3(v)
        pieces += [h, m]
    r = _dot(jnp.concatenate(pieces, axis=0), sel)
    out, row = [], 0
    for v in vals:
        n = v.shape[0]
        out.append(r[row:row + n] + r[row + n:row + 2 * n])
        row += 2 * n
    return out


def _expand2(vals, sel2):
    lhs = []
    for v in vals:
        h, m, _ = _split3(v)
        lhs.append(jnp.concatenate([h, m], axis=1))
    r = _dot(jnp.concatenate(lhs, axis=0), sel2)
    n = vals[0].shape[0]
    return [r[i * n:(i + 1) * n] for i in range(len(vals))]


def _ident(s):
    return s


def _ffn_perm(s):
    return (s % 2) * 4 + s // 2


def _mm(a, b, *, ta=False, tb=False, out_dtype=F32, tm, tn, tk, name, bslots=None, oslots=None,
        resid=None, gate=None, ctx_rows=0, after=None):
    M = a.shape[1] if ta else a.shape[0]
    K = a.shape[0] if ta else a.shape[1]
    if bslots is None:
        N = b.shape[0] if tb else b.shape[1]
    else:
        G = b.shape[2]
        N = b.shape[1] if tb else NDEV * G
        assert (NDEV * G == K) if tb else (b.shape[1] == K)
    tm, tn, tk = min(tm, M), min(tn, N), min(tk, K)
    if bslots is not None:
        if tb:
            tk = min(tk, G)
            assert G % tk == 0
        else:
            tn = min(tn, G)
            assert G % tn == 0
    if oslots is not None:
        Go = N // NDEV
        tn = min(tn, Go)
        assert Go % tn == 0
    assert M % tm == 0 and N % tn == 0 and K % tk == 0, (name, M, N, K, tm, tn, tk)
    nk = K // tk
    fused = resid is not None
    dn = (((0 if ta else 1,), (1 if tb else 0,)), ((), ()))

    n_in = 2 + (2 if fused else 0) + (1 if after is not None else 0)

    def body(*refs):
        a_ref, b_ref = refs[0], refs[1]
        if fused:
            r_ref, g_ref = refs[2], refs[3]
            o_ref, x_ref = refs[n_in], refs[n_in + 1]
            rest = refs[n_in + 2:]
        else:
            o_ref = refs[n_in]
            rest = refs[n_in + 1:]
        bv = b_ref[0] if bslots is not None else b_ref[...]
        p = lax.dot_general(a_ref[...].astype(MXU), bv.astype(MXU), dn, preferred_element_type=F32)

        def finish(acc):
            if oslots is not None:
                o_ref[0] = acc.astype(o_ref.dtype)
            else:
                o_ref[...] = acc.astype(o_ref.dtype)
            if fused:
                row = pl.program_id(0) * tm + lax.broadcasted_iota(jnp.int32, (tm, 1), 0)
                g = jnp.where(row < ctx_rows, g_ref[0:1, :], g_ref[1:2, :])
                x_ref[...] = r_ref[...] + g * acc

        if nk == 1:
            finish(p)
        else:
            acc_ref = rest[0]
            k = pl.program_id(2)

            @pl.when(k == 0)
            def _():
                acc_ref[...] = p

            @pl.when(k > 0)
            def _():
                acc_ref[...] += p

            @pl.when(k == nk - 1)
            def _():
                finish(acc_ref[...])

    a_spec = pl.BlockSpec((tk, tm), lambda i, j, k: (k, i)) if ta else pl.BlockSpec((tm, tk), lambda i, j, k: (i, k))
    if bslots is None:
        b_spec = pl.BlockSpec((tn, tk), lambda i, j, k: (j, k)) if tb else pl.BlockSpec((tk, tn), lambda i, j, k: (k, j))
    elif tb:
        kpg = G // tk
        b_spec = pl.BlockSpec((1, tn, tk), lambda i, j, k: (bslots(k // kpg), j, k % kpg))
    else:
        npg = G // tn
        b_spec = pl.BlockSpec((1, tk, tn), lambda i, j, k: (bslots(j // npg), k, j % npg))
    if oslots is None:
        o_spec = pl.BlockSpec((tm, tn), lambda i, j, k: (i, j))
        o_shape = jax.ShapeDtypeStruct((M, N), out_dtype)
    else:
        opg = Go // tn
        o_spec = pl.BlockSpec((1, tm, tn), lambda i, j, k: (oslots(j // opg), i, j % opg))
        o_shape = jax.ShapeDtypeStruct((NDEV, M, Go), out_dtype)
    in_specs = [a_spec, b_spec]
    out_shape = [o_shape]
    out_specs = [o_spec]
    args = [a, b]
    est = 2 * (_nbytes((tm, tk), a.dtype) + _nbytes((tk, tn), b.dtype) + _nbytes((tm, tn), out_dtype)) + 3 * _nbytes((tm, tn), F32)
    if fused:
        in_specs += [o_spec, pl.BlockSpec((2, tn), lambda i, j, k: (0, j))]
        out_shape.append(jax.ShapeDtypeStruct((M, N), F32))
        out_specs.append(o_spec)
        args += [resid, gate]
        est += 4 * _nbytes((tm, tn), F32)
    if after is not None:
        in_specs.append(pl.BlockSpec((8, LANE), lambda i, j, k: (0, 0)))
        args.append(after)
    scratch = [] if nk == 1 else [pltpu.VMEM((tm, tn), F32)]
    res = pl.pallas_call(
        body, name=name, grid=(M // tm, N // tn, nk), in_specs=in_specs, out_specs=out_specs, out_shape=out_shape,
        scratch_shapes=scratch, compiler_params=_cparams(("parallel", "parallel", "arbitrary"), est),
    )(*args)
    return res if fused else res[0]


def _stream_of(i, tr, ctx_rows):
    return jnp.where(i * tr < ctx_rows, 0, 1)


def _acc_by_stream(sums_ref, part, i, n_ctx):
    @pl.when((i == 0) | (i == n_ctx))
    def _():
        sums_ref[0] = part

    @pl.when((i != 0) & (i != n_ctx))
    def _():
        sums_ref[0] += part


def _normmod(x, nw, mod, which, ctx_rows, name, tr=256):
    T, D = x.shape
    tr = min(tr, ctx_rows)
    assert T % tr == 0 and ctx_rows % tr == 0
    s_sh, s_sc = 3 * which, 3 * which + 1

    def body(x_ref, nw_ref, mod_ref, h_ref):
        xv = x_ref[...]
        r = lax.rsqrt(jnp.mean(xv * xv, axis=-1, keepdims=True) + NORM_EPS)
        y = (xv * r) * nw_ref[...]
        h_ref[...] = (y * (1.0 + mod_ref[0, s_sc:s_sc + 1, :]) + mod_ref[0, s_sh:s_sh + 1, :]).astype(h_ref.dtype)

    return pl.pallas_call(
        body, name=name, grid=(T // tr,),
        in_specs=[pl.BlockSpec((tr, D), lambda i: (i, 0)), pl.BlockSpec((1, D), lambda i: (0, 0)),
                  pl.BlockSpec((1, 6, D), lambda i: (_stream_of(i, tr, ctx_rows), 0, 0))],
        out_specs=pl.BlockSpec((tr, D), lambda i: (i, 0)),
        out_shape=jax.ShapeDtypeStruct((T, D), MXU),
        compiler_params=_cparams(("parallel",), 10 * _nbytes((tr, D), F32)),
    )(x, nw, mod)


def _normmod_bwd(x, nw, mod, dh, dx_in, which, ctx_rows, name, tr=256):
    T, D = x.shape
    tr = min(tr, ctx_rows)
    s_sc = 3 * which + 1
    n_ctx = ctx_rows // tr

    def body(x_ref, nw_ref, mod_ref, dh_ref, dxi_ref, dx_ref, sums_ref):
        i = pl.program_id(0)
        xv = x_ref[...]
        r = lax.rsqrt(jnp.mean(xv * xv, axis=-1, keepdims=True) + NORM_EPS)
        xh = xv * r
        dh_v = dh_ref[...].astype(F32)
        sc1 = 1.0 + mod_ref[0, s_sc:s_sc + 1, :]
        nwv = nw_ref[...]
        dxh = dh_v * (nwv * sc1)
        dx_ref[...] = dxi_ref[...] + r * (dxh - xh * jnp.mean(dxh * xh, axis=-1, keepdims=True))
        t = dh_v * xh
        part = jnp.concatenate([jnp.sum(dh_v, axis=0, keepdims=True), jnp.sum(t * nwv, axis=0, keepdims=True),
                                jnp.sum(t * sc1, axis=0, keepdims=True), jnp.zeros((5, D), F32)], axis=0)
        _acc_by_stream(sums_ref, part, i, n_ctx)

    row = pl.BlockSpec((tr, D), lambda i: (i, 0))
    return pl.pallas_call(
        body, name=name, grid=(T // tr,),
        in_specs=[row, pl.BlockSpec((1, D), lambda i: (0, 0)),
                  pl.BlockSpec((1, 6, D), lambda i: (_stream_of(i, tr, ctx_rows), 0, 0)), row, row],
        out_specs=[row, pl.BlockSpec((1, 8, D), lambda i: (_stream_of(i, tr, ctx_rows), 0, 0))],
        out_shape=[jax.ShapeDtypeStruct((T, D), F32), jax.ShapeDtypeStruct((2, 8, D), F32)],
        compiler_params=_cparams(("arbitrary",), 16 * _nbytes((tr, D), F32)),
    )(x, nw, mod, dh, dx_in)


def _gate_bwd(dx, mix, mod, which, ctx_rows, name, tr=256):
    T, D = dx.shape
    tr = min(tr, ctx_rows)
    s_g = 3 * which + 2
    n_ctx = ctx_rows // tr

    def body(dx_ref, mix_ref, mod_ref, dm_ref, sums_ref):
        i = pl.program_id(0)
        dxv = dx_ref[...]
        dm_ref[...] = (dxv * mod_ref[0, s_g:s_g + 1, :]).astype(dm_ref.dtype)
        part = jnp.concatenate([jnp.sum(dxv * mix_ref[...], axis=0, keepdims=True), jnp.zeros((7, D), F32)], axis=0)
        _acc_by_stream(sums_ref, part, i, n_ctx)

    row = pl.BlockSpec((tr, D), lambda i: (i, 0))
    return pl.pallas_call(
        body, name=name, grid=(T // tr,),
        in_specs=[row, row, pl.BlockSpec((1, 6, D), lambda i: (_stream_of(i, tr, ctx_rows), 0, 0))],
        out_specs=[row, pl.BlockSpec((1, 8, D), lambda i: (_stream_of(i, tr, ctx_rows), 0, 0))],
        out_shape=[jax.ShapeDtypeStruct((T, D), MXU), jax.ShapeDtypeStruct((2, 8, D), F32)],
        compiler_params=_cparams(("arbitrary",), 10 * _nbytes((tr, D), F32)),
    )(dx, mix, mod)


def _halo_specs(tr, tn, n_row_tiles, col_of):
    g = tr // HALO
    last = n_row_tiles * g - 1
    return [pl.BlockSpec((HALO, tn), lambda j, i: (jnp.maximum(i * g - 1, 0), col_of(j))),
            pl.BlockSpec((tr, tn), lambda j, i: (i, col_of(j))),
            pl.BlockSpec((HALO, tn), lambda j, i: (jnp.minimum((i + 1) * g, last), col_of(j)))]


def _ext(p_ref, m_ref, n_ref):
    return jnp.concatenate([p_ref[...].astype(F32), m_ref[...].astype(F32), n_ref[...].astype(F32)], axis=0)


def _seq_masks(i, tr, ctx_rows, total_rows):
    row = i * tr - HALO + lax.broadcasted_iota(jnp.int32, (tr + 2 * HALO, 1), 0)
    has_prev = (row != 0) & (row != ctx_rows)
    has_next = (row != ctx_rows - 1) & (row != total_rows - 1)
    return has_prev, has_next


def _shift_down(e):
    return pltpu.roll(e, 1, 0)


def _shift_up(e):
    return pltpu.roll(e, e.shape[0] - 1, 0)


def _neighbours(e, masks):
    prev, nxt = _shift_down(e), _shift_up(e)
    if masks is not None:
        prev, nxt = jnp.where(masks[0], prev, 0.0), jnp.where(masks[1], nxt, 0.0)
    return prev, nxt


def _conv3(e, prev, nxt, w):
    return prev * w[0:1, :] + e * w[1:2, :] + nxt * w[2:3, :]


def _conv3_t(d, w, masks):
    from_prev, from_next = _neighbours(d, masks)
    return from_next * w[0:1, :] + d * w[1:2, :] + from_prev * w[2:3, :]


def _conv_wgrad(d, e, prev, nxt):
    c = slice(HALO, e.shape[0] - HALO)
    dc = d[c]
    return jnp.concatenate([jnp.sum(dc * prev[c], axis=0, keepdims=True), jnp.sum(dc * e[c], axis=0, keepdims=True),
                            jnp.sum(dc * nxt[c], axis=0, keepdims=True), jnp.sum(dc, axis=0, keepdims=True),
                            jnp.zeros((4, e.shape[1]), F32)], axis=0)


def _per_tile_kind(i, tr, ctx_rows, total_rows, fn):
    n_ctx, n_all = ctx_rows // tr, total_rows // tr
    at_end = (i == 0) | (i == n_ctx - 1) | (i == n_ctx) | (i == n_all - 1)

    @pl.when(at_end)
    def _():
        fn(_seq_masks(i, tr, ctx_rows, total_rows))

    @pl.when(jnp.logical_not(at_end))
    def _():
        fn(None)


def _acc_first(ref, part, i):
    @pl.when(i == 0)
    def _():
        ref[...] = part

    @pl.when(i > 0)
    def _():
        ref[...] += part


def _ffn_mid(u, cw, cb, ctx_rows, name, tr=128):
    T, F2 = u.shape
    G = F2 // NDEV
    tr = min(tr, ctx_rows)
    nr, nc = T // tr, NDEV // 2

    def body(up, um, un, w_ref, b_ref, a_ref):
        def tile(masks):
            e = _ext(up, um, un)
            uc = _conv3(e, *_neighbours(e, masks), w_ref[...])[HALO:HALO + tr] + b_ref[...]
            a_ref[...] = (_silu(uc[:, G:]) * uc[:, :G]).astype(a_ref.dtype)

        _per_tile_kind(pl.program_id(1), tr, ctx_rows, T, tile)

    return pl.pallas_call(
        body, name=name, grid=(nc, nr),
        in_specs=_halo_specs(tr, 2 * G, nr, lambda j: j) + [pl.BlockSpec((3, 2 * G), lambda j, i: (0, j)),
                                                             pl.BlockSpec((1, 2 * G), lambda j, i: (0, j))],
        out_specs=pl.BlockSpec((tr, G), lambda j, i: (i, j)),
        out_shape=jax.ShapeDtypeStruct((T, F2 // 2), MXU),
        compiler_params=_cparams(("parallel", "parallel"), 12 * _nbytes((tr + 16, 2 * G), F32)),
    )(u, u, u, cw, cb)


def _ffn_mid_bwd(u, da, cw, cb, ctx_rows, name, tr=128):
    T, F2 = u.shape
    G = F2 // NDEV
    tr = min(tr, ctx_rows)
    nr, nc = T // tr, NDEV // 2

    def body(up, um, un, dp, dm, dn_, w_ref, b_ref, du_ref, gw_ref):
        i = pl.program_id(1)

        def tile(masks):
            e = _ext(up, um, un)
            w = w_ref[...]
            prev, nxt = _neighbours(e, masks)
            uc = _conv3(e, prev, nxt, w) + b_ref[...]
            val, gt = uc[:, :G], uc[:, G:]
            dav = _ext(dp, dm, dn_)
            sg = jax.nn.sigmoid(gt)
            duc = jnp.concatenate([dav * (gt * sg), dav * val * (sg * (1.0 + gt * (1.0 - sg)))], axis=1)
            du_ref[...] = _conv3_t(duc, w, masks)[HALO:HALO + tr].astype(du_ref.dtype)
            _acc_first(gw_ref, _conv_wgrad(duc, e, prev, nxt), i)

        _per_tile_kind(i, tr, ctx_rows, T, tile)

    du, gw = pl.pallas_call(
        body, name=name, grid=(nc, nr),
        in_specs=(_halo_specs(tr, 2 * G, nr, lambda j: j) + _halo_specs(tr, G, nr, lambda j: j)
                  + [pl.BlockSpec((3, 2 * G), lambda j, i: (0, j)), pl.BlockSpec((1, 2 * G), lambda j, i: (0, j))]),
        out_specs=[pl.BlockSpec((tr, 2 * G), lambda j, i: (i, j)), pl.BlockSpec((8, 2 * G), lambda j, i: (0, j))],
        out_shape=[jax.ShapeDtypeStruct((T, F2), MXU), jax.ShapeDtypeStruct((8, F2), F32)],
        compiler_params=_cparams(("parallel", "arbitrary"), 24 * _nbytes((tr + 16, 2 * G), F32)),
    )(u, u, u, da, da, da, cw, cb)
    return du, gw


def _ssd_conv(zx, cw, cb, col0, ctx_rows, name, tr=256, tn=512):
    T = zx.shape[0]
    C = cw.shape[1]
    tr, tn = min(tr, ctx_rows), min(tn, C)
    assert C % tn == 0 and col0 % tn == 0
    nr, nc, cb0 = T // tr, C // tn, col0 // tn

    def body(zp, zm, zn, w_ref, b_ref, o_ref):
        def tile(masks):
            e = _ext(zp, zm, zn)
            o_ref[...] = _silu(_conv3(e, *_neighbours(e, masks), w_ref[...])[HALO:HALO + tr] + b_ref[...])

        _per_tile_kind(pl.program_id(1), tr, ctx_rows, T, tile)

    return pl.pallas_call(
        body, name=name, grid=(nc, nr),
        in_specs=_halo_specs(tr, tn, nr, lambda j: j + cb0) + [pl.BlockSpec((3, tn), lambda j, i: (0, j)),
                                                                pl.BlockSpec((1, tn), lambda j, i: (0, j))],
        out_specs=pl.BlockSpec((tr, tn), lambda j, i: (i, j)),
        out_shape=jax.ShapeDtypeStruct((T, C), F32),
        compiler_params=_cparams(("parallel", "parallel"), 12 * _nbytes((tr + 16, tn), F32)),
    )(zx, zx, zx, cw, cb)


def _ssd_conv_bwd(zx, dxbc, dzx, cw, cb, col0, ctx_rows, name, tr=256, tn=512):
    T = zx.shape[0]
    C = cw.shape[1]
    tr, tn = min(tr, ctx_rows), min(tn, C)
    nr, nc, cb0 = T // tr, C // tn, col0 // tn

    def body(zp, zm, zn, dp, dm, dn_, w_ref, b_ref, dzx_in, dz_ref, gw_ref):
        del dzx_in
        i = pl.program_id(1)

        def tile(masks):
            e = _ext(zp, zm, zn)
            w = w_ref[...]
            prev, nxt = _neighbours(e, masks)
            pre = _conv3(e, prev, nxt, w) + b_ref[...]
            dpre = _ext(dp, dm, dn_) * _dsilu(pre)
            dz_ref[...] = _conv3_t(dpre, w, masks)[HALO:HALO + tr].astype(dz_ref.dtype)
            _acc_first(gw_ref, _conv_wgrad(dpre, e, prev, nxt), i)

        _per_tile_kind(i, tr, ctx_rows, T, tile)

    return pl.pallas_call(
        body, name=name, grid=(nc, nr),
        in_specs=(_halo_specs(tr, tn, nr, lambda j: j + cb0) + _halo_specs(tr, tn, nr, lambda j: j)
                  + [pl.BlockSpec((3, tn), lambda j, i: (0, j)), pl.BlockSpec((1, tn), lambda j, i: (0, j)),
                     pl.BlockSpec(memory_space=pl.ANY)]),
        out_specs=[pl.BlockSpec((tr, tn), lambda j, i: (i, j + cb0)), pl.BlockSpec((8, tn), lambda j, i: (0, j))],
        out_shape=[jax.ShapeDtypeStruct(dzx.shape, dzx.dtype), jax.ShapeDtypeStruct((8, C), F32)],
        input_output_aliases={8: 0},
        compiler_params=_cparams(("parallel", "arbitrary"), 24 * _nbytes((tr + 16, tn), F32)),
    )(zx, zx, zx, dxbc, dxbc, dxbc, cw, cb, dzx)


def _cat_cols(w3, name, tr=256):
    n, K, G = w3.shape
    tr = min(tr, K)

    def body(w_ref, o_ref):
        o_ref[...] = jnp.concatenate([w_ref[d].astype(F32) for d in range(n)], axis=1).astype(o_ref.dtype)

    return pl.pallas_call(
        body, name=name, grid=(K // tr,),
        in_specs=[pl.BlockSpec((n, tr, G), lambda i: (0, i, 0))], out_specs=pl.BlockSpec((tr, n * G), lambda i: (i, 0)),
        out_shape=jax.ShapeDtypeStruct((K, n * G), w3.dtype),
        compiler_params=_cparams(("parallel",), 6 * _nbytes((tr, n * G), F32)),
    )(w3)


def _split_cols(g, n, out_dtype, name, tr=256):
    K, NG = g.shape
    G = NG // n
    tr = min(tr, K)

    def body(g_ref, o_ref):
        for d in range(n):
            o_ref[d] = g_ref[:, d * G:(d + 1) * G].astype(o_ref.dtype)

    return pl.pallas_call(
        body, name=name, grid=(K // tr,),
        in_specs=[pl.BlockSpec((tr, NG), lambda i: (i, 0))], out_specs=pl.BlockSpec((n, tr, G), lambda i: (0, i, 0)),
        out_shape=jax.ShapeDtypeStruct((n, K, G), out_dtype),
        compiler_params=_cparams(("parallel",), 6 * _nbytes((tr, NG), F32)),
    )(g)


def _put_cols(dst, src, col_blk, name, tr=256):
    T, W = src.shape
    tr = min(tr, T)

    def body(s_ref, d_in, o_ref):
        del d_in
        o_ref[...] = s_ref[...].astype(o_ref.dtype)

    return pl.pallas_call(
        body, name=name, grid=(T // tr,),
        in_specs=[pl.BlockSpec((tr, W), lambda i: (i, 0)), pl.BlockSpec(memory_space=pl.ANY)],
        out_specs=pl.BlockSpec((tr, W), lambda i: (i, col_blk)),
        out_shape=jax.ShapeDtypeStruct(dst.shape, dst.dtype),
        input_output_aliases={1: 0},
        compiler_params=_cparams(("parallel",), 8 * _nbytes((tr, W), F32)),
    )(src, dst)


def _rope_tables(t_lat, ctx_rows, hd):
    half, quarter = hd // 2, hd // 4
    pos = jnp.arange(t_lat)
    row = (pos // GRID_W).astype(F32)
    col = (pos % GRID_W).astype(F32)
    inv_freq = ROPE_THETA ** (-jnp.arange(0, half, 2, dtype=F32) / half)
    ar, ac = row[:, None] * inv_freq[None, :], col[:, None] * inv_freq[None, :]
    cos = jnp.concatenate([jnp.cos(ar), jnp.cos(ar), jnp.cos(ac), jnp.cos(ac)], axis=1)
    sin = jnp.concatenate([-jnp.sin(ar), jnp.sin(ar), -jnp.sin(ac), jnp.sin(ac)], axis=1)
    del quarter
    cos = jnp.concatenate([jnp.ones((ctx_rows, hd), F32), cos], axis=0)
    sin = jnp.concatenate([jnp.zeros((ctx_rows, hd), F32), sin], axis=0)
    return cos, sin


def _partner(y):
    hd = y.shape[1]
    q = hd // 4
    lane = lax.broadcasted_iota(jnp.int32, y.shape, 1)
    return jnp.where((lane % (2 * q)) < q, pltpu.roll(y, hd - q, 1), pltpu.roll(y, q, 1))


def _qk_prep(qkv, qg, kg, cos, sin, n_q, ctx_rows, name, tr=256):
    T = qkv.shape[0]
    hd = qg.shape[1]
    n_kv = ATTN_KV_HEADS
    tr = min(tr, ctx_rows)

    def body(x_ref, qg_ref, kg_ref, c_ref, s_ref, q_ref, k_ref, v_ref):
        cv, sv = c_ref[...], s_ref[...]
        for h in range(n_q + n_kv):
            xh = x_ref[:, h * hd:(h + 1) * hd]
            r = lax.rsqrt(jnp.mean(xh * xh, axis=-1, keepdims=True) + NORM_EPS)
            y = (xh * r) * (qg_ref[...] if h < n_q else kg_ref[...])
            rot = y * cv + _partner(y) * sv
            if h < n_q:
                q_ref[:, h * hd:(h + 1) * hd] = rot.astype(q_ref.dtype)
            else:
                k_ref[:, (h - n_q) * hd:(h - n_q + 1) * hd] = rot.astype(k_ref.dtype)
        v_ref[...] = x_ref[:, (n_q + n_kv) * hd:].astype(v_ref.dtype)

    W = qkv.shape[1]
    return pl.pallas_call(
        body, name=name, grid=(T // tr,),
        in_specs=[pl.BlockSpec((tr, W), lambda i: (i, 0)), pl.BlockSpec((1, hd), lambda i: (0, 0)),
                  pl.BlockSpec((1, hd), lambda i: (0, 0)), pl.BlockSpec((tr, hd), lambda i: (i, 0)),
                  pl.BlockSpec((tr, hd), lambda i: (i, 0))],
        out_specs=[pl.BlockSpec((tr, n_q * hd), lambda i: (i, 0)), pl.BlockSpec((tr, n_kv * hd), lambda i: (i, 0)),
                   pl.BlockSpec((tr, n_kv * hd), lambda i: (i, 0))],
        out_shape=[jax.ShapeDtypeStruct((T, n_q * hd), MXU), jax.ShapeDtypeStruct((T, n_kv * hd), MXU),
                   jax.ShapeDtypeStruct((T, n_kv * hd), MXU)],
        compiler_params=_cparams(("parallel",), 6 * _nbytes((tr, W), F32)),
    )(qkv, qg, kg, cos, sin)


def _qk_prep_bwd(qkv, qg, kg, cos, sin, dq, dk, dv, n_q, ctx_rows, name, tr=256):
    T, W = qkv.shape
    hd = qg.shape[1]
    n_kv = ATTN_KV_HEADS
    tr = min(tr, ctx_rows)

    def body(x_ref, qg_ref, kg_ref, c_ref, s_ref, dq_ref, dk_ref, dv_ref, o_ref, g_ref):
        i = pl.program_id(0)
        cv, sv = c_ref[...], s_ref[...]
        gq = jnp.zeros((1, hd), F32)
        gk = jnp.zeros((1, hd), F32)
        for h in range(n_q + n_kv):
            xh = x_ref[:, h * hd:(h + 1) * hd]
            gain = qg_ref[...] if h < n_q else kg_ref[...]
            drot = (dq_ref[:, h * hd:(h + 1) * hd] if h < n_q else dk_ref[:, (h - n_q) * hd:(h - n_q + 1) * hd]).astype(F32)
            dy = drot * cv + _partner(drot * sv)
            r = lax.rsqrt(jnp.mean(xh * xh, axis=-1, keepdims=True) + NORM_EPS)
            xn = xh * r
            gsum = jnp.sum(dy * xn, axis=0, keepdims=True)
            if h < n_q:
                gq = gq + gsum
            else:
                gk = gk + gsum
            dxn = dy * gain
            o_ref[:, h * hd:(h + 1) * hd] = (r * (dxn - xn * jnp.mean(dxn * xn, axis=-1, keepdims=True))).astype(o_ref.dtype)
        o_ref[:, (n_q + n_kv) * hd:] = dv_ref[...].astype(o_ref.dtype)
        _acc_first(g_ref, jnp.concatenate([gq, gk, jnp.zeros((6, hd), F32)], axis=0), i)

    return pl.pallas_call(
        body, name=name, grid=(T // tr,),
        in_specs=[pl.BlockSpec((tr, W), lambda i: (i, 0)), pl.BlockSpec((1, hd), lambda i: (0, 0)),
                  pl.BlockSpec((1, hd), lambda i: (0, 0)), pl.BlockSpec((tr, hd), lambda i: (i, 0)),
                  pl.BlockSpec((tr, hd), lambda i: (i, 0)), pl.BlockSpec((tr, n_q * hd), lambda i: (i, 0)),
                  pl.BlockSpec((tr, n_kv * hd), lambda i: (i, 0)), pl.BlockSpec((tr, n_kv * hd), lambda i: (i, 0))],
        out_specs=[pl.BlockSpec((tr, W), lambda i: (i, 0)), pl.BlockSpec((8, hd), lambda i: (0, 0))],
        out_shape=[jax.ShapeDtypeStruct((T, W), MXU), jax.ShapeDtypeStruct((8, hd), F32)],
        compiler_params=_cparams(("arbitrary",), 8 * _nbytes((tr, W), F32)),
    )(qkv, qg, kg, cos, sin, dq, dk, dv)


def _attn_scores(q_ref, k_ref, sink_ref, h, qb, ctx_rows, nb, hd, grp):
    scale = hd ** -0.5
    w0 = jnp.clip(qb - 1, 0, nb - 3) * BLK
    w0 = pl.multiple_of(w0, BLK)
    qv = q_ref[...]
    qs = jnp.concatenate([qv[:, g * hd:(g + 1) * hd] for g in range(grp)], axis=0)
    kc = k_ref[0:ctx_rows, :]
    kb = k_ref[pl.ds(w0, 3 * BLK), :]
    s_c = _dot(qs, kc, NT) * scale
    s_b = _dot(qs, kb, NT) * scale
    n = grp * BLK
    qpos = qb * BLK + lax.broadcasted_iota(jnp.int32, (n, 3 * BLK), 0) % BLK
    kpos = w0 + lax.broadcasted_iota(jnp.int32, (n, 3 * BLK), 1)
    ok = (jnp.abs(kpos - qpos) <= ATTN_WINDOW) & (kpos >= ctx_rows) & (qpos >= ctx_rows)
    s_b = jnp.where(ok, s_b, -jnp.inf)
    gi = lax.broadcasted_iota(jnp.int32, (n, 1), 0) // BLK
    sink = jnp.zeros((n, 1), F32)
    for g in range(grp):
        sink = jnp.where(gi == g, sink_ref[h * grp + g], sink)
    m = jnp.maximum(jnp.maximum(jnp.max(s_c, axis=1, keepdims=True), jnp.max(s_b, axis=1, keepdims=True)), sink)
    e_c, e_b, e_s = jnp.exp(s_c - m), jnp.exp(s_b - m), jnp.exp(sink - m)
    inv = 1.0 / (jnp.sum(e_c, axis=1, keepdims=True) + jnp.sum(e_b, axis=1, keepdims=True) + e_s)
    return qs, kc, kb, w0, e_c * inv, e_b * inv, e_s * inv, gi


def _attn_fwd(qr, kr, vb, sinks, n_q, ctx_rows, name):
    T = qr.shape[0]
    n_kv = ATTN_KV_HEADS
    grp = n_q // n_kv
    hd = qr.shape[1] // n_q
    nb = T // BLK

    def body(sink_ref, q_ref, k_ref, v_ref, o_ref):
        h, qb = pl.program_id(0), pl.program_id(1)
        _, _, _, w0, p_c, p_b, _, _ = _attn_scores(q_ref, k_ref, sink_ref, h, qb, ctx_rows, nb, hd, grp)
        o = _dot(p_c.astype(MXU), v_ref[0:ctx_rows, :]) + _dot(p_b.astype(MXU), v_ref[pl.ds(w0, 3 * BLK), :])
        o_ref[...] = jnp.concatenate([o[g * BLK:(g + 1) * BLK] for g in range(grp)], axis=1).astype(o_ref.dtype)

    return pl.pallas_call(
        body, name=name, grid=(n_kv, nb),
        in_specs=[pl.BlockSpec(memory_space=pltpu.SMEM), pl.BlockSpec((BLK, grp * hd), lambda h, i: (i, h)),
                  pl.BlockSpec((T, hd), lambda h, i: (0, h)), pl.BlockSpec((T, hd), lambda h, i: (0, h))],
        out_specs=pl.BlockSpec((BLK, grp * hd), lambda h, i: (i, h)),
        out_shape=jax.ShapeDtypeStruct((T, n_q * hd), MXU),
        compiler_params=_cparams(("parallel", "arbitrary"), 4 * _nbytes((T, hd), MXU) + 24 * _nbytes((grp * BLK, 5 * BLK), F32)),
    )(sinks, qr, kr, vb)


def _attn_bwd(qr, kr, vb, sinks, do, n_q, ctx_rows, name):
    T = qr.shape[0]
    n_kv = ATTN_KV_HEADS
    grp = n_q // n_kv
    hd = qr.shape[1] // n_q
    nb = T // BLK
    scale = hd ** -0.5

    def body(sink_ref, q_ref, k_ref, v_ref, do_ref, dq_ref, dk_ref, dv_ref, ds_ref):
        h, qb = pl.program_id(0), pl.program_id(1)
        qs, kc, kb, w0, p_c, p_b, p_s, gi = _attn_scores(q_ref, k_ref, sink_ref, h, qb, ctx_rows, nb, hd, grp)
        dov = do_ref[...]
        dos = jnp.concatenate([dov[:, g * hd:(g + 1) * hd] for g in range(grp)], axis=0)
        vc = v_ref[0:ctx_rows, :]
        vw = v_ref[pl.ds(w0, 3 * BLK), :]
        dp_c = _dot(dos, vc, NT)
        dp_b = _dot(dos, vw, NT)
        delta = jnp.sum(p_c * dp_c, axis=1, keepdims=True) + jnp.sum(p_b * dp_b, axis=1, keepdims=True)
        ds_c = (p_c * (dp_c - delta) * scale).astype(MXU)
        ds_b = (p_b * (dp_b - delta) * scale).astype(MXU)
        dq = _dot(ds_c, kc) + _dot(ds_b, kb)
        dq_ref[...] = jnp.concatenate([dq[g * BLK:(g + 1) * BLK] for g in range(grp)], axis=1)

        @pl.when(qb == 0)
        def _():
            dk_ref[...] = jnp.zeros(dk_ref.shape, F32)
            dv_ref[...] = jnp.zeros(dv_ref.shape, F32)

        dk_ref[0:ctx_rows, :] += _dot(ds_c, qs, TN)
        dv_ref[0:ctx_rows, :] += _dot(p_c.astype(MXU), dos, TN)
        dk_ref[pl.ds(w0, 3 * BLK), :] += _dot(ds_b, qs, TN)
        dv_ref[pl.ds(w0, 3 * BLK), :] += _dot(p_b.astype(MXU), dos, TN)
        t = -(p_s * delta)
        lane = lax.broadcasted_iota(jnp.int32, (8, LANE), 1)
        part = jnp.zeros((8, LANE), F32)
        for g in range(grp):
            part = jnp.where(lane == g, jnp.sum(jnp.where(gi == g, t, 0.0)), part)
        _acc_first(ds_ref, part, qb)

    return pl.pallas_call(
        body, name=name, grid=(n_kv, nb),
        in_specs=[pl.BlockSpec(memory_space=pltpu.SMEM), pl.BlockSpec((BLK, grp * hd), lambda h, i: (i, h)),
                  pl.BlockSpec((T, hd), lambda h, i: (0, h)), pl.BlockSpec((T, hd), lambda h, i: (0, h)),
                  pl.BlockSpec((BLK, grp * hd), lambda h, i: (i, h))],
        out_specs=[pl.BlockSpec((BLK, grp * hd), lambda h, i: (i, h)), pl.BlockSpec((T, hd), lambda h, i: (0, h)),
                   pl.BlockSpec((T, hd), lambda h, i: (0, h)), pl.BlockSpec((8, LANE), lambda h, i: (h, 0))],
        out_shape=[jax.ShapeDtypeStruct((T, n_q * hd), F32), jax.ShapeDtypeStruct((T, n_kv * hd), F32),
                   jax.ShapeDtypeStruct((T, n_kv * hd), F32), jax.ShapeDtypeStruct((n_kv * 8, LANE), F32)],
        compiler_params=_cparams(("parallel", "arbitrary"), 4 * _nbytes((T, hd), MXU) + 4 * _nbytes((T, hd), F32)
                                 + 40 * _nbytes((grp * BLK, 5 * BLK), F32)),
    )(sinks, qr, kr, vb, do)


def _chunk_order(s, n_chunks, n_ctx, rev):
    if not rev:
        return s
    return jnp.where(s < n_ctx, n_ctx - 1 - s, n_chunks - 1 + n_ctx - s)


def _softplus(x):
    return jnp.maximum(x, 0.0) + jnp.log(1.0 + jnp.exp(-jnp.abs(x)))


def _expand_matrix(n_heads, p, rev):
    e = np.zeros((LANE, n_heads * p), np.float32)
    for h in range(n_heads):
        e[h + (n_heads if rev else 0), h * p:(h + 1) * p] = 1.0
    return jnp.asarray(e, BF16)


def _ssd_chunk_prep(dt_ref, dtb_ref, alog_ref, e_ref, rev):
    dt = _softplus(dt_ref[...] + dtb_ref[...])
    a = -jnp.exp(alog_ref[...])
    li = lax.broadcasted_iota(jnp.int32, (BLK, BLK), 0)
    si = lax.broadcasted_iota(jnp.int32, (BLK, BLK), 1)
    tri = (si >= li) if rev else (si <= li)
    acs = _dot3_rhs(tri.astype(BF16), a * dt)
    dtexp, aexp = _expand2([dt, acs], e_ref[...])
    return dt, a, tri, acs, dtexp, aexp


def _pair_cols(ap):
    lane = lax.broadcasted_iota(jnp.int32, ap.shape, 1)
    apr = pltpu.roll(ap, LANE // 2, 1)
    return jnp.where(lane < LANE // 2, ap, apr), jnp.where(lane < LANE // 2, apr, ap)


def _ssd_scan(xbc, zx, dtb, alog, emat, n_heads, n_ctx, rev, name):
    T, C = xbc.shape
    P = emat.shape[1] // n_heads
    DI = n_heads * P
    GN = (C - DI) // 2
    N = GN // SSD_GROUPS
    n_pairs = DI // LANE
    ppg = n_pairs // SSD_GROUPS
    n_chunks = T // BLK
    hoff = n_heads if rev else 0
    last = 0 if rev else BLK - 1
    dt_blk = zx.shape[1] // LANE - 1
    assert N == LANE and 2 * P == LANE and 2 * n_heads == LANE

    def body(xs_ref, b_ref, c_ref, dt_ref, dtb_ref, alog_ref, e_ref, y_ref, hin_ref, state_ref, xdt_s, xdec_s, aexp_s, at_s):
        s = pl.program_id(0)

        @pl.when(s == 0)
        def _():
            state_ref[...] = jnp.zeros(state_ref.shape, F32)

        dt, a, tri, acs, dtexp, aexp = _ssd_chunk_prep(dt_ref, dtb_ref, alog_ref, e_ref, rev)
        at_s[...] = acs.T
        aexp_s[...] = aexp
        xdt = xs_ref[...] * dtexp
        xdt_s[...] = xdt.astype(MXU)
        xdec_s[...] = (xdt * jnp.exp(aexp[last:last + 1, :] - aexp)).astype(MXU)
        hin_ref[0] = state_ref[...]
        lane = lax.broadcasted_iota(jnp.int32, (BLK, LANE), 1)

        def pair(k, carry):
            col = pl.multiple_of(k * LANE, LANE)
            gcol = pl.multiple_of((k // ppg) * N, N)
            bg = b_ref[:, pl.ds(gcol, N)].astype(MXU)
            cg = c_ref[:, pl.ds(gcol, N)].astype(MXU)
            cb = _dot(cg, bg, NT)
            ap = aexp_s[:, pl.ds(col, LANE)]
            ac0, ac1 = _pair_cols(ap)
            ar0 = at_s[pl.ds(2 * k + hoff, 1), :]
            ar1 = at_s[pl.ds(2 * k + 1 + hoff, 1), :]
            m0 = (cb * jnp.exp(jnp.where(tri, ac0 - ar0, -jnp.inf))).astype(MXU)
            m1 = (cb * jnp.exp(jnp.where(tri, ac1 - ar1, -jnp.inf))).astype(MXU)
            xp = xdt_s[:, pl.ds(col, LANE)]
            zero = jnp.zeros_like(xp)
            xbd = jnp.concatenate([jnp.where(lane < LANE // 2, xp, zero), jnp.where(lane >= LANE // 2, xp, zero)], axis=0)
            yd = _dot(jnp.concatenate([m0, m1], axis=1), xbd)
            ht = state_ref[k]
            yo = _dot(cg, ht.astype(MXU)) * jnp.exp(ap)
            y_ref[:, pl.ds(col, LANE)] = yd + yo
            st = _dot(bg, xdec_s[:, pl.ds(col, LANE)], TN)
            state_ref[k] = jnp.exp(aexp_s[pl.ds(last, 1), pl.ds(col, LANE)]) * ht + st
            return carry

        lax.fori_loop(0, n_pairs, pair, 0, unroll=4)

    order = lambda s: _chunk_order(s, n_chunks, n_ctx, rev)
    return pl.pallas_call(
        body, name=name, grid=(n_chunks,),
        in_specs=[pl.BlockSpec((BLK, DI), lambda s: (order(s), 0)),
                  pl.BlockSpec((BLK, GN), lambda s: (order(s), DI // GN)),
                  pl.BlockSpec((BLK, GN), lambda s: (order(s), DI // GN + 1)),
                  pl.BlockSpec((BLK, LANE), lambda s: (order(s), dt_blk)),
                  pl.BlockSpec((1, LANE), lambda s: (0, 0)), pl.BlockSpec((1, LANE), lambda s: (0, 0)),
                  pl.BlockSpec((2 * LANE, DI), lambda s: (0, 0))],
        out_specs=[pl.BlockSpec((BLK, DI), lambda s: (order(s), 0)),
                   pl.BlockSpec((1, n_pairs, N, LANE), lambda s: (order(s), 0, 0, 0))],
        out_shape=[jax.ShapeDtypeStruct((T, DI), F32), jax.ShapeDtypeStruct((n_chunks, n_pairs, N, LANE), F32)],
        scratch_shapes=[pltpu.VMEM((n_pairs, N, LANE), F32), pltpu.VMEM((BLK, DI), MXU), pltpu.VMEM((BLK, DI), MXU),
                        pltpu.VMEM((BLK, DI), F32), pltpu.VMEM((LANE, BLK), F32)],
        compiler_params=_cparams(("arbitrary",), 20 * _nbytes((BLK, DI), F32)),
    )(xbc, xbc, xbc, zx, dtb, alog, jnp.concatenate([emat, emat], axis=0))


def _ssd_scan_bwd(xbc, zx, dtb, alog, emat, emat_t, dexp, hin, dy, acc, n_heads, n_ctx, rev, name):
    T, C = xbc.shape
    P = emat.shape[1] // n_heads
    DI = n_heads * P
    GN = (C - DI) // 2
    N = GN // SSD_GROUPS
    n_pairs = DI // LANE
    ppg = n_pairs // SSD_GROUPS
    n_chunks = T // BLK
    hoff = n_heads if rev else 0
    last = 0 if rev else BLK - 1
    dt_blk = zx.shape[1] // LANE - 1
    has_acc = acc is not None

    def body(*refs):
        (xs_ref, b_ref, c_ref, dt_ref, dtb_ref, alog_ref, e_ref, et_ref, dexp_ref, hin_ref, dy_ref) = refs[:11]
        n_in = 11
        if has_acc:
            dxbc_in, ddt_in, sums_in = refs[11:14]
            n_in = 14
        dxbc_ref, ddt_ref, sums_ref = refs[n_in:n_in + 3]
        dstate_ref, xdt_s, xdec_s, aexp_s, at_s, dyd_s, z1_s, z3_s, dxdt_s, cdrow_s, rmat_s, cst_s = refs[n_in + 3:]
        s = pl.program_id(0)

        @pl.when(s == 0)
        def _():
            dstate_ref[...] = jnp.zeros(dstate_ref.shape, F32)

        dt, a, tri, acs, dtexp, aexp = _ssd_chunk_prep(dt_ref, dtb_ref, alog_ref, e_ref, rev)
        at_s[...] = acs.T
        aexp_s[...] = aexp
        xsv = xs_ref[...]
        xdt = xsv * dtexp
        xdt_s[...] = xdt.astype(MXU)
        decend = jnp.exp(aexp[last:last + 1, :] - aexp)
        xdec_s[...] = (xdt * decend).astype(MXU)
        dyv = dy_ref[...]
        dyd_s[...] = (dyv * jnp.exp(aexp)).astype(MXU)
        dxbc_ref[:, DI:] = jnp.zeros((BLK, 2 * GN), F32)
        rmat_s[...] = jnp.zeros(rmat_s.shape, F32)
        lane = lax.broadcasted_iota(jnp.int32, (BLK, LANE), 1)
        lo = lane < LANE // 2
        tri_t = (lax.broadcasted_iota(jnp.int32, (BLK, BLK), 0) <= lax.broadcasted_iota(jnp.int32, (BLK, BLK), 1)) if not rev \
            else (lax.broadcasted_iota(jnp.int32, (BLK, BLK), 0) >= lax.broadcasted_iota(jnp.int32, (BLK, BLK), 1))

        def pair(k, carry):
            col = pl.multiple_of(k * LANE, LANE)
            gcol = pl.multiple_of((k // ppg) * N, N)
            bg = b_ref[:, pl.ds(gcol, N)].astype(MXU)
            cg = c_ref[:, pl.ds(gcol, N)].astype(MXU)
            cb = _dot(cg, bg, NT)
            cbt = _dot(bg, cg, NT)
            ap = aexp_s[:, pl.ds(col, LANE)]
            ac0, ac1 = _pair_cols(ap)
            ar0 = at_s[pl.ds(2 * k + hoff, 1), :]
            ar1 = at_s[pl.ds(2 * k + 1 + hoff, 1), :]
            seg0 = jnp.exp(jnp.where(tri, ac0 - ar0, -jnp.inf))
            seg1 = jnp.exp(jnp.where(tri, ac1 - ar1, -jnp.inf))
            segt0 = jnp.exp(jnp.where(tri_t, ar0 - ac0, -jnp.inf))
            segt1 = jnp.exp(jnp.where(tri_t, ar1 - ac1, -jnp.inf))
            dyp = dy_ref[:, pl.ds(col, LANE)].astype(MXU)
            zero = jnp.zeros_like(dyp)
            dy0, dy1 = jnp.where(lo, dyp, zero), jnp.where(lo, zero, dyp)
            dht = dstate_ref[k]
            dhb = dht.astype(MXU)
            ht = hin_ref[0, k]
            mt = jnp.concatenate([(cbt * segt0).astype(MXU), (cbt * segt1).astype(MXU)], axis=1)
            bdh = _dot(bg, dhb)
            dec_p = jnp.exp(aexp_s[pl.ds(last, 1), pl.ds(col, LANE)] - ap)
            dxdt_s[:, pl.ds(col, LANE)] = _dot(mt, jnp.concatenate([dy0, dy1], axis=0)) + dec_p * bdh
            z3_s[:, pl.ds(col, LANE)] = bdh
            cdec = jnp.exp(aexp_s[pl.ds(last, 1), pl.ds(col, LANE)])
            dydp = dyd_s[:, pl.ds(col, LANE)]
            dstate_ref[k] = _dot(cg, dydp, TN) + cdec * dht
            cdrow_s[0:1, pl.ds(col, LANE)] = cdec * jnp.sum(dht * ht, axis=0, keepdims=True)
            z1_s[:, pl.ds(col, LANE)] = _dot(cg, ht.astype(MXU))
            dcg = _dot(dydp, ht.astype(MXU), NT)
            dbg = _dot(xdec_s[:, pl.ds(col, LANE)], dhb, NT)
            xp = xdt_s[:, pl.ds(col, LANE)]
            dg0 = _dot(dy0, xp, NT)
            dg1 = _dot(dy1, xp, NT)
            ds0, ds1 = dg0 * seg0, dg1 * seg1
            w0, w1 = ds0 * cb, ds1 * cb
            dcb = ds0 + ds1
            lane_h = lax.broadcasted_iota(jnp.int32, (BLK, LANE), 1)
            rmat_s[...] += (jnp.where(lane_h == 2 * k + hoff, jnp.sum(w0, axis=1, keepdims=True), 0.0)
                            + jnp.where(lane_h == 2 * k + 1 + hoff, jnp.sum(w1, axis=1, keepdims=True), 0.0))
            cst_s[pl.ds(2 * k + hoff, 1), :] = jnp.sum(w0, axis=0, keepdims=True)
            cst_s[pl.ds(2 * k + 1 + hoff, 1), :] = jnp.sum(w1, axis=0, keepdims=True)
            dcbb = dcb.astype(MXU)
            dxbc_ref[:, pl.ds(DI + GN + gcol, N)] += dcg + _dot(dcbb, bg)
            dxbc_ref[:, pl.ds(DI + gcol, N)] += dbg + _dot(dcbb, cg, TN)
            return carry

        cst_s[...] = jnp.zeros(cst_s.shape, F32)
        lax.fori_loop(0, n_pairs, pair, 0, unroll=4)

        etv = et_ref[...]
        dxdt = dxdt_s[...]
        qfull = xdt * decend * z3_s[...]
        tot8 = jnp.concatenate([jnp.sum(qfull, axis=0, keepdims=True) + cdrow_s[0:1, :], jnp.zeros((7, DI), F32)], axis=0)
        z1q, z2, tot = _dot2_stacked([dyv * (z1_s[...] * jnp.exp(aexp)) - qfull, dxdt * xsv, tot8], etv)
        dacs = rmat_s[...] - cst_s[...].T + z1q
        rowi = lax.broadcasted_iota(jnp.int32, (BLK, LANE), 0)
        dacs = dacs + jnp.where(rowi == last, tot[0:1, :], 0.0)
        d_a = _dot3_rhs(tri_t.astype(BF16), dacs)
        ddt = a * d_a + z2
        x_raw = dt_ref[...] + dtb_ref[...]
        ddt_raw = ddt * jax.nn.sigmoid(x_raw)
        lane_l = lax.broadcasted_iota(jnp.int32, (BLK, LANE), 1)
        mine = (lane_l >= hoff) & (lane_l < hoff + n_heads)
        ddt_raw = jnp.where(mine, ddt_raw, 0.0)
        part = jnp.concatenate([jnp.sum(jnp.where(mine, dt * d_a, 0.0), axis=0, keepdims=True) * a,
                                jnp.sum(ddt_raw, axis=0, keepdims=True), jnp.zeros((6, LANE), F32)], axis=0)
        dxs = dxdt * dtexp
        if has_acc:
            dxbc_ref[:, 0:DI] = dxs + dxbc_in[:, 0:DI]
            dxbc_ref[:, DI:] += dxbc_in[:, DI:]
            ddt_ref[...] = ddt_raw + ddt_in[...]
            part = part + jnp.where(s == 0, sums_in[...], 0.0)
        else:
            dxbc_ref[:, 0:DI] = dxs + dyv * dexp_ref[...]
            ddt_ref[...] = ddt_raw
        _acc_first(sums_ref, part, s)

    order = lambda s: _chunk_order(n_chunks - 1 - s, n_chunks, n_ctx, rev)
    in_specs = [pl.BlockSpec((BLK, DI), lambda s: (order(s), 0)),
                pl.BlockSpec((BLK, GN), lambda s: (order(s), DI // GN)),
                pl.BlockSpec((BLK, GN), lambda s: (order(s), DI // GN + 1)),
                pl.BlockSpec((BLK, LANE), lambda s: (order(s), dt_blk)),
                pl.BlockSpec((1, LANE), lambda s: (0, 0)), pl.BlockSpec((1, LANE), lambda s: (0, 0)),
                pl.BlockSpec((2 * LANE, DI), lambda s: (0, 0)), pl.BlockSpec((DI, LANE), lambda s: (0, 0)),
                pl.BlockSpec((1, DI), lambda s: (0, 0)),
                pl.BlockSpec((1, n_pairs, N, LANE), lambda s: (order(s), 0, 0, 0)),
                pl.BlockSpec((BLK, DI), lambda s: (order(s), 0))]
    args = [xbc, xbc, xbc, zx, dtb, alog, jnp.concatenate([emat, emat], axis=0), emat_t, dexp, hin, dy]
    if has_acc:
        in_specs += [pl.BlockSpec((BLK, C), lambda s: (order(s), 0)), pl.BlockSpec((BLK, LANE), lambda s: (order(s), 0)),
                     pl.BlockSpec((8, LANE), lambda s: (0, 0))]
        args += list(acc)
    return pl.pallas_call(
        body, name=name, grid=(n_chunks,),
        in_specs=in_specs,
        out_specs=[pl.BlockSpec((BLK, C), lambda s: (order(s), 0)), pl.BlockSpec((BLK, LANE), lambda s: (order(s), 0)),
                   pl.BlockSpec((8, LANE), lambda s: (0, 0))],
        out_shape=[jax.ShapeDtypeStruct((T, C), F32), jax.ShapeDtypeStruct((T, LANE), F32), jax.ShapeDtypeStruct((8, LANE), F32)],
        scratch_shapes=[pltpu.VMEM((n_pairs, N, LANE), F32), pltpu.VMEM((BLK, DI), MXU), pltpu.VMEM((BLK, DI), MXU),
                        pltpu.VMEM((BLK, DI), F32), pltpu.VMEM((LANE, BLK), F32), pltpu.VMEM((BLK, DI), MXU),
                        pltpu.VMEM((BLK, DI), F32), pltpu.VMEM((BLK, DI), F32), pltpu.VMEM((BLK, DI), F32),
                        pltpu.VMEM((8, DI), F32), pltpu.VMEM((BLK, LANE), F32), pltpu.VMEM((LANE, BLK), F32)],
        compiler_params=_cparams(("arbitrary",), 36 * _nbytes((BLK, DI), F32)),
    )(*args)


def _ssd_finish(yf, yb, xbc, zx, dexp, nw, name, tr=256):
    T, DI = yf.shape
    tr = min(tr, T)

    def body(yf_ref, yb_ref, xs_ref, z_ref, d_ref, nw_ref, o_ref):
        y = yf_ref[...] + yb_ref[...] + xs_ref[...] * d_ref[...]
        gt = y * _silu(z_ref[...])
        r = lax.rsqrt(jnp.mean(gt * gt, axis=-1, keepdims=True) + NORM_EPS)
        o_ref[...] = ((gt * r) * nw_ref[...]).astype(o_ref.dtype)

    row = pl.BlockSpec((tr, DI), lambda i: (i, 0))
    vec = pl.BlockSpec((1, DI), lambda i: (0, 0))
    return pl.pallas_call(
        body, name=name, grid=(T // tr,), in_specs=[row, row, row, row, vec, vec], out_specs=row,
        out_shape=jax.ShapeDtypeStruct((T, DI), MXU),
        compiler_params=_cparams(("parallel",), 16 * _nbytes((tr, DI), F32)),
    )(yf, yb, xbc, zx, dexp, nw)


def _ssd_finish_bwd(yf, yb, xbc, zx, dexp, nw, do, emat_t, dzx_shape, name, tr=64):
    T, DI = yf.shape
    tr = min(tr, T)
    n_steps = T // tr

    def body(yf_ref, yb_ref, xs_ref, z_ref, d_ref, nw_ref, do_ref, et_ref, dy_ref, dz_ref, sums_ref, dd_ref):
        i = pl.program_id(0)
        xs = xs_ref[...]
        zv = z_ref[...]
        y = yf_ref[...] + yb_ref[...] + xs * d_ref[...]
        sz = _silu(zv)
        gt = y * sz
        r = lax.rsqrt(jnp.mean(gt * gt, axis=-1, keepdims=True) + NORM_EPS)
        gn = gt * r
        dov = do_ref[...].astype(F32)
        dgn = dov * nw_ref[...]
        dgt = r * (dgn - gn * jnp.mean(dgn * gn, axis=-1, keepdims=True))
        dy = dgt * sz
        dy_ref[...] = dy
        dz_ref[...] = (dgt * y * _dsilu(zv)).astype(dz_ref.dtype)
        part = jnp.concatenate([jnp.sum(dov * gn, axis=0, keepdims=True), jnp.sum(dy * xs, axis=0, keepdims=True),
                                jnp.zeros((6, DI), F32)], axis=0)
        _acc_first(sums_ref, part, i)

        @pl.when(i == n_steps - 1)
        def _():
            dd_ref[...] = _dot3_lhs(sums_ref[...], et_ref[...])

    row = pl.BlockSpec((tr, DI), lambda i: (i, 0))
    vec = pl.BlockSpec((1, DI), lambda i: (0, 0))
    return pl.pallas_call(
        body, name=name, grid=(n_steps,),
        in_specs=[row, row, row, row, vec, vec, row, pl.BlockSpec((DI, LANE), lambda i: (0, 0))],
        out_specs=[row, row, pl.BlockSpec((8, DI), lambda i: (0, 0)), pl.BlockSpec((8, LANE), lambda i: (0, 0))],
        out_shape=[jax.ShapeDtypeStruct((T, DI), F32), jax.ShapeDtypeStruct(dzx_shape, MXU), jax.ShapeDtypeStruct((8, DI), F32),
                   jax.ShapeDtypeStruct((8, LANE), F32)],
        compiler_params=_cparams(("arbitrary",), 40 * _nbytes((tr, DI), F32)),
    )(yf, yb, xbc, zx, dexp, nw, do, emat_t)


def _loss_head(xf, tgt, ctx_rows, name, tr=256):
    T, D = xf.shape
    tr = min(tr, ctx_rows)
    n_ctx = ctx_rows // tr

    def body(x_ref, t_ref, dx_ref, l_ref):
        i = pl.program_id(0)

        @pl.when(i < n_ctx)
        def _():
            dx_ref[...] = jnp.zeros(dx_ref.shape, F32)

        @pl.when(i == 0)
        def _():
            l_ref[...] = jnp.zeros(l_ref.shape, F32)

        @pl.when(i >= n_ctx)
        def _():
            e = x_ref[...] - t_ref[...]
            dx_ref[...] = e * (1.0 / D)
            l_ref[...] += 0.5 * jnp.sum(jnp.mean(e * e, axis=-1, keepdims=True))

    return pl.pallas_call(
        body, name=name, grid=(T // tr,),
        in_specs=[pl.BlockSpec((tr, D), lambda i: (i, 0)), pl.BlockSpec((tr, D), lambda i: (jnp.maximum(i - n_ctx, 0), 0))],
        out_specs=[pl.BlockSpec((tr, D), lambda i: (i, 0)), pl.BlockSpec((8, LANE), lambda i: (0, 0))],
        out_shape=[jax.ShapeDtypeStruct((T, D), F32), jax.ShapeDtypeStruct((8, LANE), F32)],
        compiler_params=_cparams(("arbitrary",), 10 * _nbytes((tr, D), F32)),
    )(xf, tgt)


def _adamw_math(w, g, m, v):
    m2 = ADAM_B1 * m + (1.0 - ADAM_B1) * g
    v2 = ADAM_B2 * v + (1.0 - ADAM_B2) * (g * g)
    m_hat = m2 / (1.0 - ADAM_B1 ** ADAM_STEP)
    v_hat = v2 / (1.0 - ADAM_B2 ** ADAM_STEP)
    delta = -ADAM_LR * (m_hat / (jnp.sqrt(v_hat) + ADAM_EPS) + ADAM_WD * w)
    return delta, m2, v2


def _row_tile(rows, target):
    if rows <= target:
        return rows
    t = target - target % 8
    while rows % t:
        t -= 8
    return t


def _adamw(parts, w, m, v, name, tr=128):
    n, L, rows, cols = parts.shape
    tr = _row_tile(rows, tr)

    def body(p_ref, w_ref, m_ref, v_ref, g_ref, d_ref, m2_ref, v2_ref):
        g = p_ref[0, 0].astype(F32)
        for q in range(1, n):
            g = g + p_ref[q, 0].astype(F32)
        d, m2, v2 = _adamw_math(w_ref[0], g, m_ref[0], v_ref[0])
        g_ref[0], d_ref[0], m2_ref[0], v2_ref[0] = g, d, m2, v2

    blk = pl.BlockSpec((1, tr, cols), lambda l, i: (l, i, 0))
    shp = jax.ShapeDtypeStruct((L, rows, cols), F32)
    return pl.pallas_call(
        body, name=name, grid=(L, rows // tr),
        in_specs=[pl.BlockSpec((n, 1, tr, cols), lambda l, i: (0, l, i, 0)), blk, blk, blk],
        out_specs=[blk, blk, blk, blk], out_shape=[shp, shp, shp, shp],
        compiler_params=_cparams(("parallel", "parallel"), 2 * (n + 8) * _nbytes((tr, cols), F32)),
    )(parts, w, m, v)


def _adamw_small(bufs, w, m, v, name):
    n, R, _ = bufs.shape

    def body(b_ref, w_ref, m_ref, v_ref, g_ref, d_ref, m2_ref, v2_ref):
        g = b_ref[0]
        for q in range(1, n):
            g = g + b_ref[q]
        d, m2, v2 = _adamw_math(w_ref[...], g, m_ref[...], v_ref[...])
        g_ref[...], d_ref[...], m2_ref[...], v2_ref[...] = g, d, m2, v2

    vm = pl.BlockSpec(memory_space=pltpu.VMEM)
    shp = jax.ShapeDtypeStruct((R, LANE), F32)
    return pl.pallas_call(body, name=name, in_specs=[vm, vm, vm, vm], out_specs=[vm, vm, vm, vm],
                          out_shape=[shp, shp, shp, shp],
                          compiler_params=pltpu.CompilerParams(vmem_limit_bytes=32 * 1024 * 1024))(bufs, w, m, v)


def _add_blocks(g, r1, c_idx, name, tr=128):
    _, rows, cols = g.shape
    tr = _row_tile(rows, tr)

    def body(c_ref, g_ref, r_ref, p_ref):
        del c_ref
        p_ref[...] = (g_ref[...] + r_ref[...]).astype(p_ref.dtype)

    return pl.pallas_call(
        body, name=name,
        grid_spec=pltpu.PrefetchScalarGridSpec(
            num_scalar_prefetch=1, grid=(NDEV // 2, rows // tr),
            in_specs=[pl.BlockSpec((1, tr, cols), lambda q, i, c: (2 * q + c[0], i, 0)),
                      pl.BlockSpec((1, tr, cols), lambda q, i, c: (q, i, 0))],
            out_specs=pl.BlockSpec((1, tr, cols), lambda q, i, c: (q, i, 0))),
        out_shape=jax.ShapeDtypeStruct((NDEV // 2, rows, cols), XFER),
        compiler_params=_cparams(("parallel", "parallel"), 8 * _nbytes((tr, cols), F32)),
    )(c_idx, g, r1)


def _me():
    return lax.axis_index("x"), lax.axis_index("y"), lax.axis_index("c")


def _flip(v, bit):
    return 1 - v if bit else v


def _peer(k):
    x, y, c = _me()
    return _flip(x, (k >> 2) & 1), _flip(y, (k >> 1) & 1), _flip(c, k & 1)


def _dev_index(p):
    return 4 * p[0] + 2 * p[1] + p[2]


def _chip_index(p):
    return 2 * p[0] + p[1]


def _small_allgather(v, name):
    R, C = v.shape

    def body(v_ref, out_ref, send_sems, recv_sems, loc_sem):
        me = _dev_index(_me())
        mine = pltpu.make_async_copy(v_ref, out_ref.at[me], loc_sem)
        mine.start()
        sends = []
        for k in range(1, NDEV):
            cp = pltpu.make_async_remote_copy(src_ref=v_ref, dst_ref=out_ref.at[me], send_sem=send_sems.at[k - 1],
                                              recv_sem=recv_sems.at[k - 1], device_id=_peer(k), device_id_type=MESH_ID)
            cp.start()
            sends.append(cp)
        for k in range(1, NDEV):
            pltpu.make_async_remote_copy(src_ref=v_ref, dst_ref=out_ref.at[_dev_index(_peer(k))], send_sem=send_sems.at[k - 1],
                                         recv_sem=recv_sems.at[k - 1], device_id=_peer(k), device_id_type=MESH_ID).wait_recv()
        for cp in sends:
            cp.wait_send()
        mine.wait()

    vm = pl.BlockSpec(memory_space=pltpu.VMEM)
    return pl.pallas_call(
        body, name=name, in_specs=[vm], out_specs=vm, out_shape=jax.ShapeDtypeStruct((NDEV, R, C), F32),
        scratch_shapes=[pltpu.SemaphoreType.DMA((NDEV - 1,)), pltpu.SemaphoreType.DMA((NDEV - 1,)), pltpu.SemaphoreType.DMA(())],
        compiler_params=pltpu.CompilerParams(vmem_limit_bytes=48 * 1024 * 1024),
    )(v)


def _allgather_weights(arrs, name):
    n = len(arrs)

    def body(*refs):
        x_refs, out_refs = refs[:n], refs[n:2 * n]
        send_sems, recv_sems, loc_sems = refs[2 * n:]
        x, y, c = _me()
        me, sib = (x, y, c), (x, y, 1 - c)
        chips = [(1 - x, y), (x, 1 - y), (1 - x, 1 - y)]

        def copy(a, k, block, to, src=None):
            dst = out_refs[a].at[_dev_index(block)]
            return pltpu.make_async_remote_copy(src_ref=dst if src is None else src, dst_ref=dst, send_sem=send_sems.at[a, k],
                                                recv_sem=recv_sems.at[a, k], device_id=to, device_id_type=MESH_ID)

        locs, first, passed = [], [], []
        for a in range(n):
            lc = pltpu.make_async_copy(x_refs[a], out_refs[a].at[_dev_index(me)], loc_sems.at[a])
            lc.start()
            locs.append(lc)
            cps = [copy(a, 0, me, sib, src=x_refs[a])] + [copy(a, 1 + j, me, (*chip, c), src=x_refs[a]) for j, chip in enumerate(chips)]
            for cp in cps:
                cp.start()
            first += cps
        for j, chip in enumerate(chips):
            for a in range(n):
                copy(a, 1 + j, (*chip, c), me).wait_recv()
                cp = copy(a, 4 + j, (*chip, c), sib)
                cp.start()
                passed.append(cp)
        for a in range(n):
            copy(a, 0, sib, me).wait_recv()
            for j, chip in enumerate(chips):
                copy(a, 4 + j, (*chip, 1 - c), me).wait_recv()
        for cp in first + passed:
            cp.wait_send()
        for lc in locs:
            lc.wait()

    hbm = pl.BlockSpec(memory_space=pl.ANY)
    return pl.pallas_call(
        body, name=name, in_specs=[hbm] * n, out_specs=[hbm] * n,
        out_shape=[jax.ShapeDtypeStruct((NDEV,) + a.shape, a.dtype) for a in arrs],
        scratch_shapes=[pltpu.SemaphoreType.DMA((n, 7)), pltpu.SemaphoreType.DMA((n, 7)), pltpu.SemaphoreType.DMA((n,))],
    )(*arrs)


def _rs_sibling(gs, name):
    n = len(gs)

    def body(*refs):
        g_refs, r_refs = refs[:n], refs[n:2 * n]
        send_sems, recv_sems = refs[2 * n:]
        x, y, c = _me()
        sib = (x, y, 1 - c)
        sends = []
        for a in range(n):
            for q in range(NDEV // 2):
                cp = pltpu.make_async_remote_copy(src_ref=g_refs[a].at[2 * q + (1 - c)], dst_ref=r_refs[a].at[q],
                                                  send_sem=send_sems.at[a, q], recv_sem=recv_sems.at[a, q],
                                                  device_id=sib, device_id_type=MESH_ID)
                cp.start()
                sends.append(cp)
        for cp in sends:
            cp.wait_recv()
        for cp in sends:
            cp.wait_send()

    hbm = pl.BlockSpec(memory_space=pl.ANY)
    return pl.pallas_call(
        body, name=name, in_specs=[hbm] * n, out_specs=[hbm] * n,
        out_shape=[jax.ShapeDtypeStruct((NDEV // 2,) + g.shape[1:], g.dtype) for g in gs],
        scratch_shapes=[pltpu.SemaphoreType.DMA((n, NDEV // 2)), pltpu.SemaphoreType.DMA((n, NDEV // 2))],
    )(*gs)


def _rs_chips(ps, groups, name):
    n = len(ps)
    where = {}
    for gi, grp in enumerate(groups):
        for li, a in enumerate(grp):
            where[a] = (gi, li)
    ng = len(groups)

    def body(*refs):
        p_refs, r_refs = refs[:n], refs[n:n + ng]
        send_sems, recv_sems, loc_sems = refs[n + ng:]
        x, y, c = _me()
        mychip = _chip_index((x, y))
        chips = [(1 - x, y), (x, 1 - y), (1 - x, 1 - y)]
        sends, locs = [], []
        for a in range(n):
            gi, li = where[a]
            lc = pltpu.make_async_copy(p_refs[a].at[mychip], r_refs[gi].at[mychip, li], loc_sems.at[a])
            lc.start()
            locs.append(lc)
            for j, chip in enumerate(chips):
                cp = pltpu.make_async_remote_copy(src_ref=p_refs[a].at[_chip_index(chip)], dst_ref=r_refs[gi].at[mychip, li],
                                                  send_sem=send_sems.at[a, j], recv_sem=recv_sems.at[a, j],
                                                  device_id=(*chip, c), device_id_type=MESH_ID)
                cp.start()
                sends.append(cp)
        for a in range(n):
            gi, li = where[a]
            for j, chip in enumerate(chips):
                pltpu.make_async_remote_copy(src_ref=p_refs[a].at[mychip], dst_ref=r_refs[gi].at[_chip_index(chip), li],
                                             send_sem=send_sems.at[a, j], recv_sem=recv_sems.at[a, j],
                                             device_id=(*chip, c), device_id_type=MESH_ID).wait_recv()
        for cp in sends:
            cp.wait_send()
        for lc in locs:
            lc.wait()

    hbm = pl.BlockSpec(memory_space=pl.ANY)
    return pl.pallas_call(
        body, name=name, in_specs=[hbm] * n, out_specs=[hbm] * ng,
        out_shape=[jax.ShapeDtypeStruct((NDEV // 2, len(grp)) + ps[grp[0]].shape[1:], ps[grp[0]].dtype) for grp in groups],
        scratch_shapes=[pltpu.SemaphoreType.DMA((n, 3)), pltpu.SemaphoreType.DMA((n, 3)), pltpu.SemaphoreType.DMA((n,))],
    )(*ps)


HBM_SPEC = pl.BlockSpec(memory_space=pltpu.HBM)
SEM_SPEC = pl.BlockSpec(memory_space=pltpu.SEMAPHORE)
DATAFLOW = pltpu.SideEffectType.DATAFLOW_SIDE_EFFECTING


def _xfer_copy(src_ref, land_ref, sems, a, k, layer, scatter, arriving):
    send_sems, recv_sems = sems
    me, peer = _dev_index(_me()), _dev_index(_peer(k))
    src = src_ref.at[peer] if scatter else src_ref
    slot = peer if arriving else me
    dst = land_ref.at[slot] if layer is None else land_ref.at[slot, layer]
    return pltpu.make_async_remote_copy(src_ref=src, dst_ref=dst, send_sem=send_sems.at[a * (NDEV - 1) + k - 1],
                                        recv_sem=recv_sems.at[a * (NDEV - 1) + k - 1], device_id=_peer(k),
                                        device_id_type=MESH_ID)


def _xfer_start(srcs, lands, layers, scatter, name):
    n = len(srcs)

    def body(*refs):
        src_refs, land_refs = refs[:n], refs[n:2 * n]
        sems = refs[2 * n], refs[2 * n + 1]
        token = refs[-1]
        for a in range(n):
            for k in range(1, NDEV):
                _xfer_copy(src_refs[a], land_refs[a], sems, a, k, layers[a], scatter, False).start()
        token[...] = jnp.zeros(token.shape, token.dtype)

    ops = [pltpu.with_memory_space_constraint(t, pltpu.HBM) for t in list(srcs) + list(lands)]
    n_sem = n * (NDEV - 1)
    res = pl.pallas_call(
        body, name=name,
        out_shape=(pltpu.SemaphoreType.DMA((n_sem,)), pltpu.SemaphoreType.DMA((n_sem,)),
                   *[pltpu.HBM(t.shape, t.dtype) for t in ops], jax.ShapeDtypeStruct((8, LANE), F32)),
        in_specs=[HBM_SPEC] * (2 * n),
        out_specs=(SEM_SPEC, SEM_SPEC, *[HBM_SPEC] * (2 * n), pl.BlockSpec(memory_space=pltpu.VMEM)),
        input_output_aliases={i: 2 + i for i in range(2 * n)},
        compiler_params=pltpu.CompilerParams(has_side_effects=DATAFLOW),
    )(*ops)
    return res[0], res[1], list(res[2:2 + n]), list(res[2 + n:2 + 2 * n]), res[-1]


def _xfer_wait(send_sems, recv_sems, srcs, lands, layers, scatter, after, name):
    n = len(srcs)

    def body(*refs):
        src_refs, land_refs = refs[:n], refs[n:2 * n]
        sems = refs[2 * n], refs[2 * n + 1]
        for a in range(n):
            for k in range(1, NDEV):
                _xfer_copy(src_refs[a], land_refs[a], sems, a, k, layers[a], scatter, False).wait_send()
                _xfer_copy(src_refs[a], land_refs[a], sems, a, k, layers[a], scatter, True).wait_recv()

    ops = list(srcs) + list(lands)
    res = pl.pallas_call(
        body, name=name,
        out_shape=tuple(pltpu.HBM(t.shape, t.dtype) for t in ops),
        in_specs=[HBM_SPEC] * (2 * n) + [SEM_SPEC, SEM_SPEC, pl.BlockSpec(memory_space=pl.ANY)],
        out_specs=tuple([HBM_SPEC] * (2 * n)),
        input_output_aliases={i: i for i in range(2 * n)},
        compiler_params=pltpu.CompilerParams(has_side_effects=DATAFLOW),
    )(*ops, send_sems, recv_sems, after)
    return list(res[n:])


HI = lax.Precision.HIGHEST
MOD_ROWS = 16


def _col_tile(n, target=512):
    return target if n % target == 0 else n


def _modulation(s_in, ada_w, b_loc, name):
    L, D, nl = ada_w.shape
    tn = _col_tile(nl)

    def body(s_ref, w_ref, b_ref, o_ref):
        o_ref[0] = jnp.dot(_silu(s_ref[...]), w_ref[0], preferred_element_type=F32, precision=HI) + b_ref[0]

    return pl.pallas_call(
        body, name=name, grid=(L, nl // tn),
        in_specs=[pl.BlockSpec((MOD_ROWS, D), lambda l, j: (0, 0)), pl.BlockSpec((1, D, tn), lambda l, j: (l, 0, j)),
                  pl.BlockSpec((1, 1, tn), lambda l, j: (l, 0, j))],
        out_specs=pl.BlockSpec((1, MOD_ROWS, tn), lambda l, j: (l, 0, j)),
        out_shape=jax.ShapeDtypeStruct((L, MOD_ROWS, nl), F32),
        compiler_params=_cparams(("parallel", "parallel"), 4 * _nbytes((D, tn), F32)),
    )(s_in, ada_w, b_loc)


def _modulation_bwd(s_in, dml, dmc, ada_w, name):
    L, D, nl = ada_w.shape
    tn = _col_tile(nl)

    def body(s_ref, dml_ref, dmc_ref, w_ref, g_ref, pc_ref):
        l, j = pl.program_id(0), pl.program_id(1)
        a = _silu(s_ref[...])
        tot = dmc_ref[0, 0]
        for d in range(1, NDEV):
            tot = tot + dmc_ref[d, 0]
        row = lax.broadcasted_iota(jnp.int32, (MOD_ROWS, tn), 0)
        dm = jnp.where(row == NDEV, tot, dml_ref[:, 0, 0, :])
        g_ref[0] = lax.dot_general(a, dm, TN, preferred_element_type=F32, precision=HI)
        tot8 = jnp.where(lax.broadcasted_iota(jnp.int32, (8, tn), 0) == 0, tot, 0.0)
        part = lax.dot_general(tot8, w_ref[0], NT, preferred_element_type=F32, precision=HI)

        @pl.when((l == 0) & (j == 0))
        def _():
            pc_ref[...] = part

        @pl.when((l != 0) | (j != 0))
        def _():
            pc_ref[...] += part

    return pl.pallas_call(
        body, name=name, grid=(L, nl // tn),
        in_specs=[pl.BlockSpec((MOD_ROWS, D), lambda l, j: (0, 0)), pl.BlockSpec((MOD_ROWS, 1, 1, tn), lambda l, j: (0, l, 0, j)),
                  pl.BlockSpec((NDEV, 1, 1, tn), lambda l, j: (0, l, 0, j)), pl.BlockSpec((1, D, tn), lambda l, j: (l, 0, j))],
        out_specs=[pl.BlockSpec((1, D, tn), lambda l, j: (l, 0, j)), pl.BlockSpec((8, D), lambda l, j: (0, 0))],
        out_shape=[jax.ShapeDtypeStruct((L, D, nl), F32), jax.ShapeDtypeStruct((8, D), F32)],
        compiler_params=_cparams(("arbitrary", "arbitrary"), 8 * _nbytes((D, tn), F32)),
    )(s_in, dml.reshape(MOD_ROWS, L, 1, nl), dmc.reshape(NDEV, L, 1, nl), ada_w)


def _cctx_update(bufs, c_ctx, m, v, name):
    n, R, _ = bufs.shape

    def body(b_ref, w_ref, m_ref, v_ref, g_ref, d_ref, m2_ref, v2_ref):
        g = b_ref[0]
        for q in range(1, n):
            g = g + b_ref[q]
        g = g * _dsilu(w_ref[...])
        d, m2, v2 = _adamw_math(w_ref[...], g, m_ref[...], v_ref[...])
        g_ref[...], d_ref[...], m2_ref[...], v2_ref[...] = g, d, m2, v2

    vm = pl.BlockSpec(memory_space=pltpu.VMEM)
    shp = jax.ShapeDtypeStruct((R, LANE), F32)
    return pl.pallas_call(body, name=name, in_specs=[vm, vm, vm, vm], out_specs=[vm, vm, vm, vm],
                          out_shape=[shp, shp, shp, shp])(bufs, c_ctx, m, v)


def _pack(arrs):
    flat = jnp.concatenate([a.reshape(-1).astype(F32) for a in arrs])
    n = flat.shape[0]
    total = -(-n // (8 * LANE)) * (8 * LANE)
    return jnp.pad(flat, (0, total - n)).reshape(total // LANE, LANE)


def _unpack(buf, shapes):
    lead = buf.shape[:-2]
    flat = buf.reshape(lead + (-1,))
    out, off = [], 0
    for s in shapes:
        n = int(np.prod(s))
        out.append(flat[..., off:off + n].reshape(lead + tuple(s)))
        off += n
    return out


WEIGHTS = ['c_ctx', 'ada_w', 'ada_b', 'norm1_w', 'norm2_w', 'ssd_w_in', 'ssd_conv_w', 'ssd_conv_b', 'ssd_dt_bias_f',
           'ssd_dt_bias_b', 'ssd_a_log_f', 'ssd_a_log_b', 'ssd_d', 'ssd_norm_w', 'ssd_w_out', 'attn_w_qkv', 'attn_q_gain',
           'attn_k_gain', 'attn_sinks', 'attn_w_o', 'ffn_w_up', 'ffn_conv_w', 'ffn_conv_b', 'ffn_w_down']
SMALL = ['ada_b', 'norm1_w', 'norm2_w', 'ssd_conv_b', 'ssd_dt_bias_f', 'ssd_dt_bias_b', 'ssd_a_log_f', 'ssd_a_log_b', 'ssd_d',
         'ssd_norm_w', 'attn_q_gain', 'attn_k_gain', 'attn_sinks', 'ffn_conv_b']
BIG = ['ssd_w_in', 'ssd_w_out', 'attn_w_qkv', 'attn_w_o', 'ffn_w_up', 'ffn_w_down']


def _step(x, c, ctx, w, tgt, m, v):
    xi, yi, ci = _me()
    me = 4 * xi + 2 * yi + ci
    t_lat, D = x.shape[1], x.shape[2]
    ctx_rows = ctx.shape[1]
    T = ctx_rows + t_lat
    L, n_ssd, n_att = w['norm1_w'].shape[0], w['ssd_d'].shape[0], w['attn_sinks'].shape[0]
    H, DI, XBC = w['ssd_d'].shape[1], w['ssd_norm_w'].shape[1], w['ssd_conv_b'].shape[1]
    P = DI // H
    IN = w['ssd_w_in'].shape[2] * NDEV
    hd, n_q = w['attn_q_gain'].shape[1], w['attn_sinks'].shape[1]
    F2 = w['ffn_conv_b'].shape[1]
    G = F2 // NDEV
    nl = w['ada_w'].shape[2]
    ncc = ctx_rows // BLK
    perm = [_ffn_perm(s) for s in range(NDEV)]
    inv = [perm.index(d) for d in range(NDEV)]

    def reorder(t, order):
        return jnp.concatenate([t[..., o * G:(o + 1) * G] for o in order], axis=-1)

    def interleave(t):
        return reorder(t, perm)

    def deinterleave(t):
        return reorder(t, inv)

    def layer_weights(i):
        mixer = ['ssd_w_in', 'ssd_w_out'] if i % 2 == 0 else ['attn_w_qkv', 'attn_w_o']
        return [(n, i // 2) for n in mixer] + [('ffn_w_up', i), ('ffn_w_down', i)]

    def gather_start(keys, tag, zero):
        srcs = [(w[n][j] + zero).astype(MXU) for n, j in keys]
        lands = [lax.dynamic_update_index_in_dim(lax.empty((NDEV,) + t.shape, t.dtype), t, me, 0) for t in srcs]
        ss, rs, srcs, lands, tok = _xfer_start(srcs, lands, [None] * len(srcs), False, f"gather_start_{tag}")
        return (ss, rs, srcs, lands, tag), tok[0, 0]

    def gather_wait(p, after):
        return _xfer_wait(*p[:4], [None] * len(p[2]), False, after, f"gather_wait_{p[4]}")

    first, tok = gather_start(layer_weights(0)[:1], "0a", jnp.zeros((), F32))
    pending, tok = gather_start(layer_weights(0)[1:], "0b", tok)

    shapes_a = [(D,), w['ssd_conv_w'].shape, w['ffn_conv_w'].shape]
    g_a = _small_allgather(_pack([c[0], w['ssd_conv_w'], w['ffn_conv_w']]), "gather_cond")
    c_all, scw_all, fcw_all = _unpack(g_a, shapes_a)
    ssd_cw = scw_all.transpose(1, 2, 0, 3).reshape(n_ssd, 3, XBC)
    ffn_cw = jnp.concatenate([fcw_all[d] for d in perm], axis=-1)
    ffn_cb = interleave(w['ffn_conv_b'])[:, None, :]

    s_in = jnp.concatenate([c_all, w['c_ctx'][None], jnp.zeros((MOD_ROWS - NDEV - 1, D), F32)], axis=0)
    b_loc = lax.dynamic_slice(w['ada_b'], (0, me * nl), (L, nl))[:, None, :]
    mod_loc = _modulation(s_in, w['ada_w'], b_loc, "modulation")
    g_b = _small_allgather(_pack([mod_loc]), "gather_mod")
    (mod_all,) = _unpack(g_b, [mod_loc.shape])
    mod_lat = lax.dynamic_index_in_dim(mod_all, me, axis=2, keepdims=False)
    mod_ctx = mod_all[:, :, NDEV, :]
    to_mod = lambda t: t.transpose(1, 0, 2).reshape(L, 6, D)
    mod = jnp.stack([to_mod(mod_ctx), to_mod(mod_lat)], axis=1)

    w_in, w_out, w_qkv, w_o = [None] * n_ssd, [None] * n_ssd, [None] * n_att, [None] * n_att
    w_up, w_down = [None] * L, [None] * L

    cos, sin = _rope_tables(t_lat, ctx_rows, hd)
    e_f, e_b = _expand_matrix(H, P, False), _expand_matrix(H, P, True)
    et_f, et_b = e_f.T, e_b.T

    xs = jnp.concatenate([ctx[0], x[0]], axis=0)
    saved = []
    for i in range(L):
        j = i // 2
        s = dict(x0=xs)
        zero_of = lambda t: jnp.minimum(jnp.abs(t[0, 0, 0].astype(F32)), 0.0)
        if i == 0:
            w_in[0] = _cat_cols(gather_wait(first, mod)[0], "ssd_in_cat")
        else:
            got = gather_wait(pending, xs)
            if i % 2 == 0:
                w_in[j], w_out[j] = _cat_cols(got[0], "ssd_in_cat"), got[1].reshape(DI, D)
            else:
                w_qkv[j], w_o[j] = got[0], got[1].reshape(n_q * hd, D)
            w_up[i], w_down[i] = got[2], got[3].reshape(F2 // 2, D)
            if i + 1 < L:
                pending, tok = gather_start(layer_weights(i + 1), f"{i + 1}", zero_of(got[1]))
        nw1, nw2 = w['norm1_w'][i][None] + tok, w['norm2_w'][i][None]
        s['h1'] = _normmod(xs, nw1, mod[i], 0, ctx_rows, "normmod")
        if i % 2 == 0:
            s['zx'] = _mm(s['h1'], w_in[j], tm=768, tn=1152, tk=2048, name="mm_ssd_in")
            s['cw'], s['cb'] = ssd_cw[j], w['ssd_conv_b'][j][None]
            s['xbc'] = _ssd_conv(s['zx'], s['cw'], s['cb'], DI, ctx_rows, "ssd_conv")
            s['dtb'] = jnp.concatenate([w['ssd_dt_bias_f'][j], w['ssd_dt_bias_b'][j]])[None]
            s['alog'] = jnp.concatenate([w['ssd_a_log_f'][j], w['ssd_a_log_b'][j]])[None]
            s['yf'], s['hin_f'] = _ssd_scan(s['xbc'], s['zx'], s['dtb'], s['alog'], e_f, H, ncc, False, "ssd_scan_f")
            s['yb'], s['hin_b'] = _ssd_scan(s['xbc'], s['zx'], s['dtb'], s['alog'], e_b, H, ncc, True, "ssd_scan_b")
            s['dexp'], s['snw'] = jnp.repeat(w['ssd_d'][j], P)[None], w['ssd_norm_w'][j][None]
            if i == 0:
                got = gather_wait(pending, s['yb'])
                w_out[0], w_up[0], w_down[0] = got[0].reshape(DI, D), got[1], got[2].reshape(F2 // 2, D)
                if L > 1:
                    pending, tok = gather_start(layer_weights(1), "1", zero_of(got[0]))
                    s['snw'] = s['snw'] + tok
            s['o'] = _ssd_finish(s['yf'], s['yb'], s['xbc'], s['zx'], s['dexp'], s['snw'], "ssd_finish")
            s['mix'], x1 = _mm(s['o'], w_out[j], tm=768, tn=512, tk=4096, name="mm_ssd_out",
                               resid=xs, gate=mod[i][:, 2], ctx_rows=ctx_rows)
        else:
            s['qkv'] = _mm(s['h1'], w_qkv[j], tm=768, tn=384, tk=2048, name="mm_qkv", bslots=_ident)
            s['qg'], s['kg'] = w['attn_q_gain'][j][None], w['attn_k_gain'][j][None]
            s['qr'], s['kr'], s['vb'] = _qk_prep(s['qkv'], s['qg'], s['kg'], cos, sin, n_q, ctx_rows, "qk_prep")
            s['o'] = _attn_fwd(s['qr'], s['kr'], s['vb'], w['attn_sinks'][j], n_q, ctx_rows, "attn_fwd")
            s['mix'], x1 = _mm(s['o'], w_o[j], tm=768, tn=1024, tk=2048, name="mm_attn_out",
                               resid=xs, gate=mod[i][:, 2], ctx_rows=ctx_rows)
        s['x1'] = x1
        s['h2'] = _normmod(x1, nw2, mod[i], 1, ctx_rows, "normmod")
        s['u'] = _mm(s['h2'], w_up[i], tm=768, tn=1408, tk=2048, name="mm_ffn_up", bslots=_ffn_perm)
        s['a'] = _ffn_mid(s['u'], ffn_cw[i], ffn_cb[i], ctx_rows, "ffn_mid")
        s['f'], xs = _mm(s['a'], w_down[i], tm=768, tn=512, tk=5632, name="mm_ffn_down",
                         resid=x1, gate=mod[i][:, 5], ctx_rows=ctx_rows)
        saved.append(s)

    dx, lacc = _loss_head(xs, tgt[0], ctx_rows, "loss_head")
    loss = lax.psum(lacc[0, 0], ("x", "y", "c"))

    gbig = {name: [None] * w[name].shape[0] for name in BIG}
    gs = {name: [None] * w[name].shape[0] for name in SMALL + ['ssd_conv_w', 'ffn_conv_w']}
    dmod = [None] * L
    lands = {n: lax.empty((NDEV,) + w[n].shape, XFER) for n in BIG}
    pend_early, pend_late, tok = None, None, jnp.zeros((), F32)

    def scatter_start(keys, tag, zero=None):
        srcs = [gbig[n][j] for n, j in keys]
        for (n, j), g in zip(keys, srcs):
            own = lax.dynamic_index_in_dim(g, me, 0, keepdims=False)
            if zero is not None:
                own = own + zero.astype(own.dtype)
            lands[n] = lax.dynamic_update_slice(lands[n], own[None, None], (me, j, 0, 0))
        ss, rs, srcs, got, t = _xfer_start(srcs, [lands[n] for n, _ in keys], [j for _, j in keys], True, f"scatter_start_{tag}")
        return (ss, rs, srcs, got, keys, tag), t

    def scatter_wait(p, after):
        ss, rs, srcs, got, keys, tag = p
        got = _xfer_wait(ss, rs, srcs, got, [j for _, j in keys], True, after, f"scatter_wait_{tag}")
        for (n, _), t in zip(keys, got):
            lands[n] = t

    for i in reversed(range(L)):
        j = i // 2
        s = saved[i]
        nw1, nw2 = w['norm1_w'][i][None], w['norm2_w'][i][None]
        dm2, dg2 = _gate_bwd(dx, s['f'], mod[i], 1, ctx_rows, "gate_bwd")
        da = _mm(dm2, w_down[i], tb=True, out_dtype=MXU, tm=768, tn=1408, tk=2048, name="mm_ffn_down_dx")
        gbig['ffn_w_down'][i] = _mm(s['a'], dm2, ta=True, out_dtype=XFER, tm=1408, tn=1024, tk=2112,
                                    name="mm_ffn_down_dw").reshape(NDEV, -1, D)
        du, gcw = _ffn_mid_bwd(s['u'], da, ffn_cw[i], ffn_cb[i], ctx_rows, "ffn_mid_bwd")
        gcw = deinterleave(gcw)
        gs['ffn_conv_w'][i], gs['ffn_conv_b'][i] = gcw[0:3], gcw[3]
        dh2 = _mm(du, w_up[i], tb=True, out_dtype=MXU, tm=768, tn=1024, tk=1408, name="mm_ffn_up_dx", bslots=_ffn_perm)
        gbig['ffn_w_up'][i] = _mm(s['h2'], du, ta=True, out_dtype=XFER, tm=1024, tn=1408, tk=2112, name="mm_ffn_up_dw",
                                  oslots=_ffn_perm)
        dx1, sums2 = _normmod_bwd(s['x1'], nw2, mod[i], dh2, dx, 1, ctx_rows, "normmod_bwd")
        dmix, dg1 = _gate_bwd(dx1, s['mix'], mod[i], 0, ctx_rows, "gate_bwd")

        def send_early(after):
            if pend_early is not None:
                scatter_wait(pend_early, after)
            return scatter_start(layer_weights(i)[1:], f"early_{i}")

        def send_late(after):
            if pend_late is not None:
                scatter_wait(pend_late, after)
            return scatter_start(layer_weights(i)[:1], f"late_{i}")

        if i % 2 == 0:
            do = _mm(dmix, w_out[j], tb=True, out_dtype=MXU, tm=768, tn=1024, tk=2048, name="mm_ssd_out_dx")
            gbig['ssd_w_out'][j] = _mm(s['o'], dmix, ta=True, out_dtype=XFER, tm=1024, tn=1024, tk=2112,
                                       name="mm_ssd_out_dw").reshape(NDEV, -1, D)
            pend_early, tok = send_early(do)
            dy, dzx, fs, dd = _ssd_finish_bwd(s['yf'], s['yb'], s['xbc'], s['zx'], s['dexp'], s['snw'] + tok[0, 0], do, et_f,
                                              (T, IN), "ssd_finish_bwd")
            acc = _ssd_scan_bwd(s['xbc'], s['zx'], s['dtb'], s['alog'], e_f, et_f, s['dexp'], s['hin_f'], dy, None,
                                H, ncc, False, "ssd_scan_bwd_f")
            dxbc, ddt, ssm = _ssd_scan_bwd(s['xbc'], s['zx'], s['dtb'], s['alog'], e_b, et_b, s['dexp'], s['hin_b'], dy, acc,
                                           H, ncc, True, "ssd_scan_bwd_b")
            dzx, gscw = _ssd_conv_bwd(s['zx'], dxbc, dzx, s['cw'], s['cb'], DI, ctx_rows, "ssd_conv_bwd")
            dzx = _put_cols(dzx, ddt, IN // LANE - 1, "ssd_put_ddt")
            dwi = _mm(s['h1'], dzx, ta=True, tm=1024, tn=1152, tk=2112, name="mm_ssd_in_dw")
            gbig['ssd_w_in'][j] = _split_cols(dwi, NDEV, XFER, "ssd_in_split")
            pend_late, tok = send_late(gbig['ssd_w_in'][j])
            dh1 = _mm(dzx, w_in[j], tb=True, out_dtype=MXU, tm=768, tn=1024, tk=3456, name="mm_ssd_in_dx", after=tok)
            gs['ssd_conv_w'][j], gs['ssd_conv_b'][j] = gscw[0:3], gscw[3]
            gs['ssd_dt_bias_f'][j], gs['ssd_dt_bias_b'][j] = ssm[1, :H], ssm[1, H:]
            gs['ssd_a_log_f'][j], gs['ssd_a_log_b'][j] = ssm[0, :H], ssm[0, H:]
            gs['ssd_d'][j], gs['ssd_norm_w'][j] = dd[1, :H], fs[0]
        else:
            do = _mm(dmix, w_o[j], tb=True, out_dtype=MXU, tm=768, tn=1024, tk=2048, name="mm_attn_out_dx")
            gbig['attn_w_o'][j] = _mm(s['o'], dmix, ta=True, out_dtype=XFER, tm=1024, tn=1024, tk=2112,
                                      name="mm_attn_out_dw").reshape(NDEV, -1, D)
            pend_early, tok = send_early(do)
            dq, dk, dv, dsk = _attn_bwd(s['qr'], s['kr'], s['vb'], w['attn_sinks'][j] + tok[0, 0], do, n_q, ctx_rows, "attn_bwd")
            dqkv, gg = _qk_prep_bwd(s['qkv'], s['qg'], s['kg'], cos, sin, dq, dk, dv, n_q, ctx_rows, "qk_prep_bwd")
            gbig['attn_w_qkv'][j] = _mm(s['h1'], dqkv, ta=True, out_dtype=XFER, tm=1024, tn=384, tk=2112, name="mm_qkv_dw",
                                        oslots=_ident)
            pend_late, tok = send_late(gbig['attn_w_qkv'][j])
            dh1 = _mm(dqkv, w_qkv[j], tb=True, out_dtype=MXU, tm=768, tn=1024, tk=384, name="mm_qkv_dx", bslots=_ident,
                      after=tok)
            gs['attn_q_gain'][j], gs['attn_k_gain'][j] = gg[0], gg[1]
            gs['attn_sinks'][j] = dsk.reshape(ATTN_KV_HEADS, 8, LANE)[:, 0, :n_q // ATTN_KV_HEADS].reshape(n_q)
        dx, sums1 = _normmod_bwd(s['x0'], nw1, mod[i], dh1, dx1, 0, ctx_rows, "normmod_bwd")
        gs['norm1_w'][i], gs['norm2_w'][i] = sums1[0, 2] + sums1[1, 2], sums2[0, 2] + sums2[1, 2]
        dmod[i] = jnp.stack([sums1[:, 0], sums1[:, 1], dg1[:, 0], sums2[:, 0], sums2[:, 1], dg2[:, 0]], axis=1)
    grad_x = dx[ctx_rows:][None]
    dmod = jnp.stack(dmod)
    dmod_ctx, dmod_lat = dmod[:, 0].reshape(L, 6 * D), dmod[:, 1].reshape(L, 6 * D)
    gs['ada_b'] = dmod_ctx + dmod_lat

    out = {}

    small_g = [jnp.stack(gs[n]) if isinstance(gs[n], list) else gs[n] for n in SMALL]
    extras = [jnp.stack(gs['ssd_conv_w']), jnp.stack(gs['ffn_conv_w'])]
    shapes_c = [w[n].shape for n in SMALL] + [e.shape for e in extras]
    g_c = _small_allgather(_pack(small_g + extras), "gather_small")
    zeros = [jnp.zeros(e.shape, F32) for e in extras]
    res = _adamw_small(g_c, _pack([w[n] for n in SMALL] + zeros), _pack([m[n] for n in SMALL] + zeros),
                       _pack([v[n] for n in SMALL] + zeros), "adamw_small")
    res = [_unpack(r, shapes_c) for r in res]
    for k, n in enumerate(SMALL):
        out[n] = tuple(r[k] for r in res)
    g_scw, g_fcw = res[0][len(SMALL)], res[0][len(SMALL) + 1]
    g_scw = lax.dynamic_index_in_dim(g_scw.reshape(n_ssd, 3, NDEV, XBC // NDEV), me, axis=2, keepdims=False)
    g_fcw = lax.dynamic_index_in_dim(g_fcw.reshape(L, 3, NDEV, G), me, axis=2, keepdims=False)
    conv = ['ssd_conv_w', 'ffn_conv_w']
    res = _adamw_small(_pack([g_scw, g_fcw])[None], _pack([w[n] for n in conv]), _pack([m[n] for n in conv]),
                       _pack([v[n] for n in conv]), "adamw_conv")
    res = [_unpack(r, [w[n].shape for n in conv]) for r in res]
    for k, n in enumerate(conv):
        out[n] = tuple(r[k] for r in res)

    g_m = _small_allgather(jnp.concatenate([dmod_lat, dmod_ctx], axis=0), "gather_dmod")
    all_lat, all_ctx = g_m[:, :L], g_m[:, L:]
    my_cols = lambda t: lax.dynamic_slice(t, (0, 0, me * nl), (NDEV, L, nl))
    dml = jnp.concatenate([my_cols(all_lat), jnp.zeros((MOD_ROWS - NDEV, L, nl), F32)], axis=0)
    g_ada, pc = _modulation_bwd(s_in, dml, my_cols(all_ctx), w['ada_w'], "modulation_bwd")
    out['ada_w'] = _adamw(g_ada[None], w['ada_w'], m['ada_w'], v['ada_w'], "adamw")
    g_d = _small_allgather(_pack([pc[0]]), "gather_cctx")
    res = _cctx_update(g_d, _pack([w['c_ctx']]), _pack([m['c_ctx']]), _pack([v['c_ctx']]), "adamw_cctx")
    out['c_ctx'] = tuple(_unpack(r, [(D,)])[0] for r in res)

    c_new = out['c_ctx'][0]
    scatter_wait(pend_early, c_new)
    last = [n for n, _ in layer_weights(0)[:1]]
    for name in [n for n in BIG if n not in last]:
        out[name] = _adamw(lands[name], w[name], m[name], v[name], "adamw")
    scatter_wait(pend_late, out['ffn_w_down'][0])
    for name in last:
        out[name] = _adamw(lands[name], w[name], m[name], v[name], "adamw")

    return (loss, grad_x) + tuple(out[n][k] for k in range(4) for n in WEIGHTS)


def kernel(x, c, ctx, c_ctx, ada_w, ada_b, norm1_w, norm2_w, ssd_w_in, ssd_conv_w, ssd_conv_b, ssd_dt_bias_f, ssd_dt_bias_b, ssd_a_log_f, ssd_a_log_b, ssd_d, ssd_norm_w, ssd_w_out, attn_w_qkv, attn_q_gain, attn_k_gain, attn_sinks, attn_w_o, ffn_w_up, ffn_conv_w, ffn_conv_b, ffn_w_down, loss_target, m_c_ctx, m_ada_w, m_ada_b, m_norm1_w, m_norm2_w, m_ssd_w_in, m_ssd_conv_w, m_ssd_conv_b, m_ssd_dt_bias_f, m_ssd_dt_bias_b, m_ssd_a_log_f, m_ssd_a_log_b, m_ssd_d, m_ssd_norm_w, m_ssd_w_out, m_attn_w_qkv, m_attn_q_gain, m_attn_k_gain, m_attn_sinks, m_attn_w_o, m_ffn_w_up, m_ffn_conv_w, m_ffn_conv_b, m_ffn_w_down, v_c_ctx, v_ada_w, v_ada_b, v_norm1_w, v_norm2_w, v_ssd_w_in, v_ssd_conv_w, v_ssd_conv_b, v_ssd_dt_bias_f, v_ssd_dt_bias_b, v_ssd_a_log_f, v_ssd_a_log_b, v_ssd_d, v_ssd_norm_w, v_ssd_w_out, v_attn_w_qkv, v_attn_q_gain, v_attn_k_gain, v_attn_sinks, v_attn_w_o, v_ffn_w_up, v_ffn_conv_w, v_ffn_conv_b, v_ffn_w_down):
    w = dict(c_ctx=c_ctx, ada_w=ada_w, ada_b=ada_b, norm1_w=norm1_w, norm2_w=norm2_w, ssd_w_in=ssd_w_in, ssd_conv_w=ssd_conv_w, ssd_conv_b=ssd_conv_b, ssd_dt_bias_f=ssd_dt_bias_f, ssd_dt_bias_b=ssd_dt_bias_b, ssd_a_log_f=ssd_a_log_f, ssd_a_log_b=ssd_a_log_b, ssd_d=ssd_d, ssd_norm_w=ssd_norm_w, ssd_w_out=ssd_w_out, attn_w_qkv=attn_w_qkv, attn_q_gain=attn_q_gain, attn_k_gain=attn_k_gain, attn_sinks=attn_sinks, attn_w_o=attn_w_o, ffn_w_up=ffn_w_up, ffn_conv_w=ffn_conv_w, ffn_conv_b=ffn_conv_b, ffn_w_down=ffn_w_down)
    m = dict(c_ctx=m_c_ctx, ada_w=m_ada_w, ada_b=m_ada_b, norm1_w=m_norm1_w, norm2_w=m_norm2_w, ssd_w_in=m_ssd_w_in, ssd_conv_w=m_ssd_conv_w, ssd_conv_b=m_ssd_conv_b, ssd_dt_bias_f=m_ssd_dt_bias_f, ssd_dt_bias_b=m_ssd_dt_bias_b, ssd_a_log_f=m_ssd_a_log_f, ssd_a_log_b=m_ssd_a_log_b, ssd_d=m_ssd_d, ssd_norm_w=m_ssd_norm_w, ssd_w_out=m_ssd_w_out, attn_w_qkv=m_attn_w_qkv, attn_q_gain=m_attn_q_gain, attn_k_gain=m_attn_k_gain, attn_sinks=m_attn_sinks, attn_w_o=m_attn_w_o, ffn_w_up=m_ffn_w_up, ffn_conv_w=m_ffn_conv_w, ffn_conv_b=m_ffn_conv_b, ffn_w_down=m_ffn_w_down)
    v = dict(c_ctx=v_c_ctx, ada_w=v_ada_w, ada_b=v_ada_b, norm1_w=v_norm1_w, norm2_w=v_norm2_w, ssd_w_in=v_ssd_w_in, ssd_conv_w=v_ssd_conv_w, ssd_conv_b=v_ssd_conv_b, ssd_dt_bias_f=v_ssd_dt_bias_f, ssd_dt_bias_b=v_ssd_dt_bias_b, ssd_a_log_f=v_ssd_a_log_f, ssd_a_log_b=v_ssd_a_log_b, ssd_d=v_ssd_d, ssd_norm_w=v_ssd_norm_w, ssd_w_out=v_ssd_w_out, attn_w_qkv=v_attn_w_qkv, attn_q_gain=v_attn_q_gain, attn_k_gain=v_attn_k_gain, attn_sinks=v_attn_sinks, attn_w_o=v_attn_w_o, ffn_w_up=v_ffn_w_up, ffn_conv_w=v_ffn_conv_w, ffn_conv_b=v_ffn_conv_b, ffn_w_down=v_ffn_w_down)
    return _step(x, c, ctx, w, loss_target, m, v)
```

```python
import functools

import numpy as np
import jax
import jax.numpy as jnp
from jax import lax
from jax.experimental import pallas as pl
from jax.experimental.pallas import tpu as pltpu

F32 = jnp.float32
BF16 = jnp.bfloat16
MXU = BF16
XFER = BF16
NORM_EPS = 1e-6
VMEM_CAP = 56 * 1024 * 1024
HALO = 8
LANE = 128
NDEV = 8

GRID_W = 64
ROPE_THETA = 10000.0
ATTN_KV_HEADS = 4
ATTN_WINDOW = 128
BLK = 128
SSD_GROUPS = 8

ADAM_LR, ADAM_B1, ADAM_B2, ADAM_EPS, ADAM_WD, ADAM_STEP = 0.001, 0.9, 0.999, 1e-08, 0.01, 10

MESH_ID = pl.DeviceIdType.MESH


def _cparams(sem, est_bytes):
    lim = int(min(VMEM_CAP, max(16 * 1024 * 1024, est_bytes * 1.3 + (4 << 20))))
    return pltpu.CompilerParams(dimension_semantics=sem, vmem_limit_bytes=lim)


def _nbytes(shape, dtype):
    return int(np.prod(shape)) * jnp.dtype(dtype).itemsize


def _silu(x):
    return x * jax.nn.sigmoid(x)


def _dsilu(x):
    s = jax.nn.sigmoid(x)
    return s * (1.0 + x * (1.0 - s))


def _split3(v):
    h = v.astype(BF16)
    r = v - h.astype(F32)
    m = r.astype(BF16)
    l = (r - m.astype(F32)).astype(BF16)
    return h, m, l


def _dot(a, b, dn=(((1,), (0,)), ((), ()))):
    return lax.dot_general(a, b, dn, preferred_element_type=F32)


NT = (((1,), (1,)), ((), ()))
TN = (((0,), (0,)), ((), ()))


def _dot3_rhs(sel, v):
    return sum(_dot(sel, p) for p in _split3(v))


def _dot3_lhs(v, sel, dn=(((1,), (0,)), ((), ()))):
    return sum(_dot(p, sel, dn) for p in _split3(v))


def _dot2_stacked(vals, sel):
    pieces = []
    for v in vals:
        h, m, _ = _split3(v)
        pieces += [h, m]
    r = _dot(jnp.concatenate(pieces, axis=0), sel)
    out, row = [], 0
    for v in vals:
        n = v.shape[0]
        out.append(r[row:row + n] + r[row + n:row + 2 * n])
        row += 2 * n
    return out


def _expand2(vals, sel2):
    lhs = []
    for v in vals:
        h, m, _ = _split3(v)
        lhs.append(jnp.concatenate([h, m], axis=1))
    r = _dot(jnp.concatenate(lhs, axis=0), sel2)
    n = vals[0].shape[0]
    return [r[i * n:(i + 1) * n] for i in range(len(vals))]


def _ident(s):
    return s


def _ffn_perm(s):
    return (s % 2) * 4 + s // 2


def _mm(a, b, *, ta=False, tb=False, out_dtype=F32, tm, tn, tk, name, bslots=None, oslots=None,
        resid=None, gate=None, ctx_rows=0, after=None):
    M = a.shape[1] if ta else a.shape[0]
    K = a.shape[0] if ta else a.shape[1]
    if bslots is None:
        N = b.shape[0] if tb else b.shape[1]
    else:
        G = b.shape[2]
        N = b.shape[1] if tb else NDEV * G
        assert (NDEV * G == K) if tb else (b.shape[1] == K)
    tm, tn, tk = min(tm, M), min(tn, N), min(tk, K)
    if bslots is not None:
        if tb:
            tk = min(tk, G)
            assert G % tk == 0
        else:
            tn = min(tn, G)
            assert G % tn == 0
    if oslots is not None:
        Go = N // NDEV
        tn = min(tn, Go)
        assert Go % tn == 0
    assert M % tm == 0 and N % tn == 0 and K % tk == 0, (name, M, N, K, tm, tn, tk)
    nk = K // tk
    fused = resid is not None
    dn = (((0 if ta else 1,), (1 if tb else 0,)), ((), ()))

    n_in = 2 + (2 if fused else 0) + (1 if after is not None else 0)

    def body(*refs):
        a_ref, b_ref = refs[0], refs[1]
        if fused:
            r_ref, g_ref = refs[2], refs[3]
            o_ref, x_ref = refs[n_in], refs[n_in + 1]
            rest = refs[n_in + 2:]
        else:
            o_ref = refs[n_in]
            rest = refs[n_in + 1:]
        bv = b_ref[0] if bslots is not None else b_ref[...]
        p = lax.dot_general(a_ref[...].astype(MXU), bv.astype(MXU), dn, preferred_element_type=F32)

        def finish(acc):
            if oslots is not None:
                o_ref[0] = acc.astype(o_ref.dtype)
            else:
                o_ref[...] = acc.astype(o_ref.dtype)
            if fused:
                row = pl.program_id(0) * tm + lax.broadcasted_iota(jnp.int32, (tm, 1), 0)
                g = jnp.where(row < ctx_rows, g_ref[0:1, :], g_ref[1:2, :])
                x_ref[...] = r_ref[...] + g * acc

        if nk == 1:
            finish(p)
        else:
            acc_ref = rest[0]
            k = pl.program_id(2)

            @pl.when(k == 0)
            def _():
                acc_ref[...] = p

            @pl.when(k > 0)
            def _():
                acc_ref[...] += p

            @pl.when(k == nk - 1)
            def _():
                finish(acc_ref[...])

    a_spec = pl.BlockSpec((tk, tm), lambda i, j, k: (k, i)) if ta else pl.BlockSpec((tm, tk), lambda i, j, k: (i, k))
    if bslots is None:
        b_spec = pl.BlockSpec((tn, tk), lambda i, j, k: (j, k)) if tb else pl.BlockSpec((tk, tn), lambda i, j, k: (k, j))
    elif tb:
        kpg = G // tk
        b_spec = pl.BlockSpec((1, tn, tk), lambda i, j, k: (bslots(k // kpg), j, k % kpg))
    else:
        npg = G // tn
        b_spec = pl.BlockSpec((1, tk, tn), lambda i, j, k: (bslots(j // npg), k, j % npg))
    if oslots is None:
        o_spec = pl.BlockSpec((tm, tn), lambda i, j, k: (i, j))
        o_shape = jax.ShapeDtypeStruct((M, N), out_dtype)
    else:
        opg = Go // tn
        o_spec = pl.BlockSpec((1, tm, tn), lambda i, j, k: (oslots(j // opg), i, j % opg))
        o_shape = jax.ShapeDtypeStruct((NDEV, M, Go), out_dtype)
    in_specs = [a_spec, b_spec]
    out_shape = [o_shape]
    out_specs = [o_spec]
    args = [a, b]
    est = 2 * (_nbytes((tm, tk), a.dtype) + _nbytes((tk, tn), b.dtype) + _nbytes((tm, tn), out_dtype)) + 3 * _nbytes((tm, tn), F32)
    if fused:
        in_specs += [o_spec, pl.BlockSpec((2, tn), lambda i, j, k: (0, j))]
        out_shape.append(jax.ShapeDtypeStruct((M, N), F32))
        out_specs.append(o_spec)
        args += [resid, gate]
        est += 4 * _nbytes((tm, tn), F32)
    if after is not None:
        in_specs.append(pl.BlockSpec((8, LANE), lambda i, j, k: (0, 0)))
        args.append(after)
    scratch = [] if nk == 1 else [pltpu.VMEM((tm, tn), F32)]
    res = pl.pallas_call(
        body, name=name, grid=(M // tm, N // tn, nk), in_specs=in_specs, out_specs=out_specs, out_shape=out_shape,
        scratch_shapes=scratch, compiler_params=_cparams(("parallel", "parallel", "arbitrary"), est),
    )(*args)
    return res if fused else res[0]


def _stream_of(i, tr, ctx_rows):
    return jnp.where(i * tr < ctx_rows, 0, 1)


def _acc_by_stream(sums_ref, part, i, n_ctx):
    @pl.when((i == 0) | (i == n_ctx))
    def _():
        sums_ref[0] = part

    @pl.when((i != 0) & (i != n_ctx))
    def _():
        sums_ref[0] += part


def _normmod(x, nw, mod, which, ctx_rows, name, tr=256):
    T, D = x.shape
    tr = min(tr, ctx_rows)
    assert T % tr == 0 and ctx_rows % tr == 0
    s_sh, s_sc = 3 * which, 3 * which + 1

    def body(x_ref, nw_ref, mod_ref, h_ref):
        xv = x_ref[...]
        r = lax.rsqrt(jnp.mean(xv * xv, axis=-1, keepdims=True) + NORM_EPS)
        y = (xv * r) * nw_ref[...]
        h_ref[...] = (y * (1.0 + mod_ref[0, s_sc:s_sc + 1, :]) + mod_ref[0, s_sh:s_sh + 1, :]).astype(h_ref.dtype)

    return pl.pallas_call(
        body, name=name, grid=(T // tr,),
        in_specs=[pl.BlockSpec((tr, D), lambda i: (i, 0)), pl.BlockSpec((1, D), lambda i: (0, 0)),
                  pl.BlockSpec((1, 6, D), lambda i: (_stream_of(i, tr, ctx_rows), 0, 0))],
        out_specs=pl.BlockSpec((tr, D), lambda i: (i, 0)),
        out_shape=jax.ShapeDtypeStruct((T, D), MXU),
        compiler_params=_cparams(("parallel",), 10 * _nbytes((tr, D), F32)),
    )(x, nw, mod)


def _normmod_bwd(x, nw, mod, dh, dx_in, which, ctx_rows, name, tr=256):
    T, D = x.shape
    tr = min(tr, ctx_rows)
    s_sc = 3 * which + 1
    n_ctx = ctx_rows // tr

    def body(x_ref, nw_ref, mod_ref, dh_ref, dxi_ref, dx_ref, sums_ref):
        i = pl.program_id(0)
        xv = x_ref[...]
        r = lax.rsqrt(jnp.mean(xv * xv, axis=-1, keepdims=True) + NORM_EPS)
        xh = xv * r
        dh_v = dh_ref[...].astype(F32)
        sc1 = 1.0 + mod_ref[0, s_sc:s_sc + 1, :]
        nwv = nw_ref[...]
        dxh = dh_v * (nwv * sc1)
        dx_ref[...] = dxi_ref[...] + r * (dxh - xh * jnp.mean(dxh * xh, axis=-1, keepdims=True))
        t = dh_v * xh
        part = jnp.concatenate([jnp.sum(dh_v, axis=0, keepdims=True), jnp.sum(t * nwv, axis=0, keepdims=True),
                                jnp.sum(t * sc1, axis=0, keepdims=True), jnp.zeros((5, D), F32)], axis=0)
        _acc_by_stream(sums_ref, part, i, n_ctx)

    row = pl.BlockSpec((tr, D), lambda i: (i, 0))
    return pl.pallas_call(
        body, name=name, grid=(T // tr,),
        in_specs=[row, pl.BlockSpec((1, D), lambda i: (0, 0)),
                  pl.BlockSpec((1, 6, D), lambda i: (_stream_of(i, tr, ctx_rows), 0, 0)), row, row],
        out_specs=[row, pl.BlockSpec((1, 8, D), lambda i: (_stream_of(i, tr, ctx_rows), 0, 0))],
        out_shape=[jax.ShapeDtypeStruct((T, D), F32), jax.ShapeDtypeStruct((2, 8, D), F32)],
        compiler_params=_cparams(("arbitrary",), 16 * _nbytes((tr, D), F32)),
    )(x, nw, mod, dh, dx_in)


def _gate_bwd(dx, mix, mod, which, ctx_rows, name, tr=256):
    T, D = dx.shape
    tr = min(tr, ctx_rows)
    s_g = 3 * which + 2
    n_ctx = ctx_rows // tr

    def body(dx_ref, mix_ref, mod_ref, dm_ref, sums_ref):
        i = pl.program_id(0)
        dxv = dx_ref[...]
        dm_ref[...] = (dxv * mod_ref[0, s_g:s_g + 1, :]).astype(dm_ref.dtype)
        part = jnp.concatenate([jnp.sum(dxv * mix_ref[...], axis=0, keepdims=True), jnp.zeros((7, D), F32)], axis=0)
        _acc_by_stream(sums_ref, part, i, n_ctx)

    row = pl.BlockSpec((tr, D), lambda i: (i, 0))
    return pl.pallas_call(
        body, name=name, grid=(T // tr,),
        in_specs=[row, row, pl.BlockSpec((1, 6, D), lambda i: (_stream_of(i, tr, ctx_rows), 0, 0))],
        out_specs=[row, pl.BlockSpec((1, 8, D), lambda i: (_stream_of(i, tr, ctx_rows), 0, 0))],
        out_shape=[jax.ShapeDtypeStruct((T, D), MXU), jax.ShapeDtypeStruct((2, 8, D), F32)],
        compiler_params=_cparams(("arbitrary",), 10 * _nbytes((tr, D), F32)),
    )(dx, mix, mod)


def _halo_specs(tr, tn, n_row_tiles, col_of):
    g = tr // HALO
    last = n_row_tiles * g - 1
    return [pl.BlockSpec((HALO, tn), lambda j, i: (jnp.maximum(i * g - 1, 0), col_of(j))),
            pl.BlockSpec((tr, tn), lambda j, i: (i, col_of(j))),
            pl.BlockSpec((HALO, tn), lambda j, i: (jnp.minimum((i + 1) * g, last), col_of(j)))]


def _ext(p_ref, m_ref, n_ref):
    return jnp.concatenate([p_ref[...].astype(F32), m_ref[...].astype(F32), n_ref[...].astype(F32)], axis=0)


def _seq_masks(i, tr, ctx_rows, total_rows):
    row = i * tr - HALO + lax.broadcasted_iota(jnp.int32, (tr + 2 * HALO, 1), 0)
    has_prev = (row != 0) & (row != ctx_rows)
    has_next = (row != ctx_rows - 1) & (row != total_rows - 1)
    return has_prev, has_next


def _shift_down(e):
    return pltpu.roll(e, 1, 0)


def _shift_up(e):
    return pltpu.roll(e, e.shape[0] - 1, 0)


def _neighbours(e, masks):
    prev, nxt = _shift_down(e), _shift_up(e)
    if masks is not None:
        prev, nxt = jnp.where(masks[0], prev, 0.0), jnp.where(masks[1], nxt, 0.0)
    return prev, nxt


def _conv3(e, prev, nxt, w):
    return prev * w[0:1, :] + e * w[1:2, :] + nxt * w[2:3, :]


def _conv3_t(d, w, masks):
    from_prev, from_next = _neighbours(d, masks)
    return from_next * w[0:1, :] + d * w[1:2, :] + from_prev * w[2:3, :]


def _conv_wgrad(d, e, prev, nxt):
    c = slice(HALO, e.shape[0] - HALO)
    dc = d[c]
    return jnp.concatenate([jnp.sum(dc * prev[c], axis=0, keepdims=True), jnp.sum(dc * e[c], axis=0, keepdims=True),
                            jnp.sum(dc * nxt[c], axis=0, keepdims=True), jnp.sum(dc, axis=0, keepdims=True),
                            jnp.zeros((4, e.shape[1]), F32)], axis=0)


def _per_tile_kind(i, tr, ctx_rows, total_rows, fn):
    n_ctx, n_all = ctx_rows // tr, total_rows // tr
    at_end = (i == 0) | (i == n_ctx - 1) | (i == n_ctx) | (i == n_all - 1)

    @pl.when(at_end)
    def _():
        fn(_seq_masks(i, tr, ctx_rows, total_rows))

    @pl.when(jnp.logical_not(at_end))
    def _():
        fn(None)


def _acc_first(ref, part, i):
    @pl.when(i == 0)
    def _():
        ref[...] = part

    @pl.when(i > 0)
    def _():
        ref[...] += part


def _ffn_mid(u, cw, cb, ctx_rows, name, tr=128):
    T, F2 = u.shape
    G = F2 // NDEV
    tr = min(tr, ctx_rows)
    nr, nc = T // tr, NDEV // 2

    def body(up, um, un, w_ref, b_ref, a_ref):
        def tile(masks):
            e = _ext(up, um, un)
            uc = _conv3(e, *_neighbours(e, masks), w_ref[...])[HALO:HALO + tr] + b_ref[...]
            a_ref[...] = (_silu(uc[:, G:]) * uc[:, :G]).astype(a_ref.dtype)

        _per_tile_kind(pl.program_id(1), tr, ctx_rows, T, tile)

    return pl.pallas_call(
        body, name=name, grid=(nc, nr),
        in_specs=_halo_specs(tr, 2 * G, nr, lambda j: j) + [pl.BlockSpec((3, 2 * G), lambda j, i: (0, j)),
                                                             pl.BlockSpec((1, 2 * G), lambda j, i: (0, j))],
        out_specs=pl.BlockSpec((tr, G), lambda j, i: (i, j)),
        out_shape=jax.ShapeDtypeStruct((T, F2 // 2), MXU),
        compiler_params=_cparams(("parallel", "parallel"), 12 * _nbytes((tr + 16, 2 * G), F32)),
    )(u, u, u, cw, cb)


def _ffn_mid_bwd(u, da, cw, cb, ctx_rows, name, tr=128):
    T, F2 = u.shape
    G = F2 // NDEV
    tr = min(tr, ctx_rows)
    nr, nc = T // tr, NDEV // 2

    def body(up, um, un, dp, dm, dn_, w_ref, b_ref, du_ref, gw_ref):
        i = pl.program_id(1)

        def tile(masks):
            e = _ext(up, um, un)
            w = w_ref[...]
            prev, nxt = _neighbours(e, masks)
            uc = _conv3(e, prev, nxt, w) + b_ref[...]
            val, gt = uc[:, :G], uc[:, G:]
            dav = _ext(dp, dm, dn_)
            sg = jax.nn.sigmoid(gt)
            duc = jnp.concatenate([dav * (gt * sg), dav * val * (sg * (1.0 + gt * (1.0 - sg)))], axis=1)
            du_ref[...] = _conv3_t(duc, w, masks)[HALO:HALO + tr].astype(du_ref.dtype)
            _acc_first(gw_ref, _conv_wgrad(duc, e, prev, nxt), i)

        _per_tile_kind(i, tr, ctx_rows, T, tile)

    du, gw = pl.pallas_call(
        body, name=name, grid=(nc, nr),
        in_specs=(_halo_specs(tr, 2 * G, nr, lambda j: j) + _halo_specs(tr, G, nr, lambda j: j)
                  + [pl.BlockSpec((3, 2 * G), lambda j, i: (0, j)), pl.BlockSpec((1, 2 * G), lambda j, i: (0, j))]),
        out_specs=[pl.BlockSpec((tr, 2 * G), lambda j, i: (i, j)), pl.BlockSpec((8, 2 * G), lambda j, i: (0, j))],
        out_shape=[jax.ShapeDtypeStruct((T, F2), MXU), jax.ShapeDtypeStruct((8, F2), F32)],
        compiler_params=_cparams(("parallel", "arbitrary"), 24 * _nbytes((tr + 16, 2 * G), F32)),
    )(u, u, u, da, da, da, cw, cb)
    return du, gw


def _ssd_conv(zx, cw, cb, col0, ctx_rows, name, tr=256, tn=512):
    T = zx.shape[0]
    C = cw.shape[1]
    tr, tn = min(tr, ctx_rows), min(tn, C)
    assert C % tn == 0 and col0 % tn == 0
    nr, nc, cb0 = T // tr, C // tn, col0 // tn

    def body(zp, zm, zn, w_ref, b_ref, o_ref):
        def tile(masks):
            e = _ext(zp, zm, zn)
            o_ref[...] = _silu(_conv3(e, *_neighbours(e, masks), w_ref[...])[HALO:HALO + tr] + b_ref[...])

        _per_tile_kind(pl.program_id(1), tr, ctx_rows, T, tile)

    return pl.pallas_call(
        body, name=name, grid=(nc, nr),
        in_specs=_halo_specs(tr, tn, nr, lambda j: j + cb0) + [pl.BlockSpec((3, tn), lambda j, i: (0, j)),
                                                                pl.BlockSpec((1, tn), lambda j, i: (0, j))],
        out_specs=pl.BlockSpec((tr, tn), lambda j, i: (i, j)),
        out_shape=jax.ShapeDtypeStruct((T, C), F32),
        compiler_params=_cparams(("parallel", "parallel"), 12 * _nbytes((tr + 16, tn), F32)),
    )(zx, zx, zx, cw, cb)


def _ssd_conv_bwd(zx, dxbc, dzx, cw, cb, col0, ctx_rows, name, tr=256, tn=512):
    T = zx.shape[0]
    C = cw.shape[1]
    tr, tn = min(tr, ctx_rows), min(tn, C)
    nr, nc, cb0 = T // tr, C // tn, col0 // tn

    def body(zp, zm, zn, dp, dm, dn_, w_ref, b_ref, dzx_in, dz_ref, gw_ref):
        del dzx_in
        i = pl.program_id(1)

        def tile(masks):
            e = _ext(zp, zm, zn)
            w = w_ref[...]
            prev, nxt = _neighbours(e, masks)
            pre = _conv3(e, prev, nxt, w) + b_ref[...]
            dpre = _ext(dp, dm, dn_) * _dsilu(pre)
            dz_ref[...] = _conv3_t(dpre, w, masks)[HALO:HALO + tr].astype(dz_ref.dtype)
            _acc_first(gw_ref, _conv_wgrad(dpre, e, prev, nxt), i)

        _per_tile_kind(i, tr, ctx_rows, T, tile)

    return pl.pallas_call(
        body, name=name, grid=(nc, nr),
        in_specs=(_halo_specs(tr, tn, nr, lambda j: j + cb0) + _halo_specs(tr, tn, nr, lambda j: j)
                  + [pl.BlockSpec((3, tn), lambda j, i: (0, j)), pl.BlockSpec((1, tn), lambda j, i: (0, j)),
                     pl.BlockSpec(memory_space=pl.ANY)]),
        out_specs=[pl.BlockSpec((tr, tn), lambda j, i: (i, j + cb0)), pl.BlockSpec((8, tn), lambda j, i: (0, j))],
        out_shape=[jax.ShapeDtypeStruct(dzx.shape, dzx.dtype), jax.ShapeDtypeStruct((8, C), F32)],
        input_output_aliases={8: 0},
        compiler_params=_cparams(("parallel", "arbitrary"), 24 * _nbytes((tr + 16, tn), F32)),
    )(zx, zx, zx, dxbc, dxbc, dxbc, cw, cb, dzx)


def _cat_cols(w3, name, tr=256):
    n, K, G = w3.shape
    tr = min(tr, K)

    def body(w_ref, o_ref):
        o_ref[...] = jnp.concatenate([w_ref[d].astype(F32) for d in range(n)], axis=1).astype(o_ref.dtype)

    return pl.pallas_call(
        body, name=name, grid=(K // tr,),
        in_specs=[pl.BlockSpec((n, tr, G), lambda i: (0, i, 0))], out_specs=pl.BlockSpec((tr, n * G), lambda i: (i, 0)),
        out_shape=jax.ShapeDtypeStruct((K, n * G), w3.dtype),
        compiler_params=_cparams(("parallel",), 6 * _nbytes((tr, n * G), F32)),
    )(w3)


def _split_cols(g, n, out_dtype, name, tr=256):
    K, NG = g.shape
    G = NG // n
    tr = min(tr, K)

    def body(g_ref, o_ref):
        for d in range(n):
            o_ref[d] = g_ref[:, d * G:(d + 1) * G].astype(o_ref.dtype)

    return pl.pallas_call(
        body, name=name, grid=(K // tr,),
        in_specs=[pl.BlockSpec((tr, NG), lambda i: (i, 0))], out_specs=pl.BlockSpec((n, tr, G), lambda i: (0, i, 0)),
        out_shape=jax.ShapeDtypeStruct((n, K, G), out_dtype),
        compiler_params=_cparams(("parallel",), 6 * _nbytes((tr, NG), F32)),
    )(g)


def _put_cols(dst, src, col_blk, name, tr=256):
    T, W = src.shape
    tr = min(tr, T)

    def body(s_ref, d_in, o_ref):
        del d_in
        o_ref[...] = s_ref[...].astype(o_ref.dtype)

    return pl.pallas_call(
        body, name=name, grid=(T // tr,),
        in_specs=[pl.BlockSpec((tr, W), lambda i: (i, 0)), pl.BlockSpec(memory_space=pl.ANY)],
        out_specs=pl.BlockSpec((tr, W), lambda i: (i, col_blk)),
        out_shape=jax.ShapeDtypeStruct(dst.shape, dst.dtype),
        input_output_aliases={1: 0},
        compiler_params=_cparams(("parallel",), 8 * _nbytes((tr, W), F32)),
    )(src, dst)


def _rope_tables(t_lat, ctx_rows, hd):
    half, quarter = hd // 2, hd // 4
    pos = jnp.arange(t_lat)
    row = (pos // GRID_W).astype(F32)
    col = (pos % GRID_W).astype(F32)
    inv_freq = ROPE_THETA ** (-jnp.arange(0, half, 2, dtype=F32) / half)
    ar, ac = row[:, None] * inv_freq[None, :], col[:, None] * inv_freq[None, :]
    cos = jnp.concatenate([jnp.cos(ar), jnp.cos(ar), jnp.cos(ac), jnp.cos(ac)], axis=1)
    sin = jnp.concatenate([-jnp.sin(ar), jnp.sin(ar), -jnp.sin(ac), jnp.sin(ac)], axis=1)
    del quarter
    cos = jnp.concatenate([jnp.ones((ctx_rows, hd), F32), cos], axis=0)
    sin = jnp.concatenate([jnp.zeros((ctx_rows, hd), F32), sin], axis=0)
    return cos, sin


def _partner(y):
    hd = y.shape[1]
    q = hd // 4
    lane = lax.broadcasted_iota(jnp.int32, y.shape, 1)
    return jnp.where((lane % (2 * q)) < q, pltpu.roll(y, hd - q, 1), pltpu.roll(y, q, 1))


def _qk_prep(qkv, qg, kg, cos, sin, n_q, ctx_rows, name, tr=256):
    T = qkv.shape[0]
    hd = qg.shape[1]
    n_kv = ATTN_KV_HEADS
    tr = min(tr, ctx_rows)

    def body(x_ref, qg_ref, kg_ref, c_ref, s_ref, q_ref, k_ref, v_ref):
        cv, sv = c_ref[...], s_ref[...]
        for h in range(n_q + n_kv):
            xh = x_ref[:, h * hd:(h + 1) * hd]
            r = lax.rsqrt(jnp.mean(xh * xh, axis=-1, keepdims=True) + NORM_EPS)
            y = (xh * r) * (qg_ref[...] if h < n_q else kg_ref[...])
            rot = y * cv + _partner(y) * sv
            if h < n_q:
                q_ref[:, h * hd:(h + 1) * hd] = rot.astype(q_ref.dtype)
            else:
                k_ref[:, (h - n_q) * hd:(h - n_q + 1) * hd] = rot.astype(k_ref.dtype)
        v_ref[...] = x_ref[:, (n_q + n_kv) * hd:].astype(v_ref.dtype)

    W = qkv.shape[1]
    return pl.pallas_call(
        body, name=name, grid=(T // tr,),
        in_specs=[pl.BlockSpec((tr, W), lambda i: (i, 0)), pl.BlockSpec((1, hd), lambda i: (0, 0)),
                  pl.BlockSpec((1, hd), lambda i: (0, 0)), pl.BlockSpec((tr, hd), lambda i: (i, 0)),
                  pl.BlockSpec((tr, hd), lambda i: (i, 0))],
        out_specs=[pl.BlockSpec((tr, n_q * hd), lambda i: (i, 0)), pl.BlockSpec((tr, n_kv * hd), lambda i: (i, 0)),
                   pl.BlockSpec((tr, n_kv * hd), lambda i: (i, 0))],
        out_shape=[jax.ShapeDtypeStruct((T, n_q * hd), MXU), jax.ShapeDtypeStruct((T, n_kv * hd), MXU),
                   jax.ShapeDtypeStruct((T, n_kv * hd), MXU)],
        compiler_params=_cparams(("parallel",), 6 * _nbytes((tr, W), F32)),
    )(qkv, qg, kg, cos, sin)


def _qk_prep_bwd(qkv, qg, kg, cos, sin, dq, dk, dv, n_q, ctx_rows, name, tr=256):
    T, W = qkv.shape
    hd = qg.shape[1]
    n_kv = ATTN_KV_HEADS
    tr = min(tr, ctx_rows)

    def body(x_ref, qg_ref, kg_ref, c_ref, s_ref, dq_ref, dk_ref, dv_ref, o_ref, g_ref):
        i = pl.program_id(0)
        cv, sv = c_ref[...], s_ref[...]
        gq = jnp.zeros((1, hd), F32)
        gk = jnp.zeros((1, hd), F32)
        for h in range(n_q + n_kv):
            xh = x_ref[:, h * hd:(h + 1) * hd]
            gain = qg_ref[...] if h < n_q else kg_ref[...]
            drot = (dq_ref[:, h * hd:(h + 1) * hd] if h < n_q else dk_ref[:, (h - n_q) * hd:(h - n_q + 1) * hd]).astype(F32)
            dy = drot * cv + _partner(drot * sv)
            r = lax.rsqrt(jnp.mean(xh * xh, axis=-1, keepdims=True) + NORM_EPS)
            xn = xh * r
            gsum = jnp.sum(dy * xn, axis=0, keepdims=True)
            if h < n_q:
                gq = gq + gsum
            else:
                gk = gk + gsum
            dxn = dy * gain
            o_ref[:, h * hd:(h + 1) * hd] = (r * (dxn - xn * jnp.mean(dxn * xn, axis=-1, keepdims=True))).astype(o_ref.dtype)
        o_ref[:, (n_q + n_kv) * hd:] = dv_ref[...].astype(o_ref.dtype)
        _acc_first(g_ref, jnp.concatenate([gq, gk, jnp.zeros((6, hd), F32)], axis=0), i)

    return pl.pallas_call(
        body, name=name, grid=(T // tr,),
        in_specs=[pl.BlockSpec((tr, W), lambda i: (i, 0)), pl.BlockSpec((1, hd), lambda i: (0, 0)),
                  pl.BlockSpec((1, hd), lambda i: (0, 0)), pl.BlockSpec((tr, hd), lambda i: (i, 0)),
                  pl.BlockSpec((tr, hd), lambda i: (i, 0)), pl.BlockSpec((tr, n_q * hd), lambda i: (i, 0)),
                  pl.BlockSpec((tr, n_kv * hd), lambda i: (i, 0)), pl.BlockSpec((tr, n_kv * hd), lambda i: (i, 0))],
        out_specs=[pl.BlockSpec((tr, W), lambda i: (i, 0)), pl.BlockSpec((8, hd), lambda i: (0, 0))],
        out_shape=[jax.ShapeDtypeStruct((T, W), MXU), jax.ShapeDtypeStruct((8, hd), F32)],
        compiler_params=_cparams(("arbitrary",), 8 * _nbytes((tr, W), F32)),
    )(qkv, qg, kg, cos, sin, dq, dk, dv)


def _attn_scores(q_ref, k_ref, sink_ref, h, qb, ctx_rows, nb, hd, grp):
    scale = hd ** -0.5
    w0 = jnp.clip(qb - 1, 0, nb - 3) * BLK
    w0 = pl.multiple_of(w0, BLK)
    qv = q_ref[...]
    qs = jnp.concatenate([qv[:, g * hd:(g + 1) * hd] for g in range(grp)], axis=0)
    kc = k_ref[0:ctx_rows, :]
    kb = k_ref[pl.ds(w0, 3 * BLK), :]
    s_c = _dot(qs, kc, NT) * scale
    s_b = _dot(qs, kb, NT) * scale
    n = grp * BLK
    qpos = qb * BLK + lax.broadcasted_iota(jnp.int32, (n, 3 * BLK), 0) % BLK
    kpos = w0 + lax.broadcasted_iota(jnp.int32, (n, 3 * BLK), 1)
    ok = (jnp.abs(kpos - qpos) <= ATTN_WINDOW) & (kpos >= ctx_rows) & (qpos >= ctx_rows)
    s_b = jnp.where(ok, s_b, -jnp.inf)
    gi = lax.broadcasted_iota(jnp.int32, (n, 1), 0) // BLK
    sink = jnp.zeros((n, 1), F32)
    for g in range(grp):
        sink = jnp.where(gi == g, sink_ref[h * grp + g], sink)
    m = jnp.maximum(jnp.maximum(jnp.max(s_c, axis=1, keepdims=True), jnp.max(s_b, axis=1, keepdims=True)), sink)
    e_c, e_b, e_s = jnp.exp(s_c - m), jnp.exp(s_b - m), jnp.exp(sink - m)
    inv = 1.0 / (jnp.sum(e_c, axis=1, keepdims=True) + jnp.sum(e_b, axis=1, keepdims=True) + e_s)
    return qs, kc, kb, w0, e_c * inv, e_b * inv, e_s * inv, gi


def _attn_fwd(qr, kr, vb, sinks, n_q, ctx_rows, name):
    T = qr.shape[0]
    n_kv = ATTN_KV_HEADS
    grp = n_q // n_kv
    hd = qr.shape[1] // n_q
    nb = T // BLK

    def body(sink_ref, q_ref, k_ref, v_ref, o_ref):
        h, qb = pl.program_id(0), pl.program_id(1)
        _, _, _, w0, p_c, p_b, _, _ = _attn_scores(q_ref, k_ref, sink_ref, h, qb, ctx_rows, nb, hd, grp)
        o = _dot(p_c.astype(MXU), v_ref[0:ctx_rows, :]) + _dot(p_b.astype(MXU), v_ref[pl.ds(w0, 3 * BLK), :])
        o_ref[...] = jnp.concatenate([o[g * BLK:(g + 1) * BLK] for g in range(grp)], axis=1).astype(o_ref.dtype)

    return pl.pallas_call(
        body, name=name, grid=(n_kv, nb),
        in_specs=[pl.BlockSpec(memory_space=pltpu.SMEM), pl.BlockSpec((BLK, grp * hd), lambda h, i: (i, h)),
                  pl.BlockSpec((T, hd), lambda h, i: (0, h)), pl.BlockSpec((T, hd), lambda h, i: (0, h))],
        out_specs=pl.BlockSpec((BLK, grp * hd), lambda h, i: (i, h)),
        out_shape=jax.ShapeDtypeStruct((T, n_q * hd), MXU),
        compiler_params=_cparams(("parallel", "arbitrary"), 4 * _nbytes((T, hd), MXU) + 24 * _nbytes((grp * BLK, 5 * BLK), F32)),
    )(sinks, qr, kr, vb)


def _attn_bwd(qr, kr, vb, sinks, do, n_q, ctx_rows, name):
    T = qr.shape[0]
    n_kv = ATTN_KV_HEADS
    grp = n_q // n_kv
    hd = qr.shape[1] // n_q
    nb = T // BLK
    scale = hd ** -0.5

    def body(sink_ref, q_ref, k_ref, v_ref, do_ref, dq_ref, dk_ref, dv_ref, ds_ref):
        h, qb = pl.program_id(0), pl.program_id(1)
        qs, kc, kb, w0, p_c, p_b, p_s, gi = _attn_scores(q_ref, k_ref, sink_ref, h, qb, ctx_rows, nb, hd, grp)
        dov = do_ref[...]
        dos = jnp.concatenate([dov[:, g * hd:(g + 1) * hd] for g in range(grp)], axis=0)
        vc = v_ref[0:ctx_rows, :]
        vw = v_ref[pl.ds(w0, 3 * BLK), :]
        dp_c = _dot(dos, vc, NT)
        dp_b = _dot(dos, vw, NT)
        delta = jnp.sum(p_c * dp_c, axis=1, keepdims=True) + jnp.sum(p_b * dp_b, axis=1, keepdims=True)
        ds_c = (p_c * (dp_c - delta) * scale).astype(MXU)
        ds_b = (p_b * (dp_b - delta) * scale).astype(MXU)
        dq = _dot(ds_c, kc) + _dot(ds_b, kb)
        dq_ref[...] = jnp.concatenate([dq[g * BLK:(g + 1) * BLK] for g in range(grp)], axis=1)

        @pl.when(qb == 0)
        def _():
            dk_ref[...] = jnp.zeros(dk_ref.shape, F32)
            dv_ref[...] = jnp.zeros(dv_ref.shape, F32)

        dk_ref[0:ctx_rows, :] += _dot(ds_c, qs, TN)
        dv_ref[0:ctx_rows, :] += _dot(p_c.astype(MXU), dos, TN)
        dk_ref[pl.ds(w0, 3 * BLK), :] += _dot(ds_b, qs, TN)
        dv_ref[pl.ds(w0, 3 * BLK), :] += _dot(p_b.astype(MXU), dos, TN)
        t = -(p_s * delta)
        lane = lax.broadcasted_iota(jnp.int32, (8, LANE), 1)
        part = jnp.zeros((8, LANE), F32)
        for g in range(grp):
            part = jnp.where(lane == g, jnp.sum(jnp.where(gi == g, t, 0.0)), part)
        _acc_first(ds_ref, part, qb)

    return pl.pallas_call(
        body, name=name, grid=(n_kv, nb),
        in_specs=[pl.BlockSpec(memory_space=pltpu.SMEM), pl.BlockSpec((BLK, grp * hd), lambda h, i: (i, h)),
                  pl.BlockSpec((T, hd), lambda h, i: (0, h)), pl.BlockSpec((T, hd), lambda h, i: (0, h)),
                  pl.BlockSpec((BLK, grp * hd), lambda h, i: (i, h))],
        out_specs=[pl.BlockSpec((BLK, grp * hd), lambda h, i: (i, h)), pl.BlockSpec((T, hd), lambda h, i: (0, h)),
                   pl.BlockSpec((T, hd), lambda h, i: (0, h)), pl.BlockSpec((8, LANE), lambda h, i: (h, 0))],
        out_shape=[jax.ShapeDtypeStruct((T, n_q * hd), F32), jax.ShapeDtypeStruct((T, n_kv * hd), F32),
                   jax.ShapeDtypeStruct((T, n_kv * hd), F32), jax.ShapeDtypeStruct((n_kv * 8, LANE), F32)],
        compiler_params=_cparams(("parallel", "arbitrary"), 4 * _nbytes((T, hd), MXU) + 4 * _nbytes((T, hd), F32)
                                 + 40 * _nbytes((grp * BLK, 5 * BLK), F32)),
    )(sinks, qr, kr, vb, do)


def _chunk_order(s, n_chunks, n_ctx, rev):
    if not rev:
        return s
    return jnp.where(s < n_ctx, n_ctx - 1 - s, n_chunks - 1 + n_ctx - s)


def _softplus(x):
    return jnp.maximum(x, 0.0) + jnp.log(1.0 + jnp.exp(-jnp.abs(x)))


def _expand_matrix(n_heads, p, rev):
    e = np.zeros((LANE, n_heads * p), np.float32)
    for h in range(n_heads):
        e[h + (n_heads if rev else 0), h * p:(h + 1) * p] = 1.0
    return jnp.asarray(e, BF16)


def _ssd_chunk_prep(dt_ref, dtb_ref, alog_ref, e_ref, rev):
    dt = _softplus(dt_ref[...] + dtb_ref[...])
    a = -jnp.exp(alog_ref[...])
    li = lax.broadcasted_iota(jnp.int32, (BLK, BLK), 0)
    si = lax.broadcasted_iota(jnp.int32, (BLK, BLK), 1)
    tri = (si >= li) if rev else (si <= li)
    acs = _dot3_rhs(tri.astype(BF16), a * dt)
    dtexp, aexp = _expand2([dt, acs], e_ref[...])
    return dt, a, tri, acs, dtexp, aexp


def _pair_cols(ap):
    lane = lax.broadcasted_iota(jnp.int32, ap.shape, 1)
    apr = pltpu.roll(ap, LANE // 2, 1)
    return jnp.where(lane < LANE // 2, ap, apr), jnp.where(lane < LANE // 2, apr, ap)


def _ssd_scan(xbc, zx, dtb, alog, emat, n_heads, n_ctx, rev, name):
    T, C = xbc.shape
    P = emat.shape[1] // n_heads
    DI = n_heads * P
    GN = (C - DI) // 2
    N = GN // SSD_GROUPS
    n_pairs = DI // LANE
    ppg = n_pairs // SSD_GROUPS
    n_chunks = T // BLK
    hoff = n_heads if rev else 0
    last = 0 if rev else BLK - 1
    dt_blk = zx.shape[1] // LANE - 1
    assert N == LANE and 2 * P == LANE and 2 * n_heads == LANE

    def body(xs_ref, b_ref, c_ref, dt_ref, dtb_ref, alog_ref, e_ref, y_ref, hin_ref, state_ref, xdt_s, xdec_s, aexp_s, at_s):
        s = pl.program_id(0)

        @pl.when(s == 0)
        def _():
            state_ref[...] = jnp.zeros(state_ref.shape, F32)

        dt, a, tri, acs, dtexp, aexp = _ssd_chunk_prep(dt_ref, dtb_ref, alog_ref, e_ref, rev)
        at_s[...] = acs.T
        aexp_s[...] = aexp
        xdt = xs_ref[...] * dtexp
        xdt_s[...] = xdt.astype(MXU)
        xdec_s[...] = (xdt * jnp.exp(aexp[last:last + 1, :] - aexp)).astype(MXU)
        hin_ref[0] = state_ref[...]
        lane = lax.broadcasted_iota(jnp.int32, (BLK, LANE), 1)

        def pair(k, carry):
            col = pl.multiple_of(k * LANE, LANE)
            gcol = pl.multiple_of((k // ppg) * N, N)
            bg = b_ref[:, pl.ds(gcol, N)].astype(MXU)
            cg = c_ref[:, pl.ds(gcol, N)].astype(MXU)
            cb = _dot(cg, bg, NT)
            ap = aexp_s[:, pl.ds(col, LANE)]
            ac0, ac1 = _pair_cols(ap)
            ar0 = at_s[pl.ds(2 * k + hoff, 1), :]
            ar1 = at_s[pl.ds(2 * k + 1 + hoff, 1), :]
            m0 = (cb * jnp.exp(jnp.where(tri, ac0 - ar0, -jnp.inf))).astype(MXU)
            m1 = (cb * jnp.exp(jnp.where(tri, ac1 - ar1, -jnp.inf))).astype(MXU)
            xp = xdt_s[:, pl.ds(col, LANE)]
            zero = jnp.zeros_like(xp)
            xbd = jnp.concatenate([jnp.where(lane < LANE // 2, xp, zero), jnp.where(lane >= LANE // 2, xp, zero)], axis=0)
            yd = _dot(jnp.concatenate([m0, m1], axis=1), xbd)
            ht = state_ref[k]
            yo = _dot(cg, ht.astype(MXU)) * jnp.exp(ap)
            y_ref[:, pl.ds(col, LANE)] = yd + yo
            st = _dot(bg, xdec_s[:, pl.ds(col, LANE)], TN)
            state_ref[k] = jnp.exp(aexp_s[pl.ds(last, 1), pl.ds(col, LANE)]) * ht + st
            return carry

        lax.fori_loop(0, n_pairs, pair, 0, unroll=8)

    order = lambda s: _chunk_order(s, n_chunks, n_ctx, rev)
    return pl.pallas_call(
        body, name=name, grid=(n_chunks,),
        in_specs=[pl.BlockSpec((BLK, DI), lambda s: (order(s), 0)),
                  pl.BlockSpec((BLK, GN), lambda s: (order(s), DI // GN)),
                  pl.BlockSpec((BLK, GN), lambda s: (order(s), DI // GN + 1)),
                  pl.BlockSpec((BLK, LANE), lambda s: (order(s), dt_blk)),
                  pl.BlockSpec((1, LANE), lambda s: (0, 0)), pl.BlockSpec((1, LANE), lambda s: (0, 0)),
                  pl.BlockSpec((2 * LANE, DI), lambda s: (0, 0))],
        out_specs=[pl.BlockSpec((BLK, DI), lambda s: (order(s), 0)),
                   pl.BlockSpec((1, n_pairs, N, LANE), lambda s: (order(s), 0, 0, 0))],
        out_shape=[jax.ShapeDtypeStruct((T, DI), F32), jax.ShapeDtypeStruct((n_chunks, n_pairs, N, LANE), F32)],
        scratch_shapes=[pltpu.VMEM((n_pairs, N, LANE), F32), pltpu.VMEM((BLK, DI), MXU), pltpu.VMEM((BLK, DI), MXU),
                        pltpu.VMEM((BLK, DI), F32), pltpu.VMEM((LANE, BLK), F32)],
        compiler_params=_cparams(("arbitrary",), 20 * _nbytes((BLK, DI), F32)),
    )(xbc, xbc, xbc, zx, dtb, alog, jnp.concatenate([emat, emat], axis=0))


def _ssd_scan_bwd(xbc, zx, dtb, alog, emat, emat_t, dexp, hin, dy, acc, n_heads, n_ctx, rev, name):
    T, C = xbc.shape
    P = emat.shape[1] // n_heads
    DI = n_heads * P
    GN = (C - DI) // 2
    N = GN // SSD_GROUPS
    n_pairs = DI // LANE
    ppg = n_pairs // SSD_GROUPS
    n_chunks = T // BLK
    hoff = n_heads if rev else 0
    last = 0 if rev else BLK - 1
    dt_blk = zx.shape[1] // LANE - 1
    has_acc = acc is not None

    def body(*refs):
        (xs_ref, b_ref, c_ref, dt_ref, dtb_ref, alog_ref, e_ref, et_ref, dexp_ref, hin_ref, dy_ref) = refs[:11]
        n_in = 11
        if has_acc:
            dxbc_in, ddt_in, sums_in = refs[11:14]
            n_in = 14
        dxbc_ref, ddt_ref, sums_ref = refs[n_in:n_in + 3]
        dstate_ref, xdt_s, xdec_s, aexp_s, at_s, dyd_s, z1_s, z3_s, dxdt_s, cdrow_s, rmat_s, cst_s = refs[n_in + 3:]
        s = pl.program_id(0)

        @pl.when(s == 0)
        def _():
            dstate_ref[...] = jnp.zeros(dstate_ref.shape, F32)

        dt, a, tri, acs, dtexp, aexp = _ssd_chunk_prep(dt_ref, dtb_ref, alog_ref, e_ref, rev)
        at_s[...] = acs.T
        aexp_s[...] = aexp
        xsv = xs_ref[...]
        xdt = xsv * dtexp
        xdt_s[...] = xdt.astype(MXU)
        decend = jnp.exp(aexp[last:last + 1, :] - aexp)
        xdec_s[...] = (xdt * decend).astype(MXU)
        dyv = dy_ref[...]
        dyd_s[...] = (dyv * jnp.exp(aexp)).astype(MXU)
        dxbc_ref[:, DI:] = jnp.zeros((BLK, 2 * GN), F32)
        rmat_s[...] = jnp.zeros(rmat_s.shape, F32)
        lane = lax.broadcasted_iota(jnp.int32, (BLK, LANE), 1)
        lo = lane < LANE // 2
        tri_t = (lax.broadcasted_iota(jnp.int32, (BLK, BLK), 0) <= lax.broadcasted_iota(jnp.int32, (BLK, BLK), 1)) if not rev \
            else (lax.broadcasted_iota(jnp.int32, (BLK, BLK), 0) >= lax.broadcasted_iota(jnp.int32, (BLK, BLK), 1))

        def pair(k, carry):
            col = pl.multiple_of(k * LANE, LANE)
            gcol = pl.multiple_of((k // ppg) * N, N)
            bg = b_ref[:, pl.ds(gcol, N)].astype(MXU)
            cg = c_ref[:, pl.ds(gcol, N)].astype(MXU)
            cb = _dot(cg, bg, NT)
            cbt = _dot(bg, cg, NT)
            ap = aexp_s[:, pl.ds(col, LANE)]
            ac0, ac1 = _pair_cols(ap)
            ar0 = at_s[pl.ds(2 * k + hoff, 1), :]
            ar1 = at_s[pl.ds(2 * k + 1 + hoff, 1), :]
            seg0 = jnp.exp(jnp.where(tri, ac0 - ar0, -jnp.inf))
            seg1 = jnp.exp(jnp.where(tri, ac1 - ar1, -jnp.inf))
            segt0 = jnp.exp(jnp.where(tri_t, ar0 - ac0, -jnp.inf))
            segt1 = jnp.exp(jnp.where(tri_t, ar1 - ac1, -jnp.inf))
            dyp = dy_ref[:, pl.ds(col, LANE)].astype(MXU)
            zero = jnp.zeros_like(dyp)
            dy0, dy1 = jnp.where(lo, dyp, zero), jnp.where(lo, zero, dyp)
            dht = dstate_ref[k]
            dhb = dht.astype(MXU)
            ht = hin_ref[0, k]
            mt = jnp.concatenate([(cbt * segt0).astype(MXU), (cbt * segt1).astype(MXU)], axis=1)
            bdh = _dot(bg, dhb)
            dec_p = jnp.exp(aexp_s[pl.ds(last, 1), pl.ds(col, LANE)] - ap)
            dxdt_s[:, pl.ds(col, LANE)] = _dot(mt, jnp.concatenate([dy0, dy1], axis=0)) + dec_p * bdh
            z3_s[:, pl.ds(col, LANE)] = bdh
            cdec = jnp.exp(aexp_s[pl.ds(last, 1), pl.ds(col, LANE)])
            dydp = dyd_s[:, pl.ds(col, LANE)]
            dstate_ref[k] = _dot(cg, dydp, TN) + cdec * dht
            cdrow_s[0:1, pl.ds(col, LANE)] = cdec * jnp.sum(dht * ht, axis=0, keepdims=True)
            z1_s[:, pl.ds(col, LANE)] = _dot(cg, ht.astype(MXU))
            dcg = _dot(dydp, ht.astype(MXU), NT)
            dbg = _dot(xdec_s[:, pl.ds(col, LANE)], dhb, NT)
            xp = xdt_s[:, pl.ds(col, LANE)]
            dg0 = _dot(dy0, xp, NT)
            dg1 = _dot(dy1, xp, NT)
            ds0, ds1 = dg0 * seg0, dg1 * seg1
            w0, w1 = ds0 * cb, ds1 * cb
            dcb = ds0 + ds1
            lane_h = lax.broadcasted_iota(jnp.int32, (BLK, LANE), 1)
            rmat_s[...] += (jnp.where(lane_h == 2 * k + hoff, jnp.sum(w0, axis=1, keepdims=True), 0.0)
                            + jnp.where(lane_h == 2 * k + 1 + hoff, jnp.sum(w1, axis=1, keepdims=True), 0.0))
            cst_s[pl.ds(2 * k + hoff, 1), :] = jnp.sum(w0, axis=0, keepdims=True)
            cst_s[pl.ds(2 * k + 1 + hoff, 1), :] = jnp.sum(w1, axis=0, keepdims=True)
            dcbb = dcb.astype(MXU)
            dxbc_ref[:, pl.ds(DI + GN + gcol, N)] += dcg + _dot(dcbb, bg)
            dxbc_ref[:, pl.ds(DI + gcol, N)] += dbg + _dot(dcbb, cg, TN)
            return carry

        cst_s[...] = jnp.zeros(cst_s.shape, F32)
        lax.fori_loop(0, n_pairs, pair, 0, unroll=4)

        etv = et_ref[...]
        dxdt = dxdt_s[...]
        qfull = xdt * decend * z3_s[...]
        tot8 = jnp.concatenate([jnp.sum(qfull, axis=0, keepdims=True) + cdrow_s[0:1, :], jnp.zeros((7, DI), F32)], axis=0)
        z1q, z2, tot = _dot2_stacked([dyv * (z1_s[...] * jnp.exp(aexp)) - qfull, dxdt * xsv, tot8], etv)
        dacs = rmat_s[...] - cst_s[...].T + z1q
        rowi = lax.broadcasted_iota(jnp.int32, (BLK, LANE), 0)
        dacs = dacs + jnp.where(rowi == last, tot[0:1, :], 0.0)
        d_a = _dot3_rhs(tri_t.astype(BF16), dacs)
        ddt = a * d_a + z2
        x_raw = dt_ref[...] + dtb_ref[...]
        ddt_raw = ddt * jax.nn.sigmoid(x_raw)
        lane_l = lax.broadcasted_iota(jnp.int32, (BLK, LANE), 1)
        mine = (lane_l >= hoff) & (lane_l < hoff + n_heads)
        ddt_raw = jnp.where(mine, ddt_raw, 0.0)
        part = jnp.concatenate([jnp.sum(jnp.where(mine, dt * d_a, 0.0), axis=0, keepdims=True) * a,
                                jnp.sum(ddt_raw, axis=0, keepdims=True), jnp.zeros((6, LANE), F32)], axis=0)
        dxs = dxdt * dtexp
        if has_acc:
            dxbc_ref[:, 0:DI] = dxs + dxbc_in[:, 0:DI]
            dxbc_ref[:, DI:] += dxbc_in[:, DI:]
            ddt_ref[...] = ddt_raw + ddt_in[...]
            part = part + jnp.where(s == 0, sums_in[...], 0.0)
        else:
            dxbc_ref[:, 0:DI] = dxs + dyv * dexp_ref[...]
            ddt_ref[...] = ddt_raw
        _acc_first(sums_ref, part, s)

    order = lambda s: _chunk_order(n_chunks - 1 - s, n_chunks, n_ctx, rev)
    in_specs = [pl.BlockSpec((BLK, DI), lambda s: (order(s), 0)),
                pl.BlockSpec((BLK, GN), lambda s: (order(s), DI // GN)),
                pl.BlockSpec((BLK, GN), lambda s: (order(s), DI // GN + 1)),
                pl.BlockSpec((BLK, LANE), lambda s: (order(s), dt_blk)),
                pl.BlockSpec((1, LANE), lambda s: (0, 0)), pl.BlockSpec((1, LANE), lambda s: (0, 0)),
                pl.BlockSpec((2 * LANE, DI), lambda s: (0, 0)), pl.BlockSpec((DI, LANE), lambda s: (0, 0)),
                pl.BlockSpec((1, DI), lambda s: (0, 0)),
                pl.BlockSpec((1, n_pairs, N, LANE), lambda s: (order(s), 0, 0, 0)),
                pl.BlockSpec((BLK, DI), lambda s: (order(s), 0))]
    args = [xbc, xbc, xbc, zx, dtb, alog, jnp.concatenate([emat, emat], axis=0), emat_t, dexp, hin, dy]
    if has_acc:
        in_specs += [pl.BlockSpec((BLK, C), lambda s: (order(s), 0)), pl.BlockSpec((BLK, LANE), lambda s: (order(s), 0)),
                     pl.BlockSpec((8, LANE), lambda s: (0, 0))]
        args += list(acc)
    return pl.pallas_call(
        body, name=name, grid=(n_chunks,),
        in_specs=in_specs,
        out_specs=[pl.BlockSpec((BLK, C), lambda s: (order(s), 0)), pl.BlockSpec((BLK, LANE), lambda s: (order(s), 0)),
                   pl.BlockSpec((8, LANE), lambda s: (0, 0))],
        out_shape=[jax.ShapeDtypeStruct((T, C), F32), jax.ShapeDtypeStruct((T, LANE), F32), jax.ShapeDtypeStruct((8, LANE), F32)],
        scratch_shapes=[pltpu.VMEM((n_pairs, N, LANE), F32), pltpu.VMEM((BLK, DI), MXU), pltpu.VMEM((BLK, DI), MXU),
                        pltpu.VMEM((BLK, DI), F32), pltpu.VMEM((LANE, BLK), F32), pltpu.VMEM((BLK, DI), MXU),
                        pltpu.VMEM((BLK, DI), F32), pltpu.VMEM((BLK, DI), F32), pltpu.VMEM((BLK, DI), F32),
                        pltpu.VMEM((8, DI), F32), pltpu.VMEM((BLK, LANE), F32), pltpu.VMEM((LANE, BLK), F32)],
        compiler_params=_cparams(("arbitrary",), 36 * _nbytes((BLK, DI), F32)),
    )(*args)


def _ssd_finish(yf, yb, xbc, zx, dexp, nw, name, tr=256):
    T, DI = yf.shape
    tr = min(tr, T)

    def body(yf_ref, yb_ref, xs_ref, z_ref, d_ref, nw_ref, o_ref):
        y = yf_ref[...] + yb_ref[...] + xs_ref[...] * d_ref[...]
        gt = y * _silu(z_ref[...])
        r = lax.rsqrt(jnp.mean(gt * gt, axis=-1, keepdims=True) + NORM_EPS)
        o_ref[...] = ((gt * r) * nw_ref[...]).astype(o_ref.dtype)

    row = pl.BlockSpec((tr, DI), lambda i: (i, 0))
    vec = pl.BlockSpec((1, DI), lambda i: (0, 0))
    return pl.pallas_call(
        body, name=name, grid=(T // tr,), in_specs=[row, row, row, row, vec, vec], out_specs=row,
        out_shape=jax.ShapeDtypeStruct((T, DI), MXU),
        compiler_params=_cparams(("parallel",), 16 * _nbytes((tr, DI), F32)),
    )(yf, yb, xbc, zx, dexp, nw)


def _ssd_finish_bwd(yf, yb, xbc, zx, dexp, nw, do, emat_t, dzx_shape, name, tr=64):
    T, DI = yf.shape
    tr = min(tr, T)
    n_steps = T // tr

    def body(yf_ref, yb_ref, xs_ref, z_ref, d_ref, nw_ref, do_ref, et_ref, dy_ref, dz_ref, sums_ref, dd_ref):
        i = pl.program_id(0)
        xs = xs_ref[...]
        zv = z_ref[...]
        y = yf_ref[...] + yb_ref[...] + xs * d_ref[...]
        sz = _silu(zv)
        gt = y * sz
        r = lax.rsqrt(jnp.mean(gt * gt, axis=-1, keepdims=True) + NORM_EPS)
        gn = gt * r
        dov = do_ref[...].astype(F32)
        dgn = dov * nw_ref[...]
        dgt = r * (dgn - gn * jnp.mean(dgn * gn, axis=-1, keepdims=True))
        dy = dgt * sz
        dy_ref[...] = dy
        dz_ref[...] = (dgt * y * _dsilu(zv)).astype(dz_ref.dtype)
        part = jnp.concatenate([jnp.sum(dov * gn, axis=0, keepdims=True), jnp.sum(dy * xs, axis=0, keepdims=True),
                                jnp.zeros((6, DI), F32)], axis=0)
        _acc_first(sums_ref, part, i)

        @pl.when(i == n_steps - 1)
        def _():
            dd_ref[...] = _dot3_lhs(sums_ref[...], et_ref[...])

    row = pl.BlockSpec((tr, DI), lambda i: (i, 0))
    vec = pl.BlockSpec((1, DI), lambda i: (0, 0))
    return pl.pallas_call(
        body, name=name, grid=(n_steps,),
        in_specs=[row, row, row, row, vec, vec, row, pl.BlockSpec((DI, LANE), lambda i: (0, 0))],
        out_specs=[row, row, pl.BlockSpec((8, DI), lambda i: (0, 0)), pl.BlockSpec((8, LANE), lambda i: (0, 0))],
        out_shape=[jax.ShapeDtypeStruct((T, DI), F32), jax.ShapeDtypeStruct(dzx_shape, MXU), jax.ShapeDtypeStruct((8, DI), F32),
                   jax.ShapeDtypeStruct((8, LANE), F32)],
        compiler_params=_cparams(("arbitrary",), 40 * _nbytes((tr, DI), F32)),
    )(yf, yb, xbc, zx, dexp, nw, do, emat_t)


def _loss_head(xf, tgt, ctx_rows, name, tr=256):
    T, D = xf.shape
    tr = min(tr, ctx_rows)
    n_ctx = ctx_rows // tr

    def body(x_ref, t_ref, dx_ref, l_ref):
        i = pl.program_id(0)

        @pl.when(i < n_ctx)
        def _():
            dx_ref[...] = jnp.zeros(dx_ref.shape, F32)

        @pl.when(i == 0)
        def _():
            l_ref[...] = jnp.zeros(l_ref.shape, F32)

        @pl.when(i >= n_ctx)
        def _():
            e = x_ref[...] - t_ref[...]
            dx_ref[...] = e * (1.0 / D)
            l_ref[...] += 0.5 * jnp.sum(jnp.mean(e * e, axis=-1, keepdims=True))

    return pl.pallas_call(
        body, name=name, grid=(T // tr,),
        in_specs=[pl.BlockSpec((tr, D), lambda i: (i, 0)), pl.BlockSpec((tr, D), lambda i: (jnp.maximum(i - n_ctx, 0), 0))],
        out_specs=[pl.BlockSpec((tr, D), lambda i: (i, 0)), pl.BlockSpec((8, LANE), lambda i: (0, 0))],
        out_shape=[jax.ShapeDtypeStruct((T, D), F32), jax.ShapeDtypeStruct((8, LANE), F32)],
        compiler_params=_cparams(("arbitrary",), 10 * _nbytes((tr, D), F32)),
    )(xf, tgt)


def _adamw_math(w, g, m, v):
    m2 = ADAM_B1 * m + (1.0 - ADAM_B1) * g
    v2 = ADAM_B2 * v + (1.0 - ADAM_B2) * (g * g)
    m_hat = m2 / (1.0 - ADAM_B1 ** ADAM_STEP)
    v_hat = v2 / (1.0 - ADAM_B2 ** ADAM_STEP)
    delta = -ADAM_LR * (m_hat / (jnp.sqrt(v_hat) + ADAM_EPS) + ADAM_WD * w)
    return delta, m2, v2


def _row_tile(rows, target):
    if rows <= target:
        return rows
    t = target - target % 8
    while rows % t:
        t -= 8
    return t


def _adamw(parts, w, m, v, name, tr=128):
    n, L, rows, cols = parts.shape
    tr = _row_tile(rows, tr)

    def body(p_ref, w_ref, m_ref, v_ref, g_ref, d_ref, m2_ref, v2_ref):
        g = p_ref[0, 0].astype(F32)
        for q in range(1, n):
            g = g + p_ref[q, 0].astype(F32)
        d, m2, v2 = _adamw_math(w_ref[0], g, m_ref[0], v_ref[0])
        g_ref[0], d_ref[0], m2_ref[0], v2_ref[0] = g, d, m2, v2

    blk = pl.BlockSpec((1, tr, cols), lambda l, i: (l, i, 0))
    shp = jax.ShapeDtypeStruct((L, rows, cols), F32)
    return pl.pallas_call(
        body, name=name, grid=(L, rows // tr),
        in_specs=[pl.BlockSpec((n, 1, tr, cols), lambda l, i: (0, l, i, 0)), blk, blk, blk],
        out_specs=[blk, blk, blk, blk], out_shape=[shp, shp, shp, shp],
        compiler_params=_cparams(("parallel", "parallel"), 2 * (n + 8) * _nbytes((tr, cols), F32)),
    )(parts, w, m, v)


def _adamw_small(bufs, w, m, v, name):
    n, R, _ = bufs.shape

    def body(b_ref, w_ref, m_ref, v_ref, g_ref, d_ref, m2_ref, v2_ref):
        g = b_ref[0]
        for q in range(1, n):
            g = g + b_ref[q]
        d, m2, v2 = _adamw_math(w_ref[...], g, m_ref[...], v_ref[...])
        g_ref[...], d_ref[...], m2_ref[...], v2_ref[...] = g, d, m2, v2

    vm = pl.BlockSpec(memory_space=pltpu.VMEM)
    shp = jax.ShapeDtypeStruct((R, LANE), F32)
    return pl.pallas_call(body, name=name, in_specs=[vm, vm, vm, vm], out_specs=[vm, vm, vm, vm],
                          out_shape=[shp, shp, shp, shp],
                          compiler_params=pltpu.CompilerParams(vmem_limit_bytes=32 * 1024 * 1024))(bufs, w, m, v)


def _me():
    return lax.axis_index("x"), lax.axis_index("y"), lax.axis_index("c")


def _flip(v, bit):
    return 1 - v if bit else v


def _peer(k):
    x, y, c = _me()
    return _flip(x, (k >> 2) & 1), _flip(y, (k >> 1) & 1), _flip(c, k & 1)


def _dev_index(p):
    return 4 * p[0] + 2 * p[1] + p[2]


def _small_allgather(v, name):
    R, C = v.shape

    def body(v_ref, out_ref, send_sems, recv_sems, loc_sem):
        me = _dev_index(_me())
        mine = pltpu.make_async_copy(v_ref, out_ref.at[me], loc_sem)
        mine.start()
        sends = []
        for k in range(1, NDEV):
            cp = pltpu.make_async_remote_copy(src_ref=v_ref, dst_ref=out_ref.at[me], send_sem=send_sems.at[k - 1],
                                              recv_sem=recv_sems.at[k - 1], device_id=_peer(k), device_id_type=MESH_ID)
            cp.start()
            sends.append(cp)
        for k in range(1, NDEV):
            pltpu.make_async_remote_copy(src_ref=v_ref, dst_ref=out_ref.at[_dev_index(_peer(k))], send_sem=send_sems.at[k - 1],
                                         recv_sem=recv_sems.at[k - 1], device_id=_peer(k), device_id_type=MESH_ID).wait_recv()
        for cp in sends:
            cp.wait_send()
        mine.wait()

    vm = pl.BlockSpec(memory_space=pltpu.VMEM)
    return pl.pallas_call(
        body, name=name, in_specs=[vm], out_specs=vm, out_shape=jax.ShapeDtypeStruct((NDEV, R, C), F32),
        scratch_shapes=[pltpu.SemaphoreType.DMA((NDEV - 1,)), pltpu.SemaphoreType.DMA((NDEV - 1,)), pltpu.SemaphoreType.DMA(())],
        compiler_params=pltpu.CompilerParams(vmem_limit_bytes=48 * 1024 * 1024),
    )(v)


HBM_SPEC = pl.BlockSpec(memory_space=pltpu.HBM)
SEM_SPEC = pl.BlockSpec(memory_space=pltpu.SEMAPHORE)
DATAFLOW = pltpu.SideEffectType.DATAFLOW_SIDE_EFFECTING


def _xfer_copy(src_ref, land_ref, sems, a, k, layer, scatter, arriving):
    send_sems, recv_sems = sems
    me, peer = _dev_index(_me()), _dev_index(_peer(k))
    src = src_ref.at[peer] if scatter else src_ref
    slot = peer if arriving else me
    dst = land_ref.at[slot] if layer is None else land_ref.at[slot, layer]
    return pltpu.make_async_remote_copy(src_ref=src, dst_ref=dst, send_sem=send_sems.at[a * (NDEV - 1) + k - 1],
                                        recv_sem=recv_sems.at[a * (NDEV - 1) + k - 1], device_id=_peer(k),
                                        device_id_type=MESH_ID)


def _xfer_start(srcs, lands, layers, scatter, name):
    n = len(srcs)

    def body(*refs):
        src_refs, land_refs = refs[:n], refs[n:2 * n]
        sems = refs[2 * n], refs[2 * n + 1]
        token = refs[-1]
        for a in range(n):
            for k in range(1, NDEV):
                _xfer_copy(src_refs[a], land_refs[a], sems, a, k, layers[a], scatter, False).start()
        token[...] = jnp.zeros(token.shape, token.dtype)

    ops = [pltpu.with_memory_space_constraint(t, pltpu.HBM) for t in list(srcs) + list(lands)]
    n_sem = n * (NDEV - 1)
    res = pl.pallas_call(
        body, name=name,
        out_shape=(pltpu.SemaphoreType.DMA((n_sem,)), pltpu.SemaphoreType.DMA((n_sem,)),
                   *[pltpu.HBM(t.shape, t.dtype) for t in ops], jax.ShapeDtypeStruct((8, LANE), F32)),
        in_specs=[HBM_SPEC] * (2 * n),
        out_specs=(SEM_SPEC, SEM_SPEC, *[HBM_SPEC] * (2 * n), pl.BlockSpec(memory_space=pltpu.VMEM)),
        input_output_aliases={i: 2 + i for i in range(2 * n)},
        compiler_params=pltpu.CompilerParams(has_side_effects=DATAFLOW),
    )(*ops)
    return res[0], res[1], list(res[2:2 + n]), list(res[2 + n:2 + 2 * n]), res[-1]


def _xfer_wait(send_sems, recv_sems, srcs, lands, layers, scatter, after, name):
    n = len(srcs)

    def body(*refs):
        src_refs, land_refs = refs[:n], refs[n:2 * n]
        sems = refs[2 * n], refs[2 * n + 1]
        for a in range(n):
            for k in range(1, NDEV):
                _xfer_copy(src_refs[a], land_refs[a], sems, a, k, layers[a], scatter, False).wait_send()
                _xfer_copy(src_refs[a], land_refs[a], sems, a, k, layers[a], scatter, True).wait_recv()

    ops = list(srcs) + list(lands)
    res = pl.pallas_call(
        body, name=name,
        out_shape=tuple(pltpu.HBM(t.shape, t.dtype) for t in ops),
        in_specs=[HBM_SPEC] * (2 * n) + [SEM_SPEC, SEM_SPEC, pl.BlockSpec(memory_space=pl.ANY)],
        out_specs=tuple([HBM_SPEC] * (2 * n)),
        input_output_aliases={i: i for i in range(2 * n)},
        compiler_params=pltpu.CompilerParams(has_side_effects=DATAFLOW),
    )(*ops, send_sems, recv_sems, after)
    return list(res[n:])


HI = lax.Precision.HIGHEST
MOD_ROWS = 16


def _col_tile(n, target=512):
    return target if n % target == 0 else n


def _modulation(s_in, ada_w, b_loc, name):
    L, D, nl = ada_w.shape
    tn = _col_tile(nl)

    def body(s_ref, w_ref, b_ref, o_ref):
        o_ref[0] = jnp.dot(_silu(s_ref[...]), w_ref[0], preferred_element_type=F32, precision=HI) + b_ref[0]

    return pl.pallas_call(
        body, name=name, grid=(L, nl // tn),
        in_specs=[pl.BlockSpec((MOD_ROWS, D), lambda l, j: (0, 0)), pl.BlockSpec((1, D, tn), lambda l, j: (l, 0, j)),
                  pl.BlockSpec((1, 1, tn), lambda l, j: (l, 0, j))],
        out_specs=pl.BlockSpec((1, MOD_ROWS, tn), lambda l, j: (l, 0, j)),
        out_shape=jax.ShapeDtypeStruct((L, MOD_ROWS, nl), F32),
        compiler_params=_cparams(("parallel", "parallel"), 4 * _nbytes((D, tn), F32)),
    )(s_in, ada_w, b_loc)


def _modulation_bwd(s_in, dml, dmc, ada_w, name):
    L, D, nl = ada_w.shape
    tn = _col_tile(nl)

    def body(s_ref, dml_ref, dmc_ref, w_ref, g_ref, pc_ref):
        l, j = pl.program_id(0), pl.program_id(1)
        a = _silu(s_ref[...])
        tot = dmc_ref[0, 0]
        for d in range(1, NDEV):
            tot = tot + dmc_ref[d, 0]
        row = lax.broadcasted_iota(jnp.int32, (MOD_ROWS, tn), 0)
        dm = jnp.where(row == NDEV, tot, dml_ref[:, 0, 0, :])
        g_ref[0] = lax.dot_general(a, dm, TN, preferred_element_type=F32, precision=HI)
        tot8 = jnp.where(lax.broadcasted_iota(jnp.int32, (8, tn), 0) == 0, tot, 0.0)
        part = lax.dot_general(tot8, w_ref[0], NT, preferred_element_type=F32, precision=HI)

        @pl.when((l == 0) & (j == 0))
        def _():
            pc_ref[...] = part

        @pl.when((l != 0) | (j != 0))
        def _():
            pc_ref[...] += part

    return pl.pallas_call(
        body, name=name, grid=(L, nl // tn),
        in_specs=[pl.BlockSpec((MOD_ROWS, D), lambda l, j: (0, 0)), pl.BlockSpec((MOD_ROWS, 1, 1, tn), lambda l, j: (0, l, 0, j)),
                  pl.BlockSpec((NDEV, 1, 1, tn), lambda l, j: (0, l, 0, j)), pl.BlockSpec((1, D, tn), lambda l, j: (l, 0, j))],
        out_specs=[pl.BlockSpec((1, D, tn), lambda l, j: (l, 0, j)), pl.BlockSpec((8, D), lambda l, j: (0, 0))],
        out_shape=[jax.ShapeDtypeStruct((L, D, nl), F32), jax.ShapeDtypeStruct((8, D), F32)],
        compiler_params=_cparams(("arbitrary", "arbitrary"), 8 * _nbytes((D, tn), F32)),
    )(s_in, dml.reshape(MOD_ROWS, L, 1, nl), dmc.reshape(NDEV, L, 1, nl), ada_w)


def _cctx_update(bufs, c_ctx, m, v, name):
    n, R, _ = bufs.shape

    def body(b_ref, w_ref, m_ref, v_ref, g_ref, d_ref, m2_ref, v2_ref):
        g = b_ref[0]
        for q in range(1, n):
            g = g + b_ref[q]
        g = g * _dsilu(w_ref[...])
        d, m2, v2 = _adamw_math(w_ref[...], g, m_ref[...], v_ref[...])
        g_ref[...], d_ref[...], m2_ref[...], v2_ref[...] = g, d, m2, v2

    vm = pl.BlockSpec(memory_space=pltpu.VMEM)
    shp = jax.ShapeDtypeStruct((R, LANE), F32)
    return pl.pallas_call(body, name=name, in_specs=[vm, vm, vm, vm], out_specs=[vm, vm, vm, vm],
                          out_shape=[shp, shp, shp, shp])(bufs, c_ctx, m, v)


def _pack(arrs):
    flat = jnp.concatenate([a.reshape(-1).astype(F32) for a in arrs])
    n = flat.shape[0]
    total = -(-n // (8 * LANE)) * (8 * LANE)
    return jnp.pad(flat, (0, total - n)).reshape(total // LANE, LANE)


def _unpack(buf, shapes):
    lead = buf.shape[:-2]
    flat = buf.reshape(lead + (-1,))
    out, off = [], 0
    for s in shapes:
        n = int(np.prod(s))
        out.append(flat[..., off:off + n].reshape(lead + tuple(s)))
        off += n
    return out


WEIGHTS = ['c_ctx', 'ada_w', 'ada_b', 'norm1_w', 'norm2_w', 'ssd_w_in', 'ssd_conv_w', 'ssd_conv_b', 'ssd_dt_bias_f',
           'ssd_dt_bias_b', 'ssd_a_log_f', 'ssd_a_log_b', 'ssd_d', 'ssd_norm_w', 'ssd_w_out', 'attn_w_qkv', 'attn_q_gain',
           'attn_k_gain', 'attn_sinks', 'attn_w_o', 'ffn_w_up', 'ffn_conv_w', 'ffn_conv_b', 'ffn_w_down']
SMALL = ['ada_b', 'norm1_w', 'norm2_w', 'ssd_conv_b', 'ssd_dt_bias_f', 'ssd_dt_bias_b', 'ssd_a_log_f', 'ssd_a_log_b', 'ssd_d',
         'ssd_norm_w', 'attn_q_gain', 'attn_k_gain', 'attn_sinks', 'ffn_conv_b']
BIG = ['ssd_w_in', 'ssd_w_out', 'attn_w_qkv', 'attn_w_o', 'ffn_w_up', 'ffn_w_down']


def _step(x, c, ctx, w, tgt, m, v):
    xi, yi, ci = _me()
    me = 4 * xi + 2 * yi + ci
    t_lat, D = x.shape[1], x.shape[2]
    ctx_rows = ctx.shape[1]
    T = ctx_rows + t_lat
    L, n_ssd, n_att = w['norm1_w'].shape[0], w['ssd_d'].shape[0], w['attn_sinks'].shape[0]
    H, DI, XBC = w['ssd_d'].shape[1], w['ssd_norm_w'].shape[1], w['ssd_conv_b'].shape[1]
    P = DI // H
    IN = w['ssd_w_in'].shape[2] * NDEV
    hd, n_q = w['attn_q_gain'].shape[1], w['attn_sinks'].shape[1]
    F2 = w['ffn_conv_b'].shape[1]
    G = F2 // NDEV
    nl = w['ada_w'].shape[2]
    ncc = ctx_rows // BLK
    perm = [_ffn_perm(s) for s in range(NDEV)]
    inv = [perm.index(d) for d in range(NDEV)]

    def reorder(t, order):
        return jnp.concatenate([t[..., o * G:(o + 1) * G] for o in order], axis=-1)

    def interleave(t):
        return reorder(t, perm)

    def deinterleave(t):
        return reorder(t, inv)

    def layer_weights(i):
        mixer = ['ssd_w_in', 'ssd_w_out'] if i % 2 == 0 else ['attn_w_qkv', 'attn_w_o']
        return [(n, i // 2) for n in mixer] + [('ffn_w_up', i), ('ffn_w_down', i)]

    def gather_start(keys, tag, zero):
        srcs = [(w[n][j] + zero).astype(MXU) for n, j in keys]
        lands = [lax.dynamic_update_index_in_dim(lax.empty((NDEV,) + t.shape, t.dtype), t, me, 0) for t in srcs]
        ss, rs, srcs, lands, tok = _xfer_start(srcs, lands, [None] * len(srcs), False, f"gather_start_{tag}")
        return (ss, rs, srcs, lands, tag), tok[0, 0]

    def gather_wait(p, after):
        return _xfer_wait(*p[:4], [None] * len(p[2]), False, after, f"gather_wait_{p[4]}")

    first, tok = gather_start(layer_weights(0)[:1], "0a", jnp.zeros((), F32))
    pending, tok = gather_start(layer_weights(0)[1:], "0b", tok)

    shapes_a = [(D,), w['ssd_conv_w'].shape, w['ffn_conv_w'].shape]
    g_a = _small_allgather(_pack([c[0], w['ssd_conv_w'], w['ffn_conv_w']]), "gather_cond")
    c_all, scw_all, fcw_all = _unpack(g_a, shapes_a)
    ssd_cw = scw_all.transpose(1, 2, 0, 3).reshape(n_ssd, 3, XBC)
    ffn_cw = jnp.concatenate([fcw_all[d] for d in perm], axis=-1)
    ffn_cb = interleave(w['ffn_conv_b'])[:, None, :]

    s_in = jnp.concatenate([c_all, w['c_ctx'][None], jnp.zeros((MOD_ROWS - NDEV - 1, D), F32)], axis=0)
    b_loc = lax.dynamic_slice(w['ada_b'], (0, me * nl), (L, nl))[:, None, :]
    mod_loc = _modulation(s_in, w['ada_w'], b_loc, "modulation")
    g_b = _small_allgather(_pack([mod_loc]), "gather_mod")
    (mod_all,) = _unpack(g_b, [mod_loc.shape])
    mod_lat = lax.dynamic_index_in_dim(mod_all, me, axis=2, keepdims=False)
    mod_ctx = mod_all[:, :, NDEV, :]
    to_mod = lambda t: t.transpose(1, 0, 2).reshape(L, 6, D)
    mod = jnp.stack([to_mod(mod_ctx), to_mod(mod_lat)], axis=1)

    w_in, w_out, w_qkv, w_o = [None] * n_ssd, [None] * n_ssd, [None] * n_att, [None] * n_att
    w_up, w_down = [None] * L, [None] * L

    cos, sin = _rope_tables(t_lat, ctx_rows, hd)
    e_f, e_b = _expand_matrix(H, P, False), _expand_matrix(H, P, True)
    et_f, et_b = e_f.T, e_b.T

    xs = jnp.concatenate([ctx[0], x[0]], axis=0)
    saved = []
    for i in range(L):
        j = i // 2
        s = dict(x0=xs)
        zero_of = lambda t: jnp.minimum(jnp.abs(t[0, 0, 0].astype(F32)), 0.0)
        if i == 0:
            w_in[0] = _cat_cols(gather_wait(first, mod)[0], "ssd_in_cat")
        else:
            got = gather_wait(pending, xs)
            if i % 2 == 0:
                w_in[j], w_out[j] = _cat_cols(got[0], "ssd_in_cat"), got[1].reshape(DI, D)
            else:
                w_qkv[j], w_o[j] = got[0], got[1].reshape(n_q * hd, D)
            w_up[i], w_down[i] = got[2], got[3].reshape(F2 // 2, D)
            if i + 1 < L:
                pending, tok = gather_start(layer_weights(i + 1), f"{i + 1}", zero_of(got[1]))
        nw1, nw2 = w['norm1_w'][i][None] + tok, w['norm2_w'][i][None]
        s['h1'] = _normmod(xs, nw1, mod[i], 0, ctx_rows, "normmod")
        if i % 2 == 0:
            s['zx'] = _mm(s['h1'], w_in[j], tm=768, tn=1152, tk=2048, name="mm_ssd_in")
            s['cw'], s['cb'] = ssd_cw[j], w['ssd_conv_b'][j][None]
            s['xbc'] = _ssd_conv(s['zx'], s['cw'], s['cb'], DI, ctx_rows, "ssd_conv")
            s['dtb'] = jnp.concatenate([w['ssd_dt_bias_f'][j], w['ssd_dt_bias_b'][j]])[None]
            s['alog'] = jnp.concatenate([w['ssd_a_log_f'][j], w['ssd_a_log_b'][j]])[None]
            s['yf'], s['hin_f'] = _ssd_scan(s['xbc'], s['zx'], s['dtb'], s['alog'], e_f, H, ncc, False, "ssd_scan_f")
            s['yb'], s['hin_b'] = _ssd_scan(s['xbc'], s['zx'], s['dtb'], s['alog'], e_b, H, ncc, True, "ssd_scan_b")
            s['dexp'], s['snw'] = jnp.repeat(w['ssd_d'][j], P)[None], w['ssd_norm_w'][j][None]
            if i == 0:
                got = gather_wait(pending, s['yb'])
                w_out[0], w_up[0], w_down[0] = got[0].reshape(DI, D), got[1], got[2].reshape(F2 // 2, D)
                if L > 1:
                    pending, tok = gather_start(layer_weights(1), "1", zero_of(got[0]))
                    s['snw'] = s['snw'] + tok
            s['o'] = _ssd_finish(s['yf'], s['yb'], s['xbc'], s['zx'], s['dexp'], s['snw'], "ssd_finish")
            s['mix'], x1 = _mm(s['o'], w_out[j], tm=768, tn=512, tk=4096, name="mm_ssd_out",
                               resid=xs, gate=mod[i][:, 2], ctx_rows=ctx_rows)
        else:
            s['qkv'] = _mm(s['h1'], w_qkv[j], tm=768, tn=384, tk=2048, name="mm_qkv", bslots=_ident)
            s['qg'], s['kg'] = w['attn_q_gain'][j][None], w['attn_k_gain'][j][None]
            s['qr'], s['kr'], s['vb'] = _qk_prep(s['qkv'], s['qg'], s['kg'], cos, sin, n_q, ctx_rows, "qk_prep")
            s['o'] = _attn_fwd(s['qr'], s['kr'], s['vb'], w['attn_sinks'][j], n_q, ctx_rows, "attn_fwd")
            s['mix'], x1 = _mm(s['o'], w_o[j], tm=768, tn=1024, tk=2048, name="mm_attn_out",
                               resid=xs, gate=mod[i][:, 2], ctx_rows=ctx_rows)
        s['x1'] = x1
        s['h2'] = _normmod(x1, nw2, mod[i], 1, ctx_rows, "normmod")
        s['u'] = _mm(s['h2'], w_up[i], tm=768, tn=1408, tk=2048, name="mm_ffn_up", bslots=_ffn_perm)
        s['a'] = _ffn_mid(s['u'], ffn_cw[i], ffn_cb[i], ctx_rows, "ffn_mid")
        s['f'], xs = _mm(s['a'], w_down[i], tm=768, tn=512, tk=5632, name="mm_ffn_down",
                         resid=x1, gate=mod[i][:, 5], ctx_rows=ctx_rows)
        saved.append(s)

    dx, lacc = _loss_head(xs, tgt[0], ctx_rows, "loss_head")
    loss = lax.psum(lacc[0, 0], ("x", "y", "c"))

    gbig = {name: [None] * w[name].shape[0] for name in BIG}
    gs = {name: [None] * w[name].shape[0] for name in SMALL + ['ssd_conv_w', 'ffn_conv_w']}
    dmod = [None] * L
    lands = {n: lax.empty((NDEV,) + w[n].shape, XFER) for n in BIG}
    pend_early, pend_late, tok = None, None, jnp.zeros((), F32)

    def scatter_start(keys, tag):
        srcs = [gbig[n][j] for n, j in keys]
        for (n, j), g in zip(keys, srcs):
            own = lax.dynamic_index_in_dim(g, me, 0, keepdims=False)
            lands[n] = lax.dynamic_update_slice(lands[n], own[None, None], (me, j, 0, 0))
        ss, rs, srcs, got, t = _xfer_start(srcs, [lands[n] for n, _ in keys], [j for _, j in keys], True, f"scatter_start_{tag}")
        return (ss, rs, srcs, got, keys, tag), t

    def scatter_wait(p, after):
        ss, rs, srcs, got, keys, tag = p
        got = _xfer_wait(ss, rs, srcs, got, [j for _, j in keys], True, after, f"scatter_wait_{tag}")
        for (n, _), t in zip(keys, got):
            lands[n] = t

    for i in reversed(range(L)):
        j = i // 2
        s = saved[i]
        nw1, nw2 = w['norm1_w'][i][None], w['norm2_w'][i][None]
        dm2, dg2 = _gate_bwd(dx, s['f'], mod[i], 1, ctx_rows, "gate_bwd")
        da = _mm(dm2, w_down[i], tb=True, out_dtype=MXU, tm=768, tn=1408, tk=2048, name="mm_ffn_down_dx")
        gbig['ffn_w_down'][i] = _mm(s['a'], dm2, ta=True, out_dtype=XFER, tm=1408, tn=1024, tk=2112,
                                    name="mm_ffn_down_dw").reshape(NDEV, -1, D)
        du, gcw = _ffn_mid_bwd(s['u'], da, ffn_cw[i], ffn_cb[i], ctx_rows, "ffn_mid_bwd")
        gcw = deinterleave(gcw)
        gs['ffn_conv_w'][i], gs['ffn_conv_b'][i] = gcw[0:3], gcw[3]
        dh2 = _mm(du, w_up[i], tb=True, out_dtype=MXU, tm=768, tn=1024, tk=1408, name="mm_ffn_up_dx", bslots=_ffn_perm)
        gbig['ffn_w_up'][i] = _mm(s['h2'], du, ta=True, out_dtype=XFER, tm=1024, tn=1408, tk=2112, name="mm_ffn_up_dw",
                                  oslots=_ffn_perm)
        dx1, sums2 = _normmod_bwd(s['x1'], nw2, mod[i], dh2, dx, 1, ctx_rows, "normmod_bwd")
        dmix, dg1 = _gate_bwd(dx1, s['mix'], mod[i], 0, ctx_rows, "gate_bwd")

        def send_early(after):
            if pend_early is not None:
                scatter_wait(pend_early, after)
            return scatter_start(layer_weights(i)[1:], f"early_{i}")

        def send_late(after):
            if pend_late is not None:
                scatter_wait(pend_late, after)
            return scatter_start(layer_weights(i)[:1], f"late_{i}")

        if i % 2 == 0:
            do = _mm(dmix, w_out[j], tb=True, out_dtype=MXU, tm=768, tn=1024, tk=2048, name="mm_ssd_out_dx")
            gbig['ssd_w_out'][j] = _mm(s['o'], dmix, ta=True, out_dtype=XFER, tm=1024, tn=1024, tk=2112,
                                       name="mm_ssd_out_dw").reshape(NDEV, -1, D)
            pend_early, tok = send_early(do)
            dy, dzx, fs, dd = _ssd_finish_bwd(s['yf'], s['yb'], s['xbc'], s['zx'], s['dexp'], s['snw'] + tok[0, 0], do, et_f,
                                              (T, IN), "ssd_finish_bwd")
            acc = _ssd_scan_bwd(s['xbc'], s['zx'], s['dtb'], s['alog'], e_f, et_f, s['dexp'], s['hin_f'], dy, None,
                                H, ncc, False, "ssd_scan_bwd_f")
            dxbc, ddt, ssm = _ssd_scan_bwd(s['xbc'], s['zx'], s['dtb'], s['alog'], e_b, et_b, s['dexp'], s['hin_b'], dy, acc,
                                           H, ncc, True, "ssd_scan_bwd_b")
            dzx, gscw = _ssd_conv_bwd(s['zx'], dxbc, dzx, s['cw'], s['cb'], DI, ctx_rows, "ssd_conv_bwd")
            dzx = _put_cols(dzx, ddt, IN // LANE - 1, "ssd_put_ddt")
            dwi = _mm(s['h1'], dzx, ta=True, tm=1024, tn=1152, tk=2112, name="mm_ssd_in_dw")
            gbig['ssd_w_in'][j] = _split_cols(dwi, NDEV, XFER, "ssd_in_split")
            pend_late, tok = send_late(gbig['ssd_w_in'][j])
            dh1 = _mm(dzx, w_in[j], tb=True, out_dtype=MXU, tm=768, tn=1024, tk=3456, name="mm_ssd_in_dx", after=tok)
            gs['ssd_conv_w'][j], gs['ssd_conv_b'][j] = gscw[0:3], gscw[3]
            gs['ssd_dt_bias_f'][j], gs['ssd_dt_bias_b'][j] = ssm[1, :H], ssm[1, H:]
            gs['ssd_a_log_f'][j], gs['ssd_a_log_b'][j] = ssm[0, :H], ssm[0, H:]
            gs['ssd_d'][j], gs['ssd_norm_w'][j] = dd[1, :H], fs[0]
        else:
            do = _mm(dmix, w_o[j], tb=True, out_dtype=MXU, tm=768, tn=1024, tk=2048, name="mm_attn_out_dx")
            gbig['attn_w_o'][j] = _mm(s['o'], dmix, ta=True, out_dtype=XFER, tm=1024, tn=1024, tk=2112,
                                      name="mm_attn_out_dw").reshape(NDEV, -1, D)
            pend_early, tok = send_early(do)
            dq, dk, dv, dsk = _attn_bwd(s['qr'], s['kr'], s['vb'], w['attn_sinks'][j] + tok[0, 0], do, n_q, ctx_rows, "attn_bwd")
            dqkv, gg = _qk_prep_bwd(s['qkv'], s['qg'], s['kg'], cos, sin, dq, dk, dv, n_q, ctx_rows, "qk_prep_bwd")
            gbig['attn_w_qkv'][j] = _mm(s['h1'], dqkv, ta=True, out_dtype=XFER, tm=1024, tn=384, tk=2112, name="mm_qkv_dw",
                                        oslots=_ident)
            pend_late, tok = send_late(gbig['attn_w_qkv'][j])
            dh1 = _mm(dqkv, w_qkv[j], tb=True, out_dtype=MXU, tm=768, tn=1024, tk=384, name="mm_qkv_dx", bslots=_ident,
                      after=tok)
            gs['attn_q_gain'][j], gs['attn_k_gain'][j] = gg[0], gg[1]
            gs['attn_sinks'][j] = dsk.reshape(ATTN_KV_HEADS, 8, LANE)[:, 0, :n_q // ATTN_KV_HEADS].reshape(n_q)
        dx, sums1 = _normmod_bwd(s['x0'], nw1, mod[i], dh1, dx1, 0, ctx_rows, "normmod_bwd")
        gs['norm1_w'][i], gs['norm2_w'][i] = sums1[0, 2] + sums1[1, 2], sums2[0, 2] + sums2[1, 2]
        dmod[i] = jnp.stack([sums1[:, 0], sums1[:, 1], dg1[:, 0], sums2[:, 0], sums2[:, 1], dg2[:, 0]], axis=1)
    grad_x = dx[ctx_rows:][None]
    dmod = jnp.stack(dmod)
    dmod_ctx, dmod_lat = dmod[:, 0].reshape(L, 6 * D), dmod[:, 1].reshape(L, 6 * D)
    gs['ada_b'] = dmod_ctx + dmod_lat

    out = {}

    small_g = [jnp.stack(gs[n]) if isinstance(gs[n], list) else gs[n] for n in SMALL]
    extras = [jnp.stack(gs['ssd_conv_w']), jnp.stack(gs['ffn_conv_w'])]
    shapes_c = [w[n].shape for n in SMALL] + [e.shape for e in extras]
    g_c = _small_allgather(_pack(small_g + extras), "gather_small")
    zeros = [jnp.zeros(e.shape, F32) for e in extras]
    res = _adamw_small(g_c, _pack([w[n] for n in SMALL] + zeros), _pack([m[n] for n in SMALL] + zeros),
                       _pack([v[n] for n in SMALL] + zeros), "adamw_small")
    res = [_unpack(r, shapes_c) for r in res]
    for k, n in enumerate(SMALL):
        out[n] = tuple(r[k] for r in res)
    g_scw, g_fcw = res[0][len(SMALL)], res[0][len(SMALL) + 1]
    g_scw = lax.dynamic_index_in_dim(g_scw.reshape(n_ssd, 3, NDEV, XBC // NDEV), me, axis=2, keepdims=False)
    g_fcw = lax.dynamic_index_in_dim(g_fcw.reshape(L, 3, NDEV, G), me, axis=2, keepdims=False)
    conv = ['ssd_conv_w', 'ffn_conv_w']
    res = _adamw_small(_pack([g_scw, g_fcw])[None], _pack([w[n] for n in conv]), _pack([m[n] for n in conv]),
                       _pack([v[n] for n in conv]), "adamw_conv")
    res = [_unpack(r, [w[n].shape for n in conv]) for r in res]
    for k, n in enumerate(conv):
        out[n] = tuple(r[k] for r in res)

    g_m = _small_allgather(jnp.concatenate([dmod_lat, dmod_ctx], axis=0), "gather_dmod")
    all_lat, all_ctx = g_m[:, :L], g_m[:, L:]
    my_cols = lambda t: lax.dynamic_slice(t, (0, 0, me * nl), (NDEV, L, nl))
    dml = jnp.concatenate([my_cols(all_lat), jnp.zeros((MOD_ROWS - NDEV, L, nl), F32)], axis=0)
    g_ada, pc = _modulation_bwd(s_in, dml, my_cols(all_ctx), w['ada_w'], "modulation_bwd")
    out['ada_w'] = _adamw(g_ada[None], w['ada_w'], m['ada_w'], v['ada_w'], "adamw")
    g_d = _small_allgather(_pack([pc[0]]), "gather_cctx")
    res = _cctx_update(g_d, _pack([w['c_ctx']]), _pack([m['c_ctx']]), _pack([v['c_ctx']]), "adamw_cctx")
    out['c_ctx'] = tuple(_unpack(r, [(D,)])[0] for r in res)

    c_new = out['c_ctx'][0]
    scatter_wait(pend_early, c_new)
    last = [n for n, _ in layer_weights(0)[:1]]
    for name in [n for n in BIG if n not in last]:
        out[name] = _adamw(lands[name], w[name], m[name], v[name], "adamw")
    scatter_wait(pend_late, out['ffn_w_down'][0])
    for name in last:
        out[name] = _adamw(lands[name], w[name], m[name], v[name], "adamw")

    return (loss, grad_x) + tuple(out[n][k] for k in range(4) for n in WEIGHTS)


def kernel(x, c, ctx, c_ctx, ada_w, ada_b, norm1_w, norm2_w, ssd_w_in, ssd_conv_w, ssd_conv_b, ssd_dt_bias_f, ssd_dt_bias_b, ssd_a_log_f, ssd_a_log_b, ssd_d, ssd_norm_w, ssd_w_out, attn_w_qkv, attn_q_gain, attn_k_gain, attn_sinks, attn_w_o, ffn_w_up, ffn_conv_w, ffn_conv_b, ffn_w_down, loss_target, m_c_ctx, m_ada_w, m_ada_b, m_norm1_w, m_norm2_w, m_ssd_w_in, m_ssd_conv_w, m_ssd_conv_b, m_ssd_dt_bias_f, m_ssd_dt_bias_b, m_ssd_a_log_f, m_ssd_a_log_b, m_ssd_d, m_ssd_norm_w, m_ssd_w_out, m_attn_w_qkv, m_attn_q_gain, m_attn_k_gain, m_attn_sinks, m_attn_w_o, m_ffn_w_up, m_ffn_conv_w, m_ffn_conv_b, m_ffn_w_down, v_c_ctx, v_ada_w, v_ada_b, v_norm1_w, v_norm2_w, v_ssd_w_in, v_ssd_conv_w, v_ssd_conv_b, v_ssd_dt_bias_f, v_ssd_dt_bias_b, v_ssd_a_log_f, v_ssd_a_log_b, v_ssd_d, v_ssd_norm_w, v_ssd_w_out, v_attn_w_qkv, v_attn_q_gain, v_attn_k_gain, v_attn_sinks, v_attn_w_o, v_ffn_w_up, v_ffn_conv_w, v_ffn_conv_b, v_ffn_w_down):
    w = dict(c_ctx=c_ctx, ada_w=ada_w, ada_b=ada_b, norm1_w=norm1_w, norm2_w=norm2_w, ssd_w_in=ssd_w_in, ssd_conv_w=ssd_conv_w, ssd_conv_b=ssd_conv_b, ssd_dt_bias_f=ssd_dt_bias_f, ssd_dt_bias_b=ssd_dt_bias_b, ssd_a_log_f=ssd_a_log_f, ssd_a_log_b=ssd_a_log_b, ssd_d=ssd_d, ssd_norm_w=ssd_norm_w, ssd_w_out=ssd_w_out, attn_w_qkv=attn_w_qkv, attn_q_gain=attn_q_gain, attn_k_gain=attn_k_gain, attn_sinks=attn_sinks, attn_w_o=attn_w_o, ffn_w_up=ffn_w_up, ffn_conv_w=ffn_conv_w, ffn_conv_b=ffn_conv_b, ffn_w_down=ffn_w_down)
    m = dict(c_ctx=m_c_ctx, ada_w=m_ada_w, ada_b=m_ada_b, norm1_w=m_norm1_w, norm2_w=m_norm2_w, ssd_w_in=m_ssd_w_in, ssd_conv_w=m_ssd_conv_w, ssd_conv_b=m_ssd_conv_b, ssd_dt_bias_f=m_ssd_dt_bias_f, ssd_dt_bias_b=m_ssd_dt_bias_b, ssd_a_log_f=m_ssd_a_log_f, ssd_a_log_b=m_ssd_a_log_b, ssd_d=m_ssd_d, ssd_norm_w=m_ssd_norm_w, ssd_w_out=m_ssd_w_out, attn_w_qkv=m_attn_w_qkv, attn_q_gain=m_attn_q_gain, attn_k_gain=m_attn_k_gain, attn_sinks=m_attn_sinks, attn_w_o=m_attn_w_o, ffn_w_up=m_ffn_w_up, ffn_conv_w=m_ffn_conv_w, ffn_conv_b=m_ffn_conv_b, ffn_w_down=m_ffn_w_down)
    v = dict(c_ctx=v_c_ctx, ada_w=v_ada_w, ada_b=v_ada_b, norm1_w=v_norm1_w, norm2_w=v_norm2_w, ssd_w_in=v_ssd_w_in, ssd_conv_w=v_ssd_conv_w, ssd_conv_b=v_ssd_conv_b, ssd_dt_bias_f=v_ssd_dt_bias_f, ssd_dt_bias_b=v_ssd_dt_bias_b, ssd_a_log_f=v_ssd_a_log_f, ssd_a_log_b=v_ssd_a_log_b, ssd_d=v_ssd_d, ssd_norm_w=v_ssd_norm_w, ssd_w_out=v_ssd_w_out, attn_w_qkv=v_attn_w_qkv, attn_q_gain=v_attn_q_gain, attn_k_gain=v_attn_k_gain, attn_sinks=v_attn_sinks, attn_w_o=v_attn_w_o, ffn_w_up=v_ffn_w_up, ffn_conv_w=v_ffn_conv_w, ffn_conv_b=v_ffn_conv_b, ffn_w_down=v_ffn_w_down)
    return _step(x, c, ctx, w, loss_target, m, v)
```

```python
import functools

import numpy as np
import jax
import jax.numpy as jnp
from jax import lax
from jax.experimental import pallas as pl
from jax.experimental.pallas import tpu as pltpu

F32 = jnp.float32
BF16 = jnp.bfloat16
MXU = BF16
XFER = BF16
NORM_EPS = 1e-6
VMEM_CAP = 56 * 1024 * 1024
HALO = 8
LANE = 128
NDEV = 8

GRID_W = 64
ROPE_THETA = 10000.0
ATTN_KV_HEADS = 4
ATTN_WINDOW = 128
BLK = 128
SSD_GROUPS = 8

ADAM_LR, ADAM_B1, ADAM_B2, ADAM_EPS, ADAM_WD, ADAM_STEP = 0.001, 0.9, 0.999, 1e-08, 0.01, 10

MESH_ID = pl.DeviceIdType.MESH


def _cparams(sem, est_bytes):
    lim = int(min(VMEM_CAP, max(16 * 1024 * 1024, est_bytes * 1.3 + (4 << 20))))
    return pltpu.CompilerParams(dimension_semantics=sem, vmem_limit_bytes=lim)


def _nbytes(shape, dtype):
    return int(np.prod(shape)) * jnp.dtype(dtype).itemsize


def _silu(x):
    return x * jax.nn.sigmoid(x)


def _dsilu(x):
    s = jax.nn.sigmoid(x)
    return s * (1.0 + x * (1.0 - s))


def _split3(v):
    h = v.astype(BF16)
    r = v - h.astype(F32)
    m = r.astype(BF16)
    l = (r - m.astype(F32)).astype(BF16)
    return h, m, l


def _dot(a, b, dn=(((1,), (0,)), ((), ()))):
    return lax.dot_general(a, b, dn, preferred_element_type=F32)


NT = (((1,), (1,)), ((), ()))
TN = (((0,), (0,)), ((), ()))


def _dot3_rhs(sel, v):
    return sum(_dot(sel, p) for p in _split3(v))


def _dot3_lhs(v, sel, dn=(((1,), (0,)), ((), ()))):
    return sum(_dot(p, sel, dn) for p in _split3(v))


def _dot2_stacked(vals, sel):
    pieces = []
    for v in vals:
        h, m, _ = _split3(v)
        pieces += [h, m]
    r = _dot(jnp.concatenate(pieces, axis=0), sel)
    out, row = [], 0
    for v in vals:
        n = v.shape[0]
        out.append(r[row:row + n] + r[row + n:row + 2 * n])
        row += 2 * n
    return out


def _expand2(vals, sel2):
    lhs = []
    for v in vals:
        h, m, _ = _split3(v)
        lhs.append(jnp.concatenate([h, m], axis=1))
    r = _dot(jnp.concatenate(lhs, axis=0), sel2)
    n = vals[0].shape[0]
    return [r[i * n:(i + 1) * n] for i in range(len(vals))]


def _ident(s):
    return s


def _ffn_perm(s):
    return (s % 2) * 4 + s // 2


def _mm(a, b, *, ta=False, tb=False, out_dtype=F32, tm, tn, tk, name, bslots=None, oslots=None,
        resid=None, gate=None, ctx_rows=0, after=None):
    M = a.shape[1] if ta else a.shape[0]
    K = a.shape[0] if ta else a.shape[1]
    if bslots is None:
        N = b.shape[0] if tb else b.shape[1]
    else:
        G = b.shape[2]
        N = b.shape[1] if tb else NDEV * G
        assert (NDEV * G == K) if tb else (b.shape[1] == K)
    tm, tn, tk = min(tm, M), min(tn, N), min(tk, K)
    if bslots is not None:
        if tb:
            tk = min(tk, G)
            assert G % tk == 0
        else:
            tn = min(tn, G)
            assert G % tn == 0
    if oslots is not None:
        Go = N // NDEV
        tn = min(tn, Go)
        assert Go % tn == 0
    assert M % tm == 0 and N % tn == 0 and K % tk == 0, (name, M, N, K, tm, tn, tk)
    nk = K // tk
    fused = resid is not None
    dn = (((0 if ta else 1,), (1 if tb else 0,)), ((), ()))

    n_in = 2 + (2 if fused else 0) + (1 if after is not None else 0)

    def body(*refs):
        a_ref, b_ref = refs[0], refs[1]
        if fused:
            r_ref, g_ref = refs[2], refs[3]
            o_ref, x_ref = refs[n_in], refs[n_in + 1]
            rest = refs[n_in + 2:]
        else:
            o_ref = refs[n_in]
            rest = refs[n_in + 1:]
        bv = b_ref[0] if bslots is not None else b_ref[...]
        p = lax.dot_general(a_ref[...].astype(MXU), bv.astype(MXU), dn, preferred_element_type=F32)

        def finish(acc):
            if oslots is not None:
                o_ref[0] = acc.astype(o_ref.dtype)
            else:
                o_ref[...] = acc.astype(o_ref.dtype)
            if fused:
                row = pl.program_id(0) * tm + lax.broadcasted_iota(jnp.int32, (tm, 1), 0)
                g = jnp.where(row < ctx_rows, g_ref[0:1, :], g_ref[1:2, :])
                x_ref[...] = r_ref[...] + g * acc

        if nk == 1:
            finish(p)
        else:
            acc_ref = rest[0]
            k = pl.program_id(2)

            @pl.when(k == 0)
            def _():
                acc_ref[...] = p

            @pl.when(k > 0)
            def _():
                acc_ref[...] += p

            @pl.when(k == nk - 1)
            def _():
                finish(acc_ref[...])

    a_spec = pl.BlockSpec((tk, tm), lambda i, j, k: (k, i)) if ta else pl.BlockSpec((tm, tk), lambda i, j, k: (i, k))
    if bslots is None:
        b_spec = pl.BlockSpec((tn, tk), lambda i, j, k: (j, k)) if tb else pl.BlockSpec((tk, tn), lambda i, j, k: (k, j))
    elif tb:
        kpg = G // tk
        b_spec = pl.BlockSpec((1, tn, tk), lambda i, j, k: (bslots(k // kpg), j, k % kpg))
    else:
        npg = G // tn
        b_spec = pl.BlockSpec((1, tk, tn), lambda i, j, k: (bslots(j // npg), k, j % npg))
    if oslots is None:
        o_spec = pl.BlockSpec((tm, tn), lambda i, j, k: (i, j))
        o_shape = jax.ShapeDtypeStruct((M, N), out_dtype)
    else:
        opg = Go // tn
        o_spec = pl.BlockSpec((1, tm, tn), lambda i, j, k: (oslots(j // opg), i, j % opg))
        o_shape = jax.ShapeDtypeStruct((NDEV, M, Go), out_dtype)
    in_specs = [a_spec, b_spec]
    out_shape = [o_shape]
    out_specs = [o_spec]
    args = [a, b]
    est = 2 * (_nbytes((tm, tk), a.dtype) + _nbytes((tk, tn), b.dtype) + _nbytes((tm, tn), out_dtype)) + 3 * _nbytes((tm, tn), F32)
    if fused:
        in_specs += [o_spec, pl.BlockSpec((2, tn), lambda i, j, k: (0, j))]
        out_shape.append(jax.ShapeDtypeStruct((M, N), F32))
        out_specs.append(o_spec)
        args += [resid, gate]
        est += 4 * _nbytes((tm, tn), F32)
    if after is not None:
        in_specs.append(pl.BlockSpec((8, LANE), lambda i, j, k: (0, 0)))
        args.append(after)
    scratch = [] if nk == 1 else [pltpu.VMEM((tm, tn), F32)]
    res = pl.pallas_call(
        body, name=name, grid=(M // tm, N // tn, nk), in_specs=in_specs, out_specs=out_specs, out_shape=out_shape,
        scratch_shapes=scratch, compiler_params=_cparams(("parallel", "parallel", "arbitrary"), est),
    )(*args)
    return res if fused else res[0]


def _stream_of(i, tr, ctx_rows):
    return jnp.where(i * tr < ctx_rows, 0, 1)


def _acc_by_stream(sums_ref, part, i, n_ctx):
    @pl.when((i == 0) | (i == n_ctx))
    def _():
        sums_ref[0] = part

    @pl.when((i != 0) & (i != n_ctx))
    def _():
        sums_ref[0] += part


def _normmod(x, nw, mod, which, ctx_rows, name, tr=256):
    T, D = x.shape
    tr = min(tr, ctx_rows)
    assert T % tr == 0 and ctx_rows % tr == 0
    s_sh, s_sc = 3 * which, 3 * which + 1

    def body(x_ref, nw_ref, mod_ref, h_ref):
        xv = x_ref[...]
        r = lax.rsqrt(jnp.mean(xv * xv, axis=-1, keepdims=True) + NORM_EPS)
        y = (xv * r) * nw_ref[...]
        h_ref[...] = (y * (1.0 + mod_ref[0, s_sc:s_sc + 1, :]) + mod_ref[0, s_sh:s_sh + 1, :]).astype(h_ref.dtype)

    return pl.pallas_call(
        body, name=name, grid=(T // tr,),
        in_specs=[pl.BlockSpec((tr, D), lambda i: (i, 0)), pl.BlockSpec((1, D), lambda i: (0, 0)),
                  pl.BlockSpec((1, 6, D), lambda i: (_stream_of(i, tr, ctx_rows), 0, 0))],
        out_specs=pl.BlockSpec((tr, D), lambda i: (i, 0)),
        out_shape=jax.ShapeDtypeStruct((T, D), MXU),
        compiler_params=_cparams(("parallel",), 10 * _nbytes((tr, D), F32)),
    )(x, nw, mod)


def _normmod_bwd(x, nw, mod, dh, dx_in, which, ctx_rows, name, tr=256):
    T, D = x.shape
    tr = min(tr, ctx_rows)
    s_sc = 3 * which + 1
    n_ctx = ctx_rows // tr

    def body(x_ref, nw_ref, mod_ref, dh_ref, dxi_ref, dx_ref, sums_ref):
        i = pl.program_id(0)
        xv = x_ref[...]
        r = lax.rsqrt(jnp.mean(xv * xv, axis=-1, keepdims=True) + NORM_EPS)
        xh = xv * r
        dh_v = dh_ref[...].astype(F32)
        sc1 = 1.0 + mod_ref[0, s_sc:s_sc + 1, :]
        nwv = nw_ref[...]
        dxh = dh_v * (nwv * sc1)
        dx_ref[...] = dxi_ref[...] + r * (dxh - xh * jnp.mean(dxh * xh, axis=-1, keepdims=True))
        t = dh_v * xh
        part = jnp.concatenate([jnp.sum(dh_v, axis=0, keepdims=True), jnp.sum(t * nwv, axis=0, keepdims=True),
                                jnp.sum(t * sc1, axis=0, keepdims=True), jnp.zeros((5, D), F32)], axis=0)
        _acc_by_stream(sums_ref, part, i, n_ctx)

    row = pl.BlockSpec((tr, D), lambda i: (i, 0))
    return pl.pallas_call(
        body, name=name, grid=(T // tr,),
        in_specs=[row, pl.BlockSpec((1, D), lambda i: (0, 0)),
                  pl.BlockSpec((1, 6, D), lambda i: (_stream_of(i, tr, ctx_rows), 0, 0)), row, row],
        out_specs=[row, pl.BlockSpec((1, 8, D), lambda i: (_stream_of(i, tr, ctx_rows), 0, 0))],
        out_shape=[jax.ShapeDtypeStruct((T, D), F32), jax.ShapeDtypeStruct((2, 8, D), F32)],
        compiler_params=_cparams(("arbitrary",), 16 * _nbytes((tr, D), F32)),
    )(x, nw, mod, dh, dx_in)


def _gate_bwd(dx, mix, mod, which, ctx_rows, name, tr=256):
    T, D = dx.shape
    tr = min(tr, ctx_rows)
    s_g = 3 * which + 2
    n_ctx = ctx_rows // tr

    def body(dx_ref, mix_ref, mod_ref, dm_ref, sums_ref):
        i = pl.program_id(0)
        dxv = dx_ref[...]
        dm_ref[...] = (dxv * mod_ref[0, s_g:s_g + 1, :]).astype(dm_ref.dtype)
        part = jnp.concatenate([jnp.sum(dxv * mix_ref[...], axis=0, keepdims=True), jnp.zeros((7, D), F32)], axis=0)
        _acc_by_stream(sums_ref, part, i, n_ctx)

    row = pl.BlockSpec((tr, D), lambda i: (i, 0))
    return pl.pallas_call(
        body, name=name, grid=(T // tr,),
        in_specs=[row, row, pl.BlockSpec((1, 6, D), lambda i: (_stream_of(i, tr, ctx_rows), 0, 0))],
        out_specs=[row, pl.BlockSpec((1, 8, D), lambda i: (_stream_of(i, tr, ctx_rows), 0, 0))],
        out_shape=[jax.ShapeDtypeStruct((T, D), MXU), jax.ShapeDtypeStruct((2, 8, D), F32)],
        compiler_params=_cparams(("arbitrary",), 10 * _nbytes((tr, D), F32)),
    )(dx, mix, mod)


def _halo_specs(tr, tn, n_row_tiles, col_of):
    g = tr // HALO
    last = n_row_tiles * g - 1
    return [pl.BlockSpec((HALO, tn), lambda j, i: (jnp.maximum(i * g - 1, 0), col_of(j))),
            pl.BlockSpec((tr, tn), lambda j, i: (i, col_of(j))),
            pl.BlockSpec((HALO, tn), lambda j, i: (jnp.minimum((i + 1) * g, last), col_of(j)))]


def _ext(p_ref, m_ref, n_ref):
    return jnp.concatenate([p_ref[...].astype(F32), m_ref[...].astype(F32), n_ref[...].astype(F32)], axis=0)


def _seq_masks(i, tr, ctx_rows, total_rows):
    row = i * tr - HALO + lax.broadcasted_iota(jnp.int32, (tr + 2 * HALO, 1), 0)
    has_prev = (row != 0) & (row != ctx_rows)
    has_next = (row != ctx_rows - 1) & (row != total_rows - 1)
    return has_prev, has_next


def _shift_down(e):
    return pltpu.roll(e, 1, 0)


def _shift_up(e):
    return pltpu.roll(e, e.shape[0] - 1, 0)


def _neighbours(e, masks):
    prev, nxt = _shift_down(e), _shift_up(e)
    if masks is not None:
        prev, nxt = jnp.where(masks[0], prev, 0.0), jnp.where(masks[1], nxt, 0.0)
    return prev, nxt


def _conv3(e, prev, nxt, w):
    return prev * w[0:1, :] + e * w[1:2, :] + nxt * w[2:3, :]


def _conv3_t(d, w, masks):
    from_prev, from_next = _neighbours(d, masks)
    return from_next * w[0:1, :] + d * w[1:2, :] + from_prev * w[2:3, :]


def _conv_wgrad(d, e, prev, nxt):
    c = slice(HALO, e.shape[0] - HALO)
    dc = d[c]
    return jnp.concatenate([jnp.sum(dc * prev[c], axis=0, keepdims=True), jnp.sum(dc * e[c], axis=0, keepdims=True),
                            jnp.sum(dc * nxt[c], axis=0, keepdims=True), jnp.sum(dc, axis=0, keepdims=True),
                            jnp.zeros((4, e.shape[1]), F32)], axis=0)


def _per_tile_kind(i, tr, ctx_rows, total_rows, fn):
    n_ctx, n_all = ctx_rows // tr, total_rows // tr
    at_end = (i == 0) | (i == n_ctx - 1) | (i == n_ctx) | (i == n_all - 1)

    @pl.when(at_end)
    def _():
        fn(_seq_masks(i, tr, ctx_rows, total_rows))

    @pl.when(jnp.logical_not(at_end))
    def _():
        fn(None)


def _acc_first(ref, part, i):
    @pl.when(i == 0)
    def _():
        ref[...] = part

    @pl.when(i > 0)
    def _():
        ref[...] += part


def _ffn_mid(u, cw, cb, ctx_rows, name, tr=128):
    T, F2 = u.shape
    G = F2 // NDEV
    tr = min(tr, ctx_rows)
    nr, nc = T // tr, NDEV // 2

    def body(up, um, un, w_ref, b_ref, a_ref):
        def tile(masks):
            e = _ext(up, um, un)
            uc = _conv3(e, *_neighbours(e, masks), w_ref[...])[HALO:HALO + tr] + b_ref[...]
            a_ref[...] = (_silu(uc[:, G:]) * uc[:, :G]).astype(a_ref.dtype)

        _per_tile_kind(pl.program_id(1), tr, ctx_rows, T, tile)

    return pl.pallas_call(
        body, name=name, grid=(nc, nr),
        in_specs=_halo_specs(tr, 2 * G, nr, lambda j: j) + [pl.BlockSpec((3, 2 * G), lambda j, i: (0, j)),
                                                             pl.BlockSpec((1, 2 * G), lambda j, i: (0, j))],
        out_specs=pl.BlockSpec((tr, G), lambda j, i: (i, j)),
        out_shape=jax.ShapeDtypeStruct((T, F2 // 2), MXU),
        compiler_params=_cparams(("parallel", "parallel"), 12 * _nbytes((tr + 16, 2 * G), F32)),
    )(u, u, u, cw, cb)


def _ffn_mid_bwd(u, da, cw, cb, ctx_rows, name, tr=128):
    T, F2 = u.shape
    G = F2 // NDEV
    tr = min(tr, ctx_rows)
    nr, nc = T // tr, NDEV // 2

    def body(up, um, un, dp, dm, dn_, w_ref, b_ref, du_ref, gw_ref):
        i = pl.program_id(1)

        def tile(masks):
            e = _ext(up, um, un)
            w = w_ref[...]
            prev, nxt = _neighbours(e, masks)
            uc = _conv3(e, prev, nxt, w) + b_ref[...]
            val, gt = uc[:, :G], uc[:, G:]
            dav = _ext(dp, dm, dn_)
            sg = jax.nn.sigmoid(gt)
            duc = jnp.concatenate([dav * (gt * sg), dav * val * (sg * (1.0 + gt * (1.0 - sg)))], axis=1)
            du_ref[...] = _conv3_t(duc, w, masks)[HALO:HALO + tr].astype(du_ref.dtype)
            _acc_first(gw_ref, _conv_wgrad(duc, e, prev, nxt), i)

        _per_tile_kind(i, tr, ctx_rows, T, tile)

    du, gw = pl.pallas_call(
        body, name=name, grid=(nc, nr),
        in_specs=(_halo_specs(tr, 2 * G, nr, lambda j: j) + _halo_specs(tr, G, nr, lambda j: j)
                  + [pl.BlockSpec((3, 2 * G), lambda j, i: (0, j)), pl.BlockSpec((1, 2 * G), lambda j, i: (0, j))]),
        out_specs=[pl.BlockSpec((tr, 2 * G), lambda j, i: (i, j)), pl.BlockSpec((8, 2 * G), lambda j, i: (0, j))],
        out_shape=[jax.ShapeDtypeStruct((T, F2), MXU), jax.ShapeDtypeStruct((8, F2), F32)],
        compiler_params=_cparams(("parallel", "arbitrary"), 24 * _nbytes((tr + 16, 2 * G), F32)),
    )(u, u, u, da, da, da, cw, cb)
    return du, gw


def _ssd_conv(zx, cw, cb, col0, ctx_rows, name, tr=256, tn=512):
    T = zx.shape[0]
    C = cw.shape[1]
    tr, tn = min(tr, ctx_rows), min(tn, C)
    assert C % tn == 0 and col0 % tn == 0
    nr, nc, cb0 = T // tr, C // tn, col0 // tn

    def body(zp, zm, zn, w_ref, b_ref, o_ref):
        def tile(masks):
            e = _ext(zp, zm, zn)
            o_ref[...] = _silu(_conv3(e, *_neighbours(e, masks), w_ref[...])[HALO:HALO + tr] + b_ref[...])

        _per_tile_kind(pl.program_id(1), tr, ctx_rows, T, tile)

    return pl.pallas_call(
        body, name=name, grid=(nc, nr),
        in_specs=_halo_specs(tr, tn, nr, lambda j: j + cb0) + [pl.BlockSpec((3, tn), lambda j, i: (0, j)),
                                                                pl.BlockSpec((1, tn), lambda j, i: (0, j))],
        out_specs=pl.BlockSpec((tr, tn), lambda j, i: (i, j)),
        out_shape=jax.ShapeDtypeStruct((T, C), F32),
        compiler_params=_cparams(("parallel", "parallel"), 12 * _nbytes((tr + 16, tn), F32)),
    )(zx, zx, zx, cw, cb)


def _ssd_conv_bwd(zx, dxbc, dzx, cw, cb, col0, ctx_rows, name, tr=256, tn=512):
    T = zx.shape[0]
    C = cw.shape[1]
    tr, tn = min(tr, ctx_rows), min(tn, C)
    nr, nc, cb0 = T // tr, C // tn, col0 // tn

    def body(zp, zm, zn, dp, dm, dn_, w_ref, b_ref, dzx_in, dz_ref, gw_ref):
        del dzx_in
        i = pl.program_id(1)

        def tile(masks):
            e = _ext(zp, zm, zn)
            w = w_ref[...]
            prev, nxt = _neighbours(e, masks)
            pre = _conv3(e, prev, nxt, w) + b_ref[...]
            dpre = _ext(dp, dm, dn_) * _dsilu(pre)
            dz_ref[...] = _conv3_t(dpre, w, masks)[HALO:HALO + tr].astype(dz_ref.dtype)
            _acc_first(gw_ref, _conv_wgrad(dpre, e, prev, nxt), i)

        _per_tile_kind(i, tr, ctx_rows, T, tile)

    return pl.pallas_call(
        body, name=name, grid=(nc, nr),
        in_specs=(_halo_specs(tr, tn, nr, lambda j: j + cb0) + _halo_specs(tr, tn, nr, lambda j: j)
                  + [pl.BlockSpec((3, tn), lambda j, i: (0, j)), pl.BlockSpec((1, tn), lambda j, i: (0, j)),
                     pl.BlockSpec(memory_space=pl.ANY)]),
        out_specs=[pl.BlockSpec((tr, tn), lambda j, i: (i, j + cb0)), pl.BlockSpec((8, tn), lambda j, i: (0, j))],
        out_shape=[jax.ShapeDtypeStruct(dzx.shape, dzx.dtype), jax.ShapeDtypeStruct((8, C), F32)],
        input_output_aliases={8: 0},
        compiler_params=_cparams(("parallel", "arbitrary"), 24 * _nbytes((tr + 16, tn), F32)),
    )(zx, zx, zx, dxbc, dxbc, dxbc, cw, cb, dzx)


def _cat_cols(w3, name, tr=256):
    n, K, G = w3.shape
    tr = min(tr, K)

    def body(w_ref, o_ref):
        o_ref[...] = jnp.concatenate([w_ref[d].astype(F32) for d in range(n)], axis=1).astype(o_ref.dtype)

    return pl.pallas_call(
        body, name=name, grid=(K // tr,),
        in_specs=[pl.BlockSpec((n, tr, G), lambda i: (0, i, 0))], out_specs=pl.BlockSpec((tr, n * G), lambda i: (i, 0)),
        out_shape=jax.ShapeDtypeStruct((K, n * G), w3.dtype),
        compiler_params=_cparams(("parallel",), 6 * _nbytes((tr, n * G), F32)),
    )(w3)


def _split_cols(g, n, out_dtype, name, tr=256):
    K, NG = g.shape
    G = NG // n
    tr = min(tr, K)

    def body(g_ref, o_ref):
        for d in range(n):
            o_ref[d] = g_ref[:, d * G:(d + 1) * G].astype(o_ref.dtype)

    return pl.pallas_call(
        body, name=name, grid=(K // tr,),
        in_specs=[pl.BlockSpec((tr, NG), lambda i: (i, 0))], out_specs=pl.BlockSpec((n, tr, G), lambda i: (0, i, 0)),
        out_shape=jax.ShapeDtypeStruct((n, K, G), out_dtype),
        compiler_params=_cparams(("parallel",), 6 * _nbytes((tr, NG), F32)),
    )(g)


def _put_cols(dst, src, col_blk, name, tr=256):
    T, W = src.shape
    tr = min(tr, T)

    def body(s_ref, d_in, o_ref):
        del d_in
        o_ref[...] = s_ref[...].astype(o_ref.dtype)

    return pl.pallas_call(
        body, name=name, grid=(T // tr,),
        in_specs=[pl.BlockSpec((tr, W), lambda i: (i, 0)), pl.BlockSpec(memory_space=pl.ANY)],
        out_specs=pl.BlockSpec((tr, W), lambda i: (i, col_blk)),
        out_shape=jax.ShapeDtypeStruct(dst.shape, dst.dtype),
        input_output_aliases={1: 0},
        compiler_params=_cparams(("parallel",), 8 * _nbytes((tr, W), F32)),
    )(src, dst)


def _rope_tables(t_lat, ctx_rows, hd):
    half, quarter = hd // 2, hd // 4
    pos = jnp.arange(t_lat)
    row = (pos // GRID_W).astype(F32)
    col = (pos % GRID_W).astype(F32)
    inv_freq = ROPE_THETA ** (-jnp.arange(0, half, 2, dtype=F32) / half)
    ar, ac = row[:, None] * inv_freq[None, :], col[:, None] * inv_freq[None, :]
    cos = jnp.concatenate([jnp.cos(ar), jnp.cos(ar), jnp.cos(ac), jnp.cos(ac)], axis=1)
    sin = jnp.concatenate([-jnp.sin(ar), jnp.sin(ar), -jnp.sin(ac), jnp.sin(ac)], axis=1)
    del quarter
    cos = jnp.concatenate([jnp.ones((ctx_rows, hd), F32), cos], axis=0)
    sin = jnp.concatenate([jnp.zeros((ctx_rows, hd), F32), sin], axis=0)
    return cos, sin


def _partner(y):
    hd = y.shape[1]
    q = hd // 4
    lane = lax.broadcasted_iota(jnp.int32, y.shape, 1)
    return jnp.where((lane % (2 * q)) < q, pltpu.roll(y, hd - q, 1), pltpu.roll(y, q, 1))


def _qk_prep(qkv, qg, kg, cos, sin, n_q, ctx_rows, name, tr=256):
    T = qkv.shape[0]
    hd = qg.shape[1]
    n_kv = ATTN_KV_HEADS
    tr = min(tr, ctx_rows)

    def body(x_ref, qg_ref, kg_ref, c_ref, s_ref, q_ref, k_ref, v_ref):
        cv, sv = c_ref[...], s_ref[...]
        for h in range(n_q + n_kv):
            xh = x_ref[:, h * hd:(h + 1) * hd]
            r = lax.rsqrt(jnp.mean(xh * xh, axis=-1, keepdims=True) + NORM_EPS)
            y = (xh * r) * (qg_ref[...] if h < n_q else kg_ref[...])
            rot = y * cv + _partner(y) * sv
            if h < n_q:
                q_ref[:, h * hd:(h + 1) * hd] = rot.astype(q_ref.dtype)
            else:
                k_ref[:, (h - n_q) * hd:(h - n_q + 1) * hd] = rot.astype(k_ref.dtype)
        v_ref[...] = x_ref[:, (n_q + n_kv) * hd:].astype(v_ref.dtype)

    W = qkv.shape[1]
    return pl.pallas_call(
        body, name=name, grid=(T // tr,),
        in_specs=[pl.BlockSpec((tr, W), lambda i: (i, 0)), pl.BlockSpec((1, hd), lambda i: (0, 0)),
                  pl.BlockSpec((1, hd), lambda i: (0, 0)), pl.BlockSpec((tr, hd), lambda i: (i, 0)),
                  pl.BlockSpec((tr, hd), lambda i: (i, 0))],
        out_specs=[pl.BlockSpec((tr, n_q * hd), lambda i: (i, 0)), pl.BlockSpec((tr, n_kv * hd), lambda i: (i, 0)),
                   pl.BlockSpec((tr, n_kv * hd), lambda i: (i, 0))],
        out_shape=[jax.ShapeDtypeStruct((T, n_q * hd), MXU), jax.ShapeDtypeStruct((T, n_kv * hd), MXU),
                   jax.ShapeDtypeStruct((T, n_kv * hd), MXU)],
        compiler_params=_cparams(("parallel",), 6 * _nbytes((tr, W), F32)),
    )(qkv, qg, kg, cos, sin)


def _qk_prep_bwd(qkv, qg, kg, cos, sin, dq, dk, dv, n_q, ctx_rows, name, tr=256):
    T, W = qkv.shape
    hd = qg.shape[1]
    n_kv = ATTN_KV_HEADS
    tr = min(tr, ctx_rows)

    def body(x_ref, qg_ref, kg_ref, c_ref, s_ref, dq_ref, dk_ref, dv_ref, o_ref, g_ref):
        i = pl.program_id(0)
        cv, sv = c_ref[...], s_ref[...]
        gq = jnp.zeros((1, hd), F32)
        gk = jnp.zeros((1, hd), F32)
        for h in range(n_q + n_kv):
            xh = x_ref[:, h * hd:(h + 1) * hd]
            gain = qg_ref[...] if h < n_q else kg_ref[...]
            drot = (dq_ref[:, h * hd:(h + 1) * hd] if h < n_q else dk_ref[:, (h - n_q) * hd:(h - n_q + 1) * hd]).astype(F32)
            dy = drot * cv + _partner(drot * sv)
            r = lax.rsqrt(jnp.mean(xh * xh, axis=-1, keepdims=True) + NORM_EPS)
            xn = xh * r
            gsum = jnp.sum(dy * xn, axis=0, keepdims=True)
            if h < n_q:
                gq = gq + gsum
            else:
                gk = gk + gsum
            dxn = dy * gain
            o_ref[:, h * hd:(h + 1) * hd] = (r * (dxn - xn * jnp.mean(dxn * xn, axis=-1, keepdims=True))).astype(o_ref.dtype)
        o_ref[:, (n_q + n_kv) * hd:] = dv_ref[...].astype(o_ref.dtype)
        _acc_first(g_ref, jnp.concatenate([gq, gk, jnp.zeros((6, hd), F32)], axis=0), i)

    return pl.pallas_call(
        body, name=name, grid=(T // tr,),
        in_specs=[pl.BlockSpec((tr, W), lambda i: (i, 0)), pl.BlockSpec((1, hd), lambda i: (0, 0)),
                  pl.BlockSpec((1, hd), lambda i: (0, 0)), pl.BlockSpec((tr, hd), lambda i: (i, 0)),
                  pl.BlockSpec((tr, hd), lambda i: (i, 0)), pl.BlockSpec((tr, n_q * hd), lambda i: (i, 0)),
                  pl.BlockSpec((tr, n_kv * hd), lambda i: (i, 0)), pl.BlockSpec((tr, n_kv * hd), lambda i: (i, 0))],
        out_specs=[pl.BlockSpec((tr, W), lambda i: (i, 0)), pl.BlockSpec((8, hd), lambda i: (0, 0))],
        out_shape=[jax.ShapeDtypeStruct((T, W), MXU), jax.ShapeDtypeStruct((8, hd), F32)],
        compiler_params=_cparams(("arbitrary",), 8 * _nbytes((tr, W), F32)),
    )(qkv, qg, kg, cos, sin, dq, dk, dv)


def _attn_scores(q_ref, k_ref, sink_ref, h, qb, ctx_rows, nb, hd, grp):
    scale = hd ** -0.5
    w0 = jnp.clip(qb - 1, 0, nb - 3) * BLK
    w0 = pl.multiple_of(w0, BLK)
    qv = q_ref[...]
    qs = jnp.concatenate([qv[:, g * hd:(g + 1) * hd] for g in range(grp)], axis=0)
    kc = k_ref[0:ctx_rows, :]
    kb = k_ref[pl.ds(w0, 3 * BLK), :]
    s_c = _dot(qs, kc, NT) * scale
    s_b = _dot(qs, kb, NT) * scale
    n = grp * BLK
    qpos = qb * BLK + lax.broadcasted_iota(jnp.int32, (n, 3 * BLK), 0) % BLK
    kpos = w0 + lax.broadcasted_iota(jnp.int32, (n, 3 * BLK), 1)
    ok = (jnp.abs(kpos - qpos) <= ATTN_WINDOW) & (kpos >= ctx_rows) & (qpos >= ctx_rows)
    s_b = jnp.where(ok, s_b, -jnp.inf)
    gi = lax.broadcasted_iota(jnp.int32, (n, 1), 0) // BLK
    sink = jnp.zeros((n, 1), F32)
    for g in range(grp):
        sink = jnp.where(gi == g, sink_ref[h * grp + g], sink)
    m = jnp.maximum(jnp.maximum(jnp.max(s_c, axis=1, keepdims=True), jnp.max(s_b, axis=1, keepdims=True)), sink)
    e_c, e_b, e_s = jnp.exp(s_c - m), jnp.exp(s_b - m), jnp.exp(sink - m)
    inv = 1.0 / (jnp.sum(e_c, axis=1, keepdims=True) + jnp.sum(e_b, axis=1, keepdims=True) + e_s)
    return qs, kc, kb, w0, e_c * inv, e_b * inv, e_s * inv, gi


def _attn_fwd(qr, kr, vb, sinks, n_q, ctx_rows, name):
    T = qr.shape[0]
    n_kv = ATTN_KV_HEADS
    grp = n_q // n_kv
    hd = qr.shape[1] // n_q
    nb = T // BLK

    def body(sink_ref, q_ref, k_ref, v_ref, o_ref):
        h, qb = pl.program_id(0), pl.program_id(1)
        _, _, _, w0, p_c, p_b, _, _ = _attn_scores(q_ref, k_ref, sink_ref, h, qb, ctx_rows, nb, hd, grp)
        o = _dot(p_c.astype(MXU), v_ref[0:ctx_rows, :]) + _dot(p_b.astype(MXU), v_ref[pl.ds(w0, 3 * BLK), :])
        o_ref[...] = jnp.concatenate([o[g * BLK:(g + 1) * BLK] for g in range(grp)], axis=1).astype(o_ref.dtype)

    return pl.pallas_call(
        body, name=name, grid=(n_kv, nb),
        in_specs=[pl.BlockSpec(memory_space=pltpu.SMEM), pl.BlockSpec((BLK, grp * hd), lambda h, i: (i, h)),
                  pl.BlockSpec((T, hd), lambda h, i: (0, h)), pl.BlockSpec((T, hd), lambda h, i: (0, h))],
        out_specs=pl.BlockSpec((BLK, grp * hd), lambda h, i: (i, h)),
        out_shape=jax.ShapeDtypeStruct((T, n_q * hd), MXU),
        compiler_params=_cparams(("parallel", "arbitrary"), 4 * _nbytes((T, hd), MXU) + 24 * _nbytes((grp * BLK, 5 * BLK), F32)),
    )(sinks, qr, kr, vb)


def _attn_bwd(qr, kr, vb, sinks, do, n_q, ctx_rows, name):
    T = qr.shape[0]
    n_kv = ATTN_KV_HEADS
    grp = n_q // n_kv
    hd = qr.shape[1] // n_q
    nb = T // BLK
    scale = hd ** -0.5

    def body(sink_ref, q_ref, k_ref, v_ref, do_ref, dq_ref, dk_ref, dv_ref, ds_ref):
        h, qb = pl.program_id(0), pl.program_id(1)
        qs, kc, kb, w0, p_c, p_b, p_s, gi = _attn_scores(q_ref, k_ref, sink_ref, h, qb, ctx_rows, nb, hd, grp)
        dov = do_ref[...]
        dos = jnp.concatenate([dov[:, g * hd:(g + 1) * hd] for g in range(grp)], axis=0)
        vc = v_ref[0:ctx_rows, :]
        vw = v_ref[pl.ds(w0, 3 * BLK), :]
        dp_c = _dot(dos, vc, NT)
        dp_b = _dot(dos, vw, NT)
        delta = jnp.sum(p_c * dp_c, axis=1, keepdims=True) + jnp.sum(p_b * dp_b, axis=1, keepdims=True)
        ds_c = (p_c * (dp_c - delta) * scale).astype(MXU)
        ds_b = (p_b * (dp_b - delta) * scale).astype(MXU)
        dq = _dot(ds_c, kc) + _dot(ds_b, kb)
        dq_ref[...] = jnp.concatenate([dq[g * BLK:(g + 1) * BLK] for g in range(grp)], axis=1)

        @pl.when(qb == 0)
        def _():
            dk_ref[...] = jnp.zeros(dk_ref.shape, F32)
            dv_ref[...] = jnp.zeros(dv_ref.shape, F32)

        dk_ref[0:ctx_rows, :] += _dot(ds_c, qs, TN)
        dv_ref[0:ctx_rows, :] += _dot(p_c.astype(MXU), dos, TN)
        dk_ref[pl.ds(w0, 3 * BLK), :] += _dot(ds_b, qs, TN)
        dv_ref[pl.ds(w0, 3 * BLK), :] += _dot(p_b.astype(MXU), dos, TN)
        t = -(p_s * delta)
        lane = lax.broadcasted_iota(jnp.int32, (8, LANE), 1)
        part = jnp.zeros((8, LANE), F32)
        for g in range(grp):
            part = jnp.where(lane == g, jnp.sum(jnp.where(gi == g, t, 0.0)), part)
        _acc_first(ds_ref, part, qb)

    return pl.pallas_call(
        body, name=name, grid=(n_kv, nb),
        in_specs=[pl.BlockSpec(memory_space=pltpu.SMEM), pl.BlockSpec((BLK, grp * hd), lambda h, i: (i, h)),
                  pl.BlockSpec((T, hd), lambda h, i: (0, h)), pl.BlockSpec((T, hd), lambda h, i: (0, h)),
                  pl.BlockSpec((BLK, grp * hd), lambda h, i: (i, h))],
        out_specs=[pl.BlockSpec((BLK, grp * hd), lambda h, i: (i, h)), pl.BlockSpec((T, hd), lambda h, i: (0, h)),
                   pl.BlockSpec((T, hd), lambda h, i: (0, h)), pl.BlockSpec((8, LANE), lambda h, i: (h, 0))],
        out_shape=[jax.ShapeDtypeStruct((T, n_q * hd), F32), jax.ShapeDtypeStruct((T, n_kv * hd), F32),
                   jax.ShapeDtypeStruct((T, n_kv * hd), F32), jax.ShapeDtypeStruct((n_kv * 8, LANE), F32)],
        compiler_params=_cparams(("parallel", "arbitrary"), 4 * _nbytes((T, hd), MXU) + 4 * _nbytes((T, hd), F32)
                                 + 40 * _nbytes((grp * BLK, 5 * BLK), F32)),
    )(sinks, qr, kr, vb, do)


def _chunk_order(s, n_chunks, n_ctx, rev):
    if not rev:
        return s
    return jnp.where(s < n_ctx, n_ctx - 1 - s, n_chunks - 1 + n_ctx - s)


def _softplus(x):
    return jnp.maximum(x, 0.0) + jnp.log(1.0 + jnp.exp(-jnp.abs(x)))


def _expand_matrix(n_heads, p, rev):
    e = np.zeros((LANE, n_heads * p), np.float32)
    for h in range(n_heads):
        e[h + (n_heads if rev else 0), h * p:(h + 1) * p] = 1.0
    return jnp.asarray(e, BF16)


def _ssd_chunk_prep(dt_ref, dtb_ref, alog_ref, e_ref, rev):
    dt = _softplus(dt_ref[...] + dtb_ref[...])
    a = -jnp.exp(alog_ref[...])
    li = lax.broadcasted_iota(jnp.int32, (BLK, BLK), 0)
    si = lax.broadcasted_iota(jnp.int32, (BLK, BLK), 1)
    tri = (si >= li) if rev else (si <= li)
    acs = _dot3_rhs(tri.astype(BF16), a * dt)
    dtexp, aexp = _expand2([dt, acs], e_ref[...])
    return dt, a, tri, acs, dtexp, aexp


def _pair_cols(ap):
    lane = lax.broadcasted_iota(jnp.int32, ap.shape, 1)
    apr = pltpu.roll(ap, LANE // 2, 1)
    return jnp.where(lane < LANE // 2, ap, apr), jnp.where(lane < LANE // 2, apr, ap)


def _ssd_scan(xbc, zx, dtb, alog, emat, n_heads, n_ctx, rev, name):
    T, C = xbc.shape
    P = emat.shape[1] // n_heads
    DI = n_heads * P
    GN = (C - DI) // 2
    N = GN // SSD_GROUPS
    n_pairs = DI // LANE
    ppg = n_pairs // SSD_GROUPS
    n_chunks = T // BLK
    hoff = n_heads if rev else 0
    last = 0 if rev else BLK - 1
    dt_blk = zx.shape[1] // LANE - 1
    assert N == LANE and 2 * P == LANE and 2 * n_heads == LANE

    def body(xs_ref, b_ref, c_ref, dt_ref, dtb_ref, alog_ref, e_ref, y_ref, hin_ref, state_ref, xdt_s, xdec_s, aexp_s, at_s):
        s = pl.program_id(0)

        @pl.when(s == 0)
        def _():
            state_ref[...] = jnp.zeros(state_ref.shape, F32)

        dt, a, tri, acs, dtexp, aexp = _ssd_chunk_prep(dt_ref, dtb_ref, alog_ref, e_ref, rev)
        at_s[...] = acs.T
        aexp_s[...] = aexp
        xdt = xs_ref[...] * dtexp
        xdt_s[...] = xdt.astype(MXU)
        xdec_s[...] = (xdt * jnp.exp(aexp[last:last + 1, :] - aexp)).astype(MXU)
        hin_ref[0] = state_ref[...]
        lane = lax.broadcasted_iota(jnp.int32, (BLK, LANE), 1)

        def pair(k, carry):
            col = pl.multiple_of(k * LANE, LANE)
            gcol = pl.multiple_of((k // ppg) * N, N)
            bg = b_ref[:, pl.ds(gcol, N)].astype(MXU)
            cg = c_ref[:, pl.ds(gcol, N)].astype(MXU)
            cb = _dot(cg, bg, NT)
            ap = aexp_s[:, pl.ds(col, LANE)]
            ac0, ac1 = _pair_cols(ap)
            ar0 = at_s[pl.ds(2 * k + hoff, 1), :]
            ar1 = at_s[pl.ds(2 * k + 1 + hoff, 1), :]
            m0 = (cb * jnp.exp(jnp.where(tri, ac0 - ar0, -jnp.inf))).astype(MXU)
            m1 = (cb * jnp.exp(jnp.where(tri, ac1 - ar1, -jnp.inf))).astype(MXU)
            xp = xdt_s[:, pl.ds(col, LANE)]
            zero = jnp.zeros_like(xp)
            xbd = jnp.concatenate([jnp.where(lane < LANE // 2, xp, zero), jnp.where(lane >= LANE // 2, xp, zero)], axis=0)
            yd = _dot(jnp.concatenate([m0, m1], axis=1), xbd)
            ht = state_ref[k]
            yo = _dot(cg, ht.astype(MXU)) * jnp.exp(ap)
            y_ref[:, pl.ds(col, LANE)] = yd + yo
            st = _dot(bg, xdec_s[:, pl.ds(col, LANE)], TN)
            state_ref[k] = jnp.exp(aexp_s[pl.ds(last, 1), pl.ds(col, LANE)]) * ht + st
            return carry

        lax.fori_loop(0, n_pairs, pair, 0, unroll=8)

    order = lambda s: _chunk_order(s, n_chunks, n_ctx, rev)
    return pl.pallas_call(
        body, name=name, grid=(n_chunks,),
        in_specs=[pl.BlockSpec((BLK, DI), lambda s: (order(s), 0)),
                  pl.BlockSpec((BLK, GN), lambda s: (order(s), DI // GN)),
                  pl.BlockSpec((BLK, GN), lambda s: (order(s), DI // GN + 1)),
                  pl.BlockSpec((BLK, LANE), lambda s: (order(s), dt_blk)),
                  pl.BlockSpec((1, LANE), lambda s: (0, 0)), pl.BlockSpec((1, LANE), lambda s: (0, 0)),
                  pl.BlockSpec((2 * LANE, DI), lambda s: (0, 0))],
        out_specs=[pl.BlockSpec((BLK, DI), lambda s: (order(s), 0)),
                   pl.BlockSpec((1, n_pairs, N, LANE), lambda s: (order(s), 0, 0, 0))],
        out_shape=[jax.ShapeDtypeStruct((T, DI), F32), jax.ShapeDtypeStruct((n_chunks, n_pairs, N, LANE), F32)],
        scratch_shapes=[pltpu.VMEM((n_pairs, N, LANE), F32), pltpu.VMEM((BLK, DI), MXU), pltpu.VMEM((BLK, DI), MXU),
                        pltpu.VMEM((BLK, DI), F32), pltpu.VMEM((LANE, BLK), F32)],
        compiler_params=_cparams(("arbitrary",), 20 * _nbytes((BLK, DI), F32)),
    )(xbc, xbc, xbc, zx, dtb, alog, jnp.concatenate([emat, emat], axis=0))


def _ssd_scan_bwd(xbc, zx, dtb, alog, emat, emat_t, dexp, hin, dy, acc, n_heads, n_ctx, rev, name):
    T, C = xbc.shape
    P = emat.shape[1] // n_heads
    DI = n_heads * P
    GN = (C - DI) // 2
    N = GN // SSD_GROUPS
    n_pairs = DI // LANE
    ppg = n_pairs // SSD_GROUPS
    n_chunks = T // BLK
    hoff = n_heads if rev else 0
    last = 0 if rev else BLK - 1
    dt_blk = zx.shape[1] // LANE - 1
    has_acc = acc is not None

    def body(*refs):
        (xs_ref, b_ref, c_ref, dt_ref, dtb_ref, alog_ref, e_ref, et_ref, dexp_ref, hin_ref, dy_ref) = refs[:11]
        n_in = 11
        if has_acc:
            dxbc_in, ddt_in, sums_in = refs[11:14]
            n_in = 14
        dxbc_ref, ddt_ref, sums_ref = refs[n_in:n_in + 3]
        dstate_ref, xdt_s, xdec_s, aexp_s, at_s, dyd_s, z1_s, z3_s, dxdt_s, cdrow_s, rmat_s, cst_s = refs[n_in + 3:]
        s = pl.program_id(0)

        @pl.when(s == 0)
        def _():
            dstate_ref[...] = jnp.zeros(dstate_ref.shape, F32)

        dt, a, tri, acs, dtexp, aexp = _ssd_chunk_prep(dt_ref, dtb_ref, alog_ref, e_ref, rev)
        at_s[...] = acs.T
        aexp_s[...] = aexp
        xsv = xs_ref[...]
        xdt = xsv * dtexp
        xdt_s[...] = xdt.astype(MXU)
        decend = jnp.exp(aexp[last:last + 1, :] - aexp)
        xdec_s[...] = (xdt * decend).astype(MXU)
        dyv = dy_ref[...]
        dyd_s[...] = (dyv * jnp.exp(aexp)).astype(MXU)
        dxbc_ref[:, DI:] = jnp.zeros((BLK, 2 * GN), F32)
        rmat_s[...] = jnp.zeros(rmat_s.shape, F32)
        lane = lax.broadcasted_iota(jnp.int32, (BLK, LANE), 1)
        lo = lane < LANE // 2
        tri_t = (lax.broadcasted_iota(jnp.int32, (BLK, BLK), 0) <= lax.broadcasted_iota(jnp.int32, (BLK, BLK), 1)) if not rev \
            else (lax.broadcasted_iota(jnp.int32, (BLK, BLK), 0) >= lax.broadcasted_iota(jnp.int32, (BLK, BLK), 1))

        def pair(k, carry):
            col = pl.multiple_of(k * LANE, LANE)
            gcol = pl.multiple_of((k // ppg) * N, N)
            bg = b_ref[:, pl.ds(gcol, N)].astype(MXU)
            cg = c_ref[:, pl.ds(gcol, N)].astype(MXU)
            cb = _dot(cg, bg, NT)
            cbt = _dot(bg, cg, NT)
            ap = aexp_s[:, pl.ds(col, LANE)]
            ac0, ac1 = _pair_cols(ap)
            ar0 = at_s[pl.ds(2 * k + hoff, 1), :]
            ar1 = at_s[pl.ds(2 * k + 1 + hoff, 1), :]
            seg0 = jnp.exp(jnp.where(tri, ac0 - ar0, -jnp.inf))
            seg1 = jnp.exp(jnp.where(tri, ac1 - ar1, -jnp.inf))
            segt0 = jnp.exp(jnp.where(tri_t, ar0 - ac0, -jnp.inf))
            segt1 = jnp.exp(jnp.where(tri_t, ar1 - ac1, -jnp.inf))
            dyp = dy_ref[:, pl.ds(col, LANE)].astype(MXU)
            zero = jnp.zeros_like(dyp)
            dy0, dy1 = jnp.where(lo, dyp, zero), jnp.where(lo, zero, dyp)
            dht = dstate_ref[k]
            dhb = dht.astype(MXU)
            ht = hin_ref[0, k]
            mt = jnp.concatenate([(cbt * segt0).astype(MXU), (cbt * segt1).astype(MXU)], axis=1)
            bdh = _dot(bg, dhb)
            dec_p = jnp.exp(aexp_s[pl.ds(last, 1), pl.ds(col, LANE)] - ap)
            dxdt_s[:, pl.ds(col, LANE)] = _dot(mt, jnp.concatenate([dy0, dy1], axis=0)) + dec_p * bdh
            z3_s[:, pl.ds(col, LANE)] = bdh
            cdec = jnp.exp(aexp_s[pl.ds(last, 1), pl.ds(col, LANE)])
            dydp = dyd_s[:, pl.ds(col, LANE)]
            dstate_ref[k] = _dot(cg, dydp, TN) + cdec * dht
            cdrow_s[0:1, pl.ds(col, LANE)] = cdec * jnp.sum(dht * ht, axis=0, keepdims=True)
            z1_s[:, pl.ds(col, LANE)] = _dot(cg, ht.astype(MXU))
            dcg = _dot(dydp, ht.astype(MXU), NT)
            dbg = _dot(xdec_s[:, pl.ds(col, LANE)], dhb, NT)
            xp = xdt_s[:, pl.ds(col, LANE)]
            dg0 = _dot(dy0, xp, NT)
            dg1 = _dot(dy1, xp, NT)
            ds0, ds1 = dg0 * seg0, dg1 * seg1
            w0, w1 = ds0 * cb, ds1 * cb
            dcb = ds0 + ds1
            lane_h = lax.broadcasted_iota(jnp.int32, (BLK, LANE), 1)
            rmat_s[...] += (jnp.where(lane_h == 2 * k + hoff, jnp.sum(w0, axis=1, keepdims=True), 0.0)
                            + jnp.where(lane_h == 2 * k + 1 + hoff, jnp.sum(w1, axis=1, keepdims=True), 0.0))
            cst_s[pl.ds(2 * k + hoff, 1), :] = jnp.sum(w0, axis=0, keepdims=True)
            cst_s[pl.ds(2 * k + 1 + hoff, 1), :] = jnp.sum(w1, axis=0, keepdims=True)
            dcbb = dcb.astype(MXU)
            dxbc_ref[:, pl.ds(DI + GN + gcol, N)] += dcg + _dot(dcbb, bg)
            dxbc_ref[:, pl.ds(DI + gcol, N)] += dbg + _dot(dcbb, cg, TN)
            return carry

        cst_s[...] = jnp.zeros(cst_s.shape, F32)
        lax.fori_loop(0, n_pairs, pair, 0, unroll=4)

        etv = et_ref[...]
        dxdt = dxdt_s[...]
        qfull = xdt * decend * z3_s[...]
        tot8 = jnp.concatenate([jnp.sum(qfull, axis=0, keepdims=True) + cdrow_s[0:1, :], jnp.zeros((7, DI), F32)], axis=0)
        z1q, z2, tot = _dot2_stacked([dyv * (z1_s[...] * jnp.exp(aexp)) - qfull, dxdt * xsv, tot8], etv)
        dacs = rmat_s[...] - cst_s[...].T + z1q
        rowi = lax.broadcasted_iota(jnp.int32, (BLK, LANE), 0)
        dacs = dacs + jnp.where(rowi == last, tot[0:1, :], 0.0)
        d_a = _dot3_rhs(tri_t.astype(BF16), dacs)
        ddt = a * d_a + z2
        x_raw = dt_ref[...] + dtb_ref[...]
        ddt_raw = ddt * jax.nn.sigmoid(x_raw)
        lane_l = lax.broadcasted_iota(jnp.int32, (BLK, LANE), 1)
        mine = (lane_l >= hoff) & (lane_l < hoff + n_heads)
        ddt_raw = jnp.where(mine, ddt_raw, 0.0)
        part = jnp.concatenate([jnp.sum(jnp.where(mine, dt * d_a, 0.0), axis=0, keepdims=True) * a,
                                jnp.sum(ddt_raw, axis=0, keepdims=True), jnp.zeros((6, LANE), F32)], axis=0)
        dxs = dxdt * dtexp
        if has_acc:
            dxbc_ref[:, 0:DI] = dxs + dxbc_in[:, 0:DI]
            dxbc_ref[:, DI:] += dxbc_in[:, DI:]
            ddt_ref[...] = ddt_raw + ddt_in[...]
            part = part + jnp.where(s == 0, sums_in[...], 0.0)
        else:
            dxbc_ref[:, 0:DI] = dxs + dyv * dexp_ref[...]
            ddt_ref[...] = ddt_raw
        _acc_first(sums_ref, part, s)

    order = lambda s: _chunk_order(n_chunks - 1 - s, n_chunks, n_ctx, rev)
    in_specs = [pl.BlockSpec((BLK, DI), lambda s: (order(s), 0)),
                pl.BlockSpec((BLK, GN), lambda s: (order(s), DI // GN)),
                pl.BlockSpec((BLK, GN), lambda s: (order(s), DI // GN + 1)),
                pl.BlockSpec((BLK, LANE), lambda s: (order(s), dt_blk)),
                pl.BlockSpec((1, LANE), lambda s: (0, 0)), pl.BlockSpec((1, LANE), lambda s: (0, 0)),
                pl.BlockSpec((2 * LANE, DI), lambda s: (0, 0)), pl.BlockSpec((DI, LANE), lambda s: (0, 0)),
                pl.BlockSpec((1, DI), lambda s: (0, 0)),
                pl.BlockSpec((1, n_pairs, N, LANE), lambda s: (order(s), 0, 0, 0)),
                pl.BlockSpec((BLK, DI), lambda s: (order(s), 0))]
    args = [xbc, xbc, xbc, zx, dtb, alog, jnp.concatenate([emat, emat], axis=0), emat_t, dexp, hin, dy]
    if has_acc:
        in_specs += [pl.BlockSpec((BLK, C), lambda s: (order(s), 0)), pl.BlockSpec((BLK, LANE), lambda s: (order(s), 0)),
                     pl.BlockSpec((8, LANE), lambda s: (0, 0))]
        args += list(acc)
    return pl.pallas_call(
        body, name=name, grid=(n_chunks,),
        in_specs=in_specs,
        out_specs=[pl.BlockSpec((BLK, C), lambda s: (order(s), 0)), pl.BlockSpec((BLK, LANE), lambda s: (order(s), 0)),
                   pl.BlockSpec((8, LANE), lambda s: (0, 0))],
        out_shape=[jax.ShapeDtypeStruct((T, C), F32), jax.ShapeDtypeStruct((T, LANE), F32), jax.ShapeDtypeStruct((8, LANE), F32)],
        scratch_shapes=[pltpu.VMEM((n_pairs, N, LANE), F32), pltpu.VMEM((BLK, DI), MXU), pltpu.VMEM((BLK, DI), MXU),
                        pltpu.VMEM((BLK, DI), F32), pltpu.VMEM((LANE, BLK), F32), pltpu.VMEM((BLK, DI), MXU),
                        pltpu.VMEM((BLK, DI), F32), pltpu.VMEM((BLK, DI), F32), pltpu.VMEM((BLK, DI), F32),
                        pltpu.VMEM((8, DI), F32), pltpu.VMEM((BLK, LANE), F32), pltpu.VMEM((LANE, BLK), F32)],
        compiler_params=_cparams(("arbitrary",), 36 * _nbytes((BLK, DI), F32)),
    )(*args)


def _ssd_finish(yf, yb, xbc, zx, dexp, nw, name, tr=256):
    T, DI = yf.shape
    tr = min(tr, T)

    def body(yf_ref, yb_ref, xs_ref, z_ref, d_ref, nw_ref, o_ref):
        y = yf_ref[...] + yb_ref[...] + xs_ref[...] * d_ref[...]
        gt = y * _silu(z_ref[...])
        r = lax.rsqrt(jnp.mean(gt * gt, axis=-1, keepdims=True) + NORM_EPS)
        o_ref[...] = ((gt * r) * nw_ref[...]).astype(o_ref.dtype)

    row = pl.BlockSpec((tr, DI), lambda i: (i, 0))
    vec = pl.BlockSpec((1, DI), lambda i: (0, 0))
    return pl.pallas_call(
        body, name=name, grid=(T // tr,), in_specs=[row, row, row, row, vec, vec], out_specs=row,
        out_shape=jax.ShapeDtypeStruct((T, DI), MXU),
        compiler_params=_cparams(("parallel",), 16 * _nbytes((tr, DI), F32)),
    )(yf, yb, xbc, zx, dexp, nw)


def _ssd_finish_bwd(yf, yb, xbc, zx, dexp, nw, do, emat_t, dzx_shape, name, tr=64):
    T, DI = yf.shape
    tr = min(tr, T)
    n_steps = T // tr

    def body(yf_ref, yb_ref, xs_ref, z_ref, d_ref, nw_ref, do_ref, et_ref, dy_ref, dz_ref, sums_ref, dd_ref):
        i = pl.program_id(0)
        xs = xs_ref[...]
        zv = z_ref[...]
        y = yf_ref[...] + yb_ref[...] + xs * d_ref[...]
        sz = _silu(zv)
        gt = y * sz
        r = lax.rsqrt(jnp.mean(gt * gt, axis=-1, keepdims=True) + NORM_EPS)
        gn = gt * r
        dov = do_ref[...].astype(F32)
        dgn = dov * nw_ref[...]
        dgt = r * (dgn - gn * jnp.mean(dgn * gn, axis=-1, keepdims=True))
        dy = dgt * sz
        dy_ref[...] = dy
        dz_ref[...] = (dgt * y * _dsilu(zv)).astype(dz_ref.dtype)
        part = jnp.concatenate([jnp.sum(dov * gn, axis=0, keepdims=True), jnp.sum(dy * xs, axis=0, keepdims=True),
                                jnp.zeros((6, DI), F32)], axis=0)
        _acc_first(sums_ref, part, i)

        @pl.when(i == n_steps - 1)
        def _():
            dd_ref[...] = _dot3_lhs(sums_ref[...], et_ref[...])

    row = pl.BlockSpec((tr, DI), lambda i: (i, 0))
    vec = pl.BlockSpec((1, DI), lambda i: (0, 0))
    return pl.pallas_call(
        body, name=name, grid=(n_steps,),
        in_specs=[row, row, row, row, vec, vec, row, pl.BlockSpec((DI, LANE), lambda i: (0, 0))],
        out_specs=[row, row, pl.BlockSpec((8, DI), lambda i: (0, 0)), pl.BlockSpec((8, LANE), lambda i: (0, 0))],
        out_shape=[jax.ShapeDtypeStruct((T, DI), F32), jax.ShapeDtypeStruct(dzx_shape, MXU), jax.ShapeDtypeStruct((8, DI), F32),
                   jax.ShapeDtypeStruct((8, LANE), F32)],
        compiler_params=_cparams(("arbitrary",), 40 * _nbytes((tr, DI), F32)),
    )(yf, yb, xbc, zx, dexp, nw, do, emat_t)


def _loss_head(xf, tgt, ctx_rows, name, tr=256):
    T, D = xf.shape
    tr = min(tr, ctx_rows)
    n_ctx = ctx_rows // tr

    def body(x_ref, t_ref, dx_ref, l_ref):
        i = pl.program_id(0)

        @pl.when(i < n_ctx)
        def _():
            dx_ref[...] = jnp.zeros(dx_ref.shape, F32)

        @pl.when(i == 0)
        def _():
            l_ref[...] = jnp.zeros(l_ref.shape, F32)

        @pl.when(i >= n_ctx)
        def _():
            e = x_ref[...] - t_ref[...]
            dx_ref[...] = e * (1.0 / D)
            l_ref[...] += 0.5 * jnp.sum(jnp.mean(e * e, axis=-1, keepdims=True))

    return pl.pallas_call(
        body, name=name, grid=(T // tr,),
        in_specs=[pl.BlockSpec((tr, D), lambda i: (i, 0)), pl.BlockSpec((tr, D), lambda i: (jnp.maximum(i - n_ctx, 0), 0))],
        out_specs=[pl.BlockSpec((tr, D), lambda i: (i, 0)), pl.BlockSpec((8, LANE), lambda i: (0, 0))],
        out_shape=[jax.ShapeDtypeStruct((T, D), F32), jax.ShapeDtypeStruct((8, LANE), F32)],
        compiler_params=_cparams(("arbitrary",), 10 * _nbytes((tr, D), F32)),
    )(xf, tgt)


def _adamw_math(w, g, m, v):
    m2 = ADAM_B1 * m + (1.0 - ADAM_B1) * g
    v2 = ADAM_B2 * v + (1.0 - ADAM_B2) * (g * g)
    m_hat = m2 / (1.0 - ADAM_B1 ** ADAM_STEP)
    v_hat = v2 / (1.0 - ADAM_B2 ** ADAM_STEP)
    delta = -ADAM_LR * (m_hat / (jnp.sqrt(v_hat) + ADAM_EPS) + ADAM_WD * w)
    return delta, m2, v2


def _row_tile(rows, target):
    if rows <= target:
        return rows
    t = target - target % 8
    while rows % t:
        t -= 8
    return t


def _adamw(parts, w, m, v, name, tr=128):
    n, L, rows, cols = parts.shape
    tr = _row_tile(rows, tr)

    def body(p_ref, w_ref, m_ref, v_ref, g_ref, d_ref, m2_ref, v2_ref):
        g = p_ref[0, 0].astype(F32)
        for q in range(1, n):
            g = g + p_ref[q, 0].astype(F32)
        d, m2, v2 = _adamw_math(w_ref[0], g, m_ref[0], v_ref[0])
        g_ref[0], d_ref[0], m2_ref[0], v2_ref[0] = g, d, m2, v2

    blk = pl.BlockSpec((1, tr, cols), lambda l, i: (l, i, 0))
    shp = jax.ShapeDtypeStruct((L, rows, cols), F32)
    return pl.pallas_call(
        body, name=name, grid=(L, rows // tr),
        in_specs=[pl.BlockSpec((n, 1, tr, cols), lambda l, i: (0, l, i, 0)), blk, blk, blk],
        out_specs=[blk, blk, blk, blk], out_shape=[shp, shp, shp, shp],
        compiler_params=_cparams(("parallel", "parallel"), 2 * (n + 8) * _nbytes((tr, cols), F32)),
    )(parts, w, m, v)


def _adamw_small(bufs, w, m, v, name):
    n, R, _ = bufs.shape

    def body(b_ref, w_ref, m_ref, v_ref, g_ref, d_ref, m2_ref, v2_ref):
        g = b_ref[0]
        for q in range(1, n):
            g = g + b_ref[q]
        d, m2, v2 = _adamw_math(w_ref[...], g, m_ref[...], v_ref[...])
        g_ref[...], d_ref[...], m2_ref[...], v2_ref[...] = g, d, m2, v2

    vm = pl.BlockSpec(memory_space=pltpu.VMEM)
    shp = jax.ShapeDtypeStruct((R, LANE), F32)
    return pl.pallas_call(body, name=name, in_specs=[vm, vm, vm, vm], out_specs=[vm, vm, vm, vm],
                          out_shape=[shp, shp, shp, shp],
                          compiler_params=pltpu.CompilerParams(vmem_limit_bytes=32 * 1024 * 1024))(bufs, w, m, v)


def _me():
    return lax.axis_index("x"), lax.axis_index("y"), lax.axis_index("c")


def _flip(v, bit):
    return 1 - v if bit else v


def _peer(k):
    x, y, c = _me()
    return _flip(x, (k >> 2) & 1), _flip(y, (k >> 1) & 1), _flip(c, k & 1)


def _dev_index(p):
    return 4 * p[0] + 2 * p[1] + p[2]


def _small_allgather(v, name):
    R, C = v.shape

    def body(v_ref, out_ref, send_sems, recv_sems, loc_sem):
        me = _dev_index(_me())
        mine = pltpu.make_async_copy(v_ref, out_ref.at[me], loc_sem)
        mine.start()
        sends = []
        for k in range(1, NDEV):
            cp = pltpu.make_async_remote_copy(src_ref=v_ref, dst_ref=out_ref.at[me], send_sem=send_sems.at[k - 1],
                                              recv_sem=recv_sems.at[k - 1], device_id=_peer(k), device_id_type=MESH_ID)
            cp.start()
            sends.append(cp)
        for k in range(1, NDEV):
            pltpu.make_async_remote_copy(src_ref=v_ref, dst_ref=out_ref.at[_dev_index(_peer(k))], send_sem=send_sems.at[k - 1],
                                         recv_sem=recv_sems.at[k - 1], device_id=_peer(k), device_id_type=MESH_ID).wait_recv()
        for cp in sends:
            cp.wait_send()
        mine.wait()

    vm = pl.BlockSpec(memory_space=pltpu.VMEM)
    return pl.pallas_call(
        body, name=name, in_specs=[vm], out_specs=vm, out_shape=jax.ShapeDtypeStruct((NDEV, R, C), F32),
        scratch_shapes=[pltpu.SemaphoreType.DMA((NDEV - 1,)), pltpu.SemaphoreType.DMA((NDEV - 1,)), pltpu.SemaphoreType.DMA(())],
        compiler_params=pltpu.CompilerParams(vmem_limit_bytes=48 * 1024 * 1024),
    )(v)


HBM_SPEC = pl.BlockSpec(memory_space=pltpu.HBM)
SEM_SPEC = pl.BlockSpec(memory_space=pltpu.SEMAPHORE)
DATAFLOW = pltpu.SideEffectType.DATAFLOW_SIDE_EFFECTING


def _xfer_copy(src_ref, land_ref, sems, a, k, layer, scatter, arriving):
    send_sems, recv_sems = sems
    me, peer = _dev_index(_me()), _dev_index(_peer(k))
    src = src_ref.at[peer] if scatter else src_ref
    slot = peer if arriving else me
    dst = land_ref.at[slot] if layer is None else land_ref.at[slot, layer]
    return pltpu.make_async_remote_copy(src_ref=src, dst_ref=dst, send_sem=send_sems.at[a * (NDEV - 1) + k - 1],
                                        recv_sem=recv_sems.at[a * (NDEV - 1) + k - 1], device_id=_peer(k),
                                        device_id_type=MESH_ID)


def _xfer_start(srcs, lands, layers, scatter, name):
    n = len(srcs)

    def body(*refs):
        src_refs, land_refs = refs[:n], refs[n:2 * n]
        sems = refs[2 * n], refs[2 * n + 1]
        token = refs[-1]
        for a in range(n):
            for k in range(1, NDEV):
                _xfer_copy(src_refs[a], land_refs[a], sems, a, k, layers[a], scatter, False).start()
        token[...] = jnp.zeros(token.shape, token.dtype)

    ops = [pltpu.with_memory_space_constraint(t, pltpu.HBM) for t in list(srcs) + list(lands)]
    n_sem = n * (NDEV - 1)
    res = pl.pallas_call(
        body, name=name,
        out_shape=(pltpu.SemaphoreType.DMA((n_sem,)), pltpu.SemaphoreType.DMA((n_sem,)),
                   *[pltpu.HBM(t.shape, t.dtype) for t in ops], jax.ShapeDtypeStruct((8, LANE), F32)),
        in_specs=[HBM_SPEC] * (2 * n),
        out_specs=(SEM_SPEC, SEM_SPEC, *[HBM_SPEC] * (2 * n), pl.BlockSpec(memory_space=pltpu.VMEM)),
        input_output_aliases={i: 2 + i for i in range(2 * n)},
        compiler_params=pltpu.CompilerParams(has_side_effects=DATAFLOW),
    )(*ops)
    return res[0], res[1], list(res[2:2 + n]), list(res[2 + n:2 + 2 * n]), res[-1]


def _xfer_wait(send_sems, recv_sems, srcs, lands, layers, scatter, after, name):
    n = len(srcs)

    def body(*refs):
        src_refs, land_refs = refs[:n], refs[n:2 * n]
        sems = refs[2 * n], refs[2 * n + 1]
        for a in range(n):
            for k in range(1, NDEV):
                _xfer_copy(src_refs[a], land_refs[a], sems, a, k, layers[a], scatter, False).wait_send()
                _xfer_copy(src_refs[a], land_refs[a], sems, a, k, layers[a], scatter, True).wait_recv()

    ops = list(srcs) + list(lands)
    res = pl.pallas_call(
        body, name=name,
        out_shape=tuple(pltpu.HBM(t.shape, t.dtype) for t in ops),
        in_specs=[HBM_SPEC] * (2 * n) + [SEM_SPEC, SEM_SPEC, pl.BlockSpec(memory_space=pl.ANY)],
        out_specs=tuple([HBM_SPEC] * (2 * n)),
        input_output_aliases={i: i for i in range(2 * n)},
        compiler_params=pltpu.CompilerParams(has_side_effects=DATAFLOW),
    )(*ops, send_sems, recv_sems, after)
    return list(res[n:])


HI = lax.Precision.HIGHEST
MOD_ROWS = 16


def _col_tile(n, target=512):
    return target if n % target == 0 else n


def _modulation(s_in, ada_w, b_loc, name):
    L, D, nl = ada_w.shape
    tn = _col_tile(nl)

    def body(s_ref, w_ref, b_ref, o_ref):
        o_ref[0] = jnp.dot(_silu(s_ref[...]), w_ref[0], preferred_element_type=F32, precision=HI) + b_ref[0]

    return pl.pallas_call(
        body, name=name, grid=(L, nl // tn),
        in_specs=[pl.BlockSpec((MOD_ROWS, D), lambda l, j: (0, 0)), pl.BlockSpec((1, D, tn), lambda l, j: (l, 0, j)),
                  pl.BlockSpec((1, 1, tn), lambda l, j: (l, 0, j))],
        out_specs=pl.BlockSpec((1, MOD_ROWS, tn), lambda l, j: (l, 0, j)),
        out_shape=jax.ShapeDtypeStruct((L, MOD_ROWS, nl), F32),
        compiler_params=_cparams(("parallel", "parallel"), 4 * _nbytes((D, tn), F32)),
    )(s_in, ada_w, b_loc)


def _modulation_bwd(s_in, dml, dmc, ada_w, name):
    L, D, nl = ada_w.shape
    tn = _col_tile(nl)

    def body(s_ref, dml_ref, dmc_ref, w_ref, g_ref, pc_ref):
        l, j = pl.program_id(0), pl.program_id(1)
        a = _silu(s_ref[...])
        tot = dmc_ref[0, 0]
        for d in range(1, NDEV):
            tot = tot + dmc_ref[d, 0]
        row = lax.broadcasted_iota(jnp.int32, (MOD_ROWS, tn), 0)
        dm = jnp.where(row == NDEV, tot, dml_ref[:, 0, 0, :])
        g_ref[0] = lax.dot_general(a, dm, TN, preferred_element_type=F32, precision=HI)
        tot8 = jnp.where(lax.broadcasted_iota(jnp.int32, (8, tn), 0) == 0, tot, 0.0)
        part = lax.dot_general(tot8, w_ref[0], NT, preferred_element_type=F32, precision=HI)

        @pl.when((l == 0) & (j == 0))
        def _():
            pc_ref[...] = part

        @pl.when((l != 0) | (j != 0))
        def _():
            pc_ref[...] += part

    return pl.pallas_call(
        body, name=name, grid=(L, nl // tn),
        in_specs=[pl.BlockSpec((MOD_ROWS, D), lambda l, j: (0, 0)), pl.BlockSpec((MOD_ROWS, 1, 1, tn), lambda l, j: (0, l, 0, j)),
                  pl.BlockSpec((NDEV, 1, 1, tn), lambda l, j: (0, l, 0, j)), pl.BlockSpec((1, D, tn), lambda l, j: (l, 0, j))],
        out_specs=[pl.BlockSpec((1, D, tn), lambda l, j: (l, 0, j)), pl.BlockSpec((8, D), lambda l, j: (0, 0))],
        out_shape=[jax.ShapeDtypeStruct((L, D, nl), F32), jax.ShapeDtypeStruct((8, D), F32)],
        compiler_params=_cparams(("arbitrary", "arbitrary"), 8 * _nbytes((D, tn), F32)),
    )(s_in, dml.reshape(MOD_ROWS, L, 1, nl), dmc.reshape(NDEV, L, 1, nl), ada_w)


def _cctx_update(bufs, c_ctx, m, v, name):
    n, R, _ = bufs.shape

    def body(b_ref, w_ref, m_ref, v_ref, g_ref, d_ref, m2_ref, v2_ref):
        g = b_ref[0]
        for q in range(1, n):
            g = g + b_ref[q]
        g = g * _dsilu(w_ref[...])
        d, m2, v2 = _adamw_math(w_ref[...], g, m_ref[...], v_ref[...])
        g_ref[...], d_ref[...], m2_ref[...], v2_ref[...] = g, d, m2, v2

    vm = pl.BlockSpec(memory_space=pltpu.VMEM)
    shp = jax.ShapeDtypeStruct((R, LANE), F32)
    return pl.pallas_call(body, name=name, in_specs=[vm, vm, vm, vm], out_specs=[vm, vm, vm, vm],
                          out_shape=[shp, shp, shp, shp])(bufs, c_ctx, m, v)


def _pack(arrs):
    flat = jnp.concatenate([a.reshape(-1).astype(F32) for a in arrs])
    n = flat.shape[0]
    total = -(-n // (8 * LANE)) * (8 * LANE)
    return jnp.pad(flat, (0, total - n)).reshape(total // LANE, LANE)


def _unpack(buf, shapes):
    lead = buf.shape[:-2]
    flat = buf.reshape(lead + (-1,))
    out, off = [], 0
    for s in shapes:
        n = int(np.prod(s))
        out.append(flat[..., off:off + n].reshape(lead + tuple(s)))
        off += n
    return out


WEIGHTS = ['c_ctx', 'ada_w', 'ada_b', 'norm1_w', 'norm2_w', 'ssd_w_in', 'ssd_conv_w', 'ssd_conv_b', 'ssd_dt_bias_f',
           'ssd_dt_bias_b', 'ssd_a_log_f', 'ssd_a_log_b', 'ssd_d', 'ssd_norm_w', 'ssd_w_out', 'attn_w_qkv', 'attn_q_gain',
           'attn_k_gain', 'attn_sinks', 'attn_w_o', 'ffn_w_up', 'ffn_conv_w', 'ffn_conv_b', 'ffn_w_down']
SMALL = ['ada_b', 'norm1_w', 'norm2_w', 'ssd_conv_b', 'ssd_dt_bias_f', 'ssd_dt_bias_b', 'ssd_a_log_f', 'ssd_a_log_b', 'ssd_d',
         'ssd_norm_w', 'attn_q_gain', 'attn_k_gain', 'attn_sinks', 'ffn_conv_b']
BIG = ['ssd_w_in', 'ssd_w_out', 'attn_w_qkv', 'attn_w_o', 'ffn_w_up', 'ffn_w_down']


def _step(x, c, ctx, w, tgt, m, v):
    xi, yi, ci = _me()
    me = 4 * xi + 2 * yi + ci
    t_lat, D = x.shape[1], x.shape[2]
    ctx_rows = ctx.shape[1]
    T = ctx_rows + t_lat
    L, n_ssd, n_att = w['norm1_w'].shape[0], w['ssd_d'].shape[0], w['attn_sinks'].shape[0]
    H, DI, XBC = w['ssd_d'].shape[1], w['ssd_norm_w'].shape[1], w['ssd_conv_b'].shape[1]
    P = DI // H
    IN = w['ssd_w_in'].shape[2] * NDEV
    hd, n_q = w['attn_q_gain'].shape[1], w['attn_sinks'].shape[1]
    F2 = w['ffn_conv_b'].shape[1]
    G = F2 // NDEV
    nl = w['ada_w'].shape[2]
    ncc = ctx_rows // BLK
    perm = [_ffn_perm(s) for s in range(NDEV)]
    inv = [perm.index(d) for d in range(NDEV)]

    def reorder(t, order):
        return jnp.concatenate([t[..., o * G:(o + 1) * G] for o in order], axis=-1)

    def interleave(t):
        return reorder(t, perm)

    def deinterleave(t):
        return reorder(t, inv)

    def layer_weights(i):
        mixer = ['ssd_w_in', 'ssd_w_out'] if i % 2 == 0 else ['attn_w_qkv', 'attn_w_o']
        return [(n, i // 2) for n in mixer] + [('ffn_w_up', i), ('ffn_w_down', i)]

    def gather_start(keys, tag, zero):
        srcs = [(w[n][j] + zero).astype(MXU) for n, j in keys]
        lands = [lax.dynamic_update_index_in_dim(lax.empty((NDEV,) + t.shape, t.dtype), t, me, 0) for t in srcs]
        ss, rs, srcs, lands, tok = _xfer_start(srcs, lands, [None] * len(srcs), False, f"gather_start_{tag}")
        return (ss, rs, srcs, lands, tag), tok[0, 0]

    def gather_wait(p, after):
        return _xfer_wait(*p[:4], [None] * len(p[2]), False, after, f"gather_wait_{p[4]}")

    first, tok = gather_start(layer_weights(0)[:1], "0a", jnp.zeros((), F32))
    pending, tok = gather_start(layer_weights(0)[1:], "0b", tok)

    shapes_a = [(D,), w['ssd_conv_w'].shape, w['ffn_conv_w'].shape]
    g_a = _small_allgather(_pack([c[0], w['ssd_conv_w'], w['ffn_conv_w']]), "gather_cond")
    c_all, scw_all, fcw_all = _unpack(g_a, shapes_a)
    ssd_cw = scw_all.transpose(1, 2, 0, 3).reshape(n_ssd, 3, XBC)
    ffn_cw = jnp.concatenate([fcw_all[d] for d in perm], axis=-1)
    ffn_cb = interleave(w['ffn_conv_b'])[:, None, :]

    s_in = jnp.concatenate([c_all, w['c_ctx'][None], jnp.zeros((MOD_ROWS - NDEV - 1, D), F32)], axis=0)
    b_loc = lax.dynamic_slice(w['ada_b'], (0, me * nl), (L, nl))[:, None, :]
    mod_loc = _modulation(s_in, w['ada_w'], b_loc, "modulation")
    g_b = _small_allgather(_pack([mod_loc]), "gather_mod")
    (mod_all,) = _unpack(g_b, [mod_loc.shape])
    mod_lat = lax.dynamic_index_in_dim(mod_all, me, axis=2, keepdims=False)
    mod_ctx = mod_all[:, :, NDEV, :]
    to_mod = lambda t: t.transpose(1, 0, 2).reshape(L, 6, D)
    mod = jnp.stack([to_mod(mod_ctx), to_mod(mod_lat)], axis=1)

    w_in, w_out, w_qkv, w_o = [None] * n_ssd, [None] * n_ssd, [None] * n_att, [None] * n_att
    w_up, w_down = [None] * L, [None] * L

    cos, sin = _rope_tables(t_lat, ctx_rows, hd)
    e_f, e_b = _expand_matrix(H, P, False), _expand_matrix(H, P, True)
    et_f, et_b = e_f.T, e_b.T

    xs = jnp.concatenate([ctx[0], x[0]], axis=0)
    saved = []
    for i in range(L):
        j = i // 2
        s = dict(x0=xs)
        zero_of = lambda t: jnp.minimum(jnp.abs(t[0, 0, 0].astype(F32)), 0.0)
        if i == 0:
            w_in[0] = _cat_cols(gather_wait(first, mod)[0], "ssd_in_cat")
        else:
            got = gather_wait(pending, xs)
            if i % 2 == 0:
                w_in[j], w_out[j] = _cat_cols(got[0], "ssd_in_cat"), got[1].reshape(DI, D)
            else:
                w_qkv[j], w_o[j] = got[0], got[1].reshape(n_q * hd, D)
            w_up[i], w_down[i] = got[2], got[3].reshape(F2 // 2, D)
            if i + 1 < L:
                pending, tok = gather_start(layer_weights(i + 1), f"{i + 1}", zero_of(got[1]))
        nw1, nw2 = w['norm1_w'][i][None] + tok, w['norm2_w'][i][None]
        s['h1'] = _normmod(xs, nw1, mod[i], 0, ctx_rows, "normmod")
        if i % 2 == 0:
            s['zx'] = _mm(s['h1'], w_in[j], tm=768, tn=1152, tk=2048, name="mm_ssd_in")
            s['cw'], s['cb'] = ssd_cw[j], w['ssd_conv_b'][j][None]
            s['xbc'] = _ssd_conv(s['zx'], s['cw'], s['cb'], DI, ctx_rows, "ssd_conv")
            s['dtb'] = jnp.concatenate([w['ssd_dt_bias_f'][j], w['ssd_dt_bias_b'][j]])[None]
            s['alog'] = jnp.concatenate([w['ssd_a_log_f'][j], w['ssd_a_log_b'][j]])[None]
            s['yf'], s['hin_f'] = _ssd_scan(s['xbc'], s['zx'], s['dtb'], s['alog'], e_f, H, ncc, False, "ssd_scan_f")
            s['yb'], s['hin_b'] = _ssd_scan(s['xbc'], s['zx'], s['dtb'], s['alog'], e_b, H, ncc, True, "ssd_scan_b")
            s['dexp'], s['snw'] = jnp.repeat(w['ssd_d'][j], P)[None], w['ssd_norm_w'][j][None]
            if i == 0:
                got = gather_wait(pending, s['yb'])
                w_out[0], w_up[0], w_down[0] = got[0].reshape(DI, D), got[1], got[2].reshape(F2 // 2, D)
                if L > 1:
                    pending, tok = gather_start(layer_weights(1), "1", zero_of(got[0]))
                    s['snw'] = s['snw'] + tok
            s['o'] = _ssd_finish(s['yf'], s['yb'], s['xbc'], s['zx'], s['dexp'], s['snw'], "ssd_finish")
            s['mix'], x1 = _mm(s['o'], w_out[j], tm=768, tn=512, tk=4096, name="mm_ssd_out",
                               resid=xs, gate=mod[i][:, 2], ctx_rows=ctx_rows)
        else:
            s['qkv'] = _mm(s['h1'], w_qkv[j], tm=768, tn=384, tk=2048, name="mm_qkv", bslots=_ident)
            s['qg'], s['kg'] = w['attn_q_gain'][j][None], w['attn_k_gain'][j][None]
            s['qr'], s['kr'], s['vb'] = _qk_prep(s['qkv'], s['qg'], s['kg'], cos, sin, n_q, ctx_rows, "qk_prep")
            s['o'] = _attn_fwd(s['qr'], s['kr'], s['vb'], w['attn_sinks'][j], n_q, ctx_rows, "attn_fwd")
            s['mix'], x1 = _mm(s['o'], w_o[j], tm=768, tn=1024, tk=2048, name="mm_attn_out",
                               resid=xs, gate=mod[i][:, 2], ctx_rows=ctx_rows)
        s['x1'] = x1
        s['h2'] = _normmod(x1, nw2, mod[i], 1, ctx_rows, "normmod")
        s['u'] = _mm(s['h2'], w_up[i], tm=768, tn=1408, tk=2048, name="mm_ffn_up", bslots=_ffn_perm)
        s['a'] = _ffn_mid(s['u'], ffn_cw[i], ffn_cb[i], ctx_rows, "ffn_mid")
        s['f'], xs = _mm(s['a'], w_down[i], tm=768, tn=512, tk=5632, name="mm_ffn_down",
                         resid=x1, gate=mod[i][:, 5], ctx_rows=ctx_rows)
        saved.append(s)

    dx, lacc = _loss_head(xs, tgt[0], ctx_rows, "loss_head")
    loss = lax.psum(lacc[0, 0], ("x", "y", "c"))

    gbig = {name: [None] * w[name].shape[0] for name in BIG}
    gs = {name: [None] * w[name].shape[0] for name in SMALL + ['ssd_conv_w', 'ffn_conv_w']}
    dmod = [None] * L
    lands = {n: lax.empty((NDEV,) + w[n].shape, XFER) for n in BIG}
    pend_early, pend_late, tok = None, None, jnp.zeros((), F32)

    def scatter_start(keys, tag):
        srcs = [gbig[n][j] for n, j in keys]
        for (n, j), g in zip(keys, srcs):
            own = lax.dynamic_index_in_dim(g, me, 0, keepdims=False)
            lands[n] = lax.dynamic_update_slice(lands[n], own[None, None], (me, j, 0, 0))
        ss, rs, srcs, got, t = _xfer_start(srcs, [lands[n] for n, _ in keys], [j for _, j in keys], True, f"scatter_start_{tag}")
        return (ss, rs, srcs, got, keys, tag), t

    def scatter_wait(p, after):
        ss, rs, srcs, got, keys, tag = p
        got = _xfer_wait(ss, rs, srcs, got, [j for _, j in keys], True, after, f"scatter_wait_{tag}")
        for (n, _), t in zip(keys, got):
            lands[n] = t

    for i in reversed(range(L)):
        j = i // 2
        s = saved[i]
        nw1, nw2 = w['norm1_w'][i][None], w['norm2_w'][i][None]
        dm2, dg2 = _gate_bwd(dx, s['f'], mod[i], 1, ctx_rows, "gate_bwd")
        da = _mm(dm2, w_down[i], tb=True, out_dtype=MXU, tm=768, tn=1408, tk=2048, name="mm_ffn_down_dx")
        gbig['ffn_w_down'][i] = _mm(s['a'], dm2, ta=True, out_dtype=XFER, tm=512, tn=1024, tk=4224,
                                    name="mm_ffn_down_dw").reshape(NDEV, -1, D)
        du, gcw = _ffn_mid_bwd(s['u'], da, ffn_cw[i], ffn_cb[i], ctx_rows, "ffn_mid_bwd")
        gcw = deinterleave(gcw)
        gs['ffn_conv_w'][i], gs['ffn_conv_b'][i] = gcw[0:3], gcw[3]
        dh2 = _mm(du, w_up[i], tb=True, out_dtype=MXU, tm=768, tn=1024, tk=1408, name="mm_ffn_up_dx", bslots=_ffn_perm)
        gbig['ffn_w_up'][i] = _mm(s['h2'], du, ta=True, out_dtype=XFER, tm=512, tn=1408, tk=4224, name="mm_ffn_up_dw",
                                  oslots=_ffn_perm)
        dx1, sums2 = _normmod_bwd(s['x1'], nw2, mod[i], dh2, dx, 1, ctx_rows, "normmod_bwd")
        dmix, dg1 = _gate_bwd(dx1, s['mix'], mod[i], 0, ctx_rows, "gate_bwd")

        def send_early(after):
            if pend_early is not None:
                scatter_wait(pend_early, after)
            return scatter_start(layer_weights(i)[1:], f"early_{i}")

        def send_late(after):
            if pend_late is not None:
                scatter_wait(pend_late, after)
            return scatter_start(layer_weights(i)[:1], f"late_{i}")

        if i % 2 == 0:
            do = _mm(dmix, w_out[j], tb=True, out_dtype=MXU, tm=768, tn=1024, tk=2048, name="mm_ssd_out_dx")
            gbig['ssd_w_out'][j] = _mm(s['o'], dmix, ta=True, out_dtype=XFER, tm=512, tn=1024, tk=4224,
                                       name="mm_ssd_out_dw").reshape(NDEV, -1, D)
            pend_early, tok = send_early(do)
            dy, dzx, fs, dd = _ssd_finish_bwd(s['yf'], s['yb'], s['xbc'], s['zx'], s['dexp'], s['snw'] + tok[0, 0], do, et_f,
                                              (T, IN), "ssd_finish_bwd")
            acc = _ssd_scan_bwd(s['xbc'], s['zx'], s['dtb'], s['alog'], e_f, et_f, s['dexp'], s['hin_f'], dy, None,
                                H, ncc, False, "ssd_scan_bwd_f")
            dxbc, ddt, ssm = _ssd_scan_bwd(s['xbc'], s['zx'], s['dtb'], s['alog'], e_b, et_b, s['dexp'], s['hin_b'], dy, acc,
                                           H, ncc, True, "ssd_scan_bwd_b")
            dzx, gscw = _ssd_conv_bwd(s['zx'], dxbc, dzx, s['cw'], s['cb'], DI, ctx_rows, "ssd_conv_bwd")
            dzx = _put_cols(dzx, ddt, IN // LANE - 1, "ssd_put_ddt")
            dwi = _mm(s['h1'], dzx, ta=True, tm=512, tn=1152, tk=4224, name="mm_ssd_in_dw")
            gbig['ssd_w_in'][j] = _split_cols(dwi, NDEV, XFER, "ssd_in_split")
            pend_late, tok = send_late(gbig['ssd_w_in'][j])
            dh1 = _mm(dzx, w_in[j], tb=True, out_dtype=MXU, tm=768, tn=1024, tk=3456, name="mm_ssd_in_dx", after=tok)
            gs['ssd_conv_w'][j], gs['ssd_conv_b'][j] = gscw[0:3], gscw[3]
            gs['ssd_dt_bias_f'][j], gs['ssd_dt_bias_b'][j] = ssm[1, :H], ssm[1, H:]
            gs['ssd_a_log_f'][j], gs['ssd_a_log_b'][j] = ssm[0, :H], ssm[0, H:]
            gs['ssd_d'][j], gs['ssd_norm_w'][j] = dd[1, :H], fs[0]
        else:
            do = _mm(dmix, w_o[j], tb=True, out_dtype=MXU, tm=768, tn=1024, tk=2048, name="mm_attn_out_dx")
            gbig['attn_w_o'][j] = _mm(s['o'], dmix, ta=True, out_dtype=XFER, tm=512, tn=1024, tk=4224,
                                      name="mm_attn_out_dw").reshape(NDEV, -1, D)
            pend_early, tok = send_early(do)
            dq, dk, dv, dsk = _attn_bwd(s['qr'], s['kr'], s['vb'], w['attn_sinks'][j] + tok[0, 0], do, n_q, ctx_rows, "attn_bwd")
            dqkv, gg = _qk_prep_bwd(s['qkv'], s['qg'], s['kg'], cos, sin, dq, dk, dv, n_q, ctx_rows, "qk_prep_bwd")
            gbig['attn_w_qkv'][j] = _mm(s['h1'], dqkv, ta=True, out_dtype=XFER, tm=512, tn=384, tk=4224, name="mm_qkv_dw",
                                        oslots=_ident)
            pend_late, tok = send_late(gbig['attn_w_qkv'][j])
            dh1 = _mm(dqkv, w_qkv[j], tb=True, out_dtype=MXU, tm=768, tn=1024, tk=384, name="mm_qkv_dx", bslots=_ident,
                      after=tok)
            gs['attn_q_gain'][j], gs['attn_k_gain'][j] = gg[0], gg[1]
            gs['attn_sinks'][j] = dsk.reshape(ATTN_KV_HEADS, 8, LANE)[:, 0, :n_q // ATTN_KV_HEADS].reshape(n_q)
        dx, sums1 = _normmod_bwd(s['x0'], nw1, mod[i], dh1, dx1, 0, ctx_rows, "normmod_bwd")
        gs['norm1_w'][i], gs['norm2_w'][i] = sums1[0, 2] + sums1[1, 2], sums2[0, 2] + sums2[1, 2]
        dmod[i] = jnp.stack([sums1[:, 0], sums1[:, 1], dg1[:, 0], sums2[:, 0], sums2[:, 1], dg2[:, 0]], axis=1)
    grad_x = dx[ctx_rows:][None]
    dmod = jnp.stack(dmod)
    dmod_ctx, dmod_lat = dmod[:, 0].reshape(L, 6 * D), dmod[:, 1].reshape(L, 6 * D)
    gs['ada_b'] = dmod_ctx + dmod_lat

    out = {}

    small_g = [jnp.stack(gs[n]) if isinstance(gs[n], list) else gs[n] for n in SMALL]
    extras = [jnp.stack(gs['ssd_conv_w']), jnp.stack(gs['ffn_conv_w'])]
    shapes_c = [w[n].shape for n in SMALL] + [e.shape for e in extras]
    g_c = _small_allgather(_pack(small_g + extras), "gather_small")
    zeros = [jnp.zeros(e.shape, F32) for e in extras]
    res = _adamw_small(g_c, _pack([w[n] for n in SMALL] + zeros), _pack([m[n] for n in SMALL] + zeros),
                       _pack([v[n] for n in SMALL] + zeros), "adamw_small")
    res = [_unpack(r, shapes_c) for r in res]
    for k, n in enumerate(SMALL):
        out[n] = tuple(r[k] for r in res)
    g_scw, g_fcw = res[0][len(SMALL)], res[0][len(SMALL) + 1]
    g_scw = lax.dynamic_index_in_dim(g_scw.reshape(n_ssd, 3, NDEV, XBC // NDEV), me, axis=2, keepdims=False)
    g_fcw = lax.dynamic_index_in_dim(g_fcw.reshape(L, 3, NDEV, G), me, axis=2, keepdims=False)
    conv = ['ssd_conv_w', 'ffn_conv_w']
    res = _adamw_small(_pack([g_scw, g_fcw])[None], _pack([w[n] for n in conv]), _pack([m[n] for n in conv]),
                       _pack([v[n] for n in conv]), "adamw_conv")
    res = [_unpack(r, [w[n].shape for n in conv]) for r in res]
    for k, n in enumerate(conv):
        out[n] = tuple(r[k] for r in res)

    g_m = _small_allgather(jnp.concatenate([dmod_lat, dmod_ctx], axis=0), "gather_dmod")
    all_lat, all_ctx = g_m[:, :L], g_m[:, L:]
    my_cols = lambda t: lax.dynamic_slice(t, (0, 0, me * nl), (NDEV, L, nl))
    dml = jnp.concatenate([my_cols(all_lat), jnp.zeros((MOD_ROWS - NDEV, L, nl), F32)], axis=0)
    g_ada, pc = _modulation_bwd(s_in, dml, my_cols(all_ctx), w['ada_w'], "modulation_bwd")
    out['ada_w'] = _adamw(g_ada[None], w['ada_w'], m['ada_w'], v['ada_w'], "adamw")
    g_d = _small_allgather(_pack([pc[0]]), "gather_cctx")
    res = _cctx_update(g_d, _pack([w['c_ctx']]), _pack([m['c_ctx']]), _pack([v['c_ctx']]), "adamw_cctx")
    out['c_ctx'] = tuple(_unpack(r, [(D,)])[0] for r in res)

    c_new = out['c_ctx'][0]
    scatter_wait(pend_early, c_new)
    last = [n for n, _ in layer_weights(0)[:1]]
    for name in [n for n in BIG if n not in last]:
        out[name] = _adamw(lands[name], w[name], m[name], v[name], "adamw")
    scatter_wait(pend_late, out['ffn_w_down'][0])
    for name in last:
        out[name] = _adamw(lands[name], w[name], m[name], v[name], "adamw")

    return (loss, grad_x) + tuple(out[n][k] for k in range(4) for n in WEIGHTS)


def kernel(x, c, ctx, c_ctx, ada_w, ada_b, norm1_w, norm2_w, ssd_w_in, ssd_conv_w, ssd_conv_b, ssd_dt_bias_f, ssd_dt_bias_b, ssd_a_log_f, ssd_a_log_b, ssd_d, ssd_norm_w, ssd_w_out, attn_w_qkv, attn_q_gain, attn_k_gain, attn_sinks, attn_w_o, ffn_w_up, ffn_conv_w, ffn_conv_b, ffn_w_down, loss_target, m_c_ctx, m_ada_w, m_ada_b, m_norm1_w, m_norm2_w, m_ssd_w_in, m_ssd_conv_w, m_ssd_conv_b, m_ssd_dt_bias_f, m_ssd_dt_bias_b, m_ssd_a_log_f, m_ssd_a_log_b, m_ssd_d, m_ssd_norm_w, m_ssd_w_out, m_attn_w_qkv, m_attn_q_gain, m_attn_k_gain, m_attn_sinks, m_attn_w_o, m_ffn_w_up, m_ffn_conv_w, m_ffn_conv_b, m_ffn_w_down, v_c_ctx, v_ada_w, v_ada_b, v_norm1_w, v_norm2_w, v_ssd_w_in, v_ssd_conv_w, v_ssd_conv_b, v_ssd_dt_bias_f, v_ssd_dt_bias_b, v_ssd_a_log_f, v_ssd_a_log_b, v_ssd_d, v_ssd_norm_w, v_ssd_w_out, v_attn_w_qkv, v_attn_q_gain, v_attn_k_gain, v_attn_sinks, v_attn_w_o, v_ffn_w_up, v_ffn_conv_w, v_ffn_conv_b, v_ffn_w_down):
    w = dict(c_ctx=c_ctx, ada_w=ada_w, ada_b=ada_b, norm1_w=norm1_w, norm2_w=norm2_w, ssd_w_in=ssd_w_in, ssd_conv_w=ssd_conv_w, ssd_conv_b=ssd_conv_b, ssd_dt_bias_f=ssd_dt_bias_f, ssd_dt_bias_b=ssd_dt_bias_b, ssd_a_log_f=ssd_a_log_f, ssd_a_log_b=ssd_a_log_b, ssd_d=ssd_d, ssd_norm_w=ssd_norm_w, ssd_w_out=ssd_w_out, attn_w_qkv=attn_w_qkv, attn_q_gain=attn_q_gain, attn_k_gain=attn_k_gain, attn_sinks=attn_sinks, attn_w_o=attn_w_o, ffn_w_up=ffn_w_up, ffn_conv_w=ffn_conv_w, ffn_conv_b=ffn_conv_b, ffn_w_down=ffn_w_down)
    m = dict(c_ctx=m_c_ctx, ada_w=m_ada_w, ada_b=m_ada_b, norm1_w=m_norm1_w, norm2_w=m_norm2_w, ssd_w_in=m_ssd_w_in, ssd_conv_w=m_ssd_conv_w, ssd_conv_b=m_ssd_conv_b, ssd_dt_bias_f=m_ssd_dt_bias_f, ssd_dt_bias_b=m_ssd_dt_bias_b, ssd_a_log_f=m_ssd_a_log_f, ssd_a_log_b=m_ssd_a_log_b, ssd_d=m_ssd_d, ssd_norm_w=m_ssd_norm_w, ssd_w_out=m_ssd_w_out, attn_w_qkv=m_attn_w_qkv, attn_q_gain=m_attn_q_gain, attn_k_gain=m_attn_k_gain, attn_sinks=m_attn_sinks, attn_w_o=m_attn_w_o, ffn_w_up=m_ffn_w_up, ffn_conv_w=m_ffn_conv_w, ffn_conv_b=m_ffn_conv_b, ffn_w_down=m_ffn_w_down)
    v = dict(c_ctx=v_c_ctx, ada_w=v_ada_w, ada_b=v_ada_b, norm1_w=v_norm1_w, norm2_w=v_norm2_w, ssd_w_in=v_ssd_w_in, ssd_conv_w=v_ssd_conv_w, ssd_conv_b=v_ssd_conv_b, ssd_dt_bias_f=v_ssd_dt_bias_f, ssd_dt_bias_b=v_ssd_dt_bias_b, ssd_a_log_f=v_ssd_a_log_f, ssd_a_log_b=v_ssd_a_log_b, ssd_d=v_ssd_d, ssd_norm_w=v_ssd_norm_w, ssd_w_out=v_ssd_w_out, attn_w_qkv=v_attn_w_qkv, attn_q_gain=v_attn_q_gain, attn_k_gain=v_attn_k_gain, attn_sinks=v_attn_sinks, attn_w_o=v_attn_w_o, ffn_w_up=v_ffn_w_up, ffn_conv_w=v_ffn_conv_w, ffn_conv_b=v_ffn_conv_b, ffn_w_down=v_ffn_w_down)
    return _step(x, c, ctx, w, loss_target, m, v)
```

```python
import functools

import numpy as np
import jax
import jax.numpy as jnp
from jax import lax
from jax.experimental import pallas as pl
from jax.experimental.pallas import tpu as pltpu

F32 = jnp.float32
BF16 = jnp.bfloat16
MXU = BF16
XFER = BF16
NORM_EPS = 1e-6
VMEM_CAP = 56 * 1024 * 1024
HALO = 8
LANE = 128
NDEV = 8

GRID_W = 64
ROPE_THETA = 10000.0
ATTN_KV_HEADS = 4
ATTN_WINDOW = 128
BLK = 128
SSD_GROUPS = 8

ADAM_LR, ADAM_B1, ADAM_B2, ADAM_EPS, ADAM_WD, ADAM_STEP = 0.001, 0.9, 0.999, 1e-08, 0.01, 10

MESH_ID = pl.DeviceIdType.MESH


def _cparams(sem, est_bytes):
    lim = int(min(VMEM_CAP, max(16 * 1024 * 1024, est_bytes * 1.3 + (4 << 20))))
    return pltpu.CompilerParams(dimension_semantics=sem, vmem_limit_bytes=lim)


def _nbytes(shape, dtype):
    return int(np.prod(shape)) * jnp.dtype(dtype).itemsize


def _silu(x):
    return x * jax.nn.sigmoid(x)


def _dsilu(x):
    s = jax.nn.sigmoid(x)
    return s * (1.0 + x * (1.0 - s))


def _split3(v):
    h = v.astype(BF16)
    r = v - h.astype(F32)
    m = r.astype(BF16)
    l = (r - m.astype(F32)).astype(BF16)
    return h, m, l


def _dot(a, b, dn=(((1,), (0,)), ((), ()))):
    return lax.dot_general(a, b, dn, preferred_element_type=F32)


NT = (((1,), (1,)), ((), ()))
TN = (((0,), (0,)), ((), ()))


def _dot3_rhs(sel, v):
    return sum(_dot(sel, p) for p in _split3(v))


def _dot3_lhs(v, sel, dn=(((1,), (0,)), ((), ()))):
    return sum(_dot(p, sel, dn) for p in _split3(v))


def _dot2_stacked(vals, sel):
    pieces = []
    for v in vals:
        h, m, _ = _split3(v)
        pieces += [h, m]
    r = _dot(jnp.concatenate(pieces, axis=0), sel)
    out, row = [], 0
    for v in vals:
        n = v.shape[0]
        out.append(r[row:row + n] + r[row + n:row + 2 * n])
        row += 2 * n
    return out


def _expand2(vals, sel2):
    lhs = []
    for v in vals:
        h, m, _ = _split3(v)
        lhs.append(jnp.concatenate([h, m], axis=1))
    r = _dot(jnp.concatenate(lhs, axis=0), sel2)
    n = vals[0].shape[0]
    return [r[i * n:(i + 1) * n] for i in range(len(vals))]


def _ident(s):
    return s


def _ffn_perm(s):
    return (s % 2) * 4 + s // 2


def _mm(a, b, *, ta=False, tb=False, out_dtype=F32, tm, tn, tk, name, bslots=None, oslots=None,
        resid=None, gate=None, ctx_rows=0, after=None):
    M = a.shape[1] if ta else a.shape[0]
    K = a.shape[0] if ta else a.shape[1]
    if bslots is None:
        N = b.shape[0] if tb else b.shape[1]
    else:
        G = b.shape[2]
        N = b.shape[1] if tb else NDEV * G
        assert (NDEV * G == K) if tb else (b.shape[1] == K)
    tm, tn, tk = min(tm, M), min(tn, N), min(tk, K)
    if bslots is not None:
        if tb:
            tk = min(tk, G)
            assert G % tk == 0
        else:
            tn = min(tn, G)
            assert G % tn == 0
    if oslots is not None:
        Go = N // NDEV
        tn = min(tn, Go)
        assert Go % tn == 0
    assert M % tm == 0 and N % tn == 0 and K % tk == 0, (name, M, N, K, tm, tn, tk)
    nk = K // tk
    fused = resid is not None
    dn = (((0 if ta else 1,), (1 if tb else 0,)), ((), ()))

    n_in = 2 + (2 if fused else 0) + (1 if after is not None else 0)

    def body(*refs):
        a_ref, b_ref = refs[0], refs[1]
        if fused:
            r_ref, g_ref = refs[2], refs[3]
            o_ref, x_ref = refs[n_in], refs[n_in + 1]
            rest = refs[n_in + 2:]
        else:
            o_ref = refs[n_in]
            rest = refs[n_in + 1:]
        bv = b_ref[0] if bslots is not None else b_ref[...]
        p = lax.dot_general(a_ref[...].astype(MXU), bv.astype(MXU), dn, preferred_element_type=F32)

        def finish(acc):
            if oslots is not None:
                o_ref[0] = acc.astype(o_ref.dtype)
            else:
                o_ref[...] = acc.astype(o_ref.dtype)
            if fused:
                row = pl.program_id(0) * tm + lax.broadcasted_iota(jnp.int32, (tm, 1), 0)
                g = jnp.where(row < ctx_rows, g_ref[0:1, :], g_ref[1:2, :])
                x_ref[...] = r_ref[...] + g * acc

        if nk == 1:
            finish(p)
        else:
            acc_ref = rest[0]
            k = pl.program_id(2)

            @pl.when(k == 0)
            def _():
                acc_ref[...] = p

            @pl.when(k > 0)
            def _():
                acc_ref[...] += p

            @pl.when(k == nk - 1)
            def _():
                finish(acc_ref[...])

    a_spec = pl.BlockSpec((tk, tm), lambda i, j, k: (k, i)) if ta else pl.BlockSpec((tm, tk), lambda i, j, k: (i, k))
    if bslots is None:
        b_spec = pl.BlockSpec((tn, tk), lambda i, j, k: (j, k)) if tb else pl.BlockSpec((tk, tn), lambda i, j, k: (k, j))
    elif tb:
        kpg = G // tk
        b_spec = pl.BlockSpec((1, tn, tk), lambda i, j, k: (bslots(k // kpg), j, k % kpg))
    else:
        npg = G // tn
        b_spec = pl.BlockSpec((1, tk, tn), lambda i, j, k: (bslots(j // npg), k, j % npg))
    if oslots is None:
        o_spec = pl.BlockSpec((tm, tn), lambda i, j, k: (i, j))
        o_shape = jax.ShapeDtypeStruct((M, N), out_dtype)
    else:
        opg = Go // tn
        o_spec = pl.BlockSpec((1, tm, tn), lambda i, j, k: (oslots(j // opg), i, j % opg))
        o_shape = jax.ShapeDtypeStruct((NDEV, M, Go), out_dtype)
    in_specs = [a_spec, b_spec]
    out_shape = [o_shape]
    out_specs = [o_spec]
    args = [a, b]
    est = 2 * (_nbytes((tm, tk), a.dtype) + _nbytes((tk, tn), b.dtype) + _nbytes((tm, tn), out_dtype)) + 3 * _nbytes((tm, tn), F32)
    if fused:
        in_specs += [o_spec, pl.BlockSpec((2, tn), lambda i, j, k: (0, j))]
        out_shape.append(jax.ShapeDtypeStruct((M, N), F32))
        out_specs.append(o_spec)
        args += [resid, gate]
        est += 4 * _nbytes((tm, tn), F32)
    if after is not None:
        in_specs.append(pl.BlockSpec((8, LANE), lambda i, j, k: (0, 0)))
        args.append(after)
    scratch = [] if nk == 1 else [pltpu.VMEM((tm, tn), F32)]
    res = pl.pallas_call(
        body, name=name, grid=(M // tm, N // tn, nk), in_specs=in_specs, out_specs=out_specs, out_shape=out_shape,
        scratch_shapes=scratch, compiler_params=_cparams(("parallel", "parallel", "arbitrary"), est),
    )(*args)
    return res if fused else res[0]


def _stream_of(i, tr, ctx_rows):
    return jnp.where(i * tr < ctx_rows, 0, 1)


def _acc_by_stream(sums_ref, part, i, n_ctx):
    @pl.when((i == 0) | (i == n_ctx))
    def _():
        sums_ref[0] = part

    @pl.when((i != 0) & (i != n_ctx))
    def _():
        sums_ref[0] += part


def _normmod(x, nw, mod, which, ctx_rows, name, tr=256):
    T, D = x.shape
    tr = min(tr, ctx_rows)
    assert T % tr == 0 and ctx_rows % tr == 0
    s_sh, s_sc = 3 * which, 3 * which + 1

    def body(x_ref, nw_ref, mod_ref, h_ref):
        xv = x_ref[...]
        r = lax.rsqrt(jnp.mean(xv * xv, axis=-1, keepdims=True) + NORM_EPS)
        y = (xv * r) * nw_ref[...]
        h_ref[...] = (y * (1.0 + mod_ref[0, s_sc:s_sc + 1, :]) + mod_ref[0, s_sh:s_sh + 1, :]).astype(h_ref.dtype)

    return pl.pallas_call(
        body, name=name, grid=(T // tr,),
        in_specs=[pl.BlockSpec((tr, D), lambda i: (i, 0)), pl.BlockSpec((1, D), lambda i: (0, 0)),
                  pl.BlockSpec((1, 6, D), lambda i: (_stream_of(i, tr, ctx_rows), 0, 0))],
        out_specs=pl.BlockSpec((tr, D), lambda i: (i, 0)),
        out_shape=jax.ShapeDtypeStruct((T, D), MXU),
        compiler_params=_cparams(("parallel",), 10 * _nbytes((tr, D), F32)),
    )(x, nw, mod)


def _normmod_bwd(x, nw, mod, dh, dx_in, which, ctx_rows, name, tr=256):
    T, D = x.shape
    tr = min(tr, ctx_rows)
    s_sc = 3 * which + 1
    n_ctx = ctx_rows // tr

    def body(x_ref, nw_ref, mod_ref, dh_ref, dxi_ref, dx_ref, sums_ref):
        i = pl.program_id(0)
        xv = x_ref[...]
        r = lax.rsqrt(jnp.mean(xv * xv, axis=-1, keepdims=True) + NORM_EPS)
        xh = xv * r
        dh_v = dh_ref[...].astype(F32)
        sc1 = 1.0 + mod_ref[0, s_sc:s_sc + 1, :]
        nwv = nw_ref[...]
        dxh = dh_v * (nwv * sc1)
        dx_ref[...] = dxi_ref[...] + r * (dxh - xh * jnp.mean(dxh * xh, axis=-1, keepdims=True))
        t = dh_v * xh
        part = jnp.concatenate([jnp.sum(dh_v, axis=0, keepdims=True), jnp.sum(t * nwv, axis=0, keepdims=True),
                                jnp.sum(t * sc1, axis=0, keepdims=True), jnp.zeros((5, D), F32)], axis=0)
        _acc_by_stream(sums_ref, part, i, n_ctx)

    row = pl.BlockSpec((tr, D), lambda i: (i, 0))
    return pl.pallas_call(
        body, name=name, grid=(T // tr,),
        in_specs=[row, pl.BlockSpec((1, D), lambda i: (0, 0)),
                  pl.BlockSpec((1, 6, D), lambda i: (_stream_of(i, tr, ctx_rows), 0, 0)), row, row],
        out_specs=[row, pl.BlockSpec((1, 8, D), lambda i: (_stream_of(i, tr, ctx_rows), 0, 0))],
        out_shape=[jax.ShapeDtypeStruct((T, D), F32), jax.ShapeDtypeStruct((2, 8, D), F32)],
        compiler_params=_cparams(("arbitrary",), 16 * _nbytes((tr, D), F32)),
    )(x, nw, mod, dh, dx_in)


def _gate_bwd(dx, mix, mod, which, ctx_rows, name, tr=256):
    T, D = dx.shape
    tr = min(tr, ctx_rows)
    s_g = 3 * which + 2
    n_ctx = ctx_rows // tr

    def body(dx_ref, mix_ref, mod_ref, dm_ref, sums_ref):
        i = pl.program_id(0)
        dxv = dx_ref[...]
        dm_ref[...] = (dxv * mod_ref[0, s_g:s_g + 1, :]).astype(dm_ref.dtype)
        part = jnp.concatenate([jnp.sum(dxv * mix_ref[...], axis=0, keepdims=True), jnp.zeros((7, D), F32)], axis=0)
        _acc_by_stream(sums_ref, part, i, n_ctx)

    row = pl.BlockSpec((tr, D), lambda i: (i, 0))
    return pl.pallas_call(
        body, name=name, grid=(T // tr,),
        in_specs=[row, row, pl.BlockSpec((1, 6, D), lambda i: (_stream_of(i, tr, ctx_rows), 0, 0))],
        out_specs=[row, pl.BlockSpec((1, 8, D), lambda i: (_stream_of(i, tr, ctx_rows), 0, 0))],
        out_shape=[jax.ShapeDtypeStruct((T, D), MXU), jax.ShapeDtypeStruct((2, 8, D), F32)],
        compiler_params=_cparams(("arbitrary",), 10 * _nbytes((tr, D), F32)),
    )(dx, mix, mod)


def _halo_specs(tr, tn, n_row_tiles, col_of):
    g = tr // HALO
    last = n_row_tiles * g - 1
    return [pl.BlockSpec((HALO, tn), lambda j, i: (jnp.maximum(i * g - 1, 0), col_of(j))),
            pl.BlockSpec((tr, tn), lambda j, i: (i, col_of(j))),
            pl.BlockSpec((HALO, tn), lambda j, i: (jnp.minimum((i + 1) * g, last), col_of(j)))]


def _ext(p_ref, m_ref, n_ref):
    return jnp.concatenate([p_ref[...].astype(F32), m_ref[...].astype(F32), n_ref[...].astype(F32)], axis=0)


def _seq_masks(i, tr, ctx_rows, total_rows):
    row = i * tr - HALO + lax.broadcasted_iota(jnp.int32, (tr + 2 * HALO, 1), 0)
    has_prev = (row != 0) & (row != ctx_rows)
    has_next = (row != ctx_rows - 1) & (row != total_rows - 1)
    return has_prev, has_next


def _shift_down(e):
    return pltpu.roll(e, 1, 0)


def _shift_up(e):
    return pltpu.roll(e, e.shape[0] - 1, 0)


def _neighbours(e, masks):
    prev, nxt = _shift_down(e), _shift_up(e)
    if masks is not None:
        prev, nxt = jnp.where(masks[0], prev, 0.0), jnp.where(masks[1], nxt, 0.0)
    return prev, nxt


def _conv3(e, prev, nxt, w):
    return prev * w[0:1, :] + e * w[1:2, :] + nxt * w[2:3, :]


def _conv3_t(d, w, masks):
    from_prev, from_next = _neighbours(d, masks)
    return from_next * w[0:1, :] + d * w[1:2, :] + from_prev * w[2:3, :]


def _conv_wgrad(d, e, prev, nxt):
    c = slice(HALO, e.shape[0] - HALO)
    dc = d[c]
    return jnp.concatenate([jnp.sum(dc * prev[c], axis=0, keepdims=True), jnp.sum(dc * e[c], axis=0, keepdims=True),
                            jnp.sum(dc * nxt[c], axis=0, keepdims=True), jnp.sum(dc, axis=0, keepdims=True),
                            jnp.zeros((4, e.shape[1]), F32)], axis=0)


def _per_tile_kind(i, tr, ctx_rows, total_rows, fn):
    n_ctx, n_all = ctx_rows // tr, total_rows // tr
    at_end = (i == 0) | (i == n_ctx - 1) | (i == n_ctx) | (i == n_all - 1)

    @pl.when(at_end)
    def _():
        fn(_seq_masks(i, tr, ctx_rows, total_rows))

    @pl.when(jnp.logical_not(at_end))
    def _():
        fn(None)


def _acc_first(ref, part, i):
    @pl.when(i == 0)
    def _():
        ref[...] = part

    @pl.when(i > 0)
    def _():
        ref[...] += part


def _ffn_mid(u, cw, cb, ctx_rows, name, tr=128):
    T, F2 = u.shape
    G = F2 // NDEV
    tr = min(tr, ctx_rows)
    nr, nc = T // tr, NDEV // 2

    def body(up, um, un, w_ref, b_ref, a_ref):
        def tile(masks):
            e = _ext(up, um, un)
            uc = _conv3(e, *_neighbours(e, masks), w_ref[...])[HALO:HALO + tr] + b_ref[...]
            a_ref[...] = (_silu(uc[:, G:]) * uc[:, :G]).astype(a_ref.dtype)

        _per_tile_kind(pl.program_id(1), tr, ctx_rows, T, tile)

    return pl.pallas_call(
        body, name=name, grid=(nc, nr),
        in_specs=_halo_specs(tr, 2 * G, nr, lambda j: j) + [pl.BlockSpec((3, 2 * G), lambda j, i: (0, j)),
                                                             pl.BlockSpec((1, 2 * G), lambda j, i: (0, j))],
        out_specs=pl.BlockSpec((tr, G), lambda j, i: (i, j)),
        out_shape=jax.ShapeDtypeStruct((T, F2 // 2), MXU),
        compiler_params=_cparams(("parallel", "parallel"), 12 * _nbytes((tr + 16, 2 * G), F32)),
    )(u, u, u, cw, cb)


def _ffn_mid_bwd(u, da, cw, cb, ctx_rows, name, tr=128):
    T, F2 = u.shape
    G = F2 // NDEV
    tr = min(tr, ctx_rows)
    nr, nc = T // tr, NDEV // 2

    def body(up, um, un, dp, dm, dn_, w_ref, b_ref, du_ref, gw_ref):
        i = pl.program_id(1)

        def tile(masks):
            e = _ext(up, um, un)
            w = w_ref[...]
            prev, nxt = _neighbours(e, masks)
            uc = _conv3(e, prev, nxt, w) + b_ref[...]
            val, gt = uc[:, :G], uc[:, G:]
            dav = _ext(dp, dm, dn_)
            sg = jax.nn.sigmoid(gt)
            duc = jnp.concatenate([dav * (gt * sg), dav * val * (sg * (1.0 + gt * (1.0 - sg)))], axis=1)
            du_ref[...] = _conv3_t(duc, w, masks)[HALO:HALO + tr].astype(du_ref.dtype)
            _acc_first(gw_ref, _conv_wgrad(duc, e, prev, nxt), i)

        _per_tile_kind(i, tr, ctx_rows, T, tile)

    du, gw = pl.pallas_call(
        body, name=name, grid=(nc, nr),
        in_specs=(_halo_specs(tr, 2 * G, nr, lambda j: j) + _halo_specs(tr, G, nr, lambda j: j)
                  + [pl.BlockSpec((3, 2 * G), lambda j, i: (0, j)), pl.BlockSpec((1, 2 * G), lambda j, i: (0, j))]),
        out_specs=[pl.BlockSpec((tr, 2 * G), lambda j, i: (i, j)), pl.BlockSpec((8, 2 * G), lambda j, i: (0, j))],
        out_shape=[jax.ShapeDtypeStruct((T, F2), MXU), jax.ShapeDtypeStruct((8, F2), F32)],
        compiler_params=_cparams(("parallel", "arbitrary"), 24 * _nbytes((tr + 16, 2 * G), F32)),
    )(u, u, u, da, da, da, cw, cb)
    return du, gw


def _ssd_conv(zx, cw, cb, col0, ctx_rows, name, tr=256, tn=512):
    T = zx.shape[0]
    C = cw.shape[1]
    tr, tn = min(tr, ctx_rows), min(tn, C)
    assert C % tn == 0 and col0 % tn == 0
    nr, nc, cb0 = T // tr, C // tn, col0 // tn

    def body(zp, zm, zn, w_ref, b_ref, o_ref):
        def tile(masks):
            e = _ext(zp, zm, zn)
            o_ref[...] = _silu(_conv3(e, *_neighbours(e, masks), w_ref[...])[HALO:HALO + tr] + b_ref[...])

        _per_tile_kind(pl.program_id(1), tr, ctx_rows, T, tile)

    return pl.pallas_call(
        body, name=name, grid=(nc, nr),
        in_specs=_halo_specs(tr, tn, nr, lambda j: j + cb0) + [pl.BlockSpec((3, tn), lambda j, i: (0, j)),
                                                                pl.BlockSpec((1, tn), lambda j, i: (0, j))],
        out_specs=pl.BlockSpec((tr, tn), lambda j, i: (i, j)),
        out_shape=jax.ShapeDtypeStruct((T, C), F32),
        compiler_params=_cparams(("parallel", "parallel"), 12 * _nbytes((tr + 16, tn), F32)),
    )(zx, zx, zx, cw, cb)


def _ssd_conv_bwd(zx, dxbc, dzx, cw, cb, col0, ctx_rows, name, tr=256, tn=512):
    T = zx.shape[0]
    C = cw.shape[1]
    tr, tn = min(tr, ctx_rows), min(tn, C)
    nr, nc, cb0 = T // tr, C // tn, col0 // tn

    def body(zp, zm, zn, dp, dm, dn_, w_ref, b_ref, dzx_in, dz_ref, gw_ref):
        del dzx_in
        i = pl.program_id(1)

        def tile(masks):
            e = _ext(zp, zm, zn)
            w = w_ref[...]
            prev, nxt = _neighbours(e, masks)
            pre = _conv3(e, prev, nxt, w) + b_ref[...]
            dpre = _ext(dp, dm, dn_) * _dsilu(pre)
            dz_ref[...] = _conv3_t(dpre, w, masks)[HALO:HALO + tr].astype(dz_ref.dtype)
            _acc_first(gw_ref, _conv_wgrad(dpre, e, prev, nxt), i)

        _per_tile_kind(i, tr, ctx_rows, T, tile)

    return pl.pallas_call(
        body, name=name, grid=(nc, nr),
        in_specs=(_halo_specs(tr, tn, nr, lambda j: j + cb0) + _halo_specs(tr, tn, nr, lambda j: j)
                  + [pl.BlockSpec((3, tn), lambda j, i: (0, j)), pl.BlockSpec((1, tn), lambda j, i: (0, j)),
                     pl.BlockSpec(memory_space=pl.ANY)]),
        out_specs=[pl.BlockSpec((tr, tn), lambda j, i: (i, j + cb0)), pl.BlockSpec((8, tn), lambda j, i: (0, j))],
        out_shape=[jax.ShapeDtypeStruct(dzx.shape, dzx.dtype), jax.ShapeDtypeStruct((8, C), F32)],
        input_output_aliases={8: 0},
        compiler_params=_cparams(("parallel", "arbitrary"), 24 * _nbytes((tr + 16, tn), F32)),
    )(zx, zx, zx, dxbc, dxbc, dxbc, cw, cb, dzx)


def _cat_cols(w3, name, tr=256):
    n, K, G = w3.shape
    tr = min(tr, K)

    def body(w_ref, o_ref):
        o_ref[...] = jnp.concatenate([w_ref[d].astype(F32) for d in range(n)], axis=1).astype(o_ref.dtype)

    return pl.pallas_call(
        body, name=name, grid=(K // tr,),
        in_specs=[pl.BlockSpec((n, tr, G), lambda i: (0, i, 0))], out_specs=pl.BlockSpec((tr, n * G), lambda i: (i, 0)),
        out_shape=jax.ShapeDtypeStruct((K, n * G), w3.dtype),
        compiler_params=_cparams(("parallel",), 6 * _nbytes((tr, n * G), F32)),
    )(w3)


def _split_cols(g, n, out_dtype, name, tr=256):
    K, NG = g.shape
    G = NG // n
    tr = min(tr, K)

    def body(g_ref, o_ref):
        for d in range(n):
            o_ref[d] = g_ref[:, d * G:(d + 1) * G].astype(o_ref.dtype)

    return pl.pallas_call(
        body, name=name, grid=(K // tr,),
        in_specs=[pl.BlockSpec((tr, NG), lambda i: (i, 0))], out_specs=pl.BlockSpec((n, tr, G), lambda i: (0, i, 0)),
        out_shape=jax.ShapeDtypeStruct((n, K, G), out_dtype),
        compiler_params=_cparams(("parallel",), 6 * _nbytes((tr, NG), F32)),
    )(g)


def _put_cols(dst, src, col_blk, name, tr=256):
    T, W = src.shape
    tr = min(tr, T)

    def body(s_ref, d_in, o_ref):
        del d_in
        o_ref[...] = s_ref[...].astype(o_ref.dtype)

    return pl.pallas_call(
        body, name=name, grid=(T // tr,),
        in_specs=[pl.BlockSpec((tr, W), lambda i: (i, 0)), pl.BlockSpec(memory_space=pl.ANY)],
        out_specs=pl.BlockSpec((tr, W), lambda i: (i, col_blk)),
        out_shape=jax.ShapeDtypeStruct(dst.shape, dst.dtype),
        input_output_aliases={1: 0},
        compiler_params=_cparams(("parallel",), 8 * _nbytes((tr, W), F32)),
    )(src, dst)


def _rope_tables(t_lat, ctx_rows, hd):
    half, quarter = hd // 2, hd // 4
    pos = jnp.arange(t_lat)
    row = (pos // GRID_W).astype(F32)
    col = (pos % GRID_W).astype(F32)
    inv_freq = ROPE_THETA ** (-jnp.arange(0, half, 2, dtype=F32) / half)
    ar, ac = row[:, None] * inv_freq[None, :], col[:, None] * inv_freq[None, :]
    cos = jnp.concatenate([jnp.cos(ar), jnp.cos(ar), jnp.cos(ac), jnp.cos(ac)], axis=1)
    sin = jnp.concatenate([-jnp.sin(ar), jnp.sin(ar), -jnp.sin(ac), jnp.sin(ac)], axis=1)
    del quarter
    cos = jnp.concatenate([jnp.ones((ctx_rows, hd), F32), cos], axis=0)
    sin = jnp.concatenate([jnp.zeros((ctx_rows, hd), F32), sin], axis=0)
    return cos, sin


def _partner(y):
    hd = y.shape[1]
    q = hd // 4
    lane = lax.broadcasted_iota(jnp.int32, y.shape, 1)
    return jnp.where((lane % (2 * q)) < q, pltpu.roll(y, hd - q, 1), pltpu.roll(y, q, 1))


def _qk_prep(qkv, qg, kg, cos, sin, n_q, ctx_rows, name, tr=256):
    T = qkv.shape[0]
    hd = qg.shape[1]
    n_kv = ATTN_KV_HEADS
    tr = min(tr, ctx_rows)

    def body(x_ref, qg_ref, kg_ref, c_ref, s_ref, q_ref, k_ref, v_ref):
        cv, sv = c_ref[...], s_ref[...]
        for h in range(n_q + n_kv):
            xh = x_ref[:, h * hd:(h + 1) * hd]
            r = lax.rsqrt(jnp.mean(xh * xh, axis=-1, keepdims=True) + NORM_EPS)
            y = (xh * r) * (qg_ref[...] if h < n_q else kg_ref[...])
            rot = y * cv + _partner(y) * sv
            if h < n_q:
                q_ref[:, h * hd:(h + 1) * hd] = rot.astype(q_ref.dtype)
            else:
                k_ref[:, (h - n_q) * hd:(h - n_q + 1) * hd] = rot.astype(k_ref.dtype)
        v_ref[...] = x_ref[:, (n_q + n_kv) * hd:].astype(v_ref.dtype)

    W = qkv.shape[1]
    return pl.pallas_call(
        body, name=name, grid=(T // tr,),
        in_specs=[pl.BlockSpec((tr, W), lambda i: (i, 0)), pl.BlockSpec((1, hd), lambda i: (0, 0)),
                  pl.BlockSpec((1, hd), lambda i: (0, 0)), pl.BlockSpec((tr, hd), lambda i: (i, 0)),
                  pl.BlockSpec((tr, hd), lambda i: (i, 0))],
        out_specs=[pl.BlockSpec((tr, n_q * hd), lambda i: (i, 0)), pl.BlockSpec((tr, n_kv * hd), lambda i: (i, 0)),
                   pl.BlockSpec((tr, n_kv * hd), lambda i: (i, 0))],
        out_shape=[jax.ShapeDtypeStruct((T, n_q * hd), MXU), jax.ShapeDtypeStruct((T, n_kv * hd), MXU),
                   jax.ShapeDtypeStruct((T, n_kv * hd), MXU)],
        compiler_params=_cparams(("parallel",), 6 * _nbytes((tr, W), F32)),
    )(qkv, qg, kg, cos, sin)


def _qk_prep_bwd(qkv, qg, kg, cos, sin, dq, dk, dv, n_q, ctx_rows, name, tr=256):
    T, W = qkv.shape
    hd = qg.shape[1]
    n_kv = ATTN_KV_HEADS
    tr = min(tr, ctx_rows)

    def body(x_ref, qg_ref, kg_ref, c_ref, s_ref, dq_ref, dk_ref, dv_ref, o_ref, g_ref):
        i = pl.program_id(0)
        cv, sv = c_ref[...], s_ref[...]
        gq = jnp.zeros((1, hd), F32)
        gk = jnp.zeros((1, hd), F32)
        for h in range(n_q + n_kv):
            xh = x_ref[:, h * hd:(h + 1) * hd]
            gain = qg_ref[...] if h < n_q else kg_ref[...]
            drot = (dq_ref[:, h * hd:(h + 1) * hd] if h < n_q else dk_ref[:, (h - n_q) * hd:(h - n_q + 1) * hd]).astype(F32)
            dy = drot * cv + _partner(drot * sv)
            r = lax.rsqrt(jnp.mean(xh * xh, axis=-1, keepdims=True) + NORM_EPS)
            xn = xh * r
            gsum = jnp.sum(dy * xn, axis=0, keepdims=True)
            if h < n_q:
                gq = gq + gsum
            else:
                gk = gk + gsum
            dxn = dy * gain
            o_ref[:, h * hd:(h + 1) * hd] = (r * (dxn - xn * jnp.mean(dxn * xn, axis=-1, keepdims=True))).astype(o_ref.dtype)
        o_ref[:, (n_q + n_kv) * hd:] = dv_ref[...].astype(o_ref.dtype)
        _acc_first(g_ref, jnp.concatenate([gq, gk, jnp.zeros((6, hd), F32)], axis=0), i)

    return pl.pallas_call(
        body, name=name, grid=(T // tr,),
        in_specs=[pl.BlockSpec((tr, W), lambda i: (i, 0)), pl.BlockSpec((1, hd), lambda i: (0, 0)),
                  pl.BlockSpec((1, hd), lambda i: (0, 0)), pl.BlockSpec((tr, hd), lambda i: (i, 0)),
                  pl.BlockSpec((tr, hd), lambda i: (i, 0)), pl.BlockSpec((tr, n_q * hd), lambda i: (i, 0)),
                  pl.BlockSpec((tr, n_kv * hd), lambda i: (i, 0)), pl.BlockSpec((tr, n_kv * hd), lambda i: (i, 0))],
        out_specs=[pl.BlockSpec((tr, W), lambda i: (i, 0)), pl.BlockSpec((8, hd), lambda i: (0, 0))],
        out_shape=[jax.ShapeDtypeStruct((T, W), MXU), jax.ShapeDtypeStruct((8, hd), F32)],
        compiler_params=_cparams(("arbitrary",), 8 * _nbytes((tr, W), F32)),
    )(qkv, qg, kg, cos, sin, dq, dk, dv)


def _attn_scores(q_ref, k_ref, sink_ref, h, qb, ctx_rows, nb, hd, grp):
    scale = hd ** -0.5
    w0 = jnp.clip(qb - 1, 0, nb - 3) * BLK
    w0 = pl.multiple_of(w0, BLK)
    qv = q_ref[...]
    qs = jnp.concatenate([qv[:, g * hd:(g + 1) * hd] for g in range(grp)], axis=0)
    kc = k_ref[0:ctx_rows, :]
    kb = k_ref[pl.ds(w0, 3 * BLK), :]
    s_c = _dot(qs, kc, NT) * scale
    s_b = _dot(qs, kb, NT) * scale
    n = grp * BLK
    qpos = qb * BLK + lax.broadcasted_iota(jnp.int32, (n, 3 * BLK), 0) % BLK
    kpos = w0 + lax.broadcasted_iota(jnp.int32, (n, 3 * BLK), 1)
    ok = (jnp.abs(kpos - qpos) <= ATTN_WINDOW) & (kpos >= ctx_rows) & (qpos >= ctx_rows)
    s_b = jnp.where(ok, s_b, -jnp.inf)
    gi = lax.broadcasted_iota(jnp.int32, (n, 1), 0) // BLK
    sink = jnp.zeros((n, 1), F32)
    for g in range(grp):
        sink = jnp.where(gi == g, sink_ref[h * grp + g], sink)
    m = jnp.maximum(jnp.maximum(jnp.max(s_c, axis=1, keepdims=True), jnp.max(s_b, axis=1, keepdims=True)), sink)
    e_c, e_b, e_s = jnp.exp(s_c - m), jnp.exp(s_b - m), jnp.exp(sink - m)
    inv = 1.0 / (jnp.sum(e_c, axis=1, keepdims=True) + jnp.sum(e_b, axis=1, keepdims=True) + e_s)
    return qs, kc, kb, w0, e_c * inv, e_b * inv, e_s * inv, gi


def _attn_fwd(qr, kr, vb, sinks, n_q, ctx_rows, name):
    T = qr.shape[0]
    n_kv = ATTN_KV_HEADS
    grp = n_q // n_kv
    hd = qr.shape[1] // n_q
    nb = T // BLK

    def body(sink_ref, q_ref, k_ref, v_ref, o_ref):
        h, qb = pl.program_id(0), pl.program_id(1)
        _, _, _, w0, p_c, p_b, _, _ = _attn_scores(q_ref, k_ref, sink_ref, h, qb, ctx_rows, nb, hd, grp)
        o = _dot(p_c.astype(MXU), v_ref[0:ctx_rows, :]) + _dot(p_b.astype(MXU), v_ref[pl.ds(w0, 3 * BLK), :])
        o_ref[...] = jnp.concatenate([o[g * BLK:(g + 1) * BLK] for g in range(grp)], axis=1).astype(o_ref.dtype)

    return pl.pallas_call(
        body, name=name, grid=(n_kv, nb),
        in_specs=[pl.BlockSpec(memory_space=pltpu.SMEM), pl.BlockSpec((BLK, grp * hd), lambda h, i: (i, h)),
                  pl.BlockSpec((T, hd), lambda h, i: (0, h)), pl.BlockSpec((T, hd), lambda h, i: (0, h))],
        out_specs=pl.BlockSpec((BLK, grp * hd), lambda h, i: (i, h)),
        out_shape=jax.ShapeDtypeStruct((T, n_q * hd), MXU),
        compiler_params=_cparams(("parallel", "arbitrary"), 4 * _nbytes((T, hd), MXU) + 24 * _nbytes((grp * BLK, 5 * BLK), F32)),
    )(sinks, qr, kr, vb)


def _attn_bwd(qr, kr, vb, sinks, do, n_q, ctx_rows, name):
    T = qr.shape[0]
    n_kv = ATTN_KV_HEADS
    grp = n_q // n_kv
    hd = qr.shape[1] // n_q
    nb = T // BLK
    scale = hd ** -0.5

    def body(sink_ref, q_ref, k_ref, v_ref, do_ref, dq_ref, dk_ref, dv_ref, ds_ref):
        h, qb = pl.program_id(0), pl.program_id(1)
        qs, kc, kb, w0, p_c, p_b, p_s, gi = _attn_scores(q_ref, k_ref, sink_ref, h, qb, ctx_rows, nb, hd, grp)
        dov = do_ref[...]
        dos = jnp.concatenate([dov[:, g * hd:(g + 1) * hd] for g in range(grp)], axis=0)
        vc = v_ref[0:ctx_rows, :]
        vw = v_ref[pl.ds(w0, 3 * BLK), :]
        dp_c = _dot(dos, vc, NT)
        dp_b = _dot(dos, vw, NT)
        delta = jnp.sum(p_c * dp_c, axis=1, keepdims=True) + jnp.sum(p_b * dp_b, axis=1, keepdims=True)
        ds_c = (p_c * (dp_c - delta) * scale).astype(MXU)
        ds_b = (p_b * (dp_b - delta) * scale).astype(MXU)
        dq = _dot(ds_c, kc) + _dot(ds_b, kb)
        dq_ref[...] = jnp.concatenate([dq[g * BLK:(g + 1) * BLK] for g in range(grp)], axis=1)

        @pl.when(qb == 0)
        def _():
            dk_ref[...] = jnp.zeros(dk_ref.shape, F32)
            dv_ref[...] = jnp.zeros(dv_ref.shape, F32)

        dk_ref[0:ctx_rows, :] += _dot(ds_c, qs, TN)
        dv_ref[0:ctx_rows, :] += _dot(p_c.astype(MXU), dos, TN)
        dk_ref[pl.ds(w0, 3 * BLK), :] += _dot(ds_b, qs, TN)
        dv_ref[pl.ds(w0, 3 * BLK), :] += _dot(p_b.astype(MXU), dos, TN)
        t = -(p_s * delta)
        lane = lax.broadcasted_iota(jnp.int32, (8, LANE), 1)
        part = jnp.zeros((8, LANE), F32)
        for g in range(grp):
            part = jnp.where(lane == g, jnp.sum(jnp.where(gi == g, t, 0.0)), part)
        _acc_first(ds_ref, part, qb)

    return pl.pallas_call(
        body, name=name, grid=(n_kv, nb),
        in_specs=[pl.BlockSpec(memory_space=pltpu.SMEM), pl.BlockSpec((BLK, grp * hd), lambda h, i: (i, h)),
                  pl.BlockSpec((T, hd), lambda h, i: (0, h)), pl.BlockSpec((T, hd), lambda h, i: (0, h)),
                  pl.BlockSpec((BLK, grp * hd), lambda h, i: (i, h))],
        out_specs=[pl.BlockSpec((BLK, grp * hd), lambda h, i: (i, h)), pl.BlockSpec((T, hd), lambda h, i: (0, h)),
                   pl.BlockSpec((T, hd), lambda h, i: (0, h)), pl.BlockSpec((8, LANE), lambda h, i: (h, 0))],
        out_shape=[jax.ShapeDtypeStruct((T, n_q * hd), F32), jax.ShapeDtypeStruct((T, n_kv * hd), F32),
                   jax.ShapeDtypeStruct((T, n_kv * hd), F32), jax.ShapeDtypeStruct((n_kv * 8, LANE), F32)],
        compiler_params=_cparams(("parallel", "arbitrary"), 4 * _nbytes((T, hd), MXU) + 4 * _nbytes((T, hd), F32)
                                 + 40 * _nbytes((grp * BLK, 5 * BLK), F32)),
    )(sinks, qr, kr, vb, do)


def _chunk_order(s, n_chunks, n_ctx, rev):
    if not rev:
        return s
    return jnp.where(s < n_ctx, n_ctx - 1 - s, n_chunks - 1 + n_ctx - s)


def _softplus(x):
    return jnp.maximum(x, 0.0) + jnp.log(1.0 + jnp.exp(-jnp.abs(x)))


def _expand_matrix(n_heads, p, rev):
    e = np.zeros((LANE, n_heads * p), np.float32)
    for h in range(n_heads):
        e[h + (n_heads if rev else 0), h * p:(h + 1) * p] = 1.0
    return jnp.asarray(e, BF16)


def _ssd_chunk_prep(dt_ref, dtb_ref, alog_ref, e_ref, rev):
    dt = _softplus(dt_ref[...] + dtb_ref[...])
    a = -jnp.exp(alog_ref[...])
    li = lax.broadcasted_iota(jnp.int32, (BLK, BLK), 0)
    si = lax.broadcasted_iota(jnp.int32, (BLK, BLK), 1)
    tri = (si >= li) if rev else (si <= li)
    acs = _dot3_rhs(tri.astype(BF16), a * dt)
    dtexp, aexp = _expand2([dt, acs], e_ref[...])
    return dt, a, tri, acs, dtexp, aexp


def _pair_cols(ap):
    lane = lax.broadcasted_iota(jnp.int32, ap.shape, 1)
    apr = pltpu.roll(ap, LANE // 2, 1)
    return jnp.where(lane < LANE // 2, ap, apr), jnp.where(lane < LANE // 2, apr, ap)


def _ssd_scan(xbc, zx, dtb, alog, emat, n_heads, n_ctx, rev, name):
    T, C = xbc.shape
    P = emat.shape[1] // n_heads
    DI = n_heads * P
    GN = (C - DI) // 2
    N = GN // SSD_GROUPS
    n_pairs = DI // LANE
    ppg = n_pairs // SSD_GROUPS
    n_chunks = T // BLK
    hoff = n_heads if rev else 0
    last = 0 if rev else BLK - 1
    dt_blk = zx.shape[1] // LANE - 1
    assert N == LANE and 2 * P == LANE and 2 * n_heads == LANE

    def body(xs_ref, b_ref, c_ref, dt_ref, dtb_ref, alog_ref, e_ref, y_ref, hin_ref, state_ref, xdt_s, xdec_s, aexp_s, at_s):
        s = pl.program_id(0)

        @pl.when(s == 0)
        def _():
            state_ref[...] = jnp.zeros(state_ref.shape, F32)

        dt, a, tri, acs, dtexp, aexp = _ssd_chunk_prep(dt_ref, dtb_ref, alog_ref, e_ref, rev)
        at_s[...] = acs.T
        aexp_s[...] = aexp
        xdt = xs_ref[...] * dtexp
        xdt_s[...] = xdt.astype(MXU)
        xdec_s[...] = (xdt * jnp.exp(aexp[last:last + 1, :] - aexp)).astype(MXU)
        hin_ref[0] = state_ref[...]
        lane = lax.broadcasted_iota(jnp.int32, (BLK, LANE), 1)

        def pair(k, carry):
            col = pl.multiple_of(k * LANE, LANE)
            gcol = pl.multiple_of((k // ppg) * N, N)
            bg = b_ref[:, pl.ds(gcol, N)].astype(MXU)
            cg = c_ref[:, pl.ds(gcol, N)].astype(MXU)
            cb = _dot(cg, bg, NT)
            ap = aexp_s[:, pl.ds(col, LANE)]
            ac0, ac1 = _pair_cols(ap)
            ar0 = at_s[pl.ds(2 * k + hoff, 1), :]
            ar1 = at_s[pl.ds(2 * k + 1 + hoff, 1), :]
            m0 = (cb * jnp.exp(jnp.where(tri, ac0 - ar0, -jnp.inf))).astype(MXU)
            m1 = (cb * jnp.exp(jnp.where(tri, ac1 - ar1, -jnp.inf))).astype(MXU)
            xp = xdt_s[:, pl.ds(col, LANE)]
            zero = jnp.zeros_like(xp)
            xbd = jnp.concatenate([jnp.where(lane < LANE // 2, xp, zero), jnp.where(lane >= LANE // 2, xp, zero)], axis=0)
            yd = _dot(jnp.concatenate([m0, m1], axis=1), xbd)
            ht = state_ref[k]
            yo = _dot(cg, ht.astype(MXU)) * jnp.exp(ap)
            y_ref[:, pl.ds(col, LANE)] = yd + yo
            st = _dot(bg, xdec_s[:, pl.ds(col, LANE)], TN)
            state_ref[k] = jnp.exp(aexp_s[pl.ds(last, 1), pl.ds(col, LANE)]) * ht + st
            return carry

        lax.fori_loop(0, n_pairs, pair, 0, unroll=8)

    order = lambda s: _chunk_order(s, n_chunks, n_ctx, rev)
    return pl.pallas_call(
        body, name=name, grid=(n_chunks,),
        in_specs=[pl.BlockSpec((BLK, DI), lambda s: (order(s), 0)),
                  pl.BlockSpec((BLK, GN), lambda s: (order(s), DI // GN)),
                  pl.BlockSpec((BLK, GN), lambda s: (order(s), DI // GN + 1)),
                  pl.BlockSpec((BLK, LANE), lambda s: (order(s), dt_blk)),
                  pl.BlockSpec((1, LANE), lambda s: (0, 0)), pl.BlockSpec((1, LANE), lambda s: (0, 0)),
                  pl.BlockSpec((2 * LANE, DI), lambda s: (0, 0))],
        out_specs=[pl.BlockSpec((BLK, DI), lambda s: (order(s), 0)),
                   pl.BlockSpec((1, n_pairs, N, LANE), lambda s: (order(s), 0, 0, 0))],
        out_shape=[jax.ShapeDtypeStruct((T, DI), F32), jax.ShapeDtypeStruct((n_chunks, n_pairs, N, LANE), F32)],
        scratch_shapes=[pltpu.VMEM((n_pairs, N, LANE), F32), pltpu.VMEM((BLK, DI), MXU), pltpu.VMEM((BLK, DI), MXU),
                        pltpu.VMEM((BLK, DI), F32), pltpu.VMEM((LANE, BLK), F32)],
        compiler_params=_cparams(("arbitrary",), 20 * _nbytes((BLK, DI), F32)),
    )(xbc, xbc, xbc, zx, dtb, alog, jnp.concatenate([emat, emat], axis=0))


def _ssd_scan_bwd(xbc, zx, dtb, alog, emat, emat_t, dexp, hin, dy, acc, n_heads, n_ctx, rev, name):
    T, C = xbc.shape
    P = emat.shape[1] // n_heads
    DI = n_heads * P
    GN = (C - DI) // 2
    N = GN // SSD_GROUPS
    n_pairs = DI // LANE
    ppg = n_pairs // SSD_GROUPS
    n_chunks = T // BLK
    hoff = n_heads if rev else 0
    last = 0 if rev else BLK - 1
    dt_blk = zx.shape[1] // LANE - 1
    has_acc = acc is not None

    def body(*refs):
        (xs_ref, b_ref, c_ref, dt_ref, dtb_ref, alog_ref, e_ref, et_ref, dexp_ref, hin_ref, dy_ref) = refs[:11]
        n_in = 11
        if has_acc:
            dxbc_in, ddt_in, sums_in = refs[11:14]
            n_in = 14
        dxbc_ref, ddt_ref, sums_ref = refs[n_in:n_in + 3]
        dstate_ref, xdt_s, xdec_s, aexp_s, at_s, dyd_s, z1_s, z3_s, dxdt_s, cdrow_s, rmat_s, cst_s = refs[n_in + 3:]
        s = pl.program_id(0)

        @pl.when(s == 0)
        def _():
            dstate_ref[...] = jnp.zeros(dstate_ref.shape, F32)

        dt, a, tri, acs, dtexp, aexp = _ssd_chunk_prep(dt_ref, dtb_ref, alog_ref, e_ref, rev)
        at_s[...] = acs.T
        aexp_s[...] = aexp
        xsv = xs_ref[...]
        xdt = xsv * dtexp
        xdt_s[...] = xdt.astype(MXU)
        decend = jnp.exp(aexp[last:last + 1, :] - aexp)
        xdec_s[...] = (xdt * decend).astype(MXU)
        dyv = dy_ref[...]
        dyd_s[...] = (dyv * jnp.exp(aexp)).astype(MXU)
        dxbc_ref[:, DI:] = jnp.zeros((BLK, 2 * GN), F32)
        rmat_s[...] = jnp.zeros(rmat_s.shape, F32)
        lane = lax.broadcasted_iota(jnp.int32, (BLK, LANE), 1)
        lo = lane < LANE // 2
        tri_t = (lax.broadcasted_iota(jnp.int32, (BLK, BLK), 0) <= lax.broadcasted_iota(jnp.int32, (BLK, BLK), 1)) if not rev \
            else (lax.broadcasted_iota(jnp.int32, (BLK, BLK), 0) >= lax.broadcasted_iota(jnp.int32, (BLK, BLK), 1))

        def pair(k, carry):
            col = pl.multiple_of(k * LANE, LANE)
            gcol = pl.multiple_of((k // ppg) * N, N)
            bg = b_ref[:, pl.ds(gcol, N)].astype(MXU)
            cg = c_ref[:, pl.ds(gcol, N)].astype(MXU)
            cb = _dot(cg, bg, NT)
            cbt = _dot(bg, cg, NT)
            ap = aexp_s[:, pl.ds(col, LANE)]
            ac0, ac1 = _pair_cols(ap)
            ar0 = at_s[pl.ds(2 * k + hoff, 1), :]
            ar1 = at_s[pl.ds(2 * k + 1 + hoff, 1), :]
            seg0 = jnp.exp(jnp.where(tri, ac0 - ar0, -jnp.inf))
            seg1 = jnp.exp(jnp.where(tri, ac1 - ar1, -jnp.inf))
            segt0 = jnp.exp(jnp.where(tri_t, ar0 - ac0, -jnp.inf))
            segt1 = jnp.exp(jnp.where(tri_t, ar1 - ac1, -jnp.inf))
            dyp = dy_ref[:, pl.ds(col, LANE)].astype(MXU)
            zero = jnp.zeros_like(dyp)
            dy0, dy1 = jnp.where(lo, dyp, zero), jnp.where(lo, zero, dyp)
            dht = dstate_ref[k]
            dhb = dht.astype(MXU)
            ht = hin_ref[0, k]
            mt = jnp.concatenate([(cbt * segt0).astype(MXU), (cbt * segt1).astype(MXU)], axis=1)
            bdh = _dot(bg, dhb)
            dec_p = jnp.exp(aexp_s[pl.ds(last, 1), pl.ds(col, LANE)] - ap)
            dxdt_s[:, pl.ds(col, LANE)] = _dot(mt, jnp.concatenate([dy0, dy1], axis=0)) + dec_p * bdh
            z3_s[:, pl.ds(col, LANE)] = bdh
            cdec = jnp.exp(aexp_s[pl.ds(last, 1), pl.ds(col, LANE)])
            dydp = dyd_s[:, pl.ds(col, LANE)]
            dstate_ref[k] = _dot(cg, dydp, TN) + cdec * dht
            cdrow_s[0:1, pl.ds(col, LANE)] = cdec * jnp.sum(dht * ht, axis=0, keepdims=True)
            z1_s[:, pl.ds(col, LANE)] = _dot(cg, ht.astype(MXU))
            dcg = _dot(dydp, ht.astype(MXU), NT)
            dbg = _dot(xdec_s[:, pl.ds(col, LANE)], dhb, NT)
            xp = xdt_s[:, pl.ds(col, LANE)]
            dg0 = _dot(dy0, xp, NT)
            dg1 = _dot(dy1, xp, NT)
            ds0, ds1 = dg0 * seg0, dg1 * seg1
            w0, w1 = ds0 * cb, ds1 * cb
            dcb = ds0 + ds1
            lane_h = lax.broadcasted_iota(jnp.int32, (BLK, LANE), 1)
            rmat_s[...] += (jnp.where(lane_h == 2 * k + hoff, jnp.sum(w0, axis=1, keepdims=True), 0.0)
                            + jnp.where(lane_h == 2 * k + 1 + hoff, jnp.sum(w1, axis=1, keepdims=True), 0.0))
            cst_s[pl.ds(2 * k + hoff, 1), :] = jnp.sum(w0, axis=0, keepdims=True)
            cst_s[pl.ds(2 * k + 1 + hoff, 1), :] = jnp.sum(w1, axis=0, keepdims=True)
            dcbb = dcb.astype(MXU)
            dxbc_ref[:, pl.ds(DI + GN + gcol, N)] += dcg + _dot(dcbb, bg)
            dxbc_ref[:, pl.ds(DI + gcol, N)] += dbg + _dot(dcbb, cg, TN)
            return carry

        cst_s[...] = jnp.zeros(cst_s.shape, F32)
        lax.fori_loop(0, n_pairs, pair, 0, unroll=8)

        etv = et_ref[...]
        dxdt = dxdt_s[...]
        qfull = xdt * decend * z3_s[...]
        tot8 = jnp.concatenate([jnp.sum(qfull, axis=0, keepdims=True) + cdrow_s[0:1, :], jnp.zeros((7, DI), F32)], axis=0)
        z1q, z2, tot = _dot2_stacked([dyv * (z1_s[...] * jnp.exp(aexp)) - qfull, dxdt * xsv, tot8], etv)
        dacs = rmat_s[...] - cst_s[...].T + z1q
        rowi = lax.broadcasted_iota(jnp.int32, (BLK, LANE), 0)
        dacs = dacs + jnp.where(rowi == last, tot[0:1, :], 0.0)
        d_a = _dot3_rhs(tri_t.astype(BF16), dacs)
        ddt = a * d_a + z2
        x_raw = dt_ref[...] + dtb_ref[...]
        ddt_raw = ddt * jax.nn.sigmoid(x_raw)
        lane_l = lax.broadcasted_iota(jnp.int32, (BLK, LANE), 1)
        mine = (lane_l >= hoff) & (lane_l < hoff + n_heads)
        ddt_raw = jnp.where(mine, ddt_raw, 0.0)
        part = jnp.concatenate([jnp.sum(jnp.where(mine, dt * d_a, 0.0), axis=0, keepdims=True) * a,
                                jnp.sum(ddt_raw, axis=0, keepdims=True), jnp.zeros((6, LANE), F32)], axis=0)
        dxs = dxdt * dtexp
        if has_acc:
            dxbc_ref[:, 0:DI] = dxs + dxbc_in[:, 0:DI]
            dxbc_ref[:, DI:] += dxbc_in[:, DI:]
            ddt_ref[...] = ddt_raw + ddt_in[...]
            part = part + jnp.where(s == 0, sums_in[...], 0.0)
        else:
            dxbc_ref[:, 0:DI] = dxs + dyv * dexp_ref[...]
            ddt_ref[...] = ddt_raw
        _acc_first(sums_ref, part, s)

    order = lambda s: _chunk_order(n_chunks - 1 - s, n_chunks, n_ctx, rev)
    in_specs = [pl.BlockSpec((BLK, DI), lambda s: (order(s), 0)),
                pl.BlockSpec((BLK, GN), lambda s: (order(s), DI // GN)),
                pl.BlockSpec((BLK, GN), lambda s: (order(s), DI // GN + 1)),
                pl.BlockSpec((BLK, LANE), lambda s: (order(s), dt_blk)),
                pl.BlockSpec((1, LANE), lambda s: (0, 0)), pl.BlockSpec((1, LANE), lambda s: (0, 0)),
                pl.BlockSpec((2 * LANE, DI), lambda s: (0, 0)), pl.BlockSpec((DI, LANE), lambda s: (0, 0)),
                pl.BlockSpec((1, DI), lambda s: (0, 0)),
                pl.BlockSpec((1, n_pairs, N, LANE), lambda s: (order(s), 0, 0, 0)),
                pl.BlockSpec((BLK, DI), lambda s: (order(s), 0))]
    args = [xbc, xbc, xbc, zx, dtb, alog, jnp.concatenate([emat, emat], axis=0), emat_t, dexp, hin, dy]
    if has_acc:
        in_specs += [pl.BlockSpec((BLK, C), lambda s: (order(s), 0)), pl.BlockSpec((BLK, LANE), lambda s: (order(s), 0)),
                     pl.BlockSpec((8, LANE), lambda s: (0, 0))]
        args += list(acc)
    return pl.pallas_call(
        body, name=name, grid=(n_chunks,),
        in_specs=in_specs,
        out_specs=[pl.BlockSpec((BLK, C), lambda s: (order(s), 0)), pl.BlockSpec((BLK, LANE), lambda s: (order(s), 0)),
                   pl.BlockSpec((8, LANE), lambda s: (0, 0))],
        out_shape=[jax.ShapeDtypeStruct((T, C), F32), jax.ShapeDtypeStruct((T, LANE), F32), jax.ShapeDtypeStruct((8, LANE), F32)],
        scratch_shapes=[pltpu.VMEM((n_pairs, N, LANE), F32), pltpu.VMEM((BLK, DI), MXU), pltpu.VMEM((BLK, DI), MXU),
                        pltpu.VMEM((BLK, DI), F32), pltpu.VMEM((LANE, BLK), F32), pltpu.VMEM((BLK, DI), MXU),
                        pltpu.VMEM((BLK, DI), F32), pltpu.VMEM((BLK, DI), F32), pltpu.VMEM((BLK, DI), F32),
                        pltpu.VMEM((8, DI), F32), pltpu.VMEM((BLK, LANE), F32), pltpu.VMEM((LANE, BLK), F32)],
        compiler_params=_cparams(("arbitrary",), 36 * _nbytes((BLK, DI), F32)),
    )(*args)


def _ssd_finish(yf, yb, xbc, zx, dexp, nw, name, tr=256):
    T, DI = yf.shape
    tr = min(tr, T)

    def body(yf_ref, yb_ref, xs_ref, z_ref, d_ref, nw_ref, o_ref):
        y = yf_ref[...] + yb_ref[...] + xs_ref[...] * d_ref[...]
        gt = y * _silu(z_ref[...])
        r = lax.rsqrt(jnp.mean(gt * gt, axis=-1, keepdims=True) + NORM_EPS)
        o_ref[...] = ((gt * r) * nw_ref[...]).astype(o_ref.dtype)

    row = pl.BlockSpec((tr, DI), lambda i: (i, 0))
    vec = pl.BlockSpec((1, DI), lambda i: (0, 0))
    return pl.pallas_call(
        body, name=name, grid=(T // tr,), in_specs=[row, row, row, row, vec, vec], out_specs=row,
        out_shape=jax.ShapeDtypeStruct((T, DI), MXU),
        compiler_params=_cparams(("parallel",), 16 * _nbytes((tr, DI), F32)),
    )(yf, yb, xbc, zx, dexp, nw)


def _ssd_finish_bwd(yf, yb, xbc, zx, dexp, nw, do, emat_t, dzx_shape, name, tr=64):
    T, DI = yf.shape
    tr = min(tr, T)
    n_steps = T // tr

    def body(yf_ref, yb_ref, xs_ref, z_ref, d_ref, nw_ref, do_ref, et_ref, dy_ref, dz_ref, sums_ref, dd_ref):
        i = pl.program_id(0)
        xs = xs_ref[...]
        zv = z_ref[...]
        y = yf_ref[...] + yb_ref[...] + xs * d_ref[...]
        sz = _silu(zv)
        gt = y * sz
        r = lax.rsqrt(jnp.mean(gt * gt, axis=-1, keepdims=True) + NORM_EPS)
        gn = gt * r
        dov = do_ref[...].astype(F32)
        dgn = dov * nw_ref[...]
        dgt = r * (dgn - gn * jnp.mean(dgn * gn, axis=-1, keepdims=True))
        dy = dgt * sz
        dy_ref[...] = dy
        dz_ref[...] = (dgt * y * _dsilu(zv)).astype(dz_ref.dtype)
        part = jnp.concatenate([jnp.sum(dov * gn, axis=0, keepdims=True), jnp.sum(dy * xs, axis=0, keepdims=True),
                                jnp.zeros((6, DI), F32)], axis=0)
        _acc_first(sums_ref, part, i)

        @pl.when(i == n_steps - 1)
        def _():
            dd_ref[...] = _dot3_lhs(sums_ref[...], et_ref[...])

    row = pl.BlockSpec((tr, DI), lambda i: (i, 0))
    vec = pl.BlockSpec((1, DI), lambda i: (0, 0))
    return pl.pallas_call(
        body, name=name, grid=(n_steps,),
        in_specs=[row, row, row, row, vec, vec, row, pl.BlockSpec((DI, LANE), lambda i: (0, 0))],
        out_specs=[row, row, pl.BlockSpec((8, DI), lambda i: (0, 0)), pl.BlockSpec((8, LANE), lambda i: (0, 0))],
        out_shape=[jax.ShapeDtypeStruct((T, DI), F32), jax.ShapeDtypeStruct(dzx_shape, MXU), jax.ShapeDtypeStruct((8, DI), F32),
                   jax.ShapeDtypeStruct((8, LANE), F32)],
        compiler_params=_cparams(("arbitrary",), 40 * _nbytes((tr, DI), F32)),
    )(yf, yb, xbc, zx, dexp, nw, do, emat_t)


def _loss_head(xf, tgt, ctx_rows, name, tr=256):
    T, D = xf.shape
    tr = min(tr, ctx_rows)
    n_ctx = ctx_rows // tr

    def body(x_ref, t_ref, dx_ref, l_ref):
        i = pl.program_id(0)

        @pl.when(i < n_ctx)
        def _():
            dx_ref[...] = jnp.zeros(dx_ref.shape, F32)

        @pl.when(i == 0)
        def _():
            l_ref[...] = jnp.zeros(l_ref.shape, F32)

        @pl.when(i >= n_ctx)
        def _():
            e = x_ref[...] - t_ref[...]
            dx_ref[...] = e * (1.0 / D)
            l_ref[...] += 0.5 * jnp.sum(jnp.mean(e * e, axis=-1, keepdims=True))

    return pl.pallas_call(
        body, name=name, grid=(T // tr,),
        in_specs=[pl.BlockSpec((tr, D), lambda i: (i, 0)), pl.BlockSpec((tr, D), lambda i: (jnp.maximum(i - n_ctx, 0), 0))],
        out_specs=[pl.BlockSpec((tr, D), lambda i: (i, 0)), pl.BlockSpec((8, LANE), lambda i: (0, 0))],
        out_shape=[jax.ShapeDtypeStruct((T, D), F32), jax.ShapeDtypeStruct((8, LANE), F32)],
        compiler_params=_cparams(("arbitrary",), 10 * _nbytes((tr, D), F32)),
    )(xf, tgt)


def _adamw_math(w, g, m, v):
    m2 = ADAM_B1 * m + (1.0 - ADAM_B1) * g
    v2 = ADAM_B2 * v + (1.0 - ADAM_B2) * (g * g)
    m_hat = m2 / (1.0 - ADAM_B1 ** ADAM_STEP)
    v_hat = v2 / (1.0 - ADAM_B2 ** ADAM_STEP)
    delta = -ADAM_LR * (m_hat / (jnp.sqrt(v_hat) + ADAM_EPS) + ADAM_WD * w)
    return delta, m2, v2


def _row_tile(rows, target):
    if rows <= target:
        return rows
    t = target - target % 8
    while rows % t:
        t -= 8
    return t


def _adamw(parts, w, m, v, name, tr=128):
    n, L, rows, cols = parts.shape
    tr = _row_tile(rows, tr)

    def body(p_ref, w_ref, m_ref, v_ref, g_ref, d_ref, m2_ref, v2_ref):
        g = p_ref[0, 0].astype(F32)
        for q in range(1, n):
            g = g + p_ref[q, 0].astype(F32)
        d, m2, v2 = _adamw_math(w_ref[0], g, m_ref[0], v_ref[0])
        g_ref[0], d_ref[0], m2_ref[0], v2_ref[0] = g, d, m2, v2

    blk = pl.BlockSpec((1, tr, cols), lambda l, i: (l, i, 0))
    shp = jax.ShapeDtypeStruct((L, rows, cols), F32)
    return pl.pallas_call(
        body, name=name, grid=(L, rows // tr),
        in_specs=[pl.BlockSpec((n, 1, tr, cols), lambda l, i: (0, l, i, 0)), blk, blk, blk],
        out_specs=[blk, blk, blk, blk], out_shape=[shp, shp, shp, shp],
        compiler_params=_cparams(("parallel", "parallel"), 2 * (n + 8) * _nbytes((tr, cols), F32)),
    )(parts, w, m, v)


def _adamw_small(bufs, w, m, v, name):
    n, R, _ = bufs.shape

    def body(b_ref, w_ref, m_ref, v_ref, g_ref, d_ref, m2_ref, v2_ref):
        g = b_ref[0]
        for q in range(1, n):
            g = g + b_ref[q]
        d, m2, v2 = _adamw_math(w_ref[...], g, m_ref[...], v_ref[...])
        g_ref[...], d_ref[...], m2_ref[...], v2_ref[...] = g, d, m2, v2

    vm = pl.BlockSpec(memory_space=pltpu.VMEM)
    shp = jax.ShapeDtypeStruct((R, LANE), F32)
    return pl.pallas_call(body, name=name, in_specs=[vm, vm, vm, vm], out_specs=[vm, vm, vm, vm],
                          out_shape=[shp, shp, shp, shp],
                          compiler_params=pltpu.CompilerParams(vmem_limit_bytes=32 * 1024 * 1024))(bufs, w, m, v)


def _me():
    return lax.axis_index("x"), lax.axis_index("y"), lax.axis_index("c")


def _flip(v, bit):
    return 1 - v if bit else v


def _peer(k):
    x, y, c = _me()
    return _flip(x, (k >> 2) & 1), _flip(y, (k >> 1) & 1), _flip(c, k & 1)


def _dev_index(p):
    return 4 * p[0] + 2 * p[1] + p[2]


def _small_allgather(v, name):
    R, C = v.shape

    def body(v_ref, out_ref, send_sems, recv_sems, loc_sem):
        me = _dev_index(_me())
        mine = pltpu.make_async_copy(v_ref, out_ref.at[me], loc_sem)
        mine.start()
        sends = []
        for k in range(1, NDEV):
            cp = pltpu.make_async_remote_copy(src_ref=v_ref, dst_ref=out_ref.at[me], send_sem=send_sems.at[k - 1],
                                              recv_sem=recv_sems.at[k - 1], device_id=_peer(k), device_id_type=MESH_ID)
            cp.start()
            sends.append(cp)
        for k in range(1, NDEV):
            pltpu.make_async_remote_copy(src_ref=v_ref, dst_ref=out_ref.at[_dev_index(_peer(k))], send_sem=send_sems.at[k - 1],
                                         recv_sem=recv_sems.at[k - 1], device_id=_peer(k), device_id_type=MESH_ID).wait_recv()
        for cp in sends:
            cp.wait_send()
        mine.wait()

    vm = pl.BlockSpec(memory_space=pltpu.VMEM)
    return pl.pallas_call(
        body, name=name, in_specs=[vm], out_specs=vm, out_shape=jax.ShapeDtypeStruct((NDEV, R, C), F32),
        scratch_shapes=[pltpu.SemaphoreType.DMA((NDEV - 1,)), pltpu.SemaphoreType.DMA((NDEV - 1,)), pltpu.SemaphoreType.DMA(())],
        compiler_params=pltpu.CompilerParams(vmem_limit_bytes=48 * 1024 * 1024),
    )(v)


HBM_SPEC = pl.BlockSpec(memory_space=pltpu.HBM)
SEM_SPEC = pl.BlockSpec(memory_space=pltpu.SEMAPHORE)
DATAFLOW = pltpu.SideEffectType.DATAFLOW_SIDE_EFFECTING


def _xfer_copy(src_ref, land_ref, sems, a, k, layer, scatter, arriving):
    send_sems, recv_sems = sems
    me, peer = _dev_index(_me()), _dev_index(_peer(k))
    src = src_ref.at[peer] if scatter else src_ref
    slot = peer if arriving else me
    dst = land_ref.at[slot] if layer is None else land_ref.at[slot, layer]
    return pltpu.make_async_remote_copy(src_ref=src, dst_ref=dst, send_sem=send_sems.at[a * (NDEV - 1) + k - 1],
                                        recv_sem=recv_sems.at[a * (NDEV - 1) + k - 1], device_id=_peer(k),
                                        device_id_type=MESH_ID)


def _xfer_start(srcs, lands, layers, scatter, name):
    n = len(srcs)

    def body(*refs):
        src_refs, land_refs = refs[:n], refs[n:2 * n]
        sems = refs[2 * n], refs[2 * n + 1]
        token = refs[-1]
        for a in range(n):
            for k in range(1, NDEV):
                _xfer_copy(src_refs[a], land_refs[a], sems, a, k, layers[a], scatter, False).start()
        token[...] = jnp.zeros(token.shape, token.dtype)

    ops = [pltpu.with_memory_space_constraint(t, pltpu.HBM) for t in list(srcs) + list(lands)]
    n_sem = n * (NDEV - 1)
    res = pl.pallas_call(
        body, name=name,
        out_shape=(pltpu.SemaphoreType.DMA((n_sem,)), pltpu.SemaphoreType.DMA((n_sem,)),
                   *[pltpu.HBM(t.shape, t.dtype) for t in ops], jax.ShapeDtypeStruct((8, LANE), F32)),
        in_specs=[HBM_SPEC] * (2 * n),
        out_specs=(SEM_SPEC, SEM_SPEC, *[HBM_SPEC] * (2 * n), pl.BlockSpec(memory_space=pltpu.VMEM)),
        input_output_aliases={i: 2 + i for i in range(2 * n)},
        compiler_params=pltpu.CompilerParams(has_side_effects=DATAFLOW),
    )(*ops)
    return res[0], res[1], list(res[2:2 + n]), list(res[2 + n:2 + 2 * n]), res[-1]


def _xfer_wait(send_sems, recv_sems, srcs, lands, layers, scatter, after, name):
    n = len(srcs)

    def body(*refs):
        src_refs, land_refs = refs[:n], refs[n:2 * n]
        sems = refs[2 * n], refs[2 * n + 1]
        for a in range(n):
            for k in range(1, NDEV):
                _xfer_copy(src_refs[a], land_refs[a], sems, a, k, layers[a], scatter, False).wait_send()
                _xfer_copy(src_refs[a], land_refs[a], sems, a, k, layers[a], scatter, True).wait_recv()

    ops = list(srcs) + list(lands)
    res = pl.pallas_call(
        body, name=name,
        out_shape=tuple(pltpu.HBM(t.shape, t.dtype) for t in ops),
        in_specs=[HBM_SPEC] * (2 * n) + [SEM_SPEC, SEM_SPEC, pl.BlockSpec(memory_space=pl.ANY)],
        out_specs=tuple([HBM_SPEC] * (2 * n)),
        input_output_aliases={i: i for i in range(2 * n)},
        compiler_params=pltpu.CompilerParams(has_side_effects=DATAFLOW),
    )(*ops, send_sems, recv_sems, after)
    return list(res[n:])


HI = lax.Precision.HIGHEST
MOD_ROWS = 16


def _col_tile(n, target=512):
    return target if n % target == 0 else n


def _modulation(s_in, ada_w, b_loc, name):
    L, D, nl = ada_w.shape
    tn = _col_tile(nl)

    def body(s_ref, w_ref, b_ref, o_ref):
        o_ref[0] = jnp.dot(_silu(s_ref[...]), w_ref[0], preferred_element_type=F32, precision=HI) + b_ref[0]

    return pl.pallas_call(
        body, name=name, grid=(L, nl // tn),
        in_specs=[pl.BlockSpec((MOD_ROWS, D), lambda l, j: (0, 0)), pl.BlockSpec((1, D, tn), lambda l, j: (l, 0, j)),
                  pl.BlockSpec((1, 1, tn), lambda l, j: (l, 0, j))],
        out_specs=pl.BlockSpec((1, MOD_ROWS, tn), lambda l, j: (l, 0, j)),
        out_shape=jax.ShapeDtypeStruct((L, MOD_ROWS, nl), F32),
        compiler_params=_cparams(("parallel", "parallel"), 4 * _nbytes((D, tn), F32)),
    )(s_in, ada_w, b_loc)


def _modulation_bwd(s_in, dml, dmc, ada_w, name):
    L, D, nl = ada_w.shape
    tn = _col_tile(nl)

    def body(s_ref, dml_ref, dmc_ref, w_ref, g_ref, pc_ref):
        l, j = pl.program_id(0), pl.program_id(1)
        a = _silu(s_ref[...])
        tot = dmc_ref[0, 0]
        for d in range(1, NDEV):
            tot = tot + dmc_ref[d, 0]
        row = lax.broadcasted_iota(jnp.int32, (MOD_ROWS, tn), 0)
        dm = jnp.where(row == NDEV, tot, dml_ref[:, 0, 0, :])
        g_ref[0] = lax.dot_general(a, dm, TN, preferred_element_type=F32, precision=HI)
        tot8 = jnp.where(lax.broadcasted_iota(jnp.int32, (8, tn), 0) == 0, tot, 0.0)
        part = lax.dot_general(tot8, w_ref[0], NT, preferred_element_type=F32, precision=HI)

        @pl.when((l == 0) & (j == 0))
        def _():
            pc_ref[...] = part

        @pl.when((l != 0) | (j != 0))
        def _():
            pc_ref[...] += part

    return pl.pallas_call(
        body, name=name, grid=(L, nl // tn),
        in_specs=[pl.BlockSpec((MOD_ROWS, D), lambda l, j: (0, 0)), pl.BlockSpec((MOD_ROWS, 1, 1, tn), lambda l, j: (0, l, 0, j)),
                  pl.BlockSpec((NDEV, 1, 1, tn), lambda l, j: (0, l, 0, j)), pl.BlockSpec((1, D, tn), lambda l, j: (l, 0, j))],
        out_specs=[pl.BlockSpec((1, D, tn), lambda l, j: (l, 0, j)), pl.BlockSpec((8, D), lambda l, j: (0, 0))],
        out_shape=[jax.ShapeDtypeStruct((L, D, nl), F32), jax.ShapeDtypeStruct((8, D), F32)],
        compiler_params=_cparams(("arbitrary", "arbitrary"), 8 * _nbytes((D, tn), F32)),
    )(s_in, dml.reshape(MOD_ROWS, L, 1, nl), dmc.reshape(NDEV, L, 1, nl), ada_w)


def _cctx_update(bufs, c_ctx, m, v, name):
    n, R, _ = bufs.shape

    def body(b_ref, w_ref, m_ref, v_ref, g_ref, d_ref, m2_ref, v2_ref):
        g = b_ref[0]
        for q in range(1, n):
            g = g + b_ref[q]
        g = g * _dsilu(w_ref[...])
        d, m2, v2 = _adamw_math(w_ref[...], g, m_ref[...], v_ref[...])
        g_ref[...], d_ref[...], m2_ref[...], v2_ref[...] = g, d, m2, v2

    vm = pl.BlockSpec(memory_space=pltpu.VMEM)
    shp = jax.ShapeDtypeStruct((R, LANE), F32)
    return pl.pallas_call(body, name=name, in_specs=[vm, vm, vm, vm], out_specs=[vm, vm, vm, vm],
                          out_shape=[shp, shp, shp, shp])(bufs, c_ctx, m, v)


def _pack(arrs):
    flat = jnp.concatenate([a.reshape(-1).astype(F32) for a in arrs])
    n = flat.shape[0]
    total = -(-n // (8 * LANE)) * (8 * LANE)
    return jnp.pad(flat, (0, total - n)).reshape(total // LANE, LANE)


def _unpack(buf, shapes):
    lead = buf.shape[:-2]
    flat = buf.reshape(lead + (-1,))
    out, off = [], 0
    for s in shapes:
        n = int(np.prod(s))
        out.append(flat[..., off:off + n].reshape(lead + tuple(s)))
        off += n
    return out


WEIGHTS = ['c_ctx', 'ada_w', 'ada_b', 'norm1_w', 'norm2_w', 'ssd_w_in', 'ssd_conv_w', 'ssd_conv_b', 'ssd_dt_bias_f',
           'ssd_dt_bias_b', 'ssd_a_log_f', 'ssd_a_log_b', 'ssd_d', 'ssd_norm_w', 'ssd_w_out', 'attn_w_qkv', 'attn_q_gain',
           'attn_k_gain', 'attn_sinks', 'attn_w_o', 'ffn_w_up', 'ffn_conv_w', 'ffn_conv_b', 'ffn_w_down']
SMALL = ['ada_b', 'norm1_w', 'norm2_w', 'ssd_conv_b', 'ssd_dt_bias_f', 'ssd_dt_bias_b', 'ssd_a_log_f', 'ssd_a_log_b', 'ssd_d',
         'ssd_norm_w', 'attn_q_gain', 'attn_k_gain', 'attn_sinks', 'ffn_conv_b']
BIG = ['ssd_w_in', 'ssd_w_out', 'attn_w_qkv', 'attn_w_o', 'ffn_w_up', 'ffn_w_down']


def _step(x, c, ctx, w, tgt, m, v):
    xi, yi, ci = _me()
    me = 4 * xi + 2 * yi + ci
    t_lat, D = x.shape[1], x.shape[2]
    ctx_rows = ctx.shape[1]
    T = ctx_rows + t_lat
    L, n_ssd, n_att = w['norm1_w'].shape[0], w['ssd_d'].shape[0], w['attn_sinks'].shape[0]
    H, DI, XBC = w['ssd_d'].shape[1], w['ssd_norm_w'].shape[1], w['ssd_conv_b'].shape[1]
    P = DI // H
    IN = w['ssd_w_in'].shape[2] * NDEV
    hd, n_q = w['attn_q_gain'].shape[1], w['attn_sinks'].shape[1]
    F2 = w['ffn_conv_b'].shape[1]
    G = F2 // NDEV
    nl = w['ada_w'].shape[2]
    ncc = ctx_rows // BLK
    perm = [_ffn_perm(s) for s in range(NDEV)]
    inv = [perm.index(d) for d in range(NDEV)]

    def reorder(t, order):
        return jnp.concatenate([t[..., o * G:(o + 1) * G] for o in order], axis=-1)

    def interleave(t):
        return reorder(t, perm)

    def deinterleave(t):
        return reorder(t, inv)

    def layer_weights(i):
        mixer = ['ssd_w_in', 'ssd_w_out'] if i % 2 == 0 else ['attn_w_qkv', 'attn_w_o']
        return [(n, i // 2) for n in mixer] + [('ffn_w_up', i), ('ffn_w_down', i)]

    def gather_start(keys, tag, zero):
        srcs = [(w[n][j] + zero).astype(MXU) for n, j in keys]
        lands = [lax.dynamic_update_index_in_dim(lax.empty((NDEV,) + t.shape, t.dtype), t, me, 0) for t in srcs]
        ss, rs, srcs, lands, tok = _xfer_start(srcs, lands, [None] * len(srcs), False, f"gather_start_{tag}")
        return (ss, rs, srcs, lands, tag), tok[0, 0]

    def gather_wait(p, after):
        return _xfer_wait(*p[:4], [None] * len(p[2]), False, after, f"gather_wait_{p[4]}")

    first, tok = gather_start(layer_weights(0)[:1], "0a", jnp.zeros((), F32))
    pending, tok = gather_start(layer_weights(0)[1:], "0b", tok)

    shapes_a = [(D,), w['ssd_conv_w'].shape, w['ffn_conv_w'].shape]
    g_a = _small_allgather(_pack([c[0], w['ssd_conv_w'], w['ffn_conv_w']]), "gather_cond")
    c_all, scw_all, fcw_all = _unpack(g_a, shapes_a)
    ssd_cw = scw_all.transpose(1, 2, 0, 3).reshape(n_ssd, 3, XBC)
    ffn_cw = jnp.concatenate([fcw_all[d] for d in perm], axis=-1)
    ffn_cb = interleave(w['ffn_conv_b'])[:, None, :]

    s_in = jnp.concatenate([c_all, w['c_ctx'][None], jnp.zeros((MOD_ROWS - NDEV - 1, D), F32)], axis=0)
    b_loc = lax.dynamic_slice(w['ada_b'], (0, me * nl), (L, nl))[:, None, :]
    mod_loc = _modulation(s_in, w['ada_w'], b_loc, "modulation")
    g_b = _small_allgather(_pack([mod_loc]), "gather_mod")
    (mod_all,) = _unpack(g_b, [mod_loc.shape])
    mod_lat = lax.dynamic_index_in_dim(mod_all, me, axis=2, keepdims=False)
    mod_ctx = mod_all[:, :, NDEV, :]
    to_mod = lambda t: t.transpose(1, 0, 2).reshape(L, 6, D)
    mod = jnp.stack([to_mod(mod_ctx), to_mod(mod_lat)], axis=1)

    w_in, w_out, w_qkv, w_o = [None] * n_ssd, [None] * n_ssd, [None] * n_att, [None] * n_att
    w_up, w_down = [None] * L, [None] * L

    cos, sin = _rope_tables(t_lat, ctx_rows, hd)
    e_f, e_b = _expand_matrix(H, P, False), _expand_matrix(H, P, True)
    et_f, et_b = e_f.T, e_b.T

    xs = jnp.concatenate([ctx[0], x[0]], axis=0)
    saved = []
    for i in range(L):
        j = i // 2
        s = dict(x0=xs)
        zero_of = lambda t: jnp.minimum(jnp.abs(t[0, 0, 0].astype(F32)), 0.0)
        if i == 0:
            w_in[0] = _cat_cols(gather_wait(first, mod)[0], "ssd_in_cat")
        else:
            got = gather_wait(pending, xs)
            if i % 2 == 0:
                w_in[j], w_out[j] = _cat_cols(got[0], "ssd_in_cat"), got[1].reshape(DI, D)
            else:
                w_qkv[j], w_o[j] = got[0], got[1].reshape(n_q * hd, D)
            w_up[i], w_down[i] = got[2], got[3].reshape(F2 // 2, D)
            if i + 1 < L:
                pending, tok = gather_start(layer_weights(i + 1), f"{i + 1}", zero_of(got[1]))
        nw1, nw2 = w['norm1_w'][i][None] + tok, w['norm2_w'][i][None]
        s['h1'] = _normmod(xs, nw1, mod[i], 0, ctx_rows, "normmod")
        if i % 2 == 0:
            s['zx'] = _mm(s['h1'], w_in[j], tm=768, tn=1152, tk=2048, name="mm_ssd_in")
            s['cw'], s['cb'] = ssd_cw[j], w['ssd_conv_b'][j][None]
            s['xbc'] = _ssd_conv(s['zx'], s['cw'], s['cb'], DI, ctx_rows, "ssd_conv")
            s['dtb'] = jnp.concatenate([w['ssd_dt_bias_f'][j], w['ssd_dt_bias_b'][j]])[None]
            s['alog'] = jnp.concatenate([w['ssd_a_log_f'][j], w['ssd_a_log_b'][j]])[None]
            s['yf'], s['hin_f'] = _ssd_scan(s['xbc'], s['zx'], s['dtb'], s['alog'], e_f, H, ncc, False, "ssd_scan_f")
            s['yb'], s['hin_b'] = _ssd_scan(s['xbc'], s['zx'], s['dtb'], s['alog'], e_b, H, ncc, True, "ssd_scan_b")
            s['dexp'], s['snw'] = jnp.repeat(w['ssd_d'][j], P)[None], w['ssd_norm_w'][j][None]
            if i == 0:
                got = gather_wait(pending, s['yb'])
                w_out[0], w_up[0], w_down[0] = got[0].reshape(DI, D), got[1], got[2].reshape(F2 // 2, D)
                if L > 1:
                    pending, tok = gather_start(layer_weights(1), "1", zero_of(got[0]))
                    s['snw'] = s['snw'] + tok
            s['o'] = _ssd_finish(s['yf'], s['yb'], s['xbc'], s['zx'], s['dexp'], s['snw'], "ssd_finish")
            s['mix'], x1 = _mm(s['o'], w_out[j], tm=768, tn=512, tk=4096, name="mm_ssd_out",
                               resid=xs, gate=mod[i][:, 2], ctx_rows=ctx_rows)
        else:
            s['qkv'] = _mm(s['h1'], w_qkv[j], tm=768, tn=384, tk=2048, name="mm_qkv", bslots=_ident)
            s['qg'], s['kg'] = w['attn_q_gain'][j][None], w['attn_k_gain'][j][None]
            s['qr'], s['kr'], s['vb'] = _qk_prep(s['qkv'], s['qg'], s['kg'], cos, sin, n_q, ctx_rows, "qk_prep")
            s['o'] = _attn_fwd(s['qr'], s['kr'], s['vb'], w['attn_sinks'][j], n_q, ctx_rows, "attn_fwd")
            s['mix'], x1 = _mm(s['o'], w_o[j], tm=768, tn=1024, tk=2048, name="mm_attn_out",
                               resid=xs, gate=mod[i][:, 2], ctx_rows=ctx_rows)
        s['x1'] = x1
        s['h2'] = _normmod(x1, nw2, mod[i], 1, ctx_rows, "normmod")
        s['u'] = _mm(s['h2'], w_up[i], tm=768, tn=1408, tk=2048, name="mm_ffn_up", bslots=_ffn_perm)
        s['a'] = _ffn_mid(s['u'], ffn_cw[i], ffn_cb[i], ctx_rows, "ffn_mid")
        s['f'], xs = _mm(s['a'], w_down[i], tm=768, tn=512, tk=5632, name="mm_ffn_down",
                         resid=x1, gate=mod[i][:, 5], ctx_rows=ctx_rows)
        saved.append(s)

    dx, lacc = _loss_head(xs, tgt[0], ctx_rows, "loss_head")
    loss = lax.psum(lacc[0, 0], ("x", "y", "c"))

    gbig = {name: [None] * w[name].shape[0] for name in BIG}
    gs = {name: [None] * w[name].shape[0] for name in SMALL + ['ssd_conv_w', 'ffn_conv_w']}
    dmod = [None] * L
    lands = {n: lax.empty((NDEV,) + w[n].shape, XFER) for n in BIG}
    pend_early, pend_late, tok = None, None, jnp.zeros((), F32)

    def scatter_start(keys, tag):
        srcs = [gbig[n][j] for n, j in keys]
        for (n, j), g in zip(keys, srcs):
            own = lax.dynamic_index_in_dim(g, me, 0, keepdims=False)
            lands[n] = lax.dynamic_update_slice(lands[n], own[None, None], (me, j, 0, 0))
        ss, rs, srcs, got, t = _xfer_start(srcs, [lands[n] for n, _ in keys], [j for _, j in keys], True, f"scatter_start_{tag}")
        return (ss, rs, srcs, got, keys, tag), t

    def scatter_wait(p, after):
        ss, rs, srcs, got, keys, tag = p
        got = _xfer_wait(ss, rs, srcs, got, [j for _, j in keys], True, after, f"scatter_wait_{tag}")
        for (n, _), t in zip(keys, got):
            lands[n] = t

    for i in reversed(range(L)):
        j = i // 2
        s = saved[i]
        nw1, nw2 = w['norm1_w'][i][None], w['norm2_w'][i][None]
        dm2, dg2 = _gate_bwd(dx, s['f'], mod[i], 1, ctx_rows, "gate_bwd")
        da = _mm(dm2, w_down[i], tb=True, out_dtype=MXU, tm=768, tn=1408, tk=2048, name="mm_ffn_down_dx")
        gbig['ffn_w_down'][i] = _mm(s['a'], dm2, ta=True, out_dtype=XFER, tm=512, tn=1024, tk=4224,
                                    name="mm_ffn_down_dw").reshape(NDEV, -1, D)
        du, gcw = _ffn_mid_bwd(s['u'], da, ffn_cw[i], ffn_cb[i], ctx_rows, "ffn_mid_bwd")
        gcw = deinterleave(gcw)
        gs['ffn_conv_w'][i], gs['ffn_conv_b'][i] = gcw[0:3], gcw[3]
        dh2 = _mm(du, w_up[i], tb=True, out_dtype=MXU, tm=768, tn=1024, tk=1408, name="mm_ffn_up_dx", bslots=_ffn_perm)
        gbig['ffn_w_up'][i] = _mm(s['h2'], du, ta=True, out_dtype=XFER, tm=512, tn=1408, tk=4224, name="mm_ffn_up_dw",
                                  oslots=_ffn_perm)
        dx1, sums2 = _normmod_bwd(s['x1'], nw2, mod[i], dh2, dx, 1, ctx_rows, "normmod_bwd")
        dmix, dg1 = _gate_bwd(dx1, s['mix'], mod[i], 0, ctx_rows, "gate_bwd")

        def send_early(after):
            if pend_early is not None:
                scatter_wait(pend_early, after)
            return scatter_start(layer_weights(i)[1:], f"early_{i}")

        def send_late(after):
            if pend_late is not None:
                scatter_wait(pend_late, after)
            return scatter_start(layer_weights(i)[:1], f"late_{i}")

        if i % 2 == 0:
            do = _mm(dmix, w_out[j], tb=True, out_dtype=MXU, tm=768, tn=1024, tk=2048, name="mm_ssd_out_dx")
            gbig['ssd_w_out'][j] = _mm(s['o'], dmix, ta=True, out_dtype=XFER, tm=512, tn=1024, tk=4224,
                                       name="mm_ssd_out_dw").reshape(NDEV, -1, D)
            pend_early, tok = send_early(do)
            dy, dzx, fs, dd = _ssd_finish_bwd(s['yf'], s['yb'], s['xbc'], s['zx'], s['dexp'], s['snw'] + tok[0, 0], do, et_f,
                                              (T, IN), "ssd_finish_bwd")
            acc = _ssd_scan_bwd(s['xbc'], s['zx'], s['dtb'], s['alog'], e_f, et_f, s['dexp'], s['hin_f'], dy, None,
                                H, ncc, False, "ssd_scan_bwd_f")
            dxbc, ddt, ssm = _ssd_scan_bwd(s['xbc'], s['zx'], s['dtb'], s['alog'], e_b, et_b, s['dexp'], s['hin_b'], dy, acc,
                                           H, ncc, True, "ssd_scan_bwd_b")
            dzx, gscw = _ssd_conv_bwd(s['zx'], dxbc, dzx, s['cw'], s['cb'], DI, ctx_rows, "ssd_conv_bwd")
            dzx = _put_cols(dzx, ddt, IN // LANE - 1, "ssd_put_ddt")
            dwi = _mm(s['h1'], dzx, ta=True, tm=512, tn=1152, tk=4224, name="mm_ssd_in_dw")
            gbig['ssd_w_in'][j] = _split_cols(dwi, NDEV, XFER, "ssd_in_split")
            pend_late, tok = send_late(gbig['ssd_w_in'][j])
            dh1 = _mm(dzx, w_in[j], tb=True, out_dtype=MXU, tm=768, tn=1024, tk=3456, name="mm_ssd_in_dx", after=tok)
            gs['ssd_conv_w'][j], gs['ssd_conv_b'][j] = gscw[0:3], gscw[3]
            gs['ssd_dt_bias_f'][j], gs['ssd_dt_bias_b'][j] = ssm[1, :H], ssm[1, H:]
            gs['ssd_a_log_f'][j], gs['ssd_a_log_b'][j] = ssm[0, :H], ssm[0, H:]
            gs['ssd_d'][j], gs['ssd_norm_w'][j] = dd[1, :H], fs[0]
        else:
            do = _mm(dmix, w_o[j], tb=True, out_dtype=MXU, tm=768, tn=1024, tk=2048, name="mm_attn_out_dx")
            gbig['attn_w_o'][j] = _mm(s['o'], dmix, ta=True, out_dtype=XFER, tm=512, tn=1024, tk=4224,
                                      name="mm_attn_out_dw").reshape(NDEV, -1, D)
            pend_early, tok = send_early(do)
            dq, dk, dv, dsk = _attn_bwd(s['qr'], s['kr'], s['vb'], w['attn_sinks'][j] + tok[0, 0], do, n_q, ctx_rows, "attn_bwd")
            dqkv, gg = _qk_prep_bwd(s['qkv'], s['qg'], s['kg'], cos, sin, dq, dk, dv, n_q, ctx_rows, "qk_prep_bwd")
            gbig['attn_w_qkv'][j] = _mm(s['h1'], dqkv, ta=True, out_dtype=XFER, tm=1024, tn=384, tk=2112, name="mm_qkv_dw",
                                        oslots=_ident)
            pend_late, tok = send_late(gbig['attn_w_qkv'][j])
            dh1 = _mm(dqkv, w_qkv[j], tb=True, out_dtype=MXU, tm=768, tn=1024, tk=384, name="mm_qkv_dx", bslots=_ident,
                      after=tok)
            gs['attn_q_gain'][j], gs['attn_k_gain'][j] = gg[0], gg[1]
            gs['attn_sinks'][j] = dsk.reshape(ATTN_KV_HEADS, 8, LANE)[:, 0, :n_q // ATTN_KV_HEADS].reshape(n_q)
        dx, sums1 = _normmod_bwd(s['x0'], nw1, mod[i], dh1, dx1, 0, ctx_rows, "normmod_bwd")
        gs['norm1_w'][i], gs['norm2_w'][i] = sums1[0, 2] + sums1[1, 2], sums2[0, 2] + sums2[1, 2]
        dmod[i] = jnp.stack([sums1[:, 0], sums1[:, 1], dg1[:, 0], sums2[:, 0], sums2[:, 1], dg2[:, 0]], axis=1)
    grad_x = dx[ctx_rows:][None]
    dmod = jnp.stack(dmod)
    dmod_ctx, dmod_lat = dmod[:, 0].reshape(L, 6 * D), dmod[:, 1].reshape(L, 6 * D)
    gs['ada_b'] = dmod_ctx + dmod_lat

    out = {}

    small_g = [jnp.stack(gs[n]) if isinstance(gs[n], list) else gs[n] for n in SMALL]
    extras = [jnp.stack(gs['ssd_conv_w']), jnp.stack(gs['ffn_conv_w'])]
    shapes_c = [w[n].shape for n in SMALL] + [e.shape for e in extras]
    g_c = _small_allgather(_pack(small_g + extras), "gather_small")
    zeros = [jnp.zeros(e.shape, F32) for e in extras]
    res = _adamw_small(g_c, _pack([w[n] for n in SMALL] + zeros), _pack([m[n] for n in SMALL] + zeros),
                       _pack([v[n] for n in SMALL] + zeros), "adamw_small")
    res = [_unpack(r, shapes_c) for r in res]
    for k, n in enumerate(SMALL):
        out[n] = tuple(r[k] for r in res)
    g_scw, g_fcw = res[0][len(SMALL)], res[0][len(SMALL) + 1]
    g_scw = lax.dynamic_index_in_dim(g_scw.reshape(n_ssd, 3, NDEV, XBC // NDEV), me, axis=2, keepdims=False)
    g_fcw = lax.dynamic_index_in_dim(g_fcw.reshape(L, 3, NDEV, G), me, axis=2, keepdims=False)
    conv = ['ssd_conv_w', 'ffn_conv_w']
    res = _adamw_small(_pack([g_scw, g_fcw])[None], _pack([w[n] for n in conv]), _pack([m[n] for n in conv]),
                       _pack([v[n] for n in conv]), "adamw_conv")
    res = [_unpack(r, [w[n].shape for n in conv]) for r in res]
    for k, n in enumerate(conv):
        out[n] = tuple(r[k] for r in res)

    g_m = _small_allgather(jnp.concatenate([dmod_lat, dmod_ctx], axis=0), "gather_dmod")
    all_lat, all_ctx = g_m[:, :L], g_m[:, L:]
    my_cols = lambda t: lax.dynamic_slice(t, (0, 0, me * nl), (NDEV, L, nl))
    dml = jnp.concatenate([my_cols(all_lat), jnp.zeros((MOD_ROWS - NDEV, L, nl), F32)], axis=0)
    g_ada, pc = _modulation_bwd(s_in, dml, my_cols(all_ctx), w['ada_w'], "modulation_bwd")
    out['ada_w'] = _adamw(g_ada[None], w['ada_w'], m['ada_w'], v['ada_w'], "adamw")
    g_d = _small_allgather(_pack([pc[0]]), "gather_cctx")
    res = _cctx_update(g_d, _pack([w['c_ctx']]), _pack([m['c_ctx']]), _pack([v['c_ctx']]), "adamw_cctx")
    out['c_ctx'] = tuple(_unpack(r, [(D,)])[0] for r in res)

    c_new = out['c_ctx'][0]
    scatter_wait(pend_early, c_new)
    last = [n for n, _ in layer_weights(0)[:1]]
    for name in [n for n in BIG if n not in last]:
        out[name] = _adamw(lands[name], w[name], m[name], v[name], "adamw")
    scatter_wait(pend_late, out['ffn_w_down'][0])
    for name in last:
        out[name] = _adamw(lands[name], w[name], m[name], v[name], "adamw")

    return (loss, grad_x) + tuple(out[n][k] for k in range(4) for n in WEIGHTS)


def kernel(x, c, ctx, c_ctx, ada_w, ada_b, norm1_w, norm2_w, ssd_w_in, ssd_conv_w, ssd_conv_b, ssd_dt_bias_f, ssd_dt_bias_b, ssd_a_log_f, ssd_a_log_b, ssd_d, ssd_norm_w, ssd_w_out, attn_w_qkv, attn_q_gain, attn_k_gain, attn_sinks, attn_w_o, ffn_w_up, ffn_conv_w, ffn_conv_b, ffn_w_down, loss_target, m_c_ctx, m_ada_w, m_ada_b, m_norm1_w, m_norm2_w, m_ssd_w_in, m_ssd_conv_w, m_ssd_conv_b, m_ssd_dt_bias_f, m_ssd_dt_bias_b, m_ssd_a_log_f, m_ssd_a_log_b, m_ssd_d, m_ssd_norm_w, m_ssd_w_out, m_attn_w_qkv, m_attn_q_gain, m_attn_k_gain, m_attn_sinks, m_attn_w_o, m_ffn_w_up, m_ffn_conv_w, m_ffn_conv_b, m_ffn_w_down, v_c_ctx, v_ada_w, v_ada_b, v_norm1_w, v_norm2_w, v_ssd_w_in, v_ssd_conv_w, v_ssd_conv_b, v_ssd_dt_bias_f, v_ssd_dt_bias_b, v_ssd_a_log_f, v_ssd_a_log_b, v_ssd_d, v_ssd_norm_w, v_ssd_w_out, v_attn_w_qkv, v_attn_q_gain, v_attn_k_gain, v_attn_sinks, v_attn_w_o, v_ffn_w_up, v_ffn_conv_w, v_ffn_conv_b, v_ffn_w_down):
    w = dict(c_ctx=c_ctx, ada_w=ada_w, ada_b=ada_b, norm1_w=norm1_w, norm2_w=norm2_w, ssd_w_in=ssd_w_in, ssd_conv_w=ssd_conv_w, ssd_conv_b=ssd_conv_b, ssd_dt_bias_f=ssd_dt_bias_f, ssd_dt_bias_b=ssd_dt_bias_b, ssd_a_log_f=ssd_a_log_f, ssd_a_log_b=ssd_a_log_b, ssd_d=ssd_d, ssd_norm_w=ssd_norm_w, ssd_w_out=ssd_w_out, attn_w_qkv=attn_w_qkv, attn_q_gain=attn_q_gain, attn_k_gain=attn_k_gain, attn_sinks=attn_sinks, attn_w_o=attn_w_o, ffn_w_up=ffn_w_up, ffn_conv_w=ffn_conv_w, ffn_conv_b=ffn_conv_b, ffn_w_down=ffn_w_down)
    m = dict(c_ctx=m_c_ctx, ada_w=m_ada_w, ada_b=m_ada_b, norm1_w=m_norm1_w, norm2_w=m_norm2_w, ssd_w_in=m_ssd_w_in, ssd_conv_w=m_ssd_conv_w, ssd_conv_b=m_ssd_conv_b, ssd_dt_bias_f=m_ssd_dt_bias_f, ssd_dt_bias_b=m_ssd_dt_bias_b, ssd_a_log_f=m_ssd_a_log_f, ssd_a_log_b=m_ssd_a_log_b, ssd_d=m_ssd_d, ssd_norm_w=m_ssd_norm_w, ssd_w_out=m_ssd_w_out, attn_w_qkv=m_attn_w_qkv, attn_q_gain=m_attn_q_gain, attn_k_gain=m_attn_k_gain, attn_sinks=m_attn_sinks, attn_w_o=m_attn_w_o, ffn_w_up=m_ffn_w_up, ffn_conv_w=m_ffn_conv_w, ffn_conv_b=m_ffn_conv_b, ffn_w_down=m_ffn_w_down)
    v = dict(c_ctx=v_c_ctx, ada_w=v_ada_w, ada_b=v_ada_b, norm1_w=v_norm1_w, norm2_w=v_norm2_w, ssd_w_in=v_ssd_w_in, ssd_conv_w=v_ssd_conv_w, ssd_conv_b=v_ssd_conv_b, ssd_dt_bias_f=v_ssd_dt_bias_f, ssd_dt_bias_b=v_ssd_dt_bias_b, ssd_a_log_f=v_ssd_a_log_f, ssd_a_log_b=v_ssd_a_log_b, ssd_d=v_ssd_d, ssd_norm_w=v_ssd_norm_w, ssd_w_out=v_ssd_w_out, attn_w_qkv=v_attn_w_qkv, attn_q_gain=v_attn_q_gain, attn_k_gain=v_attn_k_gain, attn_sinks=v_attn_sinks, attn_w_o=v_attn_w_o, ffn_w_up=v_ffn_w_up, ffn_conv_w=v_ffn_conv_w, ffn_conv_b=v_ffn_conv_b, ffn_w_down=v_ffn_w_down)
    return _step(x, c, ctx, w, loss_target, m, v)
```
